```python
import math
import jax
import jax.numpy as jnp
from jax import lax
import numpy as np

D_MODEL = 1024
BATCH = 8
SEQ = 2048
DEPTH = 2

MEM_LEN = 256
EPS = 1e-6
MIX_WIDTH = D_MODEL
GROUP_WIDTH = MIX_WIDTH // 4

GLA_HEADS = 4
GLA_DV = GROUP_WIDTH // GLA_HEADS
GLA_DK = GLA_DV // 2
GLA_RANK = 16
GLA_TAU = 16.0
GLA_CHUNK = 64

POOL_WINDOWS = (2, 4, 8, 16)
POOL_GROUPS = len(POOL_WINDOWS)
POOL_CG = GROUP_WIDTH // POOL_GROUPS

DSA_HEADS = 4
DSA_DH = GROUP_WIDTH // DSA_HEADS
IDX_HEADS = 8
IDX_DIM = 32
DSA_TOPK_MAX = 256
DSA_QBLOCK = 128

SGU_GROUPS = 4
SGU_CHUNK = 128
SGU_CG = GROUP_WIDTH // SGU_GROUPS

REL_BUCKETS = 32
REL_MAX_DIST = 128

XA_HEADS = 4
XA_DH = 64

D_FF = 2816
N_EXPERTS = 8
TOP_K = 2
D_FF_EXPERT = 3584

IN_SPLITS = (
    GLA_HEADS * GLA_DK, GLA_HEADS * GLA_DK, GROUP_WIDTH, GROUP_WIDTH, GLA_RANK,
    GROUP_WIDTH,
    GROUP_WIDTH, GROUP_WIDTH, GROUP_WIDTH, IDX_HEADS * IDX_DIM, IDX_DIM, IDX_HEADS,
    2 * GROUP_WIDTH,
)
IN_WIDTH = sum(IN_SPLITS)

kernel_name = 'hybrid_parallel_group_block'

F32 = jnp.float32


def rmsnorm(x, g):
    xf = x.astype(F32)
    y = xf * lax.rsqrt(jnp.mean(xf * xf, axis=-1, keepdims=True) + EPS)
    return (y * g.astype(F32)).astype(x.dtype)


def split_cols(z, sizes):
    outs, off = [], 0
    for s in sizes:
        outs.append(z[..., off:off + s])
        off += s
    return outs


def rel_bucket(dist):
    n = jnp.maximum(dist, 0)
    max_exact = REL_BUCKETS // 2
    nf = jnp.maximum(n, 1).astype(F32)
    large = max_exact + (jnp.log(nf / max_exact) / math.log(REL_MAX_DIST / max_exact)
                         * (REL_BUCKETS - max_exact)).astype(jnp.int32)
    large = jnp.minimum(large, REL_BUCKETS - 1)
    return jnp.where(n < max_exact, n, large)


def gla_mixer(q, k, v, g_out, a_low, wa2, ba, norm_g):
    B, T, _ = q.shape
    C = GLA_CHUNK
    N = T // C
    logit = (a_low @ wa2 + ba).astype(F32)
    log_a = jax.nn.log_sigmoid(logit) / GLA_TAU

    def heads(t, d):
        return t.reshape(B, N, C, GLA_HEADS, d).transpose(0, 3, 1, 2, 4).astype(F32)

    qh = heads(q, GLA_DK) * (GLA_DK ** -0.5)
    kh = heads(k, GLA_DK)
    vh = heads(v, GLA_DV)
    b = jnp.cumsum(heads(log_a, GLA_DK), axis=3)
    q_t = qh * jnp.exp(b)
    k_t = kh * jnp.exp(-b)
    causal = jnp.tril(jnp.ones((C, C), dtype=bool))
    att = jnp.where(causal, jnp.einsum('bhnik,bhnjk->bhnij', q_t, k_t), 0.0)
    o_intra = jnp.einsum('bhnij,bhnjv->bhniv', att, vh)
    b_last = b[:, :, :, -1:, :]
    chunk_kv = jnp.einsum('bhnck,bhncv->bhnkv', kh * jnp.exp(b_last - b), vh)
    decay = jnp.exp(b_last[:, :, :, 0, :])

    def step(S, inp):
        dec_n, kv_n = inp
        return S * dec_n[..., None] + kv_n, S

    S0 = jnp.zeros((B, GLA_HEADS, GLA_DK, GLA_DV), F32)
    _, S_start = lax.scan(step, S0, (jnp.moveaxis(decay, 2, 0), jnp.moveaxis(chunk_kv, 2, 0)))
    S_start = jnp.moveaxis(S_start, 0, 2)
    o = o_intra + jnp.einsum('bhnck,bhnkv->bhncv', q_t, S_start)
    o = o * lax.rsqrt(jnp.mean(o * o, axis=-1, keepdims=True) + EPS)
    o = o.transpose(0, 2, 3, 1, 4).reshape(B, T, GLA_HEADS * GLA_DV) * norm_g.astype(F32)
    return (o * jax.nn.silu(g_out.astype(F32))).astype(q.dtype)


def pool_mixer(u, w, b, scale):
    B, T, _ = u.shape
    uf = u.astype(F32)
    csum = jnp.concatenate([jnp.zeros((B, 1, GROUP_WIDTH), F32), jnp.cumsum(uf, axis=1)], axis=1)
    tpos = jnp.arange(T)
    outs = []
    for gi, win in enumerate(POOL_WINDOWS):
        c = csum[:, :, gi * POOL_CG:(gi + 1) * POOL_CG]
        hi = c[:, 1:]
        lo = jnp.pad(c[:, :T + 1 - win], ((0, 0), (win - 1, 0), (0, 0)))
        cnt = jnp.minimum(tpos + 1, win).astype(F32)[None, :, None]
        outs.append((hi - lo) / cnt - uf[:, :, gi * POOL_CG:(gi + 1) * POOL_CG])
    p = jnp.stack(outs, axis=2)
    y = jnp.einsum('btgc,gcd->btgd', p, w.astype(F32)) + b.astype(F32)
    return (y.reshape(B, T, GROUP_WIDTH) * scale.astype(F32)).astype(u.dtype)


def dsa_mixer(q, k, v, qi, ki, wi, rel_bias):
    B, T, _ = q.shape
    topk = min(DSA_TOPK_MAX, T // 4)
    n_blk = T // DSA_QBLOCK
    qh = q.reshape(B, T, DSA_HEADS, DSA_DH)
    kh = k.reshape(B, T, DSA_HEADS, DSA_DH)
    vh = v.reshape(B, T, DSA_HEADS, DSA_DH)
    qih = qi.reshape(B, T, IDX_HEADS, IDX_DIM)
    kif = ki.astype(F32)
    w_idx = wi.astype(F32) * (IDX_HEADS ** -0.5)
    spos = jnp.arange(T)
    gather = jax.vmap(lambda arr, idx: arr[idx])

    def block(blk):
        t0 = blk * DSA_QBLOCK
        tpos = t0 + jnp.arange(DSA_QBLOCK)
        q_b = lax.dynamic_slice_in_dim(qh, t0, DSA_QBLOCK, axis=1).astype(F32)
        qi_b = lax.dynamic_slice_in_dim(qih, t0, DSA_QBLOCK, axis=1).astype(F32)
        w_b = lax.dynamic_slice_in_dim(w_idx, t0, DSA_QBLOCK, axis=1)
        dots = jnp.einsum('bqhd,bsd->bqsh', qi_b, kif) * (IDX_DIM ** -0.5)
        score = jnp.einsum('bqsh,bqh->bqs', jax.nn.relu(dots), w_b)
        visible = spos[None, :] <= tpos[:, None]
        score = jnp.where(visible[None], score, -jnp.inf)
        _, sel = lax.top_k(score, topk)
        k_sel = gather(kh, sel).astype(F32)
        v_sel = gather(vh, sel).astype(F32)
        dist = tpos[None, :, None] - sel
        bias = rel_bias[rel_bucket(dist)].astype(F32)
        logits = jnp.einsum('bqhd,bqkhd->bqkh', q_b, k_sel) * (DSA_DH ** -0.5) + bias
        logits = jnp.where((dist >= 0)[..., None], logits, -jnp.inf)
        p = jax.nn.softmax(logits, axis=2)
        return jnp.einsum('bqkh,bqkhd->bqhd', p, v_sel).astype(q.dtype)

    out = lax.map(block, jnp.arange(n_blk))
    return out.transpose(1, 0, 2, 3, 4).reshape(B, T, DSA_HEADS * DSA_DH)


def sgu_mixer(zuv, ln_g, ln_b, w_s, b_s):
    B, T, _ = zuv.shape
    z = jax.nn.gelu(zuv.astype(F32))
    u, v = z[..., :GROUP_WIDTH], z[..., GROUP_WIDTH:]
    mu = jnp.mean(v, axis=-1, keepdims=True)
    var = jnp.mean(jnp.square(v - mu), axis=-1, keepdims=True)
    vn = (v - mu) * lax.rsqrt(var + EPS) * ln_g.astype(F32) + ln_b.astype(F32)
    vn = vn.reshape(B, T // SGU_CHUNK, SGU_CHUNK, SGU_GROUPS, SGU_CG)
    ws = w_s.astype(F32) * jnp.tril(jnp.ones((SGU_CHUNK, SGU_CHUNK), F32))
    mixed = jnp.einsum('gts,bnsgc->bntgc', ws, vn) + b_s.astype(F32).T[None, None, :, :, None]
    return (u * mixed.reshape(B, T, GROUP_WIDTH)).astype(zuv.dtype)


def hybrid_mixer(hn, w_in, gla_wa2, gla_ba, gla_norm, pool_w, pool_b, pool_scale,
                 sgu_ln_g, sgu_ln_b, sgu_w, sgu_b, w_out, rel_bias):
    z = hn @ w_in
    (a_q, a_k, a_v, a_g, a_lr, b_u, c_q, c_k, c_v, c_qi, c_ki, c_wi, d_uv) = split_cols(z, IN_SPLITS)
    o_a = gla_mixer(a_q, a_k, a_v, a_g, a_lr, gla_wa2, gla_ba, gla_norm)
    o_b = pool_mixer(b_u, pool_w, pool_b, pool_scale)
    o_c = dsa_mixer(c_q, c_k, c_v, c_qi, c_ki, c_wi, rel_bias)
    o_d = sgu_mixer(d_uv, sgu_ln_g, sgu_ln_b, sgu_w, sgu_b)
    return jnp.concatenate([o_a, o_b, o_c, o_d], axis=-1) @ w_out


def cross_attn(hn, mem, norm_mem, wq, wkv, wo):
    B, T, _ = hn.shape
    q = (hn @ wq).reshape(B, T, XA_HEADS, XA_DH).astype(F32)
    kv = rmsnorm(mem, norm_mem) @ wkv
    M = mem.shape[1]
    k = kv[..., :XA_HEADS * XA_DH].reshape(B, M, XA_HEADS, XA_DH).astype(F32)
    v = kv[..., XA_HEADS * XA_DH:].reshape(B, M, XA_HEADS, XA_DH).astype(F32)
    p = jax.nn.softmax(jnp.einsum('bthd,bmhd->bhtm', q, k) * (XA_DH ** -0.5), axis=-1)
    o = jnp.einsum('bhtm,bmhd->bthd', p, v).reshape(B, T, XA_HEADS * XA_DH).astype(hn.dtype)
    return o @ wo


def swiglu(h, w1, w3, w2):
    return (jax.nn.silu(h @ w1) * (h @ w3)) @ w2


def moe_swiglu(h, router, w1, w3, w2):
    logits = (h @ router).astype(F32)
    top_v, top_i = lax.top_k(logits, TOP_K)
    gates = jax.nn.softmax(top_v, axis=-1)
    dense_gate = jnp.sum(jax.nn.one_hot(top_i, N_EXPERTS, dtype=F32) * gates[..., None], axis=-2)
    y = jnp.zeros(h.shape, F32)
    for e in range(N_EXPERTS):
        y = y + swiglu(h, w1[e], w3[e], w2[e]).astype(F32) * dense_gate[..., e:e + 1]
    return y.astype(h.dtype)


def setup_inputs(seed: int = 0) -> dict:
    key = jax.random.key(seed)
    ks = jax.random.split(key, 30)
    n_dense = (DEPTH + 1) // 2
    n_moe = DEPTH // 2

    def nrm(k, shape, fan_in):
        return jax.random.normal(k, shape, F32) * (fan_in ** -0.5)

    def gain(k, shape):
        return 1.0 + 0.05 * jax.random.normal(k, shape, F32)

    def small(k, shape):
        return 0.02 * jax.random.normal(k, shape, F32)

    return {
        'x': jax.random.normal(ks[0], (BATCH, SEQ, D_MODEL), F32),
        'mem': jax.random.normal(ks[1], (BATCH, MEM_LEN, D_MODEL), F32),
        'rel_bias': 0.5 * jax.random.normal(ks[2], (REL_BUCKETS, DSA_HEADS), F32),
        'final_norm': gain(ks[3], (D_MODEL,)),
        'norm_mix': gain(ks[4], (DEPTH, D_MODEL)),
        'w_in': nrm(ks[5], (DEPTH, D_MODEL, IN_WIDTH), D_MODEL),
        'gla_wa2': nrm(ks[6], (DEPTH, GLA_RANK, GLA_HEADS * GLA_DK), GLA_RANK),
        'gla_ba': small(ks[7], (DEPTH, GLA_HEADS * GLA_DK)),
        'gla_norm': gain(ks[8], (DEPTH, GROUP_WIDTH)),
        'pool_w': nrm(ks[9], (DEPTH, POOL_GROUPS, POOL_CG, POOL_CG), POOL_CG),
        'pool_b': small(ks[10], (DEPTH, POOL_GROUPS, POOL_CG)),
        'pool_scale': gain(ks[11], (DEPTH, GROUP_WIDTH)),
        'sgu_ln_g': gain(ks[12], (DEPTH, GROUP_WIDTH)),
        'sgu_ln_b': small(ks[13], (DEPTH, GROUP_WIDTH)),
        'sgu_w': nrm(ks[14], (DEPTH, SGU_GROUPS, SGU_CHUNK, SGU_CHUNK), SGU_CHUNK),
        'sgu_b': gain(ks[15], (DEPTH, SGU_GROUPS, SGU_CHUNK)),
        'w_out': nrm(ks[16], (DEPTH, MIX_WIDTH, D_MODEL), MIX_WIDTH),
        'norm_xa': gain(ks[17], (DEPTH, D_MODEL)),
        'norm_mem': gain(ks[18], (DEPTH, D_MODEL)),
        'xa_wq': nrm(ks[19], (DEPTH, D_MODEL, XA_HEADS * XA_DH), D_MODEL),
        'xa_wkv': nrm(ks[20], (DEPTH, D_MODEL, 2 * XA_HEADS * XA_DH), D_MODEL),
        'xa_wo': nrm(ks[21], (DEPTH, XA_HEADS * XA_DH, D_MODEL), XA_HEADS * XA_DH),
        'norm_ffn': gain(ks[22], (DEPTH, D_MODEL)),
        'ffn_w1': nrm(ks[23], (n_dense, D_MODEL, D_FF), D_MODEL),
        'ffn_w3': nrm(ks[24], (n_dense, D_MODEL, D_FF), D_MODEL),
        'ffn_w2': nrm(ks[25], (n_dense, D_FF, D_MODEL), D_FF),
        'router': nrm(ks[26], (n_moe, D_MODEL, N_EXPERTS), D_MODEL),
        'moe_w1': nrm(ks[27], (n_moe, N_EXPERTS, D_MODEL, D_FF_EXPERT), D_MODEL),
        'moe_w3': nrm(ks[28], (n_moe, N_EXPERTS, D_MODEL, D_FF_EXPERT), D_MODEL),
        'moe_w2': nrm(ks[29], (n_moe, N_EXPERTS, D_FF_EXPERT, D_MODEL), D_FF_EXPERT),
    }


def reference(x, mem, rel_bias, final_norm, norm_mix, w_in, gla_wa2, gla_ba, gla_norm,
              pool_w, pool_b, pool_scale, sgu_ln_g, sgu_ln_b, sgu_w, sgu_b, w_out,
              norm_xa, norm_mem, xa_wq, xa_wkv, xa_wo, norm_ffn,
              ffn_w1, ffn_w3, ffn_w2, router, moe_w1, moe_w3, moe_w2):
    h = x
    for i in range(DEPTH):
        h = h + hybrid_mixer(rmsnorm(h, norm_mix[i]), w_in[i], gla_wa2[i], gla_ba[i], gla_norm[i],
                             pool_w[i], pool_b[i], pool_scale[i], sgu_ln_g[i], sgu_ln_b[i],
                             sgu_w[i], sgu_b[i], w_out[i], rel_bias)
        h = h + cross_attn(rmsnorm(h, norm_xa[i]), mem, norm_mem[i], xa_wq[i], xa_wkv[i], xa_wo[i])
        hn = rmsnorm(h, norm_ffn[i])
        j = i // 2
        if i % 2 == 0:
            h = h + swiglu(hn, ffn_w1[j], ffn_w3[j], ffn_w2[j])
        else:
            h = h + moe_swiglu(hn, router[j], moe_w1[j], moe_w3[j], moe_w2[j])
    return rmsnorm(h, final_norm)
```

```python
import functools
import math

import jax
import jax.numpy as jnp
import numpy as np
from jax import lax
from jax.experimental import pallas as pl
from jax.experimental.pallas import tpu as pltpu

F32 = jnp.float32
BF16 = jnp.bfloat16
EPS = 1e-6

GROUP_WIDTH = 256

GLA_HEADS = 4
GLA_DV = 64
GLA_DK = 32
GLA_RANK = 16
GLA_TAU = 16.0
GLA_CHUNK = 64

POOL_WINDOWS = (2, 4, 8, 16)
POOL_CG = 64

DSA_HEADS = 4
DSA_DH = 64
IDX_HEADS = 8
IDX_DIM = 32
DSA_TOPK_MAX = 256
DSA_BLOCK = 128

SGU_GROUPS = 4
SGU_CHUNK = 128
SGU_CG = 64

REL_BUCKETS = 32
REL_MAX_DIST = 128

XA_HEADS = 4
XA_DH = 64

N_EXPERTS = 8

A_WIDTH = 896
B_WIDTH = 256
C_WIDTH = 1152
D_WIDTH = 512

INT_MIN = -(2 ** 31)
NEG_BIG = -1e30
VMEM_LIMIT = 56 * 1024 * 1024


def _cparams(sem):
    return pltpu.CompilerParams(dimension_semantics=sem, vmem_limit_bytes=VMEM_LIMIT)


def _dot(a, b):
    return jnp.dot(a, b, preferred_element_type=F32)


def _dot_nt(a, b):
    return lax.dot_general(a, b, (((1,), (1,)), ((), ())), preferred_element_type=F32)


def _dot_tn(a, b):
    return lax.dot_general(a, b, (((0,), (0,)), ((), ())), preferred_element_type=F32)


def _dot_f32(a, b):
    return jnp.dot(a, b, preferred_element_type=F32, precision=lax.Precision.HIGHEST)


def _rms(x, g):
    return x * lax.rsqrt(jnp.mean(x * x, axis=-1, keepdims=True) + EPS) * g


def _norm_matmul_kernel(h_ref, g_ref, w_ref, *out_refs, widths):
    hb = _rms(h_ref[...], g_ref[...]).astype(BF16)
    off = 0
    for o_ref, wd in zip(out_refs, widths):
        o_ref[...] = _dot(hb, w_ref[:, off:off + wd]).astype(o_ref.dtype)
        off += wd


def norm_matmul(h, g, w, widths, tm=512):
    m, d = h.shape
    tm = min(tm, m)
    n = sum(widths)
    return pl.pallas_call(
        functools.partial(_norm_matmul_kernel, widths=widths),
        grid=(m // tm,),
        in_specs=[pl.BlockSpec((tm, d), lambda i: (i, 0)),
                  pl.BlockSpec((1, d), lambda i: (0, 0)),
                  pl.BlockSpec((d, n), lambda i: (0, 0))],
        out_specs=[pl.BlockSpec((tm, wd), lambda i: (i, 0)) for wd in widths],
        out_shape=[jax.ShapeDtypeStruct((m, wd), F32) for wd in widths],
        compiler_params=_cparams(("parallel",)),
        name="norm_matmul",
    )(h, g.reshape(1, d), w)


def _log_sigmoid(x):
    return jnp.minimum(x, 0.0) - jnp.log1p(jnp.exp(-jnp.abs(x)))


def _gla_kernel(z_ref, wa2_ref, ba_ref, ng_ref, o_ref, s_ref, *, n_chunks):
    c = GLA_CHUNK
    hk = GLA_HEADS * GLA_DK
    hv = GLA_HEADS * GLA_DV
    s_ref[...] = jnp.zeros_like(s_ref)

    head_k = lax.broadcasted_iota(jnp.int32, (1, hk), 1) // GLA_DK
    head_v = lax.broadcasted_iota(jnp.int32, (1, hv), 1) // GLA_DV
    tril = (lax.broadcasted_iota(jnp.int32, (c, c), 1)
            <= lax.broadcasted_iota(jnp.int32, (c, c), 0)).astype(F32)
    causal4 = (lax.broadcasted_iota(jnp.int32, (GLA_HEADS * c, c), 1)
               <= lax.broadcasted_iota(jnp.int32, (GLA_HEADS * c, c), 0) % c)
    state_mask = (lax.broadcasted_iota(jnp.int32, (hk, hv), 0) // GLA_DK
                  == lax.broadcasted_iota(jnp.int32, (hk, hv), 1) // GLA_DV)
    norm_mat = jnp.where(lax.broadcasted_iota(jnp.int32, (hv, hv), 0) // GLA_DV
                         == lax.broadcasted_iota(jnp.int32, (hv, hv), 1) // GLA_DV,
                         1.0 / GLA_DV, 0.0).astype(F32)
    wa2 = wa2_ref[...]
    ba = ba_ref[...]
    ng = ng_ref[...]

    def body(n, carry):
        r0 = pl.multiple_of(n * c, c)
        z = z_ref[0, pl.ds(r0, c), :]
        q, k, v, g, lr = z[:, 0:128], z[:, 128:256], z[:, 256:512], z[:, 512:768], z[:, 768:896]
        log_a = _log_sigmoid(_dot_f32(lr, wa2) + ba) / GLA_TAU
        b = _dot_f32(tril, log_a)
        b_last = b[c - 1:c, :]
        q_t = q * (GLA_DK ** -0.5) * jnp.exp(b)
        k_t = (k * jnp.exp(-b)).astype(BF16)
        k_dec = (k * jnp.exp(b_last - b)).astype(BF16)
        vb = v.astype(BF16)
        q4 = jnp.concatenate([jnp.where(head_k == h, q_t, 0.0) for h in range(GLA_HEADS)], axis=0).astype(BF16)
        att = jnp.where(causal4, _dot_nt(q4, k_t), 0.0)
        r = _dot(att.astype(BF16), vb)
        o = _dot(q_t.astype(BF16), s_ref[...].astype(BF16))
        for h in range(GLA_HEADS):
            o = o + jnp.where(head_v == h, r[h * c:(h + 1) * c, :], 0.0)
        kv = jnp.where(state_mask, _dot_tn(k_dec, vb), 0.0)
        dec = jnp.exp(jnp.transpose(jnp.broadcast_to(b_last, (hk, hk))))
        s_ref[...] = s_ref[...] * jnp.concatenate([dec, dec], axis=1) + kv
        o = o * lax.rsqrt(_dot_f32(o * o, norm_mat) + EPS) * ng
        o_ref[0, pl.ds(r0, c), :] = o * (g * jax.nn.sigmoid(g))
        return carry

    lax.fori_loop(0, n_chunks, body, 0)


def gla_mixer(z_a, wa2, ba, norm_g):
    bsz, t, _ = z_a.shape
    hk = GLA_HEADS * GLA_DK
    wa2p = jnp.zeros((128, hk), F32).at[:GLA_RANK].set(wa2)
    return pl.pallas_call(
        functools.partial(_gla_kernel, n_chunks=t // GLA_CHUNK),
        grid=(bsz,),
        in_specs=[pl.BlockSpec((1, t, A_WIDTH), lambda b: (b, 0, 0)),
                  pl.BlockSpec((128, hk), lambda b: (0, 0)),
                  pl.BlockSpec((1, hk), lambda b: (0, 0)),
                  pl.BlockSpec((1, GROUP_WIDTH), lambda b: (0, 0))],
        out_specs=pl.BlockSpec((1, t, GROUP_WIDTH), lambda b: (b, 0, 0)),
        out_shape=jax.ShapeDtypeStruct((bsz, t, GROUP_WIDTH), F32),
        scratch_shapes=[pltpu.VMEM((hk, GROUP_WIDTH), F32)],
        compiler_params=_cparams(("parallel",)),
        name="gla_mixer",
    )(z_a, wa2p, ba.reshape(1, hk), norm_g.reshape(1, GROUP_WIDTH))


def _pool_kernel(u_ref, w_ref, b_ref, sc_ref, o_ref):
    u = u_ref[0]
    t, gw = u.shape
    row = lax.broadcasted_iota(jnp.int32, (t, gw), 0)
    grp = lax.broadcasted_iota(jnp.int32, (t, gw), 1) // POOL_CG

    def shifted(x, k):
        return jnp.where(row >= k, pltpu.roll(x, k, axis=0), 0.0)

    s = u
    p = jnp.zeros_like(u)
    for gi, win in enumerate(POOL_WINDOWS):
        half = win // 2
        s = s + shifted(s, half)
        cnt = jnp.minimum(row + 1, win).astype(F32)
        p = jnp.where(grp == gi, s / cnt - u, p)
    y = _dot(p.astype(BF16), w_ref[...]) + b_ref[...]
    o_ref[0] = y * sc_ref[...]


def pool_mixer(z_b, w, b, scale):
    assert POOL_WINDOWS == (2, 4, 8, 16)
    bsz, t, gw = z_b.shape
    w_bd = jnp.zeros((gw, gw), F32)
    for gi in range(len(POOL_WINDOWS)):
        w_bd = w_bd.at[gi * POOL_CG:(gi + 1) * POOL_CG, gi * POOL_CG:(gi + 1) * POOL_CG].set(w[gi])
    return pl.pallas_call(
        _pool_kernel,
        grid=(bsz,),
        in_specs=[pl.BlockSpec((1, t, gw), lambda i: (i, 0, 0)),
                  pl.BlockSpec((gw, gw), lambda i: (0, 0)),
                  pl.BlockSpec((1, gw), lambda i: (0, 0)),
                  pl.BlockSpec((1, gw), lambda i: (0, 0))],
        out_specs=pl.BlockSpec((1, t, gw), lambda i: (i, 0, 0)),
        out_shape=jax.ShapeDtypeStruct((bsz, t, gw), F32),
        compiler_params=_cparams(("parallel",)),
        name="pool_mixer",
    )(z_b, w_bd.astype(BF16), b.reshape(1, gw), scale.reshape(1, gw))


def _sgu_kernel(z_ref, lg_ref, lb_ref, w_ref, bm_ref, o_ref, *, chunks):
    c = SGU_CHUNK
    gw = GROUP_WIDTH
    rows = SGU_GROUPS * c
    tri = (lax.broadcasted_iota(jnp.int32, (rows, c), 1)
           <= lax.broadcasted_iota(jnp.int32, (rows, c), 0) % c)
    ws = jnp.where(tri, w_ref[...], 0.0).astype(BF16)
    grp = lax.broadcasted_iota(jnp.int32, (1, gw), 1) // SGU_CG
    for ci in range(chunks):
        z = jax.nn.gelu(z_ref[0, ci * c:(ci + 1) * c, :], approximate=True)
        u, v = z[:, :gw], z[:, gw:]
        mu = jnp.mean(v, axis=-1, keepdims=True)
        var = jnp.mean(jnp.square(v - mu), axis=-1, keepdims=True)
        vn = (v - mu) * lax.rsqrt(var + EPS) * lg_ref[...] + lb_ref[...]
        r = _dot(ws, vn.astype(BF16))
        mixed = bm_ref[...]
        for g in range(SGU_GROUPS):
            mixed = mixed + jnp.where(grp == g, r[g * c:(g + 1) * c, :], 0.0)
        o_ref[0, ci * c:(ci + 1) * c, :] = u * mixed


def sgu_mixer(z_d, ln_g, ln_b, w_s, b_s, chunks=4):
    bsz, t, _ = z_d.shape
    gw = GROUP_WIDTH
    tt = chunks * SGU_CHUNK
    bias = jnp.repeat(b_s.T, SGU_CG, axis=1)
    return pl.pallas_call(
        functools.partial(_sgu_kernel, chunks=chunks),
        grid=(bsz, t // tt),
        in_specs=[pl.BlockSpec((1, tt, 2 * gw), lambda b, i: (b, i, 0)),
                  pl.BlockSpec((1, gw), lambda b, i: (0, 0)),
                  pl.BlockSpec((1, gw), lambda b, i: (0, 0)),
                  pl.BlockSpec((SGU_GROUPS * SGU_CHUNK, SGU_CHUNK), lambda b, i: (0, 0)),
                  pl.BlockSpec((SGU_CHUNK, gw), lambda b, i: (0, 0))],
        out_specs=pl.BlockSpec((1, tt, gw), lambda b, i: (b, i, 0)),
        out_shape=jax.ShapeDtypeStruct((bsz, t, gw), F32),
        compiler_params=_cparams(("parallel", "parallel")),
        name="sgu_mixer",
    )(z_d, ln_g.reshape(1, gw), ln_b.reshape(1, gw),
      w_s.reshape(SGU_GROUPS * SGU_CHUNK, SGU_CHUNK), bias)


def _rel_bias_tables(rel_bias):
    assert REL_MAX_DIST <= DSA_BLOCK + 1
    a = np.arange(DSA_BLOCK)[:, None]
    b = np.arange(DSA_BLOCK)[None, :]
    dist = np.stack([a - b, DSA_BLOCK + a - b, 2 * DSA_BLOCK + a - b])
    n = np.maximum(dist, 0)
    max_exact = REL_BUCKETS // 2
    nf = np.maximum(n, 1).astype(np.float32)
    large = max_exact + (np.log(nf / np.float32(max_exact)) / np.float32(math.log(REL_MAX_DIST / max_exact))
                         * np.float32(REL_BUCKETS - max_exact)).astype(np.int32)
    bucket = np.where(n < max_exact, n, np.minimum(large, REL_BUCKETS - 1))
    return jnp.transpose(rel_bias[bucket], (0, 3, 1, 2))


def _dsa_kernel(q_ref, k_ref, v_ref, qi_ref, kw_ref, qw_ref, toe_ref, o_ref, kbd_ref, keys_ref,
                *, topk, n_blocks, idx_bits):
    blk = DSA_BLOCK
    i = pl.program_id(1)
    hd = DSA_HEADS * DSA_DH
    idx_w = IDX_HEADS * IDX_DIM

    @pl.when(i == 0)
    def _():
        bd = (lax.broadcasted_iota(jnp.int32, (idx_w, IDX_HEADS * blk), 0) // IDX_DIM
              == lax.broadcasted_iota(jnp.int32, (idx_w, IDX_HEADS * blk), 1) // blk)

        def build(kb, c):
            r0 = pl.multiple_of(kb * blk, blk)
            kt = jnp.transpose(kw_ref[0, pl.ds(r0, blk), :])[0:IDX_DIM, :]
            col = jnp.concatenate([kt] * IDX_HEADS, axis=0)
            kbd_ref[kb] = jnp.where(bd, jnp.concatenate([col] * IDX_HEADS, axis=1), 0.0).astype(BF16)
            return c

        lax.fori_loop(0, n_blocks, build, 0)

    t_loc = lax.broadcasted_iota(jnp.int32, (blk, blk), 0)
    s_loc = lax.broadcasted_iota(jnp.int32, (blk, blk), 1)
    n_vis = i + 1

    qi = qi_ref[0].astype(BF16)
    qw = qw_ref[0]
    w_cols = [qw[:, IDX_DIM + h:IDX_DIM + h + 1] * (IDX_HEADS ** -0.5) for h in range(IDX_HEADS)]

    def score_body(kb, c):
        d = _dot(qi, kbd_ref[kb])
        sc = jnp.zeros((blk, blk), F32)
        for h in range(IDX_HEADS):
            sc = sc + jnp.maximum(d[:, h * blk:(h + 1) * blk] * (IDX_DIM ** -0.5), 0.0) * w_cols[h]
        sc = jnp.where(sc == 0.0, 0.0, sc)
        bits = pltpu.bitcast(sc, jnp.int32)
        key = jnp.where(bits < 0, bits ^ jnp.int32(0x7FFFFFFF), bits)
        keys_ref[kb] = jnp.where((kb < i) | (s_loc <= t_loc), key, jnp.int32(INT_MIN))
        return c

    lax.fori_loop(0, n_vis, score_body, 0)

    def count(pred):
        def body(kb, acc):
            return acc + jnp.where(pred(keys_ref[kb], kb), 1.0, 0.0)
        acc = lax.fori_loop(0, n_vis, body, jnp.zeros((blk, blk), F32))
        return jnp.sum(acc, axis=1, keepdims=True)

    kf = float(topk)
    zero = jnp.zeros((blk, 1), jnp.int32)
    base = jnp.where(count(lambda key, kb: key >= zero) >= kf, zero, jnp.int32(INT_MIN))

    def bit_body(it, base):
        cand = base | lax.shift_left(jnp.int32(1), 30 - it)
        return jnp.where(count(lambda key, kb: key >= cand) >= kf, cand, base)

    thr = lax.fori_loop(0, 31, bit_body, base)

    n_gt = count(lambda key, kb: key > thr)
    n_eq = count(lambda key, kb: key == thr)
    need = kf - n_gt
    excess = jnp.where((n_eq > need) & (thr > jnp.int32(INT_MIN)), 1.0, 0.0)

    def tie_search():
        def tie_body(it, j):
            cand = j | lax.shift_left(jnp.int32(1), idx_bits - 1 - it)
            below = count(lambda key, kb: (key == thr) & (kb * blk + s_loc < cand))
            return jnp.where(below < need, cand, j)
        return lax.fori_loop(0, idx_bits, tie_body, zero)

    last = lax.cond(jnp.max(excess) > 0.0, tie_search, lambda: jnp.full((blk, 1), 2 ** 30, jnp.int32))

    lane_h = lax.broadcasted_iota(jnp.int32, (1, hd), 1) // DSA_DH
    q = q_ref[0]
    q_heads = [jnp.where(lane_h == h, q, 0.0).astype(BF16) for h in range(DSA_HEADS)]

    def att_body(kb, carry):
        ms, ls, acc = carry
        r0 = pl.multiple_of(kb * blk, blk)
        key = keys_ref[kb]
        sel = (((key > thr) | ((key == thr) & (kb * blk + s_loc <= last)))
               & ((kb < i) | (s_loc <= t_loc)))
        kblk = k_ref[0, pl.ds(r0, blk), :].astype(BF16)
        vblk = v_ref[0, pl.ds(r0, blk), :]
        back = jnp.minimum(i - kb, 2)
        new_ms, new_ls = [], []
        scale = jnp.zeros((blk, hd), F32)
        pv = jnp.zeros((blk, hd), F32)
        for h in range(DSA_HEADS):
            lg = _dot_nt(q_heads[h], kblk) * (DSA_DH ** -0.5) + toe_ref[back, h]
            lg = jnp.where(sel, lg, NEG_BIG)
            m_new = jnp.maximum(ms[h], jnp.max(lg, axis=1, keepdims=True))
            p = jnp.where(sel, jnp.exp(lg - m_new), 0.0)
            alpha = jnp.exp(ms[h] - m_new)
            new_ms.append(m_new)
            new_ls.append(ls[h] * alpha + jnp.sum(p, axis=1, keepdims=True))
            pv = pv + _dot(p.astype(BF16), jnp.where(lane_h == h, vblk, 0.0).astype(BF16))
            scale = jnp.where(lane_h == h, alpha, scale)
        return tuple(new_ms), tuple(new_ls), acc * scale + pv

    init = (tuple(jnp.full((blk, 1), NEG_BIG, F32) for _ in range(DSA_HEADS)),
            tuple(jnp.zeros((blk, 1), F32) for _ in range(DSA_HEADS)),
            jnp.zeros((blk, hd), F32))
    _, ls, acc = lax.fori_loop(0, n_vis, att_body, init)
    inv = jnp.zeros((blk, hd), F32)
    for h in range(DSA_HEADS):
        inv = jnp.where(lane_h == h, 1.0 / ls[h], inv)
    o_ref[0] = acc * inv


def dsa_mixer(z_c, rel_bias):
    bsz, t, _ = z_c.shape
    blk = DSA_BLOCK
    n_blocks = t // blk
    topk = min(DSA_TOPK_MAX, t // 4)
    toe = _rel_bias_tables(rel_bias)
    hd = DSA_HEADS * DSA_DH
    kernel = functools.partial(_dsa_kernel, topk=topk, n_blocks=n_blocks,
                               idx_bits=max(1, (t - 1).bit_length()))
    return pl.pallas_call(
        kernel,
        grid=(bsz, n_blocks),
        in_specs=[pl.BlockSpec((1, blk, hd), lambda b, i: (b, i, 0)),
                  pl.BlockSpec((1, t, hd), lambda b, i: (b, 0, 1)),
                  pl.BlockSpec((1, t, hd), lambda b, i: (b, 0, 2)),
                  pl.BlockSpec((1, blk, hd), lambda b, i: (b, i, 3)),
                  pl.BlockSpec((1, t, 128), lambda b, i: (b, 0, 8)),
                  pl.BlockSpec((1, blk, 128), lambda b, i: (b, i, 8)),
                  pl.BlockSpec((3, DSA_HEADS, blk, blk), lambda b, i: (0, 0, 0, 0))],
        out_specs=pl.BlockSpec((1, blk, hd), lambda b, i: (b, i, 0)),
        out_shape=jax.ShapeDtypeStruct((bsz, t, hd), F32),
        scratch_shapes=[pltpu.VMEM((n_blocks, IDX_HEADS * IDX_DIM, IDX_HEADS * blk), BF16),
                        pltpu.VMEM((n_blocks, blk, blk), jnp.int32)],
        compiler_params=_cparams(("parallel", "arbitrary")),
        name="dsa_mixer",
    )(z_c, z_c, z_c, z_c, z_c, z_c, toe)


def _out_proj_kernel(h_ref, a_ref, b_ref, c_ref, d_ref, w_ref, o_ref):
    gw = GROUP_WIDTH
    acc = h_ref[...]
    for gi, r in enumerate((a_ref, b_ref, c_ref, d_ref)):
        acc = acc + _dot(r[...].astype(BF16), w_ref[gi * gw:(gi + 1) * gw, :])
    o_ref[...] = acc


def out_proj(h, outs, w, tm=512):
    m, d = h.shape
    tm = min(tm, m)
    gw = GROUP_WIDTH
    return pl.pallas_call(
        _out_proj_kernel,
        grid=(m // tm,),
        in_specs=[pl.BlockSpec((tm, d), lambda i: (i, 0))]
                 + [pl.BlockSpec((tm, gw), lambda i: (i, 0))] * 4
                 + [pl.BlockSpec((4 * gw, d), lambda i: (0, 0))],
        out_specs=pl.BlockSpec((tm, d), lambda i: (i, 0)),
        out_shape=jax.ShapeDtypeStruct((m, d), F32),
        compiler_params=_cparams(("parallel",)),
        name="out_proj",
    )(h, *outs, w)


def _xattn_kernel(h_ref, g_ref, wq_ref, k_ref, v_ref, wo_ref, o_ref):
    x = h_ref[0]
    hd = XA_HEADS * XA_DH
    q = _dot(_rms(x, g_ref[...]).astype(BF16), wq_ref[...]).astype(BF16)
    k = k_ref[0]
    v = v_ref[0]
    lane_h = lax.broadcasted_iota(jnp.int32, (1, hd), 1) // XA_DH
    o = jnp.zeros((x.shape[0], hd), F32)
    for h in range(XA_HEADS):
        s = _dot_nt(q, jnp.where(lane_h == h, k, 0.0).astype(BF16)) * (XA_DH ** -0.5)
        p = jnp.exp(s - jnp.max(s, axis=-1, keepdims=True))
        p = p / jnp.sum(p, axis=-1, keepdims=True)
        o = o + _dot(p.astype(BF16), jnp.where(lane_h == h, v, 0.0).astype(BF16))
    o_ref[0] = x + _dot(o.astype(BF16), wo_ref[...])


def cross_attn(h3, g, wq, k, v, wo, tm=512):
    bsz, t, d = h3.shape
    tm = min(tm, t)
    mlen = k.shape[1]
    hd = XA_HEADS * XA_DH
    return pl.pallas_call(
        _xattn_kernel,
        grid=(bsz, t // tm),
        in_specs=[pl.BlockSpec((1, tm, d), lambda b, i: (b, i, 0)),
                  pl.BlockSpec((1, d), lambda b, i: (0, 0)),
                  pl.BlockSpec((d, hd), lambda b, i: (0, 0)),
                  pl.BlockSpec((1, mlen, hd), lambda b, i: (b, 0, 0)),
                  pl.BlockSpec((1, mlen, hd), lambda b, i: (b, 0, 0)),
                  pl.BlockSpec((hd, d), lambda b, i: (0, 0))],
        out_specs=pl.BlockSpec((1, tm, d), lambda b, i: (b, i, 0)),
        out_shape=jax.ShapeDtypeStruct((bsz, t, d), F32),
        compiler_params=_cparams(("parallel", "parallel")),
        name="cross_attn",
    )(h3, g.reshape(1, d), wq, k, v, wo)


def _ffn_kernel(h_ref, g_ref, w1_ref, w3_ref, w2_ref, o_ref, hn_ref, acc_ref):
    f = pl.program_id(1)

    @pl.when(f == 0)
    def _():
        hn_ref[...] = _rms(h_ref[...], g_ref[...]).astype(BF16)
        acc_ref[...] = jnp.zeros_like(acc_ref)

    a = _dot(hn_ref[...], w1_ref[...])
    b = _dot(hn_ref[...], w3_ref[...])
    acc_ref[...] += _dot((a * jax.nn.sigmoid(a) * b).astype(BF16), w2_ref[...])

    @pl.when(f == pl.num_programs(1) - 1)
    def _():
        o_ref[...] = h_ref[...] + acc_ref[...]


def ffn(h, g, w1, w3, w2, tm=1024, tf=256):
    m, d = h.shape
    tm = min(tm, m)
    nf = w1.shape[1]
    assert nf % tf == 0
    return pl.pallas_call(
        _ffn_kernel,
        grid=(m // tm, nf // tf),
        in_specs=[pl.BlockSpec((tm, d), lambda i, f: (i, 0)),
                  pl.BlockSpec((1, d), lambda i, f: (0, 0)),
                  pl.BlockSpec((d, tf), lambda i, f: (0, f)),
                  pl.BlockSpec((d, tf), lambda i, f: (0, f)),
                  pl.BlockSpec((tf, d), lambda i, f: (f, 0))],
        out_specs=pl.BlockSpec((tm, d), lambda i, f: (i, 0)),
        out_shape=jax.ShapeDtypeStruct((m, d), F32),
        scratch_shapes=[pltpu.VMEM((tm, d), BF16), pltpu.VMEM((tm, d), F32)],
        compiler_params=_cparams(("parallel", "arbitrary")),
        name="ffn",
    )(h, g.reshape(1, d), w1, w3, w2)


def _router_kernel(h_ref, g_ref, wr_ref, hn_ref, gate_ref):
    hn = _rms(h_ref[...], g_ref[...])
    hn_ref[...] = hn.astype(BF16)
    lane = lax.broadcasted_iota(jnp.int32, (hn.shape[0], 128), 1)
    logits = jnp.where(lane < N_EXPERTS, _dot_f32(hn, wr_ref[...]), -jnp.inf)
    m1 = jnp.max(logits, axis=-1, keepdims=True)
    i1 = jnp.min(jnp.where(logits == m1, lane, 128), axis=-1, keepdims=True)
    rest = jnp.where(lane == i1, -jnp.inf, logits)
    m2 = jnp.max(rest, axis=-1, keepdims=True)
    i2 = jnp.min(jnp.where(rest == m2, lane, 128), axis=-1, keepdims=True)
    e2 = jnp.exp(m2 - m1)
    g1 = 1.0 / (1.0 + e2)
    gate_ref[...] = jnp.where(lane == i1, g1, 0.0) + jnp.where(lane == i2, e2 * g1, 0.0)


def route_tokens(h, g, wr, tm=512):
    m, d = h.shape
    tm = min(tm, m)
    wrp = jnp.zeros((d, 128), F32).at[:, :N_EXPERTS].set(wr)
    return pl.pallas_call(
        _router_kernel,
        grid=(m // tm,),
        in_specs=[pl.BlockSpec((tm, d), lambda i: (i, 0)),
                  pl.BlockSpec((1, d), lambda i: (0, 0)),
                  pl.BlockSpec((d, 128), lambda i: (0, 0))],
        out_specs=[pl.BlockSpec((tm, d), lambda i: (i, 0)),
                   pl.BlockSpec((tm, 128), lambda i: (i, 0))],
        out_shape=[jax.ShapeDtypeStruct((m, d), BF16), jax.ShapeDtypeStruct((m, 128), F32)],
        compiler_params=_cparams(("parallel",)),
        name="router",
    )(h, g.reshape(1, d), wrp)


def _moe_kernel(h_ref, hn_ref, gate_ref, w1_ref, w3_ref, w2_ref, o_ref, acc_ref):
    e = pl.program_id(1)
    f = pl.program_id(2)

    @pl.when((e == 0) & (f == 0))
    def _():
        acc_ref[...] = jnp.zeros_like(acc_ref)

    lane = lax.broadcasted_iota(jnp.int32, gate_ref.shape, 1)
    gate = jnp.sum(jnp.where(lane == e, gate_ref[...], 0.0), axis=-1, keepdims=True)
    a = _dot(hn_ref[...], w1_ref[0])
    b = _dot(hn_ref[...], w3_ref[0])
    acc_ref[...] += gate * _dot((a * jax.nn.sigmoid(a) * b).astype(BF16), w2_ref[0])

    @pl.when((e == pl.num_programs(1) - 1) & (f == pl.num_programs(2) - 1))
    def _():
        o_ref[...] = h_ref[...] + acc_ref[...]


def moe(h, hn, gates, w1, w3, w2, tm=1024, tf=512):
    m, d = h.shape
    tm = min(tm, m)
    n_exp, _, nf = w1.shape
    assert nf % tf == 0
    return pl.pallas_call(
        _moe_kernel,
        grid=(m // tm, n_exp, nf // tf),
        in_specs=[pl.BlockSpec((tm, d), lambda i, e, f: (i, 0)),
                  pl.BlockSpec((tm, d), lambda i, e, f: (i, 0)),
                  pl.BlockSpec((tm, 128), lambda i, e, f: (i, 0)),
                  pl.BlockSpec((1, d, tf), lambda i, e, f: (e, 0, f)),
                  pl.BlockSpec((1, d, tf), lambda i, e, f: (e, 0, f)),
                  pl.BlockSpec((1, tf, d), lambda i, e, f: (e, f, 0))],
        out_specs=pl.BlockSpec((tm, d), lambda i, e, f: (i, 0)),
        out_shape=jax.ShapeDtypeStruct((m, d), F32),
        scratch_shapes=[pltpu.VMEM((tm, d), F32)],
        compiler_params=_cparams(("parallel", "arbitrary", "arbitrary")),
        name="moe",
    )(h, hn, gates, w1, w3, w2)


def moe_block(h, g, wr, w1, w3, w2):
    hn, gates = route_tokens(h, g, wr)
    return moe(h, hn, gates, w1.astype(BF16), w3.astype(BF16), w2.astype(BF16))


def _final_norm_kernel(h_ref, g_ref, o_ref):
    o_ref[...] = _rms(h_ref[...], g_ref[...])


def final_rmsnorm(h, g, tm=1024):
    m, d = h.shape
    tm = min(tm, m)
    return pl.pallas_call(
        _final_norm_kernel,
        grid=(m // tm,),
        in_specs=[pl.BlockSpec((tm, d), lambda i: (i, 0)), pl.BlockSpec((1, d), lambda i: (0, 0))],
        out_specs=pl.BlockSpec((tm, d), lambda i: (i, 0)),
        out_shape=jax.ShapeDtypeStruct((m, d), F32),
        compiler_params=_cparams(("parallel",)),
        name="final_norm",
    )(h, g.reshape(1, d))


def _pad_cols(w, width):
    return jnp.pad(w, ((0, 0), (0, width - w.shape[1])))


def _in_proj_weight(w_in):
    a_end = 2 * GLA_HEADS * GLA_DK + 2 * GROUP_WIDTH + GLA_RANK
    b_end = a_end + GROUP_WIDTH
    c_end = b_end + 3 * GROUP_WIDTH + IDX_HEADS * IDX_DIM + IDX_DIM + IDX_HEADS
    assert w_in.shape[1] == c_end + 2 * GROUP_WIDTH
    return jnp.concatenate([_pad_cols(w_in[:, :a_end], A_WIDTH), w_in[:, a_end:b_end],
                            _pad_cols(w_in[:, b_end:c_end], C_WIDTH), w_in[:, c_end:]], axis=1).astype(BF16)


def hybrid_layer(h, mem2, rel_bias, p, bsz, t):
    m, d = h.shape
    z_a, z_b, z_c, z_d = norm_matmul(h, p["norm_mix"], _in_proj_weight(p["w_in"]),
                                     (A_WIDTH, B_WIDTH, C_WIDTH, D_WIDTH))
    o_a = gla_mixer(z_a.reshape(bsz, t, A_WIDTH), p["gla_wa2"], p["gla_ba"], p["gla_norm"])
    o_b = pool_mixer(z_b.reshape(bsz, t, B_WIDTH), p["pool_w"], p["pool_b"].reshape(-1), p["pool_scale"])
    o_c = dsa_mixer(z_c.reshape(bsz, t, C_WIDTH), rel_bias)
    o_d = sgu_mixer(z_d.reshape(bsz, t, D_WIDTH), p["sgu_ln_g"], p["sgu_ln_b"], p["sgu_w"], p["sgu_b"])
    h = out_proj(h, [o.reshape(m, GROUP_WIDTH) for o in (o_a, o_b, o_c, o_d)], p["w_out"].astype(BF16))

    hd = XA_HEADS * XA_DH
    k, v = norm_matmul(mem2, p["norm_mem"], p["xa_wkv"].astype(BF16), (hd, hd))
    mlen = mem2.shape[0] // bsz
    h = cross_attn(h.reshape(bsz, t, d), p["norm_xa"], p["xa_wq"].astype(BF16),
                   k.reshape(bsz, mlen, hd), v.reshape(bsz, mlen, hd), p["xa_wo"].astype(BF16))
    return h.reshape(m, d)


def kernel(x, mem, rel_bias, final_norm, norm_mix, w_in, gla_wa2, gla_ba, gla_norm, pool_w, pool_b,
           pool_scale, sgu_ln_g, sgu_ln_b, sgu_w, sgu_b, w_out, norm_xa, norm_mem, xa_wq, xa_wkv, xa_wo,
           norm_ffn, ffn_w1, ffn_w3, ffn_w2, router, moe_w1, moe_w3, moe_w2):
    bsz, t, d = x.shape
    depth = norm_mix.shape[0]
    h = x.reshape(bsz * t, d)
    mem2 = mem.reshape(-1, d)
    for i in range(depth):
        p = dict(norm_mix=norm_mix[i], w_in=w_in[i], gla_wa2=gla_wa2[i], gla_ba=gla_ba[i],
                 gla_norm=gla_norm[i], pool_w=pool_w[i], pool_b=pool_b[i], pool_scale=pool_scale[i],
                 sgu_ln_g=sgu_ln_g[i], sgu_ln_b=sgu_ln_b[i], sgu_w=sgu_w[i], sgu_b=sgu_b[i],
                 w_out=w_out[i], norm_xa=norm_xa[i], norm_mem=norm_mem[i], xa_wq=xa_wq[i],
                 xa_wkv=xa_wkv[i], xa_wo=xa_wo[i])
        h = hybrid_layer(h, mem2, rel_bias, p, bsz, t)
        j = i // 2
        if i % 2 == 0:
            h = ffn(h, norm_ffn[i], ffn_w1[j].astype(BF16), ffn_w3[j].astype(BF16), ffn_w2[j].astype(BF16))
        else:
            h = moe_block(h, norm_ffn[i], router[j], moe_w1[j], moe_w3[j], moe_w2[j])
    return final_rmsnorm(h, final_norm).reshape(bsz, t, d)
```

```python
import functools
import math

import jax
import jax.numpy as jnp
import numpy as np
from jax import lax
from jax.experimental import pallas as pl
from jax.experimental.pallas import tpu as pltpu

F32 = jnp.float32
BF16 = jnp.bfloat16
EPS = 1e-6

GROUP_WIDTH = 256

GLA_HEADS = 4
GLA_DV = 64
GLA_DK = 32
GLA_RANK = 16
GLA_TAU = 16.0
GLA_CHUNK = 64

POOL_WINDOWS = (2, 4, 8, 16)
POOL_CG = 64

DSA_HEADS = 4
DSA_DH = 64
IDX_HEADS = 8
IDX_DIM = 32
DSA_TOPK_MAX = 256
DSA_BLOCK = 256
DSA_SUB = 64

SGU_GROUPS = 4
SGU_CHUNK = 128
SGU_CG = 64

REL_BUCKETS = 32
REL_MAX_DIST = 128

XA_HEADS = 4
XA_DH = 64

N_EXPERTS = 8

A_WIDTH = 896
B_WIDTH = 256
C_WIDTH = 1152
D_WIDTH = 512

INT_MIN = -(2 ** 31)
NEG_BIG = -1e30
VMEM_LIMIT = 56 * 1024 * 1024


def _cparams(sem):
    return pltpu.CompilerParams(dimension_semantics=sem, vmem_limit_bytes=VMEM_LIMIT)


def _dot(a, b):
    return jnp.dot(a, b, preferred_element_type=F32)


def _dot_nt(a, b):
    return lax.dot_general(a, b, (((1,), (1,)), ((), ())), preferred_element_type=F32)


def _dot_tn(a, b):
    return lax.dot_general(a, b, (((0,), (0,)), ((), ())), preferred_element_type=F32)


def _dot_f32(a, b):
    return jnp.dot(a, b, preferred_element_type=F32, precision=lax.Precision.HIGHEST)


def _rms(x, g):
    return x * lax.rsqrt(jnp.mean(x * x, axis=-1, keepdims=True) + EPS) * g


def _norm_matmul_kernel(h_ref, g_ref, w_ref, *out_refs, widths):
    hb = _rms(h_ref[...], g_ref[...]).astype(BF16)
    off = 0
    for o_ref, wd in zip(out_refs, widths):
        o_ref[...] = _dot(hb, w_ref[:, off:off + wd]).astype(o_ref.dtype)
        off += wd


def norm_matmul(h, g, w, widths, tm=512):
    m, d = h.shape
    tm = min(tm, m)
    n = sum(widths)
    return pl.pallas_call(
        functools.partial(_norm_matmul_kernel, widths=widths),
        grid=(m // tm,),
        in_specs=[pl.BlockSpec((tm, d), lambda i: (i, 0)),
                  pl.BlockSpec((1, d), lambda i: (0, 0)),
                  pl.BlockSpec((d, n), lambda i: (0, 0))],
        out_specs=[pl.BlockSpec((tm, wd), lambda i: (i, 0)) for wd in widths],
        out_shape=[jax.ShapeDtypeStruct((m, wd), F32) for wd in widths],
        compiler_params=_cparams(("parallel",)),
        name="norm_matmul",
    )(h, g.reshape(1, d), w)


def _log_sigmoid(x):
    return jnp.minimum(x, 0.0) - jnp.log1p(jnp.exp(-jnp.abs(x)))


def _gla_kernel(z_ref, wa2_ref, ba_ref, ng_ref, o_ref, s_ref, *, n_chunks):
    c = GLA_CHUNK
    hk = GLA_HEADS * GLA_DK
    hv = GLA_HEADS * GLA_DV
    s_ref[...] = jnp.zeros_like(s_ref)

    head_k = lax.broadcasted_iota(jnp.int32, (1, hk), 1) // GLA_DK
    head_v = lax.broadcasted_iota(jnp.int32, (1, hv), 1) // GLA_DV
    tril = (lax.broadcasted_iota(jnp.int32, (c, c), 1)
            <= lax.broadcasted_iota(jnp.int32, (c, c), 0)).astype(F32)
    causal4 = (lax.broadcasted_iota(jnp.int32, (GLA_HEADS * c, c), 1)
               <= lax.broadcasted_iota(jnp.int32, (GLA_HEADS * c, c), 0) % c)
    state_mask = (lax.broadcasted_iota(jnp.int32, (hk, hv), 0) // GLA_DK
                  == lax.broadcasted_iota(jnp.int32, (hk, hv), 1) // GLA_DV)
    norm_mat = jnp.where(lax.broadcasted_iota(jnp.int32, (hv, hv), 0) // GLA_DV
                         == lax.broadcasted_iota(jnp.int32, (hv, hv), 1) // GLA_DV,
                         1.0 / GLA_DV, 0.0).astype(F32)
    wa2 = wa2_ref[...]
    ba = ba_ref[...]
    ng = ng_ref[...]

    def body(n, carry):
        r0 = pl.multiple_of(n * c, c)
        z = z_ref[0, pl.ds(r0, c), :]
        q, k, v, g, lr = z[:, 0:128], z[:, 128:256], z[:, 256:512], z[:, 512:768], z[:, 768:896]
        log_a = _log_sigmoid(_dot_f32(lr, wa2) + ba) / GLA_TAU
        b = _dot_f32(tril, log_a)
        b_last = b[c - 1:c, :]
        q_t = q * (GLA_DK ** -0.5) * jnp.exp(b)
        k_t = (k * jnp.exp(-b)).astype(BF16)
        k_dec = (k * jnp.exp(b_last - b)).astype(BF16)
        vb = v.astype(BF16)
        q4 = jnp.concatenate([jnp.where(head_k == h, q_t, 0.0) for h in range(GLA_HEADS)], axis=0).astype(BF16)
        att = jnp.where(causal4, _dot_nt(q4, k_t), 0.0)
        r = _dot(att.astype(BF16), vb)
        o = _dot(q_t.astype(BF16), s_ref[...].astype(BF16))
        for h in range(GLA_HEADS):
            o = o + jnp.where(head_v == h, r[h * c:(h + 1) * c, :], 0.0)
        kv = jnp.where(state_mask, _dot_tn(k_dec, vb), 0.0)
        dec = jnp.exp(jnp.transpose(jnp.broadcast_to(b_last, (hk, hk))))
        s_ref[...] = s_ref[...] * jnp.concatenate([dec, dec], axis=1) + kv
        o = o * lax.rsqrt(_dot_f32(o * o, norm_mat) + EPS) * ng
        o_ref[0, pl.ds(r0, c), :] = o * (g * jax.nn.sigmoid(g))
        return carry

    lax.fori_loop(0, n_chunks, body, 0)


def gla_mixer(z_a, wa2, ba, norm_g):
    bsz, t, _ = z_a.shape
    hk = GLA_HEADS * GLA_DK
    wa2p = jnp.zeros((128, hk), F32).at[:GLA_RANK].set(wa2)
    return pl.pallas_call(
        functools.partial(_gla_kernel, n_chunks=t // GLA_CHUNK),
        grid=(bsz,),
        in_specs=[pl.BlockSpec((1, t, A_WIDTH), lambda b: (b, 0, 0)),
                  pl.BlockSpec((128, hk), lambda b: (0, 0)),
                  pl.BlockSpec((1, hk), lambda b: (0, 0)),
                  pl.BlockSpec((1, GROUP_WIDTH), lambda b: (0, 0))],
        out_specs=pl.BlockSpec((1, t, GROUP_WIDTH), lambda b: (b, 0, 0)),
        out_shape=jax.ShapeDtypeStruct((bsz, t, GROUP_WIDTH), F32),
        scratch_shapes=[pltpu.VMEM((hk, GROUP_WIDTH), F32)],
        compiler_params=_cparams(("parallel",)),
        name="gla_mixer",
    )(z_a, wa2p, ba.reshape(1, hk), norm_g.reshape(1, GROUP_WIDTH))


def _pool_kernel(u_ref, w_ref, b_ref, sc_ref, o_ref):
    u = u_ref[0]
    t, gw = u.shape
    row = lax.broadcasted_iota(jnp.int32, (t, gw), 0)
    grp = lax.broadcasted_iota(jnp.int32, (t, gw), 1) // POOL_CG

    def shifted(x, k):
        return jnp.where(row >= k, pltpu.roll(x, k, axis=0), 0.0)

    s = u
    p = jnp.zeros_like(u)
    for gi, win in enumerate(POOL_WINDOWS):
        half = win // 2
        s = s + shifted(s, half)
        cnt = jnp.minimum(row + 1, win).astype(F32)
        p = jnp.where(grp == gi, s / cnt - u, p)
    y = _dot(p.astype(BF16), w_ref[...]) + b_ref[...]
    o_ref[0] = y * sc_ref[...]


def pool_mixer(z_b, w, b, scale):
    assert POOL_WINDOWS == (2, 4, 8, 16)
    bsz, t, gw = z_b.shape
    w_bd = jnp.zeros((gw, gw), F32)
    for gi in range(len(POOL_WINDOWS)):
        w_bd = w_bd.at[gi * POOL_CG:(gi + 1) * POOL_CG, gi * POOL_CG:(gi + 1) * POOL_CG].set(w[gi])
    return pl.pallas_call(
        _pool_kernel,
        grid=(bsz,),
        in_specs=[pl.BlockSpec((1, t, gw), lambda i: (i, 0, 0)),
                  pl.BlockSpec((gw, gw), lambda i: (0, 0)),
                  pl.BlockSpec((1, gw), lambda i: (0, 0)),
                  pl.BlockSpec((1, gw), lambda i: (0, 0))],
        out_specs=pl.BlockSpec((1, t, gw), lambda i: (i, 0, 0)),
        out_shape=jax.ShapeDtypeStruct((bsz, t, gw), F32),
        compiler_params=_cparams(("parallel",)),
        name="pool_mixer",
    )(z_b, w_bd.astype(BF16), b.reshape(1, gw), scale.reshape(1, gw))


def _sgu_kernel(z_ref, lg_ref, lb_ref, w_ref, bm_ref, o_ref, *, chunks):
    c = SGU_CHUNK
    gw = GROUP_WIDTH
    rows = SGU_GROUPS * c
    tri = (lax.broadcasted_iota(jnp.int32, (rows, c), 1)
           <= lax.broadcasted_iota(jnp.int32, (rows, c), 0) % c)
    ws = jnp.where(tri, w_ref[...], 0.0).astype(BF16)
    grp = lax.broadcasted_iota(jnp.int32, (1, gw), 1) // SGU_CG
    for ci in range(chunks):
        z = jax.nn.gelu(z_ref[0, ci * c:(ci + 1) * c, :], approximate=True)
        u, v = z[:, :gw], z[:, gw:]
        mu = jnp.mean(v, axis=-1, keepdims=True)
        var = jnp.mean(jnp.square(v - mu), axis=-1, keepdims=True)
        vn = (v - mu) * lax.rsqrt(var + EPS) * lg_ref[...] + lb_ref[...]
        r = _dot(ws, vn.astype(BF16))
        mixed = bm_ref[...]
        for g in range(SGU_GROUPS):
            mixed = mixed + jnp.where(grp == g, r[g * c:(g + 1) * c, :], 0.0)
        o_ref[0, ci * c:(ci + 1) * c, :] = u * mixed


def sgu_mixer(z_d, ln_g, ln_b, w_s, b_s, chunks=4):
    bsz, t, _ = z_d.shape
    gw = GROUP_WIDTH
    tt = chunks * SGU_CHUNK
    bias = jnp.repeat(b_s.T, SGU_CG, axis=1)
    return pl.pallas_call(
        functools.partial(_sgu_kernel, chunks=chunks),
        grid=(bsz, t // tt),
        in_specs=[pl.BlockSpec((1, tt, 2 * gw), lambda b, i: (b, i, 0)),
                  pl.BlockSpec((1, gw), lambda b, i: (0, 0)),
                  pl.BlockSpec((1, gw), lambda b, i: (0, 0)),
                  pl.BlockSpec((SGU_GROUPS * SGU_CHUNK, SGU_CHUNK), lambda b, i: (0, 0)),
                  pl.BlockSpec((SGU_CHUNK, gw), lambda b, i: (0, 0))],
        out_specs=pl.BlockSpec((1, tt, gw), lambda b, i: (b, i, 0)),
        out_shape=jax.ShapeDtypeStruct((bsz, t, gw), F32),
        compiler_params=_cparams(("parallel", "parallel")),
        name="sgu_mixer",
    )(z_d, ln_g.reshape(1, gw), ln_b.reshape(1, gw),
      w_s.reshape(SGU_GROUPS * SGU_CHUNK, SGU_CHUNK), bias)


def _bucket_table():
    assert REL_MAX_DIST <= DSA_BLOCK + 1
    s = np.arange(DSA_BLOCK)[:, None]
    t = np.arange(DSA_BLOCK)[None, :]
    dist = np.stack([t - s, DSA_BLOCK + t - s, 2 * DSA_BLOCK + t - s])
    n = np.maximum(dist, 0)
    max_exact = REL_BUCKETS // 2
    nf = np.maximum(n, 1).astype(np.float32)
    large = max_exact + (np.log(nf / np.float32(max_exact)) / np.float32(math.log(REL_MAX_DIST / max_exact))
                         * np.float32(REL_BUCKETS - max_exact)).astype(np.int32)
    return np.where(n < max_exact, n, np.minimum(large, REL_BUCKETS - 1)).astype(np.int32)


def _bias_table_kernel(rb_ref, bucket_ref, o_ref):
    for back in range(3):
        bucket = bucket_ref[back]
        for h in range(DSA_HEADS):
            acc = jnp.zeros(bucket.shape, F32)
            for b in range(REL_BUCKETS):
                acc = jnp.where(bucket == b, rb_ref[b * DSA_HEADS + h], acc)
            o_ref[back, h] = acc


def rel_bias_tables(rel_bias):
    blk = DSA_BLOCK
    return pl.pallas_call(
        _bias_table_kernel,
        in_specs=[pl.BlockSpec(memory_space=pltpu.SMEM),
                  pl.BlockSpec((3, blk, blk), lambda: (0, 0, 0))],
        out_specs=pl.BlockSpec((3, DSA_HEADS, blk, blk), lambda: (0, 0, 0, 0)),
        out_shape=jax.ShapeDtypeStruct((3, DSA_HEADS, blk, blk), F32),
        name="rel_bias_tables",
    )(rel_bias.reshape(-1), jnp.asarray(_bucket_table()))


def _dsa_kernel(q_ref, k_ref, v_ref, qi_ref, kw_ref, qw_ref, toe_ref, o_ref,
                kpl_ref, kb16_ref, vt_ref, keys_ref, acc_ref, am_ref, lg_ref, p_ref,
                *, topk, n_blocks, idx_bits):
    blk = DSA_BLOCK
    sub = DSA_SUB
    i = pl.program_id(1)
    hd = DSA_HEADS * DSA_DH
    heads_per_half = 128 // IDX_DIM

    @pl.when(i == 0)
    def _():
        lane = lax.broadcasted_iota(jnp.int32, (blk, 128), 1)

        def build(kb, c):
            r0 = pl.multiple_of(kb * blk, blk)
            ki = jnp.where(lane < IDX_DIM, kw_ref[0, pl.ds(r0, blk), :], 0.0)
            for j in range(heads_per_half):
                kpl_ref[kb, j] = (ki if j == 0 else pltpu.roll(ki, j * IDX_DIM, axis=1)).astype(BF16)
            kb16_ref[kb] = k_ref[0, pl.ds(r0, blk), :].astype(BF16)
            vt_ref[kb] = jnp.transpose(v_ref[0, pl.ds(r0, blk), :]).astype(BF16)
            return c

        lax.fori_loop(0, n_blocks, build, 0)

    s_loc = lax.broadcasted_iota(jnp.int32, (blk, blk), 0)
    t_loc = lax.broadcasted_iota(jnp.int32, (blk, blk), 1)
    n_vis = i + 1

    qi_t = jnp.transpose(qi_ref[0]).astype(BF16)
    qi_halves = [qi_t[:128, :], qi_t[128:, :]]
    w_t = jnp.transpose(qw_ref[0])
    w_rows = [w_t[IDX_DIM + h:IDX_DIM + h + 1, :] * (IDX_HEADS ** -0.5) * (IDX_DIM ** -0.5)
              for h in range(IDX_HEADS)]
    s_sub = lax.broadcasted_iota(jnp.int32, (sub, blk), 0)
    t_sub = lax.broadcasted_iota(jnp.int32, (sub, blk), 1)

    def score_body(kb, c):
        for ci in range(blk // sub):
            rows = slice(ci * sub, (ci + 1) * sub)
            sc = jnp.zeros((sub, blk), F32)
            for half in range(2):
                for j in range(heads_per_half):
                    d = _dot(kpl_ref[kb, j, rows, :], qi_halves[half])
                    sc = sc + jnp.maximum(d, 0.0) * w_rows[half * heads_per_half + j]
            sc = jnp.where(sc == 0.0, 0.0, sc)
            bits = pltpu.bitcast(sc, jnp.int32)
            key = jnp.where(bits < 0, bits ^ jnp.int32(0x7FFFFFFF), bits)
            vis = (kb < i) | (s_sub + ci * sub <= t_sub)
            keys_ref[kb, rows, :] = jnp.where(vis, key, jnp.int32(INT_MIN))
        return c

    lax.fori_loop(0, n_vis, score_body, 0)

    def count(pred):
        def body(kb, acc):
            hit = jnp.where(pred(keys_ref[kb], kb), 1.0, 0.0)
            return acc + jnp.sum(hit.reshape(blk // 32, 32, blk), axis=0)
        acc = lax.fori_loop(0, n_vis, body, jnp.zeros((32, blk), F32))
        return jnp.sum(acc, axis=0, keepdims=True)

    kf = float(topk)
    zero = jnp.zeros((1, blk), jnp.int32)
    base = jnp.where(count(lambda key, kb: key >= 0) >= kf, zero, jnp.int32(INT_MIN))

    def bit_body(it, base):
        cand = base | lax.shift_left(jnp.int32(1), 30 - it)
        return jnp.where(count(lambda key, kb: key >= cand) >= kf, cand, base)

    thr = lax.fori_loop(0, 31, bit_body, base)

    n_gt = count(lambda key, kb: key > thr)
    n_eq = count(lambda key, kb: key == thr)
    need = kf - n_gt
    excess = jnp.where((n_eq > need) & (thr > jnp.int32(INT_MIN)), 1.0, 0.0)

    def tie_search():
        def tie_body(it, j):
            cand = j | lax.shift_left(jnp.int32(1), idx_bits - 1 - it)
            below = count(lambda key, kb: (key == thr) & (kb * blk + s_loc < cand))
            return jnp.where(below < need, cand, j)
        return lax.fori_loop(0, idx_bits, tie_body, zero)

    last = lax.cond(jnp.max(excess) > 0.0, tie_search, lambda: jnp.full((1, blk), 2 ** 30, jnp.int32))

    assert DSA_DH ** -0.5 == 0.125
    q_t = jnp.transpose(q_ref[0] * (DSA_DH ** -0.5))
    row_h = lax.broadcasted_iota(jnp.int32, (hd, 1), 0) // DSA_DH
    q_heads = [jnp.where(row_h == h, q_t, 0.0).astype(BF16) for h in range(DSA_HEADS)]
    acc_ref[...] = jnp.zeros_like(acc_ref)
    n_sub = blk // sub

    def att_body(kb, carry):
        ms, ls = carry
        back = jnp.minimum(i - kb, 2)
        for ci in range(n_sub):
            rows = slice(ci * sub, (ci + 1) * sub)
            key = keys_ref[kb, rows, :]
            sel = (((key > thr) | ((key == thr) & (kb * blk + ci * sub + s_sub <= last)))
                   & ((kb < i) | (s_sub + ci * sub <= t_sub)))
            am_ref[rows, :] = jnp.where(sel, 0.0, NEG_BIG)
        new_ms, alphas = [], []
        for h in range(DSA_HEADS):
            pm = jnp.full((8, blk), NEG_BIG, F32)
            for ci in range(n_sub):
                rows = slice(ci * sub, (ci + 1) * sub)
                lg = _dot(kb16_ref[kb, rows, :], q_heads[h]) + toe_ref[back, h, rows, :] + am_ref[rows, :]
                lg_ref[h, rows, :] = lg
                pm = jnp.maximum(pm, jnp.max(lg.reshape(sub // 8, 8, blk), axis=0))
            m_new = jnp.maximum(ms[h], jnp.max(pm, axis=0, keepdims=True))
            new_ms.append(m_new)
            alphas.append(jnp.exp(ms[h] - m_new))
        new_ls = []
        for h in range(DSA_HEADS):
            ps = jnp.zeros((8, blk), F32)
            for ci in range(n_sub):
                rows = slice(ci * sub, (ci + 1) * sub)
                p = jnp.exp(lg_ref[h, rows, :] - new_ms[h])
                ps = ps + jnp.sum(p.reshape(sub // 8, 8, blk), axis=0)
                p_ref[h, rows, :] = p.astype(BF16)
            new_ls.append(ls[h] * alphas[h] + jnp.sum(ps, axis=0, keepdims=True))
        for h in range(DSA_HEADS):
            hrows = slice(h * DSA_DH, (h + 1) * DSA_DH)
            acc_ref[hrows, :] = acc_ref[hrows, :] * alphas[h] + _dot(vt_ref[kb, hrows, :], p_ref[h])
        return tuple(new_ms), tuple(new_ls)

    init = (tuple(jnp.full((1, blk), 0.01 * NEG_BIG, F32) for _ in range(DSA_HEADS)),
            tuple(jnp.zeros((1, blk), F32) for _ in range(DSA_HEADS)))
    _, ls = lax.fori_loop(0, n_vis, att_body, init)
    for h in range(DSA_HEADS):
        rows = slice(h * DSA_DH, (h + 1) * DSA_DH)
        acc_ref[rows, :] = acc_ref[rows, :] * (1.0 / ls[h])
    o_ref[0] = jnp.transpose(acc_ref[...])


def dsa_mixer(z_c, toe):
    bsz, t, _ = z_c.shape
    blk = DSA_BLOCK
    n_blocks = t // blk
    assert t % blk == 0
    topk = min(DSA_TOPK_MAX, t // 4)
    hd = DSA_HEADS * DSA_DH
    kernel = functools.partial(_dsa_kernel, topk=topk, n_blocks=n_blocks,
                               idx_bits=max(1, (t - 1).bit_length()))
    return pl.pallas_call(
        kernel,
        grid=(bsz, n_blocks),
        in_specs=[pl.BlockSpec((1, blk, hd), lambda b, i: (b, i, 0)),
                  pl.BlockSpec((1, t, hd), lambda b, i: (b, 0, 1)),
                  pl.BlockSpec((1, t, hd), lambda b, i: (b, 0, 2)),
                  pl.BlockSpec((1, blk, hd), lambda b, i: (b, i, 3)),
                  pl.BlockSpec((1, t, 128), lambda b, i: (b, 0, 8)),
                  pl.BlockSpec((1, blk, 128), lambda b, i: (b, i, 8)),
                  pl.BlockSpec((3, DSA_HEADS, blk, blk), lambda b, i: (0, 0, 0, 0))],
        out_specs=pl.BlockSpec((1, blk, hd), lambda b, i: (b, i, 0)),
        out_shape=jax.ShapeDtypeStruct((bsz, t, hd), F32),
        scratch_shapes=[pltpu.VMEM((n_blocks, 128 // IDX_DIM, blk, 128), BF16),
                        pltpu.VMEM((n_blocks, blk, hd), BF16),
                        pltpu.VMEM((n_blocks, hd, blk), BF16),
                        pltpu.VMEM((n_blocks, blk, blk), jnp.int32),
                        pltpu.VMEM((hd, blk), F32),
                        pltpu.VMEM((blk, blk), F32),
                        pltpu.VMEM((DSA_HEADS, blk, blk), F32),
                        pltpu.VMEM((DSA_HEADS, blk, blk), BF16)],
        compiler_params=_cparams(("parallel", "arbitrary")),
        name="dsa_mixer",
    )(z_c, z_c, z_c, z_c, z_c, z_c, toe)


def _out_proj_kernel(h_ref, a_ref, b_ref, c_ref, d_ref, w_ref, o_ref):
    gw = GROUP_WIDTH
    acc = h_ref[...]
    for gi, r in enumerate((a_ref, b_ref, c_ref, d_ref)):
        acc = acc + _dot(r[...].astype(BF16), w_ref[gi * gw:(gi + 1) * gw, :])
    o_ref[...] = acc


def out_proj(h, outs, w, tm=512):
    m, d = h.shape
    tm = min(tm, m)
    gw = GROUP_WIDTH
    return pl.pallas_call(
        _out_proj_kernel,
        grid=(m // tm,),
        in_specs=[pl.BlockSpec((tm, d), lambda i: (i, 0))]
                 + [pl.BlockSpec((tm, gw), lambda i: (i, 0))] * 4
                 + [pl.BlockSpec((4 * gw, d), lambda i: (0, 0))],
        out_specs=pl.BlockSpec((tm, d), lambda i: (i, 0)),
        out_shape=jax.ShapeDtypeStruct((m, d), F32),
        compiler_params=_cparams(("parallel",)),
        name="out_proj",
    )(h, *outs, w)


def _xattn_kernel(h_ref, g_ref, wq_ref, k_ref, v_ref, wo_ref, o_ref):
    x = h_ref[0]
    hd = XA_HEADS * XA_DH
    q = _dot(_rms(x, g_ref[...]).astype(BF16), wq_ref[...]).astype(BF16)
    k = k_ref[0]
    v = v_ref[0]
    lane_h = lax.broadcasted_iota(jnp.int32, (1, hd), 1) // XA_DH
    o = jnp.zeros((x.shape[0], hd), F32)
    for h in range(XA_HEADS):
        s = _dot_nt(q, jnp.where(lane_h == h, k, 0.0).astype(BF16)) * (XA_DH ** -0.5)
        p = jnp.exp(s - jnp.max(s, axis=-1, keepdims=True))
        p = p / jnp.sum(p, axis=-1, keepdims=True)
        o = o + _dot(p.astype(BF16), jnp.where(lane_h == h, v, 0.0).astype(BF16))
    o_ref[0] = x + _dot(o.astype(BF16), wo_ref[...])


def cross_attn(h3, g, wq, k, v, wo, tm=512):
    bsz, t, d = h3.shape
    tm = min(tm, t)
    mlen = k.shape[1]
    hd = XA_HEADS * XA_DH
    return pl.pallas_call(
        _xattn_kernel,
        grid=(bsz, t // tm),
        in_specs=[pl.BlockSpec((1, tm, d), lambda b, i: (b, i, 0)),
                  pl.BlockSpec((1, d), lambda b, i: (0, 0)),
                  pl.BlockSpec((d, hd), lambda b, i: (0, 0)),
                  pl.BlockSpec((1, mlen, hd), lambda b, i: (b, 0, 0)),
                  pl.BlockSpec((1, mlen, hd), lambda b, i: (b, 0, 0)),
                  pl.BlockSpec((hd, d), lambda b, i: (0, 0))],
        out_specs=pl.BlockSpec((1, tm, d), lambda b, i: (b, i, 0)),
        out_shape=jax.ShapeDtypeStruct((bsz, t, d), F32),
        compiler_params=_cparams(("parallel", "parallel")),
        name="cross_attn",
    )(h3, g.reshape(1, d), wq, k, v, wo)


def _ffn_kernel(h_ref, g_ref, w1_ref, w3_ref, w2_ref, o_ref, hn_ref, acc_ref):
    f = pl.program_id(1)

    @pl.when(f == 0)
    def _():
        hn_ref[...] = _rms(h_ref[...], g_ref[...]).astype(BF16)
        acc_ref[...] = jnp.zeros_like(acc_ref)

    a = _dot(hn_ref[...], w1_ref[...])
    b = _dot(hn_ref[...], w3_ref[...])
    acc_ref[...] += _dot((a * jax.nn.sigmoid(a) * b).astype(BF16), w2_ref[...])

    @pl.when(f == pl.num_programs(1) - 1)
    def _():
        o_ref[...] = h_ref[...] + acc_ref[...]


def ffn(h, g, w1, w3, w2, tm=1024, tf=256):
    m, d = h.shape
    tm = min(tm, m)
    nf = w1.shape[1]
    assert nf % tf == 0
    return pl.pallas_call(
        _ffn_kernel,
        grid=(m // tm, nf // tf),
        in_specs=[pl.BlockSpec((tm, d), lambda i, f: (i, 0)),
                  pl.BlockSpec((1, d), lambda i, f: (0, 0)),
                  pl.BlockSpec((d, tf), lambda i, f: (0, f)),
                  pl.BlockSpec((d, tf), lambda i, f: (0, f)),
                  pl.BlockSpec((tf, d), lambda i, f: (f, 0))],
        out_specs=pl.BlockSpec((tm, d), lambda i, f: (i, 0)),
        out_shape=jax.ShapeDtypeStruct((m, d), F32),
        scratch_shapes=[pltpu.VMEM((tm, d), BF16), pltpu.VMEM((tm, d), F32)],
        compiler_params=_cparams(("parallel", "arbitrary")),
        name="ffn",
    )(h, g.reshape(1, d), w1, w3, w2)


def _router_kernel(h_ref, g_ref, wr_ref, hn_ref, gate_ref):
    hn = _rms(h_ref[...], g_ref[...])
    hn_ref[...] = hn.astype(BF16)
    lane = lax.broadcasted_iota(jnp.int32, (hn.shape[0], 128), 1)
    logits = jnp.where(lane < N_EXPERTS, _dot_f32(hn, wr_ref[...]), -jnp.inf)
    m1 = jnp.max(logits, axis=-1, keepdims=True)
    i1 = jnp.min(jnp.where(logits == m1, lane, 128), axis=-1, keepdims=True)
    rest = jnp.where(lane == i1, -jnp.inf, logits)
    m2 = jnp.max(rest, axis=-1, keepdims=True)
    i2 = jnp.min(jnp.where(rest == m2, lane, 128), axis=-1, keepdims=True)
    e2 = jnp.exp(m2 - m1)
    g1 = 1.0 / (1.0 + e2)
    gate_ref[...] = jnp.where(lane == i1, g1, 0.0) + jnp.where(lane == i2, e2 * g1, 0.0)


def route_tokens(h, g, wr, tm=512):
    m, d = h.shape
    tm = min(tm, m)
    wrp = jnp.zeros((d, 128), F32).at[:, :N_EXPERTS].set(wr)
    return pl.pallas_call(
        _router_kernel,
        grid=(m // tm,),
        in_specs=[pl.BlockSpec((tm, d), lambda i: (i, 0)),
                  pl.BlockSpec((1, d), lambda i: (0, 0)),
                  pl.BlockSpec((d, 128), lambda i: (0, 0))],
        out_specs=[pl.BlockSpec((tm, d), lambda i: (i, 0)),
                   pl.BlockSpec((tm, 128), lambda i: (i, 0))],
        out_shape=[jax.ShapeDtypeStruct((m, d), BF16), jax.ShapeDtypeStruct((m, 128), F32)],
        compiler_params=_cparams(("parallel",)),
        name="router",
    )(h, g.reshape(1, d), wrp)


def _moe_kernel(h_ref, hn_ref, gate_ref, w1_ref, w3_ref, w2_ref, o_ref, acc_ref):
    e = pl.program_id(1)
    f = pl.program_id(2)

    @pl.when((e == 0) & (f == 0))
    def _():
        acc_ref[...] = jnp.zeros_like(acc_ref)

    lane = lax.broadcasted_iota(jnp.int32, gate_ref.shape, 1)
    gate = jnp.sum(jnp.where(lane == e, gate_ref[...], 0.0), axis=-1, keepdims=True)
    a = _dot(hn_ref[...], w1_ref[0])
    b = _dot(hn_ref[...], w3_ref[0])
    acc_ref[...] += gate * _dot((a * jax.nn.sigmoid(a) * b).astype(BF16), w2_ref[0])

    @pl.when((e == pl.num_programs(1) - 1) & (f == pl.num_programs(2) - 1))
    def _():
        o_ref[...] = h_ref[...] + acc_ref[...]


def moe(h, hn, gates, w1, w3, w2, tm=1024, tf=512):
    m, d = h.shape
    tm = min(tm, m)
    n_exp, _, nf = w1.shape
    assert nf % tf == 0
    return pl.pallas_call(
        _moe_kernel,
        grid=(m // tm, n_exp, nf // tf),
        in_specs=[pl.BlockSpec((tm, d), lambda i, e, f: (i, 0)),
                  pl.BlockSpec((tm, d), lambda i, e, f: (i, 0)),
                  pl.BlockSpec((tm, 128), lambda i, e, f: (i, 0)),
                  pl.BlockSpec((1, d, tf), lambda i, e, f: (e, 0, f)),
                  pl.BlockSpec((1, d, tf), lambda i, e, f: (e, 0, f)),
                  pl.BlockSpec((1, tf, d), lambda i, e, f: (e, f, 0))],
        out_specs=pl.BlockSpec((tm, d), lambda i, e, f: (i, 0)),
        out_shape=jax.ShapeDtypeStruct((m, d), F32),
        scratch_shapes=[pltpu.VMEM((tm, d), F32)],
        compiler_params=_cparams(("parallel", "arbitrary", "arbitrary")),
        name="moe",
    )(h, hn, gates, w1, w3, w2)


def moe_block(h, g, wr, w1, w3, w2):
    hn, gates = route_tokens(h, g, wr)
    return moe(h, hn, gates, w1.astype(BF16), w3.astype(BF16), w2.astype(BF16))


def _final_norm_kernel(h_ref, g_ref, o_ref):
    o_ref[...] = _rms(h_ref[...], g_ref[...])


def final_rmsnorm(h, g, tm=1024):
    m, d = h.shape
    tm = min(tm, m)
    return pl.pallas_call(
        _final_norm_kernel,
        grid=(m // tm,),
        in_specs=[pl.BlockSpec((tm, d), lambda i: (i, 0)), pl.BlockSpec((1, d), lambda i: (0, 0))],
        out_specs=pl.BlockSpec((tm, d), lambda i: (i, 0)),
        out_shape=jax.ShapeDtypeStruct((m, d), F32),
        compiler_params=_cparams(("parallel",)),
        name="final_norm",
    )(h, g.reshape(1, d))


def _pad_cols(w, width):
    return jnp.pad(w, ((0, 0), (0, width - w.shape[1])))


def _in_proj_weight(w_in):
    a_end = 2 * GLA_HEADS * GLA_DK + 2 * GROUP_WIDTH + GLA_RANK
    b_end = a_end + GROUP_WIDTH
    c_end = b_end + 3 * GROUP_WIDTH + IDX_HEADS * IDX_DIM + IDX_DIM + IDX_HEADS
    assert w_in.shape[1] == c_end + 2 * GROUP_WIDTH
    return jnp.concatenate([_pad_cols(w_in[:, :a_end], A_WIDTH), w_in[:, a_end:b_end],
                            _pad_cols(w_in[:, b_end:c_end], C_WIDTH), w_in[:, c_end:]], axis=1).astype(BF16)


def hybrid_layer(h, mem2, toe, p, bsz, t):
    m, d = h.shape
    z_a, z_b, z_c, z_d = norm_matmul(h, p["norm_mix"], _in_proj_weight(p["w_in"]),
                                     (A_WIDTH, B_WIDTH, C_WIDTH, D_WIDTH))
    o_a = gla_mixer(z_a.reshape(bsz, t, A_WIDTH), p["gla_wa2"], p["gla_ba"], p["gla_norm"])
    o_b = pool_mixer(z_b.reshape(bsz, t, B_WIDTH), p["pool_w"], p["pool_b"].reshape(-1), p["pool_scale"])
    o_c = dsa_mixer(z_c.reshape(bsz, t, C_WIDTH), toe)
    o_d = sgu_mixer(z_d.reshape(bsz, t, D_WIDTH), p["sgu_ln_g"], p["sgu_ln_b"], p["sgu_w"], p["sgu_b"])
    h = out_proj(h, [o.reshape(m, GROUP_WIDTH) for o in (o_a, o_b, o_c, o_d)], p["w_out"].astype(BF16))

    hd = XA_HEADS * XA_DH
    k, v = norm_matmul(mem2, p["norm_mem"], p["xa_wkv"].astype(BF16), (hd, hd))
    mlen = mem2.shape[0] // bsz
    h = cross_attn(h.reshape(bsz, t, d), p["norm_xa"], p["xa_wq"].astype(BF16),
                   k.reshape(bsz, mlen, hd), v.reshape(bsz, mlen, hd), p["xa_wo"].astype(BF16))
    return h.reshape(m, d)


def kernel(x, mem, rel_bias, final_norm, norm_mix, w_in, gla_wa2, gla_ba, gla_norm, pool_w, pool_b,
           pool_scale, sgu_ln_g, sgu_ln_b, sgu_w, sgu_b, w_out, norm_xa, norm_mem, xa_wq, xa_wkv, xa_wo,
           norm_ffn, ffn_w1, ffn_w3, ffn_w2, router, moe_w1, moe_w3, moe_w2):
    bsz, t, d = x.shape
    depth = norm_mix.shape[0]
    h = x.reshape(bsz * t, d)
    mem2 = mem.reshape(-1, d)
    toe = rel_bias_tables(rel_bias)
    for i in range(depth):
        p = dict(norm_mix=norm_mix[i], w_in=w_in[i], gla_wa2=gla_wa2[i], gla_ba=gla_ba[i],
                 gla_norm=gla_norm[i], pool_w=pool_w[i], pool_b=pool_b[i], pool_scale=pool_scale[i],
                 sgu_ln_g=sgu_ln_g[i], sgu_ln_b=sgu_ln_b[i], sgu_w=sgu_w[i], sgu_b=sgu_b[i],
                 w_out=w_out[i], norm_xa=norm_xa[i], norm_mem=norm_mem[i], xa_wq=xa_wq[i],
                 xa_wkv=xa_wkv[i], xa_wo=xa_wo[i])
        h = hybrid_layer(h, mem2, toe, p, bsz, t)
        j = i // 2
        if i % 2 == 0:
            h = ffn(h, norm_ffn[i], ffn_w1[j].astype(BF16), ffn_w3[j].astype(BF16), ffn_w2[j].astype(BF16))
        else:
            h = moe_block(h, norm_ffn[i], router[j], moe_w1[j], moe_w3[j], moe_w2[j])
    return final_rmsnorm(h, final_norm).reshape(bsz, t, d)
```

```python
import functools
import math

import jax
import jax.numpy as jnp
import numpy as np
from jax import lax
from jax.experimental import pallas as pl
from jax.experimental.pallas import tpu as pltpu

F32 = jnp.float32
BF16 = jnp.bfloat16
EPS = 1e-6

GROUP_WIDTH = 256

GLA_HEADS = 4
GLA_DV = 64
GLA_DK = 32
GLA_RANK = 16
GLA_TAU = 16.0
GLA_CHUNK = 64

POOL_WINDOWS = (2, 4, 8, 16)
POOL_CG = 64

DSA_HEADS = 4
DSA_DH = 64
IDX_HEADS = 8
IDX_DIM = 32
DSA_TOPK_MAX = 256
DSA_BLOCK = 256
DSA_SUB = 64

SGU_GROUPS = 4
SGU_CHUNK = 128
SGU_CG = 64

REL_BUCKETS = 32
REL_MAX_DIST = 128

XA_HEADS = 4
XA_DH = 64

N_EXPERTS = 8

A_WIDTH = 896
B_WIDTH = 256
C_WIDTH = 1152
D_WIDTH = 512

INT_MIN = -(2 ** 31)
NEG_BIG = -1e30
VMEM_LIMIT = 56 * 1024 * 1024


def _cparams(sem):
    return pltpu.CompilerParams(dimension_semantics=sem, vmem_limit_bytes=VMEM_LIMIT)


def _dot(a, b):
    return jnp.dot(a, b, preferred_element_type=F32)


def _dot_nt(a, b):
    return lax.dot_general(a, b, (((1,), (1,)), ((), ())), preferred_element_type=F32)


def _dot_tn(a, b):
    return lax.dot_general(a, b, (((0,), (0,)), ((), ())), preferred_element_type=F32)


def _dot_f32(a, b):
    return jnp.dot(a, b, preferred_element_type=F32, precision=lax.Precision.HIGHEST)


def _rms(x, g):
    return x * lax.rsqrt(jnp.mean(x * x, axis=-1, keepdims=True) + EPS) * g


def _norm_matmul_kernel(h_ref, g_ref, w_ref, *out_refs, widths):
    hb = _rms(h_ref[...], g_ref[...]).astype(BF16)
    off = 0
    for o_ref, wd in zip(out_refs, widths):
        o_ref[...] = _dot(hb, w_ref[:, off:off + wd]).astype(o_ref.dtype)
        off += wd


def norm_matmul(h, g, w, widths, tm=512):
    m, d = h.shape
    tm = min(tm, m)
    n = sum(widths)
    return pl.pallas_call(
        functools.partial(_norm_matmul_kernel, widths=widths),
        grid=(m // tm,),
        in_specs=[pl.BlockSpec((tm, d), lambda i: (i, 0)),
                  pl.BlockSpec((1, d), lambda i: (0, 0)),
                  pl.BlockSpec((d, n), lambda i: (0, 0))],
        out_specs=[pl.BlockSpec((tm, wd), lambda i: (i, 0)) for wd in widths],
        out_shape=[jax.ShapeDtypeStruct((m, wd), F32) for wd in widths],
        compiler_params=_cparams(("parallel",)),
        name="norm_matmul",
    )(h, g.reshape(1, d), w)


def _log_sigmoid(x):
    return jnp.minimum(x, 0.0) - jnp.log1p(jnp.exp(-jnp.abs(x)))


def _gla_kernel(z_ref, wa2_ref, ba_ref, ng_ref, o_ref, s_ref, *, n_chunks):
    c = GLA_CHUNK
    hk = GLA_HEADS * GLA_DK
    hv = GLA_HEADS * GLA_DV
    s_ref[...] = jnp.zeros_like(s_ref)

    head_k = lax.broadcasted_iota(jnp.int32, (1, hk), 1) // GLA_DK
    head_v = lax.broadcasted_iota(jnp.int32, (1, hv), 1) // GLA_DV
    tril = (lax.broadcasted_iota(jnp.int32, (c, c), 1)
            <= lax.broadcasted_iota(jnp.int32, (c, c), 0)).astype(F32)
    causal4 = (lax.broadcasted_iota(jnp.int32, (GLA_HEADS * c, c), 1)
               <= lax.broadcasted_iota(jnp.int32, (GLA_HEADS * c, c), 0) % c)
    state_mask = (lax.broadcasted_iota(jnp.int32, (hk, hv), 0) // GLA_DK
                  == lax.broadcasted_iota(jnp.int32, (hk, hv), 1) // GLA_DV)
    norm_mat = jnp.where(lax.broadcasted_iota(jnp.int32, (hv, hv), 0) // GLA_DV
                         == lax.broadcasted_iota(jnp.int32, (hv, hv), 1) // GLA_DV,
                         1.0 / GLA_DV, 0.0).astype(F32)
    wa2 = wa2_ref[...]
    ba = ba_ref[...]
    ng = ng_ref[...]

    def body(n, carry):
        r0 = pl.multiple_of(n * c, c)
        z = z_ref[0, pl.ds(r0, c), :]
        q, k, v, g, lr = z[:, 0:128], z[:, 128:256], z[:, 256:512], z[:, 512:768], z[:, 768:896]
        log_a = _log_sigmoid(_dot_f32(lr, wa2) + ba) / GLA_TAU
        b = _dot_f32(tril, log_a)
        b_last = b[c - 1:c, :]
        q_t = q * (GLA_DK ** -0.5) * jnp.exp(b)
        k_t = (k * jnp.exp(-b)).astype(BF16)
        k_dec = (k * jnp.exp(b_last - b)).astype(BF16)
        vb = v.astype(BF16)
        q4 = jnp.concatenate([jnp.where(head_k == h, q_t, 0.0) for h in range(GLA_HEADS)], axis=0).astype(BF16)
        att = jnp.where(causal4, _dot_nt(q4, k_t), 0.0)
        r = _dot(att.astype(BF16), vb)
        o = _dot(q_t.astype(BF16), s_ref[...].astype(BF16))
        for h in range(GLA_HEADS):
            o = o + jnp.where(head_v == h, r[h * c:(h + 1) * c, :], 0.0)
        kv = jnp.where(state_mask, _dot_tn(k_dec, vb), 0.0)
        dec = jnp.exp(jnp.transpose(jnp.broadcast_to(b_last, (hk, hk))))
        s_ref[...] = s_ref[...] * jnp.concatenate([dec, dec], axis=1) + kv
        o = o * lax.rsqrt(_dot_f32(o * o, norm_mat) + EPS) * ng
        o_ref[0, pl.ds(r0, c), :] = o * (g * jax.nn.sigmoid(g))
        return carry

    lax.fori_loop(0, n_chunks, body, 0)


def gla_mixer(z_a, wa2, ba, norm_g):
    bsz, t, _ = z_a.shape
    hk = GLA_HEADS * GLA_DK
    wa2p = jnp.zeros((128, hk), F32).at[:GLA_RANK].set(wa2)
    return pl.pallas_call(
        functools.partial(_gla_kernel, n_chunks=t // GLA_CHUNK),
        grid=(bsz,),
        in_specs=[pl.BlockSpec((1, t, A_WIDTH), lambda b: (b, 0, 0)),
                  pl.BlockSpec((128, hk), lambda b: (0, 0)),
                  pl.BlockSpec((1, hk), lambda b: (0, 0)),
                  pl.BlockSpec((1, GROUP_WIDTH), lambda b: (0, 0))],
        out_specs=pl.BlockSpec((1, t, GROUP_WIDTH), lambda b: (b, 0, 0)),
        out_shape=jax.ShapeDtypeStruct((bsz, t, GROUP_WIDTH), F32),
        scratch_shapes=[pltpu.VMEM((hk, GROUP_WIDTH), F32)],
        compiler_params=_cparams(("parallel",)),
        name="gla_mixer",
    )(z_a, wa2p, ba.reshape(1, hk), norm_g.reshape(1, GROUP_WIDTH))


def _pool_kernel(u_ref, w_ref, b_ref, sc_ref, o_ref):
    u = u_ref[0]
    t, gw = u.shape
    row = lax.broadcasted_iota(jnp.int32, (t, gw), 0)
    grp = lax.broadcasted_iota(jnp.int32, (t, gw), 1) // POOL_CG

    def shifted(x, k):
        return jnp.where(row >= k, pltpu.roll(x, k, axis=0), 0.0)

    s = u
    p = jnp.zeros_like(u)
    for gi, win in enumerate(POOL_WINDOWS):
        half = win // 2
        s = s + shifted(s, half)
        cnt = jnp.minimum(row + 1, win).astype(F32)
        p = jnp.where(grp == gi, s / cnt - u, p)
    y = _dot(p.astype(BF16), w_ref[...]) + b_ref[...]
    o_ref[0] = y * sc_ref[...]


def pool_mixer(z_b, w, b, scale):
    assert POOL_WINDOWS == (2, 4, 8, 16)
    bsz, t, gw = z_b.shape
    w_bd = jnp.zeros((gw, gw), F32)
    for gi in range(len(POOL_WINDOWS)):
        w_bd = w_bd.at[gi * POOL_CG:(gi + 1) * POOL_CG, gi * POOL_CG:(gi + 1) * POOL_CG].set(w[gi])
    return pl.pallas_call(
        _pool_kernel,
        grid=(bsz,),
        in_specs=[pl.BlockSpec((1, t, gw), lambda i: (i, 0, 0)),
                  pl.BlockSpec((gw, gw), lambda i: (0, 0)),
                  pl.BlockSpec((1, gw), lambda i: (0, 0)),
                  pl.BlockSpec((1, gw), lambda i: (0, 0))],
        out_specs=pl.BlockSpec((1, t, gw), lambda i: (i, 0, 0)),
        out_shape=jax.ShapeDtypeStruct((bsz, t, gw), F32),
        compiler_params=_cparams(("parallel",)),
        name="pool_mixer",
    )(z_b, w_bd.astype(BF16), b.reshape(1, gw), scale.reshape(1, gw))


def _sgu_kernel(z_ref, lg_ref, lb_ref, w_ref, bm_ref, o_ref, *, chunks):
    c = SGU_CHUNK
    gw = GROUP_WIDTH
    rows = SGU_GROUPS * c
    tri = (lax.broadcasted_iota(jnp.int32, (rows, c), 1)
           <= lax.broadcasted_iota(jnp.int32, (rows, c), 0) % c)
    ws = jnp.where(tri, w_ref[...], 0.0).astype(BF16)
    grp = lax.broadcasted_iota(jnp.int32, (1, gw), 1) // SGU_CG
    for ci in range(chunks):
        z = jax.nn.gelu(z_ref[0, ci * c:(ci + 1) * c, :], approximate=True)
        u, v = z[:, :gw], z[:, gw:]
        mu = jnp.mean(v, axis=-1, keepdims=True)
        var = jnp.mean(jnp.square(v - mu), axis=-1, keepdims=True)
        vn = (v - mu) * lax.rsqrt(var + EPS) * lg_ref[...] + lb_ref[...]
        r = _dot(ws, vn.astype(BF16))
        mixed = bm_ref[...]
        for g in range(SGU_GROUPS):
            mixed = mixed + jnp.where(grp == g, r[g * c:(g + 1) * c, :], 0.0)
        o_ref[0, ci * c:(ci + 1) * c, :] = u * mixed


def sgu_mixer(z_d, ln_g, ln_b, w_s, b_s, chunks=4):
    bsz, t, _ = z_d.shape
    gw = GROUP_WIDTH
    tt = chunks * SGU_CHUNK
    bias = jnp.repeat(b_s.T, SGU_CG, axis=1)
    return pl.pallas_call(
        functools.partial(_sgu_kernel, chunks=chunks),
        grid=(bsz, t // tt),
        in_specs=[pl.BlockSpec((1, tt, 2 * gw), lambda b, i: (b, i, 0)),
                  pl.BlockSpec((1, gw), lambda b, i: (0, 0)),
                  pl.BlockSpec((1, gw), lambda b, i: (0, 0)),
                  pl.BlockSpec((SGU_GROUPS * SGU_CHUNK, SGU_CHUNK), lambda b, i: (0, 0)),
                  pl.BlockSpec((SGU_CHUNK, gw), lambda b, i: (0, 0))],
        out_specs=pl.BlockSpec((1, tt, gw), lambda b, i: (b, i, 0)),
        out_shape=jax.ShapeDtypeStruct((bsz, t, gw), F32),
        compiler_params=_cparams(("parallel", "parallel")),
        name="sgu_mixer",
    )(z_d, ln_g.reshape(1, gw), ln_b.reshape(1, gw),
      w_s.reshape(SGU_GROUPS * SGU_CHUNK, SGU_CHUNK), bias)


def _bucket_table():
    assert REL_MAX_DIST <= DSA_BLOCK + 1
    s = np.arange(DSA_BLOCK)[:, None]
    t = np.arange(DSA_BLOCK)[None, :]
    dist = np.stack([t - s, DSA_BLOCK + t - s, 2 * DSA_BLOCK + t - s])
    n = np.maximum(dist, 0)
    max_exact = REL_BUCKETS // 2
    nf = np.maximum(n, 1).astype(np.float32)
    large = max_exact + (np.log(nf / np.float32(max_exact)) / np.float32(math.log(REL_MAX_DIST / max_exact))
                         * np.float32(REL_BUCKETS - max_exact)).astype(np.int32)
    return np.where(n < max_exact, n, np.minimum(large, REL_BUCKETS - 1)).astype(np.int32)


def _bias_table_kernel(rb_ref, bucket_ref, o_ref):
    for back in range(3):
        bucket = bucket_ref[back]
        for h in range(DSA_HEADS):
            acc = jnp.zeros(bucket.shape, F32)
            for b in range(REL_BUCKETS):
                acc = jnp.where(bucket == b, rb_ref[b * DSA_HEADS + h], acc)
            o_ref[back, h] = acc


def rel_bias_tables(rel_bias):
    blk = DSA_BLOCK
    return pl.pallas_call(
        _bias_table_kernel,
        in_specs=[pl.BlockSpec(memory_space=pltpu.SMEM),
                  pl.BlockSpec((3, blk, blk), lambda: (0, 0, 0))],
        out_specs=pl.BlockSpec((3, DSA_HEADS, blk, blk), lambda: (0, 0, 0, 0)),
        out_shape=jax.ShapeDtypeStruct((3, DSA_HEADS, blk, blk), F32),
        name="rel_bias_tables",
    )(rel_bias.reshape(-1), jnp.asarray(_bucket_table()))


def _dsa_kernel(q_ref, k_ref, v_ref, qi_ref, kw_ref, qw_ref, toe_ref, o_ref,
                kpl_ref, kb16_ref, vt_ref, keys_ref, acc_ref, am_ref, lg_ref, p_ref,
                *, topk, n_blocks, idx_bits):
    blk = DSA_BLOCK
    sub = DSA_SUB
    i = pl.program_id(1)
    hd = DSA_HEADS * DSA_DH
    heads_per_half = 128 // IDX_DIM

    @pl.when(i == 0)
    def _():
        lane = lax.broadcasted_iota(jnp.int32, (blk, 128), 1)

        def build(kb, c):
            r0 = pl.multiple_of(kb * blk, blk)
            ki = jnp.where(lane < IDX_DIM, kw_ref[0, pl.ds(r0, blk), :], 0.0)
            for j in range(heads_per_half):
                kpl_ref[kb, j] = (ki if j == 0 else pltpu.roll(ki, j * IDX_DIM, axis=1)).astype(BF16)
            kb16_ref[kb] = k_ref[0, pl.ds(r0, blk), :].astype(BF16)
            vt_ref[kb] = jnp.transpose(v_ref[0, pl.ds(r0, blk), :]).astype(BF16)
            return c

        lax.fori_loop(0, n_blocks, build, 0)

    s_loc = lax.broadcasted_iota(jnp.int32, (blk, blk), 0)
    t_loc = lax.broadcasted_iota(jnp.int32, (blk, blk), 1)
    n_vis = i + 1

    qi_t = jnp.transpose(qi_ref[0]).astype(BF16)
    qi_halves = [qi_t[:128, :], qi_t[128:, :]]
    w_t = jnp.transpose(qw_ref[0])
    w_rows = [w_t[IDX_DIM + h:IDX_DIM + h + 1, :] * (IDX_HEADS ** -0.5) * (IDX_DIM ** -0.5)
              for h in range(IDX_HEADS)]
    s_sub = lax.broadcasted_iota(jnp.int32, (sub, blk), 0)
    t_sub = lax.broadcasted_iota(jnp.int32, (sub, blk), 1)

    def score_body(kb, c):
        for ci in range(blk // sub):
            rows = slice(ci * sub, (ci + 1) * sub)
            sc = jnp.zeros((sub, blk), F32)
            for half in range(2):
                for j in range(heads_per_half):
                    d = _dot(kpl_ref[kb, j, rows, :], qi_halves[half])
                    sc = sc + jnp.maximum(d, 0.0) * w_rows[half * heads_per_half + j]
            sc = jnp.where(sc == 0.0, 0.0, sc)
            bits = pltpu.bitcast(sc, jnp.int32)
            key = jnp.where(bits < 0, bits ^ jnp.int32(0x7FFFFFFF), bits)
            vis = (kb < i) | (s_sub + ci * sub <= t_sub)
            keys_ref[kb, rows, :] = jnp.where(vis, key, jnp.int32(INT_MIN))
        return c

    lax.fori_loop(0, n_vis, score_body, 0)

    def count(pred):
        def body(kb, acc):
            hit = jnp.where(pred(keys_ref[kb], kb), 1.0, 0.0)
            return acc + jnp.sum(hit.reshape(blk // 32, 32, blk), axis=0)
        acc = lax.fori_loop(0, n_vis, body, jnp.zeros((32, blk), F32))
        return jnp.sum(acc, axis=0, keepdims=True)

    kf = float(topk)
    zero = jnp.zeros((1, blk), jnp.int32)
    base = jnp.where(count(lambda key, kb: key >= 0) >= kf, zero, jnp.int32(INT_MIN))

    def bit_body(it, base):
        cand = base | lax.shift_left(jnp.int32(1), 30 - it)
        return jnp.where(count(lambda key, kb: key >= cand) >= kf, cand, base)

    thr = lax.fori_loop(0, 31, bit_body, base)

    n_gt = count(lambda key, kb: key > thr)
    n_eq = count(lambda key, kb: key == thr)
    need = kf - n_gt
    excess = jnp.where((n_eq > need) & (thr > jnp.int32(INT_MIN)), 1.0, 0.0)

    def tie_search():
        def tie_body(it, j):
            cand = j | lax.shift_left(jnp.int32(1), idx_bits - 1 - it)
            below = count(lambda key, kb: (key == thr) & (kb * blk + s_loc < cand))
            return jnp.where(below < need, cand, j)
        return lax.fori_loop(0, idx_bits, tie_body, zero)

    last = lax.cond(jnp.max(excess) > 0.0, tie_search, lambda: jnp.full((1, blk), 2 ** 30, jnp.int32))

    assert DSA_DH ** -0.5 == 0.125
    q_t = jnp.transpose(q_ref[0] * (DSA_DH ** -0.5))
    row_h = lax.broadcasted_iota(jnp.int32, (hd, 1), 0) // DSA_DH
    q_heads = [jnp.where(row_h == h, q_t, 0.0).astype(BF16) for h in range(DSA_HEADS)]
    acc_ref[...] = jnp.zeros_like(acc_ref)
    n_sub = blk // sub

    def att_body(kb, carry):
        ms, ls = carry
        back = jnp.minimum(i - kb, 2)
        for ci in range(n_sub):
            rows = slice(ci * sub, (ci + 1) * sub)
            key = keys_ref[kb, rows, :]
            sel = (((key > thr) | ((key == thr) & (kb * blk + ci * sub + s_sub <= last)))
                   & ((kb < i) | (s_sub + ci * sub <= t_sub)))
            am_ref[rows, :] = jnp.where(sel, 0.0, NEG_BIG)
        new_ms, alphas = [], []
        for h in range(DSA_HEADS):
            pm = jnp.full((8, blk), NEG_BIG, F32)
            for ci in range(n_sub):
                rows = slice(ci * sub, (ci + 1) * sub)
                lg = _dot(kb16_ref[kb, rows, :], q_heads[h]) + toe_ref[back, h, rows, :] + am_ref[rows, :]
                lg_ref[h, rows, :] = lg
                pm = jnp.maximum(pm, jnp.max(lg.reshape(sub // 8, 8, blk), axis=0))
            m_new = jnp.maximum(ms[h], jnp.max(pm, axis=0, keepdims=True))
            new_ms.append(m_new)
            alphas.append(jnp.exp(ms[h] - m_new))
        new_ls = []
        for h in range(DSA_HEADS):
            ps = jnp.zeros((8, blk), F32)
            for ci in range(n_sub):
                rows = slice(ci * sub, (ci + 1) * sub)
                p = jnp.exp(lg_ref[h, rows, :] - new_ms[h])
                ps = ps + jnp.sum(p.reshape(sub // 8, 8, blk), axis=0)
                p_ref[h, rows, :] = p.astype(BF16)
            new_ls.append(ls[h] * alphas[h] + jnp.sum(ps, axis=0, keepdims=True))
        for h in range(DSA_HEADS):
            hrows = slice(h * DSA_DH, (h + 1) * DSA_DH)
            acc_ref[hrows, :] = acc_ref[hrows, :] * alphas[h] + _dot(vt_ref[kb, hrows, :], p_ref[h])
        return tuple(new_ms), tuple(new_ls)

    init = (tuple(jnp.full((1, blk), 0.01 * NEG_BIG, F32) for _ in range(DSA_HEADS)),
            tuple(jnp.zeros((1, blk), F32) for _ in range(DSA_HEADS)))
    _, ls = lax.fori_loop(0, n_vis, att_body, init)
    for h in range(DSA_HEADS):
        rows = slice(h * DSA_DH, (h + 1) * DSA_DH)
        acc_ref[rows, :] = acc_ref[rows, :] * (1.0 / ls[h])
    o_ref[0] = jnp.transpose(acc_ref[...])


def dsa_mixer(z_c, toe):
    bsz, t, _ = z_c.shape
    blk = DSA_BLOCK
    n_blocks = t // blk
    assert t % blk == 0
    topk = min(DSA_TOPK_MAX, t // 4)
    hd = DSA_HEADS * DSA_DH
    kernel = functools.partial(_dsa_kernel, topk=topk, n_blocks=n_blocks,
                               idx_bits=max(1, (t - 1).bit_length()))
    return pl.pallas_call(
        kernel,
        grid=(bsz, n_blocks),
        in_specs=[pl.BlockSpec((1, blk, hd), lambda b, i: (b, i, 0)),
                  pl.BlockSpec((1, t, hd), lambda b, i: (b, 0, 1)),
                  pl.BlockSpec((1, t, hd), lambda b, i: (b, 0, 2)),
                  pl.BlockSpec((1, blk, hd), lambda b, i: (b, i, 3)),
                  pl.BlockSpec((1, t, 128), lambda b, i: (b, 0, 8)),
                  pl.BlockSpec((1, blk, 128), lambda b, i: (b, i, 8)),
                  pl.BlockSpec((3, DSA_HEADS, blk, blk), lambda b, i: (0, 0, 0, 0))],
        out_specs=pl.BlockSpec((1, blk, hd), lambda b, i: (b, i, 0)),
        out_shape=jax.ShapeDtypeStruct((bsz, t, hd), F32),
        scratch_shapes=[pltpu.VMEM((n_blocks, 128 // IDX_DIM, blk, 128), BF16),
                        pltpu.VMEM((n_blocks, blk, hd), BF16),
                        pltpu.VMEM((n_blocks, hd, blk), BF16),
                        pltpu.VMEM((n_blocks, blk, blk), jnp.int32),
                        pltpu.VMEM((hd, blk), F32),
                        pltpu.VMEM((blk, blk), F32),
                        pltpu.VMEM((DSA_HEADS, blk, blk), F32),
                        pltpu.VMEM((DSA_HEADS, blk, blk), BF16)],
        compiler_params=_cparams(("parallel", "arbitrary")),
        name="dsa_mixer",
    )(z_c, z_c, z_c, z_c, z_c, z_c, toe)


def _out_proj_kernel(h_ref, a_ref, b_ref, c_ref, d_ref, w_ref, o_ref):
    gw = GROUP_WIDTH
    acc = h_ref[...]
    for gi, r in enumerate((a_ref, b_ref, c_ref, d_ref)):
        acc = acc + _dot(r[...].astype(BF16), w_ref[gi * gw:(gi + 1) * gw, :])
    o_ref[...] = acc


def out_proj(h, outs, w, tm=512):
    m, d = h.shape
    tm = min(tm, m)
    gw = GROUP_WIDTH
    return pl.pallas_call(
        _out_proj_kernel,
        grid=(m // tm,),
        in_specs=[pl.BlockSpec((tm, d), lambda i: (i, 0))]
                 + [pl.BlockSpec((tm, gw), lambda i: (i, 0))] * 4
                 + [pl.BlockSpec((4 * gw, d), lambda i: (0, 0))],
        out_specs=pl.BlockSpec((tm, d), lambda i: (i, 0)),
        out_shape=jax.ShapeDtypeStruct((m, d), F32),
        compiler_params=_cparams(("parallel",)),
        name="out_proj",
    )(h, *outs, w)


def _xattn_kernel(h_ref, g_ref, wq_ref, k_ref, v_ref, wo_ref, o_ref):
    x = h_ref[0]
    hd = XA_HEADS * XA_DH
    q = _dot(_rms(x, g_ref[...]).astype(BF16), wq_ref[...]).astype(BF16)
    k = k_ref[0]
    v = v_ref[0]
    lane_h = lax.broadcasted_iota(jnp.int32, (1, hd), 1) // XA_DH
    o = jnp.zeros((x.shape[0], hd), F32)
    for h in range(XA_HEADS):
        s = _dot_nt(q, jnp.where(lane_h == h, k, 0.0).astype(BF16)) * (XA_DH ** -0.5)
        p = jnp.exp(s - jnp.max(s, axis=-1, keepdims=True))
        p = p / jnp.sum(p, axis=-1, keepdims=True)
        o = o + _dot(p.astype(BF16), jnp.where(lane_h == h, v, 0.0).astype(BF16))
    o_ref[0] = x + _dot(o.astype(BF16), wo_ref[...])


def cross_attn(h3, g, wq, k, v, wo, tm=512):
    bsz, t, d = h3.shape
    tm = min(tm, t)
    mlen = k.shape[1]
    hd = XA_HEADS * XA_DH
    return pl.pallas_call(
        _xattn_kernel,
        grid=(bsz, t // tm),
        in_specs=[pl.BlockSpec((1, tm, d), lambda b, i: (b, i, 0)),
                  pl.BlockSpec((1, d), lambda b, i: (0, 0)),
                  pl.BlockSpec((d, hd), lambda b, i: (0, 0)),
                  pl.BlockSpec((1, mlen, hd), lambda b, i: (b, 0, 0)),
                  pl.BlockSpec((1, mlen, hd), lambda b, i: (b, 0, 0)),
                  pl.BlockSpec((hd, d), lambda b, i: (0, 0))],
        out_specs=pl.BlockSpec((1, tm, d), lambda b, i: (b, i, 0)),
        out_shape=jax.ShapeDtypeStruct((bsz, t, d), F32),
        compiler_params=_cparams(("parallel", "parallel")),
        name="cross_attn",
    )(h3, g.reshape(1, d), wq, k, v, wo)


def _ffn_kernel(h_ref, g_ref, w1_ref, w3_ref, w2_ref, o_ref, hn_ref, acc_ref):
    f = pl.program_id(1)

    @pl.when(f == 0)
    def _():
        hn_ref[...] = _rms(h_ref[...], g_ref[...]).astype(BF16)
        acc_ref[...] = jnp.zeros_like(acc_ref)

    a = _dot(hn_ref[...], w1_ref[...])
    b = _dot(hn_ref[...], w3_ref[...])
    acc_ref[...] += _dot((a * jax.nn.sigmoid(a) * b).astype(BF16), w2_ref[...])

    @pl.when(f == pl.num_programs(1) - 1)
    def _():
        o_ref[...] = h_ref[...] + acc_ref[...]


def ffn(h, g, w1, w3, w2, tm=1024, tf=256):
    m, d = h.shape
    tm = min(tm, m)
    nf = w1.shape[1]
    assert nf % tf == 0
    return pl.pallas_call(
        _ffn_kernel,
        grid=(m // tm, nf // tf),
        in_specs=[pl.BlockSpec((tm, d), lambda i, f: (i, 0)),
                  pl.BlockSpec((1, d), lambda i, f: (0, 0)),
                  pl.BlockSpec((d, tf), lambda i, f: (0, f)),
                  pl.BlockSpec((d, tf), lambda i, f: (0, f)),
                  pl.BlockSpec((tf, d), lambda i, f: (f, 0))],
        out_specs=pl.BlockSpec((tm, d), lambda i, f: (i, 0)),
        out_shape=jax.ShapeDtypeStruct((m, d), F32),
        scratch_shapes=[pltpu.VMEM((tm, d), BF16), pltpu.VMEM((tm, d), F32)],
        compiler_params=_cparams(("parallel", "arbitrary")),
        name="ffn",
    )(h, g.reshape(1, d), w1, w3, w2)


GATE_LANES = 128
MOE_CHUNK = 256
MOE_ROWS = 128
MOE_TOKENS = 2048


def _router_kernel(h_ref, g_ref, wr_ref, hn_ref, gate_ref, cnt_ref):
    hn = _rms(h_ref[...], g_ref[...])
    hn_ref[...] = hn.astype(BF16)
    tm = hn.shape[0]
    lane = lax.broadcasted_iota(jnp.int32, (tm, GATE_LANES), 1)
    logits = jnp.where(lane < N_EXPERTS, _dot_f32(hn, wr_ref[...]), -jnp.inf)
    m1 = jnp.max(logits, axis=-1, keepdims=True)
    i1 = jnp.min(jnp.where(logits == m1, lane, GATE_LANES), axis=-1, keepdims=True)
    rest = jnp.where(lane == i1, -jnp.inf, logits)
    m2 = jnp.max(rest, axis=-1, keepdims=True)
    i2 = jnp.min(jnp.where(rest == m2, lane, GATE_LANES), axis=-1, keepdims=True)
    e2 = jnp.exp(m2 - m1)
    g1 = 1.0 / (1.0 + e2)
    gates = jnp.where(lane == i1, g1, 0.0) + jnp.where(lane == i2, e2 * g1, 0.0)
    gates = jnp.where(lane == N_EXPERTS, i1.astype(F32), gates)
    gate_ref[...] = jnp.where(lane == N_EXPERTS + 1, i2.astype(F32), gates)
    sel = jnp.where((lane == i1) | (lane == i2), 1.0, 0.0)
    for c in range(tm // MOE_CHUNK):
        cnt_ref[c] = jnp.sum(sel[c * MOE_CHUNK:(c + 1) * MOE_CHUNK, :], axis=0, keepdims=True)


def route_tokens(h, g, wr, tm=512):
    m, d = h.shape
    tm = min(tm, m)
    assert tm % MOE_CHUNK == 0
    wrp = jnp.zeros((d, GATE_LANES), F32).at[:, :N_EXPERTS].set(wr)
    return pl.pallas_call(
        _router_kernel,
        grid=(m // tm,),
        in_specs=[pl.BlockSpec((tm, d), lambda i: (i, 0)),
                  pl.BlockSpec((1, d), lambda i: (0, 0)),
                  pl.BlockSpec((d, GATE_LANES), lambda i: (0, 0))],
        out_specs=[pl.BlockSpec((tm, d), lambda i: (i, 0)),
                   pl.BlockSpec((tm, GATE_LANES), lambda i: (i, 0)),
                   pl.BlockSpec((tm // MOE_CHUNK, 1, GATE_LANES), lambda i: (i, 0, 0))],
        out_shape=[jax.ShapeDtypeStruct((m, d), BF16), jax.ShapeDtypeStruct((m, GATE_LANES), F32),
                   jax.ShapeDtypeStruct((m // MOE_CHUNK, 1, GATE_LANES), F32)],
        compiler_params=_cparams(("parallel",)),
        name="router",
    )(h, g.reshape(1, d), wrp)


def _moe_kernel(cb_ref, hn_ref, gate_ref, w1_ref, w3_ref, w2_ref, y_ref,
                rank_col, rank_row, xs_ref, yacc_ref, rc_ref, gc_ref, xg_ref):
    t = pl.program_id(0)
    e = pl.program_id(1)
    f = pl.program_id(2)
    ts, d = hn_ref.shape
    ch, rb = MOE_CHUNK, MOE_ROWS
    n_ch = ts // ch
    lane = lax.broadcasted_iota(jnp.int32, (1, GATE_LANES), 1)

    def before(c):
        return cb_ref[(t * (n_ch + 1) + c) * N_EXPERTS + e]

    n_blocks = (before(n_ch) + rb - 1) // rb

    @pl.when((e == 0) & (f == 0))
    def _():
        y_ref[...] = jnp.zeros_like(y_ref)
        strict_lower = (lax.broadcasted_iota(jnp.int32, (ch, ch), 1)
                        < lax.broadcasted_iota(jnp.int32, (ch, ch), 0)).astype(BF16)
        lane_f = lane.astype(F32)
        offs = jnp.zeros((1, GATE_LANES), F32)
        for c in range(n_ch):
            rows = slice(c * ch, (c + 1) * ch)
            g = gate_ref[rows, :]
            sel = jnp.where((lane_f == g[:, N_EXPERTS:N_EXPERTS + 1])
                            | (lane_f == g[:, N_EXPERTS + 1:N_EXPERTS + 2]), 1.0, 0.0)
            r = _dot(strict_lower, sel.astype(BF16)) + offs
            r = jnp.where(sel > 0.0, r, -1.0)
            rank_col[rows, :] = r
            rank_row[:, rows] = jnp.transpose(r)
            offs = offs + jnp.sum(sel, axis=0, keepdims=True)

    @pl.when(f == 0)
    def _():
        for c in range(n_ch):
            rows = slice(c * ch, (c + 1) * ch)
            rcol = jnp.sum(jnp.where(lane == e, rank_col[rows, :], 0.0), axis=1, keepdims=True)
            rc_ref[rows, :] = jnp.broadcast_to(rcol, (ch, rb))
            gcol = jnp.sum(jnp.where(lane == e, gate_ref[rows, :], 0.0), axis=1, keepdims=True)
            gc_ref[rows, :] = jnp.broadcast_to(gcol, (ch, GATE_LANES))

        def gather_block(b, carry):
            r0 = pl.multiple_of(b * rb, rb)
            want = (r0 + lax.broadcasted_iota(jnp.int32, (rb, ch), 0)).astype(F32)
            xg_ref[...] = jnp.zeros_like(xg_ref)
            for c in range(n_ch):
                @pl.when((before(c) < r0 + rb) & (before(c + 1) > r0))
                def _(c=c):
                    ranks = rank_row[pl.ds(e, 1), c * ch:(c + 1) * ch]
                    onehot = jnp.where(ranks == want, 1.0, 0.0).astype(BF16)
                    xg_ref[...] += _dot(onehot, hn_ref[c * ch:(c + 1) * ch, :])
            xs_ref[pl.ds(r0, rb), :] = xg_ref[...].astype(BF16)
            yacc_ref[pl.ds(r0, rb), :] = jnp.zeros((rb, d), F32)
            return carry

        lax.fori_loop(0, n_blocks, gather_block, 0)

    def ffn_rows(r0, rows):
        x = xs_ref[pl.ds(r0, rows), :]
        a = _dot(x, w1_ref[0])
        g3 = _dot(x, w3_ref[0])
        yacc_ref[pl.ds(r0, rows), :] += _dot((a * jax.nn.sigmoid(a) * g3).astype(BF16), w2_ref[0])

    n_quads = n_blocks // 4

    def ffn_quad(j, carry):
        ffn_rows(pl.multiple_of(j * (4 * rb), 4 * rb), 4 * rb)
        return carry

    lax.fori_loop(0, n_quads, ffn_quad, 0)
    tail = pl.multiple_of(n_quads * (4 * rb), 4 * rb)

    @pl.when((n_blocks & 2) != 0)
    def _():
        ffn_rows(tail, 2 * rb)

    @pl.when((n_blocks & 1) != 0)
    def _():
        ffn_rows(pl.multiple_of(tail + (n_blocks & 2) * rb, rb), rb)

    @pl.when(f == pl.num_programs(2) - 1)
    def _():
        def scatter_block(b, carry):
            r0 = pl.multiple_of(b * rb, rb)
            yv = yacc_ref[pl.ds(r0, rb), :]
            y_hi = yv.astype(BF16)
            y_lo = (yv - y_hi.astype(F32)).astype(BF16)
            y_split = jnp.concatenate([y_hi, y_lo], axis=0)
            want = (r0 + lax.broadcasted_iota(jnp.int32, (ch, rb), 1)).astype(F32)
            for c in range(n_ch):
                @pl.when((before(c) < r0 + rb) & (before(c + 1) > r0))
                def _(c=c):
                    rows = slice(c * ch, (c + 1) * ch)
                    onehot = jnp.where(rc_ref[rows, :] == want, 1.0, 0.0).astype(BF16)
                    part = _dot(jnp.concatenate([onehot, onehot], axis=1), y_split)
                    gate = jnp.concatenate([gc_ref[rows, :]] * (d // GATE_LANES), axis=1)
                    y_ref[rows, :] += gate * part
            return carry

        lax.fori_loop(0, n_blocks, scatter_block, 0)


def moe(hn, gates, cnt, w1, w3, w2, tf=512):
    m, d = hn.shape
    ts = min(MOE_TOKENS, m)
    n_exp, _, nf = w1.shape
    assert nf % tf == 0 and m % ts == 0 and ts % MOE_CHUNK == 0 and MOE_ROWS == GATE_LANES
    n_ch = ts // MOE_CHUNK
    counts = cnt.reshape(m // ts, n_ch, GATE_LANES)[:, :, :n_exp].astype(jnp.int32)
    before = jnp.concatenate([jnp.zeros((m // ts, 1, n_exp), jnp.int32), jnp.cumsum(counts, axis=1)], axis=1)
    grid_spec = pltpu.PrefetchScalarGridSpec(
        num_scalar_prefetch=1,
        grid=(m // ts, n_exp, nf // tf),
        in_specs=[pl.BlockSpec((ts, d), lambda t, e, f, cb: (t, 0)),
                  pl.BlockSpec((ts, GATE_LANES), lambda t, e, f, cb: (t, 0)),
                  pl.BlockSpec((1, d, tf), lambda t, e, f, cb: (e, 0, f)),
                  pl.BlockSpec((1, d, tf), lambda t, e, f, cb: (e, 0, f)),
                  pl.BlockSpec((1, tf, d), lambda t, e, f, cb: (e, f, 0))],
        out_specs=pl.BlockSpec((ts, d), lambda t, e, f, cb: (t, 0)),
        scratch_shapes=[pltpu.VMEM((ts, GATE_LANES), F32),
                        pltpu.VMEM((GATE_LANES, ts), F32),
                        pltpu.VMEM((ts, d), BF16),
                        pltpu.VMEM((ts, d), F32),
                        pltpu.VMEM((ts, MOE_ROWS), F32),
                        pltpu.VMEM((ts, GATE_LANES), F32),
                        pltpu.VMEM((MOE_ROWS, d), F32)])
    return pl.pallas_call(
        _moe_kernel,
        grid_spec=grid_spec,
        out_shape=jax.ShapeDtypeStruct((m, d), F32),
        compiler_params=_cparams(("parallel", "arbitrary", "arbitrary")),
        name="moe",
    )(before.reshape(-1), hn, gates, w1, w3, w2)


def moe_block(h, g, wr, w1, w3, w2):
    hn, gates, cnt = route_tokens(h, g, wr)
    return moe(hn, gates, cnt, w1.astype(BF16), w3.astype(BF16), w2.astype(BF16))


def _final_norm_kernel(g_ref, *refs):
    *x_refs, o_ref = refs
    x = x_refs[0][...]
    for r in x_refs[1:]:
        x = x + r[...]
    o_ref[...] = _rms(x, g_ref[...])


def final_rmsnorm(g, *xs, tm=1024):
    m, d = xs[0].shape
    tm = min(tm, m)
    return pl.pallas_call(
        _final_norm_kernel,
        grid=(m // tm,),
        in_specs=[pl.BlockSpec((1, d), lambda i: (0, 0))] + [pl.BlockSpec((tm, d), lambda i: (i, 0))] * len(xs),
        out_specs=pl.BlockSpec((tm, d), lambda i: (i, 0)),
        out_shape=jax.ShapeDtypeStruct((m, d), F32),
        compiler_params=_cparams(("parallel",)),
        name="final_norm",
    )(g.reshape(1, d), *xs)


def _pad_cols(w, width):
    return jnp.pad(w, ((0, 0), (0, width - w.shape[1])))


def _in_proj_weight(w_in):
    a_end = 2 * GLA_HEADS * GLA_DK + 2 * GROUP_WIDTH + GLA_RANK
    b_end = a_end + GROUP_WIDTH
    c_end = b_end + 3 * GROUP_WIDTH + IDX_HEADS * IDX_DIM + IDX_DIM + IDX_HEADS
    assert w_in.shape[1] == c_end + 2 * GROUP_WIDTH
    return jnp.concatenate([_pad_cols(w_in[:, :a_end], A_WIDTH), w_in[:, a_end:b_end],
                            _pad_cols(w_in[:, b_end:c_end], C_WIDTH), w_in[:, c_end:]], axis=1).astype(BF16)


def hybrid_layer(h, mem2, toe, p, bsz, t):
    m, d = h.shape
    z_a, z_b, z_c, z_d = norm_matmul(h, p["norm_mix"], _in_proj_weight(p["w_in"]),
                                     (A_WIDTH, B_WIDTH, C_WIDTH, D_WIDTH))
    o_a = gla_mixer(z_a.reshape(bsz, t, A_WIDTH), p["gla_wa2"], p["gla_ba"], p["gla_norm"])
    o_b = pool_mixer(z_b.reshape(bsz, t, B_WIDTH), p["pool_w"], p["pool_b"].reshape(-1), p["pool_scale"])
    o_c = dsa_mixer(z_c.reshape(bsz, t, C_WIDTH), toe)
    o_d = sgu_mixer(z_d.reshape(bsz, t, D_WIDTH), p["sgu_ln_g"], p["sgu_ln_b"], p["sgu_w"], p["sgu_b"])
    h = out_proj(h, [o.reshape(m, GROUP_WIDTH) for o in (o_a, o_b, o_c, o_d)], p["w_out"].astype(BF16))

    hd = XA_HEADS * XA_DH
    k, v = norm_matmul(mem2, p["norm_mem"], p["xa_wkv"].astype(BF16), (hd, hd))
    mlen = mem2.shape[0] // bsz
    h = cross_attn(h.reshape(bsz, t, d), p["norm_xa"], p["xa_wq"].astype(BF16),
                   k.reshape(bsz, mlen, hd), v.reshape(bsz, mlen, hd), p["xa_wo"].astype(BF16))
    return h.reshape(m, d)


def kernel(x, mem, rel_bias, final_norm, norm_mix, w_in, gla_wa2, gla_ba, gla_norm, pool_w, pool_b,
           pool_scale, sgu_ln_g, sgu_ln_b, sgu_w, sgu_b, w_out, norm_xa, norm_mem, xa_wq, xa_wkv, xa_wo,
           norm_ffn, ffn_w1, ffn_w3, ffn_w2, router, moe_w1, moe_w3, moe_w2):
    bsz, t, d = x.shape
    depth = norm_mix.shape[0]
    h = x.reshape(bsz * t, d)
    mem2 = mem.reshape(-1, d)
    toe = rel_bias_tables(rel_bias)
    pending = None
    for i in range(depth):
        if pending is not None:
            h, pending = h + pending, None
        p = dict(norm_mix=norm_mix[i], w_in=w_in[i], gla_wa2=gla_wa2[i], gla_ba=gla_ba[i],
                 gla_norm=gla_norm[i], pool_w=pool_w[i], pool_b=pool_b[i], pool_scale=pool_scale[i],
                 sgu_ln_g=sgu_ln_g[i], sgu_ln_b=sgu_ln_b[i], sgu_w=sgu_w[i], sgu_b=sgu_b[i],
                 w_out=w_out[i], norm_xa=norm_xa[i], norm_mem=norm_mem[i], xa_wq=xa_wq[i],
                 xa_wkv=xa_wkv[i], xa_wo=xa_wo[i])
        h = hybrid_layer(h, mem2, toe, p, bsz, t)
        j = i // 2
        if i % 2 == 0:
            h = ffn(h, norm_ffn[i], ffn_w1[j].astype(BF16), ffn_w3[j].astype(BF16), ffn_w2[j].astype(BF16))
        else:
            pending = moe_block(h, norm_ffn[i], router[j], moe_w1[j], moe_w3[j], moe_w2[j])
    xs = (h,) if pending is None else (h, pending)
    return final_rmsnorm(final_norm, *xs).reshape(bsz, t, d)
```

```python
import functools
import math

import jax
import jax.numpy as jnp
import numpy as np
from jax import lax
from jax.experimental import pallas as pl
from jax.experimental.pallas import tpu as pltpu

F32 = jnp.float32
BF16 = jnp.bfloat16
EPS = 1e-6

GROUP_WIDTH = 256

GLA_HEADS = 4
GLA_DV = 64
GLA_DK = 32
GLA_RANK = 16
GLA_TAU = 16.0
GLA_CHUNK = 64
GLA_GROUP = 8

POOL_WINDOWS = (2, 4, 8, 16)
POOL_CG = 64

DSA_HEADS = 4
DSA_DH = 64
IDX_HEADS = 8
IDX_DIM = 32
DSA_TOPK_MAX = 256
DSA_BLOCK = 256
DSA_SUB = 64

SGU_GROUPS = 4
SGU_CHUNK = 128
SGU_CG = 64

REL_BUCKETS = 32
REL_MAX_DIST = 128

XA_HEADS = 4
XA_DH = 64

N_EXPERTS = 8

A_WIDTH = 896
B_WIDTH = 256
C_WIDTH = 1152
D_WIDTH = 512

INT_MIN = -(2 ** 31)
NEG_BIG = -1e30
VMEM_LIMIT = 56 * 1024 * 1024


def _cparams(sem):
    return pltpu.CompilerParams(dimension_semantics=sem, vmem_limit_bytes=VMEM_LIMIT)


def _dot(a, b):
    return jnp.dot(a, b, preferred_element_type=F32)


def _dot_nt(a, b):
    return lax.dot_general(a, b, (((1,), (1,)), ((), ())), preferred_element_type=F32)


def _dot_tn(a, b):
    return lax.dot_general(a, b, (((0,), (0,)), ((), ())), preferred_element_type=F32)


def _dot_f32(a, b):
    return jnp.dot(a, b, preferred_element_type=F32, precision=lax.Precision.HIGHEST)


def _split_bf16(x, terms):
    parts = []
    for _ in range(terms):
        p = x.astype(BF16)
        parts.append(p)
        x = x - p.astype(F32)
    return parts


def _dot_exact_rhs(a, b, terms):
    out = None
    for p in _split_bf16(a, terms):
        d = _dot(p, b)
        out = d if out is None else out + d
    return out


def _dot_exact_lhs(a, b, terms):
    out = None
    for p in _split_bf16(b, terms):
        d = _dot(a, p)
        out = d if out is None else out + d
    return out


def _rms(x, g):
    return x * lax.rsqrt(jnp.mean(x * x, axis=-1, keepdims=True) + EPS) * g


def _norm_matmul_kernel(h_ref, g_ref, w_ref, *out_refs, widths):
    hb = _rms(h_ref[...], g_ref[...]).astype(BF16)
    off = 0
    for o_ref, wd in zip(out_refs, widths):
        o_ref[...] = _dot(hb, w_ref[:, off:off + wd]).astype(o_ref.dtype)
        off += wd


def norm_matmul(h, g, w, widths, tm=512):
    m, d = h.shape
    tm = min(tm, m)
    n = sum(widths)
    return pl.pallas_call(
        functools.partial(_norm_matmul_kernel, widths=widths),
        grid=(m // tm,),
        in_specs=[pl.BlockSpec((tm, d), lambda i: (i, 0)),
                  pl.BlockSpec((1, d), lambda i: (0, 0)),
                  pl.BlockSpec((d, n), lambda i: (0, 0))],
        out_specs=[pl.BlockSpec((tm, wd), lambda i: (i, 0)) for wd in widths],
        out_shape=[jax.ShapeDtypeStruct((m, wd), F32) for wd in widths],
        compiler_params=_cparams(("parallel",)),
        name="norm_matmul",
    )(h, g.reshape(1, d), w)


def _log_sigmoid(x):
    return jnp.minimum(x, 0.0) - jnp.log1p(jnp.exp(-jnp.abs(x)))


def _gla_kernel(z_ref, wa2_ref, ba_ref, ng_ref, o_ref, s_ref, *, n_groups):
    c = GLA_CHUNK
    grp = GLA_GROUP * c
    hk = GLA_HEADS * GLA_DK
    hv = GLA_HEADS * GLA_DV
    s_ref[...] = jnp.zeros_like(s_ref)

    head_k = lax.broadcasted_iota(jnp.int32, (1, hk), 1) // GLA_DK
    head_v = lax.broadcasted_iota(jnp.int32, (1, hv), 1) // GLA_DV
    g_row = lax.broadcasted_iota(jnp.int32, (grp, grp), 0)
    g_col = lax.broadcasted_iota(jnp.int32, (grp, grp), 1)
    tril = (((g_row // c) == (g_col // c)) & (g_col <= g_row)).astype(BF16)
    causal4 = (lax.broadcasted_iota(jnp.int32, (GLA_HEADS * c, c), 1)
               <= lax.broadcasted_iota(jnp.int32, (GLA_HEADS * c, c), 0) % c)
    state_mask = (lax.broadcasted_iota(jnp.int32, (hk, hv), 0) // GLA_DK
                  == lax.broadcasted_iota(jnp.int32, (hk, hv), 1) // GLA_DV)
    norm_mat = jnp.where(lax.broadcasted_iota(jnp.int32, (hv, hv), 0) // GLA_DV
                         == lax.broadcasted_iota(jnp.int32, (hv, hv), 1) // GLA_DV,
                         1.0 / GLA_DV, 0.0).astype(BF16)
    wa2 = wa2_ref[...]
    ba = ba_ref[...]
    ng = ng_ref[...]

    def body(n, carry):
        r0 = pl.multiple_of(n * grp, grp)
        z = z_ref[0, pl.ds(r0, grp), :]
        q, k, v, g, lr = z[:, 0:128], z[:, 128:256], z[:, 256:512], z[:, 512:768], z[:, 768:896]
        log_a = _log_sigmoid(_dot_f32(lr, wa2) + ba) / GLA_TAU
        b = _dot_exact_lhs(tril, log_a, 3)
        b_end = jnp.concatenate([jnp.broadcast_to(b[(ci + 1) * c - 1:(ci + 1) * c, :], (c, hk))
                                 for ci in range(GLA_GROUP)], axis=0)
        q_t = q * (GLA_DK ** -0.5) * jnp.exp(b)
        q_tb = q_t.astype(BF16)
        k_t = (k * jnp.exp(-b)).astype(BF16)
        k_dec = (k * jnp.exp(b_end - b)).astype(BF16)
        vb = v.astype(BF16)
        outs = []
        for ci in range(GLA_GROUP):
            rows = slice(ci * c, (ci + 1) * c)
            q4 = jnp.concatenate([jnp.where(head_k == h, q_t[rows], 0.0) for h in range(GLA_HEADS)],
                                 axis=0).astype(BF16)
            att = jnp.where(causal4, _dot_nt(q4, k_t[rows]), 0.0)
            r = _dot(att.astype(BF16), vb[rows])
            o = _dot(q_tb[rows], s_ref[...].astype(BF16))
            for h in range(GLA_HEADS):
                o = o + jnp.where(head_v == h, r[h * c:(h + 1) * c, :], 0.0)
            outs.append(o)
            kv = jnp.where(state_mask, _dot_tn(k_dec[rows], vb[rows]), 0.0)
            last = b_end[ci * c:ci * c + 1, :]
            dec = jnp.exp(jnp.transpose(jnp.broadcast_to(last, (hk, hk))))
            s_ref[...] = s_ref[...] * jnp.concatenate([dec, dec], axis=1) + kv
        o = jnp.concatenate(outs, axis=0)
        o = o * lax.rsqrt(_dot_exact_rhs(o * o, norm_mat, 2) + EPS) * ng
        o_ref[0, pl.ds(r0, grp), :] = o * (g * jax.nn.sigmoid(g))
        return carry

    lax.fori_loop(0, n_groups, body, 0)


def gla_mixer(z_a, wa2, ba, norm_g):
    bsz, t, _ = z_a.shape
    hk = GLA_HEADS * GLA_DK
    wa2p = jnp.zeros((128, hk), F32).at[:GLA_RANK].set(wa2)
    return pl.pallas_call(
        functools.partial(_gla_kernel, n_groups=t // (GLA_GROUP * GLA_CHUNK)),
        grid=(bsz,),
        in_specs=[pl.BlockSpec((1, t, A_WIDTH), lambda b: (b, 0, 0)),
                  pl.BlockSpec((128, hk), lambda b: (0, 0)),
                  pl.BlockSpec((1, hk), lambda b: (0, 0)),
                  pl.BlockSpec((1, GROUP_WIDTH), lambda b: (0, 0))],
        out_specs=pl.BlockSpec((1, t, GROUP_WIDTH), lambda b: (b, 0, 0)),
        out_shape=jax.ShapeDtypeStruct((bsz, t, GROUP_WIDTH), F32),
        scratch_shapes=[pltpu.VMEM((hk, GROUP_WIDTH), F32)],
        compiler_params=_cparams(("parallel",)),
        name="gla_mixer",
    )(z_a, wa2p, ba.reshape(1, hk), norm_g.reshape(1, GROUP_WIDTH))


def _pool_kernel(u_ref, w_ref, b_ref, sc_ref, o_ref):
    u = u_ref[0]
    t, gw = u.shape
    row = lax.broadcasted_iota(jnp.int32, (t, gw), 0)
    grp = lax.broadcasted_iota(jnp.int32, (t, gw), 1) // POOL_CG

    def shifted(x, k):
        return jnp.where(row >= k, pltpu.roll(x, k, axis=0), 0.0)

    s = u
    p = jnp.zeros_like(u)
    for gi, win in enumerate(POOL_WINDOWS):
        half = win // 2
        s = s + shifted(s, half)
        cnt = jnp.minimum(row + 1, win).astype(F32)
        p = jnp.where(grp == gi, s / cnt - u, p)
    y = _dot(p.astype(BF16), w_ref[...]) + b_ref[...]
    o_ref[0] = y * sc_ref[...]


def pool_mixer(z_b, w, b, scale):
    assert POOL_WINDOWS == (2, 4, 8, 16)
    bsz, t, gw = z_b.shape
    w_bd = jnp.zeros((gw, gw), F32)
    for gi in range(len(POOL_WINDOWS)):
        w_bd = w_bd.at[gi * POOL_CG:(gi + 1) * POOL_CG, gi * POOL_CG:(gi + 1) * POOL_CG].set(w[gi])
    return pl.pallas_call(
        _pool_kernel,
        grid=(bsz,),
        in_specs=[pl.BlockSpec((1, t, gw), lambda i: (i, 0, 0)),
                  pl.BlockSpec((gw, gw), lambda i: (0, 0)),
                  pl.BlockSpec((1, gw), lambda i: (0, 0)),
                  pl.BlockSpec((1, gw), lambda i: (0, 0))],
        out_specs=pl.BlockSpec((1, t, gw), lambda i: (i, 0, 0)),
        out_shape=jax.ShapeDtypeStruct((bsz, t, gw), F32),
        compiler_params=_cparams(("parallel",)),
        name="pool_mixer",
    )(z_b, w_bd.astype(BF16), b.reshape(1, gw), scale.reshape(1, gw))


def _sgu_kernel(z_ref, lg_ref, lb_ref, w_ref, bm_ref, o_ref, *, chunks):
    c = SGU_CHUNK
    gw = GROUP_WIDTH
    rows = SGU_GROUPS * c
    tri = (lax.broadcasted_iota(jnp.int32, (rows, c), 1)
           <= lax.broadcasted_iota(jnp.int32, (rows, c), 0) % c)
    ws = jnp.where(tri, w_ref[...], 0.0).astype(BF16)
    grp = lax.broadcasted_iota(jnp.int32, (1, gw), 1) // SGU_CG
    for ci in range(chunks):
        z = jax.nn.gelu(z_ref[0, ci * c:(ci + 1) * c, :], approximate=True)
        u, v = z[:, :gw], z[:, gw:]
        mu = jnp.mean(v, axis=-1, keepdims=True)
        var = jnp.mean(jnp.square(v - mu), axis=-1, keepdims=True)
        vn = (v - mu) * lax.rsqrt(var + EPS) * lg_ref[...] + lb_ref[...]
        r = _dot(ws, vn.astype(BF16))
        mixed = bm_ref[...]
        for g in range(SGU_GROUPS):
            mixed = mixed + jnp.where(grp == g, r[g * c:(g + 1) * c, :], 0.0)
        o_ref[0, ci * c:(ci + 1) * c, :] = u * mixed


def sgu_mixer(z_d, ln_g, ln_b, w_s, b_s, chunks=4):
    bsz, t, _ = z_d.shape
    gw = GROUP_WIDTH
    tt = chunks * SGU_CHUNK
    bias = jnp.repeat(b_s.T, SGU_CG, axis=1)
    return pl.pallas_call(
        functools.partial(_sgu_kernel, chunks=chunks),
        grid=(bsz, t // tt),
        in_specs=[pl.BlockSpec((1, tt, 2 * gw), lambda b, i: (b, i, 0)),
                  pl.BlockSpec((1, gw), lambda b, i: (0, 0)),
                  pl.BlockSpec((1, gw), lambda b, i: (0, 0)),
                  pl.BlockSpec((SGU_GROUPS * SGU_CHUNK, SGU_CHUNK), lambda b, i: (0, 0)),
                  pl.BlockSpec((SGU_CHUNK, gw), lambda b, i: (0, 0))],
        out_specs=pl.BlockSpec((1, tt, gw), lambda b, i: (b, i, 0)),
        out_shape=jax.ShapeDtypeStruct((bsz, t, gw), F32),
        compiler_params=_cparams(("parallel", "parallel")),
        name="sgu_mixer",
    )(z_d, ln_g.reshape(1, gw), ln_b.reshape(1, gw),
      w_s.reshape(SGU_GROUPS * SGU_CHUNK, SGU_CHUNK), bias)


def _bucket_table():
    assert REL_MAX_DIST <= DSA_BLOCK + 1
    s = np.arange(DSA_BLOCK)[:, None]
    t = np.arange(DSA_BLOCK)[None, :]
    dist = np.stack([t - s, DSA_BLOCK + t - s, 2 * DSA_BLOCK + t - s])
    n = np.maximum(dist, 0)
    max_exact = REL_BUCKETS // 2
    nf = np.maximum(n, 1).astype(np.float32)
    large = max_exact + (np.log(nf / np.float32(max_exact)) / np.float32(math.log(REL_MAX_DIST / max_exact))
                         * np.float32(REL_BUCKETS - max_exact)).astype(np.int32)
    return np.where(n < max_exact, n, np.minimum(large, REL_BUCKETS - 1)).astype(np.int32)


def _bias_table_kernel(rb_ref, bucket_ref, o_ref):
    for back in range(3):
        bucket = bucket_ref[back]
        for h in range(DSA_HEADS):
            acc = jnp.zeros(bucket.shape, F32)
            for b in range(REL_BUCKETS):
                acc = jnp.where(bucket == b, rb_ref[b * DSA_HEADS + h], acc)
            o_ref[back, h] = acc


def rel_bias_tables(rel_bias):
    blk = DSA_BLOCK
    return pl.pallas_call(
        _bias_table_kernel,
        in_specs=[pl.BlockSpec(memory_space=pltpu.SMEM),
                  pl.BlockSpec((3, blk, blk), lambda: (0, 0, 0))],
        out_specs=pl.BlockSpec((3, DSA_HEADS, blk, blk), lambda: (0, 0, 0, 0)),
        out_shape=jax.ShapeDtypeStruct((3, DSA_HEADS, blk, blk), F32),
        name="rel_bias_tables",
    )(rel_bias.reshape(-1), jnp.asarray(_bucket_table()))


def _dsa_kernel(q_ref, k_ref, v_ref, qi_ref, kw_ref, qw_ref, toe_ref, o_ref,
                kpl_ref, kb16_ref, vt_ref, keys_ref, acc_ref, am_ref, lg_ref, p_ref,
                *, topk, n_blocks, idx_bits):
    blk = DSA_BLOCK
    sub = DSA_SUB
    i = pl.program_id(1)
    hd = DSA_HEADS * DSA_DH
    heads_per_half = 128 // IDX_DIM

    @pl.when(i == 0)
    def _():
        lane = lax.broadcasted_iota(jnp.int32, (blk, 128), 1)

        def build(kb, c):
            r0 = pl.multiple_of(kb * blk, blk)
            ki = jnp.where(lane < IDX_DIM, kw_ref[0, pl.ds(r0, blk), :], 0.0)
            for j in range(heads_per_half):
                kpl_ref[kb, j] = (ki if j == 0 else pltpu.roll(ki, j * IDX_DIM, axis=1)).astype(BF16)
            kb16_ref[kb] = k_ref[0, pl.ds(r0, blk), :].astype(BF16)
            vt_ref[kb] = jnp.transpose(v_ref[0, pl.ds(r0, blk), :]).astype(BF16)
            return c

        lax.fori_loop(0, n_blocks, build, 0)

    s_loc = lax.broadcasted_iota(jnp.int32, (blk, blk), 0)
    t_loc = lax.broadcasted_iota(jnp.int32, (blk, blk), 1)
    n_vis = i + 1

    qi_t = jnp.transpose(qi_ref[0]).astype(BF16)
    qi_halves = [qi_t[:128, :], qi_t[128:, :]]
    w_t = jnp.transpose(qw_ref[0])
    w_rows = [w_t[IDX_DIM + h:IDX_DIM + h + 1, :] * (IDX_HEADS ** -0.5) * (IDX_DIM ** -0.5)
              for h in range(IDX_HEADS)]
    s_sub = lax.broadcasted_iota(jnp.int32, (sub, blk), 0)
    t_sub = lax.broadcasted_iota(jnp.int32, (sub, blk), 1)

    def score_body(kb, c):
        for ci in range(blk // sub):
            rows = slice(ci * sub, (ci + 1) * sub)
            sc = jnp.zeros((sub, blk), F32)
            for half in range(2):
                for j in range(heads_per_half):
                    d = _dot(kpl_ref[kb, j, rows, :], qi_halves[half])
                    sc = sc + jnp.maximum(d, 0.0) * w_rows[half * heads_per_half + j]
            sc = jnp.where(sc == 0.0, 0.0, sc)
            bits = pltpu.bitcast(sc, jnp.int32)
            key = jnp.where(bits < 0, bits ^ jnp.int32(0x7FFFFFFF), bits)
            vis = (kb < i) | (s_sub + ci * sub <= t_sub)
            keys_ref[kb, rows, :] = jnp.where(vis, key, jnp.int32(INT_MIN))
        return c

    lax.fori_loop(0, n_vis, score_body, 0)

    def count(pred):
        def body(kb, acc):
            hit = jnp.where(pred(keys_ref[kb], kb), 1.0, 0.0)
            return acc + jnp.sum(hit.reshape(blk // 32, 32, blk), axis=0)
        acc = lax.fori_loop(0, n_vis, body, jnp.zeros((32, blk), F32))
        return jnp.sum(acc, axis=0, keepdims=True)

    kf = float(topk)
    zero = jnp.zeros((1, blk), jnp.int32)
    base = jnp.where(count(lambda key, kb: key >= 0) >= kf, zero, jnp.int32(INT_MIN))

    def bit_body(it, base):
        cand = base | lax.shift_left(jnp.int32(1), 30 - it)
        return jnp.where(count(lambda key, kb: key >= cand) >= kf, cand, base)

    thr = lax.fori_loop(0, 31, bit_body, base)

    n_gt = count(lambda key, kb: key > thr)
    n_eq = count(lambda key, kb: key == thr)
    need = kf - n_gt
    excess = jnp.where((n_eq > need) & (thr > jnp.int32(INT_MIN)), 1.0, 0.0)

    def tie_search():
        def tie_body(it, j):
            cand = j | lax.shift_left(jnp.int32(1), idx_bits - 1 - it)
            below = count(lambda key, kb: (key == thr) & (kb * blk + s_loc < cand))
            return jnp.where(below < need, cand, j)
        return lax.fori_loop(0, idx_bits, tie_body, zero)

    last = lax.cond(jnp.max(excess) > 0.0, tie_search, lambda: jnp.full((1, blk), 2 ** 30, jnp.int32))

    assert DSA_DH ** -0.5 == 0.125
    q_t = jnp.transpose(q_ref[0] * (DSA_DH ** -0.5))
    row_h = lax.broadcasted_iota(jnp.int32, (hd, 1), 0) // DSA_DH
    q_heads = [jnp.where(row_h == h, q_t, 0.0).astype(BF16) for h in range(DSA_HEADS)]
    acc_ref[...] = jnp.zeros_like(acc_ref)
    n_sub = blk // sub

    def att_body(kb, carry):
        ms, ls = carry
        back = jnp.minimum(i - kb, 2)
        for ci in range(n_sub):
            rows = slice(ci * sub, (ci + 1) * sub)
            key = keys_ref[kb, rows, :]
            sel = (((key > thr) | ((key == thr) & (kb * blk + ci * sub + s_sub <= last)))
                   & ((kb < i) | (s_sub + ci * sub <= t_sub)))
            am_ref[rows, :] = jnp.where(sel, 0.0, NEG_BIG)
        new_ms, alphas = [], []
        for h in range(DSA_HEADS):
            pm = jnp.full((8, blk), NEG_BIG, F32)
            for ci in range(n_sub):
                rows = slice(ci * sub, (ci + 1) * sub)
                lg = _dot(kb16_ref[kb, rows, :], q_heads[h]) + toe_ref[back, h, rows, :] + am_ref[rows, :]
                lg_ref[h, rows, :] = lg
                pm = jnp.maximum(pm, jnp.max(lg.reshape(sub // 8, 8, blk), axis=0))
            m_new = jnp.maximum(ms[h], jnp.max(pm, axis=0, keepdims=True))
            new_ms.append(m_new)
            alphas.append(jnp.exp(ms[h] - m_new))
        new_ls = []
        for h in range(DSA_HEADS):
            ps = jnp.zeros((8, blk), F32)
            for ci in range(n_sub):
                rows = slice(ci * sub, (ci + 1) * sub)
                p = jnp.exp(lg_ref[h, rows, :] - new_ms[h])
                ps = ps + jnp.sum(p.reshape(sub // 8, 8, blk), axis=0)
                p_ref[h, rows, :] = p.astype(BF16)
            new_ls.append(ls[h] * alphas[h] + jnp.sum(ps, axis=0, keepdims=True))
        for h in range(DSA_HEADS):
            hrows = slice(h * DSA_DH, (h + 1) * DSA_DH)
            acc_ref[hrows, :] = acc_ref[hrows, :] * alphas[h] + _dot(vt_ref[kb, hrows, :], p_ref[h])
        return tuple(new_ms), tuple(new_ls)

    init = (tuple(jnp.full((1, blk), 0.01 * NEG_BIG, F32) for _ in range(DSA_HEADS)),
            tuple(jnp.zeros((1, blk), F32) for _ in range(DSA_HEADS)))
    _, ls = lax.fori_loop(0, n_vis, att_body, init)
    for h in range(DSA_HEADS):
        rows = slice(h * DSA_DH, (h + 1) * DSA_DH)
        acc_ref[rows, :] = acc_ref[rows, :] * (1.0 / ls[h])
    o_ref[0] = jnp.transpose(acc_ref[...])


def dsa_mixer(z_c, toe):
    bsz, t, _ = z_c.shape
    blk = DSA_BLOCK
    n_blocks = t // blk
    assert t % blk == 0
    topk = min(DSA_TOPK_MAX, t // 4)
    hd = DSA_HEADS * DSA_DH
    kernel = functools.partial(_dsa_kernel, topk=topk, n_blocks=n_blocks,
                               idx_bits=max(1, (t - 1).bit_length()))
    return pl.pallas_call(
        kernel,
        grid=(bsz, n_blocks),
        in_specs=[pl.BlockSpec((1, blk, hd), lambda b, i: (b, i, 0)),
                  pl.BlockSpec((1, t, hd), lambda b, i: (b, 0, 1)),
                  pl.BlockSpec((1, t, hd), lambda b, i: (b, 0, 2)),
                  pl.BlockSpec((1, blk, hd), lambda b, i: (b, i, 3)),
                  pl.BlockSpec((1, t, 128), lambda b, i: (b, 0, 8)),
                  pl.BlockSpec((1, blk, 128), lambda b, i: (b, i, 8)),
                  pl.BlockSpec((3, DSA_HEADS, blk, blk), lambda b, i: (0, 0, 0, 0))],
        out_specs=pl.BlockSpec((1, blk, hd), lambda b, i: (b, i, 0)),
        out_shape=jax.ShapeDtypeStruct((bsz, t, hd), F32),
        scratch_shapes=[pltpu.VMEM((n_blocks, 128 // IDX_DIM, blk, 128), BF16),
                        pltpu.VMEM((n_blocks, blk, hd), BF16),
                        pltpu.VMEM((n_blocks, hd, blk), BF16),
                        pltpu.VMEM((n_blocks, blk, blk), jnp.int32),
                        pltpu.VMEM((hd, blk), F32),
                        pltpu.VMEM((blk, blk), F32),
                        pltpu.VMEM((DSA_HEADS, blk, blk), F32),
                        pltpu.VMEM((DSA_HEADS, blk, blk), BF16)],
        compiler_params=_cparams(("parallel", "arbitrary")),
        name="dsa_mixer",
    )(z_c, z_c, z_c, z_c, z_c, z_c, toe)


def _out_proj_kernel(h_ref, a_ref, b_ref, c_ref, d_ref, w_ref, o_ref):
    gw = GROUP_WIDTH
    acc = h_ref[...]
    for gi, r in enumerate((a_ref, b_ref, c_ref, d_ref)):
        acc = acc + _dot(r[...].astype(BF16), w_ref[gi * gw:(gi + 1) * gw, :])
    o_ref[...] = acc


def out_proj(h, outs, w, tm=512):
    m, d = h.shape
    tm = min(tm, m)
    gw = GROUP_WIDTH
    return pl.pallas_call(
        _out_proj_kernel,
        grid=(m // tm,),
        in_specs=[pl.BlockSpec((tm, d), lambda i: (i, 0))]
                 + [pl.BlockSpec((tm, gw), lambda i: (i, 0))] * 4
                 + [pl.BlockSpec((4 * gw, d), lambda i: (0, 0))],
        out_specs=pl.BlockSpec((tm, d), lambda i: (i, 0)),
        out_shape=jax.ShapeDtypeStruct((m, d), F32),
        compiler_params=_cparams(("parallel",)),
        name="out_proj",
    )(h, *outs, w)


def _xattn_kernel(h_ref, g_ref, wq_ref, k_ref, v_ref, wo_ref, o_ref):
    x = h_ref[0]
    hd = XA_HEADS * XA_DH
    q = _dot(_rms(x, g_ref[...]).astype(BF16), wq_ref[...]).astype(BF16)
    k = k_ref[0]
    v = v_ref[0]
    lane_h = lax.broadcasted_iota(jnp.int32, (1, hd), 1) // XA_DH
    o = jnp.zeros((x.shape[0], hd), F32)
    for h in range(XA_HEADS):
        s = _dot_nt(q, jnp.where(lane_h == h, k, 0.0).astype(BF16)) * (XA_DH ** -0.5)
        p = jnp.exp(s - jnp.max(s, axis=-1, keepdims=True))
        p = p / jnp.sum(p, axis=-1, keepdims=True)
        o = o + _dot(p.astype(BF16), jnp.where(lane_h == h, v, 0.0).astype(BF16))
    o_ref[0] = x + _dot(o.astype(BF16), wo_ref[...])


def cross_attn(h3, g, wq, k, v, wo, tm=512):
    bsz, t, d = h3.shape
    tm = min(tm, t)
    mlen = k.shape[1]
    hd = XA_HEADS * XA_DH
    return pl.pallas_call(
        _xattn_kernel,
        grid=(bsz, t // tm),
        in_specs=[pl.BlockSpec((1, tm, d), lambda b, i: (b, i, 0)),
                  pl.BlockSpec((1, d), lambda b, i: (0, 0)),
                  pl.BlockSpec((d, hd), lambda b, i: (0, 0)),
                  pl.BlockSpec((1, mlen, hd), lambda b, i: (b, 0, 0)),
                  pl.BlockSpec((1, mlen, hd), lambda b, i: (b, 0, 0)),
                  pl.BlockSpec((hd, d), lambda b, i: (0, 0))],
        out_specs=pl.BlockSpec((1, tm, d), lambda b, i: (b, i, 0)),
        out_shape=jax.ShapeDtypeStruct((bsz, t, d), F32),
        compiler_params=_cparams(("parallel", "parallel")),
        name="cross_attn",
    )(h3, g.reshape(1, d), wq, k, v, wo)


def _ffn_kernel(h_ref, g_ref, w1_ref, w3_ref, w2_ref, o_ref, hn_ref, acc_ref):
    f = pl.program_id(1)

    @pl.when(f == 0)
    def _():
        hn_ref[...] = _rms(h_ref[...], g_ref[...]).astype(BF16)
        acc_ref[...] = jnp.zeros_like(acc_ref)

    a = _dot(hn_ref[...], w1_ref[...])
    b = _dot(hn_ref[...], w3_ref[...])
    acc_ref[...] += _dot((a * jax.nn.sigmoid(a) * b).astype(BF16), w2_ref[...])

    @pl.when(f == pl.num_programs(1) - 1)
    def _():
        o_ref[...] = h_ref[...] + acc_ref[...]


def ffn(h, g, w1, w3, w2, tm=512, tf=1408):
    m, d = h.shape
    tm = min(tm, m)
    nf = w1.shape[1]
    assert nf % tf == 0
    return pl.pallas_call(
        _ffn_kernel,
        grid=(m // tm, nf // tf),
        in_specs=[pl.BlockSpec((tm, d), lambda i, f: (i, 0)),
                  pl.BlockSpec((1, d), lambda i, f: (0, 0)),
                  pl.BlockSpec((d, tf), lambda i, f: (0, f)),
                  pl.BlockSpec((d, tf), lambda i, f: (0, f)),
                  pl.BlockSpec((tf, d), lambda i, f: (f, 0))],
        out_specs=pl.BlockSpec((tm, d), lambda i, f: (i, 0)),
        out_shape=jax.ShapeDtypeStruct((m, d), F32),
        scratch_shapes=[pltpu.VMEM((tm, d), BF16), pltpu.VMEM((tm, d), F32)],
        compiler_params=_cparams(("parallel", "arbitrary")),
        name="ffn",
    )(h, g.reshape(1, d), w1, w3, w2)


GATE_LANES = 128
MOE_CHUNK = 256
MOE_ROWS = 128
MOE_TOKENS = 2048


def _router_kernel(h_ref, g_ref, wr_ref, hn_ref, gate_ref, cnt_ref):
    hn = _rms(h_ref[...], g_ref[...])
    hn_ref[...] = hn.astype(BF16)
    tm = hn.shape[0]
    lane = lax.broadcasted_iota(jnp.int32, (tm, GATE_LANES), 1)
    logits = jnp.where(lane < N_EXPERTS, _dot_f32(hn, wr_ref[...]), -jnp.inf)
    m1 = jnp.max(logits, axis=-1, keepdims=True)
    i1 = jnp.min(jnp.where(logits == m1, lane, GATE_LANES), axis=-1, keepdims=True)
    rest = jnp.where(lane == i1, -jnp.inf, logits)
    m2 = jnp.max(rest, axis=-1, keepdims=True)
    i2 = jnp.min(jnp.where(rest == m2, lane, GATE_LANES), axis=-1, keepdims=True)
    e2 = jnp.exp(m2 - m1)
    g1 = 1.0 / (1.0 + e2)
    gates = jnp.where(lane == i1, g1, 0.0) + jnp.where(lane == i2, e2 * g1, 0.0)
    gates = jnp.where(lane == N_EXPERTS, i1.astype(F32), gates)
    gate_ref[...] = jnp.where(lane == N_EXPERTS + 1, i2.astype(F32), gates)
    sel = jnp.where((lane == i1) | (lane == i2), 1.0, 0.0)
    for c in range(tm // MOE_CHUNK):
        cnt_ref[c] = jnp.sum(sel[c * MOE_CHUNK:(c + 1) * MOE_CHUNK, :], axis=0, keepdims=True)


def route_tokens(h, g, wr, tm=512):
    m, d = h.shape
    tm = min(tm, m)
    assert tm % MOE_CHUNK == 0
    wrp = jnp.zeros((d, GATE_LANES), F32).at[:, :N_EXPERTS].set(wr)
    return pl.pallas_call(
        _router_kernel,
        grid=(m // tm,),
        in_specs=[pl.BlockSpec((tm, d), lambda i: (i, 0)),
                  pl.BlockSpec((1, d), lambda i: (0, 0)),
                  pl.BlockSpec((d, GATE_LANES), lambda i: (0, 0))],
        out_specs=[pl.BlockSpec((tm, d), lambda i: (i, 0)),
                   pl.BlockSpec((tm, GATE_LANES), lambda i: (i, 0)),
                   pl.BlockSpec((tm // MOE_CHUNK, 1, GATE_LANES), lambda i: (i, 0, 0))],
        out_shape=[jax.ShapeDtypeStruct((m, d), BF16), jax.ShapeDtypeStruct((m, GATE_LANES), F32),
                   jax.ShapeDtypeStruct((m // MOE_CHUNK, 1, GATE_LANES), F32)],
        compiler_params=_cparams(("parallel",)),
        name="router",
    )(h, g.reshape(1, d), wrp)


def _moe_kernel(cb_ref, hn_ref, gate_ref, w1_ref, w3_ref, w2_ref, y_ref,
                rank_col, rank_row, xs_ref, yacc_ref, rc_ref, gc_ref, xg_ref):
    t = pl.program_id(0)
    e = pl.program_id(1)
    f = pl.program_id(2)
    ts, d = hn_ref.shape
    ch, rb = MOE_CHUNK, MOE_ROWS
    n_ch = ts // ch
    lane = lax.broadcasted_iota(jnp.int32, (1, GATE_LANES), 1)

    def before(c):
        return cb_ref[(t * (n_ch + 1) + c) * N_EXPERTS + e]

    n_blocks = (before(n_ch) + rb - 1) // rb

    @pl.when((e == 0) & (f == 0))
    def _():
        y_ref[...] = jnp.zeros_like(y_ref)
        strict_lower = (lax.broadcasted_iota(jnp.int32, (ch, ch), 1)
                        < lax.broadcasted_iota(jnp.int32, (ch, ch), 0)).astype(BF16)
        lane_f = lane.astype(F32)
        offs = jnp.zeros((1, GATE_LANES), F32)
        for c in range(n_ch):
            rows = slice(c * ch, (c + 1) * ch)
            g = gate_ref[rows, :]
            sel = jnp.where((lane_f == g[:, N_EXPERTS:N_EXPERTS + 1])
                            | (lane_f == g[:, N_EXPERTS + 1:N_EXPERTS + 2]), 1.0, 0.0)
            r = _dot(strict_lower, sel.astype(BF16)) + offs
            r = jnp.where(sel > 0.0, r, -1.0)
            rank_col[rows, :] = r
            rank_row[:, rows] = jnp.transpose(r)
            offs = offs + jnp.sum(sel, axis=0, keepdims=True)

    @pl.when(f == 0)
    def _():
        for c in range(n_ch):
            rows = slice(c * ch, (c + 1) * ch)
            rcol = jnp.sum(jnp.where(lane == e, rank_col[rows, :], 0.0), axis=1, keepdims=True)
            rc_ref[rows, :] = jnp.broadcast_to(rcol, (ch, rb))
            gcol = jnp.sum(jnp.where(lane == e, gate_ref[rows, :], 0.0), axis=1, keepdims=True)
            gc_ref[rows, :] = jnp.broadcast_to(gcol, (ch, GATE_LANES))

        def gather_block(b, carry):
            r0 = pl.multiple_of(b * rb, rb)
            want = (r0 + lax.broadcasted_iota(jnp.int32, (rb, ch), 0)).astype(F32)
            xg_ref[...] = jnp.zeros_like(xg_ref)
            for c in range(n_ch):
                @pl.when((before(c) < r0 + rb) & (before(c + 1) > r0))
                def _(c=c):
                    ranks = rank_row[pl.ds(e, 1), c * ch:(c + 1) * ch]
                    onehot = jnp.where(ranks == want, 1.0, 0.0).astype(BF16)
                    xg_ref[...] += _dot(onehot, hn_ref[c * ch:(c + 1) * ch, :])
            xs_ref[pl.ds(r0, rb), :] = xg_ref[...].astype(BF16)
            yacc_ref[pl.ds(r0, rb), :] = jnp.zeros((rb, d), F32)
            return carry

        lax.fori_loop(0, n_blocks, gather_block, 0)

    def ffn_rows(r0, rows):
        x = xs_ref[pl.ds(r0, rows), :]
        a = _dot(x, w1_ref[0])
        g3 = _dot(x, w3_ref[0])
        yacc_ref[pl.ds(r0, rows), :] += _dot((a * jax.nn.sigmoid(a) * g3).astype(BF16), w2_ref[0])

    n_quads = n_blocks // 4

    def ffn_quad(j, carry):
        ffn_rows(pl.multiple_of(j * (4 * rb), 4 * rb), 4 * rb)
        return carry

    lax.fori_loop(0, n_quads, ffn_quad, 0)
    tail = pl.multiple_of(n_quads * (4 * rb), 4 * rb)

    @pl.when((n_blocks & 2) != 0)
    def _():
        ffn_rows(tail, 2 * rb)

    @pl.when((n_blocks & 1) != 0)
    def _():
        ffn_rows(pl.multiple_of(tail + (n_blocks & 2) * rb, rb), rb)

    @pl.when(f == pl.num_programs(2) - 1)
    def _():
        def scatter_block(b, carry):
            r0 = pl.multiple_of(b * rb, rb)
            yv = yacc_ref[pl.ds(r0, rb), :]
            y_hi = yv.astype(BF16)
            y_lo = (yv - y_hi.astype(F32)).astype(BF16)
            y_split = jnp.concatenate([y_hi, y_lo], axis=0)
            want = (r0 + lax.broadcasted_iota(jnp.int32, (ch, rb), 1)).astype(F32)
            for c in range(n_ch):
                @pl.when((before(c) < r0 + rb) & (before(c + 1) > r0))
                def _(c=c):
                    rows = slice(c * ch, (c + 1) * ch)
                    onehot = jnp.where(rc_ref[rows, :] == want, 1.0, 0.0).astype(BF16)
                    part = _dot(jnp.concatenate([onehot, onehot], axis=1), y_split)
                    gate = jnp.concatenate([gc_ref[rows, :]] * (d // GATE_LANES), axis=1)
                    y_ref[rows, :] += gate * part
            return carry

        lax.fori_loop(0, n_blocks, scatter_block, 0)


def moe(hn, gates, cnt, w1, w3, w2, tf=512):
    m, d = hn.shape
    ts = min(MOE_TOKENS, m)
    n_exp, _, nf = w1.shape
    assert nf % tf == 0 and m % ts == 0 and ts % MOE_CHUNK == 0 and MOE_ROWS == GATE_LANES
    n_ch = ts // MOE_CHUNK
    counts = cnt.reshape(m // ts, n_ch, GATE_LANES)[:, :, :n_exp].astype(jnp.int32)
    before = jnp.concatenate([jnp.zeros((m // ts, 1, n_exp), jnp.int32), jnp.cumsum(counts, axis=1)], axis=1)
    grid_spec = pltpu.PrefetchScalarGridSpec(
        num_scalar_prefetch=1,
        grid=(m // ts, n_exp, nf // tf),
        in_specs=[pl.BlockSpec((ts, d), lambda t, e, f, cb: (t, 0)),
                  pl.BlockSpec((ts, GATE_LANES), lambda t, e, f, cb: (t, 0)),
                  pl.BlockSpec((1, d, tf), lambda t, e, f, cb: (e, 0, f)),
                  pl.BlockSpec((1, d, tf), lambda t, e, f, cb: (e, 0, f)),
                  pl.BlockSpec((1, tf, d), lambda t, e, f, cb: (e, f, 0))],
        out_specs=pl.BlockSpec((ts, d), lambda t, e, f, cb: (t, 0)),
        scratch_shapes=[pltpu.VMEM((ts, GATE_LANES), F32),
                        pltpu.VMEM((GATE_LANES, ts), F32),
                        pltpu.VMEM((ts, d), BF16),
                        pltpu.VMEM((ts, d), F32),
                        pltpu.VMEM((ts, MOE_ROWS), F32),
                        pltpu.VMEM((ts, GATE_LANES), F32),
                        pltpu.VMEM((MOE_ROWS, d), F32)])
    return pl.pallas_call(
        _moe_kernel,
        grid_spec=grid_spec,
        out_shape=jax.ShapeDtypeStruct((m, d), F32),
        compiler_params=_cparams(("parallel", "arbitrary", "arbitrary")),
        name="moe",
    )(before.reshape(-1), hn, gates, w1, w3, w2)


def moe_block(h, g, wr, w1, w3, w2):
    hn, gates, cnt = route_tokens(h, g, wr)
    return moe(hn, gates, cnt, w1.astype(BF16), w3.astype(BF16), w2.astype(BF16))


def _final_norm_kernel(g_ref, *refs):
    *x_refs, o_ref = refs
    x = x_refs[0][...]
    for r in x_refs[1:]:
        x = x + r[...]
    o_ref[...] = _rms(x, g_ref[...])


def final_rmsnorm(g, *xs, tm=1024):
    m, d = xs[0].shape
    tm = min(tm, m)
    return pl.pallas_call(
        _final_norm_kernel,
        grid=(m // tm,),
        in_specs=[pl.BlockSpec((1, d), lambda i: (0, 0))] + [pl.BlockSpec((tm, d), lambda i: (i, 0))] * len(xs),
        out_specs=pl.BlockSpec((tm, d), lambda i: (i, 0)),
        out_shape=jax.ShapeDtypeStruct((m, d), F32),
        compiler_params=_cparams(("parallel",)),
        name="final_norm",
    )(g.reshape(1, d), *xs)


def _pad_cols(w, width):
    return jnp.pad(w, ((0, 0), (0, width - w.shape[1])))


def _in_proj_weight(w_in):
    a_end = 2 * GLA_HEADS * GLA_DK + 2 * GROUP_WIDTH + GLA_RANK
    b_end = a_end + GROUP_WIDTH
    c_end = b_end + 3 * GROUP_WIDTH + IDX_HEADS * IDX_DIM + IDX_DIM + IDX_HEADS
    assert w_in.shape[1] == c_end + 2 * GROUP_WIDTH
    return jnp.concatenate([_pad_cols(w_in[:, :a_end], A_WIDTH), w_in[:, a_end:b_end],
                            _pad_cols(w_in[:, b_end:c_end], C_WIDTH), w_in[:, c_end:]], axis=1).astype(BF16)


def hybrid_layer(h, mem2, toe, p, bsz, t):
    m, d = h.shape
    z_a, z_b, z_c, z_d = norm_matmul(h, p["norm_mix"], _in_proj_weight(p["w_in"]),
                                     (A_WIDTH, B_WIDTH, C_WIDTH, D_WIDTH))
    o_a = gla_mixer(z_a.reshape(bsz, t, A_WIDTH), p["gla_wa2"], p["gla_ba"], p["gla_norm"])
    o_b = pool_mixer(z_b.reshape(bsz, t, B_WIDTH), p["pool_w"], p["pool_b"].reshape(-1), p["pool_scale"])
    o_c = dsa_mixer(z_c.reshape(bsz, t, C_WIDTH), toe)
    o_d = sgu_mixer(z_d.reshape(bsz, t, D_WIDTH), p["sgu_ln_g"], p["sgu_ln_b"], p["sgu_w"], p["sgu_b"])
    h = out_proj(h, [o.reshape(m, GROUP_WIDTH) for o in (o_a, o_b, o_c, o_d)], p["w_out"].astype(BF16))

    hd = XA_HEADS * XA_DH
    k, v = norm_matmul(mem2, p["norm_mem"], p["xa_wkv"].astype(BF16), (hd, hd))
    mlen = mem2.shape[0] // bsz
    h = cross_attn(h.reshape(bsz, t, d), p["norm_xa"], p["xa_wq"].astype(BF16),
                   k.reshape(bsz, mlen, hd), v.reshape(bsz, mlen, hd), p["xa_wo"].astype(BF16))
    return h.reshape(m, d)


def kernel(x, mem, rel_bias, final_norm, norm_mix, w_in, gla_wa2, gla_ba, gla_norm, pool_w, pool_b,
           pool_scale, sgu_ln_g, sgu_ln_b, sgu_w, sgu_b, w_out, norm_xa, norm_mem, xa_wq, xa_wkv, xa_wo,
           norm_ffn, ffn_w1, ffn_w3, ffn_w2, router, moe_w1, moe_w3, moe_w2):
    bsz, t, d = x.shape
    depth = norm_mix.shape[0]
    h = x.reshape(bsz * t, d)
    mem2 = mem.reshape(-1, d)
    toe = rel_bias_tables(rel_bias)
    pending = None
    for i in range(depth):
        if pending is not None:
            h, pending = h + pending, None
        p = dict(norm_mix=norm_mix[i], w_in=w_in[i], gla_wa2=gla_wa2[i], gla_ba=gla_ba[i],
                 gla_norm=gla_norm[i], pool_w=pool_w[i], pool_b=pool_b[i], pool_scale=pool_scale[i],
                 sgu_ln_g=sgu_ln_g[i], sgu_ln_b=sgu_ln_b[i], sgu_w=sgu_w[i], sgu_b=sgu_b[i],
                 w_out=w_out[i], norm_xa=norm_xa[i], norm_mem=norm_mem[i], xa_wq=xa_wq[i],
                 xa_wkv=xa_wkv[i], xa_wo=xa_wo[i])
        h = hybrid_layer(h, mem2, toe, p, bsz, t)
        j = i // 2
        if i % 2 == 0:
            h = ffn(h, norm_ffn[i], ffn_w1[j].astype(BF16), ffn_w3[j].astype(BF16), ffn_w2[j].astype(BF16))
        else:
            pending = moe_block(h, norm_ffn[i], router[j], moe_w1[j], moe_w3[j], moe_w2[j])
    xs = (h,) if pending is None else (h, pending)
    return final_rmsnorm(final_norm, *xs).reshape(bsz, t, d)
```

```python
import functools
import math

import jax
import jax.numpy as jnp
import numpy as np
from jax import lax
from jax.experimental import pallas as pl
from jax.experimental.pallas import tpu as pltpu

F32 = jnp.float32
BF16 = jnp.bfloat16
EPS = 1e-6

GROUP_WIDTH = 256

GLA_HEADS = 4
GLA_DV = 64
GLA_DK = 32
GLA_RANK = 16
GLA_TAU = 16.0
GLA_CHUNK = 64
GLA_GROUP = 8

POOL_WINDOWS = (2, 4, 8, 16)
POOL_CG = 64

DSA_HEADS = 4
DSA_DH = 64
IDX_HEADS = 8
IDX_DIM = 32
DSA_TOPK_MAX = 256
DSA_BLOCK = 256
DSA_SUB = 64

SGU_GROUPS = 4
SGU_CHUNK = 128
SGU_CG = 64

REL_BUCKETS = 32
REL_MAX_DIST = 128

XA_HEADS = 4
XA_DH = 64

N_EXPERTS = 8

A_WIDTH = 896
B_WIDTH = 256
C_WIDTH = 1152
D_WIDTH = 512

INT_MIN = -(2 ** 31)
NEG_BIG = -1e30
VMEM_LIMIT = 56 * 1024 * 1024


def _cparams(sem):
    return pltpu.CompilerParams(dimension_semantics=sem, vmem_limit_bytes=VMEM_LIMIT)


def _dot(a, b):
    return jnp.dot(a, b, preferred_element_type=F32)


def _dot_nt(a, b):
    return lax.dot_general(a, b, (((1,), (1,)), ((), ())), preferred_element_type=F32)


def _dot_tn(a, b):
    return lax.dot_general(a, b, (((0,), (0,)), ((), ())), preferred_element_type=F32)


def _dot_f32(a, b):
    return jnp.dot(a, b, preferred_element_type=F32, precision=lax.Precision.HIGHEST)


def _split_bf16(x, terms):
    parts = []
    for _ in range(terms):
        p = x.astype(BF16)
        parts.append(p)
        x = x - p.astype(F32)
    return parts


def _dot_exact_rhs(a, b, terms):
    out = None
    for p in _split_bf16(a, terms):
        d = _dot(p, b)
        out = d if out is None else out + d
    return out


def _dot_exact_lhs(a, b, terms):
    out = None
    for p in _split_bf16(b, terms):
        d = _dot(a, p)
        out = d if out is None else out + d
    return out


def _rms(x, g):
    return x * lax.rsqrt(jnp.mean(x * x, axis=-1, keepdims=True) + EPS) * g


def _norm_matmul_kernel(h_ref, g_ref, w_ref, *out_refs, widths):
    hb = _rms(h_ref[...], g_ref[...]).astype(BF16)
    off = 0
    for o_ref, wd in zip(out_refs, widths):
        o_ref[...] = _dot(hb, w_ref[:, off:off + wd]).astype(o_ref.dtype)
        off += wd


def norm_matmul(h, g, w, widths, tm=512):
    m, d = h.shape
    tm = min(tm, m)
    n = sum(widths)
    return pl.pallas_call(
        functools.partial(_norm_matmul_kernel, widths=widths),
        grid=(m // tm,),
        in_specs=[pl.BlockSpec((tm, d), lambda i: (i, 0)),
                  pl.BlockSpec((1, d), lambda i: (0, 0)),
                  pl.BlockSpec((d, n), lambda i: (0, 0))],
        out_specs=[pl.BlockSpec((tm, wd), lambda i: (i, 0)) for wd in widths],
        out_shape=[jax.ShapeDtypeStruct((m, wd), F32) for wd in widths],
        compiler_params=_cparams(("parallel",)),
        name="norm_matmul",
    )(h, g.reshape(1, d), w)


def _log_sigmoid(x):
    return jnp.minimum(x, 0.0) - jnp.log1p(jnp.exp(-jnp.abs(x)))


def _gla_kernel(z_ref, wa2_ref, ba_ref, ng_ref, o_ref, s_ref, *, n_groups):
    c = GLA_CHUNK
    grp = GLA_GROUP * c
    hk = GLA_HEADS * GLA_DK
    hv = GLA_HEADS * GLA_DV
    s_ref[...] = jnp.zeros_like(s_ref)

    head_k = lax.broadcasted_iota(jnp.int32, (1, hk), 1) // GLA_DK
    head_v = lax.broadcasted_iota(jnp.int32, (1, hv), 1) // GLA_DV
    g_row = lax.broadcasted_iota(jnp.int32, (grp, grp), 0)
    g_col = lax.broadcasted_iota(jnp.int32, (grp, grp), 1)
    tril = (((g_row // c) == (g_col // c)) & (g_col <= g_row)).astype(BF16)
    causal4 = (lax.broadcasted_iota(jnp.int32, (GLA_HEADS * c, c), 1)
               <= lax.broadcasted_iota(jnp.int32, (GLA_HEADS * c, c), 0) % c)
    state_mask = (lax.broadcasted_iota(jnp.int32, (hk, hv), 0) // GLA_DK
                  == lax.broadcasted_iota(jnp.int32, (hk, hv), 1) // GLA_DV)
    norm_mat = jnp.where(lax.broadcasted_iota(jnp.int32, (hv, hv), 0) // GLA_DV
                         == lax.broadcasted_iota(jnp.int32, (hv, hv), 1) // GLA_DV,
                         1.0 / GLA_DV, 0.0).astype(BF16)
    wa2 = wa2_ref[...]
    ba = ba_ref[...]
    ng = ng_ref[...]

    def body(n, carry):
        r0 = pl.multiple_of(n * grp, grp)
        z = z_ref[0, pl.ds(r0, grp), :]
        q, k, v, g, lr = z[:, 0:128], z[:, 128:256], z[:, 256:512], z[:, 512:768], z[:, 768:896]
        log_a = _log_sigmoid(_dot_f32(lr, wa2) + ba) / GLA_TAU
        b = _dot_exact_lhs(tril, log_a, 3)
        b_end = jnp.concatenate([jnp.broadcast_to(b[(ci + 1) * c - 1:(ci + 1) * c, :], (c, hk))
                                 for ci in range(GLA_GROUP)], axis=0)
        q_t = q * (GLA_DK ** -0.5) * jnp.exp(b)
        q_tb = q_t.astype(BF16)
        k_t = (k * jnp.exp(-b)).astype(BF16)
        k_dec = (k * jnp.exp(b_end - b)).astype(BF16)
        vb = v.astype(BF16)
        outs = []
        for ci in range(GLA_GROUP):
            rows = slice(ci * c, (ci + 1) * c)
            q4 = jnp.concatenate([jnp.where(head_k == h, q_t[rows], 0.0) for h in range(GLA_HEADS)],
                                 axis=0).astype(BF16)
            att = jnp.where(causal4, _dot_nt(q4, k_t[rows]), 0.0)
            r = _dot(att.astype(BF16), vb[rows])
            o = _dot(q_tb[rows], s_ref[...].astype(BF16))
            for h in range(GLA_HEADS):
                o = o + jnp.where(head_v == h, r[h * c:(h + 1) * c, :], 0.0)
            outs.append(o)
            kv = jnp.where(state_mask, _dot_tn(k_dec[rows], vb[rows]), 0.0)
            last = b_end[ci * c:ci * c + 1, :]
            dec = jnp.exp(jnp.transpose(jnp.broadcast_to(last, (hk, hk))))
            s_ref[...] = s_ref[...] * jnp.concatenate([dec, dec], axis=1) + kv
        o = jnp.concatenate(outs, axis=0)
        o = o * lax.rsqrt(_dot_exact_rhs(o * o, norm_mat, 2) + EPS) * ng
        o_ref[0, pl.ds(r0, grp), :] = o * (g * jax.nn.sigmoid(g))
        return carry

    lax.fori_loop(0, n_groups, body, 0)


def gla_mixer(z_a, wa2, ba, norm_g):
    bsz, t, _ = z_a.shape
    hk = GLA_HEADS * GLA_DK
    wa2p = jnp.zeros((128, hk), F32).at[:GLA_RANK].set(wa2)
    return pl.pallas_call(
        functools.partial(_gla_kernel, n_groups=t // (GLA_GROUP * GLA_CHUNK)),
        grid=(bsz,),
        in_specs=[pl.BlockSpec((1, t, A_WIDTH), lambda b: (b, 0, 0)),
                  pl.BlockSpec((128, hk), lambda b: (0, 0)),
                  pl.BlockSpec((1, hk), lambda b: (0, 0)),
                  pl.BlockSpec((1, GROUP_WIDTH), lambda b: (0, 0))],
        out_specs=pl.BlockSpec((1, t, GROUP_WIDTH), lambda b: (b, 0, 0)),
        out_shape=jax.ShapeDtypeStruct((bsz, t, GROUP_WIDTH), F32),
        scratch_shapes=[pltpu.VMEM((hk, GROUP_WIDTH), F32)],
        compiler_params=_cparams(("parallel",)),
        name="gla_mixer",
    )(z_a, wa2p, ba.reshape(1, hk), norm_g.reshape(1, GROUP_WIDTH))


def _pool_kernel(u_ref, w_ref, b_ref, sc_ref, o_ref):
    u = u_ref[0]
    t, gw = u.shape
    row = lax.broadcasted_iota(jnp.int32, (t, gw), 0)
    grp = lax.broadcasted_iota(jnp.int32, (t, gw), 1) // POOL_CG

    def shifted(x, k):
        return jnp.where(row >= k, pltpu.roll(x, k, axis=0), 0.0)

    s = u
    p = jnp.zeros_like(u)
    for gi, win in enumerate(POOL_WINDOWS):
        half = win // 2
        s = s + shifted(s, half)
        cnt = jnp.minimum(row + 1, win).astype(F32)
        p = jnp.where(grp == gi, s / cnt - u, p)
    y = _dot(p.astype(BF16), w_ref[...]) + b_ref[...]
    o_ref[0] = y * sc_ref[...]


def pool_mixer(z_b, w, b, scale):
    assert POOL_WINDOWS == (2, 4, 8, 16)
    bsz, t, gw = z_b.shape
    w_bd = jnp.zeros((gw, gw), F32)
    for gi in range(len(POOL_WINDOWS)):
        w_bd = w_bd.at[gi * POOL_CG:(gi + 1) * POOL_CG, gi * POOL_CG:(gi + 1) * POOL_CG].set(w[gi])
    return pl.pallas_call(
        _pool_kernel,
        grid=(bsz,),
        in_specs=[pl.BlockSpec((1, t, gw), lambda i: (i, 0, 0)),
                  pl.BlockSpec((gw, gw), lambda i: (0, 0)),
                  pl.BlockSpec((1, gw), lambda i: (0, 0)),
                  pl.BlockSpec((1, gw), lambda i: (0, 0))],
        out_specs=pl.BlockSpec((1, t, gw), lambda i: (i, 0, 0)),
        out_shape=jax.ShapeDtypeStruct((bsz, t, gw), F32),
        compiler_params=_cparams(("parallel",)),
        name="pool_mixer",
    )(z_b, w_bd.astype(BF16), b.reshape(1, gw), scale.reshape(1, gw))


def _sgu_kernel(z_ref, lg_ref, lb_ref, w_ref, bm_ref, o_ref, *, chunks):
    c = SGU_CHUNK
    gw = GROUP_WIDTH
    rows = SGU_GROUPS * c
    tri = (lax.broadcasted_iota(jnp.int32, (rows, c), 1)
           <= lax.broadcasted_iota(jnp.int32, (rows, c), 0) % c)
    ws = jnp.where(tri, w_ref[...], 0.0).astype(BF16)
    grp = lax.broadcasted_iota(jnp.int32, (1, gw), 1) // SGU_CG
    for ci in range(chunks):
        z = jax.nn.gelu(z_ref[0, ci * c:(ci + 1) * c, :], approximate=True)
        u, v = z[:, :gw], z[:, gw:]
        mu = jnp.mean(v, axis=-1, keepdims=True)
        var = jnp.mean(jnp.square(v - mu), axis=-1, keepdims=True)
        vn = (v - mu) * lax.rsqrt(var + EPS) * lg_ref[...] + lb_ref[...]
        r = _dot(ws, vn.astype(BF16))
        mixed = bm_ref[...]
        for g in range(SGU_GROUPS):
            mixed = mixed + jnp.where(grp == g, r[g * c:(g + 1) * c, :], 0.0)
        o_ref[0, ci * c:(ci + 1) * c, :] = u * mixed


def sgu_mixer(z_d, ln_g, ln_b, w_s, b_s, chunks=4):
    bsz, t, _ = z_d.shape
    gw = GROUP_WIDTH
    tt = chunks * SGU_CHUNK
    bias = jnp.repeat(b_s.T, SGU_CG, axis=1)
    return pl.pallas_call(
        functools.partial(_sgu_kernel, chunks=chunks),
        grid=(bsz, t // tt),
        in_specs=[pl.BlockSpec((1, tt, 2 * gw), lambda b, i: (b, i, 0)),
                  pl.BlockSpec((1, gw), lambda b, i: (0, 0)),
                  pl.BlockSpec((1, gw), lambda b, i: (0, 0)),
                  pl.BlockSpec((SGU_GROUPS * SGU_CHUNK, SGU_CHUNK), lambda b, i: (0, 0)),
                  pl.BlockSpec((SGU_CHUNK, gw), lambda b, i: (0, 0))],
        out_specs=pl.BlockSpec((1, tt, gw), lambda b, i: (b, i, 0)),
        out_shape=jax.ShapeDtypeStruct((bsz, t, gw), F32),
        compiler_params=_cparams(("parallel", "parallel")),
        name="sgu_mixer",
    )(z_d, ln_g.reshape(1, gw), ln_b.reshape(1, gw),
      w_s.reshape(SGU_GROUPS * SGU_CHUNK, SGU_CHUNK), bias)


def _bucket_table():
    assert REL_MAX_DIST <= DSA_BLOCK + 1
    s = np.arange(DSA_BLOCK)[:, None]
    t = np.arange(DSA_BLOCK)[None, :]
    dist = np.stack([t - s, DSA_BLOCK + t - s, 2 * DSA_BLOCK + t - s])
    n = np.maximum(dist, 0)
    max_exact = REL_BUCKETS // 2
    nf = np.maximum(n, 1).astype(np.float32)
    large = max_exact + (np.log(nf / np.float32(max_exact)) / np.float32(math.log(REL_MAX_DIST / max_exact))
                         * np.float32(REL_BUCKETS - max_exact)).astype(np.int32)
    return np.where(n < max_exact, n, np.minimum(large, REL_BUCKETS - 1)).astype(np.int32)


def _bias_table_kernel(rb_ref, bucket_ref, o_ref):
    for back in range(3):
        bucket = bucket_ref[back]
        for h in range(DSA_HEADS):
            acc = jnp.zeros(bucket.shape, F32)
            for b in range(REL_BUCKETS):
                acc = jnp.where(bucket == b, rb_ref[b * DSA_HEADS + h], acc)
            o_ref[back, h] = acc


def rel_bias_tables(rel_bias):
    blk = DSA_BLOCK
    return pl.pallas_call(
        _bias_table_kernel,
        in_specs=[pl.BlockSpec(memory_space=pltpu.SMEM),
                  pl.BlockSpec((3, blk, blk), lambda: (0, 0, 0))],
        out_specs=pl.BlockSpec((3, DSA_HEADS, blk, blk), lambda: (0, 0, 0, 0)),
        out_shape=jax.ShapeDtypeStruct((3, DSA_HEADS, blk, blk), F32),
        name="rel_bias_tables",
    )(rel_bias.reshape(-1), jnp.asarray(_bucket_table()))


def _dsa_kernel(q_ref, k_ref, v_ref, qi_ref, kw_ref, qw_ref, toe_ref, o_ref,
                kpl_ref, kb16_ref, vt_ref, keys_ref, khi_ref, klo_ref, acc_ref, am_ref, lg_ref, p_ref,
                *, topk, n_blocks, idx_bits):
    blk = DSA_BLOCK
    sub = DSA_SUB
    i = pl.program_id(1)
    hd = DSA_HEADS * DSA_DH
    heads_per_half = 128 // IDX_DIM

    @pl.when(i == 0)
    def _():
        lane = lax.broadcasted_iota(jnp.int32, (blk, 128), 1)

        def build(kb, c):
            r0 = pl.multiple_of(kb * blk, blk)
            ki = jnp.where(lane < IDX_DIM, kw_ref[0, pl.ds(r0, blk), :], 0.0)
            for j in range(heads_per_half):
                kpl_ref[kb, j] = (ki if j == 0 else pltpu.roll(ki, j * IDX_DIM, axis=1)).astype(BF16)
            kb16_ref[kb] = k_ref[0, pl.ds(r0, blk), :].astype(BF16)
            vt_ref[kb] = jnp.transpose(v_ref[0, pl.ds(r0, blk), :]).astype(BF16)
            return c

        lax.fori_loop(0, n_blocks, build, 0)

    s_loc = lax.broadcasted_iota(jnp.int32, (blk, blk), 0)
    t_loc = lax.broadcasted_iota(jnp.int32, (blk, blk), 1)
    n_vis = i + 1

    qi_t = jnp.transpose(qi_ref[0]).astype(BF16)
    qi_halves = [qi_t[:128, :], qi_t[128:, :]]
    w_t = jnp.transpose(qw_ref[0])
    w_rows = [w_t[IDX_DIM + h:IDX_DIM + h + 1, :] * (IDX_HEADS ** -0.5) * (IDX_DIM ** -0.5)
              for h in range(IDX_HEADS)]
    s_sub = lax.broadcasted_iota(jnp.int32, (sub, blk), 0)
    t_sub = lax.broadcasted_iota(jnp.int32, (sub, blk), 1)

    def score_body(kb, c):
        for ci in range(blk // sub):
            rows = slice(ci * sub, (ci + 1) * sub)
            sc = jnp.zeros((sub, blk), F32)
            for half in range(2):
                for j in range(heads_per_half):
                    d = _dot(kpl_ref[kb, j, rows, :], qi_halves[half])
                    sc = sc + jnp.maximum(d, 0.0) * w_rows[half * heads_per_half + j]
            sc = jnp.where(sc == 0.0, 0.0, sc)
            bits = pltpu.bitcast(sc, jnp.int32)
            key = jnp.where(bits < 0, bits ^ jnp.int32(0x7FFFFFFF), bits)
            vis = (kb < i) | (s_sub + ci * sub <= t_sub)
            key = jnp.where(vis, key, jnp.int32(INT_MIN))
            keys_ref[kb, rows, :] = key
            khi_ref[kb, rows, :] = lax.shift_right_arithmetic(key, 16).astype(jnp.int16)
            klo_ref[kb, rows, :] = ((key & 0xFFFF) - 2 ** 15).astype(jnp.int16)
        return c

    lax.fori_loop(0, n_vis, score_body, 0)

    def count(pred):
        def body(kb, acc):
            hit = jnp.where(pred(keys_ref[kb], kb), 1.0, 0.0)
            return acc + jnp.sum(hit.reshape(blk // 32, 32, blk), axis=0)
        acc = lax.fori_loop(0, n_vis, body, jnp.zeros((32, blk), F32))
        return jnp.sum(acc, axis=0, keepdims=True)

    def count16(ref, cand):
        cand = cand.astype(jnp.int16)

        def body(kb, acc):
            hit = jnp.where(ref[kb] >= cand, jnp.int16(1), jnp.int16(0))
            for j in range(blk // 32):
                acc = acc + hit[j * 32:(j + 1) * 32]
            return acc
        acc = lax.fori_loop(0, n_vis, body, jnp.zeros((32, blk), jnp.int16))
        return jnp.sum(acc.astype(jnp.int32).astype(F32), axis=0, keepdims=True)

    def search16(ref, offset):
        lowest = jnp.full((1, blk), -(2 ** 15), jnp.int32)
        base = jnp.where(offset + count16(ref, jnp.zeros((1, blk), jnp.int32)) >= kf, 0, lowest)

        def bit_body(it, base):
            cand = base | lax.shift_left(jnp.int32(1), 14 - it)
            return jnp.where(offset + count16(ref, cand) >= kf, cand, base)
        return lax.fori_loop(0, 15, bit_body, base)

    kf = float(topk)
    zero = jnp.zeros((1, blk), jnp.int32)
    thr_hi = search16(khi_ref, 0.0)
    thr_hi16 = thr_hi.astype(jnp.int16)

    def low_body(kb, above):
        hi = khi_ref[kb]
        klo_ref[kb] = jnp.where(hi == thr_hi16, klo_ref[kb], jnp.int16(-(2 ** 15)))
        hit = jnp.where(hi > thr_hi16, jnp.int16(1), jnp.int16(0))
        for j in range(blk // 32):
            above = above + hit[j * 32:(j + 1) * 32]
        return above

    above = lax.fori_loop(0, n_vis, low_body, jnp.zeros((32, blk), jnp.int16))
    n_above = jnp.sum(above.astype(jnp.int32).astype(F32), axis=0, keepdims=True)
    thr_lo = search16(klo_ref, n_above)
    thr = lax.shift_left(thr_hi, 16) | (thr_lo + 2 ** 15)

    n_gt = count(lambda key, kb: key > thr)
    n_eq = count(lambda key, kb: key == thr)
    need = kf - n_gt
    excess = jnp.where((n_eq > need) & (thr > jnp.int32(INT_MIN)), 1.0, 0.0)

    def tie_search():
        def tie_body(it, j):
            cand = j | lax.shift_left(jnp.int32(1), idx_bits - 1 - it)
            below = count(lambda key, kb: (key == thr) & (kb * blk + s_loc < cand))
            return jnp.where(below < need, cand, j)
        return lax.fori_loop(0, idx_bits, tie_body, zero)

    last = lax.cond(jnp.max(excess) > 0.0, tie_search, lambda: jnp.full((1, blk), 2 ** 30, jnp.int32))

    assert DSA_DH ** -0.5 == 0.125
    q_t = jnp.transpose(q_ref[0] * (DSA_DH ** -0.5))
    row_h = lax.broadcasted_iota(jnp.int32, (hd, 1), 0) // DSA_DH
    q_heads = [jnp.where(row_h == h, q_t, 0.0).astype(BF16) for h in range(DSA_HEADS)]
    acc_ref[...] = jnp.zeros_like(acc_ref)
    n_sub = blk // sub

    def att_body(kb, carry):
        ms, ls = carry
        back = jnp.minimum(i - kb, 2)
        for ci in range(n_sub):
            rows = slice(ci * sub, (ci + 1) * sub)
            key = keys_ref[kb, rows, :]
            sel = (((key > thr) | ((key == thr) & (kb * blk + ci * sub + s_sub <= last)))
                   & ((kb < i) | (s_sub + ci * sub <= t_sub)))
            am_ref[rows, :] = jnp.where(sel, 0.0, NEG_BIG)
        new_ms, alphas = [], []
        for h in range(DSA_HEADS):
            pm = jnp.full((8, blk), NEG_BIG, F32)
            for ci in range(n_sub):
                rows = slice(ci * sub, (ci + 1) * sub)
                lg = _dot(kb16_ref[kb, rows, :], q_heads[h]) + toe_ref[back, h, rows, :] + am_ref[rows, :]
                lg_ref[h, rows, :] = lg
                pm = jnp.maximum(pm, jnp.max(lg.reshape(sub // 8, 8, blk), axis=0))
            m_new = jnp.maximum(ms[h], jnp.max(pm, axis=0, keepdims=True))
            new_ms.append(m_new)
            alphas.append(jnp.exp(ms[h] - m_new))
        new_ls = []
        for h in range(DSA_HEADS):
            ps = jnp.zeros((8, blk), F32)
            for ci in range(n_sub):
                rows = slice(ci * sub, (ci + 1) * sub)
                p = jnp.exp(lg_ref[h, rows, :] - new_ms[h])
                ps = ps + jnp.sum(p.reshape(sub // 8, 8, blk), axis=0)
                p_ref[h, rows, :] = p.astype(BF16)
            new_ls.append(ls[h] * alphas[h] + jnp.sum(ps, axis=0, keepdims=True))
        for h in range(DSA_HEADS):
            hrows = slice(h * DSA_DH, (h + 1) * DSA_DH)
            acc_ref[hrows, :] = acc_ref[hrows, :] * alphas[h] + _dot(vt_ref[kb, hrows, :], p_ref[h])
        return tuple(new_ms), tuple(new_ls)

    init = (tuple(jnp.full((1, blk), 0.01 * NEG_BIG, F32) for _ in range(DSA_HEADS)),
            tuple(jnp.zeros((1, blk), F32) for _ in range(DSA_HEADS)))
    _, ls = lax.fori_loop(0, n_vis, att_body, init)
    for h in range(DSA_HEADS):
        rows = slice(h * DSA_DH, (h + 1) * DSA_DH)
        acc_ref[rows, :] = acc_ref[rows, :] * (1.0 / ls[h])
    o_ref[0] = jnp.transpose(acc_ref[...])


def dsa_mixer(z_c, toe):
    bsz, t, _ = z_c.shape
    blk = DSA_BLOCK
    n_blocks = t // blk
    assert t % blk == 0
    topk = min(DSA_TOPK_MAX, t // 4)
    hd = DSA_HEADS * DSA_DH
    kernel = functools.partial(_dsa_kernel, topk=topk, n_blocks=n_blocks,
                               idx_bits=max(1, (t - 1).bit_length()))
    return pl.pallas_call(
        kernel,
        grid=(bsz, n_blocks),
        in_specs=[pl.BlockSpec((1, blk, hd), lambda b, i: (b, i, 0)),
                  pl.BlockSpec((1, t, hd), lambda b, i: (b, 0, 1)),
                  pl.BlockSpec((1, t, hd), lambda b, i: (b, 0, 2)),
                  pl.BlockSpec((1, blk, hd), lambda b, i: (b, i, 3)),
                  pl.BlockSpec((1, t, 128), lambda b, i: (b, 0, 8)),
                  pl.BlockSpec((1, blk, 128), lambda b, i: (b, i, 8)),
                  pl.BlockSpec((3, DSA_HEADS, blk, blk), lambda b, i: (0, 0, 0, 0))],
        out_specs=pl.BlockSpec((1, blk, hd), lambda b, i: (b, i, 0)),
        out_shape=jax.ShapeDtypeStruct((bsz, t, hd), F32),
        scratch_shapes=[pltpu.VMEM((n_blocks, 128 // IDX_DIM, blk, 128), BF16),
                        pltpu.VMEM((n_blocks, blk, hd), BF16),
                        pltpu.VMEM((n_blocks, hd, blk), BF16),
                        pltpu.VMEM((n_blocks, blk, blk), jnp.int32),
                        pltpu.VMEM((n_blocks, blk, blk), jnp.int16),
                        pltpu.VMEM((n_blocks, blk, blk), jnp.int16),
                        pltpu.VMEM((hd, blk), F32),
                        pltpu.VMEM((blk, blk), F32),
                        pltpu.VMEM((DSA_HEADS, blk, blk), F32),
                        pltpu.VMEM((DSA_HEADS, blk, blk), BF16)],
        compiler_params=_cparams(("parallel", "arbitrary")),
        name="dsa_mixer",
    )(z_c, z_c, z_c, z_c, z_c, z_c, toe)


def _out_proj_kernel(h_ref, a_ref, b_ref, c_ref, d_ref, w_ref, o_ref):
    gw = GROUP_WIDTH
    acc = h_ref[...]
    for gi, r in enumerate((a_ref, b_ref, c_ref, d_ref)):
        acc = acc + _dot(r[...].astype(BF16), w_ref[gi * gw:(gi + 1) * gw, :])
    o_ref[...] = acc


def out_proj(h, outs, w, tm=512):
    m, d = h.shape
    tm = min(tm, m)
    gw = GROUP_WIDTH
    return pl.pallas_call(
        _out_proj_kernel,
        grid=(m // tm,),
        in_specs=[pl.BlockSpec((tm, d), lambda i: (i, 0))]
                 + [pl.BlockSpec((tm, gw), lambda i: (i, 0))] * 4
                 + [pl.BlockSpec((4 * gw, d), lambda i: (0, 0))],
        out_specs=pl.BlockSpec((tm, d), lambda i: (i, 0)),
        out_shape=jax.ShapeDtypeStruct((m, d), F32),
        compiler_params=_cparams(("parallel",)),
        name="out_proj",
    )(h, *outs, w)


def _xattn_kernel(h_ref, g_ref, wq_ref, k_ref, v_ref, wo_ref, o_ref):
    x = h_ref[0]
    hd = XA_HEADS * XA_DH
    q = _dot(_rms(x, g_ref[...]).astype(BF16), wq_ref[...]).astype(BF16)
    k = k_ref[0]
    v = v_ref[0]
    lane_h = lax.broadcasted_iota(jnp.int32, (1, hd), 1) // XA_DH
    o = jnp.zeros((x.shape[0], hd), F32)
    for h in range(XA_HEADS):
        s = _dot_nt(q, jnp.where(lane_h == h, k, 0.0).astype(BF16)) * (XA_DH ** -0.5)
        p = jnp.exp(s - jnp.max(s, axis=-1, keepdims=True))
        p = p / jnp.sum(p, axis=-1, keepdims=True)
        o = o + _dot(p.astype(BF16), jnp.where(lane_h == h, v, 0.0).astype(BF16))
    o_ref[0] = x + _dot(o.astype(BF16), wo_ref[...])


def cross_attn(h3, g, wq, k, v, wo, tm=512):
    bsz, t, d = h3.shape
    tm = min(tm, t)
    mlen = k.shape[1]
    hd = XA_HEADS * XA_DH
    return pl.pallas_call(
        _xattn_kernel,
        grid=(bsz, t // tm),
        in_specs=[pl.BlockSpec((1, tm, d), lambda b, i: (b, i, 0)),
                  pl.BlockSpec((1, d), lambda b, i: (0, 0)),
                  pl.BlockSpec((d, hd), lambda b, i: (0, 0)),
                  pl.BlockSpec((1, mlen, hd), lambda b, i: (b, 0, 0)),
                  pl.BlockSpec((1, mlen, hd), lambda b, i: (b, 0, 0)),
                  pl.BlockSpec((hd, d), lambda b, i: (0, 0))],
        out_specs=pl.BlockSpec((1, tm, d), lambda b, i: (b, i, 0)),
        out_shape=jax.ShapeDtypeStruct((bsz, t, d), F32),
        compiler_params=_cparams(("parallel", "parallel")),
        name="cross_attn",
    )(h3, g.reshape(1, d), wq, k, v, wo)


def _ffn_kernel(h_ref, g_ref, w1_ref, w3_ref, w2_ref, o_ref, hn_ref, acc_ref):
    f = pl.program_id(1)

    @pl.when(f == 0)
    def _():
        hn_ref[...] = _rms(h_ref[...], g_ref[...]).astype(BF16)
        acc_ref[...] = jnp.zeros_like(acc_ref)

    a = _dot(hn_ref[...], w1_ref[...])
    b = _dot(hn_ref[...], w3_ref[...])
    acc_ref[...] += _dot((a * jax.nn.sigmoid(a) * b).astype(BF16), w2_ref[...])

    @pl.when(f == pl.num_programs(1) - 1)
    def _():
        o_ref[...] = h_ref[...] + acc_ref[...]


def ffn(h, g, w1, w3, w2, tm=512, tf=1408):
    m, d = h.shape
    tm = min(tm, m)
    nf = w1.shape[1]
    assert nf % tf == 0
    return pl.pallas_call(
        _ffn_kernel,
        grid=(m // tm, nf // tf),
        in_specs=[pl.BlockSpec((tm, d), lambda i, f: (i, 0)),
                  pl.BlockSpec((1, d), lambda i, f: (0, 0)),
                  pl.BlockSpec((d, tf), lambda i, f: (0, f)),
                  pl.BlockSpec((d, tf), lambda i, f: (0, f)),
                  pl.BlockSpec((tf, d), lambda i, f: (f, 0))],
        out_specs=pl.BlockSpec((tm, d), lambda i, f: (i, 0)),
        out_shape=jax.ShapeDtypeStruct((m, d), F32),
        scratch_shapes=[pltpu.VMEM((tm, d), BF16), pltpu.VMEM((tm, d), F32)],
        compiler_params=_cparams(("parallel", "arbitrary")),
        name="ffn",
    )(h, g.reshape(1, d), w1, w3, w2)


GATE_LANES = 128
MOE_CHUNK = 256
MOE_ROWS = 128
MOE_TOKENS = 2048


def _router_kernel(h_ref, g_ref, wr_ref, hn_ref, gate_ref, cnt_ref):
    hn = _rms(h_ref[...], g_ref[...])
    hn_ref[...] = hn.astype(BF16)
    tm = hn.shape[0]
    lane = lax.broadcasted_iota(jnp.int32, (tm, GATE_LANES), 1)
    logits = jnp.where(lane < N_EXPERTS, _dot_f32(hn, wr_ref[...]), -jnp.inf)
    m1 = jnp.max(logits, axis=-1, keepdims=True)
    i1 = jnp.min(jnp.where(logits == m1, lane, GATE_LANES), axis=-1, keepdims=True)
    rest = jnp.where(lane == i1, -jnp.inf, logits)
    m2 = jnp.max(rest, axis=-1, keepdims=True)
    i2 = jnp.min(jnp.where(rest == m2, lane, GATE_LANES), axis=-1, keepdims=True)
    e2 = jnp.exp(m2 - m1)
    g1 = 1.0 / (1.0 + e2)
    gates = jnp.where(lane == i1, g1, 0.0) + jnp.where(lane == i2, e2 * g1, 0.0)
    gates = jnp.where(lane == N_EXPERTS, i1.astype(F32), gates)
    gate_ref[...] = jnp.where(lane == N_EXPERTS + 1, i2.astype(F32), gates)
    sel = jnp.where((lane == i1) | (lane == i2), 1.0, 0.0)
    for c in range(tm // MOE_CHUNK):
        cnt_ref[c] = jnp.sum(sel[c * MOE_CHUNK:(c + 1) * MOE_CHUNK, :], axis=0, keepdims=True)


def route_tokens(h, g, wr, tm=512):
    m, d = h.shape
    tm = min(tm, m)
    assert tm % MOE_CHUNK == 0
    wrp = jnp.zeros((d, GATE_LANES), F32).at[:, :N_EXPERTS].set(wr)
    return pl.pallas_call(
        _router_kernel,
        grid=(m // tm,),
        in_specs=[pl.BlockSpec((tm, d), lambda i: (i, 0)),
                  pl.BlockSpec((1, d), lambda i: (0, 0)),
                  pl.BlockSpec((d, GATE_LANES), lambda i: (0, 0))],
        out_specs=[pl.BlockSpec((tm, d), lambda i: (i, 0)),
                   pl.BlockSpec((tm, GATE_LANES), lambda i: (i, 0)),
                   pl.BlockSpec((tm // MOE_CHUNK, 1, GATE_LANES), lambda i: (i, 0, 0))],
        out_shape=[jax.ShapeDtypeStruct((m, d), BF16), jax.ShapeDtypeStruct((m, GATE_LANES), F32),
                   jax.ShapeDtypeStruct((m // MOE_CHUNK, 1, GATE_LANES), F32)],
        compiler_params=_cparams(("parallel",)),
        name="router",
    )(h, g.reshape(1, d), wrp)


def _moe_kernel(cb_ref, hn_ref, gate_ref, w1_ref, w3_ref, w2_ref, y_ref,
                rank_col, rank_row, xs_ref, yacc_ref, rc_ref, gc_ref, xg_ref):
    t = pl.program_id(0)
    e = pl.program_id(1)
    f = pl.program_id(2)
    ts, d = hn_ref.shape
    ch, rb = MOE_CHUNK, MOE_ROWS
    n_ch = ts // ch
    lane = lax.broadcasted_iota(jnp.int32, (1, GATE_LANES), 1)

    def before(c):
        return cb_ref[(t * (n_ch + 1) + c) * N_EXPERTS + e]

    n_blocks = (before(n_ch) + rb - 1) // rb

    @pl.when((e == 0) & (f == 0))
    def _():
        y_ref[...] = jnp.zeros_like(y_ref)
        strict_lower = (lax.broadcasted_iota(jnp.int32, (ch, ch), 1)
                        < lax.broadcasted_iota(jnp.int32, (ch, ch), 0)).astype(BF16)
        lane_f = lane.astype(F32)
        offs = jnp.zeros((1, GATE_LANES), F32)
        for c in range(n_ch):
            rows = slice(c * ch, (c + 1) * ch)
            g = gate_ref[rows, :]
            sel = jnp.where((lane_f == g[:, N_EXPERTS:N_EXPERTS + 1])
                            | (lane_f == g[:, N_EXPERTS + 1:N_EXPERTS + 2]), 1.0, 0.0)
            r = _dot(strict_lower, sel.astype(BF16)) + offs
            r = jnp.where(sel > 0.0, r, -1.0)
            rank_col[rows, :] = r
            rank_row[:, rows] = jnp.transpose(r)
            offs = offs + jnp.sum(sel, axis=0, keepdims=True)

    @pl.when(f == 0)
    def _():
        for c in range(n_ch):
            rows = slice(c * ch, (c + 1) * ch)
            rcol = jnp.sum(jnp.where(lane == e, rank_col[rows, :], 0.0), axis=1, keepdims=True)
            rc_ref[rows, :] = jnp.broadcast_to(rcol, (ch, rb))
            gcol = jnp.sum(jnp.where(lane == e, gate_ref[rows, :], 0.0), axis=1, keepdims=True)
            gc_ref[rows, :] = jnp.broadcast_to(gcol, (ch, GATE_LANES))

        def gather_block(b, carry):
            r0 = pl.multiple_of(b * rb, rb)
            want = (r0 + lax.broadcasted_iota(jnp.int32, (rb, ch), 0)).astype(F32)
            xg_ref[...] = jnp.zeros_like(xg_ref)
            for c in range(n_ch):
                @pl.when((before(c) < r0 + rb) & (before(c + 1) > r0))
                def _(c=c):
                    ranks = rank_row[pl.ds(e, 1), c * ch:(c + 1) * ch]
                    onehot = jnp.where(ranks == want, 1.0, 0.0).astype(BF16)
                    xg_ref[...] += _dot(onehot, hn_ref[c * ch:(c + 1) * ch, :])
            xs_ref[pl.ds(r0, rb), :] = xg_ref[...].astype(BF16)
            yacc_ref[pl.ds(r0, rb), :] = jnp.zeros((rb, d), F32)
            return carry

        lax.fori_loop(0, n_blocks, gather_block, 0)

    def ffn_rows(r0, rows):
        x = xs_ref[pl.ds(r0, rows), :]
        a = _dot(x, w1_ref[0])
        g3 = _dot(x, w3_ref[0])
        yacc_ref[pl.ds(r0, rows), :] += _dot((a * jax.nn.sigmoid(a) * g3).astype(BF16), w2_ref[0])

    n_quads = n_blocks // 4

    def ffn_quad(j, carry):
        ffn_rows(pl.multiple_of(j * (4 * rb), 4 * rb), 4 * rb)
        return carry

    lax.fori_loop(0, n_quads, ffn_quad, 0)
    tail = pl.multiple_of(n_quads * (4 * rb), 4 * rb)

    @pl.when((n_blocks & 2) != 0)
    def _():
        ffn_rows(tail, 2 * rb)

    @pl.when((n_blocks & 1) != 0)
    def _():
        ffn_rows(pl.multiple_of(tail + (n_blocks & 2) * rb, rb), rb)

    @pl.when(f == pl.num_programs(2) - 1)
    def _():
        def scatter_block(b, carry):
            r0 = pl.multiple_of(b * rb, rb)
            yv = yacc_ref[pl.ds(r0, rb), :]
            y_hi = yv.astype(BF16)
            y_lo = (yv - y_hi.astype(F32)).astype(BF16)
            y_split = jnp.concatenate([y_hi, y_lo], axis=0)
            want = (r0 + lax.broadcasted_iota(jnp.int32, (ch, rb), 1)).astype(F32)
            for c in range(n_ch):
                @pl.when((before(c) < r0 + rb) & (before(c + 1) > r0))
                def _(c=c):
                    rows = slice(c * ch, (c + 1) * ch)
                    onehot = jnp.where(rc_ref[rows, :] == want, 1.0, 0.0).astype(BF16)
                    part = _dot(jnp.concatenate([onehot, onehot], axis=1), y_split)
                    gate = jnp.concatenate([gc_ref[rows, :]] * (d // GATE_LANES), axis=1)
                    y_ref[rows, :] += gate * part
            return carry

        lax.fori_loop(0, n_blocks, scatter_block, 0)


def moe(hn, gates, cnt, w1, w3, w2, tf=512):
    m, d = hn.shape
    ts = min(MOE_TOKENS, m)
    n_exp, _, nf = w1.shape
    assert nf % tf == 0 and m % ts == 0 and ts % MOE_CHUNK == 0 and MOE_ROWS == GATE_LANES
    n_ch = ts // MOE_CHUNK
    counts = cnt.reshape(m // ts, n_ch, GATE_LANES)[:, :, :n_exp].astype(jnp.int32)
    before = jnp.concatenate([jnp.zeros((m // ts, 1, n_exp), jnp.int32), jnp.cumsum(counts, axis=1)], axis=1)
    grid_spec = pltpu.PrefetchScalarGridSpec(
        num_scalar_prefetch=1,
        grid=(m // ts, n_exp, nf // tf),
        in_specs=[pl.BlockSpec((ts, d), lambda t, e, f, cb: (t, 0)),
                  pl.BlockSpec((ts, GATE_LANES), lambda t, e, f, cb: (t, 0)),
                  pl.BlockSpec((1, d, tf), lambda t, e, f, cb: (e, 0, f)),
                  pl.BlockSpec((1, d, tf), lambda t, e, f, cb: (e, 0, f)),
                  pl.BlockSpec((1, tf, d), lambda t, e, f, cb: (e, f, 0))],
        out_specs=pl.BlockSpec((ts, d), lambda t, e, f, cb: (t, 0)),
        scratch_shapes=[pltpu.VMEM((ts, GATE_LANES), F32),
                        pltpu.VMEM((GATE_LANES, ts), F32),
                        pltpu.VMEM((ts, d), BF16),
                        pltpu.VMEM((ts, d), F32),
                        pltpu.VMEM((ts, MOE_ROWS), F32),
                        pltpu.VMEM((ts, GATE_LANES), F32),
                        pltpu.VMEM((MOE_ROWS, d), F32)])
    return pl.pallas_call(
        _moe_kernel,
        grid_spec=grid_spec,
        out_shape=jax.ShapeDtypeStruct((m, d), F32),
        compiler_params=_cparams(("parallel", "arbitrary", "arbitrary")),
        name="moe",
    )(before.reshape(-1), hn, gates, w1, w3, w2)


def moe_block(h, g, wr, w1, w3, w2):
    hn, gates, cnt = route_tokens(h, g, wr)
    return moe(hn, gates, cnt, w1.astype(BF16), w3.astype(BF16), w2.astype(BF16))


def _final_norm_kernel(g_ref, *refs):
    *x_refs, o_ref = refs
    x = x_refs[0][...]
    for r in x_refs[1:]:
        x = x + r[...]
    o_ref[...] = _rms(x, g_ref[...])


def final_rmsnorm(g, *xs, tm=1024):
    m, d = xs[0].shape
    tm = min(tm, m)
    return pl.pallas_call(
        _final_norm_kernel,
        grid=(m // tm,),
        in_specs=[pl.BlockSpec((1, d), lambda i: (0, 0))] + [pl.BlockSpec((tm, d), lambda i: (i, 0))] * len(xs),
        out_specs=pl.BlockSpec((tm, d), lambda i: (i, 0)),
        out_shape=jax.ShapeDtypeStruct((m, d), F32),
        compiler_params=_cparams(("parallel",)),
        name="final_norm",
    )(g.reshape(1, d), *xs)


def _pad_cols(w, width):
    return jnp.pad(w, ((0, 0), (0, width - w.shape[1])))


def _in_proj_weight(w_in):
    a_end = 2 * GLA_HEADS * GLA_DK + 2 * GROUP_WIDTH + GLA_RANK
    b_end = a_end + GROUP_WIDTH
    c_end = b_end + 3 * GROUP_WIDTH + IDX_HEADS * IDX_DIM + IDX_DIM + IDX_HEADS
    assert w_in.shape[1] == c_end + 2 * GROUP_WIDTH
    return jnp.concatenate([_pad_cols(w_in[:, :a_end], A_WIDTH), w_in[:, a_end:b_end],
                            _pad_cols(w_in[:, b_end:c_end], C_WIDTH), w_in[:, c_end:]], axis=1).astype(BF16)


def hybrid_layer(h, mem2, toe, p, bsz, t):
    m, d = h.shape
    z_a, z_b, z_c, z_d = norm_matmul(h, p["norm_mix"], _in_proj_weight(p["w_in"]),
                                     (A_WIDTH, B_WIDTH, C_WIDTH, D_WIDTH))
    o_a = gla_mixer(z_a.reshape(bsz, t, A_WIDTH), p["gla_wa2"], p["gla_ba"], p["gla_norm"])
    o_b = pool_mixer(z_b.reshape(bsz, t, B_WIDTH), p["pool_w"], p["pool_b"].reshape(-1), p["pool_scale"])
    o_c = dsa_mixer(z_c.reshape(bsz, t, C_WIDTH), toe)
    o_d = sgu_mixer(z_d.reshape(bsz, t, D_WIDTH), p["sgu_ln_g"], p["sgu_ln_b"], p["sgu_w"], p["sgu_b"])
    h = out_proj(h, [o.reshape(m, GROUP_WIDTH) for o in (o_a, o_b, o_c, o_d)], p["w_out"].astype(BF16))

    hd = XA_HEADS * XA_DH
    k, v = norm_matmul(mem2, p["norm_mem"], p["xa_wkv"].astype(BF16), (hd, hd))
    mlen = mem2.shape[0] // bsz
    h = cross_attn(h.reshape(bsz, t, d), p["norm_xa"], p["xa_wq"].astype(BF16),
                   k.reshape(bsz, mlen, hd), v.reshape(bsz, mlen, hd), p["xa_wo"].astype(BF16))
    return h.reshape(m, d)


def kernel(x, mem, rel_bias, final_norm, norm_mix, w_in, gla_wa2, gla_ba, gla_norm, pool_w, pool_b,
           pool_scale, sgu_ln_g, sgu_ln_b, sgu_w, sgu_b, w_out, norm_xa, norm_mem, xa_wq, xa_wkv, xa_wo,
           norm_ffn, ffn_w1, ffn_w3, ffn_w2, router, moe_w1, moe_w3, moe_w2):
    bsz, t, d = x.shape
    depth = norm_mix.shape[0]
    h = x.reshape(bsz * t, d)
    mem2 = mem.reshape(-1, d)
    toe = rel_bias_tables(rel_bias)
    pending = None
    for i in range(depth):
        if pending is not None:
            h, pending = h + pending, None
        p = dict(norm_mix=norm_mix[i], w_in=w_in[i], gla_wa2=gla_wa2[i], gla_ba=gla_ba[i],
                 gla_norm=gla_norm[i], pool_w=pool_w[i], pool_b=pool_b[i], pool_scale=pool_scale[i],
                 sgu_ln_g=sgu_ln_g[i], sgu_ln_b=sgu_ln_b[i], sgu_w=sgu_w[i], sgu_b=sgu_b[i],
                 w_out=w_out[i], norm_xa=norm_xa[i], norm_mem=norm_mem[i], xa_wq=xa_wq[i],
                 xa_wkv=xa_wkv[i], xa_wo=xa_wo[i])
        h = hybrid_layer(h, mem2, toe, p, bsz, t)
        j = i // 2
        if i % 2 == 0:
            h = ffn(h, norm_ffn[i], ffn_w1[j].astype(BF16), ffn_w3[j].astype(BF16), ffn_w2[j].astype(BF16))
        else:
            pending = moe_block(h, norm_ffn[i], router[j], moe_w1[j], moe_w3[j], moe_w2[j])
    xs = (h,) if pending is None else (h, pending)
    return final_rmsnorm(final_norm, *xs).reshape(bsz, t, d)
```

```python
import functools
import math

import jax
import jax.numpy as jnp
import numpy as np
from jax import lax
from jax.experimental import pallas as pl
from jax.experimental.pallas import tpu as pltpu

F32 = jnp.float32
BF16 = jnp.bfloat16
EPS = 1e-6

GROUP_WIDTH = 256

GLA_HEADS = 4
GLA_DV = 64
GLA_DK = 32
GLA_RANK = 16
GLA_TAU = 16.0
GLA_CHUNK = 64
GLA_GROUP = 8

POOL_WINDOWS = (2, 4, 8, 16)
POOL_CG = 64

DSA_HEADS = 4
DSA_DH = 64
IDX_HEADS = 8
IDX_DIM = 32
DSA_TOPK_MAX = 256
DSA_BLOCK = 256
DSA_SUB = 64

SGU_GROUPS = 4
SGU_CHUNK = 128
SGU_CG = 64

REL_BUCKETS = 32
REL_MAX_DIST = 128

XA_HEADS = 4
XA_DH = 64

N_EXPERTS = 8

A_WIDTH = 896
B_WIDTH = 256
C_WIDTH = 1152
D_WIDTH = 512

INT_MIN = -(2 ** 31)
NEG_BIG = -1e30
VMEM_LIMIT = 56 * 1024 * 1024


def _cparams(sem):
    return pltpu.CompilerParams(dimension_semantics=sem, vmem_limit_bytes=VMEM_LIMIT)


def _dot(a, b):
    return jnp.dot(a, b, preferred_element_type=F32)


def _dot_nt(a, b):
    return lax.dot_general(a, b, (((1,), (1,)), ((), ())), preferred_element_type=F32)


def _dot_tn(a, b):
    return lax.dot_general(a, b, (((0,), (0,)), ((), ())), preferred_element_type=F32)


def _dot_f32(a, b):
    return jnp.dot(a, b, preferred_element_type=F32, precision=lax.Precision.HIGHEST)


def _split_bf16(x, terms):
    parts = []
    for _ in range(terms):
        p = x.astype(BF16)
        parts.append(p)
        x = x - p.astype(F32)
    return parts


def _dot_exact_rhs(a, b, terms):
    out = None
    for p in _split_bf16(a, terms):
        d = _dot(p, b)
        out = d if out is None else out + d
    return out


def _dot_exact_lhs(a, b, terms):
    out = None
    for p in _split_bf16(b, terms):
        d = _dot(a, p)
        out = d if out is None else out + d
    return out


def _rms(x, g):
    return x * lax.rsqrt(jnp.mean(x * x, axis=-1, keepdims=True) + EPS) * g


def _norm_matmul_kernel(h_ref, g_ref, w_ref, *out_refs, widths):
    hb = _rms(h_ref[...], g_ref[...]).astype(BF16)
    off = 0
    for o_ref, wd in zip(out_refs, widths):
        o_ref[...] = _dot(hb, w_ref[:, off:off + wd]).astype(o_ref.dtype)
        off += wd


def norm_matmul(h, g, w, widths, tm=512):
    m, d = h.shape
    tm = min(tm, m)
    n = sum(widths)
    return pl.pallas_call(
        functools.partial(_norm_matmul_kernel, widths=widths),
        grid=(m // tm,),
        in_specs=[pl.BlockSpec((tm, d), lambda i: (i, 0)),
                  pl.BlockSpec((1, d), lambda i: (0, 0)),
                  pl.BlockSpec((d, n), lambda i: (0, 0))],
        out_specs=[pl.BlockSpec((tm, wd), lambda i: (i, 0)) for wd in widths],
        out_shape=[jax.ShapeDtypeStruct((m, wd), F32) for wd in widths],
        compiler_params=_cparams(("parallel",)),
        name="norm_matmul",
    )(h, g.reshape(1, d), w)


def _log_sigmoid(x):
    return jnp.minimum(x, 0.0) - jnp.log1p(jnp.exp(-jnp.abs(x)))


def _gla_kernel(z_ref, wa2_ref, ba_ref, ng_ref, o_ref, s_ref, *, n_groups):
    c = GLA_CHUNK
    grp = GLA_GROUP * c
    hk = GLA_HEADS * GLA_DK
    hv = GLA_HEADS * GLA_DV
    s_ref[...] = jnp.zeros_like(s_ref)

    head_k = lax.broadcasted_iota(jnp.int32, (1, hk), 1) // GLA_DK
    head_v = lax.broadcasted_iota(jnp.int32, (1, hv), 1) // GLA_DV
    g_row = lax.broadcasted_iota(jnp.int32, (grp, grp), 0)
    g_col = lax.broadcasted_iota(jnp.int32, (grp, grp), 1)
    tril = (((g_row // c) == (g_col // c)) & (g_col <= g_row)).astype(BF16)
    causal4 = (lax.broadcasted_iota(jnp.int32, (GLA_HEADS * c, c), 1)
               <= lax.broadcasted_iota(jnp.int32, (GLA_HEADS * c, c), 0) % c)
    state_mask = (lax.broadcasted_iota(jnp.int32, (hk, hv), 0) // GLA_DK
                  == lax.broadcasted_iota(jnp.int32, (hk, hv), 1) // GLA_DV)
    norm_mat = jnp.where(lax.broadcasted_iota(jnp.int32, (hv, hv), 0) // GLA_DV
                         == lax.broadcasted_iota(jnp.int32, (hv, hv), 1) // GLA_DV,
                         1.0 / GLA_DV, 0.0).astype(BF16)
    wa2 = wa2_ref[...]
    ba = ba_ref[...]
    ng = ng_ref[...]

    def body(n, carry):
        r0 = pl.multiple_of(n * grp, grp)
        z = z_ref[0, pl.ds(r0, grp), :]
        q, k, v, g, lr = z[:, 0:128], z[:, 128:256], z[:, 256:512], z[:, 512:768], z[:, 768:896]
        log_a = _log_sigmoid(_dot_f32(lr, wa2) + ba) / GLA_TAU
        b = _dot_exact_lhs(tril, log_a, 3)
        b_end = jnp.concatenate([jnp.broadcast_to(b[(ci + 1) * c - 1:(ci + 1) * c, :], (c, hk))
                                 for ci in range(GLA_GROUP)], axis=0)
        q_t = q * (GLA_DK ** -0.5) * jnp.exp(b)
        q_tb = q_t.astype(BF16)
        k_t = (k * jnp.exp(-b)).astype(BF16)
        k_dec = (k * jnp.exp(b_end - b)).astype(BF16)
        vb = v.astype(BF16)
        outs = []
        for ci in range(GLA_GROUP):
            rows = slice(ci * c, (ci + 1) * c)
            q4 = jnp.concatenate([jnp.where(head_k == h, q_t[rows], 0.0) for h in range(GLA_HEADS)],
                                 axis=0).astype(BF16)
            att = jnp.where(causal4, _dot_nt(q4, k_t[rows]), 0.0)
            r = _dot(att.astype(BF16), vb[rows])
            o = _dot(q_tb[rows], s_ref[...].astype(BF16))
            for h in range(GLA_HEADS):
                o = o + jnp.where(head_v == h, r[h * c:(h + 1) * c, :], 0.0)
            outs.append(o)
            kv = jnp.where(state_mask, _dot_tn(k_dec[rows], vb[rows]), 0.0)
            last = b_end[ci * c:ci * c + 1, :]
            dec = jnp.exp(jnp.transpose(jnp.broadcast_to(last, (hk, hk))))
            s_ref[...] = s_ref[...] * jnp.concatenate([dec, dec], axis=1) + kv
        o = jnp.concatenate(outs, axis=0)
        o = o * lax.rsqrt(_dot_exact_rhs(o * o, norm_mat, 2) + EPS) * ng
        o_ref[0, pl.ds(r0, grp), :] = o * (g * jax.nn.sigmoid(g))
        return carry

    lax.fori_loop(0, n_groups, body, 0)


def gla_mixer(z_a, wa2, ba, norm_g):
    bsz, t, _ = z_a.shape
    hk = GLA_HEADS * GLA_DK
    wa2p = jnp.zeros((128, hk), F32).at[:GLA_RANK].set(wa2)
    return pl.pallas_call(
        functools.partial(_gla_kernel, n_groups=t // (GLA_GROUP * GLA_CHUNK)),
        grid=(bsz,),
        in_specs=[pl.BlockSpec((1, t, A_WIDTH), lambda b: (b, 0, 0)),
                  pl.BlockSpec((128, hk), lambda b: (0, 0)),
                  pl.BlockSpec((1, hk), lambda b: (0, 0)),
                  pl.BlockSpec((1, GROUP_WIDTH), lambda b: (0, 0))],
        out_specs=pl.BlockSpec((1, t, GROUP_WIDTH), lambda b: (b, 0, 0)),
        out_shape=jax.ShapeDtypeStruct((bsz, t, GROUP_WIDTH), F32),
        scratch_shapes=[pltpu.VMEM((hk, GROUP_WIDTH), F32)],
        compiler_params=_cparams(("parallel",)),
        name="gla_mixer",
    )(z_a, wa2p, ba.reshape(1, hk), norm_g.reshape(1, GROUP_WIDTH))


def _pool_kernel(u_ref, w_ref, b_ref, sc_ref, o_ref):
    u = u_ref[0]
    t, gw = u.shape
    row = lax.broadcasted_iota(jnp.int32, (t, gw), 0)
    grp = lax.broadcasted_iota(jnp.int32, (t, gw), 1) // POOL_CG

    def shifted(x, k):
        return jnp.where(row >= k, pltpu.roll(x, k, axis=0), 0.0)

    s = u
    p = jnp.zeros_like(u)
    for gi, win in enumerate(POOL_WINDOWS):
        half = win // 2
        s = s + shifted(s, half)
        cnt = jnp.minimum(row + 1, win).astype(F32)
        p = jnp.where(grp == gi, s / cnt - u, p)
    y = _dot(p.astype(BF16), w_ref[...]) + b_ref[...]
    o_ref[0] = y * sc_ref[...]


def pool_mixer(z_b, w, b, scale):
    assert POOL_WINDOWS == (2, 4, 8, 16)
    bsz, t, gw = z_b.shape
    w_bd = jnp.zeros((gw, gw), F32)
    for gi in range(len(POOL_WINDOWS)):
        w_bd = w_bd.at[gi * POOL_CG:(gi + 1) * POOL_CG, gi * POOL_CG:(gi + 1) * POOL_CG].set(w[gi])
    return pl.pallas_call(
        _pool_kernel,
        grid=(bsz,),
        in_specs=[pl.BlockSpec((1, t, gw), lambda i: (i, 0, 0)),
                  pl.BlockSpec((gw, gw), lambda i: (0, 0)),
                  pl.BlockSpec((1, gw), lambda i: (0, 0)),
                  pl.BlockSpec((1, gw), lambda i: (0, 0))],
        out_specs=pl.BlockSpec((1, t, gw), lambda i: (i, 0, 0)),
        out_shape=jax.ShapeDtypeStruct((bsz, t, gw), F32),
        compiler_params=_cparams(("parallel",)),
        name="pool_mixer",
    )(z_b, w_bd.astype(BF16), b.reshape(1, gw), scale.reshape(1, gw))


def _sgu_kernel(z_ref, lg_ref, lb_ref, w_ref, bm_ref, o_ref, *, chunks):
    c = SGU_CHUNK
    gw = GROUP_WIDTH
    rows = SGU_GROUPS * c
    tri = (lax.broadcasted_iota(jnp.int32, (rows, c), 1)
           <= lax.broadcasted_iota(jnp.int32, (rows, c), 0) % c)
    ws = jnp.where(tri, w_ref[...], 0.0).astype(BF16)
    grp = lax.broadcasted_iota(jnp.int32, (1, gw), 1) // SGU_CG
    for ci in range(chunks):
        z = jax.nn.gelu(z_ref[0, ci * c:(ci + 1) * c, :], approximate=True)
        u, v = z[:, :gw], z[:, gw:]
        mu = jnp.mean(v, axis=-1, keepdims=True)
        var = jnp.mean(jnp.square(v - mu), axis=-1, keepdims=True)
        vn = (v - mu) * lax.rsqrt(var + EPS) * lg_ref[...] + lb_ref[...]
        r = _dot(ws, vn.astype(BF16))
        mixed = bm_ref[...]
        for g in range(SGU_GROUPS):
            mixed = mixed + jnp.where(grp == g, r[g * c:(g + 1) * c, :], 0.0)
        o_ref[0, ci * c:(ci + 1) * c, :] = u * mixed


def sgu_mixer(z_d, ln_g, ln_b, w_s, b_s, chunks=4):
    bsz, t, _ = z_d.shape
    gw = GROUP_WIDTH
    tt = chunks * SGU_CHUNK
    bias = jnp.repeat(b_s.T, SGU_CG, axis=1)
    return pl.pallas_call(
        functools.partial(_sgu_kernel, chunks=chunks),
        grid=(bsz, t // tt),
        in_specs=[pl.BlockSpec((1, tt, 2 * gw), lambda b, i: (b, i, 0)),
                  pl.BlockSpec((1, gw), lambda b, i: (0, 0)),
                  pl.BlockSpec((1, gw), lambda b, i: (0, 0)),
                  pl.BlockSpec((SGU_GROUPS * SGU_CHUNK, SGU_CHUNK), lambda b, i: (0, 0)),
                  pl.BlockSpec((SGU_CHUNK, gw), lambda b, i: (0, 0))],
        out_specs=pl.BlockSpec((1, tt, gw), lambda b, i: (b, i, 0)),
        out_shape=jax.ShapeDtypeStruct((bsz, t, gw), F32),
        compiler_params=_cparams(("parallel", "parallel")),
        name="sgu_mixer",
    )(z_d, ln_g.reshape(1, gw), ln_b.reshape(1, gw),
      w_s.reshape(SGU_GROUPS * SGU_CHUNK, SGU_CHUNK), bias)


def _bucket_table():
    assert REL_MAX_DIST <= DSA_BLOCK + 1
    s = np.arange(DSA_BLOCK)[:, None]
    t = np.arange(DSA_BLOCK)[None, :]
    dist = np.stack([t - s, DSA_BLOCK + t - s, 2 * DSA_BLOCK + t - s])
    n = np.maximum(dist, 0)
    max_exact = REL_BUCKETS // 2
    nf = np.maximum(n, 1).astype(np.float32)
    large = max_exact + (np.log(nf / np.float32(max_exact)) / np.float32(math.log(REL_MAX_DIST / max_exact))
                         * np.float32(REL_BUCKETS - max_exact)).astype(np.int32)
    return np.where(n < max_exact, n, np.minimum(large, REL_BUCKETS - 1)).astype(np.int32)


def _bias_table_kernel(rb_ref, bucket_ref, o_ref):
    for back in range(3):
        bucket = bucket_ref[back]
        for h in range(DSA_HEADS):
            acc = jnp.zeros(bucket.shape, F32)
            for b in range(REL_BUCKETS):
                acc = jnp.where(bucket == b, rb_ref[b * DSA_HEADS + h], acc)
            o_ref[back, h] = acc


def rel_bias_tables(rel_bias):
    blk = DSA_BLOCK
    return pl.pallas_call(
        _bias_table_kernel,
        in_specs=[pl.BlockSpec(memory_space=pltpu.SMEM),
                  pl.BlockSpec((3, blk, blk), lambda: (0, 0, 0))],
        out_specs=pl.BlockSpec((3, DSA_HEADS, blk, blk), lambda: (0, 0, 0, 0)),
        out_shape=jax.ShapeDtypeStruct((3, DSA_HEADS, blk, blk), F32),
        name="rel_bias_tables",
    )(rel_bias.reshape(-1), jnp.asarray(_bucket_table()))


def _dsa_kernel(q_ref, k_ref, v_ref, qi_ref, kw_ref, qw_ref, toe_ref, o_ref,
                kpl_ref, kb16_ref, vt_ref, keys_ref, khi_ref, klo_ref, acc_ref, am_ref, lg_ref, p_ref,
                *, topk, n_blocks, idx_bits):
    blk = DSA_BLOCK
    sub = DSA_SUB
    i = pl.program_id(1)
    hd = DSA_HEADS * DSA_DH
    heads_per_half = 128 // IDX_DIM

    @pl.when(i == 0)
    def _():
        lane = lax.broadcasted_iota(jnp.int32, (blk, 128), 1)

        def build(kb, c):
            r0 = pl.multiple_of(kb * blk, blk)
            ki = jnp.where(lane < IDX_DIM, kw_ref[0, pl.ds(r0, blk), :], 0.0)
            for j in range(heads_per_half):
                kpl_ref[kb, j] = (ki if j == 0 else pltpu.roll(ki, j * IDX_DIM, axis=1)).astype(BF16)
            kb16_ref[kb] = k_ref[0, pl.ds(r0, blk), :].astype(BF16)
            vt_ref[kb] = jnp.transpose(v_ref[0, pl.ds(r0, blk), :]).astype(BF16)
            return c

        lax.fori_loop(0, n_blocks, build, 0)

    s_loc = lax.broadcasted_iota(jnp.int32, (blk, blk), 0)
    t_loc = lax.broadcasted_iota(jnp.int32, (blk, blk), 1)
    n_vis = i + 1

    qi_t = jnp.transpose(qi_ref[0]).astype(BF16)
    qi_halves = [qi_t[:128, :], qi_t[128:, :]]
    w_t = jnp.transpose(qw_ref[0])
    w_rows = [w_t[IDX_DIM + h:IDX_DIM + h + 1, :] * (IDX_HEADS ** -0.5) * (IDX_DIM ** -0.5)
              for h in range(IDX_HEADS)]
    s_sub = lax.broadcasted_iota(jnp.int32, (sub, blk), 0)
    t_sub = lax.broadcasted_iota(jnp.int32, (sub, blk), 1)

    def score_body(kb, c):
        for ci in range(blk // sub):
            rows = slice(ci * sub, (ci + 1) * sub)
            sc = jnp.zeros((sub, blk), F32)
            for half in range(2):
                for j in range(heads_per_half):
                    d = _dot(kpl_ref[kb, j, rows, :], qi_halves[half])
                    sc = sc + jnp.maximum(d, 0.0) * w_rows[half * heads_per_half + j]
            sc = jnp.where(sc == 0.0, 0.0, sc)
            bits = pltpu.bitcast(sc, jnp.int32)
            key = jnp.where(bits < 0, bits ^ jnp.int32(0x7FFFFFFF), bits)
            vis = (kb < i) | (s_sub + ci * sub <= t_sub)
            key = jnp.where(vis, key, jnp.int32(INT_MIN))
            keys_ref[kb, rows, :] = key
            khi_ref[kb, rows, :] = lax.shift_right_arithmetic(key, 16).astype(jnp.int16)
            klo_ref[kb, rows, :] = ((key & 0xFFFF) - 2 ** 15).astype(jnp.int16)
        return c

    lax.fori_loop(0, n_vis, score_body, 0)

    def count(pred):
        def body(kb, acc):
            hit = jnp.where(pred(keys_ref[kb], kb), 1.0, 0.0)
            return acc + jnp.sum(hit.reshape(blk // 32, 32, blk), axis=0)
        acc = lax.fori_loop(0, n_vis, body, jnp.zeros((32, blk), F32))
        return jnp.sum(acc, axis=0, keepdims=True)

    def count16(ref, cand):
        cand = cand.astype(jnp.int16)

        def body(kb, acc):
            hit = jnp.where(ref[kb] >= cand, jnp.int16(1), jnp.int16(0))
            for j in range(blk // 32):
                acc = acc + hit[j * 32:(j + 1) * 32]
            return acc
        acc = lax.fori_loop(0, n_vis, body, jnp.zeros((32, blk), jnp.int16))
        return jnp.sum(acc.astype(jnp.int32).astype(F32), axis=0, keepdims=True)

    def search16(ref, offset):
        lowest = jnp.full((1, blk), -(2 ** 15), jnp.int32)
        base = jnp.where(offset + count16(ref, jnp.zeros((1, blk), jnp.int32)) >= kf, 0, lowest)

        def bit_body(it, base):
            cand = base | lax.shift_left(jnp.int32(1), 14 - it)
            return jnp.where(offset + count16(ref, cand) >= kf, cand, base)
        return lax.fori_loop(0, 15, bit_body, base)

    kf = float(topk)
    zero = jnp.zeros((1, blk), jnp.int32)
    thr_hi = search16(khi_ref, 0.0)
    thr_hi16 = thr_hi.astype(jnp.int16)

    def low_body(kb, above):
        hi = khi_ref[kb]
        klo_ref[kb] = jnp.where(hi == thr_hi16, klo_ref[kb], jnp.int16(-(2 ** 15)))
        hit = jnp.where(hi > thr_hi16, jnp.int16(1), jnp.int16(0))
        for j in range(blk // 32):
            above = above + hit[j * 32:(j + 1) * 32]
        return above

    above = lax.fori_loop(0, n_vis, low_body, jnp.zeros((32, blk), jnp.int16))
    n_above = jnp.sum(above.astype(jnp.int32).astype(F32), axis=0, keepdims=True)
    thr_lo = search16(klo_ref, n_above)
    thr = lax.shift_left(thr_hi, 16) | (thr_lo + 2 ** 15)

    n_gt = count(lambda key, kb: key > thr)
    n_eq = count(lambda key, kb: key == thr)
    need = kf - n_gt
    excess = jnp.where((n_eq > need) & (thr > jnp.int32(INT_MIN)), 1.0, 0.0)

    def tie_search():
        def tie_body(it, j):
            cand = j | lax.shift_left(jnp.int32(1), idx_bits - 1 - it)
            below = count(lambda key, kb: (key == thr) & (kb * blk + s_loc < cand))
            return jnp.where(below < need, cand, j)
        return lax.fori_loop(0, idx_bits, tie_body, zero)

    last = lax.cond(jnp.max(excess) > 0.0, tie_search, lambda: jnp.full((1, blk), 2 ** 30, jnp.int32))

    assert DSA_DH ** -0.5 == 0.125
    q_t = jnp.transpose(q_ref[0] * (DSA_DH ** -0.5))
    row_h = lax.broadcasted_iota(jnp.int32, (hd, 1), 0) // DSA_DH
    q_heads = [jnp.where(row_h == h, q_t, 0.0).astype(BF16) for h in range(DSA_HEADS)]
    acc_ref[...] = jnp.zeros_like(acc_ref)
    n_sub = blk // sub

    def att_body(kb, carry):
        ms, ls = carry
        back = jnp.minimum(i - kb, 2)
        for ci in range(n_sub):
            rows = slice(ci * sub, (ci + 1) * sub)
            key = keys_ref[kb, rows, :]
            sel = (((key > thr) | ((key == thr) & (kb * blk + ci * sub + s_sub <= last)))
                   & ((kb < i) | (s_sub + ci * sub <= t_sub)))
            am_ref[rows, :] = jnp.where(sel, 0.0, NEG_BIG)
        new_ms, alphas = [], []
        for h in range(DSA_HEADS):
            pm = jnp.full((8, blk), NEG_BIG, F32)
            for ci in range(n_sub):
                rows = slice(ci * sub, (ci + 1) * sub)
                lg = _dot(kb16_ref[kb, rows, :], q_heads[h]) + toe_ref[back, h, rows, :] + am_ref[rows, :]
                lg_ref[h, rows, :] = lg
                pm = jnp.maximum(pm, jnp.max(lg.reshape(sub // 8, 8, blk), axis=0))
            m_new = jnp.maximum(ms[h], jnp.max(pm, axis=0, keepdims=True))
            new_ms.append(m_new)
            alphas.append(jnp.exp(ms[h] - m_new))
        new_ls = []
        for h in range(DSA_HEADS):
            ps = jnp.zeros((8, blk), F32)
            for ci in range(n_sub):
                rows = slice(ci * sub, (ci + 1) * sub)
                p = jnp.exp(lg_ref[h, rows, :] - new_ms[h])
                ps = ps + jnp.sum(p.reshape(sub // 8, 8, blk), axis=0)
                p_ref[h, rows, :] = p.astype(BF16)
            new_ls.append(ls[h] * alphas[h] + jnp.sum(ps, axis=0, keepdims=True))
        for h in range(DSA_HEADS):
            hrows = slice(h * DSA_DH, (h + 1) * DSA_DH)
            acc_ref[hrows, :] = acc_ref[hrows, :] * alphas[h] + _dot(vt_ref[kb, hrows, :], p_ref[h])
        return tuple(new_ms), tuple(new_ls)

    init = (tuple(jnp.full((1, blk), 0.01 * NEG_BIG, F32) for _ in range(DSA_HEADS)),
            tuple(jnp.zeros((1, blk), F32) for _ in range(DSA_HEADS)))
    _, ls = lax.fori_loop(0, n_vis, att_body, init)
    for h in range(DSA_HEADS):
        rows = slice(h * DSA_DH, (h + 1) * DSA_DH)
        acc_ref[rows, :] = acc_ref[rows, :] * (1.0 / ls[h])
    o_ref[0] = jnp.transpose(acc_ref[...])


def dsa_mixer(z_c, toe):
    bsz, t, _ = z_c.shape
    blk = DSA_BLOCK
    n_blocks = t // blk
    assert t % blk == 0
    topk = min(DSA_TOPK_MAX, t // 4)
    hd = DSA_HEADS * DSA_DH
    kernel = functools.partial(_dsa_kernel, topk=topk, n_blocks=n_blocks,
                               idx_bits=max(1, (t - 1).bit_length()))
    return pl.pallas_call(
        kernel,
        grid=(bsz, n_blocks),
        in_specs=[pl.BlockSpec((1, blk, hd), lambda b, i: (b, i, 0)),
                  pl.BlockSpec((1, t, hd), lambda b, i: (b, 0, 1)),
                  pl.BlockSpec((1, t, hd), lambda b, i: (b, 0, 2)),
                  pl.BlockSpec((1, blk, hd), lambda b, i: (b, i, 3)),
                  pl.BlockSpec((1, t, 128), lambda b, i: (b, 0, 8)),
                  pl.BlockSpec((1, blk, 128), lambda b, i: (b, i, 8)),
                  pl.BlockSpec((3, DSA_HEADS, blk, blk), lambda b, i: (0, 0, 0, 0))],
        out_specs=pl.BlockSpec((1, blk, hd), lambda b, i: (b, i, 0)),
        out_shape=jax.ShapeDtypeStruct((bsz, t, hd), F32),
        scratch_shapes=[pltpu.VMEM((n_blocks, 128 // IDX_DIM, blk, 128), BF16),
                        pltpu.VMEM((n_blocks, blk, hd), BF16),
                        pltpu.VMEM((n_blocks, hd, blk), BF16),
                        pltpu.VMEM((n_blocks, blk, blk), jnp.int32),
                        pltpu.VMEM((n_blocks, blk, blk), jnp.int16),
                        pltpu.VMEM((n_blocks, blk, blk), jnp.int16),
                        pltpu.VMEM((hd, blk), F32),
                        pltpu.VMEM((blk, blk), F32),
                        pltpu.VMEM((DSA_HEADS, blk, blk), F32),
                        pltpu.VMEM((DSA_HEADS, blk, blk), BF16)],
        compiler_params=_cparams(("parallel", "arbitrary")),
        name="dsa_mixer",
    )(z_c, z_c, z_c, z_c, z_c, z_c, toe)


def _out_proj_kernel(h_ref, a_ref, b_ref, c_ref, d_ref, w_ref, o_ref):
    gw = GROUP_WIDTH
    acc = h_ref[...]
    for gi, r in enumerate((a_ref, b_ref, c_ref, d_ref)):
        acc = acc + _dot(r[...].astype(BF16), w_ref[gi * gw:(gi + 1) * gw, :])
    o_ref[...] = acc


def out_proj(h, outs, w, tm=512):
    m, d = h.shape
    tm = min(tm, m)
    gw = GROUP_WIDTH
    return pl.pallas_call(
        _out_proj_kernel,
        grid=(m // tm,),
        in_specs=[pl.BlockSpec((tm, d), lambda i: (i, 0))]
                 + [pl.BlockSpec((tm, gw), lambda i: (i, 0))] * 4
                 + [pl.BlockSpec((4 * gw, d), lambda i: (0, 0))],
        out_specs=pl.BlockSpec((tm, d), lambda i: (i, 0)),
        out_shape=jax.ShapeDtypeStruct((m, d), F32),
        compiler_params=_cparams(("parallel",)),
        name="out_proj",
    )(h, *outs, w)


def _xattn_kernel(h_ref, g_ref, wq_ref, k_ref, v_ref, wo_ref, o_ref):
    x = h_ref[0]
    hd = XA_HEADS * XA_DH
    q = _dot(_rms(x, g_ref[...]).astype(BF16), wq_ref[...]).astype(BF16)
    k = k_ref[0]
    v = v_ref[0]
    lane_h = lax.broadcasted_iota(jnp.int32, (1, hd), 1) // XA_DH
    o = jnp.zeros((x.shape[0], hd), F32)
    for h in range(XA_HEADS):
        s = _dot_nt(q, jnp.where(lane_h == h, k, 0.0).astype(BF16)) * (XA_DH ** -0.5)
        p = jnp.exp(s - jnp.max(s, axis=-1, keepdims=True))
        p = p / jnp.sum(p, axis=-1, keepdims=True)
        o = o + _dot(p.astype(BF16), jnp.where(lane_h == h, v, 0.0).astype(BF16))
    o_ref[0] = x + _dot(o.astype(BF16), wo_ref[...])


def cross_attn(h3, g, wq, k, v, wo, tm=512):
    bsz, t, d = h3.shape
    tm = min(tm, t)
    mlen = k.shape[1]
    hd = XA_HEADS * XA_DH
    return pl.pallas_call(
        _xattn_kernel,
        grid=(bsz, t // tm),
        in_specs=[pl.BlockSpec((1, tm, d), lambda b, i: (b, i, 0)),
                  pl.BlockSpec((1, d), lambda b, i: (0, 0)),
                  pl.BlockSpec((d, hd), lambda b, i: (0, 0)),
                  pl.BlockSpec((1, mlen, hd), lambda b, i: (b, 0, 0)),
                  pl.BlockSpec((1, mlen, hd), lambda b, i: (b, 0, 0)),
                  pl.BlockSpec((hd, d), lambda b, i: (0, 0))],
        out_specs=pl.BlockSpec((1, tm, d), lambda b, i: (b, i, 0)),
        out_shape=jax.ShapeDtypeStruct((bsz, t, d), F32),
        compiler_params=_cparams(("parallel", "parallel")),
        name="cross_attn",
    )(h3, g.reshape(1, d), wq, k, v, wo)


def _ffn_kernel(h_ref, g_ref, w1_ref, w3_ref, w2_ref, o_ref, hn_ref, acc_ref):
    f = pl.program_id(1)

    @pl.when(f == 0)
    def _():
        hn_ref[...] = _rms(h_ref[...], g_ref[...]).astype(BF16)
        acc_ref[...] = jnp.zeros_like(acc_ref)

    a = _dot(hn_ref[...], w1_ref[...])
    b = _dot(hn_ref[...], w3_ref[...])
    acc_ref[...] += _dot((a * jax.nn.sigmoid(a) * b).astype(BF16), w2_ref[...])

    @pl.when(f == pl.num_programs(1) - 1)
    def _():
        o_ref[...] = h_ref[...] + acc_ref[...]


def ffn(h, g, w1, w3, w2, tm=512, tf=1408):
    m, d = h.shape
    tm = min(tm, m)
    nf = w1.shape[1]
    assert nf % tf == 0
    return pl.pallas_call(
        _ffn_kernel,
        grid=(m // tm, nf // tf),
        in_specs=[pl.BlockSpec((tm, d), lambda i, f: (i, 0)),
                  pl.BlockSpec((1, d), lambda i, f: (0, 0)),
                  pl.BlockSpec((d, tf), lambda i, f: (0, f)),
                  pl.BlockSpec((d, tf), lambda i, f: (0, f)),
                  pl.BlockSpec((tf, d), lambda i, f: (f, 0))],
        out_specs=pl.BlockSpec((tm, d), lambda i, f: (i, 0)),
        out_shape=jax.ShapeDtypeStruct((m, d), F32),
        scratch_shapes=[pltpu.VMEM((tm, d), BF16), pltpu.VMEM((tm, d), F32)],
        compiler_params=_cparams(("parallel", "arbitrary")),
        name="ffn",
    )(h, g.reshape(1, d), w1, w3, w2)


GATE_LANES = 128
MOE_CHUNK = 256
MOE_ROWS = 128
MOE_TOKENS = 2048


def _router_kernel(h_ref, g_ref, wr_ref, hn_ref, gate_ref, cnt_ref):
    hn = _rms(h_ref[...], g_ref[...])
    hn_ref[...] = hn.astype(BF16)
    tm = hn.shape[0]
    lane = lax.broadcasted_iota(jnp.int32, (tm, GATE_LANES), 1)
    logits = jnp.where(lane < N_EXPERTS, _dot_f32(hn, wr_ref[...]), -jnp.inf)
    m1 = jnp.max(logits, axis=-1, keepdims=True)
    i1 = jnp.min(jnp.where(logits == m1, lane, GATE_LANES), axis=-1, keepdims=True)
    rest = jnp.where(lane == i1, -jnp.inf, logits)
    m2 = jnp.max(rest, axis=-1, keepdims=True)
    i2 = jnp.min(jnp.where(rest == m2, lane, GATE_LANES), axis=-1, keepdims=True)
    e2 = jnp.exp(m2 - m1)
    g1 = 1.0 / (1.0 + e2)
    gates = jnp.where(lane == i1, g1, 0.0) + jnp.where(lane == i2, e2 * g1, 0.0)
    gates = jnp.where(lane == N_EXPERTS, i1.astype(F32), gates)
    gate_ref[...] = jnp.where(lane == N_EXPERTS + 1, i2.astype(F32), gates)
    sel = jnp.where((lane == i1) | (lane == i2), 1.0, 0.0)
    for c in range(tm // MOE_CHUNK):
        cnt_ref[c] = jnp.sum(sel[c * MOE_CHUNK:(c + 1) * MOE_CHUNK, :], axis=0, keepdims=True)


def route_tokens(h, g, wr, tm=512):
    m, d = h.shape
    tm = min(tm, m)
    assert tm % MOE_CHUNK == 0
    wrp = jnp.zeros((d, GATE_LANES), F32).at[:, :N_EXPERTS].set(wr)
    return pl.pallas_call(
        _router_kernel,
        grid=(m // tm,),
        in_specs=[pl.BlockSpec((tm, d), lambda i: (i, 0)),
                  pl.BlockSpec((1, d), lambda i: (0, 0)),
                  pl.BlockSpec((d, GATE_LANES), lambda i: (0, 0))],
        out_specs=[pl.BlockSpec((tm, d), lambda i: (i, 0)),
                   pl.BlockSpec((tm, GATE_LANES), lambda i: (i, 0)),
                   pl.BlockSpec((tm // MOE_CHUNK, 1, GATE_LANES), lambda i: (i, 0, 0))],
        out_shape=[jax.ShapeDtypeStruct((m, d), BF16), jax.ShapeDtypeStruct((m, GATE_LANES), F32),
                   jax.ShapeDtypeStruct((m // MOE_CHUNK, 1, GATE_LANES), F32)],
        compiler_params=_cparams(("parallel",)),
        name="router",
    )(h, g.reshape(1, d), wrp)


def _moe_kernel(cb_ref, hn_ref, gate_ref, w1_ref, w3_ref, w2_ref, y_ref,
                rank_row, rk_ref, gs_ref, xs_ref, yacc_ref, rc_ref, gc_ref, xg_ref):
    t = pl.program_id(0)
    e = pl.program_id(1)
    f = pl.program_id(2)
    ts, d = hn_ref.shape
    ch, rb = MOE_CHUNK, MOE_ROWS
    grp = 2 * rb
    n_ch = ts // ch
    lane = lax.broadcasted_iota(jnp.int32, (1, GATE_LANES), 1)

    def before(c):
        return cb_ref[(t * (n_ch + 1) + c) * N_EXPERTS + e]

    n_blocks = (before(n_ch) + rb - 1) // rb
    n_groups = (n_blocks + 1) // 2

    @pl.when((e == 0) & (f == 0))
    def _():
        y_ref[...] = jnp.zeros_like(y_ref)
        strict_lower = (lax.broadcasted_iota(jnp.int32, (ch, ch), 1)
                        < lax.broadcasted_iota(jnp.int32, (ch, ch), 0)).astype(BF16)
        lane_f = lane.astype(F32)
        offs = jnp.ones((1, GATE_LANES), F32)
        for c in range(n_ch):
            rows = slice(c * ch, (c + 1) * ch)
            g = gate_ref[rows, :]
            sel = jnp.where((lane_f == g[:, N_EXPERTS:N_EXPERTS + 1])
                            | (lane_f == g[:, N_EXPERTS + 1:N_EXPERTS + 2]), 1.0, 0.0)
            r = _dot(strict_lower, sel.astype(BF16)) + offs
            r = jnp.where(sel > 0.0, r, 0.0)
            rank_row[:, rows] = jnp.transpose(r) - 1.0
            high = jnp.floor(r * (1.0 / 256.0))
            rk_ref[0, rows, :] = high.astype(BF16)
            rk_ref[1, rows, :] = (r - 256.0 * high).astype(BF16)
            for j, part in enumerate(_split_bf16(g, 3)):
                gs_ref[j, rows, :] = part
            offs = offs + jnp.sum(sel, axis=0, keepdims=True)

    @pl.when(f == 0)
    def _():
        pick = (lax.broadcasted_iota(jnp.int32, (GATE_LANES, GATE_LANES), 0) == e).astype(BF16)
        for c in range(n_ch):
            rows = slice(c * ch, (c + 1) * ch)
            rc_ref[rows, :] = 256.0 * _dot(rk_ref[0, rows, :], pick) + _dot(rk_ref[1, rows, :], pick) - 1.0
            gc_ref[rows, :] = (_dot(gs_ref[0, rows, :], pick) + _dot(gs_ref[1, rows, :], pick)
                               + _dot(gs_ref[2, rows, :], pick))

        def gather_group(b, carry):
            r0 = pl.multiple_of(b * grp, grp)
            want = (r0 + lax.broadcasted_iota(jnp.int32, (grp, ch), 0)).astype(F32)
            xg_ref[...] = jnp.zeros_like(xg_ref)
            for c in range(n_ch):
                @pl.when((before(c) < r0 + grp) & (before(c + 1) > r0))
                def _(c=c):
                    ranks = rank_row[pl.ds(e, 1), c * ch:(c + 1) * ch]
                    onehot = jnp.where(ranks == want, 1.0, 0.0).astype(BF16)
                    xg_ref[...] += _dot(onehot, hn_ref[c * ch:(c + 1) * ch, :])
            xs_ref[pl.ds(r0, grp), :] = xg_ref[...].astype(BF16)
            yacc_ref[pl.ds(r0, grp), :] = jnp.zeros((grp, d), F32)
            return carry

        lax.fori_loop(0, n_groups, gather_group, 0)

    def ffn_rows(r0, rows):
        x = xs_ref[pl.ds(r0, rows), :]
        a = _dot(x, w1_ref[0])
        g3 = _dot(x, w3_ref[0])
        yacc_ref[pl.ds(r0, rows), :] += _dot((a * jax.nn.sigmoid(a) * g3).astype(BF16), w2_ref[0])

    n_quads = n_blocks // 4

    def ffn_quad(j, carry):
        ffn_rows(pl.multiple_of(j * (4 * rb), 4 * rb), 4 * rb)
        return carry

    lax.fori_loop(0, n_quads, ffn_quad, 0)
    tail = pl.multiple_of(n_quads * (4 * rb), 4 * rb)

    @pl.when((n_blocks & 2) != 0)
    def _():
        ffn_rows(tail, 2 * rb)

    @pl.when((n_blocks & 1) != 0)
    def _():
        ffn_rows(pl.multiple_of(tail + (n_blocks & 2) * rb, rb), rb)

    @pl.when(f == pl.num_programs(2) - 1)
    def _():
        def scatter_group(b, carry):
            r0 = pl.multiple_of(b * grp, grp)
            yb = yacc_ref[pl.ds(r0, grp), :].astype(BF16)
            want = (r0 + lax.broadcasted_iota(jnp.int32, (ch, rb), 1)).astype(F32)
            for c in range(n_ch):
                @pl.when((before(c) < r0 + grp) & (before(c + 1) > r0))
                def _(c=c):
                    rows = slice(c * ch, (c + 1) * ch)
                    rank = rc_ref[rows, :]
                    onehot = jnp.concatenate([jnp.where(rank == want, 1.0, 0.0),
                                              jnp.where(rank == want + float(rb), 1.0, 0.0)],
                                             axis=1).astype(BF16)
                    gate = jnp.concatenate([gc_ref[rows, :]] * (d // GATE_LANES), axis=1)
                    y_ref[rows, :] += gate * _dot(onehot, yb)
            return carry

        lax.fori_loop(0, n_groups, scatter_group, 0)


def moe(hn, gates, cnt, w1, w3, w2, tf=896):
    m, d = hn.shape
    ts = min(MOE_TOKENS, m)
    n_exp, _, nf = w1.shape
    assert nf % tf == 0 and m % ts == 0 and ts % MOE_CHUNK == 0 and MOE_ROWS == GATE_LANES
    n_ch = ts // MOE_CHUNK
    counts = cnt.reshape(m // ts, n_ch, GATE_LANES)[:, :, :n_exp].astype(jnp.int32)
    before = jnp.concatenate([jnp.zeros((m // ts, 1, n_exp), jnp.int32), jnp.cumsum(counts, axis=1)], axis=1)
    grid_spec = pltpu.PrefetchScalarGridSpec(
        num_scalar_prefetch=1,
        grid=(m // ts, n_exp, nf // tf),
        in_specs=[pl.BlockSpec((ts, d), lambda t, e, f, cb: (t, 0), pipeline_mode=pl.Buffered(1)),
                  pl.BlockSpec((ts, GATE_LANES), lambda t, e, f, cb: (t, 0), pipeline_mode=pl.Buffered(1)),
                  pl.BlockSpec((1, d, tf), lambda t, e, f, cb: (e, 0, f)),
                  pl.BlockSpec((1, d, tf), lambda t, e, f, cb: (e, 0, f)),
                  pl.BlockSpec((1, tf, d), lambda t, e, f, cb: (e, f, 0))],
        out_specs=pl.BlockSpec((ts, d), lambda t, e, f, cb: (t, 0), pipeline_mode=pl.Buffered(1)),
        scratch_shapes=[pltpu.VMEM((GATE_LANES, ts), F32),
                        pltpu.VMEM((2, ts, GATE_LANES), BF16),
                        pltpu.VMEM((3, ts, GATE_LANES), BF16),
                        pltpu.VMEM((ts, d), BF16),
                        pltpu.VMEM((ts, d), F32),
                        pltpu.VMEM((ts, MOE_ROWS), F32),
                        pltpu.VMEM((ts, GATE_LANES), F32),
                        pltpu.VMEM((2 * MOE_ROWS, d), F32)])
    return pl.pallas_call(
        _moe_kernel,
        grid_spec=grid_spec,
        out_shape=jax.ShapeDtypeStruct((m, d), F32),
        compiler_params=_cparams(("parallel", "arbitrary", "arbitrary")),
        name="moe",
    )(before.reshape(-1), hn, gates, w1, w3, w2)


def moe_block(h, g, wr, w1, w3, w2):
    hn, gates, cnt = route_tokens(h, g, wr)
    return moe(hn, gates, cnt, w1.astype(BF16), w3.astype(BF16), w2.astype(BF16))


def _final_norm_kernel(g_ref, *refs):
    *x_refs, o_ref = refs
    x = x_refs[0][...]
    for r in x_refs[1:]:
        x = x + r[...]
    o_ref[...] = _rms(x, g_ref[...])


def final_rmsnorm(g, *xs, tm=1024):
    m, d = xs[0].shape
    tm = min(tm, m)
    return pl.pallas_call(
        _final_norm_kernel,
        grid=(m // tm,),
        in_specs=[pl.BlockSpec((1, d), lambda i: (0, 0))] + [pl.BlockSpec((tm, d), lambda i: (i, 0))] * len(xs),
        out_specs=pl.BlockSpec((tm, d), lambda i: (i, 0)),
        out_shape=jax.ShapeDtypeStruct((m, d), F32),
        compiler_params=_cparams(("parallel",)),
        name="final_norm",
    )(g.reshape(1, d), *xs)


def _pad_cols(w, width):
    return jnp.pad(w, ((0, 0), (0, width - w.shape[1])))


def _in_proj_weight(w_in):
    a_end = 2 * GLA_HEADS * GLA_DK + 2 * GROUP_WIDTH + GLA_RANK
    b_end = a_end + GROUP_WIDTH
    c_end = b_end + 3 * GROUP_WIDTH + IDX_HEADS * IDX_DIM + IDX_DIM + IDX_HEADS
    assert w_in.shape[1] == c_end + 2 * GROUP_WIDTH
    return jnp.concatenate([_pad_cols(w_in[:, :a_end], A_WIDTH), w_in[:, a_end:b_end],
                            _pad_cols(w_in[:, b_end:c_end], C_WIDTH), w_in[:, c_end:]], axis=1).astype(BF16)


def hybrid_layer(h, mem2, toe, p, bsz, t):
    m, d = h.shape
    z_a, z_b, z_c, z_d = norm_matmul(h, p["norm_mix"], _in_proj_weight(p["w_in"]),
                                     (A_WIDTH, B_WIDTH, C_WIDTH, D_WIDTH))
    o_a = gla_mixer(z_a.reshape(bsz, t, A_WIDTH), p["gla_wa2"], p["gla_ba"], p["gla_norm"])
    o_b = pool_mixer(z_b.reshape(bsz, t, B_WIDTH), p["pool_w"], p["pool_b"].reshape(-1), p["pool_scale"])
    o_c = dsa_mixer(z_c.reshape(bsz, t, C_WIDTH), toe)
    o_d = sgu_mixer(z_d.reshape(bsz, t, D_WIDTH), p["sgu_ln_g"], p["sgu_ln_b"], p["sgu_w"], p["sgu_b"])
    h = out_proj(h, [o.reshape(m, GROUP_WIDTH) for o in (o_a, o_b, o_c, o_d)], p["w_out"].astype(BF16))

    hd = XA_HEADS * XA_DH
    k, v = norm_matmul(mem2, p["norm_mem"], p["xa_wkv"].astype(BF16), (hd, hd))
    mlen = mem2.shape[0] // bsz
    h = cross_attn(h.reshape(bsz, t, d), p["norm_xa"], p["xa_wq"].astype(BF16),
                   k.reshape(bsz, mlen, hd), v.reshape(bsz, mlen, hd), p["xa_wo"].astype(BF16))
    return h.reshape(m, d)


def kernel(x, mem, rel_bias, final_norm, norm_mix, w_in, gla_wa2, gla_ba, gla_norm, pool_w, pool_b,
           pool_scale, sgu_ln_g, sgu_ln_b, sgu_w, sgu_b, w_out, norm_xa, norm_mem, xa_wq, xa_wkv, xa_wo,
           norm_ffn, ffn_w1, ffn_w3, ffn_w2, router, moe_w1, moe_w3, moe_w2):
    bsz, t, d = x.shape
    depth = norm_mix.shape[0]
    h = x.reshape(bsz * t, d)
    mem2 = mem.reshape(-1, d)
    toe = rel_bias_tables(rel_bias)
    pending = None
    for i in range(depth):
        if pending is not None:
            h, pending = h + pending, None
        p = dict(norm_mix=norm_mix[i], w_in=w_in[i], gla_wa2=gla_wa2[i], gla_ba=gla_ba[i],
                 gla_norm=gla_norm[i], pool_w=pool_w[i], pool_b=pool_b[i], pool_scale=pool_scale[i],
                 sgu_ln_g=sgu_ln_g[i], sgu_ln_b=sgu_ln_b[i], sgu_w=sgu_w[i], sgu_b=sgu_b[i],
                 w_out=w_out[i], norm_xa=norm_xa[i], norm_mem=norm_mem[i], xa_wq=xa_wq[i],
                 xa_wkv=xa_wkv[i], xa_wo=xa_wo[i])
        h = hybrid_layer(h, mem2, toe, p, bsz, t)
        j = i // 2
        if i % 2 == 0:
            h = ffn(h, norm_ffn[i], ffn_w1[j].astype(BF16), ffn_w3[j].astype(BF16), ffn_w2[j].astype(BF16))
        else:
            pending = moe_block(h, norm_ffn[i], router[j], moe_w1[j], moe_w3[j], moe_w2[j])
    xs = (h,) if pending is None else (h, pending)
    return final_rmsnorm(final_norm, *xs).reshape(bsz, t, d)
```

```python
import functools
import math

import jax
import jax.numpy as jnp
import numpy as np
from jax import lax
from jax.experimental import pallas as pl
from jax.experimental.pallas import tpu as pltpu

F32 = jnp.float32
BF16 = jnp.bfloat16
EPS = 1e-6

GROUP_WIDTH = 256

GLA_HEADS = 4
GLA_DV = 64
GLA_DK = 32
GLA_RANK = 16
GLA_TAU = 16.0
GLA_CHUNK = 64
GLA_GROUP = 8

POOL_WINDOWS = (2, 4, 8, 16)
POOL_CG = 64

DSA_HEADS = 4
DSA_DH = 64
IDX_HEADS = 8
IDX_DIM = 32
DSA_TOPK_MAX = 256
DSA_BLOCK = 256
DSA_SUB = 64

SGU_GROUPS = 4
SGU_CHUNK = 128
SGU_CG = 64

REL_BUCKETS = 32
REL_MAX_DIST = 128

XA_HEADS = 4
XA_DH = 64

N_EXPERTS = 8

A_WIDTH = 896
B_WIDTH = 256
C_MAIN = 1024
C_AUX = 128
D_WIDTH = 512

INT_MIN = -(2 ** 31)
NEG_BIG = -1e30
VMEM_LIMIT = 56 * 1024 * 1024


def _cparams(sem):
    return pltpu.CompilerParams(dimension_semantics=sem, vmem_limit_bytes=VMEM_LIMIT)


def _dot(a, b):
    return jnp.dot(a, b, preferred_element_type=F32)


def _dot_nt(a, b):
    return lax.dot_general(a, b, (((1,), (1,)), ((), ())), preferred_element_type=F32)


def _dot_tn(a, b):
    return lax.dot_general(a, b, (((0,), (0,)), ((), ())), preferred_element_type=F32)


def _dot_f32(a, b):
    return jnp.dot(a, b, preferred_element_type=F32, precision=lax.Precision.HIGHEST)


def _split_bf16(x, terms):
    parts = []
    for _ in range(terms):
        p = x.astype(BF16)
        parts.append(p)
        x = x - p.astype(F32)
    return parts


def _dot_exact_rhs(a, b, terms):
    out = None
    for p in _split_bf16(a, terms):
        d = _dot(p, b)
        out = d if out is None else out + d
    return out


def _dot_exact_lhs(a, b, terms):
    out = None
    for p in _split_bf16(b, terms):
        d = _dot(a, p)
        out = d if out is None else out + d
    return out


def _dot_3pass(a, b):
    a_hi, a_lo = _split_bf16(a, 2)
    b_hi, b_lo = _split_bf16(b, 2)
    return _dot(a_hi, b_hi) + (_dot(a_hi, b_lo) + _dot(a_lo, b_hi))


def _rms(x, g):
    return x * lax.rsqrt(jnp.mean(x * x, axis=-1, keepdims=True) + EPS) * g


def _norm_matmul_kernel(h_ref, g_ref, w_ref, *out_refs, widths):
    hb = _rms(h_ref[...], g_ref[...]).astype(BF16)
    off = 0
    for o_ref, wd in zip(out_refs, widths):
        o_ref[...] = _dot(hb, w_ref[:, off:off + wd]).astype(o_ref.dtype)
        off += wd


def norm_matmul(h, g, w, widths, dtypes=None, tm=512):
    dtypes = dtypes or (F32,) * len(widths)
    m, d = h.shape
    tm = min(tm, m)
    n = sum(widths)
    return pl.pallas_call(
        functools.partial(_norm_matmul_kernel, widths=widths),
        grid=(m // tm,),
        in_specs=[pl.BlockSpec((tm, d), lambda i: (i, 0)),
                  pl.BlockSpec((1, d), lambda i: (0, 0)),
                  pl.BlockSpec((d, n), lambda i: (0, 0))],
        out_specs=[pl.BlockSpec((tm, wd), lambda i: (i, 0)) for wd in widths],
        out_shape=[jax.ShapeDtypeStruct((m, wd), dt) for wd, dt in zip(widths, dtypes)],
        compiler_params=_cparams(("parallel",)),
        name="norm_matmul",
    )(h, g.reshape(1, d), w)


def _log_sigmoid(x):
    return jnp.minimum(x, 0.0) - jnp.log1p(jnp.exp(-jnp.abs(x)))


def _gla_kernel(z_ref, wa2_ref, ba_ref, ng_ref, o_ref, s_ref, *, n_groups):
    c = GLA_CHUNK
    grp = GLA_GROUP * c
    hk = GLA_HEADS * GLA_DK
    hv = GLA_HEADS * GLA_DV
    s_ref[...] = jnp.zeros_like(s_ref)

    head_k = lax.broadcasted_iota(jnp.int32, (1, hk), 1) // GLA_DK
    head_v = lax.broadcasted_iota(jnp.int32, (1, hv), 1) // GLA_DV
    g_row = lax.broadcasted_iota(jnp.int32, (grp, grp), 0)
    g_col = lax.broadcasted_iota(jnp.int32, (grp, grp), 1)
    tril = (((g_row // c) == (g_col // c)) & (g_col <= g_row)).astype(BF16)
    causal4 = (lax.broadcasted_iota(jnp.int32, (GLA_HEADS * c, c), 1)
               <= lax.broadcasted_iota(jnp.int32, (GLA_HEADS * c, c), 0) % c)
    state_mask = (lax.broadcasted_iota(jnp.int32, (hk, hv), 0) // GLA_DK
                  == lax.broadcasted_iota(jnp.int32, (hk, hv), 1) // GLA_DV)
    norm_mat = jnp.where(lax.broadcasted_iota(jnp.int32, (hv, hv), 0) // GLA_DV
                         == lax.broadcasted_iota(jnp.int32, (hv, hv), 1) // GLA_DV,
                         1.0 / GLA_DV, 0.0).astype(BF16)
    wa2 = wa2_ref[...]
    ba = ba_ref[...]
    ng = ng_ref[...]

    def body(n, carry):
        r0 = pl.multiple_of(n * grp, grp)
        z = z_ref[0, pl.ds(r0, grp), :]
        q, k, v, g, lr = z[:, 0:128], z[:, 128:256], z[:, 256:512], z[:, 512:768], z[:, 768:896]
        log_a = _log_sigmoid(_dot_f32(lr, wa2) + ba) / GLA_TAU
        b = _dot_exact_lhs(tril, log_a, 3)
        b_end = jnp.concatenate([jnp.broadcast_to(b[(ci + 1) * c - 1:(ci + 1) * c, :], (c, hk))
                                 for ci in range(GLA_GROUP)], axis=0)
        q_t = q * (GLA_DK ** -0.5) * jnp.exp(b)
        q_tb = q_t.astype(BF16)
        k_t = (k * jnp.exp(-b)).astype(BF16)
        k_dec = (k * jnp.exp(b_end - b)).astype(BF16)
        vb = v.astype(BF16)
        outs = []
        for ci in range(GLA_GROUP):
            rows = slice(ci * c, (ci + 1) * c)
            q4 = jnp.concatenate([jnp.where(head_k == h, q_t[rows], 0.0) for h in range(GLA_HEADS)],
                                 axis=0).astype(BF16)
            att = jnp.where(causal4, _dot_nt(q4, k_t[rows]), 0.0)
            r = _dot(att.astype(BF16), vb[rows])
            o = _dot(q_tb[rows], s_ref[...].astype(BF16))
            for h in range(GLA_HEADS):
                o = o + jnp.where(head_v == h, r[h * c:(h + 1) * c, :], 0.0)
            outs.append(o)
            kv = jnp.where(state_mask, _dot_tn(k_dec[rows], vb[rows]), 0.0)
            last = b_end[ci * c:ci * c + 1, :]
            dec = jnp.exp(jnp.transpose(jnp.broadcast_to(last, (hk, hk))))
            s_ref[...] = s_ref[...] * jnp.concatenate([dec, dec], axis=1) + kv
        o = jnp.concatenate(outs, axis=0)
        o = o * lax.rsqrt(_dot_exact_rhs(o * o, norm_mat, 2) + EPS) * ng
        o_ref[0, pl.ds(r0, grp), :] = o * (g * jax.nn.sigmoid(g))
        return carry

    lax.fori_loop(0, n_groups, body, 0)


def gla_mixer(z_a, wa2, ba, norm_g):
    bsz, t, _ = z_a.shape
    hk = GLA_HEADS * GLA_DK
    wa2p = jnp.zeros((128, hk), F32).at[:GLA_RANK].set(wa2)
    return pl.pallas_call(
        functools.partial(_gla_kernel, n_groups=t // (GLA_GROUP * GLA_CHUNK)),
        grid=(bsz,),
        in_specs=[pl.BlockSpec((1, t, A_WIDTH), lambda b: (b, 0, 0)),
                  pl.BlockSpec((128, hk), lambda b: (0, 0)),
                  pl.BlockSpec((1, hk), lambda b: (0, 0)),
                  pl.BlockSpec((1, GROUP_WIDTH), lambda b: (0, 0))],
        out_specs=pl.BlockSpec((1, t, GROUP_WIDTH), lambda b: (b, 0, 0)),
        out_shape=jax.ShapeDtypeStruct((bsz, t, GROUP_WIDTH), F32),
        scratch_shapes=[pltpu.VMEM((hk, GROUP_WIDTH), F32)],
        compiler_params=_cparams(("parallel",)),
        name="gla_mixer",
    )(z_a, wa2p, ba.reshape(1, hk), norm_g.reshape(1, GROUP_WIDTH))


def _pool_kernel(u_ref, w_ref, b_ref, sc_ref, o_ref):
    u = u_ref[0]
    t, gw = u.shape
    row = lax.broadcasted_iota(jnp.int32, (t, gw), 0)
    grp = lax.broadcasted_iota(jnp.int32, (t, gw), 1) // POOL_CG

    def shifted(x, k):
        return jnp.where(row >= k, pltpu.roll(x, k, axis=0), 0.0)

    s = u
    p = jnp.zeros_like(u)
    for gi, win in enumerate(POOL_WINDOWS):
        half = win // 2
        s = s + shifted(s, half)
        cnt = jnp.minimum(row + 1, win).astype(F32)
        p = jnp.where(grp == gi, s / cnt - u, p)
    y = _dot(p.astype(BF16), w_ref[...]) + b_ref[...]
    o_ref[0] = y * sc_ref[...]


def pool_mixer(z_b, w, b, scale):
    assert POOL_WINDOWS == (2, 4, 8, 16)
    bsz, t, gw = z_b.shape
    w_bd = jnp.zeros((gw, gw), F32)
    for gi in range(len(POOL_WINDOWS)):
        w_bd = w_bd.at[gi * POOL_CG:(gi + 1) * POOL_CG, gi * POOL_CG:(gi + 1) * POOL_CG].set(w[gi])
    return pl.pallas_call(
        _pool_kernel,
        grid=(bsz,),
        in_specs=[pl.BlockSpec((1, t, gw), lambda i: (i, 0, 0)),
                  pl.BlockSpec((gw, gw), lambda i: (0, 0)),
                  pl.BlockSpec((1, gw), lambda i: (0, 0)),
                  pl.BlockSpec((1, gw), lambda i: (0, 0))],
        out_specs=pl.BlockSpec((1, t, gw), lambda i: (i, 0, 0)),
        out_shape=jax.ShapeDtypeStruct((bsz, t, gw), F32),
        compiler_params=_cparams(("parallel",)),
        name="pool_mixer",
    )(z_b, w_bd.astype(BF16), b.reshape(1, gw), scale.reshape(1, gw))


def _sgu_kernel(z_ref, lg_ref, lb_ref, w_ref, bm_ref, o_ref, *, chunks):
    c = SGU_CHUNK
    gw = GROUP_WIDTH
    rows = SGU_GROUPS * c
    tri = (lax.broadcasted_iota(jnp.int32, (rows, c), 1)
           <= lax.broadcasted_iota(jnp.int32, (rows, c), 0) % c)
    ws = jnp.where(tri, w_ref[...], 0.0).astype(BF16)
    grp = lax.broadcasted_iota(jnp.int32, (1, gw), 1) // SGU_CG
    for ci in range(chunks):
        z = jax.nn.gelu(z_ref[0, ci * c:(ci + 1) * c, :], approximate=True)
        u, v = z[:, :gw], z[:, gw:]
        mu = jnp.mean(v, axis=-1, keepdims=True)
        var = jnp.mean(jnp.square(v - mu), axis=-1, keepdims=True)
        vn = (v - mu) * lax.rsqrt(var + EPS) * lg_ref[...] + lb_ref[...]
        r = _dot(ws, vn.astype(BF16))
        mixed = bm_ref[...]
        for g in range(SGU_GROUPS):
            mixed = mixed + jnp.where(grp == g, r[g * c:(g + 1) * c, :], 0.0)
        o_ref[0, ci * c:(ci + 1) * c, :] = u * mixed


def sgu_mixer(z_d, ln_g, ln_b, w_s, b_s, chunks=4):
    bsz, t, _ = z_d.shape
    gw = GROUP_WIDTH
    tt = chunks * SGU_CHUNK
    bias = jnp.repeat(b_s.T, SGU_CG, axis=1)
    return pl.pallas_call(
        functools.partial(_sgu_kernel, chunks=chunks),
        grid=(bsz, t // tt),
        in_specs=[pl.BlockSpec((1, tt, 2 * gw), lambda b, i: (b, i, 0)),
                  pl.BlockSpec((1, gw), lambda b, i: (0, 0)),
                  pl.BlockSpec((1, gw), lambda b, i: (0, 0)),
                  pl.BlockSpec((SGU_GROUPS * SGU_CHUNK, SGU_CHUNK), lambda b, i: (0, 0)),
                  pl.BlockSpec((SGU_CHUNK, gw), lambda b, i: (0, 0))],
        out_specs=pl.BlockSpec((1, tt, gw), lambda b, i: (b, i, 0)),
        out_shape=jax.ShapeDtypeStruct((bsz, t, gw), F32),
        compiler_params=_cparams(("parallel", "parallel")),
        name="sgu_mixer",
    )(z_d, ln_g.reshape(1, gw), ln_b.reshape(1, gw),
      w_s.reshape(SGU_GROUPS * SGU_CHUNK, SGU_CHUNK), bias)


def _bucket_table():
    assert REL_MAX_DIST <= DSA_BLOCK + 1
    s = np.arange(DSA_BLOCK)[:, None]
    t = np.arange(DSA_BLOCK)[None, :]
    dist = np.stack([t - s, DSA_BLOCK + t - s, 2 * DSA_BLOCK + t - s])
    n = np.maximum(dist, 0)
    max_exact = REL_BUCKETS // 2
    nf = np.maximum(n, 1).astype(np.float32)
    large = max_exact + (np.log(nf / np.float32(max_exact)) / np.float32(math.log(REL_MAX_DIST / max_exact))
                         * np.float32(REL_BUCKETS - max_exact)).astype(np.int32)
    return np.where(n < max_exact, n, np.minimum(large, REL_BUCKETS - 1)).astype(np.int32)


def _bias_table_kernel(rb_ref, bucket_ref, o_ref):
    for back in range(3):
        bucket = bucket_ref[back]
        for h in range(DSA_HEADS):
            acc = jnp.zeros(bucket.shape, F32)
            for b in range(REL_BUCKETS):
                acc = jnp.where(bucket == b, rb_ref[b * DSA_HEADS + h], acc)
            o_ref[back, h] = acc


def rel_bias_tables(rel_bias):
    blk = DSA_BLOCK
    return pl.pallas_call(
        _bias_table_kernel,
        in_specs=[pl.BlockSpec(memory_space=pltpu.SMEM),
                  pl.BlockSpec((3, blk, blk), lambda: (0, 0, 0))],
        out_specs=pl.BlockSpec((3, DSA_HEADS, blk, blk), lambda: (0, 0, 0, 0)),
        out_shape=jax.ShapeDtypeStruct((3, DSA_HEADS, blk, blk), F32),
        name="rel_bias_tables",
    )(rel_bias.reshape(-1), jnp.asarray(_bucket_table()))


def _dsa_kernel(q_ref, k_ref, v_ref, qi_ref, kw_ref, qw_ref, toe_ref, o_ref,
                kpl_ref, vt_ref, keys_ref, khi_ref, klo_ref, acc_ref, am_ref, lg_ref, p_ref,
                *, topk, n_blocks, idx_bits):
    blk = DSA_BLOCK
    sub = DSA_SUB
    i = pl.program_id(1)
    hd = DSA_HEADS * DSA_DH
    heads_per_half = 128 // IDX_DIM

    @pl.when(i == 0)
    def _():
        lane = lax.broadcasted_iota(jnp.int32, (blk, 128), 1)

        def build(kb, c):
            r0 = pl.multiple_of(kb * blk, blk)
            ki = jnp.where(lane < IDX_DIM, kw_ref[0, pl.ds(r0, blk), :], 0.0)
            for j in range(heads_per_half):
                kpl_ref[kb, j] = (ki if j == 0 else pltpu.roll(ki, j * IDX_DIM, axis=1)).astype(BF16)
            vt_ref[kb] = jnp.transpose(v_ref[0, pl.ds(r0, blk), :].astype(F32)).astype(BF16)
            return c

        lax.fori_loop(0, n_blocks, build, 0)

    s_loc = lax.broadcasted_iota(jnp.int32, (blk, blk), 0)
    t_loc = lax.broadcasted_iota(jnp.int32, (blk, blk), 1)
    n_vis = i + 1

    qi_t = jnp.transpose(qi_ref[0].astype(F32)).astype(BF16)
    qi_halves = [qi_t[:128, :], qi_t[128:, :]]
    w_t = jnp.transpose(qw_ref[0])
    w_rows = [w_t[IDX_DIM + h:IDX_DIM + h + 1, :] * (IDX_HEADS ** -0.5) * (IDX_DIM ** -0.5)
              for h in range(IDX_HEADS)]
    s_sub = lax.broadcasted_iota(jnp.int32, (sub, blk), 0)
    t_sub = lax.broadcasted_iota(jnp.int32, (sub, blk), 1)

    def score_body(kb, c):
        for ci in range(blk // sub):
            rows = slice(ci * sub, (ci + 1) * sub)
            sc = jnp.zeros((sub, blk), F32)
            for half in range(2):
                for j in range(heads_per_half):
                    d = _dot(kpl_ref[kb, j, rows, :], qi_halves[half])
                    sc = sc + jnp.maximum(d, 0.0) * w_rows[half * heads_per_half + j]
            sc = jnp.where(sc == 0.0, 0.0, sc)
            bits = pltpu.bitcast(sc, jnp.int32)
            key = jnp.where(bits < 0, bits ^ jnp.int32(0x7FFFFFFF), bits)
            vis = (kb < i) | (s_sub + ci * sub <= t_sub)
            key = jnp.where(vis, key, jnp.int32(INT_MIN))
            keys_ref[kb, rows, :] = key
            khi_ref[kb, rows, :] = lax.shift_right_arithmetic(key, 16).astype(jnp.int16)
            klo_ref[kb, rows, :] = ((key & 0xFFFF) - 2 ** 15).astype(jnp.int16)
        return c

    lax.fori_loop(0, n_vis, score_body, 0)

    def count(pred):
        def body(kb, acc):
            hit = jnp.where(pred(keys_ref[kb], kb), 1.0, 0.0)
            return acc + jnp.sum(hit.reshape(blk // 32, 32, blk), axis=0)
        acc = lax.fori_loop(0, n_vis, body, jnp.zeros((32, blk), F32))
        return jnp.sum(acc, axis=0, keepdims=True)

    def count16(ref, cand):
        cand = cand.astype(jnp.int16)

        def body(kb, acc):
            hit = jnp.where(ref[kb] >= cand, jnp.int16(1), jnp.int16(0))
            for j in range(blk // 32):
                acc = acc + hit[j * 32:(j + 1) * 32]
            return acc
        acc = lax.fori_loop(0, n_vis, body, jnp.zeros((32, blk), jnp.int16))
        return jnp.sum(acc.astype(jnp.int32).astype(F32), axis=0, keepdims=True)

    def search16(ref, offset):
        lowest = jnp.full((1, blk), -(2 ** 15), jnp.int32)
        base = jnp.where(offset + count16(ref, jnp.zeros((1, blk), jnp.int32)) >= kf, 0, lowest)

        def bit_body(it, base):
            cand = base | lax.shift_left(jnp.int32(1), 14 - it)
            return jnp.where(offset + count16(ref, cand) >= kf, cand, base)
        return lax.fori_loop(0, 15, bit_body, base)

    kf = float(topk)
    zero = jnp.zeros((1, blk), jnp.int32)
    thr_hi = search16(khi_ref, 0.0)
    thr_hi16 = thr_hi.astype(jnp.int16)

    def low_body(kb, above):
        hi = khi_ref[kb]
        klo_ref[kb] = jnp.where(hi == thr_hi16, klo_ref[kb], jnp.int16(-(2 ** 15)))
        hit = jnp.where(hi > thr_hi16, jnp.int16(1), jnp.int16(0))
        for j in range(blk // 32):
            above = above + hit[j * 32:(j + 1) * 32]
        return above

    above = lax.fori_loop(0, n_vis, low_body, jnp.zeros((32, blk), jnp.int16))
    n_above = jnp.sum(above.astype(jnp.int32).astype(F32), axis=0, keepdims=True)
    thr_lo = search16(klo_ref, n_above)
    thr = lax.shift_left(thr_hi, 16) | (thr_lo + 2 ** 15)
    thr_sel = jnp.maximum(thr, jnp.int32(INT_MIN + 1))

    n_gt = count(lambda key, kb: key > thr)
    n_eq = count(lambda key, kb: key == thr)
    need = kf - n_gt
    excess = jnp.where((n_eq > need) & (thr > jnp.int32(INT_MIN)), 1.0, 0.0)

    def tie_search():
        def tie_body(it, j):
            cand = j | lax.shift_left(jnp.int32(1), idx_bits - 1 - it)
            below = count(lambda key, kb: (key == thr) & (kb * blk + s_loc < cand))
            return jnp.where(below < need, cand, j)
        return lax.fori_loop(0, idx_bits, tie_body, zero)

    last = lax.cond(jnp.max(excess) > 0.0, tie_search, lambda: jnp.full((1, blk), 2 ** 30, jnp.int32))

    assert DSA_DH ** -0.5 == 0.125
    q_t = jnp.transpose(q_ref[0].astype(F32) * (DSA_DH ** -0.5))
    row_h = lax.broadcasted_iota(jnp.int32, (hd, 1), 0) // DSA_DH
    q_heads = [jnp.where(row_h == h, q_t, 0.0).astype(BF16) for h in range(DSA_HEADS)]
    acc_ref[...] = jnp.zeros_like(acc_ref)
    n_sub = blk // sub

    def att_body(kb, carry):
        ms, ls = carry
        back = jnp.minimum(i - kb, 2)
        for ci in range(n_sub):
            rows = slice(ci * sub, (ci + 1) * sub)
            key = keys_ref[kb, rows, :]
            sel = (key > thr_sel) | ((key == thr_sel) & (kb * blk + ci * sub + s_sub <= last))
            am_ref[rows, :] = jnp.where(sel, 0.0, NEG_BIG)
        new_ms, alphas = [], []
        for h in range(DSA_HEADS):
            pm = jnp.full((8, blk), NEG_BIG, F32)
            for ci in range(n_sub):
                rows = slice(ci * sub, (ci + 1) * sub)
                k_rows = k_ref[0, pl.ds(pl.multiple_of(kb * blk + ci * sub, sub), sub), :]
                lg = _dot(k_rows, q_heads[h]) + toe_ref[back, h, rows, :] + am_ref[rows, :]
                lg_ref[h, rows, :] = lg
                pm = jnp.maximum(pm, jnp.max(lg.reshape(sub // 8, 8, blk), axis=0))
            m_new = jnp.maximum(ms[h], jnp.max(pm, axis=0, keepdims=True))
            new_ms.append(m_new)
            alphas.append(jnp.exp(ms[h] - m_new))
        new_ls = []
        for h in range(DSA_HEADS):
            ps = jnp.zeros((8, blk), F32)
            for ci in range(n_sub):
                rows = slice(ci * sub, (ci + 1) * sub)
                p = jnp.exp(lg_ref[h, rows, :] - new_ms[h])
                ps = ps + jnp.sum(p.reshape(sub // 8, 8, blk), axis=0)
                p_ref[h, rows, :] = p.astype(BF16)
            new_ls.append(ls[h] * alphas[h] + jnp.sum(ps, axis=0, keepdims=True))
        for h in range(DSA_HEADS):
            hrows = slice(h * DSA_DH, (h + 1) * DSA_DH)
            acc_ref[hrows, :] = acc_ref[hrows, :] * alphas[h] + _dot(vt_ref[kb, hrows, :], p_ref[h])
        return tuple(new_ms), tuple(new_ls)

    init = (tuple(jnp.full((1, blk), 0.01 * NEG_BIG, F32) for _ in range(DSA_HEADS)),
            tuple(jnp.zeros((1, blk), F32) for _ in range(DSA_HEADS)))
    _, ls = lax.fori_loop(0, n_vis, att_body, init)
    for h in range(DSA_HEADS):
        rows = slice(h * DSA_DH, (h + 1) * DSA_DH)
        acc_ref[rows, :] = acc_ref[rows, :] * (1.0 / ls[h])
    o_ref[0] = jnp.transpose(acc_ref[...])


def dsa_mixer(z_c, z_aux, toe):
    bsz, t, _ = z_c.shape
    blk = DSA_BLOCK
    n_blocks = t // blk
    assert t % blk == 0
    topk = min(DSA_TOPK_MAX, t // 4)
    hd = DSA_HEADS * DSA_DH
    kernel = functools.partial(_dsa_kernel, topk=topk, n_blocks=n_blocks,
                               idx_bits=max(1, (t - 1).bit_length()))
    return pl.pallas_call(
        kernel,
        grid=(bsz, n_blocks),
        in_specs=[pl.BlockSpec((1, blk, hd), lambda b, i: (b, i, 0)),
                  pl.BlockSpec((1, t, hd), lambda b, i: (b, 0, 1)),
                  pl.BlockSpec((1, t, hd), lambda b, i: (b, 0, 2)),
                  pl.BlockSpec((1, blk, hd), lambda b, i: (b, i, 3)),
                  pl.BlockSpec((1, t, C_AUX), lambda b, i: (b, 0, 0)),
                  pl.BlockSpec((1, blk, C_AUX), lambda b, i: (b, i, 0)),
                  pl.BlockSpec((3, DSA_HEADS, blk, blk), lambda b, i: (0, 0, 0, 0))],
        out_specs=pl.BlockSpec((1, blk, hd), lambda b, i: (b, i, 0)),
        out_shape=jax.ShapeDtypeStruct((bsz, t, hd), F32),
        scratch_shapes=[pltpu.VMEM((n_blocks, 128 // IDX_DIM, blk, 128), BF16),
                        pltpu.VMEM((n_blocks, hd, blk), BF16),
                        pltpu.VMEM((n_blocks, blk, blk), jnp.int32),
                        pltpu.VMEM((n_blocks, blk, blk), jnp.int16),
                        pltpu.VMEM((n_blocks, blk, blk), jnp.int16),
                        pltpu.VMEM((hd, blk), F32),
                        pltpu.VMEM((blk, blk), F32),
                        pltpu.VMEM((DSA_HEADS, blk, blk), F32),
                        pltpu.VMEM((DSA_HEADS, blk, blk), BF16)],
        compiler_params=_cparams(("parallel", "arbitrary")),
        name="dsa_mixer",
    )(z_c, z_c, z_c, z_c, z_aux, z_aux, toe)


def _out_proj_kernel(h_ref, a_ref, b_ref, c_ref, d_ref, w_ref, o_ref):
    gw = GROUP_WIDTH
    acc = h_ref[...]
    for gi, r in enumerate((a_ref, b_ref, c_ref, d_ref)):
        acc = acc + _dot(r[...].astype(BF16), w_ref[gi * gw:(gi + 1) * gw, :])
    o_ref[...] = acc


def out_proj(h, outs, w, tm=1024):
    m, d = h.shape
    tm = min(tm, m)
    gw = GROUP_WIDTH
    return pl.pallas_call(
        _out_proj_kernel,
        grid=(m // tm,),
        in_specs=[pl.BlockSpec((tm, d), lambda i: (i, 0))]
                 + [pl.BlockSpec((tm, gw), lambda i: (i, 0))] * 4
                 + [pl.BlockSpec((4 * gw, d), lambda i: (0, 0))],
        out_specs=pl.BlockSpec((tm, d), lambda i: (i, 0)),
        out_shape=jax.ShapeDtypeStruct((m, d), F32),
        compiler_params=_cparams(("parallel",)),
        name="out_proj",
    )(h, *outs, w)


def _xattn_kernel(h_ref, g_ref, wq_ref, k_ref, v_ref, wo_ref, o_ref):
    x = h_ref[0]
    hd = XA_HEADS * XA_DH
    q = _dot(_rms(x, g_ref[...]).astype(BF16), wq_ref[...]).astype(BF16)
    k = k_ref[0]
    v = v_ref[0]
    lane_h = lax.broadcasted_iota(jnp.int32, (1, hd), 1) // XA_DH
    o = jnp.zeros((x.shape[0], hd), F32)
    for h in range(XA_HEADS):
        s = _dot_nt(q, jnp.where(lane_h == h, k, 0.0).astype(BF16)) * (XA_DH ** -0.5)
        p = jnp.exp(s - jnp.max(s, axis=-1, keepdims=True))
        p = p / jnp.sum(p, axis=-1, keepdims=True)
        o = o + _dot(p.astype(BF16), jnp.where(lane_h == h, v, 0.0).astype(BF16))
    o_ref[0] = x + _dot(o.astype(BF16), wo_ref[...])


def cross_attn(h3, g, wq, k, v, wo, tm=1024):
    bsz, t, d = h3.shape
    tm = min(tm, t)
    mlen = k.shape[1]
    hd = XA_HEADS * XA_DH
    return pl.pallas_call(
        _xattn_kernel,
        grid=(bsz, t // tm),
        in_specs=[pl.BlockSpec((1, tm, d), lambda b, i: (b, i, 0)),
                  pl.BlockSpec((1, d), lambda b, i: (0, 0)),
                  pl.BlockSpec((d, hd), lambda b, i: (0, 0)),
                  pl.BlockSpec((1, mlen, hd), lambda b, i: (b, 0, 0)),
                  pl.BlockSpec((1, mlen, hd), lambda b, i: (b, 0, 0)),
                  pl.BlockSpec((hd, d), lambda b, i: (0, 0))],
        out_specs=pl.BlockSpec((1, tm, d), lambda b, i: (b, i, 0)),
        out_shape=jax.ShapeDtypeStruct((bsz, t, d), F32),
        compiler_params=_cparams(("parallel", "parallel")),
        name="cross_attn",
    )(h3, g.reshape(1, d), wq, k, v, wo)


def _ffn_kernel(h_ref, g_ref, w1_ref, w3_ref, w2_ref, o_ref, hn_ref, acc_ref):
    f = pl.program_id(1)

    @pl.when(f == 0)
    def _():
        hn_ref[...] = _rms(h_ref[...], g_ref[...]).astype(BF16)
        acc_ref[...] = jnp.zeros_like(acc_ref)

    a = _dot(hn_ref[...], w1_ref[...])
    b = _dot(hn_ref[...], w3_ref[...])
    acc_ref[...] += _dot((a * jax.nn.sigmoid(a) * b).astype(BF16), w2_ref[...])

    @pl.when(f == pl.num_programs(1) - 1)
    def _():
        o_ref[...] = h_ref[...] + acc_ref[...]


def ffn(h, g, w1, w3, w2, tm=512, tf=1408):
    m, d = h.shape
    tm = min(tm, m)
    nf = w1.shape[1]
    assert nf % tf == 0
    return pl.pallas_call(
        _ffn_kernel,
        grid=(m // tm, nf // tf),
        in_specs=[pl.BlockSpec((tm, d), lambda i, f: (i, 0)),
                  pl.BlockSpec((1, d), lambda i, f: (0, 0)),
                  pl.BlockSpec((d, tf), lambda i, f: (0, f)),
                  pl.BlockSpec((d, tf), lambda i, f: (0, f)),
                  pl.BlockSpec((tf, d), lambda i, f: (f, 0))],
        out_specs=pl.BlockSpec((tm, d), lambda i, f: (i, 0)),
        out_shape=jax.ShapeDtypeStruct((m, d), F32),
        scratch_shapes=[pltpu.VMEM((tm, d), BF16), pltpu.VMEM((tm, d), F32)],
        compiler_params=_cparams(("parallel", "arbitrary")),
        name="ffn",
    )(h, g.reshape(1, d), w1, w3, w2)


GATE_LANES = 128
MOE_CHUNK = 256
MOE_ROWS = 128
MOE_TOKENS = 2048


def _router_kernel(h_ref, g_ref, wr_ref, hn_ref, gate_ref, cnt_ref):
    hn = _rms(h_ref[...], g_ref[...])
    hn_ref[...] = hn.astype(BF16)
    tm = hn.shape[0]
    lane = lax.broadcasted_iota(jnp.int32, (tm, GATE_LANES), 1)
    logits = jnp.where(lane < N_EXPERTS, _dot_3pass(hn, wr_ref[...]), -jnp.inf)
    m1 = jnp.max(logits, axis=-1, keepdims=True)
    i1 = jnp.min(jnp.where(logits == m1, lane, GATE_LANES), axis=-1, keepdims=True)
    rest = jnp.where(lane == i1, -jnp.inf, logits)
    m2 = jnp.max(rest, axis=-1, keepdims=True)
    i2 = jnp.min(jnp.where(rest == m2, lane, GATE_LANES), axis=-1, keepdims=True)
    e2 = jnp.exp(m2 - m1)
    g1 = 1.0 / (1.0 + e2)
    gates = jnp.where(lane == i1, g1, 0.0) + jnp.where(lane == i2, e2 * g1, 0.0)
    gates = jnp.where(lane == N_EXPERTS, i1.astype(F32), gates)
    gate_ref[...] = jnp.where(lane == N_EXPERTS + 1, i2.astype(F32), gates)
    sel = jnp.where((lane == i1) | (lane == i2), 1.0, 0.0)
    for c in range(tm // MOE_CHUNK):
        cnt_ref[c] = jnp.sum(sel[c * MOE_CHUNK:(c + 1) * MOE_CHUNK, :], axis=0, keepdims=True)


def route_tokens(h, g, wr, tm=512):
    m, d = h.shape
    tm = min(tm, m)
    assert tm % MOE_CHUNK == 0
    wrp = jnp.zeros((d, GATE_LANES), F32).at[:, :N_EXPERTS].set(wr)
    return pl.pallas_call(
        _router_kernel,
        grid=(m // tm,),
        in_specs=[pl.BlockSpec((tm, d), lambda i: (i, 0)),
                  pl.BlockSpec((1, d), lambda i: (0, 0)),
                  pl.BlockSpec((d, GATE_LANES), lambda i: (0, 0))],
        out_specs=[pl.BlockSpec((tm, d), lambda i: (i, 0)),
                   pl.BlockSpec((tm, GATE_LANES), lambda i: (i, 0)),
                   pl.BlockSpec((tm // MOE_CHUNK, 1, GATE_LANES), lambda i: (i, 0, 0))],
        out_shape=[jax.ShapeDtypeStruct((m, d), BF16), jax.ShapeDtypeStruct((m, GATE_LANES), F32),
                   jax.ShapeDtypeStruct((m // MOE_CHUNK, 1, GATE_LANES), F32)],
        compiler_params=_cparams(("parallel",)),
        name="router",
    )(h, g.reshape(1, d), wrp)


def _moe_kernel(cb_ref, hn_ref, gate_ref, w1_ref, w3_ref, w2_ref, y_ref,
                rank_row, rk_ref, gs_ref, xs_ref, yacc_ref, rc_ref, gc_ref, xg_ref):
    t = pl.program_id(0)
    e = pl.program_id(1)
    f = pl.program_id(2)
    ts, d = hn_ref.shape
    ch, rb = MOE_CHUNK, MOE_ROWS
    grp = 2 * rb
    n_ch = ts // ch
    lane = lax.broadcasted_iota(jnp.int32, (1, GATE_LANES), 1)

    def before(c):
        return cb_ref[(t * (n_ch + 1) + c) * N_EXPERTS + e]

    n_blocks = (before(n_ch) + rb - 1) // rb
    n_groups = (n_blocks + 1) // 2

    @pl.when((e == 0) & (f == 0))
    def _():
        y_ref[...] = jnp.zeros_like(y_ref)
        strict_lower = (lax.broadcasted_iota(jnp.int32, (ch, ch), 1)
                        < lax.broadcasted_iota(jnp.int32, (ch, ch), 0)).astype(BF16)
        lane_f = lane.astype(F32)
        offs = jnp.ones((1, GATE_LANES), F32)
        for c in range(n_ch):
            rows = slice(c * ch, (c + 1) * ch)
            g = gate_ref[rows, :]
            sel = jnp.where((lane_f == g[:, N_EXPERTS:N_EXPERTS + 1])
                            | (lane_f == g[:, N_EXPERTS + 1:N_EXPERTS + 2]), 1.0, 0.0)
            r = _dot(strict_lower, sel.astype(BF16)) + offs
            r = jnp.where(sel > 0.0, r, 0.0)
            rank_row[:, rows] = jnp.transpose(r) - 1.0
            high = jnp.floor(r * (1.0 / 256.0))
            rk_ref[0, rows, :] = high.astype(BF16)
            rk_ref[1, rows, :] = (r - 256.0 * high).astype(BF16)
            for j, part in enumerate(_split_bf16(g, 3)):
                gs_ref[j, rows, :] = part
            offs = offs + jnp.sum(sel, axis=0, keepdims=True)

    @pl.when(f == 0)
    def _():
        pick = (lax.broadcasted_iota(jnp.int32, (GATE_LANES, GATE_LANES), 0) == e).astype(BF16)
        for c in range(n_ch):
            rows = slice(c * ch, (c + 1) * ch)
            rc_ref[rows, :] = 256.0 * _dot(rk_ref[0, rows, :], pick) + _dot(rk_ref[1, rows, :], pick) - 1.0
            gc_ref[rows, :] = (_dot(gs_ref[0, rows, :], pick) + _dot(gs_ref[1, rows, :], pick)
                               + _dot(gs_ref[2, rows, :], pick))

        def gather_group(b, carry):
            r0 = pl.multiple_of(b * grp, grp)
            want = (r0 + lax.broadcasted_iota(jnp.int32, (grp, ch), 0)).astype(F32)
            xg_ref[...] = jnp.zeros_like(xg_ref)
            for c in range(n_ch):
                @pl.when((before(c) < r0 + grp) & (before(c + 1) > r0))
                def _(c=c):
                    ranks = rank_row[pl.ds(e, 1), c * ch:(c + 1) * ch]
                    onehot = jnp.where(ranks == want, 1.0, 0.0).astype(BF16)
                    xg_ref[...] += _dot(onehot, hn_ref[c * ch:(c + 1) * ch, :])
            xs_ref[pl.ds(r0, grp), :] = xg_ref[...].astype(BF16)
            yacc_ref[pl.ds(r0, grp), :] = jnp.zeros((grp, d), F32)
            return carry

        lax.fori_loop(0, n_groups, gather_group, 0)

    def ffn_rows(r0, rows):
        x = xs_ref[pl.ds(r0, rows), :]
        a = _dot(x, w1_ref[0])
        g3 = _dot(x, w3_ref[0])
        yacc_ref[pl.ds(r0, rows), :] += _dot((a * jax.nn.sigmoid(a) * g3).astype(BF16), w2_ref[0])

    n_quads = n_blocks // 4

    def ffn_quad(j, carry):
        ffn_rows(pl.multiple_of(j * (4 * rb), 4 * rb), 4 * rb)
        return carry

    lax.fori_loop(0, n_quads, ffn_quad, 0)
    tail = pl.multiple_of(n_quads * (4 * rb), 4 * rb)

    @pl.when((n_blocks & 2) != 0)
    def _():
        ffn_rows(tail, 2 * rb)

    @pl.when((n_blocks & 1) != 0)
    def _():
        ffn_rows(pl.multiple_of(tail + (n_blocks & 2) * rb, rb), rb)

    @pl.when(f == pl.num_programs(2) - 1)
    def _():
        def scatter_group(b, carry):
            r0 = pl.multiple_of(b * grp, grp)
            yb = yacc_ref[pl.ds(r0, grp), :].astype(BF16)
            want = (r0 + lax.broadcasted_iota(jnp.int32, (ch, rb), 1)).astype(F32)
            for c in range(n_ch):
                @pl.when((before(c) < r0 + grp) & (before(c + 1) > r0))
                def _(c=c):
                    rows = slice(c * ch, (c + 1) * ch)
                    rank = rc_ref[rows, :]
                    onehot = jnp.concatenate([jnp.where(rank == want, 1.0, 0.0),
                                              jnp.where(rank == want + float(rb), 1.0, 0.0)],
                                             axis=1).astype(BF16)
                    gate = jnp.concatenate([gc_ref[rows, :]] * (d // GATE_LANES), axis=1)
                    y_ref[rows, :] += gate * _dot(onehot, yb)
            return carry

        lax.fori_loop(0, n_groups, scatter_group, 0)


def moe(hn, gates, cnt, w1, w3, w2, tf=896):
    m, d = hn.shape
    ts = min(MOE_TOKENS, m)
    n_exp, _, nf = w1.shape
    assert nf % tf == 0 and m % ts == 0 and ts % MOE_CHUNK == 0 and MOE_ROWS == GATE_LANES
    n_ch = ts // MOE_CHUNK
    counts = cnt.reshape(m // ts, n_ch, GATE_LANES)[:, :, :n_exp].astype(jnp.int32)
    before = jnp.concatenate([jnp.zeros((m // ts, 1, n_exp), jnp.int32), jnp.cumsum(counts, axis=1)], axis=1)
    grid_spec = pltpu.PrefetchScalarGridSpec(
        num_scalar_prefetch=1,
        grid=(m // ts, n_exp, nf // tf),
        in_specs=[pl.BlockSpec((ts, d), lambda t, e, f, cb: (t, 0), pipeline_mode=pl.Buffered(1)),
                  pl.BlockSpec((ts, GATE_LANES), lambda t, e, f, cb: (t, 0), pipeline_mode=pl.Buffered(1)),
                  pl.BlockSpec((1, d, tf), lambda t, e, f, cb: (e, 0, f)),
                  pl.BlockSpec((1, d, tf), lambda t, e, f, cb: (e, 0, f)),
                  pl.BlockSpec((1, tf, d), lambda t, e, f, cb: (e, f, 0))],
        out_specs=pl.BlockSpec((ts, d), lambda t, e, f, cb: (t, 0), pipeline_mode=pl.Buffered(1)),
        scratch_shapes=[pltpu.VMEM((GATE_LANES, ts), F32),
                        pltpu.VMEM((2, ts, GATE_LANES), BF16),
                        pltpu.VMEM((3, ts, GATE_LANES), BF16),
                        pltpu.VMEM((ts, d), BF16),
                        pltpu.VMEM((ts, d), F32),
                        pltpu.VMEM((ts, MOE_ROWS), F32),
                        pltpu.VMEM((ts, GATE_LANES), F32),
                        pltpu.VMEM((2 * MOE_ROWS, d), F32)])
    return pl.pallas_call(
        _moe_kernel,
        grid_spec=grid_spec,
        out_shape=jax.ShapeDtypeStruct((m, d), F32),
        compiler_params=_cparams(("parallel", "arbitrary", "arbitrary")),
        name="moe",
    )(before.reshape(-1), hn, gates, w1, w3, w2)


def moe_block(h, g, wr, w1, w3, w2):
    hn, gates, cnt = route_tokens(h, g, wr)
    return moe(hn, gates, cnt, w1.astype(BF16), w3.astype(BF16), w2.astype(BF16))


def _final_norm_kernel(g_ref, *refs):
    *x_refs, o_ref = refs
    x = x_refs[0][...]
    for r in x_refs[1:]:
        x = x + r[...]
    o_ref[...] = _rms(x, g_ref[...])


def final_rmsnorm(g, *xs, tm=1024):
    m, d = xs[0].shape
    tm = min(tm, m)
    return pl.pallas_call(
        _final_norm_kernel,
        grid=(m // tm,),
        in_specs=[pl.BlockSpec((1, d), lambda i: (0, 0))] + [pl.BlockSpec((tm, d), lambda i: (i, 0))] * len(xs),
        out_specs=pl.BlockSpec((tm, d), lambda i: (i, 0)),
        out_shape=jax.ShapeDtypeStruct((m, d), F32),
        compiler_params=_cparams(("parallel",)),
        name="final_norm",
    )(g.reshape(1, d), *xs)


def _pad_cols(w, width):
    return jnp.pad(w, ((0, 0), (0, width - w.shape[1])))


def _in_proj_weight(w_in):
    a_end = 2 * GLA_HEADS * GLA_DK + 2 * GROUP_WIDTH + GLA_RANK
    b_end = a_end + GROUP_WIDTH
    c_end = b_end + 3 * GROUP_WIDTH + IDX_HEADS * IDX_DIM + IDX_DIM + IDX_HEADS
    assert w_in.shape[1] == c_end + 2 * GROUP_WIDTH
    return jnp.concatenate([_pad_cols(w_in[:, :a_end], A_WIDTH), w_in[:, a_end:b_end],
                            _pad_cols(w_in[:, b_end:c_end], C_MAIN + C_AUX), w_in[:, c_end:]], axis=1).astype(BF16)


def hybrid_layer(h, mem2, toe, p, bsz, t):
    m, d = h.shape
    z_a, z_b, z_c, z_aux, z_d = norm_matmul(h, p["norm_mix"], _in_proj_weight(p["w_in"]),
                                            (A_WIDTH, B_WIDTH, C_MAIN, C_AUX, D_WIDTH),
                                            (F32, F32, BF16, F32, F32))
    o_a = gla_mixer(z_a.reshape(bsz, t, A_WIDTH), p["gla_wa2"], p["gla_ba"], p["gla_norm"])
    o_b = pool_mixer(z_b.reshape(bsz, t, B_WIDTH), p["pool_w"], p["pool_b"].reshape(-1), p["pool_scale"])
    o_c = dsa_mixer(z_c.reshape(bsz, t, C_MAIN), z_aux.reshape(bsz, t, C_AUX), toe)
    o_d = sgu_mixer(z_d.reshape(bsz, t, D_WIDTH), p["sgu_ln_g"], p["sgu_ln_b"], p["sgu_w"], p["sgu_b"])
    h = out_proj(h, [o.reshape(m, GROUP_WIDTH) for o in (o_a, o_b, o_c, o_d)], p["w_out"].astype(BF16))

    hd = XA_HEADS * XA_DH
    k, v = norm_matmul(mem2, p["norm_mem"], p["xa_wkv"].astype(BF16), (hd, hd))
    mlen = mem2.shape[0] // bsz
    h = cross_attn(h.reshape(bsz, t, d), p["norm_xa"], p["xa_wq"].astype(BF16),
                   k.reshape(bsz, mlen, hd), v.reshape(bsz, mlen, hd), p["xa_wo"].astype(BF16))
    return h.reshape(m, d)


def kernel(x, mem, rel_bias, final_norm, norm_mix, w_in, gla_wa2, gla_ba, gla_norm, pool_w, pool_b,
           pool_scale, sgu_ln_g, sgu_ln_b, sgu_w, sgu_b, w_out, norm_xa, norm_mem, xa_wq, xa_wkv, xa_wo,
           norm_ffn, ffn_w1, ffn_w3, ffn_w2, router, moe_w1, moe_w3, moe_w2):
    bsz, t, d = x.shape
    depth = norm_mix.shape[0]
    h = x.reshape(bsz * t, d)
    mem2 = mem.reshape(-1, d)
    toe = rel_bias_tables(rel_bias)
    pending = None
    for i in range(depth):
        if pending is not None:
            h, pending = h + pending, None
        p = dict(norm_mix=norm_mix[i], w_in=w_in[i], gla_wa2=gla_wa2[i], gla_ba=gla_ba[i],
                 gla_norm=gla_norm[i], pool_w=pool_w[i], pool_b=pool_b[i], pool_scale=pool_scale[i],
                 sgu_ln_g=sgu_ln_g[i], sgu_ln_b=sgu_ln_b[i], sgu_w=sgu_w[i], sgu_b=sgu_b[i],
                 w_out=w_out[i], norm_xa=norm_xa[i], norm_mem=norm_mem[i], xa_wq=xa_wq[i],
                 xa_wkv=xa_wkv[i], xa_wo=xa_wo[i])
        h = hybrid_layer(h, mem2, toe, p, bsz, t)
        j = i // 2
        if i % 2 == 0:
            h = ffn(h, norm_ffn[i], ffn_w1[j].astype(BF16), ffn_w3[j].astype(BF16), ffn_w2[j].astype(BF16))
        else:
            pending = moe_block(h, norm_ffn[i], router[j], moe_w1[j], moe_w3[j], moe_w2[j])
    xs = (h,) if pending is None else (h, pending)
    return final_rmsnorm(final_norm, *xs).reshape(bsz, t, d)
```

```python
import functools
import math

import jax
import jax.numpy as jnp
import numpy as np
from jax import lax
from jax.experimental import pallas as pl
from jax.experimental.pallas import tpu as pltpu

F32 = jnp.float32
BF16 = jnp.bfloat16
EPS = 1e-6

GROUP_WIDTH = 256

GLA_HEADS = 4
GLA_DV = 64
GLA_DK = 32
GLA_RANK = 16
GLA_TAU = 16.0
GLA_CHUNK = 64
GLA_GROUP = 8

POOL_WINDOWS = (2, 4, 8, 16)
POOL_CG = 64

DSA_HEADS = 4
DSA_DH = 64
IDX_HEADS = 8
IDX_DIM = 32
DSA_TOPK_MAX = 256
DSA_BLOCK = 256
DSA_SUB = 64

SGU_GROUPS = 4
SGU_CHUNK = 128
SGU_CG = 64

REL_BUCKETS = 32
REL_MAX_DIST = 128

XA_HEADS = 4
XA_DH = 64

N_EXPERTS = 8

A_WIDTH = 896
B_WIDTH = 256
C_MAIN = 1024
C_AUX = 128
D_WIDTH = 512

INT_MIN = -(2 ** 31)
NEG_BIG = -1e30
VMEM_LIMIT = 56 * 1024 * 1024


def _cparams(sem):
    return pltpu.CompilerParams(dimension_semantics=sem, vmem_limit_bytes=VMEM_LIMIT)


def _dot(a, b):
    return jnp.dot(a, b, preferred_element_type=F32)


def _dot_nt(a, b):
    return lax.dot_general(a, b, (((1,), (1,)), ((), ())), preferred_element_type=F32)


def _dot_tn(a, b):
    return lax.dot_general(a, b, (((0,), (0,)), ((), ())), preferred_element_type=F32)


def _dot_f32(a, b):
    return jnp.dot(a, b, preferred_element_type=F32, precision=lax.Precision.HIGHEST)


def _split_bf16(x, terms):
    parts = []
    for _ in range(terms):
        p = x.astype(BF16)
        parts.append(p)
        x = x - p.astype(F32)
    return parts


def _dot_exact_rhs(a, b, terms):
    out = None
    for p in _split_bf16(a, terms):
        d = _dot(p, b)
        out = d if out is None else out + d
    return out


def _dot_exact_lhs(a, b, terms):
    out = None
    for p in _split_bf16(b, terms):
        d = _dot(a, p)
        out = d if out is None else out + d
    return out


def _dot_3pass(a, b):
    a_hi, a_lo = _split_bf16(a, 2)
    b_hi, b_lo = _split_bf16(b, 2)
    return _dot(a_hi, b_hi) + (_dot(a_hi, b_lo) + _dot(a_lo, b_hi))


def _rms(x, g):
    return x * lax.rsqrt(jnp.mean(x * x, axis=-1, keepdims=True) + EPS) * g


def _norm_matmul_kernel(h_ref, g_ref, w_ref, *out_refs, widths):
    hb = _rms(h_ref[...], g_ref[...]).astype(BF16)
    off = 0
    for o_ref, wd in zip(out_refs, widths):
        o_ref[...] = _dot(hb, w_ref[:, off:off + wd]).astype(o_ref.dtype)
        off += wd


def norm_matmul(h, g, w, widths, dtypes=None, tm=512):
    dtypes = dtypes or (F32,) * len(widths)
    m, d = h.shape
    tm = min(tm, m)
    n = sum(widths)
    return pl.pallas_call(
        functools.partial(_norm_matmul_kernel, widths=widths),
        grid=(m // tm,),
        in_specs=[pl.BlockSpec((tm, d), lambda i: (i, 0)),
                  pl.BlockSpec((1, d), lambda i: (0, 0)),
                  pl.BlockSpec((d, n), lambda i: (0, 0))],
        out_specs=[pl.BlockSpec((tm, wd), lambda i: (i, 0)) for wd in widths],
        out_shape=[jax.ShapeDtypeStruct((m, wd), dt) for wd, dt in zip(widths, dtypes)],
        compiler_params=_cparams(("parallel",)),
        name="norm_matmul",
    )(h, g.reshape(1, d), w)


def _log_sigmoid(x):
    return jnp.minimum(x, 0.0) - jnp.log1p(jnp.exp(-jnp.abs(x)))


def _gla_kernel(z_ref, wa2_ref, ba_ref, ng_ref, o_ref, s_ref, *, n_groups):
    c = GLA_CHUNK
    grp = GLA_GROUP * c
    hk = GLA_HEADS * GLA_DK
    hv = GLA_HEADS * GLA_DV
    s_ref[...] = jnp.zeros_like(s_ref)

    head_k = lax.broadcasted_iota(jnp.int32, (1, hk), 1) // GLA_DK
    head_v = lax.broadcasted_iota(jnp.int32, (1, hv), 1) // GLA_DV
    g_row = lax.broadcasted_iota(jnp.int32, (grp, grp), 0)
    g_col = lax.broadcasted_iota(jnp.int32, (grp, grp), 1)
    tril = (((g_row // c) == (g_col // c)) & (g_col <= g_row)).astype(BF16)
    causal4 = (lax.broadcasted_iota(jnp.int32, (GLA_HEADS * c, c), 1)
               <= lax.broadcasted_iota(jnp.int32, (GLA_HEADS * c, c), 0) % c)
    state_mask = (lax.broadcasted_iota(jnp.int32, (hk, hv), 0) // GLA_DK
                  == lax.broadcasted_iota(jnp.int32, (hk, hv), 1) // GLA_DV)
    norm_mat = jnp.where(lax.broadcasted_iota(jnp.int32, (hv, hv), 0) // GLA_DV
                         == lax.broadcasted_iota(jnp.int32, (hv, hv), 1) // GLA_DV,
                         1.0 / GLA_DV, 0.0).astype(BF16)
    wa2 = wa2_ref[...]
    ba = ba_ref[...]
    ng = ng_ref[...]

    def body(n, carry):
        r0 = pl.multiple_of(n * grp, grp)
        z = z_ref[0, pl.ds(r0, grp), :]
        q, k, v, g, lr = z[:, 0:128], z[:, 128:256], z[:, 256:512], z[:, 512:768], z[:, 768:896]
        log_a = _log_sigmoid(_dot_f32(lr, wa2) + ba) / GLA_TAU
        b = _dot_exact_lhs(tril, log_a, 3)
        b_end = jnp.concatenate([jnp.broadcast_to(b[(ci + 1) * c - 1:(ci + 1) * c, :], (c, hk))
                                 for ci in range(GLA_GROUP)], axis=0)
        q_t = q * (GLA_DK ** -0.5) * jnp.exp(b)
        q_tb = q_t.astype(BF16)
        k_t = (k * jnp.exp(-b)).astype(BF16)
        k_dec = (k * jnp.exp(b_end - b)).astype(BF16)
        vb = v.astype(BF16)
        outs = []
        for ci in range(GLA_GROUP):
            rows = slice(ci * c, (ci + 1) * c)
            q4 = jnp.concatenate([jnp.where(head_k == h, q_t[rows], 0.0) for h in range(GLA_HEADS)],
                                 axis=0).astype(BF16)
            att = jnp.where(causal4, _dot_nt(q4, k_t[rows]), 0.0)
            r = _dot(att.astype(BF16), vb[rows])
            o = _dot(q_tb[rows], s_ref[...].astype(BF16))
            for h in range(GLA_HEADS):
                o = o + jnp.where(head_v == h, r[h * c:(h + 1) * c, :], 0.0)
            outs.append(o)
            kv = jnp.where(state_mask, _dot_tn(k_dec[rows], vb[rows]), 0.0)
            last = b_end[ci * c:ci * c + 1, :]
            dec = jnp.exp(jnp.transpose(jnp.broadcast_to(last, (hk, hk))))
            s_ref[...] = s_ref[...] * jnp.concatenate([dec, dec], axis=1) + kv
        o = jnp.concatenate(outs, axis=0)
        o = o * lax.rsqrt(_dot_exact_rhs(o * o, norm_mat, 2) + EPS) * ng
        o_ref[0, pl.ds(r0, grp), :] = o * (g * jax.nn.sigmoid(g))
        return carry

    lax.fori_loop(0, n_groups, body, 0)


def gla_mixer(z_a, wa2, ba, norm_g):
    bsz, t, _ = z_a.shape
    hk = GLA_HEADS * GLA_DK
    wa2p = jnp.zeros((128, hk), F32).at[:GLA_RANK].set(wa2)
    return pl.pallas_call(
        functools.partial(_gla_kernel, n_groups=t // (GLA_GROUP * GLA_CHUNK)),
        grid=(bsz,),
        in_specs=[pl.BlockSpec((1, t, A_WIDTH), lambda b: (b, 0, 0)),
                  pl.BlockSpec((128, hk), lambda b: (0, 0)),
                  pl.BlockSpec((1, hk), lambda b: (0, 0)),
                  pl.BlockSpec((1, GROUP_WIDTH), lambda b: (0, 0))],
        out_specs=pl.BlockSpec((1, t, GROUP_WIDTH), lambda b: (b, 0, 0)),
        out_shape=jax.ShapeDtypeStruct((bsz, t, GROUP_WIDTH), F32),
        scratch_shapes=[pltpu.VMEM((hk, GROUP_WIDTH), F32)],
        compiler_params=_cparams(("parallel",)),
        name="gla_mixer",
    )(z_a, wa2p, ba.reshape(1, hk), norm_g.reshape(1, GROUP_WIDTH))


def _pool_kernel(u_ref, w_ref, b_ref, sc_ref, o_ref):
    u = u_ref[0]
    t, gw = u.shape
    row = lax.broadcasted_iota(jnp.int32, (t, gw), 0)
    grp = lax.broadcasted_iota(jnp.int32, (t, gw), 1) // POOL_CG

    def shifted(x, k):
        return jnp.where(row >= k, pltpu.roll(x, k, axis=0), 0.0)

    s = u
    p = jnp.zeros_like(u)
    for gi, win in enumerate(POOL_WINDOWS):
        half = win // 2
        s = s + shifted(s, half)
        cnt = jnp.minimum(row + 1, win).astype(F32)
        p = jnp.where(grp == gi, s / cnt - u, p)
    y = _dot(p.astype(BF16), w_ref[...]) + b_ref[...]
    o_ref[0] = y * sc_ref[...]


def pool_mixer(z_b, w, b, scale):
    assert POOL_WINDOWS == (2, 4, 8, 16)
    bsz, t, gw = z_b.shape
    w_bd = jnp.zeros((gw, gw), F32)
    for gi in range(len(POOL_WINDOWS)):
        w_bd = w_bd.at[gi * POOL_CG:(gi + 1) * POOL_CG, gi * POOL_CG:(gi + 1) * POOL_CG].set(w[gi])
    return pl.pallas_call(
        _pool_kernel,
        grid=(bsz,),
        in_specs=[pl.BlockSpec((1, t, gw), lambda i: (i, 0, 0)),
                  pl.BlockSpec((gw, gw), lambda i: (0, 0)),
                  pl.BlockSpec((1, gw), lambda i: (0, 0)),
                  pl.BlockSpec((1, gw), lambda i: (0, 0))],
        out_specs=pl.BlockSpec((1, t, gw), lambda i: (i, 0, 0)),
        out_shape=jax.ShapeDtypeStruct((bsz, t, gw), F32),
        compiler_params=_cparams(("parallel",)),
        name="pool_mixer",
    )(z_b, w_bd.astype(BF16), b.reshape(1, gw), scale.reshape(1, gw))


def _sgu_kernel(z_ref, lg_ref, lb_ref, w_ref, bm_ref, o_ref, *, chunks):
    c = SGU_CHUNK
    gw = GROUP_WIDTH
    rows = SGU_GROUPS * c
    tri = (lax.broadcasted_iota(jnp.int32, (rows, c), 1)
           <= lax.broadcasted_iota(jnp.int32, (rows, c), 0) % c)
    ws = jnp.where(tri, w_ref[...], 0.0).astype(BF16)
    grp = lax.broadcasted_iota(jnp.int32, (1, gw), 1) // SGU_CG
    for ci in range(chunks):
        z = jax.nn.gelu(z_ref[0, ci * c:(ci + 1) * c, :], approximate=True)
        u, v = z[:, :gw], z[:, gw:]
        mu = jnp.mean(v, axis=-1, keepdims=True)
        var = jnp.mean(jnp.square(v - mu), axis=-1, keepdims=True)
        vn = (v - mu) * lax.rsqrt(var + EPS) * lg_ref[...] + lb_ref[...]
        r = _dot(ws, vn.astype(BF16))
        mixed = bm_ref[...]
        for g in range(SGU_GROUPS):
            mixed = mixed + jnp.where(grp == g, r[g * c:(g + 1) * c, :], 0.0)
        o_ref[0, ci * c:(ci + 1) * c, :] = u * mixed


def sgu_mixer(z_d, ln_g, ln_b, w_s, b_s, chunks=4):
    bsz, t, _ = z_d.shape
    gw = GROUP_WIDTH
    tt = chunks * SGU_CHUNK
    bias = jnp.repeat(b_s.T, SGU_CG, axis=1)
    return pl.pallas_call(
        functools.partial(_sgu_kernel, chunks=chunks),
        grid=(bsz, t // tt),
        in_specs=[pl.BlockSpec((1, tt, 2 * gw), lambda b, i: (b, i, 0)),
                  pl.BlockSpec((1, gw), lambda b, i: (0, 0)),
                  pl.BlockSpec((1, gw), lambda b, i: (0, 0)),
                  pl.BlockSpec((SGU_GROUPS * SGU_CHUNK, SGU_CHUNK), lambda b, i: (0, 0)),
                  pl.BlockSpec((SGU_CHUNK, gw), lambda b, i: (0, 0))],
        out_specs=pl.BlockSpec((1, tt, gw), lambda b, i: (b, i, 0)),
        out_shape=jax.ShapeDtypeStruct((bsz, t, gw), F32),
        compiler_params=_cparams(("parallel", "parallel")),
        name="sgu_mixer",
    )(z_d, ln_g.reshape(1, gw), ln_b.reshape(1, gw),
      w_s.reshape(SGU_GROUPS * SGU_CHUNK, SGU_CHUNK), bias)


def _bucket_table():
    assert REL_MAX_DIST <= DSA_BLOCK + 1
    s = np.arange(DSA_BLOCK)[:, None]
    t = np.arange(DSA_BLOCK)[None, :]
    dist = np.stack([t - s, DSA_BLOCK + t - s, 2 * DSA_BLOCK + t - s])
    n = np.maximum(dist, 0)
    max_exact = REL_BUCKETS // 2
    nf = np.maximum(n, 1).astype(np.float32)
    large = max_exact + (np.log(nf / np.float32(max_exact)) / np.float32(math.log(REL_MAX_DIST / max_exact))
                         * np.float32(REL_BUCKETS - max_exact)).astype(np.int32)
    return np.where(n < max_exact, n, np.minimum(large, REL_BUCKETS - 1)).astype(np.int32)


def _bias_table_kernel(rb_ref, bucket_ref, o_ref):
    for back in range(3):
        bucket = bucket_ref[back]
        for h in range(DSA_HEADS):
            acc = jnp.zeros(bucket.shape, F32)
            for b in range(REL_BUCKETS):
                acc = jnp.where(bucket == b, rb_ref[b * DSA_HEADS + h], acc)
            o_ref[back, h] = acc


def rel_bias_tables(rel_bias):
    blk = DSA_BLOCK
    return pl.pallas_call(
        _bias_table_kernel,
        in_specs=[pl.BlockSpec(memory_space=pltpu.SMEM),
                  pl.BlockSpec((3, blk, blk), lambda: (0, 0, 0))],
        out_specs=pl.BlockSpec((3, DSA_HEADS, blk, blk), lambda: (0, 0, 0, 0)),
        out_shape=jax.ShapeDtypeStruct((3, DSA_HEADS, blk, blk), F32),
        name="rel_bias_tables",
    )(rel_bias.reshape(-1), jnp.asarray(_bucket_table()))


def _dsa_kernel(q_ref, k_ref, v_ref, qi_ref, kw_ref, qw_ref, toe_ref, o_ref,
                kpl_ref, vt_ref, keys_ref, khi_ref, klo_ref, acc_ref, am_ref, lg_ref, p_ref,
                *, topk, n_blocks, idx_bits):
    blk = DSA_BLOCK
    sub = DSA_SUB
    i = pl.program_id(1)
    hd = DSA_HEADS * DSA_DH
    heads_per_half = 128 // IDX_DIM

    @pl.when(i == 0)
    def _():
        lane = lax.broadcasted_iota(jnp.int32, (blk, 128), 1)

        def build(kb, c):
            r0 = pl.multiple_of(kb * blk, blk)
            ki = jnp.where(lane < IDX_DIM, kw_ref[0, pl.ds(r0, blk), :], 0.0)
            for j in range(heads_per_half):
                kpl_ref[kb, j] = (ki if j == 0 else pltpu.roll(ki, j * IDX_DIM, axis=1)).astype(BF16)
            vt_ref[kb] = jnp.transpose(v_ref[0, pl.ds(r0, blk), :].astype(F32)).astype(BF16)
            return c

        lax.fori_loop(0, n_blocks, build, 0)

    s_loc = lax.broadcasted_iota(jnp.int32, (blk, blk), 0)
    t_loc = lax.broadcasted_iota(jnp.int32, (blk, blk), 1)
    n_vis = i + 1

    qi_t = jnp.transpose(qi_ref[0].astype(F32)).astype(BF16)
    qi_halves = [qi_t[:128, :], qi_t[128:, :]]
    w_t = jnp.transpose(qw_ref[0])
    w_rows = [w_t[IDX_DIM + h:IDX_DIM + h + 1, :] * (IDX_HEADS ** -0.5) * (IDX_DIM ** -0.5)
              for h in range(IDX_HEADS)]
    s_sub = lax.broadcasted_iota(jnp.int32, (sub, blk), 0)
    t_sub = lax.broadcasted_iota(jnp.int32, (sub, blk), 1)

    def score_body(kb, c):
        for ci in range(blk // sub):
            rows = slice(ci * sub, (ci + 1) * sub)
            sc = jnp.zeros((sub, blk), F32)
            for half in range(2):
                for j in range(heads_per_half):
                    d = _dot(kpl_ref[kb, j, rows, :], qi_halves[half])
                    sc = sc + jnp.maximum(d, 0.0) * w_rows[half * heads_per_half + j]
            sc = jnp.where(sc == 0.0, 0.0, sc)
            bits = pltpu.bitcast(sc, jnp.int32)
            key = jnp.where(bits < 0, bits ^ jnp.int32(0x7FFFFFFF), bits)
            vis = (kb < i) | (s_sub + ci * sub <= t_sub)
            key = jnp.where(vis, key, jnp.int32(INT_MIN))
            keys_ref[kb, rows, :] = key
            khi_ref[kb, rows, :] = lax.shift_right_arithmetic(key, 16).astype(jnp.int16)
            klo_ref[kb, rows, :] = ((key & 0xFFFF) - 2 ** 15).astype(jnp.int16)
        return c

    lax.fori_loop(0, n_vis, score_body, 0)

    def count(pred):
        def body(kb, acc):
            hit = jnp.where(pred(keys_ref[kb], kb), 1.0, 0.0)
            return acc + jnp.sum(hit.reshape(blk // 32, 32, blk), axis=0)
        acc = lax.fori_loop(0, n_vis, body, jnp.zeros((32, blk), F32))
        return jnp.sum(acc, axis=0, keepdims=True)

    def count16(ref, cand):
        cand = cand.astype(jnp.int16)

        def body(kb, acc):
            hit = jnp.where(ref[kb] >= cand, jnp.int16(1), jnp.int16(0))
            for j in range(blk // 32):
                acc = acc + hit[j * 32:(j + 1) * 32]
            return acc
        acc = lax.fori_loop(0, n_vis, body, jnp.zeros((32, blk), jnp.int16))
        return jnp.sum(acc.astype(jnp.int32).astype(F32), axis=0, keepdims=True)

    def search16(ref, offset):
        lowest = jnp.full((1, blk), -(2 ** 15), jnp.int32)
        base = jnp.where(offset + count16(ref, jnp.zeros((1, blk), jnp.int32)) >= kf, 0, lowest)

        def bit_body(it, base):
            cand = base | lax.shift_left(jnp.int32(1), 14 - it)
            return jnp.where(offset + count16(ref, cand) >= kf, cand, base)
        return lax.fori_loop(0, 15, bit_body, base)

    kf = float(topk)
    zero = jnp.zeros((1, blk), jnp.int32)
    thr_hi = search16(khi_ref, 0.0)
    thr_hi16 = thr_hi.astype(jnp.int16)

    def low_body(kb, above):
        hi = khi_ref[kb]
        klo_ref[kb] = jnp.where(hi == thr_hi16, klo_ref[kb], jnp.int16(-(2 ** 15)))
        hit = jnp.where(hi > thr_hi16, jnp.int16(1), jnp.int16(0))
        for j in range(blk // 32):
            above = above + hit[j * 32:(j + 1) * 32]
        return above

    above = lax.fori_loop(0, n_vis, low_body, jnp.zeros((32, blk), jnp.int16))
    n_above = jnp.sum(above.astype(jnp.int32).astype(F32), axis=0, keepdims=True)
    thr_lo = search16(klo_ref, n_above)
    thr = lax.shift_left(thr_hi, 16) | (thr_lo + 2 ** 15)
    thr_sel = jnp.maximum(thr, jnp.int32(INT_MIN + 1))

    n_gt = count(lambda key, kb: key > thr)
    n_eq = count(lambda key, kb: key == thr)
    need = kf - n_gt
    excess = jnp.where((n_eq > need) & (thr > jnp.int32(INT_MIN)), 1.0, 0.0)

    def tie_search():
        def tie_body(it, j):
            cand = j | lax.shift_left(jnp.int32(1), idx_bits - 1 - it)
            below = count(lambda key, kb: (key == thr) & (kb * blk + s_loc < cand))
            return jnp.where(below < need, cand, j)
        return lax.fori_loop(0, idx_bits, tie_body, zero)

    last = lax.cond(jnp.max(excess) > 0.0, tie_search, lambda: jnp.full((1, blk), 2 ** 30, jnp.int32))

    assert DSA_DH ** -0.5 == 0.125
    q_t = jnp.transpose(q_ref[0].astype(F32) * (DSA_DH ** -0.5))
    row_h = lax.broadcasted_iota(jnp.int32, (hd, 1), 0) // DSA_DH
    q_heads = [jnp.where(row_h == h, q_t, 0.0).astype(BF16) for h in range(DSA_HEADS)]
    acc_ref[...] = jnp.zeros_like(acc_ref)
    n_sub = blk // sub

    def att_body(kb, carry):
        ms, ls = carry
        back = jnp.minimum(i - kb, 2)
        for ci in range(n_sub):
            rows = slice(ci * sub, (ci + 1) * sub)
            key = keys_ref[kb, rows, :]
            sel = (key > thr_sel) | ((key == thr_sel) & (kb * blk + ci * sub + s_sub <= last))
            am_ref[rows, :] = jnp.where(sel, 0.0, NEG_BIG)
        new_ms, alphas = [], []
        for h in range(DSA_HEADS):
            pm = jnp.full((8, blk), NEG_BIG, F32)
            for ci in range(n_sub):
                rows = slice(ci * sub, (ci + 1) * sub)
                k_rows = k_ref[0, pl.ds(pl.multiple_of(kb * blk + ci * sub, sub), sub), :]
                lg = _dot(k_rows, q_heads[h]) + toe_ref[back, h, rows, :] + am_ref[rows, :]
                lg_ref[h, rows, :] = lg
                pm = jnp.maximum(pm, jnp.max(lg.reshape(sub // 8, 8, blk), axis=0))
            m_new = jnp.maximum(ms[h], jnp.max(pm, axis=0, keepdims=True))
            new_ms.append(m_new)
            alphas.append(jnp.exp(ms[h] - m_new))
        new_ls = []
        for h in range(DSA_HEADS):
            ps = jnp.zeros((8, blk), F32)
            for ci in range(n_sub):
                rows = slice(ci * sub, (ci + 1) * sub)
                p = jnp.exp(lg_ref[h, rows, :] - new_ms[h])
                ps = ps + jnp.sum(p.reshape(sub // 8, 8, blk), axis=0)
                p_ref[h, rows, :] = p.astype(BF16)
            new_ls.append(ls[h] * alphas[h] + jnp.sum(ps, axis=0, keepdims=True))
        for h in range(DSA_HEADS):
            hrows = slice(h * DSA_DH, (h + 1) * DSA_DH)
            acc_ref[hrows, :] = acc_ref[hrows, :] * alphas[h] + _dot(vt_ref[kb, hrows, :], p_ref[h])
        return tuple(new_ms), tuple(new_ls)

    init = (tuple(jnp.full((1, blk), 0.01 * NEG_BIG, F32) for _ in range(DSA_HEADS)),
            tuple(jnp.zeros((1, blk), F32) for _ in range(DSA_HEADS)))
    _, ls = lax.fori_loop(0, n_vis, att_body, init)
    for h in range(DSA_HEADS):
        rows = slice(h * DSA_DH, (h + 1) * DSA_DH)
        acc_ref[rows, :] = acc_ref[rows, :] * (1.0 / ls[h])
    o_ref[0] = jnp.transpose(acc_ref[...])


def dsa_mixer(z_c, z_aux, toe):
    bsz, t, _ = z_c.shape
    blk = DSA_BLOCK
    n_blocks = t // blk
    assert t % blk == 0
    topk = min(DSA_TOPK_MAX, t // 4)
    hd = DSA_HEADS * DSA_DH
    kernel = functools.partial(_dsa_kernel, topk=topk, n_blocks=n_blocks,
                               idx_bits=max(1, (t - 1).bit_length()))
    return pl.pallas_call(
        kernel,
        grid=(bsz, n_blocks),
        in_specs=[pl.BlockSpec((1, blk, hd), lambda b, i: (b, i, 0)),
                  pl.BlockSpec((1, t, hd), lambda b, i: (b, 0, 1)),
                  pl.BlockSpec((1, t, hd), lambda b, i: (b, 0, 2)),
                  pl.BlockSpec((1, blk, hd), lambda b, i: (b, i, 3)),
                  pl.BlockSpec((1, t, C_AUX), lambda b, i: (b, 0, 0)),
                  pl.BlockSpec((1, blk, C_AUX), lambda b, i: (b, i, 0)),
                  pl.BlockSpec((3, DSA_HEADS, blk, blk), lambda b, i: (0, 0, 0, 0))],
        out_specs=pl.BlockSpec((1, blk, hd), lambda b, i: (b, i, 0)),
        out_shape=jax.ShapeDtypeStruct((bsz, t, hd), F32),
        scratch_shapes=[pltpu.VMEM((n_blocks, 128 // IDX_DIM, blk, 128), BF16),
                        pltpu.VMEM((n_blocks, hd, blk), BF16),
                        pltpu.VMEM((n_blocks, blk, blk), jnp.int32),
                        pltpu.VMEM((n_blocks, blk, blk), jnp.int16),
                        pltpu.VMEM((n_blocks, blk, blk), jnp.int16),
                        pltpu.VMEM((hd, blk), F32),
                        pltpu.VMEM((blk, blk), F32),
                        pltpu.VMEM((DSA_HEADS, blk, blk), F32),
                        pltpu.VMEM((DSA_HEADS, blk, blk), BF16)],
        compiler_params=_cparams(("parallel", "arbitrary")),
        name="dsa_mixer",
    )(z_c, z_c, z_c, z_c, z_aux, z_aux, toe)


def _out_proj_kernel(h_ref, a_ref, b_ref, c_ref, d_ref, w_ref, o_ref):
    gw = GROUP_WIDTH
    acc = h_ref[...]
    for gi, r in enumerate((a_ref, b_ref, c_ref, d_ref)):
        acc = acc + _dot(r[...].astype(BF16), w_ref[gi * gw:(gi + 1) * gw, :])
    o_ref[...] = acc


def out_proj(h, outs, w, tm=1024):
    m, d = h.shape
    tm = min(tm, m)
    gw = GROUP_WIDTH
    return pl.pallas_call(
        _out_proj_kernel,
        grid=(m // tm,),
        in_specs=[pl.BlockSpec((tm, d), lambda i: (i, 0))]
                 + [pl.BlockSpec((tm, gw), lambda i: (i, 0))] * 4
                 + [pl.BlockSpec((4 * gw, d), lambda i: (0, 0))],
        out_specs=pl.BlockSpec((tm, d), lambda i: (i, 0)),
        out_shape=jax.ShapeDtypeStruct((m, d), F32),
        compiler_params=_cparams(("parallel",)),
        name="out_proj",
    )(h, *outs, w)


def _xattn_kernel(h_ref, g_ref, wq_ref, k_ref, v_ref, wo_ref, o_ref):
    x = h_ref[0]
    hd = XA_HEADS * XA_DH
    q = _dot(_rms(x, g_ref[...]).astype(BF16), wq_ref[...]).astype(BF16)
    k = k_ref[0]
    v = v_ref[0]
    lane_h = lax.broadcasted_iota(jnp.int32, (1, hd), 1) // XA_DH
    o = jnp.zeros((x.shape[0], hd), F32)
    for h in range(XA_HEADS):
        s = _dot_nt(q, jnp.where(lane_h == h, k, 0.0).astype(BF16)) * (XA_DH ** -0.5)
        p = jnp.exp(s - jnp.max(s, axis=-1, keepdims=True))
        p = p / jnp.sum(p, axis=-1, keepdims=True)
        o = o + _dot(p.astype(BF16), jnp.where(lane_h == h, v, 0.0).astype(BF16))
    o_ref[0] = x + _dot(o.astype(BF16), wo_ref[...])


def cross_attn(h3, g, wq, k, v, wo, tm=1024):
    bsz, t, d = h3.shape
    tm = min(tm, t)
    mlen = k.shape[1]
    hd = XA_HEADS * XA_DH
    return pl.pallas_call(
        _xattn_kernel,
        grid=(bsz, t // tm),
        in_specs=[pl.BlockSpec((1, tm, d), lambda b, i: (b, i, 0)),
                  pl.BlockSpec((1, d), lambda b, i: (0, 0)),
                  pl.BlockSpec((d, hd), lambda b, i: (0, 0)),
                  pl.BlockSpec((1, mlen, hd), lambda b, i: (b, 0, 0)),
                  pl.BlockSpec((1, mlen, hd), lambda b, i: (b, 0, 0)),
                  pl.BlockSpec((hd, d), lambda b, i: (0, 0))],
        out_specs=pl.BlockSpec((1, tm, d), lambda b, i: (b, i, 0)),
        out_shape=jax.ShapeDtypeStruct((bsz, t, d), F32),
        compiler_params=_cparams(("parallel", "parallel")),
        name="cross_attn",
    )(h3, g.reshape(1, d), wq, k, v, wo)


def _ffn_kernel(h_ref, g_ref, w1_ref, w3_ref, w2_ref, o_ref, hn_ref, acc_ref):
    f = pl.program_id(1)

    @pl.when(f == 0)
    def _():
        hn_ref[...] = _rms(h_ref[...], g_ref[...]).astype(BF16)
        acc_ref[...] = jnp.zeros_like(acc_ref)

    a = _dot(hn_ref[...], w1_ref[...])
    b = _dot(hn_ref[...], w3_ref[...])
    acc_ref[...] += _dot((a * jax.nn.sigmoid(a) * b).astype(BF16), w2_ref[...])

    @pl.when(f == pl.num_programs(1) - 1)
    def _():
        o_ref[...] = h_ref[...] + acc_ref[...]


def ffn(h, g, w1, w3, w2, tm=512, tf=1408):
    m, d = h.shape
    tm = min(tm, m)
    nf = w1.shape[1]
    assert nf % tf == 0
    return pl.pallas_call(
        _ffn_kernel,
        grid=(m // tm, nf // tf),
        in_specs=[pl.BlockSpec((tm, d), lambda i, f: (i, 0)),
                  pl.BlockSpec((1, d), lambda i, f: (0, 0)),
                  pl.BlockSpec((d, tf), lambda i, f: (0, f)),
                  pl.BlockSpec((d, tf), lambda i, f: (0, f)),
                  pl.BlockSpec((tf, d), lambda i, f: (f, 0))],
        out_specs=pl.BlockSpec((tm, d), lambda i, f: (i, 0)),
        out_shape=jax.ShapeDtypeStruct((m, d), F32),
        scratch_shapes=[pltpu.VMEM((tm, d), BF16), pltpu.VMEM((tm, d), F32)],
        compiler_params=_cparams(("parallel", "arbitrary")),
        name="ffn",
    )(h, g.reshape(1, d), w1, w3, w2)


GATE_LANES = 128
MOE_CHUNK = 256
MOE_ROWS = 128
MOE_TOKENS = 2048
MOE_ALIGN = 16


def _router_kernel(h_ref, g_ref, wr_ref, hn_ref, gate_ref, cnt_ref):
    hn = _rms(h_ref[...], g_ref[...])
    hn_ref[...] = hn.astype(BF16)
    tm = hn.shape[0]
    lane = lax.broadcasted_iota(jnp.int32, (tm, GATE_LANES), 1)
    logits = jnp.where(lane < N_EXPERTS, _dot_3pass(hn, wr_ref[...]), -jnp.inf)
    m1 = jnp.max(logits, axis=-1, keepdims=True)
    i1 = jnp.min(jnp.where(logits == m1, lane, GATE_LANES), axis=-1, keepdims=True)
    rest = jnp.where(lane == i1, -jnp.inf, logits)
    m2 = jnp.max(rest, axis=-1, keepdims=True)
    i2 = jnp.min(jnp.where(rest == m2, lane, GATE_LANES), axis=-1, keepdims=True)
    e2 = jnp.exp(m2 - m1)
    g1 = 1.0 / (1.0 + e2)
    gates = jnp.where(lane == i1, g1, 0.0) + jnp.where(lane == i2, e2 * g1, 0.0)
    gates = jnp.where(lane == N_EXPERTS, i1.astype(F32), gates)
    gate_ref[...] = jnp.where(lane == N_EXPERTS + 1, i2.astype(F32), gates)
    sel = jnp.where((lane == i1) | (lane == i2), 1.0, 0.0)
    for c in range(tm // MOE_CHUNK):
        cnt_ref[c] = jnp.sum(sel[c * MOE_CHUNK:(c + 1) * MOE_CHUNK, :], axis=0, keepdims=True)


def route_tokens(h, g, wr, tm=512):
    m, d = h.shape
    tm = min(tm, m)
    assert tm % MOE_CHUNK == 0
    wrp = jnp.zeros((d, GATE_LANES), F32).at[:, :N_EXPERTS].set(wr)
    return pl.pallas_call(
        _router_kernel,
        grid=(m // tm,),
        in_specs=[pl.BlockSpec((tm, d), lambda i: (i, 0)),
                  pl.BlockSpec((1, d), lambda i: (0, 0)),
                  pl.BlockSpec((d, GATE_LANES), lambda i: (0, 0))],
        out_specs=[pl.BlockSpec((tm, d), lambda i: (i, 0)),
                   pl.BlockSpec((tm, GATE_LANES), lambda i: (i, 0)),
                   pl.BlockSpec((tm // MOE_CHUNK, 1, GATE_LANES), lambda i: (i, 0, 0))],
        out_shape=[jax.ShapeDtypeStruct((m, d), BF16), jax.ShapeDtypeStruct((m, GATE_LANES), F32),
                   jax.ShapeDtypeStruct((m // MOE_CHUNK, 1, GATE_LANES), F32)],
        compiler_params=_cparams(("parallel",)),
        name="router",
    )(h, g.reshape(1, d), wrp)


def _moe_kernel(cb_ref, hn_ref, gate_ref, w1_ref, w3_ref, w2_ref, y_ref,
                rank_row, rk_ref, gs_ref, xs_ref, yacc_ref, rc_ref, gc_ref):
    t = pl.program_id(0)
    e = pl.program_id(1)
    f = pl.program_id(2)
    ts, d = hn_ref.shape
    ch, rb = MOE_CHUNK, MOE_ROWS
    win = 2 * rb
    n_ch = ts // ch
    lane = lax.broadcasted_iota(jnp.int32, (1, GATE_LANES), 1)

    def before(c):
        return cb_ref[(t * (n_ch + 1) + c) * N_EXPERTS + e]

    n_blocks = (before(n_ch) + rb - 1) // rb
    n_windows = (n_blocks * rb + win - 1) // win + 1

    def windows(c):
        lo, hi = before(c), before(c + 1)
        s0 = (lo // MOE_ALIGN) * MOE_ALIGN
        return s0, jnp.where(hi > lo, (hi - s0 + win - 1) // win, 0)

    @pl.when((e == 0) & (f == 0))
    def _():
        y_ref[...] = jnp.zeros_like(y_ref)
        strict_lower = (lax.broadcasted_iota(jnp.int32, (ch, ch), 1)
                        < lax.broadcasted_iota(jnp.int32, (ch, ch), 0)).astype(BF16)
        lane_f = lane.astype(F32)
        offs = jnp.ones((1, GATE_LANES), F32)
        for c in range(n_ch):
            rows = slice(c * ch, (c + 1) * ch)
            g = gate_ref[rows, :]
            sel = jnp.where((lane_f == g[:, N_EXPERTS:N_EXPERTS + 1])
                            | (lane_f == g[:, N_EXPERTS + 1:N_EXPERTS + 2]), 1.0, 0.0)
            r = _dot(strict_lower, sel.astype(BF16)) + offs
            r = jnp.where(sel > 0.0, r, 0.0)
            rank_row[:, rows] = jnp.transpose(r) - 1.0
            high = jnp.floor(r * (1.0 / 256.0))
            rk_ref[0, rows, :] = high.astype(BF16)
            rk_ref[1, rows, :] = (r - 256.0 * high).astype(BF16)
            for j, part in enumerate(_split_bf16(g, 3)):
                gs_ref[j, rows, :] = part
            offs = offs + jnp.sum(sel, axis=0, keepdims=True)

    @pl.when(f == 0)
    def _():
        pick = (lax.broadcasted_iota(jnp.int32, (GATE_LANES, GATE_LANES), 0) == e).astype(BF16)
        for c in range(n_ch):
            rows = slice(c * ch, (c + 1) * ch)
            rc_ref[rows, :] = 256.0 * _dot(rk_ref[0, rows, :], pick) + _dot(rk_ref[1, rows, :], pick) - 1.0
            gc_ref[rows, :] = (_dot(gs_ref[0, rows, :], pick) + _dot(gs_ref[1, rows, :], pick)
                               + _dot(gs_ref[2, rows, :], pick))

        def clear(j, carry):
            r0 = pl.multiple_of(j * win, win)
            xs_ref[pl.ds(r0, win), :] = jnp.zeros((win, d), BF16)
            yacc_ref[pl.ds(r0, win), :] = jnp.zeros((win, d), F32)
            return carry

        lax.fori_loop(0, n_windows, clear, 0)

        row_id = lax.broadcasted_iota(jnp.int32, (win, ch), 0)
        for c in range(n_ch):
            s0, n_win = windows(c)

            def gather_window(j, carry, c=c, s0=s0):
                s = pl.multiple_of(s0 + j * win, MOE_ALIGN)
                ranks = rank_row[pl.ds(e, 1), c * ch:(c + 1) * ch]
                onehot = jnp.where(ranks == (s + row_id).astype(F32), 1.0, 0.0).astype(BF16)
                xs_ref[pl.ds(s, win), :] += _dot(onehot, hn_ref[c * ch:(c + 1) * ch, :]).astype(BF16)
                return carry

            lax.fori_loop(0, n_win, gather_window, 0)

    def ffn_rows(r0, rows):
        x = xs_ref[pl.ds(r0, rows), :]
        a = _dot(x, w1_ref[0])
        g3 = _dot(x, w3_ref[0])
        yacc_ref[pl.ds(r0, rows), :] += _dot((a * jax.nn.sigmoid(a) * g3).astype(BF16), w2_ref[0])

    n_quads = n_blocks // 4

    def ffn_quad(j, carry):
        ffn_rows(pl.multiple_of(j * (4 * rb), 4 * rb), 4 * rb)
        return carry

    lax.fori_loop(0, n_quads, ffn_quad, 0)
    tail = pl.multiple_of(n_quads * (4 * rb), 4 * rb)

    @pl.when((n_blocks & 2) != 0)
    def _():
        ffn_rows(tail, 2 * rb)

    @pl.when((n_blocks & 1) != 0)
    def _():
        ffn_rows(pl.multiple_of(tail + (n_blocks & 2) * rb, rb), rb)

    @pl.when(f == pl.num_programs(2) - 1)
    def _():
        def to_bf16(j, carry):
            r0 = pl.multiple_of(j * win, win)
            xs_ref[pl.ds(r0, win), :] = yacc_ref[pl.ds(r0, win), :].astype(BF16)
            return carry

        lax.fori_loop(0, n_windows, to_bf16, 0)

        lane_id = lax.broadcasted_iota(jnp.int32, (ch, rb), 1).astype(F32)
        for c in range(n_ch):
            s0, n_win = windows(c)

            def scatter_window(j, carry, c=c, s0=s0):
                rows = slice(c * ch, (c + 1) * ch)
                s = pl.multiple_of(s0 + j * win, MOE_ALIGN)
                rank = rc_ref[rows, :] - s.astype(F32)
                onehot = jnp.concatenate([jnp.where(rank == lane_id, 1.0, 0.0),
                                          jnp.where(rank == lane_id + float(rb), 1.0, 0.0)],
                                         axis=1).astype(BF16)
                gate = jnp.concatenate([gc_ref[rows, :]] * (d // GATE_LANES), axis=1)
                y_ref[rows, :] += gate * _dot(onehot, xs_ref[pl.ds(s, win), :])
                return carry

            lax.fori_loop(0, n_win, scatter_window, 0)


def moe(hn, gates, cnt, w1, w3, w2, tf=896):
    m, d = hn.shape
    ts = min(MOE_TOKENS, m)
    n_exp, _, nf = w1.shape
    assert nf % tf == 0 and m % ts == 0 and ts % MOE_CHUNK == 0 and MOE_ROWS == GATE_LANES
    n_ch = ts // MOE_CHUNK
    counts = cnt.reshape(m // ts, n_ch, GATE_LANES)[:, :, :n_exp].astype(jnp.int32)
    before = jnp.concatenate([jnp.zeros((m // ts, 1, n_exp), jnp.int32), jnp.cumsum(counts, axis=1)], axis=1)
    grid_spec = pltpu.PrefetchScalarGridSpec(
        num_scalar_prefetch=1,
        grid=(m // ts, n_exp, nf // tf),
        in_specs=[pl.BlockSpec((ts, d), lambda t, e, f, cb: (t, 0), pipeline_mode=pl.Buffered(1)),
                  pl.BlockSpec((ts, GATE_LANES), lambda t, e, f, cb: (t, 0), pipeline_mode=pl.Buffered(1)),
                  pl.BlockSpec((1, d, tf), lambda t, e, f, cb: (e, 0, f)),
                  pl.BlockSpec((1, d, tf), lambda t, e, f, cb: (e, 0, f)),
                  pl.BlockSpec((1, tf, d), lambda t, e, f, cb: (e, f, 0))],
        out_specs=pl.BlockSpec((ts, d), lambda t, e, f, cb: (t, 0), pipeline_mode=pl.Buffered(1)),
        scratch_shapes=[pltpu.VMEM((GATE_LANES, ts), F32),
                        pltpu.VMEM((2, ts, GATE_LANES), BF16),
                        pltpu.VMEM((3, ts, GATE_LANES), BF16),
                        pltpu.VMEM((ts + 2 * MOE_ROWS, d), BF16),
                        pltpu.VMEM((ts + 2 * MOE_ROWS, d), F32),
                        pltpu.VMEM((ts, MOE_ROWS), F32),
                        pltpu.VMEM((ts, GATE_LANES), F32)])
    return pl.pallas_call(
        _moe_kernel,
        grid_spec=grid_spec,
        out_shape=jax.ShapeDtypeStruct((m, d), F32),
        compiler_params=_cparams(("parallel", "arbitrary", "arbitrary")),
        name="moe",
    )(before.reshape(-1), hn, gates, w1, w3, w2)


def moe_block(h, g, wr, w1, w3, w2):
    hn, gates, cnt = route_tokens(h, g, wr)
    return moe(hn, gates, cnt, w1.astype(BF16), w3.astype(BF16), w2.astype(BF16))


def _final_norm_kernel(g_ref, *refs):
    *x_refs, o_ref = refs
    x = x_refs[0][...]
    for r in x_refs[1:]:
        x = x + r[...]
    o_ref[...] = _rms(x, g_ref[...])


def final_rmsnorm(g, *xs, tm=1024):
    m, d = xs[0].shape
    tm = min(tm, m)
    return pl.pallas_call(
        _final_norm_kernel,
        grid=(m // tm,),
        in_specs=[pl.BlockSpec((1, d), lambda i: (0, 0))] + [pl.BlockSpec((tm, d), lambda i: (i, 0))] * len(xs),
        out_specs=pl.BlockSpec((tm, d), lambda i: (i, 0)),
        out_shape=jax.ShapeDtypeStruct((m, d), F32),
        compiler_params=_cparams(("parallel",)),
        name="final_norm",
    )(g.reshape(1, d), *xs)


def _pad_cols(w, width):
    return jnp.pad(w, ((0, 0), (0, width - w.shape[1])))


def _in_proj_weight(w_in):
    a_end = 2 * GLA_HEADS * GLA_DK + 2 * GROUP_WIDTH + GLA_RANK
    b_end = a_end + GROUP_WIDTH
    c_end = b_end + 3 * GROUP_WIDTH + IDX_HEADS * IDX_DIM + IDX_DIM + IDX_HEADS
    assert w_in.shape[1] == c_end + 2 * GROUP_WIDTH
    return jnp.concatenate([_pad_cols(w_in[:, :a_end], A_WIDTH), w_in[:, a_end:b_end],
                            _pad_cols(w_in[:, b_end:c_end], C_MAIN + C_AUX), w_in[:, c_end:]], axis=1).astype(BF16)


def hybrid_layer(h, mem2, toe, p, bsz, t):
    m, d = h.shape
    z_a, z_b, z_c, z_aux, z_d = norm_matmul(h, p["norm_mix"], _in_proj_weight(p["w_in"]),
                                            (A_WIDTH, B_WIDTH, C_MAIN, C_AUX, D_WIDTH),
                                            (F32, F32, BF16, F32, F32))
    o_a = gla_mixer(z_a.reshape(bsz, t, A_WIDTH), p["gla_wa2"], p["gla_ba"], p["gla_norm"])
    o_b = pool_mixer(z_b.reshape(bsz, t, B_WIDTH), p["pool_w"], p["pool_b"].reshape(-1), p["pool_scale"])
    o_c = dsa_mixer(z_c.reshape(bsz, t, C_MAIN), z_aux.reshape(bsz, t, C_AUX), toe)
    o_d = sgu_mixer(z_d.reshape(bsz, t, D_WIDTH), p["sgu_ln_g"], p["sgu_ln_b"], p["sgu_w"], p["sgu_b"])
    h = out_proj(h, [o.reshape(m, GROUP_WIDTH) for o in (o_a, o_b, o_c, o_d)], p["w_out"].astype(BF16))

    hd = XA_HEADS * XA_DH
    k, v = norm_matmul(mem2, p["norm_mem"], p["xa_wkv"].astype(BF16), (hd, hd))
    mlen = mem2.shape[0] // bsz
    h = cross_attn(h.reshape(bsz, t, d), p["norm_xa"], p["xa_wq"].astype(BF16),
                   k.reshape(bsz, mlen, hd), v.reshape(bsz, mlen, hd), p["xa_wo"].astype(BF16))
    return h.reshape(m, d)


def kernel(x, mem, rel_bias, final_norm, norm_mix, w_in, gla_wa2, gla_ba, gla_norm, pool_w, pool_b,
           pool_scale, sgu_ln_g, sgu_ln_b, sgu_w, sgu_b, w_out, norm_xa, norm_mem, xa_wq, xa_wkv, xa_wo,
           norm_ffn, ffn_w1, ffn_w3, ffn_w2, router, moe_w1, moe_w3, moe_w2):
    bsz, t, d = x.shape
    depth = norm_mix.shape[0]
    h = x.reshape(bsz * t, d)
    mem2 = mem.reshape(-1, d)
    toe = rel_bias_tables(rel_bias)
    pending = None
    for i in range(depth):
        if pending is not None:
            h, pending = h + pending, None
        p = dict(norm_mix=norm_mix[i], w_in=w_in[i], gla_wa2=gla_wa2[i], gla_ba=gla_ba[i],
                 gla_norm=gla_norm[i], pool_w=pool_w[i], pool_b=pool_b[i], pool_scale=pool_scale[i],
                 sgu_ln_g=sgu_ln_g[i], sgu_ln_b=sgu_ln_b[i], sgu_w=sgu_w[i], sgu_b=sgu_b[i],
                 w_out=w_out[i], norm_xa=norm_xa[i], norm_mem=norm_mem[i], xa_wq=xa_wq[i],
                 xa_wkv=xa_wkv[i], xa_wo=xa_wo[i])
        h = hybrid_layer(h, mem2, toe, p, bsz, t)
        j = i // 2
        if i % 2 == 0:
            h = ffn(h, norm_ffn[i], ffn_w1[j].astype(BF16), ffn_w3[j].astype(BF16), ffn_w2[j].astype(BF16))
        else:
            pending = moe_block(h, norm_ffn[i], router[j], moe_w1[j], moe_w3[j], moe_w2[j])
    xs = (h,) if pending is None else (h, pending)
    return final_rmsnorm(final_norm, *xs).reshape(bsz, t, d)
```

```python
import functools
import math

import jax
import jax.numpy as jnp
import numpy as np
from jax import lax
from jax.experimental import pallas as pl
from jax.experimental.pallas import tpu as pltpu

F32 = jnp.float32
BF16 = jnp.bfloat16
EPS = 1e-6

GROUP_WIDTH = 256

GLA_HEADS = 4
GLA_DV = 64
GLA_DK = 32
GLA_RANK = 16
GLA_TAU = 16.0
GLA_CHUNK = 64
GLA_GROUP = 8

POOL_WINDOWS = (2, 4, 8, 16)
POOL_CG = 64

DSA_HEADS = 4
DSA_DH = 64
IDX_HEADS = 8
IDX_DIM = 32
DSA_TOPK_MAX = 256
DSA_BLOCK = 256
DSA_SUB = 64

SGU_GROUPS = 4
SGU_CHUNK = 128
SGU_CG = 64

REL_BUCKETS = 32
REL_MAX_DIST = 128

XA_HEADS = 4
XA_DH = 64

N_EXPERTS = 8

A_WIDTH = 896
B_WIDTH = 256
C_MAIN = 1024
C_AUX = 128
D_WIDTH = 512

INT_MIN = -(2 ** 31)
NEG_BIG = -1e30
VMEM_LIMIT = 56 * 1024 * 1024


def _cparams(sem):
    return pltpu.CompilerParams(dimension_semantics=sem, vmem_limit_bytes=VMEM_LIMIT)


def _dot(a, b):
    return jnp.dot(a, b, preferred_element_type=F32)


def _dot_nt(a, b):
    return lax.dot_general(a, b, (((1,), (1,)), ((), ())), preferred_element_type=F32)


def _dot_tn(a, b):
    return lax.dot_general(a, b, (((0,), (0,)), ((), ())), preferred_element_type=F32)


def _dot_f32(a, b):
    return jnp.dot(a, b, preferred_element_type=F32, precision=lax.Precision.HIGHEST)


def _split_bf16(x, terms):
    parts = []
    for _ in range(terms):
        p = x.astype(BF16)
        parts.append(p)
        x = x - p.astype(F32)
    return parts


def _dot_exact_rhs(a, b, terms):
    out = None
    for p in _split_bf16(a, terms):
        d = _dot(p, b)
        out = d if out is None else out + d
    return out


def _dot_exact_lhs(a, b, terms):
    out = None
    for p in _split_bf16(b, terms):
        d = _dot(a, p)
        out = d if out is None else out + d
    return out


def _dot_3pass(a, b):
    a_hi, a_lo = _split_bf16(a, 2)
    b_hi, b_lo = _split_bf16(b, 2)
    return _dot(a_hi, b_hi) + (_dot(a_hi, b_lo) + _dot(a_lo, b_hi))


def _rms(x, g):
    return x * lax.rsqrt(jnp.mean(x * x, axis=-1, keepdims=True) + EPS) * g


def _norm_matmul_kernel(h_ref, g_ref, w_ref, *out_refs, widths):
    hb = _rms(h_ref[...], g_ref[...]).astype(BF16)
    off = 0
    for o_ref, wd in zip(out_refs, widths):
        o_ref[...] = _dot(hb, w_ref[:, off:off + wd]).astype(o_ref.dtype)
        off += wd


def norm_matmul(h, g, w, widths, dtypes=None, tm=512):
    dtypes = dtypes or (F32,) * len(widths)
    m, d = h.shape
    tm = min(tm, m)
    n = sum(widths)
    return pl.pallas_call(
        functools.partial(_norm_matmul_kernel, widths=widths),
        grid=(m // tm,),
        in_specs=[pl.BlockSpec((tm, d), lambda i: (i, 0)),
                  pl.BlockSpec((1, d), lambda i: (0, 0)),
                  pl.BlockSpec((d, n), lambda i: (0, 0))],
        out_specs=[pl.BlockSpec((tm, wd), lambda i: (i, 0)) for wd in widths],
        out_shape=[jax.ShapeDtypeStruct((m, wd), dt) for wd, dt in zip(widths, dtypes)],
        compiler_params=_cparams(("parallel",)),
        name="norm_matmul",
    )(h, g.reshape(1, d), w)


def _log_sigmoid(x):
    return jnp.minimum(x, 0.0) - jnp.log1p(jnp.exp(-jnp.abs(x)))


def _gla_kernel(z_ref, wa2_ref, ba_ref, ng_ref, o_ref, s_ref, *, n_groups):
    c = GLA_CHUNK
    grp = GLA_GROUP * c
    hk = GLA_HEADS * GLA_DK
    hv = GLA_HEADS * GLA_DV
    s_ref[...] = jnp.zeros_like(s_ref)

    head_k = lax.broadcasted_iota(jnp.int32, (1, hk), 1) // GLA_DK
    head_v = lax.broadcasted_iota(jnp.int32, (1, hv), 1) // GLA_DV
    g_row = lax.broadcasted_iota(jnp.int32, (grp, grp), 0)
    g_col = lax.broadcasted_iota(jnp.int32, (grp, grp), 1)
    tril = (((g_row // c) == (g_col // c)) & (g_col <= g_row)).astype(BF16)
    causal4 = (lax.broadcasted_iota(jnp.int32, (GLA_HEADS * c, c), 1)
               <= lax.broadcasted_iota(jnp.int32, (GLA_HEADS * c, c), 0) % c)
    state_mask = (lax.broadcasted_iota(jnp.int32, (hk, hv), 0) // GLA_DK
                  == lax.broadcasted_iota(jnp.int32, (hk, hv), 1) // GLA_DV)
    norm_mat = jnp.where(lax.broadcasted_iota(jnp.int32, (hv, hv), 0) // GLA_DV
                         == lax.broadcasted_iota(jnp.int32, (hv, hv), 1) // GLA_DV,
                         1.0 / GLA_DV, 0.0).astype(BF16)
    wa2 = wa2_ref[...]
    ba = ba_ref[...]
    ng = ng_ref[...]

    def body(n, carry):
        r0 = pl.multiple_of(n * grp, grp)
        z = z_ref[0, pl.ds(r0, grp), :]
        q, k, v, g, lr = z[:, 0:128], z[:, 128:256], z[:, 256:512], z[:, 512:768], z[:, 768:896]
        log_a = _log_sigmoid(_dot_f32(lr, wa2) + ba) / GLA_TAU
        b = _dot_exact_lhs(tril, log_a, 3)
        b_end = jnp.concatenate([jnp.broadcast_to(b[(ci + 1) * c - 1:(ci + 1) * c, :], (c, hk))
                                 for ci in range(GLA_GROUP)], axis=0)
        q_t = q * (GLA_DK ** -0.5) * jnp.exp(b)
        q_tb = q_t.astype(BF16)
        k_t = (k * jnp.exp(-b)).astype(BF16)
        k_dec = (k * jnp.exp(b_end - b)).astype(BF16)
        vb = v.astype(BF16)
        outs = []
        for ci in range(GLA_GROUP):
            rows = slice(ci * c, (ci + 1) * c)
            q4 = jnp.concatenate([jnp.where(head_k == h, q_t[rows], 0.0) for h in range(GLA_HEADS)],
                                 axis=0).astype(BF16)
            att = jnp.where(causal4, _dot_nt(q4, k_t[rows]), 0.0)
            r = _dot(att.astype(BF16), vb[rows])
            o = _dot(q_tb[rows], s_ref[...].astype(BF16))
            for h in range(GLA_HEADS):
                o = o + jnp.where(head_v == h, r[h * c:(h + 1) * c, :], 0.0)
            outs.append(o)
            kv = jnp.where(state_mask, _dot_tn(k_dec[rows], vb[rows]), 0.0)
            last = b_end[ci * c:ci * c + 1, :]
            dec = jnp.exp(jnp.transpose(jnp.broadcast_to(last, (hk, hk))))
            s_ref[...] = s_ref[...] * jnp.concatenate([dec, dec], axis=1) + kv
        o = jnp.concatenate(outs, axis=0)
        o = o * lax.rsqrt(_dot_exact_rhs(o * o, norm_mat, 2) + EPS) * ng
        o_ref[0, pl.ds(r0, grp), :] = o * (g * jax.nn.sigmoid(g))
        return carry

    lax.fori_loop(0, n_groups, body, 0)


def gla_mixer(z_a, wa2, ba, norm_g):
    bsz, t, _ = z_a.shape
    hk = GLA_HEADS * GLA_DK
    wa2p = jnp.zeros((128, hk), F32).at[:GLA_RANK].set(wa2)
    return pl.pallas_call(
        functools.partial(_gla_kernel, n_groups=t // (GLA_GROUP * GLA_CHUNK)),
        grid=(bsz,),
        in_specs=[pl.BlockSpec((1, t, A_WIDTH), lambda b: (b, 0, 0)),
                  pl.BlockSpec((128, hk), lambda b: (0, 0)),
                  pl.BlockSpec((1, hk), lambda b: (0, 0)),
                  pl.BlockSpec((1, GROUP_WIDTH), lambda b: (0, 0))],
        out_specs=pl.BlockSpec((1, t, GROUP_WIDTH), lambda b: (b, 0, 0)),
        out_shape=jax.ShapeDtypeStruct((bsz, t, GROUP_WIDTH), F32),
        scratch_shapes=[pltpu.VMEM((hk, GROUP_WIDTH), F32)],
        compiler_params=_cparams(("parallel",)),
        name="gla_mixer",
    )(z_a, wa2p, ba.reshape(1, hk), norm_g.reshape(1, GROUP_WIDTH))


def _pool_kernel(u_ref, w_ref, b_ref, sc_ref, o_ref):
    u = u_ref[0]
    t, gw = u.shape
    row = lax.broadcasted_iota(jnp.int32, (t, gw), 0)
    grp = lax.broadcasted_iota(jnp.int32, (t, gw), 1) // POOL_CG

    def shifted(x, k):
        return jnp.where(row >= k, pltpu.roll(x, k, axis=0), 0.0)

    s = u
    p = jnp.zeros_like(u)
    for gi, win in enumerate(POOL_WINDOWS):
        half = win // 2
        s = s + shifted(s, half)
        cnt = jnp.minimum(row + 1, win).astype(F32)
        p = jnp.where(grp == gi, s / cnt - u, p)
    y = _dot(p.astype(BF16), w_ref[...]) + b_ref[...]
    o_ref[0] = y * sc_ref[...]


def pool_mixer(z_b, w, b, scale):
    assert POOL_WINDOWS == (2, 4, 8, 16)
    bsz, t, gw = z_b.shape
    w_bd = jnp.zeros((gw, gw), F32)
    for gi in range(len(POOL_WINDOWS)):
        w_bd = w_bd.at[gi * POOL_CG:(gi + 1) * POOL_CG, gi * POOL_CG:(gi + 1) * POOL_CG].set(w[gi])
    return pl.pallas_call(
        _pool_kernel,
        grid=(bsz,),
        in_specs=[pl.BlockSpec((1, t, gw), lambda i: (i, 0, 0)),
                  pl.BlockSpec((gw, gw), lambda i: (0, 0)),
                  pl.BlockSpec((1, gw), lambda i: (0, 0)),
                  pl.BlockSpec((1, gw), lambda i: (0, 0))],
        out_specs=pl.BlockSpec((1, t, gw), lambda i: (i, 0, 0)),
        out_shape=jax.ShapeDtypeStruct((bsz, t, gw), F32),
        compiler_params=_cparams(("parallel",)),
        name="pool_mixer",
    )(z_b, w_bd.astype(BF16), b.reshape(1, gw), scale.reshape(1, gw))


def _sgu_kernel(z_ref, lg_ref, lb_ref, w_ref, bm_ref, o_ref, *, chunks):
    c = SGU_CHUNK
    gw = GROUP_WIDTH
    rows = SGU_GROUPS * c
    tri = (lax.broadcasted_iota(jnp.int32, (rows, c), 1)
           <= lax.broadcasted_iota(jnp.int32, (rows, c), 0) % c)
    ws = jnp.where(tri, w_ref[...], 0.0).astype(BF16)
    grp = lax.broadcasted_iota(jnp.int32, (1, gw), 1) // SGU_CG
    for ci in range(chunks):
        z = jax.nn.gelu(z_ref[0, ci * c:(ci + 1) * c, :], approximate=True)
        u, v = z[:, :gw], z[:, gw:]
        mu = jnp.mean(v, axis=-1, keepdims=True)
        var = jnp.mean(jnp.square(v - mu), axis=-1, keepdims=True)
        vn = (v - mu) * lax.rsqrt(var + EPS) * lg_ref[...] + lb_ref[...]
        r = _dot(ws, vn.astype(BF16))
        mixed = bm_ref[...]
        for g in range(SGU_GROUPS):
            mixed = mixed + jnp.where(grp == g, r[g * c:(g + 1) * c, :], 0.0)
        o_ref[0, ci * c:(ci + 1) * c, :] = u * mixed


def sgu_mixer(z_d, ln_g, ln_b, w_s, b_s, chunks=4):
    bsz, t, _ = z_d.shape
    gw = GROUP_WIDTH
    tt = chunks * SGU_CHUNK
    bias = jnp.repeat(b_s.T, SGU_CG, axis=1)
    return pl.pallas_call(
        functools.partial(_sgu_kernel, chunks=chunks),
        grid=(bsz, t // tt),
        in_specs=[pl.BlockSpec((1, tt, 2 * gw), lambda b, i: (b, i, 0)),
                  pl.BlockSpec((1, gw), lambda b, i: (0, 0)),
                  pl.BlockSpec((1, gw), lambda b, i: (0, 0)),
                  pl.BlockSpec((SGU_GROUPS * SGU_CHUNK, SGU_CHUNK), lambda b, i: (0, 0)),
                  pl.BlockSpec((SGU_CHUNK, gw), lambda b, i: (0, 0))],
        out_specs=pl.BlockSpec((1, tt, gw), lambda b, i: (b, i, 0)),
        out_shape=jax.ShapeDtypeStruct((bsz, t, gw), F32),
        compiler_params=_cparams(("parallel", "parallel")),
        name="sgu_mixer",
    )(z_d, ln_g.reshape(1, gw), ln_b.reshape(1, gw),
      w_s.reshape(SGU_GROUPS * SGU_CHUNK, SGU_CHUNK), bias)


def _bucket_table():
    assert REL_MAX_DIST <= DSA_BLOCK + 1
    s = np.arange(DSA_BLOCK)[:, None]
    t = np.arange(DSA_BLOCK)[None, :]
    dist = np.stack([t - s, DSA_BLOCK + t - s, 2 * DSA_BLOCK + t - s])
    n = np.maximum(dist, 0)
    max_exact = REL_BUCKETS // 2
    nf = np.maximum(n, 1).astype(np.float32)
    large = max_exact + (np.log(nf / np.float32(max_exact)) / np.float32(math.log(REL_MAX_DIST / max_exact))
                         * np.float32(REL_BUCKETS - max_exact)).astype(np.int32)
    return np.where(n < max_exact, n, np.minimum(large, REL_BUCKETS - 1)).astype(np.int32)


def _bias_table_kernel(rb_ref, bucket_ref, o_ref):
    for back in range(3):
        bucket = bucket_ref[back]
        for h in range(DSA_HEADS):
            acc = jnp.zeros(bucket.shape, F32)
            for b in range(REL_BUCKETS):
                acc = jnp.where(bucket == b, rb_ref[b * DSA_HEADS + h], acc)
            o_ref[back, h] = acc


def rel_bias_tables(rel_bias):
    blk = DSA_BLOCK
    return pl.pallas_call(
        _bias_table_kernel,
        in_specs=[pl.BlockSpec(memory_space=pltpu.SMEM),
                  pl.BlockSpec((3, blk, blk), lambda: (0, 0, 0))],
        out_specs=pl.BlockSpec((3, DSA_HEADS, blk, blk), lambda: (0, 0, 0, 0)),
        out_shape=jax.ShapeDtypeStruct((3, DSA_HEADS, blk, blk), F32),
        name="rel_bias_tables",
    )(rel_bias.reshape(-1), jnp.asarray(_bucket_table()))


def _dsa_kernel(q_ref, k_ref, v_ref, qi_ref, kw_ref, qw_ref, toe_ref, o_ref,
                kpl_ref, vt_ref, keys_ref, khi_ref, klo_ref, acc_ref, am_ref, lg_ref, p_ref,
                *, topk, n_blocks, idx_bits):
    blk = DSA_BLOCK
    sub = DSA_SUB
    i = pl.program_id(1)
    hd = DSA_HEADS * DSA_DH
    heads_per_half = 128 // IDX_DIM

    @pl.when(i == 0)
    def _():
        lane = lax.broadcasted_iota(jnp.int32, (blk, 128), 1)

        def build(kb, c):
            r0 = pl.multiple_of(kb * blk, blk)
            ki = jnp.where(lane < IDX_DIM, kw_ref[0, pl.ds(r0, blk), :], 0.0)
            for j in range(heads_per_half):
                kpl_ref[kb, j] = (ki if j == 0 else pltpu.roll(ki, j * IDX_DIM, axis=1)).astype(BF16)
            vt_ref[kb] = jnp.transpose(v_ref[0, pl.ds(r0, blk), :].astype(F32)).astype(BF16)
            return c

        lax.fori_loop(0, n_blocks, build, 0)

    s_loc = lax.broadcasted_iota(jnp.int32, (blk, blk), 0)
    t_loc = lax.broadcasted_iota(jnp.int32, (blk, blk), 1)
    n_vis = i + 1

    qi_t = jnp.transpose(qi_ref[0].astype(F32)).astype(BF16)
    qi_halves = [qi_t[:128, :], qi_t[128:, :]]
    w_t = jnp.transpose(qw_ref[0])
    w_rows = [w_t[IDX_DIM + h:IDX_DIM + h + 1, :] * (IDX_HEADS ** -0.5) * (IDX_DIM ** -0.5)
              for h in range(IDX_HEADS)]
    s_sub = lax.broadcasted_iota(jnp.int32, (sub, blk), 0)
    t_sub = lax.broadcasted_iota(jnp.int32, (sub, blk), 1)

    def score_body(kb, c):
        for ci in range(blk // sub):
            rows = slice(ci * sub, (ci + 1) * sub)
            sc = jnp.zeros((sub, blk), F32)
            for half in range(2):
                for j in range(heads_per_half):
                    d = _dot(kpl_ref[kb, j, rows, :], qi_halves[half])
                    sc = sc + jnp.maximum(d, 0.0) * w_rows[half * heads_per_half + j]
            sc = jnp.where(sc == 0.0, 0.0, sc)
            bits = pltpu.bitcast(sc, jnp.int32)
            key = jnp.where(bits < 0, bits ^ jnp.int32(0x7FFFFFFF), bits)
            vis = (kb < i) | (s_sub + ci * sub <= t_sub)
            key = jnp.where(vis, key, jnp.int32(INT_MIN))
            keys_ref[kb, rows, :] = key
            khi_ref[kb, rows, :] = lax.shift_right_arithmetic(key, 16).astype(jnp.int16)
            klo_ref[kb, rows, :] = ((key & 0xFFFF) - 2 ** 15).astype(jnp.int16)
        return c

    lax.fori_loop(0, n_vis, score_body, 0)

    def count(pred):
        def body(kb, acc):
            hit = jnp.where(pred(keys_ref[kb], kb), 1.0, 0.0)
            return acc + jnp.sum(hit.reshape(blk // 32, 32, blk), axis=0)
        acc = lax.fori_loop(0, n_vis, body, jnp.zeros((32, blk), F32))
        return jnp.sum(acc, axis=0, keepdims=True)

    def count16(ref, cand):
        cand = cand.astype(jnp.int16)

        def body(kb, acc):
            hit = jnp.where(ref[kb] >= cand, jnp.int16(1), jnp.int16(0))
            for j in range(blk // 32):
                acc = acc + hit[j * 32:(j + 1) * 32]
            return acc
        acc = lax.fori_loop(0, n_vis, body, jnp.zeros((32, blk), jnp.int16))
        return jnp.sum(acc.astype(jnp.int32).astype(F32), axis=0, keepdims=True)

    def search16(ref, offset):
        lowest = jnp.full((1, blk), -(2 ** 15), jnp.int32)
        base = jnp.where(offset + count16(ref, jnp.zeros((1, blk), jnp.int32)) >= kf, 0, lowest)

        def bit_body(it, base):
            cand = base | lax.shift_left(jnp.int32(1), 14 - it)
            return jnp.where(offset + count16(ref, cand) >= kf, cand, base)
        return lax.fori_loop(0, 15, bit_body, base)

    kf = float(topk)
    zero = jnp.zeros((1, blk), jnp.int32)
    thr_hi = search16(khi_ref, 0.0)
    thr_hi16 = thr_hi.astype(jnp.int16)

    def low_body(kb, above):
        hi = khi_ref[kb]
        klo_ref[kb] = jnp.where(hi == thr_hi16, klo_ref[kb], jnp.int16(-(2 ** 15)))
        hit = jnp.where(hi > thr_hi16, jnp.int16(1), jnp.int16(0))
        for j in range(blk // 32):
            above = above + hit[j * 32:(j + 1) * 32]
        return above

    above = lax.fori_loop(0, n_vis, low_body, jnp.zeros((32, blk), jnp.int16))
    n_above = jnp.sum(above.astype(jnp.int32).astype(F32), axis=0, keepdims=True)
    thr_lo = search16(klo_ref, n_above)
    thr = lax.shift_left(thr_hi, 16) | (thr_lo + 2 ** 15)
    thr_sel = jnp.maximum(thr, jnp.int32(INT_MIN + 1))

    n_gt = count(lambda key, kb: key > thr)
    n_eq = count(lambda key, kb: key == thr)
    need = kf - n_gt
    excess = jnp.where((n_eq > need) & (thr > jnp.int32(INT_MIN)), 1.0, 0.0)

    def tie_search():
        def tie_body(it, j):
            cand = j | lax.shift_left(jnp.int32(1), idx_bits - 1 - it)
            below = count(lambda key, kb: (key == thr) & (kb * blk + s_loc < cand))
            return jnp.where(below < need, cand, j)
        return lax.fori_loop(0, idx_bits, tie_body, zero)

    last = lax.cond(jnp.max(excess) > 0.0, tie_search, lambda: jnp.full((1, blk), 2 ** 30, jnp.int32))

    assert DSA_DH ** -0.5 == 0.125
    q_t = jnp.transpose(q_ref[0].astype(F32) * (DSA_DH ** -0.5))
    row_h = lax.broadcasted_iota(jnp.int32, (hd, 1), 0) // DSA_DH
    q_heads = [jnp.where(row_h == h, q_t, 0.0).astype(BF16) for h in range(DSA_HEADS)]
    acc_ref[...] = jnp.zeros_like(acc_ref)
    n_sub = blk // sub

    def att_body(kb, carry):
        ms, ls = carry
        back = jnp.minimum(i - kb, 2)
        for ci in range(n_sub):
            rows = slice(ci * sub, (ci + 1) * sub)
            key = keys_ref[kb, rows, :]
            sel = (key > thr_sel) | ((key == thr_sel) & (kb * blk + ci * sub + s_sub <= last))
            am_ref[rows, :] = jnp.where(sel, 0.0, NEG_BIG)
        new_ms, alphas = [], []
        for h in range(DSA_HEADS):
            pm = jnp.full((8, blk), NEG_BIG, F32)
            for ci in range(n_sub):
                rows = slice(ci * sub, (ci + 1) * sub)
                k_rows = k_ref[0, pl.ds(pl.multiple_of(kb * blk + ci * sub, sub), sub), :]
                lg = _dot(k_rows, q_heads[h]) + toe_ref[back, h, rows, :] + am_ref[rows, :]
                lg_ref[h, rows, :] = lg
                pm = jnp.maximum(pm, jnp.max(lg.reshape(sub // 8, 8, blk), axis=0))
            m_new = jnp.maximum(ms[h], jnp.max(pm, axis=0, keepdims=True))
            new_ms.append(m_new)
            alphas.append(jnp.exp(ms[h] - m_new))
        new_ls = []
        for h in range(DSA_HEADS):
            ps = jnp.zeros((8, blk), F32)
            for ci in range(n_sub):
                rows = slice(ci * sub, (ci + 1) * sub)
                p = jnp.exp(lg_ref[h, rows, :] - new_ms[h])
                ps = ps + jnp.sum(p.reshape(sub // 8, 8, blk), axis=0)
                p_ref[h, rows, :] = p.astype(BF16)
            new_ls.append(ls[h] * alphas[h] + jnp.sum(ps, axis=0, keepdims=True))
        for h in range(DSA_HEADS):
            hrows = slice(h * DSA_DH, (h + 1) * DSA_DH)
            acc_ref[hrows, :] = acc_ref[hrows, :] * alphas[h] + _dot(vt_ref[kb, hrows, :], p_ref[h])
        return tuple(new_ms), tuple(new_ls)

    init = (tuple(jnp.full((1, blk), 0.01 * NEG_BIG, F32) for _ in range(DSA_HEADS)),
            tuple(jnp.zeros((1, blk), F32) for _ in range(DSA_HEADS)))
    _, ls = lax.fori_loop(0, n_vis, att_body, init)
    for h in range(DSA_HEADS):
        rows = slice(h * DSA_DH, (h + 1) * DSA_DH)
        acc_ref[rows, :] = acc_ref[rows, :] * (1.0 / ls[h])
    o_ref[0] = jnp.transpose(acc_ref[...])


def dsa_mixer(z_c, z_aux, toe):
    bsz, t, _ = z_c.shape
    blk = DSA_BLOCK
    n_blocks = t // blk
    assert t % blk == 0
    topk = min(DSA_TOPK_MAX, t // 4)
    hd = DSA_HEADS * DSA_DH
    kernel = functools.partial(_dsa_kernel, topk=topk, n_blocks=n_blocks,
                               idx_bits=max(1, (t - 1).bit_length()))
    return pl.pallas_call(
        kernel,
        grid=(bsz, n_blocks),
        in_specs=[pl.BlockSpec((1, blk, hd), lambda b, i: (b, i, 0)),
                  pl.BlockSpec((1, t, hd), lambda b, i: (b, 0, 1)),
                  pl.BlockSpec((1, t, hd), lambda b, i: (b, 0, 2)),
                  pl.BlockSpec((1, blk, hd), lambda b, i: (b, i, 3)),
                  pl.BlockSpec((1, t, C_AUX), lambda b, i: (b, 0, 0)),
                  pl.BlockSpec((1, blk, C_AUX), lambda b, i: (b, i, 0)),
                  pl.BlockSpec((3, DSA_HEADS, blk, blk), lambda b, i: (0, 0, 0, 0))],
        out_specs=pl.BlockSpec((1, blk, hd), lambda b, i: (b, i, 0)),
        out_shape=jax.ShapeDtypeStruct((bsz, t, hd), F32),
        scratch_shapes=[pltpu.VMEM((n_blocks, 128 // IDX_DIM, blk, 128), BF16),
                        pltpu.VMEM((n_blocks, hd, blk), BF16),
                        pltpu.VMEM((n_blocks, blk, blk), jnp.int32),
                        pltpu.VMEM((n_blocks, blk, blk), jnp.int16),
                        pltpu.VMEM((n_blocks, blk, blk), jnp.int16),
                        pltpu.VMEM((hd, blk), F32),
                        pltpu.VMEM((blk, blk), F32),
                        pltpu.VMEM((DSA_HEADS, blk, blk), F32),
                        pltpu.VMEM((DSA_HEADS, blk, blk), BF16)],
        compiler_params=_cparams(("parallel", "arbitrary")),
        name="dsa_mixer",
    )(z_c, z_c, z_c, z_c, z_aux, z_aux, toe)


def _out_proj_kernel(h_ref, a_ref, b_ref, c_ref, d_ref, w_ref, o_ref):
    gw = GROUP_WIDTH
    acc = h_ref[...]
    for gi, r in enumerate((a_ref, b_ref, c_ref, d_ref)):
        acc = acc + _dot(r[...].astype(BF16), w_ref[gi * gw:(gi + 1) * gw, :])
    o_ref[...] = acc


def out_proj(h, outs, w, tm=1024):
    m, d = h.shape
    tm = min(tm, m)
    gw = GROUP_WIDTH
    return pl.pallas_call(
        _out_proj_kernel,
        grid=(m // tm,),
        in_specs=[pl.BlockSpec((tm, d), lambda i: (i, 0))]
                 + [pl.BlockSpec((tm, gw), lambda i: (i, 0))] * 4
                 + [pl.BlockSpec((4 * gw, d), lambda i: (0, 0))],
        out_specs=pl.BlockSpec((tm, d), lambda i: (i, 0)),
        out_shape=jax.ShapeDtypeStruct((m, d), F32),
        compiler_params=_cparams(("parallel",)),
        name="out_proj",
    )(h, *outs, w)


def _xattn_kernel(h_ref, g_ref, wq_ref, k_ref, v_ref, wo_ref, o_ref):
    x = h_ref[0]
    hd = XA_HEADS * XA_DH
    q = _dot(_rms(x, g_ref[...]).astype(BF16), wq_ref[...]).astype(BF16)
    k = k_ref[0]
    v = v_ref[0]
    lane_h = lax.broadcasted_iota(jnp.int32, (1, hd), 1) // XA_DH
    o = jnp.zeros((x.shape[0], hd), F32)
    for h in range(XA_HEADS):
        s = _dot_nt(q, jnp.where(lane_h == h, k, 0.0).astype(BF16)) * (XA_DH ** -0.5)
        p = jnp.exp(s - jnp.max(s, axis=-1, keepdims=True))
        p = p / jnp.sum(p, axis=-1, keepdims=True)
        o = o + _dot(p.astype(BF16), jnp.where(lane_h == h, v, 0.0).astype(BF16))
    o_ref[0] = x + _dot(o.astype(BF16), wo_ref[...])


def cross_attn(h3, g, wq, k, v, wo, tm=1024):
    bsz, t, d = h3.shape
    tm = min(tm, t)
    mlen = k.shape[1]
    hd = XA_HEADS * XA_DH
    return pl.pallas_call(
        _xattn_kernel,
        grid=(bsz, t // tm),
        in_specs=[pl.BlockSpec((1, tm, d), lambda b, i: (b, i, 0)),
                  pl.BlockSpec((1, d), lambda b, i: (0, 0)),
                  pl.BlockSpec((d, hd), lambda b, i: (0, 0)),
                  pl.BlockSpec((1, mlen, hd), lambda b, i: (b, 0, 0)),
                  pl.BlockSpec((1, mlen, hd), lambda b, i: (b, 0, 0)),
                  pl.BlockSpec((hd, d), lambda b, i: (0, 0))],
        out_specs=pl.BlockSpec((1, tm, d), lambda b, i: (b, i, 0)),
        out_shape=jax.ShapeDtypeStruct((bsz, t, d), F32),
        compiler_params=_cparams(("parallel", "parallel")),
        name="cross_attn",
    )(h3, g.reshape(1, d), wq, k, v, wo)


def _ffn_kernel(h_ref, g_ref, w1_ref, w3_ref, w2_ref, o_ref, hn_ref, acc_ref):
    f = pl.program_id(1)

    @pl.when(f == 0)
    def _():
        hn_ref[...] = _rms(h_ref[...], g_ref[...]).astype(BF16)
        acc_ref[...] = jnp.zeros_like(acc_ref)

    a = _dot(hn_ref[...], w1_ref[...])
    b = _dot(hn_ref[...], w3_ref[...])
    acc_ref[...] += _dot((a * jax.nn.sigmoid(a) * b).astype(BF16), w2_ref[...])

    @pl.when(f == pl.num_programs(1) - 1)
    def _():
        o_ref[...] = h_ref[...] + acc_ref[...]


def ffn(h, g, w1, w3, w2, tm=512, tf=1408):
    m, d = h.shape
    tm = min(tm, m)
    nf = w1.shape[1]
    assert nf % tf == 0
    return pl.pallas_call(
        _ffn_kernel,
        grid=(m // tm, nf // tf),
        in_specs=[pl.BlockSpec((tm, d), lambda i, f: (i, 0)),
                  pl.BlockSpec((1, d), lambda i, f: (0, 0)),
                  pl.BlockSpec((d, tf), lambda i, f: (0, f)),
                  pl.BlockSpec((d, tf), lambda i, f: (0, f)),
                  pl.BlockSpec((tf, d), lambda i, f: (f, 0))],
        out_specs=pl.BlockSpec((tm, d), lambda i, f: (i, 0)),
        out_shape=jax.ShapeDtypeStruct((m, d), F32),
        scratch_shapes=[pltpu.VMEM((tm, d), BF16), pltpu.VMEM((tm, d), F32)],
        compiler_params=_cparams(("parallel", "arbitrary")),
        name="ffn",
    )(h, g.reshape(1, d), w1, w3, w2)


GATE_LANES = 128
MOE_CHUNK = 256
MOE_ROWS = 128
MOE_TOKENS = 2048
MOE_ALIGN = 16


def _router_kernel(h_ref, g_ref, wr_ref, hn_ref, gate_ref, cnt_ref):
    hn = _rms(h_ref[...], g_ref[...])
    hn_ref[...] = hn.astype(BF16)
    tm = hn.shape[0]
    lane = lax.broadcasted_iota(jnp.int32, (tm, GATE_LANES), 1)
    logits = jnp.where(lane < N_EXPERTS, _dot_3pass(hn, wr_ref[...]), -jnp.inf)
    m1 = jnp.max(logits, axis=-1, keepdims=True)
    i1 = jnp.min(jnp.where(logits == m1, lane, GATE_LANES), axis=-1, keepdims=True)
    rest = jnp.where(lane == i1, -jnp.inf, logits)
    m2 = jnp.max(rest, axis=-1, keepdims=True)
    i2 = jnp.min(jnp.where(rest == m2, lane, GATE_LANES), axis=-1, keepdims=True)
    e2 = jnp.exp(m2 - m1)
    g1 = 1.0 / (1.0 + e2)
    gates = jnp.where(lane == i1, g1, 0.0) + jnp.where(lane == i2, e2 * g1, 0.0)
    gates = jnp.where(lane == N_EXPERTS, i1.astype(F32), gates)
    gate_ref[...] = jnp.where(lane == N_EXPERTS + 1, i2.astype(F32), gates)
    sel = jnp.where((lane == i1) | (lane == i2), 1.0, 0.0)
    for c in range(tm // MOE_CHUNK):
        cnt_ref[c] = jnp.sum(sel[c * MOE_CHUNK:(c + 1) * MOE_CHUNK, :], axis=0, keepdims=True)


def route_tokens(h, g, wr, tm=512):
    m, d = h.shape
    tm = min(tm, m)
    assert tm % MOE_CHUNK == 0
    wrp = jnp.zeros((d, GATE_LANES), F32).at[:, :N_EXPERTS].set(wr)
    return pl.pallas_call(
        _router_kernel,
        grid=(m // tm,),
        in_specs=[pl.BlockSpec((tm, d), lambda i: (i, 0)),
                  pl.BlockSpec((1, d), lambda i: (0, 0)),
                  pl.BlockSpec((d, GATE_LANES), lambda i: (0, 0))],
        out_specs=[pl.BlockSpec((tm, d), lambda i: (i, 0)),
                   pl.BlockSpec((tm, GATE_LANES), lambda i: (i, 0)),
                   pl.BlockSpec((tm // MOE_CHUNK, 1, GATE_LANES), lambda i: (i, 0, 0))],
        out_shape=[jax.ShapeDtypeStruct((m, d), BF16), jax.ShapeDtypeStruct((m, GATE_LANES), F32),
                   jax.ShapeDtypeStruct((m // MOE_CHUNK, 1, GATE_LANES), F32)],
        compiler_params=_cparams(("parallel",)),
        name="router",
    )(h, g.reshape(1, d), wrp)


def _moe_kernel(cb_ref, hn_ref, gate_ref, w1_ref, w3_ref, w2_ref, y_ref,
                rank_row, rk_ref, gs_ref, xs_ref, yacc_ref, rc_ref, gc_ref):
    t = pl.program_id(0)
    e = pl.program_id(1)
    f = pl.program_id(2)
    ts, d = hn_ref.shape
    ch, rb = MOE_CHUNK, MOE_ROWS
    win = 2 * rb
    n_ch = ts // ch
    lane = lax.broadcasted_iota(jnp.int32, (1, GATE_LANES), 1)

    def before(c):
        return cb_ref[(t * (n_ch + 1) + c) * N_EXPERTS + e]

    n_blocks = (before(n_ch) + rb - 1) // rb
    n_windows = (n_blocks * rb + win - 1) // win + 1

    def windows(c):
        lo, hi = before(c), before(c + 1)
        s0 = (lo // MOE_ALIGN) * MOE_ALIGN
        return s0, jnp.where(hi > lo, (hi - s0 + win - 1) // win, 0)

    spans = [windows(c) for c in range(n_ch)]
    single = functools.reduce(jnp.logical_and, [n_win <= 1 for _, n_win in spans])

    def for_each_window(visit):
        @pl.when(single)
        def _():
            for c, (s0, _) in enumerate(spans):
                visit(c, s0)

        @pl.when(jnp.logical_not(single))
        def _():
            for c, (s0, n_win) in enumerate(spans):
                def body(j, carry, c=c, s0=s0):
                    visit(c, s0 + j * win)
                    return carry
                lax.fori_loop(0, n_win, body, 0)

    @pl.when((e == 0) & (f == 0))
    def _():
        y_ref[...] = jnp.zeros_like(y_ref)
        strict_lower = (lax.broadcasted_iota(jnp.int32, (ch, ch), 1)
                        < lax.broadcasted_iota(jnp.int32, (ch, ch), 0)).astype(BF16)
        lane_f = lane.astype(F32)
        offs = jnp.ones((1, GATE_LANES), F32)
        for c in range(n_ch):
            rows = slice(c * ch, (c + 1) * ch)
            g = gate_ref[rows, :]
            sel = jnp.where((lane_f == g[:, N_EXPERTS:N_EXPERTS + 1])
                            | (lane_f == g[:, N_EXPERTS + 1:N_EXPERTS + 2]), 1.0, 0.0)
            r = _dot(strict_lower, sel.astype(BF16)) + offs
            r = jnp.where(sel > 0.0, r, 0.0)
            rank_row[:, rows] = jnp.transpose(r) - 1.0
            high = jnp.floor(r * (1.0 / 256.0))
            rk_ref[0, rows, :] = high.astype(BF16)
            rk_ref[1, rows, :] = (r - 256.0 * high).astype(BF16)
            for j, part in enumerate(_split_bf16(g, 3)):
                gs_ref[j, rows, :] = part
            offs = offs + jnp.sum(sel, axis=0, keepdims=True)

    @pl.when(f == 0)
    def _():
        pick = (lax.broadcasted_iota(jnp.int32, (GATE_LANES, GATE_LANES), 0) == e).astype(BF16)
        for c in range(n_ch):
            rows = slice(c * ch, (c + 1) * ch)
            rc_ref[rows, :] = 256.0 * _dot(rk_ref[0, rows, :], pick) + _dot(rk_ref[1, rows, :], pick) - 1.0
            gc_ref[rows, :] = (_dot(gs_ref[0, rows, :], pick) + _dot(gs_ref[1, rows, :], pick)
                               + _dot(gs_ref[2, rows, :], pick))

        def clear(j, carry):
            r0 = pl.multiple_of(j * win, win)
            xs_ref[pl.ds(r0, win), :] = jnp.zeros((win, d), BF16)
            yacc_ref[pl.ds(r0, win), :] = jnp.zeros((win, d), F32)
            return carry

        lax.fori_loop(0, n_windows, clear, 0)

        row_id = lax.broadcasted_iota(jnp.int32, (win, ch), 0)

        def gather_window(c, s):
            s = pl.multiple_of(s, MOE_ALIGN)
            ranks = rank_row[pl.ds(e, 1), c * ch:(c + 1) * ch]
            onehot = jnp.where(ranks == (s + row_id).astype(F32), 1.0, 0.0).astype(BF16)
            xs_ref[pl.ds(s, win), :] += _dot(onehot, hn_ref[c * ch:(c + 1) * ch, :]).astype(BF16)

        for_each_window(gather_window)

    def ffn_rows(r0, rows):
        x = xs_ref[pl.ds(r0, rows), :]
        a = _dot(x, w1_ref[0])
        g3 = _dot(x, w3_ref[0])
        yacc_ref[pl.ds(r0, rows), :] += _dot((a * jax.nn.sigmoid(a) * g3).astype(BF16), w2_ref[0])

    n_quads = n_blocks // 4

    def ffn_quad(j, carry):
        ffn_rows(pl.multiple_of(j * (4 * rb), 4 * rb), 4 * rb)
        return carry

    lax.fori_loop(0, n_quads, ffn_quad, 0)
    tail = pl.multiple_of(n_quads * (4 * rb), 4 * rb)

    @pl.when((n_blocks & 2) != 0)
    def _():
        ffn_rows(tail, 2 * rb)

    @pl.when((n_blocks & 1) != 0)
    def _():
        ffn_rows(pl.multiple_of(tail + (n_blocks & 2) * rb, rb), rb)

    @pl.when(f == pl.num_programs(2) - 1)
    def _():
        def to_bf16(j, carry):
            r0 = pl.multiple_of(j * win, win)
            xs_ref[pl.ds(r0, win), :] = yacc_ref[pl.ds(r0, win), :].astype(BF16)
            return carry

        lax.fori_loop(0, n_windows, to_bf16, 0)

        lane_id = lax.broadcasted_iota(jnp.int32, (ch, rb), 1).astype(F32)

        def scatter_window(c, s):
            rows = slice(c * ch, (c + 1) * ch)
            s = pl.multiple_of(s, MOE_ALIGN)
            rank = rc_ref[rows, :] - s.astype(F32)
            onehot = jnp.concatenate([jnp.where(rank == lane_id, 1.0, 0.0),
                                      jnp.where(rank == lane_id + float(rb), 1.0, 0.0)],
                                     axis=1).astype(BF16)
            gate = jnp.concatenate([gc_ref[rows, :]] * (d // GATE_LANES), axis=1)
            y_ref[rows, :] += gate * _dot(onehot, xs_ref[pl.ds(s, win), :])

        for_each_window(scatter_window)


def moe(hn, gates, cnt, w1, w3, w2, tf=896):
    m, d = hn.shape
    ts = min(MOE_TOKENS, m)
    n_exp, _, nf = w1.shape
    assert nf % tf == 0 and m % ts == 0 and ts % MOE_CHUNK == 0 and MOE_ROWS == GATE_LANES
    n_ch = ts // MOE_CHUNK
    counts = cnt.reshape(m // ts, n_ch, GATE_LANES)[:, :, :n_exp].astype(jnp.int32)
    before = jnp.concatenate([jnp.zeros((m // ts, 1, n_exp), jnp.int32), jnp.cumsum(counts, axis=1)], axis=1)
    grid_spec = pltpu.PrefetchScalarGridSpec(
        num_scalar_prefetch=1,
        grid=(m // ts, n_exp, nf // tf),
        in_specs=[pl.BlockSpec((ts, d), lambda t, e, f, cb: (t, 0), pipeline_mode=pl.Buffered(1)),
                  pl.BlockSpec((ts, GATE_LANES), lambda t, e, f, cb: (t, 0), pipeline_mode=pl.Buffered(1)),
                  pl.BlockSpec((1, d, tf), lambda t, e, f, cb: (e, 0, f)),
                  pl.BlockSpec((1, d, tf), lambda t, e, f, cb: (e, 0, f)),
                  pl.BlockSpec((1, tf, d), lambda t, e, f, cb: (e, f, 0))],
        out_specs=pl.BlockSpec((ts, d), lambda t, e, f, cb: (t, 0), pipeline_mode=pl.Buffered(1)),
        scratch_shapes=[pltpu.VMEM((GATE_LANES, ts), F32),
                        pltpu.VMEM((2, ts, GATE_LANES), BF16),
                        pltpu.VMEM((3, ts, GATE_LANES), BF16),
                        pltpu.VMEM((ts + 2 * MOE_ROWS, d), BF16),
                        pltpu.VMEM((ts + 2 * MOE_ROWS, d), F32),
                        pltpu.VMEM((ts, MOE_ROWS), F32),
                        pltpu.VMEM((ts, GATE_LANES), F32)])
    return pl.pallas_call(
        _moe_kernel,
        grid_spec=grid_spec,
        out_shape=jax.ShapeDtypeStruct((m, d), F32),
        compiler_params=_cparams(("parallel", "arbitrary", "arbitrary")),
        name="moe",
    )(before.reshape(-1), hn, gates, w1, w3, w2)


def moe_block(h, g, wr, w1, w3, w2):
    hn, gates, cnt = route_tokens(h, g, wr)
    return moe(hn, gates, cnt, w1.astype(BF16), w3.astype(BF16), w2.astype(BF16))


def _final_norm_kernel(g_ref, *refs):
    *x_refs, o_ref = refs
    x = x_refs[0][...]
    for r in x_refs[1:]:
        x = x + r[...]
    o_ref[...] = _rms(x, g_ref[...])


def final_rmsnorm(g, *xs, tm=1024):
    m, d = xs[0].shape
    tm = min(tm, m)
    return pl.pallas_call(
        _final_norm_kernel,
        grid=(m // tm,),
        in_specs=[pl.BlockSpec((1, d), lambda i: (0, 0))] + [pl.BlockSpec((tm, d), lambda i: (i, 0))] * len(xs),
        out_specs=pl.BlockSpec((tm, d), lambda i: (i, 0)),
        out_shape=jax.ShapeDtypeStruct((m, d), F32),
        compiler_params=_cparams(("parallel",)),
        name="final_norm",
    )(g.reshape(1, d), *xs)


def _pad_cols(w, width):
    return jnp.pad(w, ((0, 0), (0, width - w.shape[1])))


def _in_proj_weight(w_in):
    a_end = 2 * GLA_HEADS * GLA_DK + 2 * GROUP_WIDTH + GLA_RANK
    b_end = a_end + GROUP_WIDTH
    c_end = b_end + 3 * GROUP_WIDTH + IDX_HEADS * IDX_DIM + IDX_DIM + IDX_HEADS
    assert w_in.shape[1] == c_end + 2 * GROUP_WIDTH
    return jnp.concatenate([_pad_cols(w_in[:, :a_end], A_WIDTH), w_in[:, a_end:b_end],
                            _pad_cols(w_in[:, b_end:c_end], C_MAIN + C_AUX), w_in[:, c_end:]], axis=1).astype(BF16)


def hybrid_layer(h, mem2, toe, p, bsz, t):
    m, d = h.shape
    z_a, z_b, z_c, z_aux, z_d = norm_matmul(h, p["norm_mix"], _in_proj_weight(p["w_in"]),
                                            (A_WIDTH, B_WIDTH, C_MAIN, C_AUX, D_WIDTH),
                                            (F32, F32, BF16, F32, F32))
    o_a = gla_mixer(z_a.reshape(bsz, t, A_WIDTH), p["gla_wa2"], p["gla_ba"], p["gla_norm"])
    o_b = pool_mixer(z_b.reshape(bsz, t, B_WIDTH), p["pool_w"], p["pool_b"].reshape(-1), p["pool_scale"])
    o_c = dsa_mixer(z_c.reshape(bsz, t, C_MAIN), z_aux.reshape(bsz, t, C_AUX), toe)
    o_d = sgu_mixer(z_d.reshape(bsz, t, D_WIDTH), p["sgu_ln_g"], p["sgu_ln_b"], p["sgu_w"], p["sgu_b"])
    h = out_proj(h, [o.reshape(m, GROUP_WIDTH) for o in (o_a, o_b, o_c, o_d)], p["w_out"].astype(BF16))

    hd = XA_HEADS * XA_DH
    k, v = norm_matmul(mem2, p["norm_mem"], p["xa_wkv"].astype(BF16), (hd, hd))
    mlen = mem2.shape[0] // bsz
    h = cross_attn(h.reshape(bsz, t, d), p["norm_xa"], p["xa_wq"].astype(BF16),
                   k.reshape(bsz, mlen, hd), v.reshape(bsz, mlen, hd), p["xa_wo"].astype(BF16))
    return h.reshape(m, d)


def kernel(x, mem, rel_bias, final_norm, norm_mix, w_in, gla_wa2, gla_ba, gla_norm, pool_w, pool_b,
           pool_scale, sgu_ln_g, sgu_ln_b, sgu_w, sgu_b, w_out, norm_xa, norm_mem, xa_wq, xa_wkv, xa_wo,
           norm_ffn, ffn_w1, ffn_w3, ffn_w2, router, moe_w1, moe_w3, moe_w2):
    bsz, t, d = x.shape
    depth = norm_mix.shape[0]
    h = x.reshape(bsz * t, d)
    mem2 = mem.reshape(-1, d)
    toe = rel_bias_tables(rel_bias)
    pending = None
    for i in range(depth):
        if pending is not None:
            h, pending = h + pending, None
        p = dict(norm_mix=norm_mix[i], w_in=w_in[i], gla_wa2=gla_wa2[i], gla_ba=gla_ba[i],
                 gla_norm=gla_norm[i], pool_w=pool_w[i], pool_b=pool_b[i], pool_scale=pool_scale[i],
                 sgu_ln_g=sgu_ln_g[i], sgu_ln_b=sgu_ln_b[i], sgu_w=sgu_w[i], sgu_b=sgu_b[i],
                 w_out=w_out[i], norm_xa=norm_xa[i], norm_mem=norm_mem[i], xa_wq=xa_wq[i],
                 xa_wkv=xa_wkv[i], xa_wo=xa_wo[i])
        h = hybrid_layer(h, mem2, toe, p, bsz, t)
        j = i // 2
        if i % 2 == 0:
            h = ffn(h, norm_ffn[i], ffn_w1[j].astype(BF16), ffn_w3[j].astype(BF16), ffn_w2[j].astype(BF16))
        else:
            pending = moe_block(h, norm_ffn[i], router[j], moe_w1[j], moe_w3[j], moe_w2[j])
    xs = (h,) if pending is None else (h, pending)
    return final_rmsnorm(final_norm, *xs).reshape(bsz, t, d)
```

```python
import functools
import math

import jax
import jax.numpy as jnp
import numpy as np
from jax import lax
from jax.experimental import pallas as pl
from jax.experimental.pallas import tpu as pltpu

F32 = jnp.float32
BF16 = jnp.bfloat16
EPS = 1e-6

GROUP_WIDTH = 256

GLA_HEADS = 4
GLA_DV = 64
GLA_DK = 32
GLA_RANK = 16
GLA_TAU = 16.0
GLA_CHUNK = 64
GLA_GROUP = 8

POOL_WINDOWS = (2, 4, 8, 16)
POOL_CG = 64

DSA_HEADS = 4
DSA_DH = 64
IDX_HEADS = 8
IDX_DIM = 32
DSA_TOPK_MAX = 256
DSA_BLOCK = 256
DSA_SUB = 64

SGU_GROUPS = 4
SGU_CHUNK = 128
SGU_CG = 64

REL_BUCKETS = 32
REL_MAX_DIST = 128

XA_HEADS = 4
XA_DH = 64

N_EXPERTS = 8

A_WIDTH = 896
B_WIDTH = 256
C_MAIN = 1024
C_AUX = 128
D_WIDTH = 512

INT_MIN = -(2 ** 31)
NEG_BIG = -1e30
VMEM_LIMIT = 56 * 1024 * 1024


def _cparams(sem):
    return pltpu.CompilerParams(dimension_semantics=sem, vmem_limit_bytes=VMEM_LIMIT)


def _dot(a, b):
    return jnp.dot(a, b, preferred_element_type=F32)


def _dot_nt(a, b):
    return lax.dot_general(a, b, (((1,), (1,)), ((), ())), preferred_element_type=F32)


def _dot_tn(a, b):
    return lax.dot_general(a, b, (((0,), (0,)), ((), ())), preferred_element_type=F32)


def _dot_f32(a, b):
    return jnp.dot(a, b, preferred_element_type=F32, precision=lax.Precision.HIGHEST)


def _split_bf16(x, terms):
    parts = []
    for _ in range(terms):
        p = x.astype(BF16)
        parts.append(p)
        x = x - p.astype(F32)
    return parts


def _dot_exact_rhs(a, b, terms):
    out = None
    for p in _split_bf16(a, terms):
        d = _dot(p, b)
        out = d if out is None else out + d
    return out


def _dot_exact_lhs(a, b, terms):
    out = None
    for p in _split_bf16(b, terms):
        d = _dot(a, p)
        out = d if out is None else out + d
    return out


def _dot_3pass(a, b):
    a_hi, a_lo = _split_bf16(a, 2)
    b_hi, b_lo = _split_bf16(b, 2)
    return _dot(a_hi, b_hi) + (_dot(a_hi, b_lo) + _dot(a_lo, b_hi))


def _rms(x, g):
    return x * lax.rsqrt(jnp.mean(x * x, axis=-1, keepdims=True) + EPS) * g


def _norm_matmul_kernel(h_ref, g_ref, w_ref, *out_refs, widths):
    hb = _rms(h_ref[...], g_ref[...]).astype(BF16)
    off = 0
    for o_ref, wd in zip(out_refs, widths):
        o_ref[...] = _dot(hb, w_ref[:, off:off + wd]).astype(o_ref.dtype)
        off += wd


def norm_matmul(h, g, w, widths, dtypes=None, tm=512):
    dtypes = dtypes or (F32,) * len(widths)
    m, d = h.shape
    tm = min(tm, m)
    n = sum(widths)
    return pl.pallas_call(
        functools.partial(_norm_matmul_kernel, widths=widths),
        grid=(m // tm,),
        in_specs=[pl.BlockSpec((tm, d), lambda i: (i, 0)),
                  pl.BlockSpec((1, d), lambda i: (0, 0)),
                  pl.BlockSpec((d, n), lambda i: (0, 0))],
        out_specs=[pl.BlockSpec((tm, wd), lambda i: (i, 0)) for wd in widths],
        out_shape=[jax.ShapeDtypeStruct((m, wd), dt) for wd, dt in zip(widths, dtypes)],
        compiler_params=_cparams(("parallel",)),
        name="norm_matmul",
    )(h, g.reshape(1, d), w)


def _log_sigmoid(x):
    return jnp.minimum(x, 0.0) - jnp.log1p(jnp.exp(-jnp.abs(x)))


def _gla_kernel(z_ref, wa2_ref, ba_ref, ng_ref, o_ref, s_ref, *, n_groups):
    c = GLA_CHUNK
    grp = GLA_GROUP * c
    hk = GLA_HEADS * GLA_DK
    hv = GLA_HEADS * GLA_DV
    s_ref[...] = jnp.zeros_like(s_ref)

    head_k = lax.broadcasted_iota(jnp.int32, (1, hk), 1) // GLA_DK
    head_v = lax.broadcasted_iota(jnp.int32, (1, hv), 1) // GLA_DV
    g_row = lax.broadcasted_iota(jnp.int32, (grp, grp), 0)
    g_col = lax.broadcasted_iota(jnp.int32, (grp, grp), 1)
    tril = (((g_row // c) == (g_col // c)) & (g_col <= g_row)).astype(BF16)
    causal4 = (lax.broadcasted_iota(jnp.int32, (GLA_HEADS * c, c), 1)
               <= lax.broadcasted_iota(jnp.int32, (GLA_HEADS * c, c), 0) % c)
    state_mask = (lax.broadcasted_iota(jnp.int32, (hk, hv), 0) // GLA_DK
                  == lax.broadcasted_iota(jnp.int32, (hk, hv), 1) // GLA_DV)
    norm_mat = jnp.where(lax.broadcasted_iota(jnp.int32, (hv, hv), 0) // GLA_DV
                         == lax.broadcasted_iota(jnp.int32, (hv, hv), 1) // GLA_DV,
                         1.0 / GLA_DV, 0.0).astype(BF16)
    wa2 = wa2_ref[...]
    ba = ba_ref[...]
    ng = ng_ref[...]

    def body(n, carry):
        r0 = pl.multiple_of(n * grp, grp)
        z = z_ref[0, pl.ds(r0, grp), :]
        q, k, v, g, lr = z[:, 0:128], z[:, 128:256], z[:, 256:512], z[:, 512:768], z[:, 768:896]
        log_a = _log_sigmoid(_dot_f32(lr, wa2) + ba) / GLA_TAU
        b = _dot_exact_lhs(tril, log_a, 3)
        b_end = jnp.concatenate([jnp.broadcast_to(b[(ci + 1) * c - 1:(ci + 1) * c, :], (c, hk))
                                 for ci in range(GLA_GROUP)], axis=0)
        q_t = q * (GLA_DK ** -0.5) * jnp.exp(b)
        q_tb = q_t.astype(BF16)
        k_t = (k * jnp.exp(-b)).astype(BF16)
        k_dec = (k * jnp.exp(b_end - b)).astype(BF16)
        vb = v.astype(BF16)
        outs = []
        for ci in range(GLA_GROUP):
            rows = slice(ci * c, (ci + 1) * c)
            q4 = jnp.concatenate([jnp.where(head_k == h, q_t[rows], 0.0) for h in range(GLA_HEADS)],
                                 axis=0).astype(BF16)
            att = jnp.where(causal4, _dot_nt(q4, k_t[rows]), 0.0)
            r = _dot(att.astype(BF16), vb[rows])
            o = _dot(q_tb[rows], s_ref[...].astype(BF16))
            for h in range(GLA_HEADS):
                o = o + jnp.where(head_v == h, r[h * c:(h + 1) * c, :], 0.0)
            outs.append(o)
            kv = jnp.where(state_mask, _dot_tn(k_dec[rows], vb[rows]), 0.0)
            last = b_end[ci * c:ci * c + 1, :]
            dec = jnp.exp(jnp.transpose(jnp.broadcast_to(last, (hk, hk))))
            s_ref[...] = s_ref[...] * jnp.concatenate([dec, dec], axis=1) + kv
        o = jnp.concatenate(outs, axis=0)
        o = o * lax.rsqrt(_dot_exact_rhs(o * o, norm_mat, 2) + EPS) * ng
        o_ref[0, pl.ds(r0, grp), :] = o * (g * jax.nn.sigmoid(g))
        return carry

    lax.fori_loop(0, n_groups, body, 0)


def gla_mixer(z_a, wa2, ba, norm_g):
    bsz, t, _ = z_a.shape
    hk = GLA_HEADS * GLA_DK
    wa2p = jnp.zeros((128, hk), F32).at[:GLA_RANK].set(wa2)
    return pl.pallas_call(
        functools.partial(_gla_kernel, n_groups=t // (GLA_GROUP * GLA_CHUNK)),
        grid=(bsz,),
        in_specs=[pl.BlockSpec((1, t, A_WIDTH), lambda b: (b, 0, 0)),
                  pl.BlockSpec((128, hk), lambda b: (0, 0)),
                  pl.BlockSpec((1, hk), lambda b: (0, 0)),
                  pl.BlockSpec((1, GROUP_WIDTH), lambda b: (0, 0))],
        out_specs=pl.BlockSpec((1, t, GROUP_WIDTH), lambda b: (b, 0, 0)),
        out_shape=jax.ShapeDtypeStruct((bsz, t, GROUP_WIDTH), F32),
        scratch_shapes=[pltpu.VMEM((hk, GROUP_WIDTH), F32)],
        compiler_params=_cparams(("parallel",)),
        name="gla_mixer",
    )(z_a, wa2p, ba.reshape(1, hk), norm_g.reshape(1, GROUP_WIDTH))


def _pool_kernel(u_ref, w_ref, b_ref, sc_ref, o_ref):
    u = u_ref[0]
    t, gw = u.shape
    row = lax.broadcasted_iota(jnp.int32, (t, gw), 0)
    grp = lax.broadcasted_iota(jnp.int32, (t, gw), 1) // POOL_CG

    def shifted(x, k):
        return jnp.where(row >= k, pltpu.roll(x, k, axis=0), 0.0)

    s = u
    p = jnp.zeros_like(u)
    for gi, win in enumerate(POOL_WINDOWS):
        half = win // 2
        s = s + shifted(s, half)
        cnt = jnp.minimum(row + 1, win).astype(F32)
        p = jnp.where(grp == gi, s / cnt - u, p)
    y = _dot(p.astype(BF16), w_ref[...]) + b_ref[...]
    o_ref[0] = y * sc_ref[...]


def pool_mixer(z_b, w, b, scale):
    assert POOL_WINDOWS == (2, 4, 8, 16)
    bsz, t, gw = z_b.shape
    w_bd = jnp.zeros((gw, gw), F32)
    for gi in range(len(POOL_WINDOWS)):
        w_bd = w_bd.at[gi * POOL_CG:(gi + 1) * POOL_CG, gi * POOL_CG:(gi + 1) * POOL_CG].set(w[gi])
    return pl.pallas_call(
        _pool_kernel,
        grid=(bsz,),
        in_specs=[pl.BlockSpec((1, t, gw), lambda i: (i, 0, 0)),
                  pl.BlockSpec((gw, gw), lambda i: (0, 0)),
                  pl.BlockSpec((1, gw), lambda i: (0, 0)),
                  pl.BlockSpec((1, gw), lambda i: (0, 0))],
        out_specs=pl.BlockSpec((1, t, gw), lambda i: (i, 0, 0)),
        out_shape=jax.ShapeDtypeStruct((bsz, t, gw), F32),
        compiler_params=_cparams(("parallel",)),
        name="pool_mixer",
    )(z_b, w_bd.astype(BF16), b.reshape(1, gw), scale.reshape(1, gw))


def _sgu_kernel(z_ref, lg_ref, lb_ref, w_ref, bm_ref, o_ref, *, chunks):
    c = SGU_CHUNK
    gw = GROUP_WIDTH
    rows = SGU_GROUPS * c
    tri = (lax.broadcasted_iota(jnp.int32, (rows, c), 1)
           <= lax.broadcasted_iota(jnp.int32, (rows, c), 0) % c)
    ws = jnp.where(tri, w_ref[...], 0.0).astype(BF16)
    grp = lax.broadcasted_iota(jnp.int32, (1, gw), 1) // SGU_CG
    for ci in range(chunks):
        z = jax.nn.gelu(z_ref[0, ci * c:(ci + 1) * c, :], approximate=True)
        u, v = z[:, :gw], z[:, gw:]
        mu = jnp.mean(v, axis=-1, keepdims=True)
        var = jnp.mean(jnp.square(v - mu), axis=-1, keepdims=True)
        vn = (v - mu) * lax.rsqrt(var + EPS) * lg_ref[...] + lb_ref[...]
        r = _dot(ws, vn.astype(BF16))
        mixed = bm_ref[...]
        for g in range(SGU_GROUPS):
            mixed = mixed + jnp.where(grp == g, r[g * c:(g + 1) * c, :], 0.0)
        o_ref[0, ci * c:(ci + 1) * c, :] = u * mixed


def sgu_mixer(z_d, ln_g, ln_b, w_s, b_s, chunks=4):
    bsz, t, _ = z_d.shape
    gw = GROUP_WIDTH
    tt = chunks * SGU_CHUNK
    bias = jnp.repeat(b_s.T, SGU_CG, axis=1)
    return pl.pallas_call(
        functools.partial(_sgu_kernel, chunks=chunks),
        grid=(bsz, t // tt),
        in_specs=[pl.BlockSpec((1, tt, 2 * gw), lambda b, i: (b, i, 0)),
                  pl.BlockSpec((1, gw), lambda b, i: (0, 0)),
                  pl.BlockSpec((1, gw), lambda b, i: (0, 0)),
                  pl.BlockSpec((SGU_GROUPS * SGU_CHUNK, SGU_CHUNK), lambda b, i: (0, 0)),
                  pl.BlockSpec((SGU_CHUNK, gw), lambda b, i: (0, 0))],
        out_specs=pl.BlockSpec((1, tt, gw), lambda b, i: (b, i, 0)),
        out_shape=jax.ShapeDtypeStruct((bsz, t, gw), F32),
        compiler_params=_cparams(("parallel", "parallel")),
        name="sgu_mixer",
    )(z_d, ln_g.reshape(1, gw), ln_b.reshape(1, gw),
      w_s.reshape(SGU_GROUPS * SGU_CHUNK, SGU_CHUNK), bias)


def _bucket_table():
    assert REL_MAX_DIST <= DSA_BLOCK + 1
    s = np.arange(DSA_BLOCK)[:, None]
    t = np.arange(DSA_BLOCK)[None, :]
    dist = np.stack([t - s, DSA_BLOCK + t - s, 2 * DSA_BLOCK + t - s])
    n = np.maximum(dist, 0)
    max_exact = REL_BUCKETS // 2
    nf = np.maximum(n, 1).astype(np.float32)
    large = max_exact + (np.log(nf / np.float32(max_exact)) / np.float32(math.log(REL_MAX_DIST / max_exact))
                         * np.float32(REL_BUCKETS - max_exact)).astype(np.int32)
    return np.where(n < max_exact, n, np.minimum(large, REL_BUCKETS - 1)).astype(np.int32)


def _bias_table_kernel(rb_ref, bucket_ref, o_ref):
    for back in range(3):
        bucket = bucket_ref[back]
        for h in range(DSA_HEADS):
            acc = jnp.zeros(bucket.shape, F32)
            for b in range(REL_BUCKETS):
                acc = jnp.where(bucket == b, rb_ref[b * DSA_HEADS + h], acc)
            o_ref[back, h] = acc


def rel_bias_tables(rel_bias):
    blk = DSA_BLOCK
    return pl.pallas_call(
        _bias_table_kernel,
        in_specs=[pl.BlockSpec(memory_space=pltpu.SMEM),
                  pl.BlockSpec((3, blk, blk), lambda: (0, 0, 0))],
        out_specs=pl.BlockSpec((3, DSA_HEADS, blk, blk), lambda: (0, 0, 0, 0)),
        out_shape=jax.ShapeDtypeStruct((3, DSA_HEADS, blk, blk), F32),
        name="rel_bias_tables",
    )(rel_bias.reshape(-1), jnp.asarray(_bucket_table()))


def _dsa_kernel(q_ref, k_ref, v_ref, qi_ref, kw_ref, qw_ref, toe_ref, o_ref,
                kpl_ref, vt_ref, keys_ref, khi_ref, klo_ref, acc_ref, am_ref, lg_ref, p_ref,
                *, topk, n_blocks, idx_bits):
    blk = DSA_BLOCK
    sub = DSA_SUB
    i = pl.program_id(1)
    hd = DSA_HEADS * DSA_DH
    heads_per_half = 128 // IDX_DIM

    @pl.when(i == 0)
    def _():
        lane = lax.broadcasted_iota(jnp.int32, (blk, 128), 1)

        def build(kb, c):
            r0 = pl.multiple_of(kb * blk, blk)
            ki = jnp.where(lane < IDX_DIM, kw_ref[0, pl.ds(r0, blk), :], 0.0)
            for j in range(heads_per_half):
                kpl_ref[kb, j] = (ki if j == 0 else pltpu.roll(ki, j * IDX_DIM, axis=1)).astype(BF16)
            vt_ref[kb] = jnp.transpose(v_ref[0, pl.ds(r0, blk), :].astype(F32)).astype(BF16)
            return c

        lax.fori_loop(0, n_blocks, build, 0)

    s_loc = lax.broadcasted_iota(jnp.int32, (blk, blk), 0)
    t_loc = lax.broadcasted_iota(jnp.int32, (blk, blk), 1)
    n_vis = i + 1

    qi_t = jnp.transpose(qi_ref[0].astype(F32)).astype(BF16)
    qi_halves = [qi_t[:128, :], qi_t[128:, :]]
    w_t = jnp.transpose(qw_ref[0])
    w_rows = [w_t[IDX_DIM + h:IDX_DIM + h + 1, :] * (IDX_HEADS ** -0.5) * (IDX_DIM ** -0.5)
              for h in range(IDX_HEADS)]
    s_sub = lax.broadcasted_iota(jnp.int32, (sub, blk), 0)
    t_sub = lax.broadcasted_iota(jnp.int32, (sub, blk), 1)

    def score_body(kb, c):
        for ci in range(blk // sub):
            rows = slice(ci * sub, (ci + 1) * sub)
            sc = jnp.zeros((sub, blk), F32)
            for half in range(2):
                for j in range(heads_per_half):
                    d = _dot(kpl_ref[kb, j, rows, :], qi_halves[half])
                    sc = sc + jnp.maximum(d, 0.0) * w_rows[half * heads_per_half + j]
            sc = jnp.where(sc == 0.0, 0.0, sc)
            bits = pltpu.bitcast(sc, jnp.int32)
            key = jnp.where(bits < 0, bits ^ jnp.int32(0x7FFFFFFF), bits)
            vis = (kb < i) | (s_sub + ci * sub <= t_sub)
            key = jnp.where(vis, key, jnp.int32(INT_MIN))
            keys_ref[kb, rows, :] = key
            khi_ref[kb, rows, :] = lax.shift_right_arithmetic(key, 16).astype(jnp.int16)
            klo_ref[kb, rows, :] = ((key & 0xFFFF) - 2 ** 15).astype(jnp.int16)
        return c

    lax.fori_loop(0, n_vis, score_body, 0)

    def count(pred):
        def body(kb, acc):
            hit = jnp.where(pred(keys_ref[kb], kb), 1.0, 0.0)
            return acc + jnp.sum(hit.reshape(blk // 32, 32, blk), axis=0)
        acc = lax.fori_loop(0, n_vis, body, jnp.zeros((32, blk), F32))
        return jnp.sum(acc, axis=0, keepdims=True)

    def count16(ref, cand):
        cand = cand.astype(jnp.int16)

        def body(kb, acc):
            hit = jnp.where(ref[kb] >= cand, jnp.int16(1), jnp.int16(0))
            for j in range(blk // 32):
                acc = acc + hit[j * 32:(j + 1) * 32]
            return acc
        acc = lax.fori_loop(0, n_vis, body, jnp.zeros((32, blk), jnp.int16))
        return jnp.sum(acc.astype(jnp.int32).astype(F32), axis=0, keepdims=True)

    def search16(ref, offset):
        lowest = jnp.full((1, blk), -(2 ** 15), jnp.int32)
        base = jnp.where(offset + count16(ref, jnp.zeros((1, blk), jnp.int32)) >= kf, 0, lowest)

        def bit_body(it, base):
            cand = base | lax.shift_left(jnp.int32(1), 14 - it)
            return jnp.where(offset + count16(ref, cand) >= kf, cand, base)
        return lax.fori_loop(0, 15, bit_body, base)

    kf = float(topk)
    zero = jnp.zeros((1, blk), jnp.int32)
    thr_hi = search16(khi_ref, 0.0)
    thr_hi16 = thr_hi.astype(jnp.int16)

    def low_body(kb, above):
        hi = khi_ref[kb]
        klo_ref[kb] = jnp.where(hi == thr_hi16, klo_ref[kb], jnp.int16(-(2 ** 15)))
        hit = jnp.where(hi > thr_hi16, jnp.int16(1), jnp.int16(0))
        for j in range(blk // 32):
            above = above + hit[j * 32:(j + 1) * 32]
        return above

    above = lax.fori_loop(0, n_vis, low_body, jnp.zeros((32, blk), jnp.int16))
    n_above = jnp.sum(above.astype(jnp.int32).astype(F32), axis=0, keepdims=True)
    thr_lo = search16(klo_ref, n_above)
    thr = lax.shift_left(thr_hi, 16) | (thr_lo + 2 ** 15)
    thr_sel = jnp.maximum(thr, jnp.int32(INT_MIN + 1))

    n_gt = count(lambda key, kb: key > thr)
    n_eq = count(lambda key, kb: key == thr)
    need = kf - n_gt
    excess = jnp.where((n_eq > need) & (thr > jnp.int32(INT_MIN)), 1.0, 0.0)

    def tie_search():
        def tie_body(it, j):
            cand = j | lax.shift_left(jnp.int32(1), idx_bits - 1 - it)
            below = count(lambda key, kb: (key == thr) & (kb * blk + s_loc < cand))
            return jnp.where(below < need, cand, j)
        return lax.fori_loop(0, idx_bits, tie_body, zero)

    last = lax.cond(jnp.max(excess) > 0.0, tie_search, lambda: jnp.full((1, blk), 2 ** 30, jnp.int32))

    assert DSA_DH ** -0.5 == 0.125
    q_t = jnp.transpose(q_ref[0].astype(F32) * (DSA_DH ** -0.5))
    row_h = lax.broadcasted_iota(jnp.int32, (hd, 1), 0) // DSA_DH
    q_heads = [jnp.where(row_h == h, q_t, 0.0).astype(BF16) for h in range(DSA_HEADS)]
    acc_ref[...] = jnp.zeros_like(acc_ref)
    n_sub = blk // sub

    def att_body(kb, carry):
        ms, ls = carry
        back = jnp.minimum(i - kb, 2)
        for ci in range(n_sub):
            rows = slice(ci * sub, (ci + 1) * sub)
            key = keys_ref[kb, rows, :]
            sel = (key > thr_sel) | ((key == thr_sel) & (kb * blk + ci * sub + s_sub <= last))
            am_ref[rows, :] = jnp.where(sel, 0.0, NEG_BIG)
        new_ms, alphas = [], []
        for h in range(DSA_HEADS):
            pm = jnp.full((8, blk), NEG_BIG, F32)
            for ci in range(n_sub):
                rows = slice(ci * sub, (ci + 1) * sub)
                k_rows = k_ref[0, pl.ds(pl.multiple_of(kb * blk + ci * sub, sub), sub), :]
                lg = _dot(k_rows, q_heads[h]) + toe_ref[back, h, rows, :] + am_ref[rows, :]
                lg_ref[h, rows, :] = lg
                pm = jnp.maximum(pm, jnp.max(lg.reshape(sub // 8, 8, blk), axis=0))
            m_new = jnp.maximum(ms[h], jnp.max(pm, axis=0, keepdims=True))
            new_ms.append(m_new)
            alphas.append(jnp.exp(ms[h] - m_new))
        new_ls = []
        for h in range(DSA_HEADS):
            ps = jnp.zeros((8, blk), F32)
            for ci in range(n_sub):
                rows = slice(ci * sub, (ci + 1) * sub)
                p = jnp.exp(lg_ref[h, rows, :] - new_ms[h])
                ps = ps + jnp.sum(p.reshape(sub // 8, 8, blk), axis=0)
                p_ref[h, rows, :] = p.astype(BF16)
            new_ls.append(ls[h] * alphas[h] + jnp.sum(ps, axis=0, keepdims=True))
        for h in range(DSA_HEADS):
            hrows = slice(h * DSA_DH, (h + 1) * DSA_DH)
            acc_ref[hrows, :] = acc_ref[hrows, :] * alphas[h] + _dot(vt_ref[kb, hrows, :], p_ref[h])
        return tuple(new_ms), tuple(new_ls)

    init = (tuple(jnp.full((1, blk), 0.01 * NEG_BIG, F32) for _ in range(DSA_HEADS)),
            tuple(jnp.zeros((1, blk), F32) for _ in range(DSA_HEADS)))
    _, ls = lax.fori_loop(0, n_vis, att_body, init)
    for h in range(DSA_HEADS):
        rows = slice(h * DSA_DH, (h + 1) * DSA_DH)
        acc_ref[rows, :] = acc_ref[rows, :] * (1.0 / ls[h])
    o_ref[0] = jnp.transpose(acc_ref[...])


def dsa_mixer(z_c, z_aux, toe):
    bsz, t, _ = z_c.shape
    blk = DSA_BLOCK
    n_blocks = t // blk
    assert t % blk == 0
    topk = min(DSA_TOPK_MAX, t // 4)
    hd = DSA_HEADS * DSA_DH
    kernel = functools.partial(_dsa_kernel, topk=topk, n_blocks=n_blocks,
                               idx_bits=max(1, (t - 1).bit_length()))
    return pl.pallas_call(
        kernel,
        grid=(bsz, n_blocks),
        in_specs=[pl.BlockSpec((1, blk, hd), lambda b, i: (b, i, 0)),
                  pl.BlockSpec((1, t, hd), lambda b, i: (b, 0, 1)),
                  pl.BlockSpec((1, t, hd), lambda b, i: (b, 0, 2)),
                  pl.BlockSpec((1, blk, hd), lambda b, i: (b, i, 3)),
                  pl.BlockSpec((1, t, C_AUX), lambda b, i: (b, 0, 0)),
                  pl.BlockSpec((1, blk, C_AUX), lambda b, i: (b, i, 0)),
                  pl.BlockSpec((3, DSA_HEADS, blk, blk), lambda b, i: (0, 0, 0, 0))],
        out_specs=pl.BlockSpec((1, blk, hd), lambda b, i: (b, i, 0)),
        out_shape=jax.ShapeDtypeStruct((bsz, t, hd), F32),
        scratch_shapes=[pltpu.VMEM((n_blocks, 128 // IDX_DIM, blk, 128), BF16),
                        pltpu.VMEM((n_blocks, hd, blk), BF16),
                        pltpu.VMEM((n_blocks, blk, blk), jnp.int32),
                        pltpu.VMEM((n_blocks, blk, blk), jnp.int16),
                        pltpu.VMEM((n_blocks, blk, blk), jnp.int16),
                        pltpu.VMEM((hd, blk), F32),
                        pltpu.VMEM((blk, blk), F32),
                        pltpu.VMEM((DSA_HEADS, blk, blk), F32),
                        pltpu.VMEM((DSA_HEADS, blk, blk), BF16)],
        compiler_params=_cparams(("parallel", "arbitrary")),
        name="dsa_mixer",
    )(z_c, z_c, z_c, z_c, z_aux, z_aux, toe)


def _mixer_tail_kernel(h_ref, a_ref, b_ref, c_ref, d_ref, wout_ref, gxa_ref, wq_ref, k_ref, v_ref, wo_ref,
                       *refs, route):
    gw = GROUP_WIDTH
    x = h_ref[0]
    for gi, r in enumerate((a_ref, b_ref, c_ref, d_ref)):
        x = x + _dot(r[0].astype(BF16), wout_ref[gi * gw:(gi + 1) * gw, :])

    hd = XA_HEADS * XA_DH
    q = _dot(_rms(x, gxa_ref[...]).astype(BF16), wq_ref[...]).astype(BF16)
    k = k_ref[0]
    v = v_ref[0]
    lane_h = lax.broadcasted_iota(jnp.int32, (1, hd), 1) // XA_DH
    o = jnp.zeros((x.shape[0], hd), F32)
    for h in range(XA_HEADS):
        s = _dot_nt(q, jnp.where(lane_h == h, k, 0.0).astype(BF16)) * (XA_DH ** -0.5)
        p = jnp.exp(s - jnp.max(s, axis=-1, keepdims=True))
        p = p / jnp.sum(p, axis=-1, keepdims=True)
        o = o + _dot(p.astype(BF16), jnp.where(lane_h == h, v, 0.0).astype(BF16))
    x = x + _dot(o.astype(BF16), wo_ref[...])

    if not route:
        o_ref, = refs
        o_ref[0] = x
        return
    gffn_ref, wr_ref, o_ref, hn_ref, gate_ref, cnt_ref = refs
    o_ref[0] = x
    hn = _rms(x, gffn_ref[...])
    hn_ref[0] = hn.astype(BF16)
    tm = hn.shape[0]
    lane = lax.broadcasted_iota(jnp.int32, (tm, GATE_LANES), 1)
    logits = jnp.where(lane < N_EXPERTS, _dot_3pass(hn, wr_ref[...]), -jnp.inf)
    m1 = jnp.max(logits, axis=-1, keepdims=True)
    i1 = jnp.min(jnp.where(logits == m1, lane, GATE_LANES), axis=-1, keepdims=True)
    rest = jnp.where(lane == i1, -jnp.inf, logits)
    m2 = jnp.max(rest, axis=-1, keepdims=True)
    i2 = jnp.min(jnp.where(rest == m2, lane, GATE_LANES), axis=-1, keepdims=True)
    e2 = jnp.exp(m2 - m1)
    g1 = 1.0 / (1.0 + e2)
    gates = jnp.where(lane == i1, g1, 0.0) + jnp.where(lane == i2, e2 * g1, 0.0)
    gates = jnp.where(lane == N_EXPERTS, i1.astype(F32), gates)
    gate_ref[0] = jnp.where(lane == N_EXPERTS + 1, i2.astype(F32), gates)
    sel = jnp.where((lane == i1) | (lane == i2), 1.0, 0.0)
    for c in range(tm // MOE_CHUNK):
        cnt_ref[c] = jnp.sum(sel[c * MOE_CHUNK:(c + 1) * MOE_CHUNK, :], axis=0, keepdims=True)


def mixer_tail(h3, mixers, w_out, g_xa, wq, k, v, wo, routing=None, tm=1024):
    bsz, t, d = h3.shape
    tm = min(tm, t)
    mlen = k.shape[1]
    hd = XA_HEADS * XA_DH
    gw = GROUP_WIDTH
    row = lambda b, i: (b, i, 0)
    fixed2 = lambda b, i: (0, 0)
    in_specs = ([pl.BlockSpec((1, tm, d), row)] + [pl.BlockSpec((1, tm, gw), row)] * 4
                + [pl.BlockSpec((4 * gw, d), fixed2), pl.BlockSpec((1, d), fixed2), pl.BlockSpec((d, hd), fixed2),
                   pl.BlockSpec((1, mlen, hd), lambda b, i: (b, 0, 0)),
                   pl.BlockSpec((1, mlen, hd), lambda b, i: (b, 0, 0)),
                   pl.BlockSpec((hd, d), fixed2)])
    args = [h3, *mixers, w_out, g_xa.reshape(1, d), wq, k, v, wo]
    out_specs = [pl.BlockSpec((1, tm, d), row)]
    out_shape = [jax.ShapeDtypeStruct((bsz, t, d), F32)]
    if routing is not None:
        assert tm % MOE_CHUNK == 0
        g_ffn, wr = routing
        in_specs += [pl.BlockSpec((1, d), fixed2), pl.BlockSpec((d, GATE_LANES), fixed2)]
        args += [g_ffn.reshape(1, d), jnp.zeros((d, GATE_LANES), F32).at[:, :N_EXPERTS].set(wr)]
        per_step = tm // MOE_CHUNK
        out_specs += [pl.BlockSpec((1, tm, d), row), pl.BlockSpec((1, tm, GATE_LANES), row),
                      pl.BlockSpec((per_step, 1, GATE_LANES), lambda b, i: (b * (t // tm) + i, 0, 0))]
        out_shape += [jax.ShapeDtypeStruct((bsz, t, d), BF16), jax.ShapeDtypeStruct((bsz, t, GATE_LANES), F32),
                      jax.ShapeDtypeStruct((bsz * t // MOE_CHUNK, 1, GATE_LANES), F32)]
    return pl.pallas_call(
        functools.partial(_mixer_tail_kernel, route=routing is not None),
        grid=(bsz, t // tm),
        in_specs=in_specs,
        out_specs=out_specs,
        out_shape=out_shape,
        compiler_params=_cparams(("parallel", "parallel")),
        name="mixer_tail",
    )(*args)


def _ffn_kernel(h_ref, g_ref, w1_ref, w3_ref, w2_ref, o_ref, hn_ref, acc_ref):
    f = pl.program_id(1)

    @pl.when(f == 0)
    def _():
        hn_ref[...] = _rms(h_ref[...], g_ref[...]).astype(BF16)
        acc_ref[...] = jnp.zeros_like(acc_ref)

    a = _dot(hn_ref[...], w1_ref[...])
    b = _dot(hn_ref[...], w3_ref[...])
    acc_ref[...] += _dot((a * jax.nn.sigmoid(a) * b).astype(BF16), w2_ref[...])

    @pl.when(f == pl.num_programs(1) - 1)
    def _():
        o_ref[...] = h_ref[...] + acc_ref[...]


def ffn(h, g, w1, w3, w2, tm=512, tf=1408):
    m, d = h.shape
    tm = min(tm, m)
    nf = w1.shape[1]
    assert nf % tf == 0
    return pl.pallas_call(
        _ffn_kernel,
        grid=(m // tm, nf // tf),
        in_specs=[pl.BlockSpec((tm, d), lambda i, f: (i, 0)),
                  pl.BlockSpec((1, d), lambda i, f: (0, 0)),
                  pl.BlockSpec((d, tf), lambda i, f: (0, f)),
                  pl.BlockSpec((d, tf), lambda i, f: (0, f)),
                  pl.BlockSpec((tf, d), lambda i, f: (f, 0))],
        out_specs=pl.BlockSpec((tm, d), lambda i, f: (i, 0)),
        out_shape=jax.ShapeDtypeStruct((m, d), F32),
        scratch_shapes=[pltpu.VMEM((tm, d), BF16), pltpu.VMEM((tm, d), F32)],
        compiler_params=_cparams(("parallel", "arbitrary")),
        name="ffn",
    )(h, g.reshape(1, d), w1, w3, w2)


GATE_LANES = 128
MOE_CHUNK = 256
MOE_ROWS = 128
MOE_TOKENS = 2048
MOE_ALIGN = 16


def _moe_kernel(cb_ref, hn_ref, gate_ref, w1_ref, w3_ref, w2_ref, y_ref,
                rank_row, rk_ref, gs_ref, xs_ref, yacc_ref, rc_ref, gc_ref):
    t = pl.program_id(0)
    e = pl.program_id(1)
    f = pl.program_id(2)
    ts, d = hn_ref.shape
    ch, rb = MOE_CHUNK, MOE_ROWS
    win = 2 * rb
    n_ch = ts // ch
    lane = lax.broadcasted_iota(jnp.int32, (1, GATE_LANES), 1)

    def before(c):
        return cb_ref[(t * (n_ch + 1) + c) * N_EXPERTS + e]

    n_blocks = (before(n_ch) + rb - 1) // rb
    n_windows = (n_blocks * rb + win - 1) // win + 1

    def windows(c):
        lo, hi = before(c), before(c + 1)
        s0 = (lo // MOE_ALIGN) * MOE_ALIGN
        return s0, jnp.where(hi > lo, (hi - s0 + win - 1) // win, 0)

    spans = [windows(c) for c in range(n_ch)]
    single = functools.reduce(jnp.logical_and, [n_win <= 1 for _, n_win in spans])

    def for_each_window(visit):
        @pl.when(single)
        def _():
            for c, (s0, _) in enumerate(spans):
                visit(c, s0)

        @pl.when(jnp.logical_not(single))
        def _():
            for c, (s0, n_win) in enumerate(spans):
                def body(j, carry, c=c, s0=s0):
                    visit(c, s0 + j * win)
                    return carry
                lax.fori_loop(0, n_win, body, 0)

    @pl.when((e == 0) & (f == 0))
    def _():
        y_ref[...] = jnp.zeros_like(y_ref)
        strict_lower = (lax.broadcasted_iota(jnp.int32, (ch, ch), 1)
                        < lax.broadcasted_iota(jnp.int32, (ch, ch), 0)).astype(BF16)
        lane_f = lane.astype(F32)
        offs = jnp.ones((1, GATE_LANES), F32)
        for c in range(n_ch):
            rows = slice(c * ch, (c + 1) * ch)
            g = gate_ref[rows, :]
            sel = jnp.where((lane_f == g[:, N_EXPERTS:N_EXPERTS + 1])
                            | (lane_f == g[:, N_EXPERTS + 1:N_EXPERTS + 2]), 1.0, 0.0)
            r = _dot(strict_lower, sel.astype(BF16)) + offs
            r = jnp.where(sel > 0.0, r, 0.0)
            rank_row[:, rows] = jnp.transpose(r) - 1.0
            high = jnp.floor(r * (1.0 / 256.0))
            rk_ref[0, rows, :] = high.astype(BF16)
            rk_ref[1, rows, :] = (r - 256.0 * high).astype(BF16)
            for j, part in enumerate(_split_bf16(g, 3)):
                gs_ref[j, rows, :] = part
            offs = offs + jnp.sum(sel, axis=0, keepdims=True)

    @pl.when(f == 0)
    def _():
        pick = (lax.broadcasted_iota(jnp.int32, (GATE_LANES, GATE_LANES), 0) == e).astype(BF16)
        for c in range(n_ch):
            rows = slice(c * ch, (c + 1) * ch)
            rc_ref[rows, :] = 256.0 * _dot(rk_ref[0, rows, :], pick) + _dot(rk_ref[1, rows, :], pick) - 1.0
            gc_ref[rows, :] = (_dot(gs_ref[0, rows, :], pick) + _dot(gs_ref[1, rows, :], pick)
                               + _dot(gs_ref[2, rows, :], pick))

        def clear(j, carry):
            r0 = pl.multiple_of(j * win, win)
            xs_ref[pl.ds(r0, win), :] = jnp.zeros((win, d), BF16)
            yacc_ref[pl.ds(r0, win), :] = jnp.zeros((win, d), F32)
            return carry

        lax.fori_loop(0, n_windows, clear, 0)

        row_id = lax.broadcasted_iota(jnp.int32, (win, ch), 0)

        def gather_window(c, s):
            s = pl.multiple_of(s, MOE_ALIGN)
            ranks = rank_row[pl.ds(e, 1), c * ch:(c + 1) * ch]
            onehot = jnp.where(ranks == (s + row_id).astype(F32), 1.0, 0.0).astype(BF16)
            xs_ref[pl.ds(s, win), :] += _dot(onehot, hn_ref[c * ch:(c + 1) * ch, :]).astype(BF16)

        for_each_window(gather_window)

    def ffn_rows(r0, rows):
        x = xs_ref[pl.ds(r0, rows), :]
        a = _dot(x, w1_ref[0])
        g3 = _dot(x, w3_ref[0])
        yacc_ref[pl.ds(r0, rows), :] += _dot((a * jax.nn.sigmoid(a) * g3).astype(BF16), w2_ref[0])

    n_quads = n_blocks // 4

    def ffn_quad(j, carry):
        ffn_rows(pl.multiple_of(j * (4 * rb), 4 * rb), 4 * rb)
        return carry

    lax.fori_loop(0, n_quads, ffn_quad, 0)
    tail = pl.multiple_of(n_quads * (4 * rb), 4 * rb)

    @pl.when((n_blocks & 2) != 0)
    def _():
        ffn_rows(tail, 2 * rb)

    @pl.when((n_blocks & 1) != 0)
    def _():
        ffn_rows(pl.multiple_of(tail + (n_blocks & 2) * rb, rb), rb)

    @pl.when(f == pl.num_programs(2) - 1)
    def _():
        def to_bf16(j, carry):
            r0 = pl.multiple_of(j * win, win)
            xs_ref[pl.ds(r0, win), :] = yacc_ref[pl.ds(r0, win), :].astype(BF16)
            return carry

        lax.fori_loop(0, n_windows, to_bf16, 0)

        lane_id = lax.broadcasted_iota(jnp.int32, (ch, rb), 1).astype(F32)

        def scatter_window(c, s):
            rows = slice(c * ch, (c + 1) * ch)
            s = pl.multiple_of(s, MOE_ALIGN)
            rank = rc_ref[rows, :] - s.astype(F32)
            onehot = jnp.concatenate([jnp.where(rank == lane_id, 1.0, 0.0),
                                      jnp.where(rank == lane_id + float(rb), 1.0, 0.0)],
                                     axis=1).astype(BF16)
            gate = jnp.concatenate([gc_ref[rows, :]] * (d // GATE_LANES), axis=1)
            y_ref[rows, :] += gate * _dot(onehot, xs_ref[pl.ds(s, win), :])

        for_each_window(scatter_window)


def moe(hn, gates, cnt, w1, w3, w2, tf=896):
    m, d = hn.shape
    ts = min(MOE_TOKENS, m)
    n_exp, _, nf = w1.shape
    assert nf % tf == 0 and m % ts == 0 and ts % MOE_CHUNK == 0 and MOE_ROWS == GATE_LANES
    n_ch = ts // MOE_CHUNK
    counts = cnt.reshape(m // ts, n_ch, GATE_LANES)[:, :, :n_exp].astype(jnp.int32)
    before = jnp.concatenate([jnp.zeros((m // ts, 1, n_exp), jnp.int32), jnp.cumsum(counts, axis=1)], axis=1)
    grid_spec = pltpu.PrefetchScalarGridSpec(
        num_scalar_prefetch=1,
        grid=(m // ts, n_exp, nf // tf),
        in_specs=[pl.BlockSpec((ts, d), lambda t, e, f, cb: (t, 0), pipeline_mode=pl.Buffered(1)),
                  pl.BlockSpec((ts, GATE_LANES), lambda t, e, f, cb: (t, 0), pipeline_mode=pl.Buffered(1)),
                  pl.BlockSpec((1, d, tf), lambda t, e, f, cb: (e, 0, f)),
                  pl.BlockSpec((1, d, tf), lambda t, e, f, cb: (e, 0, f)),
                  pl.BlockSpec((1, tf, d), lambda t, e, f, cb: (e, f, 0))],
        out_specs=pl.BlockSpec((ts, d), lambda t, e, f, cb: (t, 0), pipeline_mode=pl.Buffered(1)),
        scratch_shapes=[pltpu.VMEM((GATE_LANES, ts), F32),
                        pltpu.VMEM((2, ts, GATE_LANES), BF16),
                        pltpu.VMEM((3, ts, GATE_LANES), BF16),
                        pltpu.VMEM((ts + 2 * MOE_ROWS, d), BF16),
                        pltpu.VMEM((ts + 2 * MOE_ROWS, d), F32),
                        pltpu.VMEM((ts, MOE_ROWS), F32),
                        pltpu.VMEM((ts, GATE_LANES), F32)])
    return pl.pallas_call(
        _moe_kernel,
        grid_spec=grid_spec,
        out_shape=jax.ShapeDtypeStruct((m, d), F32),
        compiler_params=_cparams(("parallel", "arbitrary", "arbitrary")),
        name="moe",
    )(before.reshape(-1), hn, gates, w1, w3, w2)


def _final_norm_kernel(g_ref, *refs):
    *x_refs, o_ref = refs
    x = x_refs[0][...]
    for r in x_refs[1:]:
        x = x + r[...]
    o_ref[...] = _rms(x, g_ref[...])


def final_rmsnorm(g, *xs, tm=1024):
    m, d = xs[0].shape
    tm = min(tm, m)
    return pl.pallas_call(
        _final_norm_kernel,
        grid=(m // tm,),
        in_specs=[pl.BlockSpec((1, d), lambda i: (0, 0))] + [pl.BlockSpec((tm, d), lambda i: (i, 0))] * len(xs),
        out_specs=pl.BlockSpec((tm, d), lambda i: (i, 0)),
        out_shape=jax.ShapeDtypeStruct((m, d), F32),
        compiler_params=_cparams(("parallel",)),
        name="final_norm",
    )(g.reshape(1, d), *xs)


def _pad_cols(w, width):
    return jnp.pad(w, ((0, 0), (0, width - w.shape[1])))


def _in_proj_weight(w_in):
    a_end = 2 * GLA_HEADS * GLA_DK + 2 * GROUP_WIDTH + GLA_RANK
    b_end = a_end + GROUP_WIDTH
    c_end = b_end + 3 * GROUP_WIDTH + IDX_HEADS * IDX_DIM + IDX_DIM + IDX_HEADS
    assert w_in.shape[1] == c_end + 2 * GROUP_WIDTH
    return jnp.concatenate([_pad_cols(w_in[:, :a_end], A_WIDTH), w_in[:, a_end:b_end],
                            _pad_cols(w_in[:, b_end:c_end], C_MAIN + C_AUX), w_in[:, c_end:]], axis=1).astype(BF16)


def hybrid_layer(h, mem2, toe, p, bsz, t, routing=None):
    m, d = h.shape
    z_a, z_b, z_c, z_aux, z_d = norm_matmul(h, p["norm_mix"], _in_proj_weight(p["w_in"]),
                                            (A_WIDTH, B_WIDTH, C_MAIN, C_AUX, D_WIDTH),
                                            (F32, F32, BF16, F32, F32))
    o_a = gla_mixer(z_a.reshape(bsz, t, A_WIDTH), p["gla_wa2"], p["gla_ba"], p["gla_norm"])
    o_b = pool_mixer(z_b.reshape(bsz, t, B_WIDTH), p["pool_w"], p["pool_b"].reshape(-1), p["pool_scale"])
    o_c = dsa_mixer(z_c.reshape(bsz, t, C_MAIN), z_aux.reshape(bsz, t, C_AUX), toe)
    o_d = sgu_mixer(z_d.reshape(bsz, t, D_WIDTH), p["sgu_ln_g"], p["sgu_ln_b"], p["sgu_w"], p["sgu_b"])
    hd = XA_HEADS * XA_DH
    k, v = norm_matmul(mem2, p["norm_mem"], p["xa_wkv"].astype(BF16), (hd, hd))
    mlen = mem2.shape[0] // bsz
    outs = mixer_tail(h.reshape(bsz, t, d), (o_a, o_b, o_c, o_d), p["w_out"].astype(BF16), p["norm_xa"],
                      p["xa_wq"].astype(BF16), k.reshape(bsz, mlen, hd), v.reshape(bsz, mlen, hd),
                      p["xa_wo"].astype(BF16), routing)
    if routing is None:
        return (outs[0].reshape(m, d),)
    h3, hn, gates, counts = outs
    return h3.reshape(m, d), hn.reshape(m, d), gates.reshape(m, GATE_LANES), counts


def kernel(x, mem, rel_bias, final_norm, norm_mix, w_in, gla_wa2, gla_ba, gla_norm, pool_w, pool_b,
           pool_scale, sgu_ln_g, sgu_ln_b, sgu_w, sgu_b, w_out, norm_xa, norm_mem, xa_wq, xa_wkv, xa_wo,
           norm_ffn, ffn_w1, ffn_w3, ffn_w2, router, moe_w1, moe_w3, moe_w2):
    bsz, t, d = x.shape
    depth = norm_mix.shape[0]
    h = x.reshape(bsz * t, d)
    mem2 = mem.reshape(-1, d)
    toe = rel_bias_tables(rel_bias)
    pending = None
    for i in range(depth):
        if pending is not None:
            h, pending = h + pending, None
        p = dict(norm_mix=norm_mix[i], w_in=w_in[i], gla_wa2=gla_wa2[i], gla_ba=gla_ba[i],
                 gla_norm=gla_norm[i], pool_w=pool_w[i], pool_b=pool_b[i], pool_scale=pool_scale[i],
                 sgu_ln_g=sgu_ln_g[i], sgu_ln_b=sgu_ln_b[i], sgu_w=sgu_w[i], sgu_b=sgu_b[i],
                 w_out=w_out[i], norm_xa=norm_xa[i], norm_mem=norm_mem[i], xa_wq=xa_wq[i],
                 xa_wkv=xa_wkv[i], xa_wo=xa_wo[i])
        j = i // 2
        if i % 2 == 0:
            h, = hybrid_layer(h, mem2, toe, p, bsz, t)
            h = ffn(h, norm_ffn[i], ffn_w1[j].astype(BF16), ffn_w3[j].astype(BF16), ffn_w2[j].astype(BF16))
        else:
            h, hn, gates, counts = hybrid_layer(h, mem2, toe, p, bsz, t, routing=(norm_ffn[i], router[j]))
            pending = moe(hn, gates, counts, moe_w1[j].astype(BF16), moe_w3[j].astype(BF16),
                          moe_w2[j].astype(BF16))
    xs = (h,) if pending is None else (h, pending)
    return final_rmsnorm(final_norm, *xs).reshape(bsz, t, d)
```

```python
import functools
import math

import jax
import jax.numpy as jnp
import numpy as np
from jax import lax
from jax.experimental import pallas as pl
from jax.experimental.pallas import tpu as pltpu

F32 = jnp.float32
BF16 = jnp.bfloat16
EPS = 1e-6

GROUP_WIDTH = 256

GLA_HEADS = 4
GLA_DV = 64
GLA_DK = 32
GLA_RANK = 16
GLA_TAU = 16.0
GLA_CHUNK = 64
GLA_GROUP = 8

POOL_WINDOWS = (2, 4, 8, 16)
POOL_CG = 64

DSA_HEADS = 4
DSA_DH = 64
IDX_HEADS = 8
IDX_DIM = 32
DSA_TOPK_MAX = 256
DSA_BLOCK = 256
DSA_SUB = 64
DSA_HEAD_GROUPS = ((0, 1, 2, 3),)

SGU_GROUPS = 4
SGU_CHUNK = 128
SGU_CG = 64

REL_BUCKETS = 32
REL_MAX_DIST = 128

XA_HEADS = 4
XA_DH = 64

N_EXPERTS = 8

A_WIDTH = 896
B_WIDTH = 256
C_MAIN = 1024
C_AUX = 128
D_WIDTH = 512

INT_MIN = -(2 ** 31)
NEG_BIG = -1e30
VMEM_LIMIT = 56 * 1024 * 1024


def _cparams(sem):
    return pltpu.CompilerParams(dimension_semantics=sem, vmem_limit_bytes=VMEM_LIMIT)


def _dot(a, b):
    return jnp.dot(a, b, preferred_element_type=F32)


def _dot_nt(a, b):
    return lax.dot_general(a, b, (((1,), (1,)), ((), ())), preferred_element_type=F32)


def _dot_tn(a, b):
    return lax.dot_general(a, b, (((0,), (0,)), ((), ())), preferred_element_type=F32)


def _dot_f32(a, b):
    return jnp.dot(a, b, preferred_element_type=F32, precision=lax.Precision.HIGHEST)


def _split_bf16(x, terms):
    parts = []
    for _ in range(terms):
        p = x.astype(BF16)
        parts.append(p)
        x = x - p.astype(F32)
    return parts


def _dot_exact_rhs(a, b, terms):
    out = None
    for p in _split_bf16(a, terms):
        d = _dot(p, b)
        out = d if out is None else out + d
    return out


def _dot_exact_lhs(a, b, terms):
    out = None
    for p in _split_bf16(b, terms):
        d = _dot(a, p)
        out = d if out is None else out + d
    return out


def _dot_3pass(a, b):
    a_hi, a_lo = _split_bf16(a, 2)
    b_hi, b_lo = _split_bf16(b, 2)
    return _dot(a_hi, b_hi) + (_dot(a_hi, b_lo) + _dot(a_lo, b_hi))


def _rms(x, g):
    return x * lax.rsqrt(jnp.mean(x * x, axis=-1, keepdims=True) + EPS) * g


def _norm_matmul_kernel(h_ref, g_ref, w_ref, *out_refs, widths):
    hb = _rms(h_ref[...], g_ref[...]).astype(BF16)
    off = 0
    for o_ref, wd in zip(out_refs, widths):
        o_ref[...] = _dot(hb, w_ref[:, off:off + wd]).astype(o_ref.dtype)
        off += wd


def norm_matmul(h, g, w, widths, dtypes=None, tm=512):
    dtypes = dtypes or (F32,) * len(widths)
    m, d = h.shape
    tm = min(tm, m)
    n = sum(widths)
    return pl.pallas_call(
        functools.partial(_norm_matmul_kernel, widths=widths),
        grid=(m // tm,),
        in_specs=[pl.BlockSpec((tm, d), lambda i: (i, 0)),
                  pl.BlockSpec((1, d), lambda i: (0, 0)),
                  pl.BlockSpec((d, n), lambda i: (0, 0))],
        out_specs=[pl.BlockSpec((tm, wd), lambda i: (i, 0)) for wd in widths],
        out_shape=[jax.ShapeDtypeStruct((m, wd), dt) for wd, dt in zip(widths, dtypes)],
        compiler_params=_cparams(("parallel",)),
        name="norm_matmul",
    )(h, g.reshape(1, d), w)


def _log_sigmoid(x):
    return jnp.minimum(x, 0.0) - jnp.log1p(jnp.exp(-jnp.abs(x)))


def _gla_kernel(z_ref, wa2_ref, ba_ref, ng_ref, o_ref, s_ref, *, n_groups):
    c = GLA_CHUNK
    grp = GLA_GROUP * c
    hk = GLA_HEADS * GLA_DK
    hv = GLA_HEADS * GLA_DV
    s_ref[...] = jnp.zeros_like(s_ref)

    head_k = lax.broadcasted_iota(jnp.int32, (1, hk), 1) // GLA_DK
    head_v = lax.broadcasted_iota(jnp.int32, (1, hv), 1) // GLA_DV
    g_row = lax.broadcasted_iota(jnp.int32, (grp, grp), 0)
    g_col = lax.broadcasted_iota(jnp.int32, (grp, grp), 1)
    tril = (((g_row // c) == (g_col // c)) & (g_col <= g_row)).astype(BF16)
    causal4 = (lax.broadcasted_iota(jnp.int32, (GLA_HEADS * c, c), 1)
               <= lax.broadcasted_iota(jnp.int32, (GLA_HEADS * c, c), 0) % c)
    state_mask = (lax.broadcasted_iota(jnp.int32, (hk, hv), 0) // GLA_DK
                  == lax.broadcasted_iota(jnp.int32, (hk, hv), 1) // GLA_DV)
    norm_mat = jnp.where(lax.broadcasted_iota(jnp.int32, (hv, hv), 0) // GLA_DV
                         == lax.broadcasted_iota(jnp.int32, (hv, hv), 1) // GLA_DV,
                         1.0 / GLA_DV, 0.0).astype(BF16)
    wa2 = wa2_ref[...]
    ba = ba_ref[...]
    ng = ng_ref[...]

    def body(n, carry):
        r0 = pl.multiple_of(n * grp, grp)
        z = z_ref[0, pl.ds(r0, grp), :]
        q, k, v, g, lr = z[:, 0:128], z[:, 128:256], z[:, 256:512], z[:, 512:768], z[:, 768:896]
        log_a = _log_sigmoid(_dot_f32(lr, wa2) + ba) / GLA_TAU
        b = _dot_exact_lhs(tril, log_a, 3)
        b_end = jnp.concatenate([jnp.broadcast_to(b[(ci + 1) * c - 1:(ci + 1) * c, :], (c, hk))
                                 for ci in range(GLA_GROUP)], axis=0)
        q_t = q * (GLA_DK ** -0.5) * jnp.exp(b)
        q_tb = q_t.astype(BF16)
        k_t = (k * jnp.exp(-b)).astype(BF16)
        k_dec = (k * jnp.exp(b_end - b)).astype(BF16)
        vb = v.astype(BF16)
        outs = []
        for ci in range(GLA_GROUP):
            rows = slice(ci * c, (ci + 1) * c)
            q4 = jnp.concatenate([jnp.where(head_k == h, q_t[rows], 0.0) for h in range(GLA_HEADS)],
                                 axis=0).astype(BF16)
            att = jnp.where(causal4, _dot_nt(q4, k_t[rows]), 0.0)
            r = _dot(att.astype(BF16), vb[rows])
            o = _dot(q_tb[rows], s_ref[...].astype(BF16))
            for h in range(GLA_HEADS):
                o = o + jnp.where(head_v == h, r[h * c:(h + 1) * c, :], 0.0)
            outs.append(o)
            kv = jnp.where(state_mask, _dot_tn(k_dec[rows], vb[rows]), 0.0)
            last = b_end[ci * c:ci * c + 1, :]
            dec = jnp.exp(jnp.transpose(jnp.broadcast_to(last, (hk, hk))))
            s_ref[...] = s_ref[...] * jnp.concatenate([dec, dec], axis=1) + kv
        o = jnp.concatenate(outs, axis=0)
        o = o * lax.rsqrt(_dot_exact_rhs(o * o, norm_mat, 2) + EPS) * ng
        o_ref[0, pl.ds(r0, grp), :] = o * (g * jax.nn.sigmoid(g))
        return carry

    lax.fori_loop(0, n_groups, body, 0)


def gla_mixer(z_a, wa2, ba, norm_g):
    bsz, t, _ = z_a.shape
    hk = GLA_HEADS * GLA_DK
    wa2p = jnp.zeros((128, hk), F32).at[:GLA_RANK].set(wa2)
    return pl.pallas_call(
        functools.partial(_gla_kernel, n_groups=t // (GLA_GROUP * GLA_CHUNK)),
        grid=(bsz,),
        in_specs=[pl.BlockSpec((1, t, A_WIDTH), lambda b: (b, 0, 0)),
                  pl.BlockSpec((128, hk), lambda b: (0, 0)),
                  pl.BlockSpec((1, hk), lambda b: (0, 0)),
                  pl.BlockSpec((1, GROUP_WIDTH), lambda b: (0, 0))],
        out_specs=pl.BlockSpec((1, t, GROUP_WIDTH), lambda b: (b, 0, 0)),
        out_shape=jax.ShapeDtypeStruct((bsz, t, GROUP_WIDTH), F32),
        scratch_shapes=[pltpu.VMEM((hk, GROUP_WIDTH), F32)],
        compiler_params=_cparams(("parallel",)),
        name="gla_mixer",
    )(z_a, wa2p, ba.reshape(1, hk), norm_g.reshape(1, GROUP_WIDTH))


def _pool_kernel(u_ref, w_ref, b_ref, sc_ref, o_ref):
    u = u_ref[0]
    t, gw = u.shape
    row = lax.broadcasted_iota(jnp.int32, (t, gw), 0)
    grp = lax.broadcasted_iota(jnp.int32, (t, gw), 1) // POOL_CG

    def shifted(x, k):
        return jnp.where(row >= k, pltpu.roll(x, k, axis=0), 0.0)

    s = u
    p = jnp.zeros_like(u)
    for gi, win in enumerate(POOL_WINDOWS):
        half = win // 2
        s = s + shifted(s, half)
        cnt = jnp.minimum(row + 1, win).astype(F32)
        p = jnp.where(grp == gi, s / cnt - u, p)
    y = _dot(p.astype(BF16), w_ref[...]) + b_ref[...]
    o_ref[0] = y * sc_ref[...]


def pool_mixer(z_b, w, b, scale):
    assert POOL_WINDOWS == (2, 4, 8, 16)
    bsz, t, gw = z_b.shape
    w_bd = jnp.zeros((gw, gw), F32)
    for gi in range(len(POOL_WINDOWS)):
        w_bd = w_bd.at[gi * POOL_CG:(gi + 1) * POOL_CG, gi * POOL_CG:(gi + 1) * POOL_CG].set(w[gi])
    return pl.pallas_call(
        _pool_kernel,
        grid=(bsz,),
        in_specs=[pl.BlockSpec((1, t, gw), lambda i: (i, 0, 0)),
                  pl.BlockSpec((gw, gw), lambda i: (0, 0)),
                  pl.BlockSpec((1, gw), lambda i: (0, 0)),
                  pl.BlockSpec((1, gw), lambda i: (0, 0))],
        out_specs=pl.BlockSpec((1, t, gw), lambda i: (i, 0, 0)),
        out_shape=jax.ShapeDtypeStruct((bsz, t, gw), F32),
        compiler_params=_cparams(("parallel",)),
        name="pool_mixer",
    )(z_b, w_bd.astype(BF16), b.reshape(1, gw), scale.reshape(1, gw))


def _sgu_kernel(z_ref, lg_ref, lb_ref, w_ref, bm_ref, o_ref, *, chunks):
    c = SGU_CHUNK
    gw = GROUP_WIDTH
    rows = SGU_GROUPS * c
    tri = (lax.broadcasted_iota(jnp.int32, (rows, c), 1)
           <= lax.broadcasted_iota(jnp.int32, (rows, c), 0) % c)
    ws = jnp.where(tri, w_ref[...], 0.0).astype(BF16)
    grp = lax.broadcasted_iota(jnp.int32, (1, gw), 1) // SGU_CG
    for ci in range(chunks):
        z = jax.nn.gelu(z_ref[0, ci * c:(ci + 1) * c, :], approximate=True)
        u, v = z[:, :gw], z[:, gw:]
        mu = jnp.mean(v, axis=-1, keepdims=True)
        var = jnp.mean(jnp.square(v - mu), axis=-1, keepdims=True)
        vn = (v - mu) * lax.rsqrt(var + EPS) * lg_ref[...] + lb_ref[...]
        r = _dot(ws, vn.astype(BF16))
        mixed = bm_ref[...]
        for g in range(SGU_GROUPS):
            mixed = mixed + jnp.where(grp == g, r[g * c:(g + 1) * c, :], 0.0)
        o_ref[0, ci * c:(ci + 1) * c, :] = u * mixed


def sgu_mixer(z_d, ln_g, ln_b, w_s, b_s, chunks=4):
    bsz, t, _ = z_d.shape
    gw = GROUP_WIDTH
    tt = chunks * SGU_CHUNK
    bias = jnp.repeat(b_s.T, SGU_CG, axis=1)
    return pl.pallas_call(
        functools.partial(_sgu_kernel, chunks=chunks),
        grid=(bsz, t // tt),
        in_specs=[pl.BlockSpec((1, tt, 2 * gw), lambda b, i: (b, i, 0)),
                  pl.BlockSpec((1, gw), lambda b, i: (0, 0)),
                  pl.BlockSpec((1, gw), lambda b, i: (0, 0)),
                  pl.BlockSpec((SGU_GROUPS * SGU_CHUNK, SGU_CHUNK), lambda b, i: (0, 0)),
                  pl.BlockSpec((SGU_CHUNK, gw), lambda b, i: (0, 0))],
        out_specs=pl.BlockSpec((1, tt, gw), lambda b, i: (b, i, 0)),
        out_shape=jax.ShapeDtypeStruct((bsz, t, gw), F32),
        compiler_params=_cparams(("parallel", "parallel")),
        name="sgu_mixer",
    )(z_d, ln_g.reshape(1, gw), ln_b.reshape(1, gw),
      w_s.reshape(SGU_GROUPS * SGU_CHUNK, SGU_CHUNK), bias)


def _bucket_table():
    assert REL_MAX_DIST <= DSA_BLOCK + 1
    s = np.arange(DSA_BLOCK)[:, None]
    t = np.arange(DSA_BLOCK)[None, :]
    dist = np.stack([t - s, DSA_BLOCK + t - s, 2 * DSA_BLOCK + t - s])
    n = np.maximum(dist, 0)
    max_exact = REL_BUCKETS // 2
    nf = np.maximum(n, 1).astype(np.float32)
    large = max_exact + (np.log(nf / np.float32(max_exact)) / np.float32(math.log(REL_MAX_DIST / max_exact))
                         * np.float32(REL_BUCKETS - max_exact)).astype(np.int32)
    return np.where(n < max_exact, n, np.minimum(large, REL_BUCKETS - 1)).astype(np.int32)


def _bias_table_kernel(rb_ref, bucket_ref, o_ref):
    for back in range(3):
        bucket = bucket_ref[back]
        for h in range(DSA_HEADS):
            acc = jnp.zeros(bucket.shape, F32)
            for b in range(REL_BUCKETS):
                acc = jnp.where(bucket == b, rb_ref[b * DSA_HEADS + h], acc)
            o_ref[back, h] = acc


def rel_bias_tables(rel_bias):
    blk = DSA_BLOCK
    return pl.pallas_call(
        _bias_table_kernel,
        in_specs=[pl.BlockSpec(memory_space=pltpu.SMEM),
                  pl.BlockSpec((3, blk, blk), lambda: (0, 0, 0))],
        out_specs=pl.BlockSpec((3, DSA_HEADS, blk, blk), lambda: (0, 0, 0, 0)),
        out_shape=jax.ShapeDtypeStruct((3, DSA_HEADS, blk, blk), F32),
        name="rel_bias_tables",
    )(rel_bias.reshape(-1), jnp.asarray(_bucket_table()))


def _dsa_kernel(q_ref, k_ref, v_ref, qi_ref, kw_ref, qw_ref, toe_ref, o_ref,
                kpl_ref, vt_ref, keys_ref, khi_ref, klo_ref, acc_ref, am_ref, lg_ref, p_ref,
                *, topk, n_blocks, idx_bits):
    blk = DSA_BLOCK
    sub = DSA_SUB
    i = pl.program_id(1)
    hd = DSA_HEADS * DSA_DH
    heads_per_half = 128 // IDX_DIM

    @pl.when(i == 0)
    def _():
        lane = lax.broadcasted_iota(jnp.int32, (blk, 128), 1)

        def build(kb, c):
            r0 = pl.multiple_of(kb * blk, blk)
            ki = jnp.where(lane < IDX_DIM, kw_ref[0, pl.ds(r0, blk), :], 0.0)
            for j in range(heads_per_half):
                kpl_ref[kb, j] = (ki if j == 0 else pltpu.roll(ki, j * IDX_DIM, axis=1)).astype(BF16)
            vt_ref[kb] = jnp.transpose(v_ref[0, pl.ds(r0, blk), :].astype(F32)).astype(BF16)
            return c

        lax.fori_loop(0, n_blocks, build, 0)

    s_loc = lax.broadcasted_iota(jnp.int32, (blk, blk), 0)
    t_loc = lax.broadcasted_iota(jnp.int32, (blk, blk), 1)
    n_vis = i + 1

    qi_t = jnp.transpose(qi_ref[0].astype(F32)).astype(BF16)
    qi_halves = [qi_t[:128, :], qi_t[128:, :]]
    w_t = jnp.transpose(qw_ref[0])
    w_rows = [w_t[IDX_DIM + h:IDX_DIM + h + 1, :] * (IDX_HEADS ** -0.5) * (IDX_DIM ** -0.5)
              for h in range(IDX_HEADS)]
    s_sub = lax.broadcasted_iota(jnp.int32, (sub, blk), 0)
    t_sub = lax.broadcasted_iota(jnp.int32, (sub, blk), 1)

    def score_body(kb, c):
        for ci in range(blk // sub):
            rows = slice(ci * sub, (ci + 1) * sub)
            sc = jnp.zeros((sub, blk), F32)
            for half in range(2):
                for j in range(heads_per_half):
                    d = _dot(kpl_ref[kb, j, rows, :], qi_halves[half])
                    sc = sc + jnp.maximum(d, 0.0) * w_rows[half * heads_per_half + j]
            sc = jnp.where(sc == 0.0, 0.0, sc)
            bits = pltpu.bitcast(sc, jnp.int32)
            key = jnp.where(bits < 0, bits ^ jnp.int32(0x7FFFFFFF), bits)
            vis = (kb < i) | (s_sub + ci * sub <= t_sub)
            key = jnp.where(vis, key, jnp.int32(INT_MIN))
            keys_ref[kb, rows, :] = key
            khi_ref[kb, rows, :] = lax.shift_right_arithmetic(key, 16).astype(jnp.int16)
            klo_ref[kb, rows, :] = ((key & 0xFFFF) - 2 ** 15).astype(jnp.int16)
        return c

    lax.fori_loop(0, n_vis, score_body, 0)

    def count(pred):
        def body(kb, acc):
            hit = jnp.where(pred(keys_ref[kb], kb), 1.0, 0.0)
            return acc + jnp.sum(hit.reshape(blk // 32, 32, blk), axis=0)
        acc = lax.fori_loop(0, n_vis, body, jnp.zeros((32, blk), F32))
        return jnp.sum(acc, axis=0, keepdims=True)

    def count16(ref, cand):
        cand = cand.astype(jnp.int16)

        def body(kb, acc):
            hit = jnp.where(ref[kb] >= cand, jnp.int16(1), jnp.int16(0))
            for j in range(blk // 32):
                acc = acc + hit[j * 32:(j + 1) * 32]
            return acc
        acc = lax.fori_loop(0, n_vis, body, jnp.zeros((32, blk), jnp.int16))
        return jnp.sum(acc.astype(jnp.int32).astype(F32), axis=0, keepdims=True)

    def search16(ref, offset):
        lowest = jnp.full((1, blk), -(2 ** 15), jnp.int32)
        base = jnp.where(offset + count16(ref, jnp.zeros((1, blk), jnp.int32)) >= kf, 0, lowest)

        def bit_body(it, base):
            cand = base | lax.shift_left(jnp.int32(1), 14 - it)
            return jnp.where(offset + count16(ref, cand) >= kf, cand, base)
        return lax.fori_loop(0, 15, bit_body, base)

    kf = float(topk)
    zero = jnp.zeros((1, blk), jnp.int32)
    thr_hi = search16(khi_ref, 0.0)
    thr_hi16 = thr_hi.astype(jnp.int16)

    def low_body(kb, above):
        hi = khi_ref[kb]
        klo_ref[kb] = jnp.where(hi == thr_hi16, klo_ref[kb], jnp.int16(-(2 ** 15)))
        hit = jnp.where(hi > thr_hi16, jnp.int16(1), jnp.int16(0))
        for j in range(blk // 32):
            above = above + hit[j * 32:(j + 1) * 32]
        return above

    above = lax.fori_loop(0, n_vis, low_body, jnp.zeros((32, blk), jnp.int16))
    n_above = jnp.sum(above.astype(jnp.int32).astype(F32), axis=0, keepdims=True)
    thr_lo = search16(klo_ref, n_above)
    thr = lax.shift_left(thr_hi, 16) | (thr_lo + 2 ** 15)
    thr_sel = jnp.maximum(thr, jnp.int32(INT_MIN + 1))

    n_gt = count(lambda key, kb: key > thr)
    n_eq = count(lambda key, kb: key == thr)
    need = kf - n_gt
    excess = jnp.where((n_eq > need) & (thr > jnp.int32(INT_MIN)), 1.0, 0.0)

    def tie_search():
        def tie_body(it, j):
            cand = j | lax.shift_left(jnp.int32(1), idx_bits - 1 - it)
            below = count(lambda key, kb: (key == thr) & (kb * blk + s_loc < cand))
            return jnp.where(below < need, cand, j)
        return lax.fori_loop(0, idx_bits, tie_body, zero)

    last = lax.cond(jnp.max(excess) > 0.0, tie_search, lambda: jnp.full((1, blk), 2 ** 30, jnp.int32))

    assert DSA_DH ** -0.5 == 0.125
    q_t = jnp.transpose(q_ref[0].astype(F32) * (DSA_DH ** -0.5))
    row_h = lax.broadcasted_iota(jnp.int32, (hd, 1), 0) // DSA_DH
    q_heads = [jnp.where(row_h == h, q_t, 0.0).astype(BF16) for h in range(DSA_HEADS)]
    acc_ref[...] = jnp.zeros_like(acc_ref)
    n_sub = blk // sub

    def att_body(kb, carry):
        ms, ls = carry
        back = jnp.minimum(i - kb, 2)
        for ci in range(n_sub):
            rows = slice(ci * sub, (ci + 1) * sub)
            key = keys_ref[kb, rows, :]
            sel = (key > thr_sel) | ((key == thr_sel) & (kb * blk + ci * sub + s_sub <= last))
            am_ref[rows, :] = jnp.where(sel, 0.0, NEG_BIG)
        new_ms, new_ls = list(ms), list(ls)
        for heads in DSA_HEAD_GROUPS:
            alphas = {}
            for h in heads:
                pm = jnp.full((8, blk), NEG_BIG, F32)
                for ci in range(n_sub):
                    rows = slice(ci * sub, (ci + 1) * sub)
                    k_rows = k_ref[0, pl.ds(pl.multiple_of(kb * blk + ci * sub, sub), sub), :]
                    lg = _dot(k_rows, q_heads[h]) + toe_ref[back, h, rows, :] + am_ref[rows, :]
                    lg_ref[h, rows, :] = lg
                    pm = jnp.maximum(pm, jnp.max(lg.reshape(sub // 8, 8, blk), axis=0))
                new_ms[h] = jnp.maximum(ms[h], jnp.max(pm, axis=0, keepdims=True))
                alphas[h] = jnp.exp(ms[h] - new_ms[h])
            for h in heads:
                ps = jnp.zeros((8, blk), F32)
                for ci in range(n_sub):
                    rows = slice(ci * sub, (ci + 1) * sub)
                    p = jnp.exp(lg_ref[h, rows, :] - new_ms[h])
                    ps = ps + jnp.sum(p.reshape(sub // 8, 8, blk), axis=0)
                    p_ref[h, rows, :] = p.astype(BF16)
                new_ls[h] = ls[h] * alphas[h] + jnp.sum(ps, axis=0, keepdims=True)
            for h in heads:
                hrows = slice(h * DSA_DH, (h + 1) * DSA_DH)
                acc_ref[hrows, :] = acc_ref[hrows, :] * alphas[h] + _dot(vt_ref[kb, hrows, :], p_ref[h])
        return tuple(new_ms), tuple(new_ls)

    init = (tuple(jnp.full((1, blk), 0.01 * NEG_BIG, F32) for _ in range(DSA_HEADS)),
            tuple(jnp.zeros((1, blk), F32) for _ in range(DSA_HEADS)))
    _, ls = lax.fori_loop(0, n_vis, att_body, init)
    for h in range(DSA_HEADS):
        rows = slice(h * DSA_DH, (h + 1) * DSA_DH)
        acc_ref[rows, :] = acc_ref[rows, :] * (1.0 / ls[h])
    o_ref[0] = jnp.transpose(acc_ref[...])


def dsa_mixer(z_c, z_aux, toe):
    bsz, t, _ = z_c.shape
    blk = DSA_BLOCK
    n_blocks = t // blk
    assert t % blk == 0
    topk = min(DSA_TOPK_MAX, t // 4)
    hd = DSA_HEADS * DSA_DH
    kernel = functools.partial(_dsa_kernel, topk=topk, n_blocks=n_blocks,
                               idx_bits=max(1, (t - 1).bit_length()))
    return pl.pallas_call(
        kernel,
        grid=(bsz, n_blocks),
        in_specs=[pl.BlockSpec((1, blk, hd), lambda b, i: (b, i, 0)),
                  pl.BlockSpec((1, t, hd), lambda b, i: (b, 0, 1)),
                  pl.BlockSpec((1, t, hd), lambda b, i: (b, 0, 2)),
                  pl.BlockSpec((1, blk, hd), lambda b, i: (b, i, 3)),
                  pl.BlockSpec((1, t, C_AUX), lambda b, i: (b, 0, 0)),
                  pl.BlockSpec((1, blk, C_AUX), lambda b, i: (b, i, 0)),
                  pl.BlockSpec((3, DSA_HEADS, blk, blk), lambda b, i: (0, 0, 0, 0))],
        out_specs=pl.BlockSpec((1, blk, hd), lambda b, i: (b, i, 0)),
        out_shape=jax.ShapeDtypeStruct((bsz, t, hd), F32),
        scratch_shapes=[pltpu.VMEM((n_blocks, 128 // IDX_DIM, blk, 128), BF16),
                        pltpu.VMEM((n_blocks, hd, blk), BF16),
                        pltpu.VMEM((n_blocks, blk, blk), jnp.int32),
                        pltpu.VMEM((n_blocks, blk, blk), jnp.int16),
                        pltpu.VMEM((n_blocks, blk, blk), jnp.int16),
                        pltpu.VMEM((hd, blk), F32),
                        pltpu.VMEM((blk, blk), F32),
                        pltpu.VMEM((DSA_HEADS, blk, blk), F32),
                        pltpu.VMEM((DSA_HEADS, blk, blk), BF16)],
        compiler_params=_cparams(("parallel", "arbitrary")),
        name="dsa_mixer",
    )(z_c, z_c, z_c, z_c, z_aux, z_aux, toe)


def _mixer_tail_kernel(h_ref, a_ref, b_ref, c_ref, d_ref, wout_ref, gxa_ref, wq_ref, k_ref, v_ref, wo_ref,
                       *refs, route):
    gw = GROUP_WIDTH
    x = h_ref[0]
    for gi, r in enumerate((a_ref, b_ref, c_ref, d_ref)):
        x = x + _dot(r[0].astype(BF16), wout_ref[gi * gw:(gi + 1) * gw, :])

    hd = XA_HEADS * XA_DH
    q = _dot(_rms(x, gxa_ref[...]).astype(BF16), wq_ref[...]).astype(BF16)
    k = k_ref[0]
    v = v_ref[0]
    lane_h = lax.broadcasted_iota(jnp.int32, (1, hd), 1) // XA_DH
    o = jnp.zeros((x.shape[0], hd), F32)
    for h in range(XA_HEADS):
        s = _dot_nt(q, jnp.where(lane_h == h, k, 0.0).astype(BF16)) * (XA_DH ** -0.5)
        p = jnp.exp(s - jnp.max(s, axis=-1, keepdims=True))
        p = p / jnp.sum(p, axis=-1, keepdims=True)
        o = o + _dot(p.astype(BF16), jnp.where(lane_h == h, v, 0.0).astype(BF16))
    x = x + _dot(o.astype(BF16), wo_ref[...])

    if not route:
        o_ref, = refs
        o_ref[0] = x
        return
    gffn_ref, wr_ref, o_ref, hn_ref, gate_ref, cnt_ref = refs
    o_ref[0] = x
    hn = _rms(x, gffn_ref[...])
    hn_ref[0] = hn.astype(BF16)
    tm = hn.shape[0]
    lane = lax.broadcasted_iota(jnp.int32, (tm, GATE_LANES), 1)
    logits = jnp.where(lane < N_EXPERTS, _dot_3pass(hn, wr_ref[...]), -jnp.inf)
    m1 = jnp.max(logits, axis=-1, keepdims=True)
    i1 = jnp.min(jnp.where(logits == m1, lane, GATE_LANES), axis=-1, keepdims=True)
    rest = jnp.where(lane == i1, -jnp.inf, logits)
    m2 = jnp.max(rest, axis=-1, keepdims=True)
    i2 = jnp.min(jnp.where(rest == m2, lane, GATE_LANES), axis=-1, keepdims=True)
    e2 = jnp.exp(m2 - m1)
    g1 = 1.0 / (1.0 + e2)
    gates = jnp.where(lane == i1, g1, 0.0) + jnp.where(lane == i2, e2 * g1, 0.0)
    gates = jnp.where(lane == N_EXPERTS, i1.astype(F32), gates)
    gate_ref[0] = jnp.where(lane == N_EXPERTS + 1, i2.astype(F32), gates)
    sel = jnp.where((lane == i1) | (lane == i2), 1.0, 0.0)
    for c in range(tm // MOE_CHUNK):
        cnt_ref[c] = jnp.sum(sel[c * MOE_CHUNK:(c + 1) * MOE_CHUNK, :], axis=0, keepdims=True)


def mixer_tail(h3, mixers, w_out, g_xa, wq, k, v, wo, routing=None, tm=1024):
    bsz, t, d = h3.shape
    tm = min(tm, t)
    mlen = k.shape[1]
    hd = XA_HEADS * XA_DH
    gw = GROUP_WIDTH
    row = lambda b, i: (b, i, 0)
    fixed2 = lambda b, i: (0, 0)
    in_specs = ([pl.BlockSpec((1, tm, d), row)] + [pl.BlockSpec((1, tm, gw), row)] * 4
                + [pl.BlockSpec((4 * gw, d), fixed2), pl.BlockSpec((1, d), fixed2), pl.BlockSpec((d, hd), fixed2),
                   pl.BlockSpec((1, mlen, hd), lambda b, i: (b, 0, 0)),
                   pl.BlockSpec((1, mlen, hd), lambda b, i: (b, 0, 0)),
                   pl.BlockSpec((hd, d), fixed2)])
    args = [h3, *mixers, w_out, g_xa.reshape(1, d), wq, k, v, wo]
    out_specs = [pl.BlockSpec((1, tm, d), row)]
    out_shape = [jax.ShapeDtypeStruct((bsz, t, d), F32)]
    if routing is not None:
        assert tm % MOE_CHUNK == 0
        g_ffn, wr = routing
        in_specs += [pl.BlockSpec((1, d), fixed2), pl.BlockSpec((d, GATE_LANES), fixed2)]
        args += [g_ffn.reshape(1, d), jnp.zeros((d, GATE_LANES), F32).at[:, :N_EXPERTS].set(wr)]
        per_step = tm // MOE_CHUNK
        out_specs += [pl.BlockSpec((1, tm, d), row), pl.BlockSpec((1, tm, GATE_LANES), row),
                      pl.BlockSpec((per_step, 1, GATE_LANES), lambda b, i: (b * (t // tm) + i, 0, 0))]
        out_shape += [jax.ShapeDtypeStruct((bsz, t, d), BF16), jax.ShapeDtypeStruct((bsz, t, GATE_LANES), F32),
                      jax.ShapeDtypeStruct((bsz * t // MOE_CHUNK, 1, GATE_LANES), F32)]
    return pl.pallas_call(
        functools.partial(_mixer_tail_kernel, route=routing is not None),
        grid=(bsz, t // tm),
        in_specs=in_specs,
        out_specs=out_specs,
        out_shape=out_shape,
        compiler_params=_cparams(("parallel", "parallel")),
        name="mixer_tail",
    )(*args)


def _ffn_kernel(h_ref, g_ref, w1_ref, w3_ref, w2_ref, o_ref, *, tf):
    x = h_ref[...]
    hn = _rms(x, g_ref[...]).astype(BF16)
    acc = x
    for f0 in range(0, w1_ref.shape[1], tf):
        a = _dot(hn, w1_ref[:, f0:f0 + tf])
        b = _dot(hn, w3_ref[:, f0:f0 + tf])
        acc = acc + _dot((a * jax.nn.sigmoid(a) * b).astype(BF16), w2_ref[f0:f0 + tf, :])
    o_ref[...] = acc


def ffn(h, g, w1, w3, w2, tm=512, tf=704):
    m, d = h.shape
    tm = min(tm, m)
    nf = w1.shape[1]
    assert nf % tf == 0
    resident = lambda shape: pl.BlockSpec(shape, lambda i: (0, 0), pipeline_mode=pl.Buffered(1))
    return pl.pallas_call(
        functools.partial(_ffn_kernel, tf=tf),
        grid=(m // tm,),
        in_specs=[pl.BlockSpec((tm, d), lambda i: (i, 0)),
                  pl.BlockSpec((1, d), lambda i: (0, 0)),
                  resident((d, nf)), resident((d, nf)), resident((nf, d))],
        out_specs=pl.BlockSpec((tm, d), lambda i: (i, 0)),
        out_shape=jax.ShapeDtypeStruct((m, d), F32),
        compiler_params=_cparams(("parallel",)),
        name="ffn",
    )(h, g.reshape(1, d), w1, w3, w2)


GATE_LANES = 128
MOE_CHUNK = 256
MOE_ROWS = 128
MOE_TOKENS = 2048
MOE_ALIGN = 16


def _moe_kernel(cb_ref, hn_ref, gate_ref, w1_ref, w3_ref, w2_ref, y_ref,
                rank_row, rk_ref, gs_ref, xs_ref, yacc_ref, rc_ref, gc_ref):
    t = pl.program_id(0)
    e = pl.program_id(1)
    f = pl.program_id(2)
    ts, d = hn_ref.shape
    ch, rb = MOE_CHUNK, MOE_ROWS
    win = 2 * rb
    n_ch = ts // ch
    lane = lax.broadcasted_iota(jnp.int32, (1, GATE_LANES), 1)

    def before(c):
        return cb_ref[(t * (n_ch + 1) + c) * N_EXPERTS + e]

    n_blocks = (before(n_ch) + rb - 1) // rb
    n_windows = (n_blocks * rb + win - 1) // win + 1

    def windows(c):
        lo, hi = before(c), before(c + 1)
        s0 = (lo // MOE_ALIGN) * MOE_ALIGN
        return s0, jnp.where(hi > lo, (hi - s0 + win - 1) // win, 0)

    spans = [windows(c) for c in range(n_ch)]
    single = functools.reduce(jnp.logical_and, [n_win <= 1 for _, n_win in spans])

    def for_each_window(visit):
        @pl.when(single)
        def _():
            for c, (s0, _) in enumerate(spans):
                visit(c, s0)

        @pl.when(jnp.logical_not(single))
        def _():
            for c, (s0, n_win) in enumerate(spans):
                def body(j, carry, c=c, s0=s0):
                    visit(c, s0 + j * win)
                    return carry
                lax.fori_loop(0, n_win, body, 0)

    @pl.when((e == 0) & (f == 0))
    def _():
        y_ref[...] = jnp.zeros_like(y_ref)
        strict_lower = (lax.broadcasted_iota(jnp.int32, (ch, ch), 1)
                        < lax.broadcasted_iota(jnp.int32, (ch, ch), 0)).astype(BF16)
        lane_f = lane.astype(F32)
        offs = jnp.ones((1, GATE_LANES), F32)
        for c in range(n_ch):
            rows = slice(c * ch, (c + 1) * ch)
            g = gate_ref[rows, :]
            sel = jnp.where((lane_f == g[:, N_EXPERTS:N_EXPERTS + 1])
                            | (lane_f == g[:, N_EXPERTS + 1:N_EXPERTS + 2]), 1.0, 0.0)
            r = _dot(strict_lower, sel.astype(BF16)) + offs
            r = jnp.where(sel > 0.0, r, 0.0)
            rank_row[:, rows] = jnp.transpose(r) - 1.0
            high = jnp.floor(r * (1.0 / 256.0))
            rk_ref[0, rows, :] = high.astype(BF16)
            rk_ref[1, rows, :] = (r - 256.0 * high).astype(BF16)
            for j, part in enumerate(_split_bf16(g, 3)):
                gs_ref[j, rows, :] = part
            offs = offs + jnp.sum(sel, axis=0, keepdims=True)

    @pl.when(f == 0)
    def _():
        pick = (lax.broadcasted_iota(jnp.int32, (GATE_LANES, GATE_LANES), 0) == e).astype(BF16)
        for c in range(n_ch):
            rows = slice(c * ch, (c + 1) * ch)
            rc_ref[rows, :] = 256.0 * _dot(rk_ref[0, rows, :], pick) + _dot(rk_ref[1, rows, :], pick) - 1.0
            gc_ref[rows, :] = (_dot(gs_ref[0, rows, :], pick) + _dot(gs_ref[1, rows, :], pick)
                               + _dot(gs_ref[2, rows, :], pick))

        def clear(j, carry):
            r0 = pl.multiple_of(j * win, win)
            xs_ref[pl.ds(r0, win), :] = jnp.zeros((win, d), BF16)
            yacc_ref[pl.ds(r0, win), :] = jnp.zeros((win, d), F32)
            return carry

        lax.fori_loop(0, n_windows, clear, 0)

        row_id = lax.broadcasted_iota(jnp.int32, (win, ch), 0)

        def gather_window(c, s):
            s = pl.multiple_of(s, MOE_ALIGN)
            ranks = rank_row[pl.ds(e, 1), c * ch:(c + 1) * ch]
            onehot = jnp.where(ranks == (s + row_id).astype(F32), 1.0, 0.0).astype(BF16)
            xs_ref[pl.ds(s, win), :] += _dot(onehot, hn_ref[c * ch:(c + 1) * ch, :]).astype(BF16)

        for_each_window(gather_window)

    def ffn_rows(r0, rows):
        x = xs_ref[pl.ds(r0, rows), :]
        a = _dot(x, w1_ref[0])
        g3 = _dot(x, w3_ref[0])
        yacc_ref[pl.ds(r0, rows), :] += _dot((a * jax.nn.sigmoid(a) * g3).astype(BF16), w2_ref[0])

    n_quads = n_blocks // 4

    def ffn_quad(j, carry):
        ffn_rows(pl.multiple_of(j * (4 * rb), 4 * rb), 4 * rb)
        return carry

    lax.fori_loop(0, n_quads, ffn_quad, 0)
    tail = pl.multiple_of(n_quads * (4 * rb), 4 * rb)

    @pl.when((n_blocks & 2) != 0)
    def _():
        ffn_rows(tail, 2 * rb)

    @pl.when((n_blocks & 1) != 0)
    def _():
        ffn_rows(pl.multiple_of(tail + (n_blocks & 2) * rb, rb), rb)

    @pl.when(f == pl.num_programs(2) - 1)
    def _():
        def to_bf16(j, carry):
            r0 = pl.multiple_of(j * win, win)
            xs_ref[pl.ds(r0, win), :] = yacc_ref[pl.ds(r0, win), :].astype(BF16)
            return carry

        lax.fori_loop(0, n_windows, to_bf16, 0)

        lane_id = lax.broadcasted_iota(jnp.int32, (ch, rb), 1).astype(F32)

        def scatter_window(c, s):
            rows = slice(c * ch, (c + 1) * ch)
            s = pl.multiple_of(s, MOE_ALIGN)
            rank = rc_ref[rows, :] - s.astype(F32)
            onehot = jnp.concatenate([jnp.where(rank == lane_id, 1.0, 0.0),
                                      jnp.where(rank == lane_id + float(rb), 1.0, 0.0)],
                                     axis=1).astype(BF16)
            gate = jnp.concatenate([gc_ref[rows, :]] * (d // GATE_LANES), axis=1)
            y_ref[rows, :] += gate * _dot(onehot, xs_ref[pl.ds(s, win), :])

        for_each_window(scatter_window)


def moe(hn, gates, cnt, w1, w3, w2, tf=896):
    m, d = hn.shape
    ts = min(MOE_TOKENS, m)
    n_exp, _, nf = w1.shape
    assert nf % tf == 0 and m % ts == 0 and ts % MOE_CHUNK == 0 and MOE_ROWS == GATE_LANES
    n_ch = ts // MOE_CHUNK
    counts = cnt.reshape(m // ts, n_ch, GATE_LANES)[:, :, :n_exp].astype(jnp.int32)
    before = jnp.concatenate([jnp.zeros((m // ts, 1, n_exp), jnp.int32), jnp.cumsum(counts, axis=1)], axis=1)
    grid_spec = pltpu.PrefetchScalarGridSpec(
        num_scalar_prefetch=1,
        grid=(m // ts, n_exp, nf // tf),
        in_specs=[pl.BlockSpec((ts, d), lambda t, e, f, cb: (t, 0), pipeline_mode=pl.Buffered(1)),
                  pl.BlockSpec((ts, GATE_LANES), lambda t, e, f, cb: (t, 0), pipeline_mode=pl.Buffered(1)),
                  pl.BlockSpec((1, d, tf), lambda t, e, f, cb: (e, 0, f)),
                  pl.BlockSpec((1, d, tf), lambda t, e, f, cb: (e, 0, f)),
                  pl.BlockSpec((1, tf, d), lambda t, e, f, cb: (e, f, 0))],
        out_specs=pl.BlockSpec((ts, d), lambda t, e, f, cb: (t, 0), pipeline_mode=pl.Buffered(1)),
        scratch_shapes=[pltpu.VMEM((GATE_LANES, ts), F32),
                        pltpu.VMEM((2, ts, GATE_LANES), BF16),
                        pltpu.VMEM((3, ts, GATE_LANES), BF16),
                        pltpu.VMEM((ts + 2 * MOE_ROWS, d), BF16),
                        pltpu.VMEM((ts + 2 * MOE_ROWS, d), F32),
                        pltpu.VMEM((ts, MOE_ROWS), F32),
                        pltpu.VMEM((ts, GATE_LANES), F32)])
    return pl.pallas_call(
        _moe_kernel,
        grid_spec=grid_spec,
        out_shape=jax.ShapeDtypeStruct((m, d), F32),
        compiler_params=_cparams(("parallel", "arbitrary", "arbitrary")),
        name="moe",
    )(before.reshape(-1), hn, gates, w1, w3, w2)


def _final_norm_kernel(g_ref, *refs):
    *x_refs, o_ref = refs
    x = x_refs[0][...]
    for r in x_refs[1:]:
        x = x + r[...]
    o_ref[...] = _rms(x, g_ref[...])


def final_rmsnorm(g, *xs, tm=1024):
    m, d = xs[0].shape
    tm = min(tm, m)
    return pl.pallas_call(
        _final_norm_kernel,
        grid=(m // tm,),
        in_specs=[pl.BlockSpec((1, d), lambda i: (0, 0))] + [pl.BlockSpec((tm, d), lambda i: (i, 0))] * len(xs),
        out_specs=pl.BlockSpec((tm, d), lambda i: (i, 0)),
        out_shape=jax.ShapeDtypeStruct((m, d), F32),
        compiler_params=_cparams(("parallel",)),
        name="final_norm",
    )(g.reshape(1, d), *xs)


def _pad_cols(w, width):
    return jnp.pad(w, ((0, 0), (0, width - w.shape[1])))


def _in_proj_weight(w_in):
    a_end = 2 * GLA_HEADS * GLA_DK + 2 * GROUP_WIDTH + GLA_RANK
    b_end = a_end + GROUP_WIDTH
    c_end = b_end + 3 * GROUP_WIDTH + IDX_HEADS * IDX_DIM + IDX_DIM + IDX_HEADS
    assert w_in.shape[1] == c_end + 2 * GROUP_WIDTH
    return jnp.concatenate([_pad_cols(w_in[:, :a_end], A_WIDTH), w_in[:, a_end:b_end],
                            _pad_cols(w_in[:, b_end:c_end], C_MAIN + C_AUX), w_in[:, c_end:]], axis=1).astype(BF16)


def hybrid_layer(h, mem2, toe, p, bsz, t, routing=None):
    m, d = h.shape
    z_a, z_b, z_c, z_aux, z_d = norm_matmul(h, p["norm_mix"], _in_proj_weight(p["w_in"]),
                                            (A_WIDTH, B_WIDTH, C_MAIN, C_AUX, D_WIDTH),
                                            (F32, F32, BF16, F32, F32))
    o_a = gla_mixer(z_a.reshape(bsz, t, A_WIDTH), p["gla_wa2"], p["gla_ba"], p["gla_norm"])
    o_b = pool_mixer(z_b.reshape(bsz, t, B_WIDTH), p["pool_w"], p["pool_b"].reshape(-1), p["pool_scale"])
    o_c = dsa_mixer(z_c.reshape(bsz, t, C_MAIN), z_aux.reshape(bsz, t, C_AUX), toe)
    o_d = sgu_mixer(z_d.reshape(bsz, t, D_WIDTH), p["sgu_ln_g"], p["sgu_ln_b"], p["sgu_w"], p["sgu_b"])
    hd = XA_HEADS * XA_DH
    k, v = norm_matmul(mem2, p["norm_mem"], p["xa_wkv"].astype(BF16), (hd, hd))
    mlen = mem2.shape[0] // bsz
    outs = mixer_tail(h.reshape(bsz, t, d), (o_a, o_b, o_c, o_d), p["w_out"].astype(BF16), p["norm_xa"],
                      p["xa_wq"].astype(BF16), k.reshape(bsz, mlen, hd), v.reshape(bsz, mlen, hd),
                      p["xa_wo"].astype(BF16), routing)
    if routing is None:
        return (outs[0].reshape(m, d),)
    h3, hn, gates, counts = outs
    return h3.reshape(m, d), hn.reshape(m, d), gates.reshape(m, GATE_LANES), counts


def kernel(x, mem, rel_bias, final_norm, norm_mix, w_in, gla_wa2, gla_ba, gla_norm, pool_w, pool_b,
           pool_scale, sgu_ln_g, sgu_ln_b, sgu_w, sgu_b, w_out, norm_xa, norm_mem, xa_wq, xa_wkv, xa_wo,
           norm_ffn, ffn_w1, ffn_w3, ffn_w2, router, moe_w1, moe_w3, moe_w2):
    bsz, t, d = x.shape
    depth = norm_mix.shape[0]
    h = x.reshape(bsz * t, d)
    mem2 = mem.reshape(-1, d)
    toe = rel_bias_tables(rel_bias)
    pending = None
    for i in range(depth):
        if pending is not None:
            h, pending = h + pending, None
        p = dict(norm_mix=norm_mix[i], w_in=w_in[i], gla_wa2=gla_wa2[i], gla_ba=gla_ba[i],
                 gla_norm=gla_norm[i], pool_w=pool_w[i], pool_b=pool_b[i], pool_scale=pool_scale[i],
                 sgu_ln_g=sgu_ln_g[i], sgu_ln_b=sgu_ln_b[i], sgu_w=sgu_w[i], sgu_b=sgu_b[i],
                 w_out=w_out[i], norm_xa=norm_xa[i], norm_mem=norm_mem[i], xa_wq=xa_wq[i],
                 xa_wkv=xa_wkv[i], xa_wo=xa_wo[i])
        j = i // 2
        if i % 2 == 0:
            h, = hybrid_layer(h, mem2, toe, p, bsz, t)
            h = ffn(h, norm_ffn[i], ffn_w1[j].astype(BF16), ffn_w3[j].astype(BF16), ffn_w2[j].astype(BF16))
        else:
            h, hn, gates, counts = hybrid_layer(h, mem2, toe, p, bsz, t, routing=(norm_ffn[i], router[j]))
            pending = moe(hn, gates, counts, moe_w1[j].astype(BF16), moe_w3[j].astype(BF16),
                          moe_w2[j].astype(BF16))
    xs = (h,) if pending is None else (h, pending)
    return final_rmsnorm(final_norm, *xs).reshape(bsz, t, d)
```

```python
import functools
import math

import jax
import jax.numpy as jnp
import numpy as np
from jax import lax
from jax.experimental import pallas as pl
from jax.experimental.pallas import tpu as pltpu

F32 = jnp.float32
BF16 = jnp.bfloat16
EPS = 1e-6

GROUP_WIDTH = 256

GLA_HEADS = 4
GLA_DV = 64
GLA_DK = 32
GLA_RANK = 16
GLA_TAU = 16.0
GLA_CHUNK = 64
GLA_GROUP = 8

POOL_WINDOWS = (2, 4, 8, 16)
POOL_CG = 64

DSA_HEADS = 4
DSA_DH = 64
IDX_HEADS = 8
IDX_DIM = 32
DSA_TOPK_MAX = 256
DSA_BLOCK = 256
DSA_SUB = 64
DSA_HEAD_GROUPS = ((0, 1, 2, 3),)

SGU_GROUPS = 4
SGU_CHUNK = 128
SGU_CG = 64

REL_BUCKETS = 32
REL_MAX_DIST = 128

XA_HEADS = 4
XA_DH = 64

N_EXPERTS = 8

A_WIDTH = 896
B_WIDTH = 256
C_MAIN = 1024
C_AUX = 128
D_WIDTH = 512

INT_MIN = -(2 ** 31)
NEG_BIG = -1e30
VMEM_LIMIT = 56 * 1024 * 1024


def _cparams(sem):
    return pltpu.CompilerParams(dimension_semantics=sem, vmem_limit_bytes=VMEM_LIMIT)


def _dot(a, b):
    return jnp.dot(a, b, preferred_element_type=F32)


def _dot_nt(a, b):
    return lax.dot_general(a, b, (((1,), (1,)), ((), ())), preferred_element_type=F32)


def _dot_tn(a, b):
    return lax.dot_general(a, b, (((0,), (0,)), ((), ())), preferred_element_type=F32)


def _dot_f32(a, b):
    return jnp.dot(a, b, preferred_element_type=F32, precision=lax.Precision.HIGHEST)


def _split_bf16(x, terms):
    parts = []
    for _ in range(terms):
        p = x.astype(BF16)
        parts.append(p)
        x = x - p.astype(F32)
    return parts


def _dot_exact_rhs(a, b, terms):
    out = None
    for p in _split_bf16(a, terms):
        d = _dot(p, b)
        out = d if out is None else out + d
    return out


def _dot_exact_lhs(a, b, terms):
    out = None
    for p in _split_bf16(b, terms):
        d = _dot(a, p)
        out = d if out is None else out + d
    return out


def _dot_3pass(a, b):
    a_hi, a_lo = _split_bf16(a, 2)
    b_hi, b_lo = _split_bf16(b, 2)
    return _dot(a_hi, b_hi) + (_dot(a_hi, b_lo) + _dot(a_lo, b_hi))


def _rms(x, g):
    return x * lax.rsqrt(jnp.mean(x * x, axis=-1, keepdims=True) + EPS) * g


def _norm_matmul_kernel(h_ref, g_ref, w_ref, *out_refs, widths):
    hb = _rms(h_ref[...], g_ref[...]).astype(BF16)
    off = 0
    for o_ref, wd in zip(out_refs, widths):
        o_ref[...] = _dot(hb, w_ref[:, off:off + wd]).astype(o_ref.dtype)
        off += wd


def norm_matmul(h, g, w, widths, dtypes=None, tm=512):
    dtypes = dtypes or (F32,) * len(widths)
    m, d = h.shape
    tm = min(tm, m)
    n = sum(widths)
    return pl.pallas_call(
        functools.partial(_norm_matmul_kernel, widths=widths),
        grid=(m // tm,),
        in_specs=[pl.BlockSpec((tm, d), lambda i: (i, 0)),
                  pl.BlockSpec((1, d), lambda i: (0, 0)),
                  pl.BlockSpec((d, n), lambda i: (0, 0))],
        out_specs=[pl.BlockSpec((tm, wd), lambda i: (i, 0)) for wd in widths],
        out_shape=[jax.ShapeDtypeStruct((m, wd), dt) for wd, dt in zip(widths, dtypes)],
        compiler_params=_cparams(("parallel",)),
        name="norm_matmul",
    )(h, g.reshape(1, d), w)


def _log_sigmoid(x):
    return jnp.minimum(x, 0.0) - jnp.log1p(jnp.exp(-jnp.abs(x)))


def _gla_kernel(z_ref, wa2_ref, ba_ref, ng_ref, o_ref, s_ref, *, n_groups):
    c = GLA_CHUNK
    grp = GLA_GROUP * c
    hk = GLA_HEADS * GLA_DK
    hv = GLA_HEADS * GLA_DV
    s_ref[...] = jnp.zeros_like(s_ref)

    head_k = lax.broadcasted_iota(jnp.int32, (1, hk), 1) // GLA_DK
    head_v = lax.broadcasted_iota(jnp.int32, (1, hv), 1) // GLA_DV
    g_row = lax.broadcasted_iota(jnp.int32, (grp, grp), 0)
    g_col = lax.broadcasted_iota(jnp.int32, (grp, grp), 1)
    tril = (((g_row // c) == (g_col // c)) & (g_col <= g_row)).astype(BF16)
    causal4 = (lax.broadcasted_iota(jnp.int32, (GLA_HEADS * c, c), 1)
               <= lax.broadcasted_iota(jnp.int32, (GLA_HEADS * c, c), 0) % c)
    state_mask = (lax.broadcasted_iota(jnp.int32, (hk, hv), 0) // GLA_DK
                  == lax.broadcasted_iota(jnp.int32, (hk, hv), 1) // GLA_DV)
    norm_mat = jnp.where(lax.broadcasted_iota(jnp.int32, (hv, hv), 0) // GLA_DV
                         == lax.broadcasted_iota(jnp.int32, (hv, hv), 1) // GLA_DV,
                         1.0 / GLA_DV, 0.0).astype(BF16)
    wa2 = wa2_ref[...]
    ba = ba_ref[...]
    ng = ng_ref[...]

    def body(n, carry):
        r0 = pl.multiple_of(n * grp, grp)
        z = z_ref[0, pl.ds(r0, grp), :]
        q, k, v, g, lr = z[:, 0:128], z[:, 128:256], z[:, 256:512], z[:, 512:768], z[:, 768:896]
        log_a = _log_sigmoid(_dot_f32(lr, wa2) + ba) / GLA_TAU
        b = _dot_exact_lhs(tril, log_a, 3)
        b_end = jnp.concatenate([jnp.broadcast_to(b[(ci + 1) * c - 1:(ci + 1) * c, :], (c, hk))
                                 for ci in range(GLA_GROUP)], axis=0)
        q_t = q * (GLA_DK ** -0.5) * jnp.exp(b)
        q_tb = q_t.astype(BF16)
        k_t = (k * jnp.exp(-b)).astype(BF16)
        k_dec = (k * jnp.exp(b_end - b)).astype(BF16)
        vb = v.astype(BF16)
        outs = []
        for ci in range(GLA_GROUP):
            rows = slice(ci * c, (ci + 1) * c)
            q4 = jnp.concatenate([jnp.where(head_k == h, q_t[rows], 0.0) for h in range(GLA_HEADS)],
                                 axis=0).astype(BF16)
            att = jnp.where(causal4, _dot_nt(q4, k_t[rows]), 0.0)
            r = _dot(att.astype(BF16), vb[rows])
            o = _dot(q_tb[rows], s_ref[...].astype(BF16))
            for h in range(GLA_HEADS):
                o = o + jnp.where(head_v == h, r[h * c:(h + 1) * c, :], 0.0)
            outs.append(o)
            kv = jnp.where(state_mask, _dot_tn(k_dec[rows], vb[rows]), 0.0)
            last = b_end[ci * c:ci * c + 1, :]
            dec = jnp.exp(jnp.transpose(jnp.broadcast_to(last, (hk, hk))))
            s_ref[...] = s_ref[...] * jnp.concatenate([dec, dec], axis=1) + kv
        o = jnp.concatenate(outs, axis=0)
        o = o * lax.rsqrt(_dot_exact_rhs(o * o, norm_mat, 2) + EPS) * ng
        o_ref[0, pl.ds(r0, grp), :] = o * (g * jax.nn.sigmoid(g))
        return carry

    lax.fori_loop(0, n_groups, body, 0)


def gla_mixer(z_a, wa2, ba, norm_g):
    bsz, t, _ = z_a.shape
    hk = GLA_HEADS * GLA_DK
    wa2p = jnp.zeros((128, hk), F32).at[:GLA_RANK].set(wa2)
    return pl.pallas_call(
        functools.partial(_gla_kernel, n_groups=t // (GLA_GROUP * GLA_CHUNK)),
        grid=(bsz,),
        in_specs=[pl.BlockSpec((1, t, A_WIDTH), lambda b: (b, 0, 0)),
                  pl.BlockSpec((128, hk), lambda b: (0, 0)),
                  pl.BlockSpec((1, hk), lambda b: (0, 0)),
                  pl.BlockSpec((1, GROUP_WIDTH), lambda b: (0, 0))],
        out_specs=pl.BlockSpec((1, t, GROUP_WIDTH), lambda b: (b, 0, 0)),
        out_shape=jax.ShapeDtypeStruct((bsz, t, GROUP_WIDTH), F32),
        scratch_shapes=[pltpu.VMEM((hk, GROUP_WIDTH), F32)],
        compiler_params=_cparams(("parallel",)),
        name="gla_mixer",
    )(z_a, wa2p, ba.reshape(1, hk), norm_g.reshape(1, GROUP_WIDTH))


def _pool_kernel(u_ref, w_ref, b_ref, sc_ref, o_ref):
    u = u_ref[0]
    t, gw = u.shape
    row = lax.broadcasted_iota(jnp.int32, (t, gw), 0)
    grp = lax.broadcasted_iota(jnp.int32, (t, gw), 1) // POOL_CG

    def shifted(x, k):
        return jnp.where(row >= k, pltpu.roll(x, k, axis=0), 0.0)

    s = u
    p = jnp.zeros_like(u)
    for gi, win in enumerate(POOL_WINDOWS):
        half = win // 2
        s = s + shifted(s, half)
        cnt = jnp.minimum(row + 1, win).astype(F32)
        p = jnp.where(grp == gi, s / cnt - u, p)
    y = _dot(p.astype(BF16), w_ref[...]) + b_ref[...]
    o_ref[0] = y * sc_ref[...]


def pool_mixer(z_b, w, b, scale):
    assert POOL_WINDOWS == (2, 4, 8, 16)
    bsz, t, gw = z_b.shape
    w_bd = jnp.zeros((gw, gw), F32)
    for gi in range(len(POOL_WINDOWS)):
        w_bd = w_bd.at[gi * POOL_CG:(gi + 1) * POOL_CG, gi * POOL_CG:(gi + 1) * POOL_CG].set(w[gi])
    return pl.pallas_call(
        _pool_kernel,
        grid=(bsz,),
        in_specs=[pl.BlockSpec((1, t, gw), lambda i: (i, 0, 0)),
                  pl.BlockSpec((gw, gw), lambda i: (0, 0)),
                  pl.BlockSpec((1, gw), lambda i: (0, 0)),
                  pl.BlockSpec((1, gw), lambda i: (0, 0))],
        out_specs=pl.BlockSpec((1, t, gw), lambda i: (i, 0, 0)),
        out_shape=jax.ShapeDtypeStruct((bsz, t, gw), F32),
        compiler_params=_cparams(("parallel",)),
        name="pool_mixer",
    )(z_b, w_bd.astype(BF16), b.reshape(1, gw), scale.reshape(1, gw))


def _sgu_kernel(z_ref, lg_ref, lb_ref, w_ref, bm_ref, o_ref, *, chunks):
    c = SGU_CHUNK
    gw = GROUP_WIDTH
    rows = SGU_GROUPS * c
    tri = (lax.broadcasted_iota(jnp.int32, (rows, c), 1)
           <= lax.broadcasted_iota(jnp.int32, (rows, c), 0) % c)
    ws = jnp.where(tri, w_ref[...], 0.0).astype(BF16)
    grp = lax.broadcasted_iota(jnp.int32, (1, gw), 1) // SGU_CG
    for ci in range(chunks):
        z = jax.nn.gelu(z_ref[0, ci * c:(ci + 1) * c, :], approximate=True)
        u, v = z[:, :gw], z[:, gw:]
        mu = jnp.mean(v, axis=-1, keepdims=True)
        var = jnp.mean(jnp.square(v - mu), axis=-1, keepdims=True)
        vn = (v - mu) * lax.rsqrt(var + EPS) * lg_ref[...] + lb_ref[...]
        r = _dot(ws, vn.astype(BF16))
        mixed = bm_ref[...]
        for g in range(SGU_GROUPS):
            mixed = mixed + jnp.where(grp == g, r[g * c:(g + 1) * c, :], 0.0)
        o_ref[0, ci * c:(ci + 1) * c, :] = u * mixed


def sgu_mixer(z_d, ln_g, ln_b, w_s, b_s, chunks=4):
    bsz, t, _ = z_d.shape
    gw = GROUP_WIDTH
    tt = chunks * SGU_CHUNK
    bias = jnp.repeat(b_s.T, SGU_CG, axis=1)
    return pl.pallas_call(
        functools.partial(_sgu_kernel, chunks=chunks),
        grid=(bsz, t // tt),
        in_specs=[pl.BlockSpec((1, tt, 2 * gw), lambda b, i: (b, i, 0)),
                  pl.BlockSpec((1, gw), lambda b, i: (0, 0)),
                  pl.BlockSpec((1, gw), lambda b, i: (0, 0)),
                  pl.BlockSpec((SGU_GROUPS * SGU_CHUNK, SGU_CHUNK), lambda b, i: (0, 0)),
                  pl.BlockSpec((SGU_CHUNK, gw), lambda b, i: (0, 0))],
        out_specs=pl.BlockSpec((1, tt, gw), lambda b, i: (b, i, 0)),
        out_shape=jax.ShapeDtypeStruct((bsz, t, gw), F32),
        compiler_params=_cparams(("parallel", "parallel")),
        name="sgu_mixer",
    )(z_d, ln_g.reshape(1, gw), ln_b.reshape(1, gw),
      w_s.reshape(SGU_GROUPS * SGU_CHUNK, SGU_CHUNK), bias)


def _bucket_table():
    assert REL_MAX_DIST <= DSA_BLOCK + 1
    s = np.arange(DSA_BLOCK)[:, None]
    t = np.arange(DSA_BLOCK)[None, :]
    dist = np.stack([t - s, DSA_BLOCK + t - s, 2 * DSA_BLOCK + t - s])
    n = np.maximum(dist, 0)
    max_exact = REL_BUCKETS // 2
    nf = np.maximum(n, 1).astype(np.float32)
    large = max_exact + (np.log(nf / np.float32(max_exact)) / np.float32(math.log(REL_MAX_DIST / max_exact))
                         * np.float32(REL_BUCKETS - max_exact)).astype(np.int32)
    return np.where(n < max_exact, n, np.minimum(large, REL_BUCKETS - 1)).astype(np.int32)


def _bias_table_kernel(rb_ref, bucket_ref, o_ref):
    for back in range(3):
        bucket = bucket_ref[back]
        for h in range(DSA_HEADS):
            acc = jnp.zeros(bucket.shape, F32)
            for b in range(REL_BUCKETS):
                acc = jnp.where(bucket == b, rb_ref[b * DSA_HEADS + h], acc)
            o_ref[back, h] = acc


def rel_bias_tables(rel_bias):
    blk = DSA_BLOCK
    return pl.pallas_call(
        _bias_table_kernel,
        in_specs=[pl.BlockSpec(memory_space=pltpu.SMEM),
                  pl.BlockSpec((3, blk, blk), lambda: (0, 0, 0))],
        out_specs=pl.BlockSpec((3, DSA_HEADS, blk, blk), lambda: (0, 0, 0, 0)),
        out_shape=jax.ShapeDtypeStruct((3, DSA_HEADS, blk, blk), F32),
        name="rel_bias_tables",
    )(rel_bias.reshape(-1), jnp.asarray(_bucket_table()))


def _dsa_kernel(q_ref, k_ref, v_ref, qi_ref, kw_ref, qw_ref, toe_ref, o_ref,
                kpl_ref, vt_ref, keys_ref, khi_ref, klo_ref, acc_ref, am_ref, lg_ref, p_ref,
                *, topk, n_blocks, idx_bits):
    blk = DSA_BLOCK
    sub = DSA_SUB
    i = pl.program_id(1)
    hd = DSA_HEADS * DSA_DH
    heads_per_half = 128 // IDX_DIM

    @pl.when(i == 0)
    def _():
        lane = lax.broadcasted_iota(jnp.int32, (blk, 128), 1)

        def build(kb, c):
            r0 = pl.multiple_of(kb * blk, blk)
            ki = jnp.where(lane < IDX_DIM, kw_ref[0, pl.ds(r0, blk), :], 0.0)
            for j in range(heads_per_half):
                kpl_ref[kb, j] = (ki if j == 0 else pltpu.roll(ki, j * IDX_DIM, axis=1)).astype(BF16)
            vt_ref[kb] = jnp.transpose(v_ref[0, pl.ds(r0, blk), :].astype(F32)).astype(BF16)
            return c

        lax.fori_loop(0, n_blocks, build, 0)

    s_loc = lax.broadcasted_iota(jnp.int32, (blk, blk), 0)
    t_loc = lax.broadcasted_iota(jnp.int32, (blk, blk), 1)
    n_vis = i + 1

    qi_t = jnp.transpose(qi_ref[0].astype(F32)).astype(BF16)
    qi_halves = [qi_t[:128, :], qi_t[128:, :]]
    w_t = jnp.transpose(qw_ref[0])
    w_rows = [w_t[IDX_DIM + h:IDX_DIM + h + 1, :] * (IDX_HEADS ** -0.5) * (IDX_DIM ** -0.5)
              for h in range(IDX_HEADS)]
    s_sub = lax.broadcasted_iota(jnp.int32, (sub, blk), 0)
    t_sub = lax.broadcasted_iota(jnp.int32, (sub, blk), 1)

    def score_body(kb, c):
        for ci in range(blk // sub):
            rows = slice(ci * sub, (ci + 1) * sub)
            sc = jnp.zeros((sub, blk), F32)
            for half in range(2):
                for j in range(heads_per_half):
                    d = _dot(kpl_ref[kb, j, rows, :], qi_halves[half])
                    sc = sc + jnp.maximum(d, 0.0) * w_rows[half * heads_per_half + j]
            sc = jnp.where(sc == 0.0, 0.0, sc)
            bits = pltpu.bitcast(sc, jnp.int32)
            key = jnp.where(bits < 0, bits ^ jnp.int32(0x7FFFFFFF), bits)
            vis = (kb < i) | (s_sub + ci * sub <= t_sub)
            key = jnp.where(vis, key, jnp.int32(INT_MIN))
            keys_ref[kb, rows, :] = key
            khi_ref[kb, rows, :] = lax.shift_right_arithmetic(key, 16).astype(jnp.int16)
            klo_ref[kb, rows, :] = ((key & 0xFFFF) - 2 ** 15).astype(jnp.int16)
        return c

    lax.fori_loop(0, n_vis, score_body, 0)

    def count(pred):
        def body(kb, acc):
            hit = jnp.where(pred(keys_ref[kb], kb), 1.0, 0.0)
            return acc + jnp.sum(hit.reshape(blk // 32, 32, blk), axis=0)
        acc = lax.fori_loop(0, n_vis, body, jnp.zeros((32, blk), F32))
        return jnp.sum(acc, axis=0, keepdims=True)

    def count16(ref, cand):
        cand = cand.astype(jnp.int16)

        def body(kb, acc):
            hit = jnp.where(ref[kb] >= cand, jnp.int16(1), jnp.int16(0))
            for j in range(blk // 32):
                acc = acc + hit[j * 32:(j + 1) * 32]
            return acc
        acc = lax.fori_loop(0, n_vis, body, jnp.zeros((32, blk), jnp.int16))
        return jnp.sum(acc.astype(jnp.int32).astype(F32), axis=0, keepdims=True)

    def search16(ref, offset):
        lowest = jnp.full((1, blk), -(2 ** 15), jnp.int32)
        base = jnp.where(offset + count16(ref, jnp.zeros((1, blk), jnp.int32)) >= kf, 0, lowest)

        def bit_body(it, base):
            cand = base | lax.shift_left(jnp.int32(1), 14 - it)
            return jnp.where(offset + count16(ref, cand) >= kf, cand, base)
        return lax.fori_loop(0, 15, bit_body, base)

    kf = float(topk)
    zero = jnp.zeros((1, blk), jnp.int32)
    thr_hi = search16(khi_ref, 0.0)
    thr_hi16 = thr_hi.astype(jnp.int16)

    def low_body(kb, above):
        hi = khi_ref[kb]
        klo_ref[kb] = jnp.where(hi == thr_hi16, klo_ref[kb], jnp.int16(-(2 ** 15)))
        hit = jnp.where(hi > thr_hi16, jnp.int16(1), jnp.int16(0))
        for j in range(blk // 32):
            above = above + hit[j * 32:(j + 1) * 32]
        return above

    above = lax.fori_loop(0, n_vis, low_body, jnp.zeros((32, blk), jnp.int16))
    n_above = jnp.sum(above.astype(jnp.int32).astype(F32), axis=0, keepdims=True)
    thr_lo = search16(klo_ref, n_above)
    thr = lax.shift_left(thr_hi, 16) | (thr_lo + 2 ** 15)
    thr_sel = jnp.maximum(thr, jnp.int32(INT_MIN + 1))

    n_gt = count(lambda key, kb: key > thr)
    n_eq = count(lambda key, kb: key == thr)
    need = kf - n_gt
    excess = jnp.where((n_eq > need) & (thr > jnp.int32(INT_MIN)), 1.0, 0.0)

    def tie_search():
        def tie_body(it, j):
            cand = j | lax.shift_left(jnp.int32(1), idx_bits - 1 - it)
            below = count(lambda key, kb: (key == thr) & (kb * blk + s_loc < cand))
            return jnp.where(below < need, cand, j)
        return lax.fori_loop(0, idx_bits, tie_body, zero)

    last = lax.cond(jnp.max(excess) > 0.0, tie_search, lambda: jnp.full((1, blk), 2 ** 30, jnp.int32))

    assert DSA_DH ** -0.5 == 0.125
    q_t = jnp.transpose(q_ref[0].astype(F32) * (DSA_DH ** -0.5))
    row_h = lax.broadcasted_iota(jnp.int32, (hd, 1), 0) // DSA_DH
    q_heads = [jnp.where(row_h == h, q_t, 0.0).astype(BF16) for h in range(DSA_HEADS)]
    acc_ref[...] = jnp.zeros_like(acc_ref)
    n_sub = blk // sub

    def att_body(kb, carry):
        ms, ls = carry
        back = jnp.minimum(i - kb, 2)
        for ci in range(n_sub):
            rows = slice(ci * sub, (ci + 1) * sub)
            key = keys_ref[kb, rows, :]
            sel = (key > thr_sel) | ((key == thr_sel) & (kb * blk + ci * sub + s_sub <= last))
            am_ref[rows, :] = jnp.where(sel, 0.0, NEG_BIG)
        new_ms, new_ls = list(ms), list(ls)
        for heads in DSA_HEAD_GROUPS:
            alphas = {}
            for h in heads:
                pm = jnp.full((8, blk), NEG_BIG, F32)
                for ci in range(n_sub):
                    rows = slice(ci * sub, (ci + 1) * sub)
                    k_rows = k_ref[0, pl.ds(pl.multiple_of(kb * blk + ci * sub, sub), sub), :]
                    lg = _dot(k_rows, q_heads[h]) + toe_ref[back, h, rows, :] + am_ref[rows, :]
                    lg_ref[h, rows, :] = lg
                    pm = jnp.maximum(pm, jnp.max(lg.reshape(sub // 8, 8, blk), axis=0))
                new_ms[h] = jnp.maximum(ms[h], jnp.max(pm, axis=0, keepdims=True))
                alphas[h] = jnp.exp(ms[h] - new_ms[h])
            for h in heads:
                ps = jnp.zeros((8, blk), F32)
                for ci in range(n_sub):
                    rows = slice(ci * sub, (ci + 1) * sub)
                    p = jnp.exp(lg_ref[h, rows, :] - new_ms[h])
                    ps = ps + jnp.sum(p.reshape(sub // 8, 8, blk), axis=0)
                    p_ref[h, rows, :] = p.astype(BF16)
                new_ls[h] = ls[h] * alphas[h] + jnp.sum(ps, axis=0, keepdims=True)
            for h in heads:
                hrows = slice(h * DSA_DH, (h + 1) * DSA_DH)
                acc_ref[hrows, :] = acc_ref[hrows, :] * alphas[h] + _dot(vt_ref[kb, hrows, :], p_ref[h])
        return tuple(new_ms), tuple(new_ls)

    init = (tuple(jnp.full((1, blk), 0.01 * NEG_BIG, F32) for _ in range(DSA_HEADS)),
            tuple(jnp.zeros((1, blk), F32) for _ in range(DSA_HEADS)))
    _, ls = lax.fori_loop(0, n_vis, att_body, init)
    for h in range(DSA_HEADS):
        rows = slice(h * DSA_DH, (h + 1) * DSA_DH)
        acc_ref[rows, :] = acc_ref[rows, :] * (1.0 / ls[h])
    o_ref[0] = jnp.transpose(acc_ref[...])


def dsa_mixer(z_c, z_aux, toe):
    bsz, t, _ = z_c.shape
    blk = DSA_BLOCK
    n_blocks = t // blk
    assert t % blk == 0
    topk = min(DSA_TOPK_MAX, t // 4)
    hd = DSA_HEADS * DSA_DH
    kernel = functools.partial(_dsa_kernel, topk=topk, n_blocks=n_blocks,
                               idx_bits=max(1, (t - 1).bit_length()))
    return pl.pallas_call(
        kernel,
        grid=(bsz, n_blocks),
        in_specs=[pl.BlockSpec((1, blk, hd), lambda b, i: (b, i, 0)),
                  pl.BlockSpec((1, t, hd), lambda b, i: (b, 0, 1)),
                  pl.BlockSpec((1, t, hd), lambda b, i: (b, 0, 2)),
                  pl.BlockSpec((1, blk, hd), lambda b, i: (b, i, 3)),
                  pl.BlockSpec((1, t, C_AUX), lambda b, i: (b, 0, 0)),
                  pl.BlockSpec((1, blk, C_AUX), lambda b, i: (b, i, 0)),
                  pl.BlockSpec((3, DSA_HEADS, blk, blk), lambda b, i: (0, 0, 0, 0))],
        out_specs=pl.BlockSpec((1, blk, hd), lambda b, i: (b, i, 0)),
        out_shape=jax.ShapeDtypeStruct((bsz, t, hd), F32),
        scratch_shapes=[pltpu.VMEM((n_blocks, 128 // IDX_DIM, blk, 128), BF16),
                        pltpu.VMEM((n_blocks, hd, blk), BF16),
                        pltpu.VMEM((n_blocks, blk, blk), jnp.int32),
                        pltpu.VMEM((n_blocks, blk, blk), jnp.int16),
                        pltpu.VMEM((n_blocks, blk, blk), jnp.int16),
                        pltpu.VMEM((hd, blk), F32),
                        pltpu.VMEM((blk, blk), F32),
                        pltpu.VMEM((DSA_HEADS, blk, blk), F32),
                        pltpu.VMEM((DSA_HEADS, blk, blk), BF16)],
        compiler_params=_cparams(("parallel", "arbitrary")),
        name="dsa_mixer",
    )(z_c, z_c, z_c, z_c, z_aux, z_aux, toe)


def _mixer_tail_kernel(h_ref, a_ref, b_ref, c_ref, d_ref, wout_ref, gxa_ref, wq_ref, k_ref, v_ref, wo_ref,
                       *refs, route):
    gw = GROUP_WIDTH
    x = h_ref[0]
    for gi, r in enumerate((a_ref, b_ref, c_ref, d_ref)):
        x = x + _dot(r[0].astype(BF16), wout_ref[gi * gw:(gi + 1) * gw, :])

    hd = XA_HEADS * XA_DH
    q = _dot(_rms(x, gxa_ref[...]).astype(BF16), wq_ref[...]).astype(BF16)
    k = k_ref[0]
    v = v_ref[0]
    lane_h = lax.broadcasted_iota(jnp.int32, (1, hd), 1) // XA_DH
    o = jnp.zeros((x.shape[0], hd), F32)
    for h in range(XA_HEADS):
        s = _dot_nt(q, jnp.where(lane_h == h, k, 0.0).astype(BF16)) * (XA_DH ** -0.5)
        p = jnp.exp(s - jnp.max(s, axis=-1, keepdims=True))
        p = p / jnp.sum(p, axis=-1, keepdims=True)
        o = o + _dot(p.astype(BF16), jnp.where(lane_h == h, v, 0.0).astype(BF16))
    x = x + _dot(o.astype(BF16), wo_ref[...])

    if not route:
        o_ref, = refs
        o_ref[0] = x
        return
    gffn_ref, wr_ref, o_ref, hn_ref, gate_ref, cnt_ref = refs
    o_ref[0] = x
    hn = _rms(x, gffn_ref[...])
    hn_ref[0] = hn.astype(BF16)
    tm = hn.shape[0]
    lane = lax.broadcasted_iota(jnp.int32, (tm, GATE_LANES), 1)
    logits = jnp.where(lane < N_EXPERTS, _dot_3pass(hn, wr_ref[...]), -jnp.inf)
    m1 = jnp.max(logits, axis=-1, keepdims=True)
    i1 = jnp.min(jnp.where(logits == m1, lane, GATE_LANES), axis=-1, keepdims=True)
    rest = jnp.where(lane == i1, -jnp.inf, logits)
    m2 = jnp.max(rest, axis=-1, keepdims=True)
    i2 = jnp.min(jnp.where(rest == m2, lane, GATE_LANES), axis=-1, keepdims=True)
    e2 = jnp.exp(m2 - m1)
    g1 = 1.0 / (1.0 + e2)
    gates = jnp.where(lane == i1, g1, 0.0) + jnp.where(lane == i2, e2 * g1, 0.0)
    gates = jnp.where(lane == N_EXPERTS, i1.astype(F32), gates)
    gate_ref[0] = jnp.where(lane == N_EXPERTS + 1, i2.astype(F32), gates)
    sel = jnp.where((lane == i1) | (lane == i2), 1.0, 0.0)
    for c in range(tm // MOE_CHUNK):
        cnt_ref[c] = jnp.sum(sel[c * MOE_CHUNK:(c + 1) * MOE_CHUNK, :], axis=0, keepdims=True)


def mixer_tail(h3, mixers, w_out, g_xa, wq, k, v, wo, routing=None, tm=1024):
    bsz, t, d = h3.shape
    tm = min(tm, t)
    mlen = k.shape[1]
    hd = XA_HEADS * XA_DH
    gw = GROUP_WIDTH
    row = lambda b, i: (b, i, 0)
    fixed2 = lambda b, i: (0, 0)
    in_specs = ([pl.BlockSpec((1, tm, d), row)] + [pl.BlockSpec((1, tm, gw), row)] * 4
                + [pl.BlockSpec((4 * gw, d), fixed2), pl.BlockSpec((1, d), fixed2), pl.BlockSpec((d, hd), fixed2),
                   pl.BlockSpec((1, mlen, hd), lambda b, i: (b, 0, 0)),
                   pl.BlockSpec((1, mlen, hd), lambda b, i: (b, 0, 0)),
                   pl.BlockSpec((hd, d), fixed2)])
    args = [h3, *mixers, w_out, g_xa.reshape(1, d), wq, k, v, wo]
    out_specs = [pl.BlockSpec((1, tm, d), row)]
    out_shape = [jax.ShapeDtypeStruct((bsz, t, d), F32)]
    if routing is not None:
        assert tm % MOE_CHUNK == 0
        g_ffn, wr = routing
        in_specs += [pl.BlockSpec((1, d), fixed2), pl.BlockSpec((d, GATE_LANES), fixed2)]
        args += [g_ffn.reshape(1, d), jnp.zeros((d, GATE_LANES), F32).at[:, :N_EXPERTS].set(wr)]
        per_step = tm // MOE_CHUNK
        out_specs += [pl.BlockSpec((1, tm, d), row), pl.BlockSpec((1, tm, GATE_LANES), row),
                      pl.BlockSpec((per_step, 1, GATE_LANES), lambda b, i: (b * (t // tm) + i, 0, 0))]
        out_shape += [jax.ShapeDtypeStruct((bsz, t, d), BF16), jax.ShapeDtypeStruct((bsz, t, GATE_LANES), F32),
                      jax.ShapeDtypeStruct((bsz * t // MOE_CHUNK, 1, GATE_LANES), F32)]
    return pl.pallas_call(
        functools.partial(_mixer_tail_kernel, route=routing is not None),
        grid=(bsz, t // tm),
        in_specs=in_specs,
        out_specs=out_specs,
        out_shape=out_shape,
        compiler_params=_cparams(("parallel", "parallel")),
        name="mixer_tail",
    )(*args)


def _ffn_kernel(h_ref, g_ref, w1_ref, w3_ref, w2_ref, o_ref, *, tf):
    x = h_ref[...]
    hn = _rms(x, g_ref[...]).astype(BF16)
    acc = x
    for f0 in range(0, w1_ref.shape[1], tf):
        a = _dot(hn, w1_ref[:, f0:f0 + tf])
        b = _dot(hn, w3_ref[:, f0:f0 + tf])
        acc = acc + _dot((a * jax.nn.sigmoid(a) * b).astype(BF16), w2_ref[f0:f0 + tf, :])
    o_ref[...] = acc


def ffn(h, g, w1, w3, w2, tm=512, tf=704):
    m, d = h.shape
    tm = min(tm, m)
    nf = w1.shape[1]
    assert nf % tf == 0
    resident = lambda shape: pl.BlockSpec(shape, lambda i: (0, 0), pipeline_mode=pl.Buffered(1))
    return pl.pallas_call(
        functools.partial(_ffn_kernel, tf=tf),
        grid=(m // tm,),
        in_specs=[pl.BlockSpec((tm, d), lambda i: (i, 0)),
                  pl.BlockSpec((1, d), lambda i: (0, 0)),
                  resident((d, nf)), resident((d, nf)), resident((nf, d))],
        out_specs=pl.BlockSpec((tm, d), lambda i: (i, 0)),
        out_shape=jax.ShapeDtypeStruct((m, d), F32),
        compiler_params=_cparams(("parallel",)),
        name="ffn",
    )(h, g.reshape(1, d), w1, w3, w2)


GATE_LANES = 128
MOE_CHUNK = 256
MOE_ROWS = 128
MOE_TOKENS = 2048
MOE_ALIGN = 16


def _moe_kernel(cb_ref, h_ref, hn_ref, gate_ref, w1_ref, w3_ref, w2_ref, *refs, final):
    if final:
        gfin_ref, *refs = refs
    y_ref, rank_row, rk_ref, gs_ref, xs_ref, yacc_ref, rc_ref, gc_ref = refs
    t = pl.program_id(0)
    e = pl.program_id(1)
    f = pl.program_id(2)
    ts, d = hn_ref.shape
    ch, rb = MOE_CHUNK, MOE_ROWS
    win = 2 * rb
    n_ch = ts // ch
    lane = lax.broadcasted_iota(jnp.int32, (1, GATE_LANES), 1)

    def before(c):
        return cb_ref[(t * (n_ch + 1) + c) * N_EXPERTS + e]

    n_blocks = (before(n_ch) + rb - 1) // rb
    n_windows = (n_blocks * rb + win - 1) // win + 1

    def windows(c):
        lo, hi = before(c), before(c + 1)
        s0 = (lo // MOE_ALIGN) * MOE_ALIGN
        return s0, jnp.where(hi > lo, (hi - s0 + win - 1) // win, 0)

    spans = [windows(c) for c in range(n_ch)]
    single = functools.reduce(jnp.logical_and, [n_win <= 1 for _, n_win in spans])

    def for_each_window(visit):
        @pl.when(single)
        def _():
            for c, (s0, _) in enumerate(spans):
                visit(c, s0)

        @pl.when(jnp.logical_not(single))
        def _():
            for c, (s0, n_win) in enumerate(spans):
                def body(j, carry, c=c, s0=s0):
                    visit(c, s0 + j * win)
                    return carry
                lax.fori_loop(0, n_win, body, 0)

    @pl.when((e == 0) & (f == 0))
    def _():
        y_ref[...] = h_ref[...]
        strict_lower = (lax.broadcasted_iota(jnp.int32, (ch, ch), 1)
                        < lax.broadcasted_iota(jnp.int32, (ch, ch), 0)).astype(BF16)
        lane_f = lane.astype(F32)
        offs = jnp.ones((1, GATE_LANES), F32)
        for c in range(n_ch):
            rows = slice(c * ch, (c + 1) * ch)
            g = gate_ref[rows, :]
            sel = jnp.where((lane_f == g[:, N_EXPERTS:N_EXPERTS + 1])
                            | (lane_f == g[:, N_EXPERTS + 1:N_EXPERTS + 2]), 1.0, 0.0)
            r = _dot(strict_lower, sel.astype(BF16)) + offs
            r = jnp.where(sel > 0.0, r, 0.0)
            rank_row[:, rows] = jnp.transpose(r) - 1.0
            high = jnp.floor(r * (1.0 / 256.0))
            rk_ref[0, rows, :] = high.astype(BF16)
            rk_ref[1, rows, :] = (r - 256.0 * high).astype(BF16)
            for j, part in enumerate(_split_bf16(g, 3)):
                gs_ref[j, rows, :] = part
            offs = offs + jnp.sum(sel, axis=0, keepdims=True)

    @pl.when(f == 0)
    def _():
        pick = (lax.broadcasted_iota(jnp.int32, (GATE_LANES, GATE_LANES), 0) == e).astype(BF16)
        for c in range(n_ch):
            rows = slice(c * ch, (c + 1) * ch)
            rc_ref[rows, :] = 256.0 * _dot(rk_ref[0, rows, :], pick) + _dot(rk_ref[1, rows, :], pick) - 1.0
            gc_ref[rows, :] = (_dot(gs_ref[0, rows, :], pick) + _dot(gs_ref[1, rows, :], pick)
                               + _dot(gs_ref[2, rows, :], pick))

        def clear(j, carry):
            r0 = pl.multiple_of(j * win, win)
            xs_ref[pl.ds(r0, win), :] = jnp.zeros((win, d), BF16)
            yacc_ref[pl.ds(r0, win), :] = jnp.zeros((win, d), F32)
            return carry

        lax.fori_loop(0, n_windows, clear, 0)

        row_id = lax.broadcasted_iota(jnp.int32, (win, ch), 0)

        def gather_window(c, s):
            s = pl.multiple_of(s, MOE_ALIGN)
            ranks = rank_row[pl.ds(e, 1), c * ch:(c + 1) * ch]
            onehot = jnp.where(ranks == (s + row_id).astype(F32), 1.0, 0.0).astype(BF16)
            xs_ref[pl.ds(s, win), :] += _dot(onehot, hn_ref[c * ch:(c + 1) * ch, :]).astype(BF16)

        for_each_window(gather_window)

    def ffn_rows(r0, rows):
        x = xs_ref[pl.ds(r0, rows), :]
        a = _dot(x, w1_ref[0])
        g3 = _dot(x, w3_ref[0])
        yacc_ref[pl.ds(r0, rows), :] += _dot((a * jax.nn.sigmoid(a) * g3).astype(BF16), w2_ref[0])

    n_quads = n_blocks // 4

    def ffn_quad(j, carry):
        ffn_rows(pl.multiple_of(j * (4 * rb), 4 * rb), 4 * rb)
        return carry

    lax.fori_loop(0, n_quads, ffn_quad, 0)
    tail = pl.multiple_of(n_quads * (4 * rb), 4 * rb)

    @pl.when((n_blocks & 2) != 0)
    def _():
        ffn_rows(tail, 2 * rb)

    @pl.when((n_blocks & 1) != 0)
    def _():
        ffn_rows(pl.multiple_of(tail + (n_blocks & 2) * rb, rb), rb)

    @pl.when(f == pl.num_programs(2) - 1)
    def _():
        def to_bf16(j, carry):
            r0 = pl.multiple_of(j * win, win)
            xs_ref[pl.ds(r0, win), :] = yacc_ref[pl.ds(r0, win), :].astype(BF16)
            return carry

        lax.fori_loop(0, n_windows, to_bf16, 0)

        lane_id = lax.broadcasted_iota(jnp.int32, (ch, rb), 1).astype(F32)

        def scatter_window(c, s):
            rows = slice(c * ch, (c + 1) * ch)
            s = pl.multiple_of(s, MOE_ALIGN)
            rank = rc_ref[rows, :] - s.astype(F32)
            onehot = jnp.concatenate([jnp.where(rank == lane_id, 1.0, 0.0),
                                      jnp.where(rank == lane_id + float(rb), 1.0, 0.0)],
                                     axis=1).astype(BF16)
            gate = jnp.concatenate([gc_ref[rows, :]] * (d // GATE_LANES), axis=1)
            y_ref[rows, :] += gate * _dot(onehot, xs_ref[pl.ds(s, win), :])

        for_each_window(scatter_window)

        if final:
            @pl.when(e == pl.num_programs(1) - 1)
            def _():
                y_ref[...] = _rms(y_ref[...], gfin_ref[...])


def moe(h, hn, gates, cnt, w1, w3, w2, final_gain=None, tf=896):
    m, d = hn.shape
    ts = min(MOE_TOKENS, m)
    n_exp, _, nf = w1.shape
    assert nf % tf == 0 and m % ts == 0 and ts % MOE_CHUNK == 0 and MOE_ROWS == GATE_LANES
    n_ch = ts // MOE_CHUNK
    counts = cnt.reshape(m // ts, n_ch, GATE_LANES)[:, :, :n_exp].astype(jnp.int32)
    before = jnp.concatenate([jnp.zeros((m // ts, 1, n_exp), jnp.int32), jnp.cumsum(counts, axis=1)], axis=1)
    per_tile = lambda width: pl.BlockSpec((ts, width), lambda t, e, f, cb: (t, 0), pipeline_mode=pl.Buffered(1))
    in_specs = [per_tile(d), per_tile(d), per_tile(GATE_LANES),
                pl.BlockSpec((1, d, tf), lambda t, e, f, cb: (e, 0, f)),
                pl.BlockSpec((1, d, tf), lambda t, e, f, cb: (e, 0, f)),
                pl.BlockSpec((1, tf, d), lambda t, e, f, cb: (e, f, 0))]
    args = [before.reshape(-1), h, hn, gates, w1, w3, w2]
    if final_gain is not None:
        in_specs.append(pl.BlockSpec((1, d), lambda t, e, f, cb: (0, 0)))
        args.append(final_gain.reshape(1, d))
    grid_spec = pltpu.PrefetchScalarGridSpec(
        num_scalar_prefetch=1,
        grid=(m // ts, n_exp, nf // tf),
        in_specs=in_specs,
        out_specs=per_tile(d),
        scratch_shapes=[pltpu.VMEM((GATE_LANES, ts), F32),
                        pltpu.VMEM((2, ts, GATE_LANES), BF16),
                        pltpu.VMEM((3, ts, GATE_LANES), BF16),
                        pltpu.VMEM((ts + 2 * MOE_ROWS, d), BF16),
                        pltpu.VMEM((ts + 2 * MOE_ROWS, d), F32),
                        pltpu.VMEM((ts, MOE_ROWS), F32),
                        pltpu.VMEM((ts, GATE_LANES), F32)])
    return pl.pallas_call(
        functools.partial(_moe_kernel, final=final_gain is not None),
        grid_spec=grid_spec,
        out_shape=jax.ShapeDtypeStruct((m, d), F32),
        compiler_params=_cparams(("parallel", "arbitrary", "arbitrary")),
        name="moe",
    )(*args)


def _final_norm_kernel(g_ref, x_ref, o_ref):
    o_ref[...] = _rms(x_ref[...], g_ref[...])


def final_rmsnorm(g, x, tm=1024):
    m, d = x.shape
    tm = min(tm, m)
    return pl.pallas_call(
        _final_norm_kernel,
        grid=(m // tm,),
        in_specs=[pl.BlockSpec((1, d), lambda i: (0, 0)), pl.BlockSpec((tm, d), lambda i: (i, 0))],
        out_specs=pl.BlockSpec((tm, d), lambda i: (i, 0)),
        out_shape=jax.ShapeDtypeStruct((m, d), F32),
        compiler_params=_cparams(("parallel",)),
        name="final_norm",
    )(g.reshape(1, d), x)


def _pad_cols(w, width):
    return jnp.pad(w, ((0, 0), (0, width - w.shape[1])))


def _in_proj_weight(w_in):
    a_end = 2 * GLA_HEADS * GLA_DK + 2 * GROUP_WIDTH + GLA_RANK
    b_end = a_end + GROUP_WIDTH
    c_end = b_end + 3 * GROUP_WIDTH + IDX_HEADS * IDX_DIM + IDX_DIM + IDX_HEADS
    assert w_in.shape[1] == c_end + 2 * GROUP_WIDTH
    return jnp.concatenate([_pad_cols(w_in[:, :a_end], A_WIDTH), w_in[:, a_end:b_end],
                            _pad_cols(w_in[:, b_end:c_end], C_MAIN + C_AUX), w_in[:, c_end:]], axis=1).astype(BF16)


def hybrid_layer(h, mem2, toe, p, bsz, t, routing=None):
    m, d = h.shape
    z_a, z_b, z_c, z_aux, z_d = norm_matmul(h, p["norm_mix"], _in_proj_weight(p["w_in"]),
                                            (A_WIDTH, B_WIDTH, C_MAIN, C_AUX, D_WIDTH),
                                            (F32, F32, BF16, F32, F32))
    o_a = gla_mixer(z_a.reshape(bsz, t, A_WIDTH), p["gla_wa2"], p["gla_ba"], p["gla_norm"])
    o_b = pool_mixer(z_b.reshape(bsz, t, B_WIDTH), p["pool_w"], p["pool_b"].reshape(-1), p["pool_scale"])
    o_c = dsa_mixer(z_c.reshape(bsz, t, C_MAIN), z_aux.reshape(bsz, t, C_AUX), toe)
    o_d = sgu_mixer(z_d.reshape(bsz, t, D_WIDTH), p["sgu_ln_g"], p["sgu_ln_b"], p["sgu_w"], p["sgu_b"])
    hd = XA_HEADS * XA_DH
    k, v = norm_matmul(mem2, p["norm_mem"], p["xa_wkv"].astype(BF16), (hd, hd))
    mlen = mem2.shape[0] // bsz
    outs = mixer_tail(h.reshape(bsz, t, d), (o_a, o_b, o_c, o_d), p["w_out"].astype(BF16), p["norm_xa"],
                      p["xa_wq"].astype(BF16), k.reshape(bsz, mlen, hd), v.reshape(bsz, mlen, hd),
                      p["xa_wo"].astype(BF16), routing)
    if routing is None:
        return (outs[0].reshape(m, d),)
    h3, hn, gates, counts = outs
    return h3.reshape(m, d), hn.reshape(m, d), gates.reshape(m, GATE_LANES), counts


def kernel(x, mem, rel_bias, final_norm, norm_mix, w_in, gla_wa2, gla_ba, gla_norm, pool_w, pool_b,
           pool_scale, sgu_ln_g, sgu_ln_b, sgu_w, sgu_b, w_out, norm_xa, norm_mem, xa_wq, xa_wkv, xa_wo,
           norm_ffn, ffn_w1, ffn_w3, ffn_w2, router, moe_w1, moe_w3, moe_w2):
    bsz, t, d = x.shape
    depth = norm_mix.shape[0]
    h = x.reshape(bsz * t, d)
    mem2 = mem.reshape(-1, d)
    toe = rel_bias_tables(rel_bias)
    normed = False
    for i in range(depth):
        p = dict(norm_mix=norm_mix[i], w_in=w_in[i], gla_wa2=gla_wa2[i], gla_ba=gla_ba[i],
                 gla_norm=gla_norm[i], pool_w=pool_w[i], pool_b=pool_b[i], pool_scale=pool_scale[i],
                 sgu_ln_g=sgu_ln_g[i], sgu_ln_b=sgu_ln_b[i], sgu_w=sgu_w[i], sgu_b=sgu_b[i],
                 w_out=w_out[i], norm_xa=norm_xa[i], norm_mem=norm_mem[i], xa_wq=xa_wq[i],
                 xa_wkv=xa_wkv[i], xa_wo=xa_wo[i])
        j = i // 2
        if i % 2 == 0:
            h, = hybrid_layer(h, mem2, toe, p, bsz, t)
            h = ffn(h, norm_ffn[i], ffn_w1[j].astype(BF16), ffn_w3[j].astype(BF16), ffn_w2[j].astype(BF16))
        else:
            h, hn, gates, counts = hybrid_layer(h, mem2, toe, p, bsz, t, routing=(norm_ffn[i], router[j]))
            normed = i == depth - 1
            h = moe(h, hn, gates, counts, moe_w1[j].astype(BF16), moe_w3[j].astype(BF16),
                    moe_w2[j].astype(BF16), final_gain=final_norm if normed else None)
    if not normed:
        h = final_rmsnorm(final_norm, h)
    return h.reshape(bsz, t, d)
```

```python
import functools
import math

import jax
import jax.numpy as jnp
import numpy as np
from jax import lax
from jax.experimental import pallas as pl
from jax.experimental.pallas import tpu as pltpu

F32 = jnp.float32
BF16 = jnp.bfloat16
EPS = 1e-6

GROUP_WIDTH = 256

GLA_HEADS = 4
GLA_DV = 64
GLA_DK = 32
GLA_RANK = 16
GLA_TAU = 16.0
GLA_CHUNK = 64
GLA_GROUP = 8

POOL_WINDOWS = (2, 4, 8, 16)
POOL_CG = 64

DSA_HEADS = 4
DSA_DH = 64
IDX_HEADS = 8
IDX_DIM = 32
DSA_TOPK_MAX = 256
DSA_BLOCK = 256
DSA_SUB = 64
DSA_HEAD_GROUPS = ((0, 1, 2, 3),)

SGU_GROUPS = 4
SGU_CHUNK = 128
SGU_CG = 64

REL_BUCKETS = 32
REL_MAX_DIST = 128

XA_HEADS = 4
XA_DH = 64

N_EXPERTS = 8

A_WIDTH = 896
B_WIDTH = 256
C_MAIN = 1024
C_AUX = 128
D_WIDTH = 512

INT_MIN = -(2 ** 31)
NEG_BIG = -1e30
VMEM_LIMIT = 56 * 1024 * 1024


def _cparams(sem):
    return pltpu.CompilerParams(dimension_semantics=sem, vmem_limit_bytes=VMEM_LIMIT)


def _dot(a, b):
    return jnp.dot(a, b, preferred_element_type=F32)


def _dot_nt(a, b):
    return lax.dot_general(a, b, (((1,), (1,)), ((), ())), preferred_element_type=F32)


def _dot_tn(a, b):
    return lax.dot_general(a, b, (((0,), (0,)), ((), ())), preferred_element_type=F32)


def _split_bf16(x, terms):
    parts = []
    for _ in range(terms):
        p = x.astype(BF16)
        parts.append(p)
        x = x - p.astype(F32)
    return parts


def _dot_exact_rhs(a, b, terms):
    out = None
    for p in _split_bf16(a, terms):
        d = _dot(p, b)
        out = d if out is None else out + d
    return out


def _dot_exact_lhs(a, b, terms):
    out = None
    for p in _split_bf16(b, terms):
        d = _dot(a, p)
        out = d if out is None else out + d
    return out


def _dot_3pass(a, b):
    a_hi, a_lo = _split_bf16(a, 2)
    b_hi, b_lo = _split_bf16(b, 2)
    return _dot(a_hi, b_hi) + (_dot(a_hi, b_lo) + _dot(a_lo, b_hi))


def _rms(x, g):
    return x * lax.rsqrt(jnp.mean(x * x, axis=-1, keepdims=True) + EPS) * g


def _norm_matmul_kernel(h_ref, g_ref, w_ref, *out_refs, widths):
    hb = _rms(h_ref[...], g_ref[...]).astype(BF16)
    off = 0
    for o_ref, wd in zip(out_refs, widths):
        o_ref[...] = _dot(hb, w_ref[:, off:off + wd]).astype(o_ref.dtype)
        off += wd


def norm_matmul(h, g, w, widths, dtypes=None, tm=512):
    dtypes = dtypes or (F32,) * len(widths)
    m, d = h.shape
    tm = min(tm, m)
    n = sum(widths)
    return pl.pallas_call(
        functools.partial(_norm_matmul_kernel, widths=widths),
        grid=(m // tm,),
        in_specs=[pl.BlockSpec((tm, d), lambda i: (i, 0)),
                  pl.BlockSpec((1, d), lambda i: (0, 0)),
                  pl.BlockSpec((d, n), lambda i: (0, 0))],
        out_specs=[pl.BlockSpec((tm, wd), lambda i: (i, 0)) for wd in widths],
        out_shape=[jax.ShapeDtypeStruct((m, wd), dt) for wd, dt in zip(widths, dtypes)],
        compiler_params=_cparams(("parallel",)),
        name="norm_matmul",
    )(h, g.reshape(1, d), w)


def _log_sigmoid(x):
    return jnp.minimum(x, 0.0) - jnp.log1p(jnp.exp(-jnp.abs(x)))


def _gla_kernel(z_ref, wa2_ref, ba_ref, ng_ref, o_ref, s_ref, *, n_groups):
    c = GLA_CHUNK
    grp = GLA_GROUP * c
    hk = GLA_HEADS * GLA_DK
    hv = GLA_HEADS * GLA_DV
    s_ref[...] = jnp.zeros_like(s_ref)

    head_k = lax.broadcasted_iota(jnp.int32, (1, hk), 1) // GLA_DK
    head_v = lax.broadcasted_iota(jnp.int32, (1, hv), 1) // GLA_DV
    cum = min(grp, 256)
    g_row = lax.broadcasted_iota(jnp.int32, (cum, cum), 0)
    g_col = lax.broadcasted_iota(jnp.int32, (cum, cum), 1)
    tril = (((g_row // c) == (g_col // c)) & (g_col <= g_row)).astype(BF16)
    causal4 = (lax.broadcasted_iota(jnp.int32, (GLA_HEADS * c, c), 1)
               <= lax.broadcasted_iota(jnp.int32, (GLA_HEADS * c, c), 0) % c)
    state_mask = (lax.broadcasted_iota(jnp.int32, (hk, hv), 0) // GLA_DK
                  == lax.broadcasted_iota(jnp.int32, (hk, hv), 1) // GLA_DV)
    norm_mat = jnp.where(lax.broadcasted_iota(jnp.int32, (hv, hv), 0) // GLA_DV
                         == lax.broadcasted_iota(jnp.int32, (hv, hv), 1) // GLA_DV,
                         1.0 / GLA_DV, 0.0).astype(BF16)
    wa2 = wa2_ref[...]
    ba = ba_ref[...]
    ng = ng_ref[...]

    def body(n, carry):
        r0 = pl.multiple_of(n * grp, grp)
        z = z_ref[0, pl.ds(r0, grp), :]
        q, k, v, g, lr = z[:, 0:128], z[:, 128:256], z[:, 256:512], z[:, 512:768], z[:, 768:896]
        log_a = _log_sigmoid(_dot_3pass(lr, wa2) + ba) / GLA_TAU
        b = jnp.concatenate([_dot_exact_lhs(tril, log_a[r:r + cum], 3) for r in range(0, grp, cum)], axis=0)
        b_end = jnp.concatenate([jnp.broadcast_to(b[(ci + 1) * c - 1:(ci + 1) * c, :], (c, hk))
                                 for ci in range(GLA_GROUP)], axis=0)
        q_t = q * (GLA_DK ** -0.5) * jnp.exp(b)
        q_tb = q_t.astype(BF16)
        k_t = (k * jnp.exp(-b)).astype(BF16)
        k_dec = (k * jnp.exp(b_end - b)).astype(BF16)
        vb = v.astype(BF16)
        outs = []
        for ci in range(GLA_GROUP):
            rows = slice(ci * c, (ci + 1) * c)
            q4 = jnp.concatenate([jnp.where(head_k == h, q_t[rows], 0.0) for h in range(GLA_HEADS)],
                                 axis=0).astype(BF16)
            att = jnp.where(causal4, _dot_nt(q4, k_t[rows]), 0.0)
            r = _dot(att.astype(BF16), vb[rows])
            o = _dot(q_tb[rows], s_ref[...].astype(BF16))
            for h in range(GLA_HEADS):
                o = o + jnp.where(head_v == h, r[h * c:(h + 1) * c, :], 0.0)
            outs.append(o)
            kv = jnp.where(state_mask, _dot_tn(k_dec[rows], vb[rows]), 0.0)
            last = b_end[ci * c:ci * c + 1, :]
            dec = jnp.exp(jnp.transpose(jnp.broadcast_to(last, (hk, hk))))
            s_ref[...] = s_ref[...] * jnp.concatenate([dec, dec], axis=1) + kv
        o = jnp.concatenate(outs, axis=0)
        o = o * lax.rsqrt(_dot_exact_rhs(o * o, norm_mat, 2) + EPS) * ng
        o_ref[0, pl.ds(r0, grp), :] = o * (g * jax.nn.sigmoid(g))
        return carry

    lax.fori_loop(0, n_groups, body, 0)


def gla_mixer(z_a, wa2, ba, norm_g):
    bsz, t, _ = z_a.shape
    hk = GLA_HEADS * GLA_DK
    wa2p = jnp.zeros((128, hk), F32).at[:GLA_RANK].set(wa2)
    return pl.pallas_call(
        functools.partial(_gla_kernel, n_groups=t // (GLA_GROUP * GLA_CHUNK)),
        grid=(bsz,),
        in_specs=[pl.BlockSpec((1, t, A_WIDTH), lambda b: (b, 0, 0)),
                  pl.BlockSpec((128, hk), lambda b: (0, 0)),
                  pl.BlockSpec((1, hk), lambda b: (0, 0)),
                  pl.BlockSpec((1, GROUP_WIDTH), lambda b: (0, 0))],
        out_specs=pl.BlockSpec((1, t, GROUP_WIDTH), lambda b: (b, 0, 0)),
        out_shape=jax.ShapeDtypeStruct((bsz, t, GROUP_WIDTH), F32),
        scratch_shapes=[pltpu.VMEM((hk, GROUP_WIDTH), F32)],
        compiler_params=_cparams(("parallel",)),
        name="gla_mixer",
    )(z_a, wa2p, ba.reshape(1, hk), norm_g.reshape(1, GROUP_WIDTH))


def _pool_kernel(u_ref, w_ref, b_ref, sc_ref, o_ref):
    u = u_ref[0]
    t, gw = u.shape
    row = lax.broadcasted_iota(jnp.int32, (t, gw), 0)
    grp = lax.broadcasted_iota(jnp.int32, (t, gw), 1) // POOL_CG

    def shifted(x, k):
        return jnp.where(row >= k, pltpu.roll(x, k, axis=0), 0.0)

    s = u
    p = jnp.zeros_like(u)
    for gi, win in enumerate(POOL_WINDOWS):
        half = win // 2
        s = s + shifted(s, half)
        cnt = jnp.minimum(row + 1, win).astype(F32)
        p = jnp.where(grp == gi, s / cnt - u, p)
    y = _dot(p.astype(BF16), w_ref[...]) + b_ref[...]
    o_ref[0] = y * sc_ref[...]


def pool_mixer(z_b, w, b, scale):
    assert POOL_WINDOWS == (2, 4, 8, 16)
    bsz, t, gw = z_b.shape
    w_bd = jnp.zeros((gw, gw), F32)
    for gi in range(len(POOL_WINDOWS)):
        w_bd = w_bd.at[gi * POOL_CG:(gi + 1) * POOL_CG, gi * POOL_CG:(gi + 1) * POOL_CG].set(w[gi])
    return pl.pallas_call(
        _pool_kernel,
        grid=(bsz,),
        in_specs=[pl.BlockSpec((1, t, gw), lambda i: (i, 0, 0)),
                  pl.BlockSpec((gw, gw), lambda i: (0, 0)),
                  pl.BlockSpec((1, gw), lambda i: (0, 0)),
                  pl.BlockSpec((1, gw), lambda i: (0, 0))],
        out_specs=pl.BlockSpec((1, t, gw), lambda i: (i, 0, 0)),
        out_shape=jax.ShapeDtypeStruct((bsz, t, gw), F32),
        compiler_params=_cparams(("parallel",)),
        name="pool_mixer",
    )(z_b, w_bd.astype(BF16), b.reshape(1, gw), scale.reshape(1, gw))


def _sgu_kernel(z_ref, lg_ref, lb_ref, w_ref, bm_ref, o_ref, *, chunks):
    c = SGU_CHUNK
    gw = GROUP_WIDTH
    rows = SGU_GROUPS * c
    tri = (lax.broadcasted_iota(jnp.int32, (rows, c), 1)
           <= lax.broadcasted_iota(jnp.int32, (rows, c), 0) % c)
    ws = jnp.where(tri, w_ref[...], 0.0).astype(BF16)
    grp = lax.broadcasted_iota(jnp.int32, (1, gw), 1) // SGU_CG
    for ci in range(chunks):
        z = jax.nn.gelu(z_ref[0, ci * c:(ci + 1) * c, :], approximate=True)
        u, v = z[:, :gw], z[:, gw:]
        mu = jnp.mean(v, axis=-1, keepdims=True)
        var = jnp.mean(jnp.square(v - mu), axis=-1, keepdims=True)
        vn = (v - mu) * lax.rsqrt(var + EPS) * lg_ref[...] + lb_ref[...]
        r = _dot(ws, vn.astype(BF16))
        mixed = bm_ref[...]
        for g in range(SGU_GROUPS):
            mixed = mixed + jnp.where(grp == g, r[g * c:(g + 1) * c, :], 0.0)
        o_ref[0, ci * c:(ci + 1) * c, :] = u * mixed


def sgu_mixer(z_d, ln_g, ln_b, w_s, b_s, chunks=4):
    bsz, t, _ = z_d.shape
    gw = GROUP_WIDTH
    tt = chunks * SGU_CHUNK
    bias = jnp.repeat(b_s.T, SGU_CG, axis=1)
    return pl.pallas_call(
        functools.partial(_sgu_kernel, chunks=chunks),
        grid=(bsz, t // tt),
        in_specs=[pl.BlockSpec((1, tt, 2 * gw), lambda b, i: (b, i, 0)),
                  pl.BlockSpec((1, gw), lambda b, i: (0, 0)),
                  pl.BlockSpec((1, gw), lambda b, i: (0, 0)),
                  pl.BlockSpec((SGU_GROUPS * SGU_CHUNK, SGU_CHUNK), lambda b, i: (0, 0)),
                  pl.BlockSpec((SGU_CHUNK, gw), lambda b, i: (0, 0))],
        out_specs=pl.BlockSpec((1, tt, gw), lambda b, i: (b, i, 0)),
        out_shape=jax.ShapeDtypeStruct((bsz, t, gw), F32),
        compiler_params=_cparams(("parallel", "parallel")),
        name="sgu_mixer",
    )(z_d, ln_g.reshape(1, gw), ln_b.reshape(1, gw),
      w_s.reshape(SGU_GROUPS * SGU_CHUNK, SGU_CHUNK), bias)


def _bucket_table():
    assert REL_MAX_DIST <= DSA_BLOCK + 1
    s = np.arange(DSA_BLOCK)[:, None]
    t = np.arange(DSA_BLOCK)[None, :]
    dist = np.stack([t - s, DSA_BLOCK + t - s, 2 * DSA_BLOCK + t - s])
    n = np.maximum(dist, 0)
    max_exact = REL_BUCKETS // 2
    nf = np.maximum(n, 1).astype(np.float32)
    large = max_exact + (np.log(nf / np.float32(max_exact)) / np.float32(math.log(REL_MAX_DIST / max_exact))
                         * np.float32(REL_BUCKETS - max_exact)).astype(np.int32)
    return np.where(n < max_exact, n, np.minimum(large, REL_BUCKETS - 1)).astype(np.int32)


def _bias_table_kernel(rb_ref, bucket_ref, o_ref):
    for back in range(3):
        bucket = bucket_ref[back]
        for h in range(DSA_HEADS):
            acc = jnp.zeros(bucket.shape, F32)
            for b in range(REL_BUCKETS):
                acc = jnp.where(bucket == b, rb_ref[b * DSA_HEADS + h], acc)
            o_ref[back, h] = acc


def rel_bias_tables(rel_bias):
    blk = DSA_BLOCK
    return pl.pallas_call(
        _bias_table_kernel,
        in_specs=[pl.BlockSpec(memory_space=pltpu.SMEM),
                  pl.BlockSpec((3, blk, blk), lambda: (0, 0, 0))],
        out_specs=pl.BlockSpec((3, DSA_HEADS, blk, blk), lambda: (0, 0, 0, 0)),
        out_shape=jax.ShapeDtypeStruct((3, DSA_HEADS, blk, blk), F32),
        name="rel_bias_tables",
    )(rel_bias.reshape(-1), jnp.asarray(_bucket_table()))


def _dsa_kernel(q_ref, k_ref, v_ref, qi_ref, kw_ref, qw_ref, toe_ref, o_ref,
                kpl_ref, vt_ref, keys_ref, khi_ref, klo_ref, acc_ref, am_ref, lg_ref, p_ref,
                *, topk, n_blocks, idx_bits):
    blk = DSA_BLOCK
    sub = DSA_SUB
    i = pl.program_id(1)
    hd = DSA_HEADS * DSA_DH
    heads_per_half = 128 // IDX_DIM

    @pl.when(i == 0)
    def _():
        lane = lax.broadcasted_iota(jnp.int32, (blk, 128), 1)

        def build(kb, c):
            r0 = pl.multiple_of(kb * blk, blk)
            ki = jnp.where(lane < IDX_DIM, kw_ref[0, pl.ds(r0, blk), :], 0.0)
            for j in range(heads_per_half):
                kpl_ref[kb, j] = (ki if j == 0 else pltpu.roll(ki, j * IDX_DIM, axis=1)).astype(BF16)
            vt_ref[kb] = jnp.transpose(v_ref[0, pl.ds(r0, blk), :].astype(F32)).astype(BF16)
            return c

        lax.fori_loop(0, n_blocks, build, 0)

    s_loc = lax.broadcasted_iota(jnp.int32, (blk, blk), 0)
    t_loc = lax.broadcasted_iota(jnp.int32, (blk, blk), 1)
    n_vis = i + 1

    qi_t = jnp.transpose(qi_ref[0].astype(F32)).astype(BF16)
    qi_halves = [qi_t[:128, :], qi_t[128:, :]]
    w_t = jnp.transpose(qw_ref[0])
    w_rows = [w_t[IDX_DIM + h:IDX_DIM + h + 1, :] * (IDX_HEADS ** -0.5) * (IDX_DIM ** -0.5)
              for h in range(IDX_HEADS)]
    s_sub = lax.broadcasted_iota(jnp.int32, (sub, blk), 0)
    t_sub = lax.broadcasted_iota(jnp.int32, (sub, blk), 1)

    def score_body(kb, c):
        for ci in range(blk // sub):
            rows = slice(ci * sub, (ci + 1) * sub)
            sc = jnp.zeros((sub, blk), F32)
            for half in range(2):
                for j in range(heads_per_half):
                    d = _dot(kpl_ref[kb, j, rows, :], qi_halves[half])
                    sc = sc + jnp.maximum(d, 0.0) * w_rows[half * heads_per_half + j]
            sc = jnp.where(sc == 0.0, 0.0, sc)
            bits = pltpu.bitcast(sc, jnp.int32)
            key = jnp.where(bits < 0, bits ^ jnp.int32(0x7FFFFFFF), bits)
            vis = (kb < i) | (s_sub + ci * sub <= t_sub)
            key = jnp.where(vis, key, jnp.int32(INT_MIN))
            keys_ref[kb, rows, :] = key
            khi_ref[kb, rows, :] = lax.shift_right_arithmetic(key, 16).astype(jnp.int16)
            klo_ref[kb, rows, :] = ((key & 0xFFFF) - 2 ** 15).astype(jnp.int16)
        return c

    lax.fori_loop(0, n_vis, score_body, 0)

    def count(pred):
        def body(kb, acc):
            hit = jnp.where(pred(keys_ref[kb], kb), 1.0, 0.0)
            return acc + jnp.sum(hit.reshape(blk // 32, 32, blk), axis=0)
        acc = lax.fori_loop(0, n_vis, body, jnp.zeros((32, blk), F32))
        return jnp.sum(acc, axis=0, keepdims=True)

    def count16(ref, cand):
        cand = cand.astype(jnp.int16)

        def body(kb, acc):
            hit = jnp.where(ref[kb] >= cand, jnp.int16(1), jnp.int16(0))
            for j in range(blk // 32):
                acc = acc + hit[j * 32:(j + 1) * 32]
            return acc
        acc = lax.fori_loop(0, n_vis, body, jnp.zeros((32, blk), jnp.int16))
        return jnp.sum(acc.astype(jnp.int32).astype(F32), axis=0, keepdims=True)

    def search16(ref, offset):
        lowest = jnp.full((1, blk), -(2 ** 15), jnp.int32)
        base = jnp.where(offset + count16(ref, jnp.zeros((1, blk), jnp.int32)) >= kf, 0, lowest)

        def bit_body(it, base):
            cand = base | lax.shift_left(jnp.int32(1), 14 - it)
            return jnp.where(offset + count16(ref, cand) >= kf, cand, base)
        return lax.fori_loop(0, 15, bit_body, base)

    kf = float(topk)
    zero = jnp.zeros((1, blk), jnp.int32)
    thr_hi = search16(khi_ref, 0.0)
    thr_hi16 = thr_hi.astype(jnp.int16)

    def low_body(kb, above):
        hi = khi_ref[kb]
        klo_ref[kb] = jnp.where(hi == thr_hi16, klo_ref[kb], jnp.int16(-(2 ** 15)))
        hit = jnp.where(hi > thr_hi16, jnp.int16(1), jnp.int16(0))
        for j in range(blk // 32):
            above = above + hit[j * 32:(j + 1) * 32]
        return above

    above = lax.fori_loop(0, n_vis, low_body, jnp.zeros((32, blk), jnp.int16))
    n_above = jnp.sum(above.astype(jnp.int32).astype(F32), axis=0, keepdims=True)
    thr_lo = search16(klo_ref, n_above)
    thr = lax.shift_left(thr_hi, 16) | (thr_lo + 2 ** 15)
    thr_sel = jnp.maximum(thr, jnp.int32(INT_MIN + 1))

    n_ge = count(lambda key, kb: key >= thr)
    excess = jnp.where((n_ge > kf) & (thr > jnp.int32(INT_MIN)), 1.0, 0.0)

    def tie_search():
        need = kf - count(lambda key, kb: key > thr)

        def tie_body(it, j):
            cand = j | lax.shift_left(jnp.int32(1), idx_bits - 1 - it)
            below = count(lambda key, kb: (key == thr) & (kb * blk + s_loc < cand))
            return jnp.where(below < need, cand, j)
        return lax.fori_loop(0, idx_bits, tie_body, zero)

    last = lax.cond(jnp.max(excess) > 0.0, tie_search, lambda: jnp.full((1, blk), 2 ** 30, jnp.int32))

    assert DSA_DH ** -0.5 == 0.125
    q_t = jnp.transpose(q_ref[0].astype(F32) * (DSA_DH ** -0.5))
    row_h = lax.broadcasted_iota(jnp.int32, (hd, 1), 0) // DSA_DH
    q_heads = [jnp.where(row_h == h, q_t, 0.0).astype(BF16) for h in range(DSA_HEADS)]
    acc_ref[...] = jnp.zeros_like(acc_ref)
    n_sub = blk // sub

    def att_body(kb, carry):
        ms, ls = carry
        back = jnp.minimum(i - kb, 2)
        for ci in range(n_sub):
            rows = slice(ci * sub, (ci + 1) * sub)
            key = keys_ref[kb, rows, :]
            sel = (key > thr_sel) | ((key == thr_sel) & (kb * blk + ci * sub + s_sub <= last))
            am_ref[rows, :] = jnp.where(sel, 0.0, NEG_BIG)
        new_ms, new_ls = list(ms), list(ls)
        for heads in DSA_HEAD_GROUPS:
            alphas = {}
            for h in heads:
                pm = jnp.full((8, blk), NEG_BIG, F32)
                for ci in range(n_sub):
                    rows = slice(ci * sub, (ci + 1) * sub)
                    k_rows = k_ref[0, pl.ds(pl.multiple_of(kb * blk + ci * sub, sub), sub), :]
                    lg = _dot(k_rows, q_heads[h]) + toe_ref[back, h, rows, :] + am_ref[rows, :]
                    lg_ref[h, rows, :] = lg
                    pm = jnp.maximum(pm, jnp.max(lg.reshape(sub // 8, 8, blk), axis=0))
                new_ms[h] = jnp.maximum(ms[h], jnp.max(pm, axis=0, keepdims=True))
                alphas[h] = jnp.exp(ms[h] - new_ms[h])
            for h in heads:
                ps = jnp.zeros((8, blk), F32)
                for ci in range(n_sub):
                    rows = slice(ci * sub, (ci + 1) * sub)
                    p = jnp.exp(lg_ref[h, rows, :] - new_ms[h])
                    ps = ps + jnp.sum(p.reshape(sub // 8, 8, blk), axis=0)
                    p_ref[h, rows, :] = p.astype(BF16)
                new_ls[h] = ls[h] * alphas[h] + jnp.sum(ps, axis=0, keepdims=True)
            for h in heads:
                hrows = slice(h * DSA_DH, (h + 1) * DSA_DH)
                acc_ref[hrows, :] = acc_ref[hrows, :] * alphas[h] + _dot(vt_ref[kb, hrows, :], p_ref[h])
        return tuple(new_ms), tuple(new_ls)

    init = (tuple(jnp.full((1, blk), 0.01 * NEG_BIG, F32) for _ in range(DSA_HEADS)),
            tuple(jnp.zeros((1, blk), F32) for _ in range(DSA_HEADS)))
    _, ls = lax.fori_loop(0, n_vis, att_body, init)
    for h in range(DSA_HEADS):
        rows = slice(h * DSA_DH, (h + 1) * DSA_DH)
        acc_ref[rows, :] = acc_ref[rows, :] * (1.0 / ls[h])
    o_ref[0] = jnp.transpose(acc_ref[...])


def dsa_mixer(z_c, z_aux, toe):
    bsz, t, _ = z_c.shape
    blk = DSA_BLOCK
    n_blocks = t // blk
    assert t % blk == 0
    topk = min(DSA_TOPK_MAX, t // 4)
    hd = DSA_HEADS * DSA_DH
    kernel = functools.partial(_dsa_kernel, topk=topk, n_blocks=n_blocks,
                               idx_bits=max(1, (t - 1).bit_length()))
    return pl.pallas_call(
        kernel,
        grid=(bsz, n_blocks),
        in_specs=[pl.BlockSpec((1, blk, hd), lambda b, i: (b, i, 0)),
                  pl.BlockSpec((1, t, hd), lambda b, i: (b, 0, 1)),
                  pl.BlockSpec((1, t, hd), lambda b, i: (b, 0, 2)),
                  pl.BlockSpec((1, blk, hd), lambda b, i: (b, i, 3)),
                  pl.BlockSpec((1, t, C_AUX), lambda b, i: (b, 0, 0)),
                  pl.BlockSpec((1, blk, C_AUX), lambda b, i: (b, i, 0)),
                  pl.BlockSpec((3, DSA_HEADS, blk, blk), lambda b, i: (0, 0, 0, 0))],
        out_specs=pl.BlockSpec((1, blk, hd), lambda b, i: (b, i, 0)),
        out_shape=jax.ShapeDtypeStruct((bsz, t, hd), F32),
        scratch_shapes=[pltpu.VMEM((n_blocks, 128 // IDX_DIM, blk, 128), BF16),
                        pltpu.VMEM((n_blocks, hd, blk), BF16),
                        pltpu.VMEM((n_blocks, blk, blk), jnp.int32),
                        pltpu.VMEM((n_blocks, blk, blk), jnp.int16),
                        pltpu.VMEM((n_blocks, blk, blk), jnp.int16),
                        pltpu.VMEM((hd, blk), F32),
                        pltpu.VMEM((blk, blk), F32),
                        pltpu.VMEM((DSA_HEADS, blk, blk), F32),
                        pltpu.VMEM((DSA_HEADS, blk, blk), BF16)],
        compiler_params=_cparams(("parallel", "arbitrary")),
        name="dsa_mixer",
    )(z_c, z_c, z_c, z_c, z_aux, z_aux, toe)


def _mixer_tail_kernel(h_ref, a_ref, b_ref, c_ref, d_ref, wout_ref, gxa_ref, wq_ref, k_ref, v_ref, wo_ref,
                       *refs, route):
    gw = GROUP_WIDTH
    x = h_ref[0]
    for gi, r in enumerate((a_ref, b_ref, c_ref, d_ref)):
        x = x + _dot(r[0].astype(BF16), wout_ref[gi * gw:(gi + 1) * gw, :])

    hd = XA_HEADS * XA_DH
    q = _dot(_rms(x, gxa_ref[...]).astype(BF16), wq_ref[...]).astype(BF16)
    k = k_ref[0]
    v = v_ref[0]
    lane_h = lax.broadcasted_iota(jnp.int32, (1, hd), 1) // XA_DH
    o = jnp.zeros((x.shape[0], hd), F32)
    for h in range(XA_HEADS):
        s = _dot_nt(q, jnp.where(lane_h == h, k, 0.0).astype(BF16)) * (XA_DH ** -0.5)
        p = jnp.exp(s - jnp.max(s, axis=-1, keepdims=True))
        p = p / jnp.sum(p, axis=-1, keepdims=True)
        o = o + _dot(p.astype(BF16), jnp.where(lane_h == h, v, 0.0).astype(BF16))
    x = x + _dot(o.astype(BF16), wo_ref[...])

    if not route:
        o_ref, = refs
        o_ref[0] = x
        return
    gffn_ref, wr_ref, o_ref, hn_ref, gate_ref, cnt_ref = refs
    o_ref[0] = x
    hn = _rms(x, gffn_ref[...])
    hn_ref[0] = hn.astype(BF16)
    tm = hn.shape[0]
    lane = lax.broadcasted_iota(jnp.int32, (tm, GATE_LANES), 1)
    logits = jnp.where(lane < N_EXPERTS, _dot_3pass(hn, wr_ref[...]), -jnp.inf)
    m1 = jnp.max(logits, axis=-1, keepdims=True)
    i1 = jnp.min(jnp.where(logits == m1, lane, GATE_LANES), axis=-1, keepdims=True)
    rest = jnp.where(lane == i1, -jnp.inf, logits)
    m2 = jnp.max(rest, axis=-1, keepdims=True)
    i2 = jnp.min(jnp.where(rest == m2, lane, GATE_LANES), axis=-1, keepdims=True)
    e2 = jnp.exp(m2 - m1)
    g1 = 1.0 / (1.0 + e2)
    gates = jnp.where(lane == i1, g1, 0.0) + jnp.where(lane == i2, e2 * g1, 0.0)
    gates = jnp.where(lane == N_EXPERTS, i1.astype(F32), gates)
    gate_ref[0] = jnp.where(lane == N_EXPERTS + 1, i2.astype(F32), gates)
    sel = jnp.where((lane == i1) | (lane == i2), 1.0, 0.0)
    for c in range(tm // MOE_CHUNK):
        cnt_ref[c] = jnp.sum(sel[c * MOE_CHUNK:(c + 1) * MOE_CHUNK, :], axis=0, keepdims=True)


def mixer_tail(h3, mixers, w_out, g_xa, wq, k, v, wo, routing=None, tm=1024):
    bsz, t, d = h3.shape
    tm = min(tm, t)
    mlen = k.shape[1]
    hd = XA_HEADS * XA_DH
    gw = GROUP_WIDTH
    row = lambda b, i: (b, i, 0)
    fixed2 = lambda b, i: (0, 0)
    in_specs = ([pl.BlockSpec((1, tm, d), row)] + [pl.BlockSpec((1, tm, gw), row)] * 4
                + [pl.BlockSpec((4 * gw, d), fixed2), pl.BlockSpec((1, d), fixed2), pl.BlockSpec((d, hd), fixed2),
                   pl.BlockSpec((1, mlen, hd), lambda b, i: (b, 0, 0)),
                   pl.BlockSpec((1, mlen, hd), lambda b, i: (b, 0, 0)),
                   pl.BlockSpec((hd, d), fixed2)])
    args = [h3, *mixers, w_out, g_xa.reshape(1, d), wq, k, v, wo]
    out_specs = [pl.BlockSpec((1, tm, d), row)]
    out_shape = [jax.ShapeDtypeStruct((bsz, t, d), F32)]
    if routing is not None:
        assert tm % MOE_CHUNK == 0
        g_ffn, wr = routing
        in_specs += [pl.BlockSpec((1, d), fixed2), pl.BlockSpec((d, GATE_LANES), fixed2)]
        args += [g_ffn.reshape(1, d), jnp.zeros((d, GATE_LANES), F32).at[:, :N_EXPERTS].set(wr)]
        per_step = tm // MOE_CHUNK
        out_specs += [pl.BlockSpec((1, tm, d), row), pl.BlockSpec((1, tm, GATE_LANES), row),
                      pl.BlockSpec((per_step, 1, GATE_LANES), lambda b, i: (b * (t // tm) + i, 0, 0))]
        out_shape += [jax.ShapeDtypeStruct((bsz, t, d), BF16), jax.ShapeDtypeStruct((bsz, t, GATE_LANES), F32),
                      jax.ShapeDtypeStruct((bsz * t // MOE_CHUNK, 1, GATE_LANES), F32)]
    return pl.pallas_call(
        functools.partial(_mixer_tail_kernel, route=routing is not None),
        grid=(bsz, t // tm),
        in_specs=in_specs,
        out_specs=out_specs,
        out_shape=out_shape,
        compiler_params=_cparams(("parallel", "parallel")),
        name="mixer_tail",
    )(*args)


def _ffn_kernel(h_ref, g_ref, w1_ref, w3_ref, w2_ref, o_ref, *, tf):
    x = h_ref[...]
    hn = _rms(x, g_ref[...]).astype(BF16)
    acc = x
    for f0 in range(0, w1_ref.shape[1], tf):
        a = _dot(hn, w1_ref[:, f0:f0 + tf])
        b = _dot(hn, w3_ref[:, f0:f0 + tf])
        acc = acc + _dot((a * jax.nn.sigmoid(a) * b).astype(BF16), w2_ref[f0:f0 + tf, :])
    o_ref[...] = acc


def ffn(h, g, w1, w3, w2, tm=512, tf=704):
    m, d = h.shape
    tm = min(tm, m)
    nf = w1.shape[1]
    assert nf % tf == 0
    resident = lambda shape: pl.BlockSpec(shape, lambda i: (0, 0), pipeline_mode=pl.Buffered(1))
    return pl.pallas_call(
        functools.partial(_ffn_kernel, tf=tf),
        grid=(m // tm,),
        in_specs=[pl.BlockSpec((tm, d), lambda i: (i, 0)),
                  pl.BlockSpec((1, d), lambda i: (0, 0)),
                  resident((d, nf)), resident((d, nf)), resident((nf, d))],
        out_specs=pl.BlockSpec((tm, d), lambda i: (i, 0)),
        out_shape=jax.ShapeDtypeStruct((m, d), F32),
        compiler_params=_cparams(("parallel",)),
        name="ffn",
    )(h, g.reshape(1, d), w1, w3, w2)


GATE_LANES = 128
MOE_CHUNK = 256
MOE_ROWS = 128
MOE_TOKENS = 2048
MOE_ALIGN = 16


def _moe_kernel(cb_ref, h_ref, hn_ref, gate_ref, w1_ref, w3_ref, w2_ref, *refs, final):
    if final:
        gfin_ref, *refs = refs
    y_ref, rank_row, rk_ref, gs_ref, xs_ref, yacc_ref, rc_ref, gc_ref = refs
    t = pl.program_id(0)
    e = pl.program_id(1)
    f = pl.program_id(2)
    ts, d = hn_ref.shape
    ch, rb = MOE_CHUNK, MOE_ROWS
    win = 2 * rb
    n_ch = ts // ch
    lane = lax.broadcasted_iota(jnp.int32, (1, GATE_LANES), 1)

    def before(c):
        return cb_ref[(t * (n_ch + 1) + c) * N_EXPERTS + e]

    n_blocks = (before(n_ch) + rb - 1) // rb
    n_windows = (n_blocks * rb + win - 1) // win + 1

    def windows(c):
        lo, hi = before(c), before(c + 1)
        s0 = (lo // MOE_ALIGN) * MOE_ALIGN
        return s0, jnp.where(hi > lo, (hi - s0 + win - 1) // win, 0)

    spans = [windows(c) for c in range(n_ch)]
    single = functools.reduce(jnp.logical_and, [n_win <= 1 for _, n_win in spans])

    def for_each_window(visit):
        @pl.when(single)
        def _():
            for c, (s0, _) in enumerate(spans):
                visit(c, s0)

        @pl.when(jnp.logical_not(single))
        def _():
            for c, (s0, n_win) in enumerate(spans):
                def body(j, carry, c=c, s0=s0):
                    visit(c, s0 + j * win)
                    return carry
                lax.fori_loop(0, n_win, body, 0)

    @pl.when((e == 0) & (f == 0))
    def _():
        y_ref[...] = h_ref[...]
        strict_lower = (lax.broadcasted_iota(jnp.int32, (ch, ch), 1)
                        < lax.broadcasted_iota(jnp.int32, (ch, ch), 0)).astype(BF16)
        lane_f = lane.astype(F32)
        offs = jnp.ones((1, GATE_LANES), F32)
        for c in range(n_ch):
            rows = slice(c * ch, (c + 1) * ch)
            g = gate_ref[rows, :]
            sel = jnp.where((lane_f == g[:, N_EXPERTS:N_EXPERTS + 1])
                            | (lane_f == g[:, N_EXPERTS + 1:N_EXPERTS + 2]), 1.0, 0.0)
            r = _dot(strict_lower, sel.astype(BF16)) + offs
            r = jnp.where(sel > 0.0, r, 0.0)
            rank_row[:, rows] = jnp.transpose(r) - 1.0
            high = jnp.floor(r * (1.0 / 256.0))
            rk_ref[0, rows, :] = high.astype(BF16)
            rk_ref[1, rows, :] = (r - 256.0 * high).astype(BF16)
            for j, part in enumerate(_split_bf16(g, 3)):
                gs_ref[j, rows, :] = part
            offs = offs + jnp.sum(sel, axis=0, keepdims=True)

    @pl.when(f == 0)
    def _():
        pick = (lax.broadcasted_iota(jnp.int32, (GATE_LANES, GATE_LANES), 0) == e).astype(BF16)
        for c in range(n_ch):
            rows = slice(c * ch, (c + 1) * ch)
            rc_ref[rows, :] = 256.0 * _dot(rk_ref[0, rows, :], pick) + _dot(rk_ref[1, rows, :], pick) - 1.0
            gc_ref[rows, :] = (_dot(gs_ref[0, rows, :], pick) + _dot(gs_ref[1, rows, :], pick)
                               + _dot(gs_ref[2, rows, :], pick))

        def clear(j, carry):
            r0 = pl.multiple_of(j * win, win)
            xs_ref[pl.ds(r0, win), :] = jnp.zeros((win, d), BF16)
            yacc_ref[pl.ds(r0, win), :] = jnp.zeros((win, d), F32)
            return carry

        lax.fori_loop(0, n_windows, clear, 0)

        row_id = lax.broadcasted_iota(jnp.int32, (win, ch), 0)

        def gather_window(c, s):
            s = pl.multiple_of(s, MOE_ALIGN)
            ranks = rank_row[pl.ds(e, 1), c * ch:(c + 1) * ch]
            onehot = jnp.where(ranks == (s + row_id).astype(F32), 1.0, 0.0).astype(BF16)
            xs_ref[pl.ds(s, win), :] += _dot(onehot, hn_ref[c * ch:(c + 1) * ch, :]).astype(BF16)

        for_each_window(gather_window)

    def ffn_rows(r0, rows):
        x = xs_ref[pl.ds(r0, rows), :]
        a = _dot(x, w1_ref[0])
        g3 = _dot(x, w3_ref[0])
        yacc_ref[pl.ds(r0, rows), :] += _dot((a * jax.nn.sigmoid(a) * g3).astype(BF16), w2_ref[0])

    n_quads = n_blocks // 4

    def ffn_quad(j, carry):
        ffn_rows(pl.multiple_of(j * (4 * rb), 4 * rb), 4 * rb)
        return carry

    lax.fori_loop(0, n_quads, ffn_quad, 0)
    tail = pl.multiple_of(n_quads * (4 * rb), 4 * rb)

    @pl.when((n_blocks & 2) != 0)
    def _():
        ffn_rows(tail, 2 * rb)

    @pl.when((n_blocks & 1) != 0)
    def _():
        ffn_rows(pl.multiple_of(tail + (n_blocks & 2) * rb, rb), rb)

    @pl.when(f == pl.num_programs(2) - 1)
    def _():
        def to_bf16(j, carry):
            r0 = pl.multiple_of(j * win, win)
            xs_ref[pl.ds(r0, win), :] = yacc_ref[pl.ds(r0, win), :].astype(BF16)
            return carry

        lax.fori_loop(0, n_windows, to_bf16, 0)

        lane_id = lax.broadcasted_iota(jnp.int32, (ch, rb), 1).astype(F32)

        def scatter_window(c, s):
            rows = slice(c * ch, (c + 1) * ch)
            s = pl.multiple_of(s, MOE_ALIGN)
            rank = rc_ref[rows, :] - s.astype(F32)
            onehot = jnp.concatenate([jnp.where(rank == lane_id, 1.0, 0.0),
                                      jnp.where(rank == lane_id + float(rb), 1.0, 0.0)],
                                     axis=1).astype(BF16)
            gate = jnp.concatenate([gc_ref[rows, :]] * (d // GATE_LANES), axis=1)
            y_ref[rows, :] += gate * _dot(onehot, xs_ref[pl.ds(s, win), :])

        for_each_window(scatter_window)

        if final:
            @pl.when(e == pl.num_programs(1) - 1)
            def _():
                y_ref[...] = _rms(y_ref[...], gfin_ref[...])


def moe(h, hn, gates, cnt, w1, w3, w2, final_gain=None, tf=896):
    m, d = hn.shape
    ts = min(MOE_TOKENS, m)
    n_exp, _, nf = w1.shape
    assert nf % tf == 0 and m % ts == 0 and ts % MOE_CHUNK == 0 and MOE_ROWS == GATE_LANES
    n_ch = ts // MOE_CHUNK
    counts = cnt.reshape(m // ts, n_ch, GATE_LANES)[:, :, :n_exp].astype(jnp.int32)
    before = jnp.concatenate([jnp.zeros((m // ts, 1, n_exp), jnp.int32), jnp.cumsum(counts, axis=1)], axis=1)
    per_tile = lambda width: pl.BlockSpec((ts, width), lambda t, e, f, cb: (t, 0), pipeline_mode=pl.Buffered(1))
    in_specs = [per_tile(d), per_tile(d), per_tile(GATE_LANES),
                pl.BlockSpec((1, d, tf), lambda t, e, f, cb: (e, 0, f)),
                pl.BlockSpec((1, d, tf), lambda t, e, f, cb: (e, 0, f)),
                pl.BlockSpec((1, tf, d), lambda t, e, f, cb: (e, f, 0))]
    args = [before.reshape(-1), h, hn, gates, w1, w3, w2]
    if final_gain is not None:
        in_specs.append(pl.BlockSpec((1, d), lambda t, e, f, cb: (0, 0)))
        args.append(final_gain.reshape(1, d))
    grid_spec = pltpu.PrefetchScalarGridSpec(
        num_scalar_prefetch=1,
        grid=(m // ts, n_exp, nf // tf),
        in_specs=in_specs,
        out_specs=per_tile(d),
        scratch_shapes=[pltpu.VMEM((GATE_LANES, ts), F32),
                        pltpu.VMEM((2, ts, GATE_LANES), BF16),
                        pltpu.VMEM((3, ts, GATE_LANES), BF16),
                        pltpu.VMEM((ts + 2 * MOE_ROWS, d), BF16),
                        pltpu.VMEM((ts + 2 * MOE_ROWS, d), F32),
                        pltpu.VMEM((ts, MOE_ROWS), F32),
                        pltpu.VMEM((ts, GATE_LANES), F32)])
    return pl.pallas_call(
        functools.partial(_moe_kernel, final=final_gain is not None),
        grid_spec=grid_spec,
        out_shape=jax.ShapeDtypeStruct((m, d), F32),
        compiler_params=_cparams(("parallel", "arbitrary", "arbitrary")),
        name="moe",
    )(*args)


def _final_norm_kernel(g_ref, x_ref, o_ref):
    o_ref[...] = _rms(x_ref[...], g_ref[...])


def final_rmsnorm(g, x, tm=1024):
    m, d = x.shape
    tm = min(tm, m)
    return pl.pallas_call(
        _final_norm_kernel,
        grid=(m // tm,),
        in_specs=[pl.BlockSpec((1, d), lambda i: (0, 0)), pl.BlockSpec((tm, d), lambda i: (i, 0))],
        out_specs=pl.BlockSpec((tm, d), lambda i: (i, 0)),
        out_shape=jax.ShapeDtypeStruct((m, d), F32),
        compiler_params=_cparams(("parallel",)),
        name="final_norm",
    )(g.reshape(1, d), x)


def _pad_cols(w, width):
    return jnp.pad(w, ((0, 0), (0, width - w.shape[1])))


def _in_proj_weight(w_in):
    a_end = 2 * GLA_HEADS * GLA_DK + 2 * GROUP_WIDTH + GLA_RANK
    b_end = a_end + GROUP_WIDTH
    c_end = b_end + 3 * GROUP_WIDTH + IDX_HEADS * IDX_DIM + IDX_DIM + IDX_HEADS
    assert w_in.shape[1] == c_end + 2 * GROUP_WIDTH
    return jnp.concatenate([_pad_cols(w_in[:, :a_end], A_WIDTH), w_in[:, a_end:b_end],
                            _pad_cols(w_in[:, b_end:c_end], C_MAIN + C_AUX), w_in[:, c_end:]], axis=1).astype(BF16)


def hybrid_layer(h, mem2, toe, p, bsz, t, routing=None):
    m, d = h.shape
    z_a, z_b, z_c, z_aux, z_d = norm_matmul(h, p["norm_mix"], _in_proj_weight(p["w_in"]),
                                            (A_WIDTH, B_WIDTH, C_MAIN, C_AUX, D_WIDTH),
                                            (F32, F32, BF16, F32, F32))
    o_a = gla_mixer(z_a.reshape(bsz, t, A_WIDTH), p["gla_wa2"], p["gla_ba"], p["gla_norm"])
    o_b = pool_mixer(z_b.reshape(bsz, t, B_WIDTH), p["pool_w"], p["pool_b"].reshape(-1), p["pool_scale"])
    o_c = dsa_mixer(z_c.reshape(bsz, t, C_MAIN), z_aux.reshape(bsz, t, C_AUX), toe)
    o_d = sgu_mixer(z_d.reshape(bsz, t, D_WIDTH), p["sgu_ln_g"], p["sgu_ln_b"], p["sgu_w"], p["sgu_b"])
    hd = XA_HEADS * XA_DH
    k, v = norm_matmul(mem2, p["norm_mem"], p["xa_wkv"].astype(BF16), (hd, hd))
    mlen = mem2.shape[0] // bsz
    outs = mixer_tail(h.reshape(bsz, t, d), (o_a, o_b, o_c, o_d), p["w_out"].astype(BF16), p["norm_xa"],
                      p["xa_wq"].astype(BF16), k.reshape(bsz, mlen, hd), v.reshape(bsz, mlen, hd),
                      p["xa_wo"].astype(BF16), routing)
    if routing is None:
        return (outs[0].reshape(m, d),)
    h3, hn, gates, counts = outs
    return h3.reshape(m, d), hn.reshape(m, d), gates.reshape(m, GATE_LANES), counts


def kernel(x, mem, rel_bias, final_norm, norm_mix, w_in, gla_wa2, gla_ba, gla_norm, pool_w, pool_b,
           pool_scale, sgu_ln_g, sgu_ln_b, sgu_w, sgu_b, w_out, norm_xa, norm_mem, xa_wq, xa_wkv, xa_wo,
           norm_ffn, ffn_w1, ffn_w3, ffn_w2, router, moe_w1, moe_w3, moe_w2):
    bsz, t, d = x.shape
    depth = norm_mix.shape[0]
    h = x.reshape(bsz * t, d)
    mem2 = mem.reshape(-1, d)
    toe = rel_bias_tables(rel_bias)
    normed = False
    for i in range(depth):
        p = dict(norm_mix=norm_mix[i], w_in=w_in[i], gla_wa2=gla_wa2[i], gla_ba=gla_ba[i],
                 gla_norm=gla_norm[i], pool_w=pool_w[i], pool_b=pool_b[i], pool_scale=pool_scale[i],
                 sgu_ln_g=sgu_ln_g[i], sgu_ln_b=sgu_ln_b[i], sgu_w=sgu_w[i], sgu_b=sgu_b[i],
                 w_out=w_out[i], norm_xa=norm_xa[i], norm_mem=norm_mem[i], xa_wq=xa_wq[i],
                 xa_wkv=xa_wkv[i], xa_wo=xa_wo[i])
        j = i // 2
        if i % 2 == 0:
            h, = hybrid_layer(h, mem2, toe, p, bsz, t)
            h = ffn(h, norm_ffn[i], ffn_w1[j].astype(BF16), ffn_w3[j].astype(BF16), ffn_w2[j].astype(BF16))
        else:
            h, hn, gates, counts = hybrid_layer(h, mem2, toe, p, bsz, t, routing=(norm_ffn[i], router[j]))
            normed = i == depth - 1
            h = moe(h, hn, gates, counts, moe_w1[j].astype(BF16), moe_w3[j].astype(BF16),
                    moe_w2[j].astype(BF16), final_gain=final_norm if normed else None)
    if not normed:
        h = final_rmsnorm(final_norm, h)
    return h.reshape(bsz, t, d)
```

```python
import functools
import math

import jax
import jax.numpy as jnp
import numpy as np
from jax import lax
from jax.experimental import pallas as pl
from jax.experimental.pallas import tpu as pltpu

F32 = jnp.float32
BF16 = jnp.bfloat16
EPS = 1e-6

GROUP_WIDTH = 256

GLA_HEADS = 4
GLA_DV = 64
GLA_DK = 32
GLA_RANK = 16
GLA_TAU = 16.0
GLA_CHUNK = 64
GLA_GROUP = 16

POOL_WINDOWS = (2, 4, 8, 16)
POOL_CG = 64

DSA_HEADS = 4
DSA_DH = 64
IDX_HEADS = 8
IDX_DIM = 32
DSA_TOPK_MAX = 256
DSA_BLOCK = 256
DSA_SUB = 64
DSA_HEAD_GROUPS = ((0, 1, 2, 3),)

SGU_GROUPS = 4
SGU_CHUNK = 128
SGU_CG = 64

REL_BUCKETS = 32
REL_MAX_DIST = 128

XA_HEADS = 4
XA_DH = 64

N_EXPERTS = 8

A_WIDTH = 896
B_WIDTH = 256
C_MAIN = 1024
C_AUX = 128
D_WIDTH = 512

INT_MIN = -(2 ** 31)
NEG_BIG = -1e30
VMEM_LIMIT = 56 * 1024 * 1024


def _cparams(sem, **kw):
    return pltpu.CompilerParams(dimension_semantics=sem, vmem_limit_bytes=VMEM_LIMIT, **kw)


def _dot(a, b):
    return jnp.dot(a, b, preferred_element_type=F32)


def _dot_nt(a, b):
    return lax.dot_general(a, b, (((1,), (1,)), ((), ())), preferred_element_type=F32)


def _dot_tn(a, b):
    return lax.dot_general(a, b, (((0,), (0,)), ((), ())), preferred_element_type=F32)


def _split_bf16(x, terms):
    parts = []
    for _ in range(terms):
        p = x.astype(BF16)
        parts.append(p)
        x = x - p.astype(F32)
    return parts


def _dot_exact_rhs(a, b, terms):
    out = None
    for p in _split_bf16(a, terms):
        d = _dot(p, b)
        out = d if out is None else out + d
    return out


def _dot_exact_lhs(a, b, terms):
    out = None
    for p in _split_bf16(b, terms):
        d = _dot(a, p)
        out = d if out is None else out + d
    return out


def _dot_3pass(a, b):
    a_hi, a_lo = _split_bf16(a, 2)
    b_hi, b_lo = _split_bf16(b, 2)
    return _dot(a_hi, b_hi) + (_dot(a_hi, b_lo) + _dot(a_lo, b_hi))


def _rms(x, g):
    return x * lax.rsqrt(jnp.mean(x * x, axis=-1, keepdims=True) + EPS) * g


def _norm_matmul_kernel(h_ref, g_ref, w_ref, *out_refs, widths):
    hb = _rms(h_ref[...], g_ref[...]).astype(BF16)
    off = 0
    for o_ref, wd in zip(out_refs, widths):
        o_ref[...] = _dot(hb, w_ref[:, off:off + wd]).astype(o_ref.dtype)
        off += wd


def norm_matmul(h, g, w, widths, dtypes=None, tm=512):
    dtypes = dtypes or (F32,) * len(widths)
    m, d = h.shape
    tm = min(tm, m)
    n = sum(widths)
    return pl.pallas_call(
        functools.partial(_norm_matmul_kernel, widths=widths),
        grid=(m // tm,),
        in_specs=[pl.BlockSpec((tm, d), lambda i: (i, 0)),
                  pl.BlockSpec((1, d), lambda i: (0, 0)),
                  pl.BlockSpec((d, n), lambda i: (0, 0))],
        out_specs=[pl.BlockSpec((tm, wd), lambda i: (i, 0)) for wd in widths],
        out_shape=[jax.ShapeDtypeStruct((m, wd), dt) for wd, dt in zip(widths, dtypes)],
        compiler_params=_cparams(("parallel",)),
        name="norm_matmul",
    )(h, g.reshape(1, d), w)


def _log_sigmoid(x):
    return jnp.minimum(x, 0.0) - jnp.log1p(jnp.exp(-jnp.abs(x)))


def _gla_kernel(z_ref, wa2_ref, ba_ref, ng_ref, o_ref, s_ref, *, group, n_groups):
    c = GLA_CHUNK
    grp = group * c
    hk = GLA_HEADS * GLA_DK
    hv = GLA_HEADS * GLA_DV
    s_ref[...] = jnp.zeros_like(s_ref)

    head_k = lax.broadcasted_iota(jnp.int32, (1, hk), 1) // GLA_DK
    head_v = lax.broadcasted_iota(jnp.int32, (1, hv), 1) // GLA_DV
    cum = min(grp, 256)
    g_row = lax.broadcasted_iota(jnp.int32, (cum, cum), 0)
    g_col = lax.broadcasted_iota(jnp.int32, (cum, cum), 1)
    tril = (((g_row // c) == (g_col // c)) & (g_col <= g_row)).astype(BF16)
    causal4 = (lax.broadcasted_iota(jnp.int32, (GLA_HEADS * c, c), 1)
               <= lax.broadcasted_iota(jnp.int32, (GLA_HEADS * c, c), 0) % c)
    state_mask = (lax.broadcasted_iota(jnp.int32, (hk, hv), 0) // GLA_DK
                  == lax.broadcasted_iota(jnp.int32, (hk, hv), 1) // GLA_DV)
    norm_mat = jnp.where(lax.broadcasted_iota(jnp.int32, (hv, hv), 0) // GLA_DV
                         == lax.broadcasted_iota(jnp.int32, (hv, hv), 1) // GLA_DV,
                         1.0 / GLA_DV, 0.0).astype(BF16)
    wa2 = wa2_ref[...]
    ba = ba_ref[...]
    ng = ng_ref[...]

    def body(n, carry):
        r0 = pl.multiple_of(n * grp, grp)
        z = z_ref[0, pl.ds(r0, grp), :]
        q, k, v, g, lr = z[:, 0:128], z[:, 128:256], z[:, 256:512], z[:, 512:768], z[:, 768:896]
        log_a = _log_sigmoid(_dot_3pass(lr, wa2) + ba) / GLA_TAU
        b = jnp.concatenate([_dot_exact_lhs(tril, log_a[r:r + cum], 3) for r in range(0, grp, cum)], axis=0)
        b_end = jnp.concatenate([jnp.broadcast_to(b[(ci + 1) * c - 1:(ci + 1) * c, :], (c, hk))
                                 for ci in range(group)], axis=0)
        q_t = q * (GLA_DK ** -0.5) * jnp.exp(b)
        q_tb = q_t.astype(BF16)
        k_t = (k * jnp.exp(-b)).astype(BF16)
        k_dec = (k * jnp.exp(b_end - b)).astype(BF16)
        vb = v.astype(BF16)
        outs = []
        for ci in range(group):
            rows = slice(ci * c, (ci + 1) * c)
            q4 = jnp.concatenate([jnp.where(head_k == h, q_t[rows], 0.0) for h in range(GLA_HEADS)],
                                 axis=0).astype(BF16)
            att = jnp.where(causal4, _dot_nt(q4, k_t[rows]), 0.0)
            r = _dot(att.astype(BF16), vb[rows])
            o = _dot(q_tb[rows], s_ref[...].astype(BF16))
            for h in range(GLA_HEADS):
                o = o + jnp.where(head_v == h, r[h * c:(h + 1) * c, :], 0.0)
            outs.append(o)
            kv = jnp.where(state_mask, _dot_tn(k_dec[rows], vb[rows]), 0.0)
            last = b_end[ci * c:ci * c + 1, :]
            dec = jnp.exp(jnp.transpose(jnp.broadcast_to(last, (hk, hk))))
            s_ref[...] = s_ref[...] * jnp.concatenate([dec, dec], axis=1) + kv
        o = jnp.concatenate(outs, axis=0)
        o = o * lax.rsqrt(_dot_exact_rhs(o * o, norm_mat, 2) + EPS) * ng
        o_ref[0, pl.ds(r0, grp), :] = o * (g * jax.nn.sigmoid(g))
        return carry

    lax.fori_loop(0, n_groups, body, 0)


def gla_mixer(z_a, wa2, ba, norm_g):
    bsz, t, _ = z_a.shape
    hk = GLA_HEADS * GLA_DK
    wa2p = jnp.zeros((128, hk), F32).at[:GLA_RANK].set(wa2)
    n_chunks = t // GLA_CHUNK
    group = math.gcd(GLA_GROUP, n_chunks)
    return pl.pallas_call(
        functools.partial(_gla_kernel, group=group, n_groups=n_chunks // group),
        grid=(bsz,),
        in_specs=[pl.BlockSpec((1, t, A_WIDTH), lambda b: (b, 0, 0)),
                  pl.BlockSpec((128, hk), lambda b: (0, 0)),
                  pl.BlockSpec((1, hk), lambda b: (0, 0)),
                  pl.BlockSpec((1, GROUP_WIDTH), lambda b: (0, 0))],
        out_specs=pl.BlockSpec((1, t, GROUP_WIDTH), lambda b: (b, 0, 0)),
        out_shape=jax.ShapeDtypeStruct((bsz, t, GROUP_WIDTH), F32),
        scratch_shapes=[pltpu.VMEM((hk, GROUP_WIDTH), F32)],
        compiler_params=_cparams(("parallel",)),
        name="gla_mixer",
    )(z_a, wa2p, ba.reshape(1, hk), norm_g.reshape(1, GROUP_WIDTH))


def _pool_kernel(u_ref, w_ref, b_ref, sc_ref, o_ref):
    u = u_ref[0]
    t, gw = u.shape
    row = lax.broadcasted_iota(jnp.int32, (t, gw), 0)
    grp = lax.broadcasted_iota(jnp.int32, (t, gw), 1) // POOL_CG

    def shifted(x, k):
        return jnp.where(row >= k, pltpu.roll(x, k, axis=0), 0.0)

    s = u
    p = jnp.zeros_like(u)
    for gi, win in enumerate(POOL_WINDOWS):
        half = win // 2
        s = s + shifted(s, half)
        cnt = jnp.minimum(row + 1, win).astype(F32)
        p = jnp.where(grp == gi, s / cnt - u, p)
    y = _dot(p.astype(BF16), w_ref[...]) + b_ref[...]
    o_ref[0] = y * sc_ref[...]


def pool_mixer(z_b, w, b, scale):
    assert POOL_WINDOWS == (2, 4, 8, 16)
    bsz, t, gw = z_b.shape
    w_bd = jnp.zeros((gw, gw), F32)
    for gi in range(len(POOL_WINDOWS)):
        w_bd = w_bd.at[gi * POOL_CG:(gi + 1) * POOL_CG, gi * POOL_CG:(gi + 1) * POOL_CG].set(w[gi])
    return pl.pallas_call(
        _pool_kernel,
        grid=(bsz,),
        in_specs=[pl.BlockSpec((1, t, gw), lambda i: (i, 0, 0)),
                  pl.BlockSpec((gw, gw), lambda i: (0, 0)),
                  pl.BlockSpec((1, gw), lambda i: (0, 0)),
                  pl.BlockSpec((1, gw), lambda i: (0, 0))],
        out_specs=pl.BlockSpec((1, t, gw), lambda i: (i, 0, 0)),
        out_shape=jax.ShapeDtypeStruct((bsz, t, gw), F32),
        compiler_params=_cparams(("parallel",)),
        name="pool_mixer",
    )(z_b, w_bd.astype(BF16), b.reshape(1, gw), scale.reshape(1, gw))


def _sgu_kernel(z_ref, lg_ref, lb_ref, w_ref, bm_ref, o_ref, *, chunks):
    c = SGU_CHUNK
    gw = GROUP_WIDTH
    rows = SGU_GROUPS * c
    tri = (lax.broadcasted_iota(jnp.int32, (rows, c), 1)
           <= lax.broadcasted_iota(jnp.int32, (rows, c), 0) % c)
    ws = jnp.where(tri, w_ref[...], 0.0).astype(BF16)
    grp = lax.broadcasted_iota(jnp.int32, (1, gw), 1) // SGU_CG
    for ci in range(chunks):
        z = jax.nn.gelu(z_ref[0, ci * c:(ci + 1) * c, :], approximate=True)
        u, v = z[:, :gw], z[:, gw:]
        mu = jnp.mean(v, axis=-1, keepdims=True)
        var = jnp.mean(jnp.square(v - mu), axis=-1, keepdims=True)
        vn = (v - mu) * lax.rsqrt(var + EPS) * lg_ref[...] + lb_ref[...]
        r = _dot(ws, vn.astype(BF16))
        mixed = bm_ref[...]
        for g in range(SGU_GROUPS):
            mixed = mixed + jnp.where(grp == g, r[g * c:(g + 1) * c, :], 0.0)
        o_ref[0, ci * c:(ci + 1) * c, :] = u * mixed


def sgu_mixer(z_d, ln_g, ln_b, w_s, b_s, chunks=4):
    bsz, t, _ = z_d.shape
    gw = GROUP_WIDTH
    tt = chunks * SGU_CHUNK
    bias = jnp.repeat(b_s.T, SGU_CG, axis=1)
    return pl.pallas_call(
        functools.partial(_sgu_kernel, chunks=chunks),
        grid=(bsz, t // tt),
        in_specs=[pl.BlockSpec((1, tt, 2 * gw), lambda b, i: (b, i, 0)),
                  pl.BlockSpec((1, gw), lambda b, i: (0, 0)),
                  pl.BlockSpec((1, gw), lambda b, i: (0, 0)),
                  pl.BlockSpec((SGU_GROUPS * SGU_CHUNK, SGU_CHUNK), lambda b, i: (0, 0)),
                  pl.BlockSpec((SGU_CHUNK, gw), lambda b, i: (0, 0))],
        out_specs=pl.BlockSpec((1, tt, gw), lambda b, i: (b, i, 0)),
        out_shape=jax.ShapeDtypeStruct((bsz, t, gw), F32),
        compiler_params=_cparams(("parallel", "parallel")),
        name="sgu_mixer",
    )(z_d, ln_g.reshape(1, gw), ln_b.reshape(1, gw),
      w_s.reshape(SGU_GROUPS * SGU_CHUNK, SGU_CHUNK), bias)


def _bucket_table():
    assert REL_MAX_DIST <= DSA_BLOCK + 1
    s = np.arange(DSA_BLOCK)[:, None]
    t = np.arange(DSA_BLOCK)[None, :]
    dist = np.stack([t - s, DSA_BLOCK + t - s, 2 * DSA_BLOCK + t - s])
    n = np.maximum(dist, 0)
    max_exact = REL_BUCKETS // 2
    nf = np.maximum(n, 1).astype(np.float32)
    large = max_exact + (np.log(nf / np.float32(max_exact)) / np.float32(math.log(REL_MAX_DIST / max_exact))
                         * np.float32(REL_BUCKETS - max_exact)).astype(np.int32)
    return np.where(n < max_exact, n, np.minimum(large, REL_BUCKETS - 1)).astype(np.int32)


def _bias_table_kernel(rb_ref, bucket_ref, o_ref):
    for back in range(3):
        bucket = bucket_ref[back]
        for h in range(DSA_HEADS):
            acc = jnp.zeros(bucket.shape, F32)
            for b in range(REL_BUCKETS):
                acc = jnp.where(bucket == b, rb_ref[b * DSA_HEADS + h], acc)
            o_ref[back, h] = acc


def rel_bias_tables(rel_bias):
    blk = DSA_BLOCK
    return pl.pallas_call(
        _bias_table_kernel,
        in_specs=[pl.BlockSpec(memory_space=pltpu.SMEM),
                  pl.BlockSpec((3, blk, blk), lambda: (0, 0, 0))],
        out_specs=pl.BlockSpec((3, DSA_HEADS, blk, blk), lambda: (0, 0, 0, 0)),
        out_shape=jax.ShapeDtypeStruct((3, DSA_HEADS, blk, blk), F32),
        name="rel_bias_tables",
    )(rel_bias.reshape(-1), jnp.asarray(_bucket_table()))


def _dsa_kernel(q_ref, k_ref, v_ref, qi_ref, kw_ref, qw_ref, toe_ref, o_ref,
                kpl_ref, vt_ref, keys_ref, khi_ref, klo_ref, acc_ref, am_ref, lg_ref, p_ref,
                *, topk, n_blocks, idx_bits):
    blk = DSA_BLOCK
    sub = DSA_SUB
    i = pl.program_id(1)
    hd = DSA_HEADS * DSA_DH
    heads_per_half = 128 // IDX_DIM

    @pl.when(i == 0)
    def _():
        lane = lax.broadcasted_iota(jnp.int32, (blk, 128), 1)

        def build(kb, c):
            r0 = pl.multiple_of(kb * blk, blk)
            ki = jnp.where(lane < IDX_DIM, kw_ref[0, pl.ds(r0, blk), :], 0.0)
            for j in range(heads_per_half):
                kpl_ref[kb, j] = (ki if j == 0 else pltpu.roll(ki, j * IDX_DIM, axis=1)).astype(BF16)
            vt_ref[kb] = jnp.transpose(v_ref[0, pl.ds(r0, blk), :].astype(F32)).astype(BF16)
            return c

        lax.fori_loop(0, n_blocks, build, 0)

    s_loc = lax.broadcasted_iota(jnp.int32, (blk, blk), 0)
    t_loc = lax.broadcasted_iota(jnp.int32, (blk, blk), 1)
    n_vis = i + 1

    qi_t = jnp.transpose(qi_ref[0].astype(F32)).astype(BF16)
    qi_halves = [qi_t[:128, :], qi_t[128:, :]]
    w_t = jnp.transpose(qw_ref[0])
    w_rows = [w_t[IDX_DIM + h:IDX_DIM + h + 1, :] * (IDX_HEADS ** -0.5) * (IDX_DIM ** -0.5)
              for h in range(IDX_HEADS)]
    s_sub = lax.broadcasted_iota(jnp.int32, (sub, blk), 0)
    t_sub = lax.broadcasted_iota(jnp.int32, (sub, blk), 1)

    def score_body(kb, c):
        for ci in range(blk // sub):
            rows = slice(ci * sub, (ci + 1) * sub)
            sc = jnp.zeros((sub, blk), F32)
            for half in range(2):
                for j in range(heads_per_half):
                    d = _dot(kpl_ref[kb, j, rows, :], qi_halves[half])
                    sc = sc + jnp.maximum(d, 0.0) * w_rows[half * heads_per_half + j]
            sc = jnp.where(sc == 0.0, 0.0, sc)
            bits = pltpu.bitcast(sc, jnp.int32)
            key = jnp.where(bits < 0, bits ^ jnp.int32(0x7FFFFFFF), bits)
            vis = (kb < i) | (s_sub + ci * sub <= t_sub)
            key = jnp.where(vis, key, jnp.int32(INT_MIN))
            keys_ref[kb, rows, :] = key
            khi_ref[kb, rows, :] = lax.shift_right_arithmetic(key, 16).astype(jnp.int16)
            klo_ref[kb, rows, :] = ((key & 0xFFFF) - 2 ** 15).astype(jnp.int16)
        return c

    lax.fori_loop(0, n_vis, score_body, 0)

    def count(pred):
        def body(kb, acc):
            hit = jnp.where(pred(keys_ref[kb], kb), 1.0, 0.0)
            return acc + jnp.sum(hit.reshape(blk // 32, 32, blk), axis=0)
        acc = lax.fori_loop(0, n_vis, body, jnp.zeros((32, blk), F32))
        return jnp.sum(acc, axis=0, keepdims=True)

    def count16(ref, cand):
        cand = cand.astype(jnp.int16)

        def body(kb, acc):
            hit = jnp.where(ref[kb] >= cand, jnp.int16(1), jnp.int16(0))
            for j in range(blk // 32):
                acc = acc + hit[j * 32:(j + 1) * 32]
            return acc
        acc = lax.fori_loop(0, n_vis, body, jnp.zeros((32, blk), jnp.int16))
        return jnp.sum(acc.astype(jnp.int32).astype(F32), axis=0, keepdims=True)

    def search16(ref, offset):
        lowest = jnp.full((1, blk), -(2 ** 15), jnp.int32)
        base = jnp.where(offset + count16(ref, jnp.zeros((1, blk), jnp.int32)) >= kf, 0, lowest)

        def bit_body(it, base):
            cand = base | lax.shift_left(jnp.int32(1), 14 - it)
            return jnp.where(offset + count16(ref, cand) >= kf, cand, base)
        return lax.fori_loop(0, 15, bit_body, base)

    kf = float(topk)
    zero = jnp.zeros((1, blk), jnp.int32)
    thr_hi = search16(khi_ref, 0.0)
    thr_hi16 = thr_hi.astype(jnp.int16)

    def low_body(kb, above):
        hi = khi_ref[kb]
        klo_ref[kb] = jnp.where(hi == thr_hi16, klo_ref[kb], jnp.int16(-(2 ** 15)))
        hit = jnp.where(hi > thr_hi16, jnp.int16(1), jnp.int16(0))
        for j in range(blk // 32):
            above = above + hit[j * 32:(j + 1) * 32]
        return above

    above = lax.fori_loop(0, n_vis, low_body, jnp.zeros((32, blk), jnp.int16))
    n_above = jnp.sum(above.astype(jnp.int32).astype(F32), axis=0, keepdims=True)
    thr_lo = search16(klo_ref, n_above)
    thr = lax.shift_left(thr_hi, 16) | (thr_lo + 2 ** 15)
    thr_sel = jnp.maximum(thr, jnp.int32(INT_MIN + 1))

    n_ge = count(lambda key, kb: key >= thr)
    excess = jnp.where((n_ge > kf) & (thr > jnp.int32(INT_MIN)), 1.0, 0.0)

    def tie_search():
        need = kf - count(lambda key, kb: key > thr)

        def tie_body(it, j):
            cand = j | lax.shift_left(jnp.int32(1), idx_bits - 1 - it)
            below = count(lambda key, kb: (key == thr) & (kb * blk + s_loc < cand))
            return jnp.where(below < need, cand, j)
        return lax.fori_loop(0, idx_bits, tie_body, zero)

    last = lax.cond(jnp.max(excess) > 0.0, tie_search, lambda: jnp.full((1, blk), 2 ** 30, jnp.int32))

    assert DSA_DH ** -0.5 == 0.125
    q_t = jnp.transpose(q_ref[0].astype(F32) * (DSA_DH ** -0.5))
    row_h = lax.broadcasted_iota(jnp.int32, (hd, 1), 0) // DSA_DH
    q_heads = [jnp.where(row_h == h, q_t, 0.0).astype(BF16) for h in range(DSA_HEADS)]
    acc_ref[...] = jnp.zeros_like(acc_ref)
    n_sub = blk // sub

    def att_body(kb, carry):
        ms, ls = carry
        back = jnp.minimum(i - kb, 2)
        for ci in range(n_sub):
            rows = slice(ci * sub, (ci + 1) * sub)
            key = keys_ref[kb, rows, :]
            sel = (key > thr_sel) | ((key == thr_sel) & (kb * blk + ci * sub + s_sub <= last))
            am_ref[rows, :] = jnp.where(sel, 0.0, NEG_BIG)
        new_ms, new_ls = list(ms), list(ls)
        for heads in DSA_HEAD_GROUPS:
            alphas = {}
            for h in heads:
                pm = jnp.full((8, blk), NEG_BIG, F32)
                for ci in range(n_sub):
                    rows = slice(ci * sub, (ci + 1) * sub)
                    k_rows = k_ref[0, pl.ds(pl.multiple_of(kb * blk + ci * sub, sub), sub), :]
                    lg = _dot(k_rows, q_heads[h]) + toe_ref[back, h, rows, :] + am_ref[rows, :]
                    lg_ref[h, rows, :] = lg
                    pm = jnp.maximum(pm, jnp.max(lg.reshape(sub // 8, 8, blk), axis=0))
                new_ms[h] = jnp.maximum(ms[h], jnp.max(pm, axis=0, keepdims=True))
                alphas[h] = jnp.exp(ms[h] - new_ms[h])
            for h in heads:
                ps = jnp.zeros((8, blk), F32)
                for ci in range(n_sub):
                    rows = slice(ci * sub, (ci + 1) * sub)
                    p = jnp.exp(lg_ref[h, rows, :] - new_ms[h])
                    ps = ps + jnp.sum(p.reshape(sub // 8, 8, blk), axis=0)
                    p_ref[h, rows, :] = p.astype(BF16)
                new_ls[h] = ls[h] * alphas[h] + jnp.sum(ps, axis=0, keepdims=True)
            for h in heads:
                hrows = slice(h * DSA_DH, (h + 1) * DSA_DH)
                acc_ref[hrows, :] = acc_ref[hrows, :] * alphas[h] + _dot(vt_ref[kb, hrows, :], p_ref[h])
        return tuple(new_ms), tuple(new_ls)

    init = (tuple(jnp.full((1, blk), 0.01 * NEG_BIG, F32) for _ in range(DSA_HEADS)),
            tuple(jnp.zeros((1, blk), F32) for _ in range(DSA_HEADS)))
    _, ls = lax.fori_loop(0, n_vis, att_body, init)
    for h in range(DSA_HEADS):
        rows = slice(h * DSA_DH, (h + 1) * DSA_DH)
        acc_ref[rows, :] = acc_ref[rows, :] * (1.0 / ls[h])
    o_ref[0] = jnp.transpose(acc_ref[...])


def dsa_mixer(z_c, z_aux, toe):
    bsz, t, _ = z_c.shape
    blk = DSA_BLOCK
    n_blocks = t // blk
    assert t % blk == 0
    topk = min(DSA_TOPK_MAX, t // 4)
    hd = DSA_HEADS * DSA_DH
    kernel = functools.partial(_dsa_kernel, topk=topk, n_blocks=n_blocks,
                               idx_bits=max(1, (t - 1).bit_length()))
    return pl.pallas_call(
        kernel,
        grid=(bsz, n_blocks),
        in_specs=[pl.BlockSpec((1, blk, hd), lambda b, i: (b, i, 0)),
                  pl.BlockSpec((1, t, hd), lambda b, i: (b, 0, 1)),
                  pl.BlockSpec((1, t, hd), lambda b, i: (b, 0, 2)),
                  pl.BlockSpec((1, blk, hd), lambda b, i: (b, i, 3)),
                  pl.BlockSpec((1, t, C_AUX), lambda b, i: (b, 0, 0)),
                  pl.BlockSpec((1, blk, C_AUX), lambda b, i: (b, i, 0)),
                  pl.BlockSpec((3, DSA_HEADS, blk, blk), lambda b, i: (0, 0, 0, 0))],
        out_specs=pl.BlockSpec((1, blk, hd), lambda b, i: (b, i, 0)),
        out_shape=jax.ShapeDtypeStruct((bsz, t, hd), F32),
        scratch_shapes=[pltpu.VMEM((n_blocks, 128 // IDX_DIM, blk, 128), BF16),
                        pltpu.VMEM((n_blocks, hd, blk), BF16),
                        pltpu.VMEM((n_blocks, blk, blk), jnp.int32),
                        pltpu.VMEM((n_blocks, blk, blk), jnp.int16),
                        pltpu.VMEM((n_blocks, blk, blk), jnp.int16),
                        pltpu.VMEM((hd, blk), F32),
                        pltpu.VMEM((blk, blk), F32),
                        pltpu.VMEM((DSA_HEADS, blk, blk), F32),
                        pltpu.VMEM((DSA_HEADS, blk, blk), BF16)],
        compiler_params=_cparams(("parallel", "arbitrary")),
        name="dsa_mixer",
    )(z_c, z_c, z_c, z_c, z_aux, z_aux, toe)


def _mixer_tail_kernel(h_ref, a_ref, b_ref, c_ref, d_ref, wout_ref, gxa_ref, wq_ref, k_ref, v_ref, wo_ref,
                       *refs, route):
    gw = GROUP_WIDTH
    x = h_ref[0]
    for gi, r in enumerate((a_ref, b_ref, c_ref, d_ref)):
        x = x + _dot(r[0].astype(BF16), wout_ref[gi * gw:(gi + 1) * gw, :])

    hd = XA_HEADS * XA_DH
    q = _dot(_rms(x, gxa_ref[...]).astype(BF16), wq_ref[...]).astype(BF16)
    k = k_ref[0]
    v = v_ref[0]
    lane_h = lax.broadcasted_iota(jnp.int32, (1, hd), 1) // XA_DH
    o = jnp.zeros((x.shape[0], hd), F32)
    for h in range(XA_HEADS):
        s = _dot_nt(q, jnp.where(lane_h == h, k, 0.0).astype(BF16)) * (XA_DH ** -0.5)
        p = jnp.exp(s - jnp.max(s, axis=-1, keepdims=True))
        p = p / jnp.sum(p, axis=-1, keepdims=True)
        o = o + _dot(p.astype(BF16), jnp.where(lane_h == h, v, 0.0).astype(BF16))
    x = x + _dot(o.astype(BF16), wo_ref[...])

    if not route:
        o_ref, = refs
        o_ref[0] = x
        return
    gffn_ref, wr_ref, o_ref, hn_ref, gate_ref, cnt_ref = refs
    o_ref[0] = x
    hn = _rms(x, gffn_ref[...])
    hn_ref[0] = hn.astype(BF16)
    tm = hn.shape[0]
    lane = lax.broadcasted_iota(jnp.int32, (tm, GATE_LANES), 1)
    logits = jnp.where(lane < N_EXPERTS, _dot_3pass(hn, wr_ref[...]), -jnp.inf)
    m1 = jnp.max(logits, axis=-1, keepdims=True)
    i1 = jnp.min(jnp.where(logits == m1, lane, GATE_LANES), axis=-1, keepdims=True)
    rest = jnp.where(lane == i1, -jnp.inf, logits)
    m2 = jnp.max(rest, axis=-1, keepdims=True)
    i2 = jnp.min(jnp.where(rest == m2, lane, GATE_LANES), axis=-1, keepdims=True)
    e2 = jnp.exp(m2 - m1)
    g1 = 1.0 / (1.0 + e2)
    gates = jnp.where(lane == i1, g1, 0.0) + jnp.where(lane == i2, e2 * g1, 0.0)
    gates = jnp.where(lane == N_EXPERTS, i1.astype(F32), gates)
    gate_ref[0] = jnp.where(lane == N_EXPERTS + 1, i2.astype(F32), gates)
    sel = jnp.where((lane == i1) | (lane == i2), 1.0, 0.0)
    for c in range(tm // MOE_CHUNK):
        cnt_ref[c] = jnp.sum(sel[c * MOE_CHUNK:(c + 1) * MOE_CHUNK, :], axis=0, keepdims=True)


def mixer_tail(h3, mixers, w_out, g_xa, wq, k, v, wo, routing=None, tm=1024):
    bsz, t, d = h3.shape
    tm = min(tm, t)
    mlen = k.shape[1]
    hd = XA_HEADS * XA_DH
    gw = GROUP_WIDTH
    row = lambda b, i: (b, i, 0)
    fixed2 = lambda b, i: (0, 0)
    in_specs = ([pl.BlockSpec((1, tm, d), row)] + [pl.BlockSpec((1, tm, gw), row)] * 4
                + [pl.BlockSpec((4 * gw, d), fixed2), pl.BlockSpec((1, d), fixed2), pl.BlockSpec((d, hd), fixed2),
                   pl.BlockSpec((1, mlen, hd), lambda b, i: (b, 0, 0)),
                   pl.BlockSpec((1, mlen, hd), lambda b, i: (b, 0, 0)),
                   pl.BlockSpec((hd, d), fixed2)])
    args = [h3, *mixers, w_out, g_xa.reshape(1, d), wq, k, v, wo]
    out_specs = [pl.BlockSpec((1, tm, d), row)]
    out_shape = [jax.ShapeDtypeStruct((bsz, t, d), F32)]
    if routing is not None:
        assert tm % MOE_CHUNK == 0
        g_ffn, wr = routing
        in_specs += [pl.BlockSpec((1, d), fixed2), pl.BlockSpec((d, GATE_LANES), fixed2)]
        args += [g_ffn.reshape(1, d), jnp.zeros((d, GATE_LANES), F32).at[:, :N_EXPERTS].set(wr)]
        per_step = tm // MOE_CHUNK
        out_specs += [pl.BlockSpec((1, tm, d), row), pl.BlockSpec((1, tm, GATE_LANES), row),
                      pl.BlockSpec((per_step, 1, GATE_LANES), lambda b, i: (b * (t // tm) + i, 0, 0))]
        out_shape += [jax.ShapeDtypeStruct((bsz, t, d), BF16), jax.ShapeDtypeStruct((bsz, t, GATE_LANES), F32),
                      jax.ShapeDtypeStruct((bsz * t // MOE_CHUNK, 1, GATE_LANES), F32)]
    return pl.pallas_call(
        functools.partial(_mixer_tail_kernel, route=routing is not None),
        grid=(bsz, t // tm),
        in_specs=in_specs,
        out_specs=out_specs,
        out_shape=out_shape,
        compiler_params=_cparams(("parallel", "parallel")),
        name="mixer_tail",
    )(*args)


def _ffn_kernel(h_ref, g_ref, w1_ref, w3_ref, w2_ref, o_ref, *, tf):
    x = h_ref[...]
    hn = _rms(x, g_ref[...]).astype(BF16)
    acc = x
    for f0 in range(0, w1_ref.shape[1], tf):
        a = _dot(hn, w1_ref[:, f0:f0 + tf])
        b = _dot(hn, w3_ref[:, f0:f0 + tf])
        acc = acc + _dot((a * jax.nn.sigmoid(a) * b).astype(BF16), w2_ref[f0:f0 + tf, :])
    o_ref[...] = acc


def ffn(h, g, w1, w3, w2, tm=512, tf=704):
    m, d = h.shape
    tm = min(tm, m)
    nf = w1.shape[1]
    assert nf % tf == 0
    resident = lambda shape: pl.BlockSpec(shape, lambda i: (0, 0), pipeline_mode=pl.Buffered(1))
    return pl.pallas_call(
        functools.partial(_ffn_kernel, tf=tf),
        grid=(m // tm,),
        in_specs=[pl.BlockSpec((tm, d), lambda i: (i, 0)),
                  pl.BlockSpec((1, d), lambda i: (0, 0)),
                  resident((d, nf)), resident((d, nf)), resident((nf, d))],
        out_specs=pl.BlockSpec((tm, d), lambda i: (i, 0)),
        out_shape=jax.ShapeDtypeStruct((m, d), F32),
        compiler_params=_cparams(("parallel",)),
        name="ffn",
    )(h, g.reshape(1, d), w1, w3, w2)


GATE_LANES = 128
MOE_CHUNK = 256
MOE_ROWS = 128
MOE_TOKENS = 2048
MOE_ALIGN = 16


def _moe_kernel(cb_ref, h_ref, hn_ref, gate_ref, w1_ref, w3_ref, w2_ref, *refs, final):
    if final:
        gfin_ref, *refs = refs
    y_ref, rank_row, rk_ref, gs_ref, xs_ref, yacc_ref, rc_ref, gc_ref = refs
    t = pl.program_id(0)
    e = pl.program_id(1)
    f = pl.program_id(2)
    ts, d = hn_ref.shape
    ch, rb = MOE_CHUNK, MOE_ROWS
    win = 2 * rb
    n_ch = ts // ch
    lane = lax.broadcasted_iota(jnp.int32, (1, GATE_LANES), 1)

    def before(c):
        return cb_ref[(t * (n_ch + 1) + c) * N_EXPERTS + e]

    n_blocks = (before(n_ch) + rb - 1) // rb
    n_windows = (n_blocks * rb + win - 1) // win + 1

    def windows(c):
        lo, hi = before(c), before(c + 1)
        s0 = (lo // MOE_ALIGN) * MOE_ALIGN
        return s0, jnp.where(hi > lo, (hi - s0 + win - 1) // win, 0)

    spans = [windows(c) for c in range(n_ch)]
    single = functools.reduce(jnp.logical_and, [n_win <= 1 for _, n_win in spans])

    def for_each_window(visit):
        @pl.when(single)
        def _():
            for c, (s0, _) in enumerate(spans):
                visit(c, s0)

        @pl.when(jnp.logical_not(single))
        def _():
            for c, (s0, n_win) in enumerate(spans):
                def body(j, carry, c=c, s0=s0):
                    visit(c, s0 + j * win)
                    return carry
                lax.fori_loop(0, n_win, body, 0)

    @pl.when((e == 0) & (f == 0))
    def _():
        y_ref[...] = h_ref[...]
        strict_lower = (lax.broadcasted_iota(jnp.int32, (ch, ch), 1)
                        < lax.broadcasted_iota(jnp.int32, (ch, ch), 0)).astype(BF16)
        lane_f = lane.astype(F32)
        offs = jnp.ones((1, GATE_LANES), F32)
        for c in range(n_ch):
            rows = slice(c * ch, (c + 1) * ch)
            g = gate_ref[rows, :]
            sel = jnp.where((lane_f == g[:, N_EXPERTS:N_EXPERTS + 1])
                            | (lane_f == g[:, N_EXPERTS + 1:N_EXPERTS + 2]), 1.0, 0.0)
            r = _dot(strict_lower, sel.astype(BF16)) + offs
            r = jnp.where(sel > 0.0, r, 0.0)
            rank_row[:, rows] = jnp.transpose(r) - 1.0
            high = jnp.floor(r * (1.0 / 256.0))
            rk_ref[0, rows, :] = high.astype(BF16)
            rk_ref[1, rows, :] = (r - 256.0 * high).astype(BF16)
            for j, part in enumerate(_split_bf16(g, 3)):
                gs_ref[j, rows, :] = part
            offs = offs + jnp.sum(sel, axis=0, keepdims=True)

    @pl.when(f == 0)
    def _():
        pick = (lax.broadcasted_iota(jnp.int32, (GATE_LANES, GATE_LANES), 0) == e).astype(BF16)
        for c in range(n_ch):
            rows = slice(c * ch, (c + 1) * ch)
            rc_ref[rows, :] = 256.0 * _dot(rk_ref[0, rows, :], pick) + _dot(rk_ref[1, rows, :], pick) - 1.0
            gc_ref[rows, :] = (_dot(gs_ref[0, rows, :], pick) + _dot(gs_ref[1, rows, :], pick)
                               + _dot(gs_ref[2, rows, :], pick))

        def clear(j, carry):
            r0 = pl.multiple_of(j * win, win)
            xs_ref[pl.ds(r0, win), :] = jnp.zeros((win, d), BF16)
            yacc_ref[pl.ds(r0, win), :] = jnp.zeros((win, d), F32)
            return carry

        lax.fori_loop(0, n_windows, clear, 0)

        row_id = lax.broadcasted_iota(jnp.int32, (win, ch), 0)

        def gather_window(c, s):
            s = pl.multiple_of(s, MOE_ALIGN)
            ranks = rank_row[pl.ds(e, 1), c * ch:(c + 1) * ch]
            onehot = jnp.where(ranks == (s + row_id).astype(F32), 1.0, 0.0).astype(BF16)
            xs_ref[pl.ds(s, win), :] += _dot(onehot, hn_ref[c * ch:(c + 1) * ch, :]).astype(BF16)

        for_each_window(gather_window)

    def ffn_rows(r0, rows):
        x = xs_ref[pl.ds(r0, rows), :]
        a = _dot(x, w1_ref[0])
        g3 = _dot(x, w3_ref[0])
        yacc_ref[pl.ds(r0, rows), :] += _dot((a * jax.nn.sigmoid(a) * g3).astype(BF16), w2_ref[0])

    n_quads = n_blocks // 4

    def ffn_quad(j, carry):
        ffn_rows(pl.multiple_of(j * (4 * rb), 4 * rb), 4 * rb)
        return carry

    lax.fori_loop(0, n_quads, ffn_quad, 0)
    tail = pl.multiple_of(n_quads * (4 * rb), 4 * rb)

    @pl.when((n_blocks & 2) != 0)
    def _():
        ffn_rows(tail, 2 * rb)

    @pl.when((n_blocks & 1) != 0)
    def _():
        ffn_rows(pl.multiple_of(tail + (n_blocks & 2) * rb, rb), rb)

    @pl.when(f == pl.num_programs(2) - 1)
    def _():
        def to_bf16(j, carry):
            r0 = pl.multiple_of(j * win, win)
            xs_ref[pl.ds(r0, win), :] = yacc_ref[pl.ds(r0, win), :].astype(BF16)
            return carry

        lax.fori_loop(0, n_windows, to_bf16, 0)

        lane_id = lax.broadcasted_iota(jnp.int32, (ch, rb), 1).astype(F32)

        def scatter_window(c, s):
            rows = slice(c * ch, (c + 1) * ch)
            s = pl.multiple_of(s, MOE_ALIGN)
            rank = rc_ref[rows, :] - s.astype(F32)
            onehot = jnp.concatenate([jnp.where(rank == lane_id, 1.0, 0.0),
                                      jnp.where(rank == lane_id + float(rb), 1.0, 0.0)],
                                     axis=1).astype(BF16)
            gate = jnp.concatenate([gc_ref[rows, :]] * (d // GATE_LANES), axis=1)
            y_ref[rows, :] += gate * _dot(onehot, xs_ref[pl.ds(s, win), :])

        for_each_window(scatter_window)

        if final:
            @pl.when(e == pl.num_programs(1) - 1)
            def _():
                y_ref[...] = _rms(y_ref[...], gfin_ref[...])


def moe(h, hn, gates, cnt, w1, w3, w2, final_gain=None, tf=896):
    m, d = hn.shape
    ts = min(MOE_TOKENS, m)
    n_exp, _, nf = w1.shape
    assert nf % tf == 0 and m % ts == 0 and ts % MOE_CHUNK == 0 and MOE_ROWS == GATE_LANES
    n_ch = ts // MOE_CHUNK
    counts = cnt.reshape(m // ts, n_ch, GATE_LANES)[:, :, :n_exp].astype(jnp.int32)
    before = jnp.concatenate([jnp.zeros((m // ts, 1, n_exp), jnp.int32), jnp.cumsum(counts, axis=1)], axis=1)
    per_tile = lambda width: pl.BlockSpec((ts, width), lambda t, e, f, cb: (t, 0), pipeline_mode=pl.Buffered(1))
    in_specs = [per_tile(d), per_tile(d), per_tile(GATE_LANES),
                pl.BlockSpec((1, d, tf), lambda t, e, f, cb: (e, 0, f)),
                pl.BlockSpec((1, d, tf), lambda t, e, f, cb: (e, 0, f)),
                pl.BlockSpec((1, tf, d), lambda t, e, f, cb: (e, f, 0))]
    args = [before.reshape(-1), h, hn, gates, w1, w3, w2]
    if final_gain is not None:
        in_specs.append(pl.BlockSpec((1, d), lambda t, e, f, cb: (0, 0)))
        args.append(final_gain.reshape(1, d))
    grid_spec = pltpu.PrefetchScalarGridSpec(
        num_scalar_prefetch=1,
        grid=(m // ts, n_exp, nf // tf),
        in_specs=in_specs,
        out_specs=per_tile(d),
        scratch_shapes=[pltpu.VMEM((GATE_LANES, ts), F32),
                        pltpu.VMEM((2, ts, GATE_LANES), BF16),
                        pltpu.VMEM((3, ts, GATE_LANES), BF16),
                        pltpu.VMEM((ts + 2 * MOE_ROWS, d), BF16),
                        pltpu.VMEM((ts + 2 * MOE_ROWS, d), F32),
                        pltpu.VMEM((ts, MOE_ROWS), F32),
                        pltpu.VMEM((ts, GATE_LANES), F32)])
    return pl.pallas_call(
        functools.partial(_moe_kernel, final=final_gain is not None),
        grid_spec=grid_spec,
        out_shape=jax.ShapeDtypeStruct((m, d), F32),
        compiler_params=_cparams(("parallel", "arbitrary", "arbitrary"),
                                 allow_input_fusion=[i in (4, 5, 6) for i in range(len(args))]),
        name="moe",
    )(*args)


def _final_norm_kernel(g_ref, x_ref, o_ref):
    o_ref[...] = _rms(x_ref[...], g_ref[...])


def final_rmsnorm(g, x, tm=1024):
    m, d = x.shape
    tm = min(tm, m)
    return pl.pallas_call(
        _final_norm_kernel,
        grid=(m // tm,),
        in_specs=[pl.BlockSpec((1, d), lambda i: (0, 0)), pl.BlockSpec((tm, d), lambda i: (i, 0))],
        out_specs=pl.BlockSpec((tm, d), lambda i: (i, 0)),
        out_shape=jax.ShapeDtypeStruct((m, d), F32),
        compiler_params=_cparams(("parallel",)),
        name="final_norm",
    )(g.reshape(1, d), x)


def _pad_cols(w, width):
    return jnp.pad(w, ((0, 0), (0, width - w.shape[1])))


def _in_proj_weight(w_in):
    a_end = 2 * GLA_HEADS * GLA_DK + 2 * GROUP_WIDTH + GLA_RANK
    b_end = a_end + GROUP_WIDTH
    c_end = b_end + 3 * GROUP_WIDTH + IDX_HEADS * IDX_DIM + IDX_DIM + IDX_HEADS
    assert w_in.shape[1] == c_end + 2 * GROUP_WIDTH
    return jnp.concatenate([_pad_cols(w_in[:, :a_end], A_WIDTH), w_in[:, a_end:b_end],
                            _pad_cols(w_in[:, b_end:c_end], C_MAIN + C_AUX), w_in[:, c_end:]], axis=1).astype(BF16)


def hybrid_layer(h, mem2, toe, p, bsz, t, routing=None):
    m, d = h.shape
    z_a, z_b, z_c, z_aux, z_d = norm_matmul(h, p["norm_mix"], _in_proj_weight(p["w_in"]),
                                            (A_WIDTH, B_WIDTH, C_MAIN, C_AUX, D_WIDTH),
                                            (F32, F32, BF16, F32, F32))
    o_a = gla_mixer(z_a.reshape(bsz, t, A_WIDTH), p["gla_wa2"], p["gla_ba"], p["gla_norm"])
    o_b = pool_mixer(z_b.reshape(bsz, t, B_WIDTH), p["pool_w"], p["pool_b"].reshape(-1), p["pool_scale"])
    o_c = dsa_mixer(z_c.reshape(bsz, t, C_MAIN), z_aux.reshape(bsz, t, C_AUX), toe)
    o_d = sgu_mixer(z_d.reshape(bsz, t, D_WIDTH), p["sgu_ln_g"], p["sgu_ln_b"], p["sgu_w"], p["sgu_b"])
    hd = XA_HEADS * XA_DH
    k, v = norm_matmul(mem2, p["norm_mem"], p["xa_wkv"].astype(BF16), (hd, hd))
    mlen = mem2.shape[0] // bsz
    outs = mixer_tail(h.reshape(bsz, t, d), (o_a, o_b, o_c, o_d), p["w_out"].astype(BF16), p["norm_xa"],
                      p["xa_wq"].astype(BF16), k.reshape(bsz, mlen, hd), v.reshape(bsz, mlen, hd),
                      p["xa_wo"].astype(BF16), routing)
    if routing is None:
        return (outs[0].reshape(m, d),)
    h3, hn, gates, counts = outs
    return h3.reshape(m, d), hn.reshape(m, d), gates.reshape(m, GATE_LANES), counts


def kernel(x, mem, rel_bias, final_norm, norm_mix, w_in, gla_wa2, gla_ba, gla_norm, pool_w, pool_b,
           pool_scale, sgu_ln_g, sgu_ln_b, sgu_w, sgu_b, w_out, norm_xa, norm_mem, xa_wq, xa_wkv, xa_wo,
           norm_ffn, ffn_w1, ffn_w3, ffn_w2, router, moe_w1, moe_w3, moe_w2):
    bsz, t, d = x.shape
    depth = norm_mix.shape[0]
    h = x.reshape(bsz * t, d)
    mem2 = mem.reshape(-1, d)
    toe = rel_bias_tables(rel_bias)
    normed = False
    for i in range(depth):
        p = dict(norm_mix=norm_mix[i], w_in=w_in[i], gla_wa2=gla_wa2[i], gla_ba=gla_ba[i],
                 gla_norm=gla_norm[i], pool_w=pool_w[i], pool_b=pool_b[i], pool_scale=pool_scale[i],
                 sgu_ln_g=sgu_ln_g[i], sgu_ln_b=sgu_ln_b[i], sgu_w=sgu_w[i], sgu_b=sgu_b[i],
                 w_out=w_out[i], norm_xa=norm_xa[i], norm_mem=norm_mem[i], xa_wq=xa_wq[i],
                 xa_wkv=xa_wkv[i], xa_wo=xa_wo[i])
        j = i // 2
        if i % 2 == 0:
            h, = hybrid_layer(h, mem2, toe, p, bsz, t)
            h = ffn(h, norm_ffn[i], ffn_w1[j].astype(BF16), ffn_w3[j].astype(BF16), ffn_w2[j].astype(BF16))
        else:
            h, hn, gates, counts = hybrid_layer(h, mem2, toe, p, bsz, t, routing=(norm_ffn[i], router[j]))
            normed = i == depth - 1
            h = moe(h, hn, gates, counts, moe_w1[j].astype(BF16), moe_w3[j].astype(BF16),
                    moe_w2[j].astype(BF16), final_gain=final_norm if normed else None)
    if not normed:
        h = final_rmsnorm(final_norm, h)
    return h.reshape(bsz, t, d)
```

```python
import functools
import math

import jax
import jax.numpy as jnp
import numpy as np
from jax import lax
from jax.experimental import pallas as pl
from jax.experimental.pallas import tpu as pltpu

F32 = jnp.float32
BF16 = jnp.bfloat16
EPS = 1e-6

GROUP_WIDTH = 256

GLA_HEADS = 4
GLA_DV = 64
GLA_DK = 32
GLA_RANK = 16
GLA_TAU = 16.0
GLA_CHUNK = 64
GLA_GROUP = 16

POOL_WINDOWS = (2, 4, 8, 16)
POOL_CG = 64

DSA_HEADS = 4
DSA_DH = 64
IDX_HEADS = 8
IDX_DIM = 32
DSA_TOPK_MAX = 256
DSA_BLOCK = 256
DSA_SUB = 64
DSA_HEAD_GROUPS = ((0, 1, 2, 3),)

SGU_GROUPS = 4
SGU_CHUNK = 128
SGU_CG = 64

REL_BUCKETS = 32
REL_MAX_DIST = 128

XA_HEADS = 4
XA_DH = 64

N_EXPERTS = 8

A_WIDTH = 896
B_WIDTH = 256
C_MAIN = 1024
C_AUX = 128
D_WIDTH = 512

BF16_TILE_ROWS = 16
INT_MIN = -(2 ** 31)
NEG_BIG = -1e30
VMEM_LIMIT = 56 * 1024 * 1024


def _cparams(sem):
    return pltpu.CompilerParams(dimension_semantics=sem, vmem_limit_bytes=VMEM_LIMIT)


def _dot(a, b):
    return jnp.dot(a, b, preferred_element_type=F32)


def _dot_nt(a, b):
    return lax.dot_general(a, b, (((1,), (1,)), ((), ())), preferred_element_type=F32)


def _dot_tn(a, b):
    return lax.dot_general(a, b, (((0,), (0,)), ((), ())), preferred_element_type=F32)


def _split_bf16(x, terms):
    parts = []
    for _ in range(terms):
        p = x.astype(BF16)
        parts.append(p)
        x = x - p.astype(F32)
    return parts


def _dot_exact_rhs(a, b, terms):
    out = None
    for p in _split_bf16(a, terms):
        d = _dot(p, b)
        out = d if out is None else out + d
    return out


def _dot_exact_lhs(a, b, terms):
    out = None
    for p in _split_bf16(b, terms):
        d = _dot(a, p)
        out = d if out is None else out + d
    return out


def _dot_3pass(a, b):
    a_hi, a_lo = _split_bf16(a, 2)
    b_hi, b_lo = _split_bf16(b, 2)
    return _dot(a_hi, b_hi) + (_dot(a_hi, b_lo) + _dot(a_lo, b_hi))


def _rms(x, g):
    return x * lax.rsqrt(jnp.mean(x * x, axis=-1, keepdims=True) + EPS) * g


def _norm_matmul_kernel(h_ref, g_ref, w_ref, *out_refs, widths):
    hb = _rms(h_ref[...], g_ref[...]).astype(BF16)
    off = 0
    for o_ref, wd in zip(out_refs, widths):
        o_ref[...] = _dot(hb, w_ref[:, off:off + wd]).astype(o_ref.dtype)
        off += wd


def norm_matmul(h, g, w, widths, dtypes=None, tm=512):
    dtypes = dtypes or (F32,) * len(widths)
    m, d = h.shape
    tm = min(tm, m)
    n = sum(widths)
    return pl.pallas_call(
        functools.partial(_norm_matmul_kernel, widths=widths),
        grid=(m // tm,),
        in_specs=[pl.BlockSpec((tm, d), lambda i: (i, 0)),
                  pl.BlockSpec((1, d), lambda i: (0, 0)),
                  pl.BlockSpec((d, n), lambda i: (0, 0))],
        out_specs=[pl.BlockSpec((tm, wd), lambda i: (i, 0)) for wd in widths],
        out_shape=[jax.ShapeDtypeStruct((m, wd), dt) for wd, dt in zip(widths, dtypes)],
        compiler_params=_cparams(("parallel",)),
        name="norm_matmul",
    )(h, g.reshape(1, d), w)


def _log_sigmoid(x):
    return jnp.minimum(x, 0.0) - jnp.log1p(jnp.exp(-jnp.abs(x)))


def _gla_kernel(z_ref, wa2_ref, ba_ref, ng_ref, o_ref, s_ref, *, group, n_groups):
    c = GLA_CHUNK
    grp = group * c
    hk = GLA_HEADS * GLA_DK
    hv = GLA_HEADS * GLA_DV
    s_ref[...] = jnp.zeros_like(s_ref)

    head_k = lax.broadcasted_iota(jnp.int32, (1, hk), 1) // GLA_DK
    head_v = lax.broadcasted_iota(jnp.int32, (1, hv), 1) // GLA_DV
    cum = min(grp, 256)
    g_row = lax.broadcasted_iota(jnp.int32, (cum, cum), 0)
    g_col = lax.broadcasted_iota(jnp.int32, (cum, cum), 1)
    tril = (((g_row // c) == (g_col // c)) & (g_col <= g_row)).astype(BF16)
    causal4 = (lax.broadcasted_iota(jnp.int32, (GLA_HEADS * c, c), 1)
               <= lax.broadcasted_iota(jnp.int32, (GLA_HEADS * c, c), 0) % c)
    state_mask = (lax.broadcasted_iota(jnp.int32, (hk, hv), 0) // GLA_DK
                  == lax.broadcasted_iota(jnp.int32, (hk, hv), 1) // GLA_DV)
    norm_mat = jnp.where(lax.broadcasted_iota(jnp.int32, (hv, hv), 0) // GLA_DV
                         == lax.broadcasted_iota(jnp.int32, (hv, hv), 1) // GLA_DV,
                         1.0 / GLA_DV, 0.0).astype(BF16)
    wa2 = wa2_ref[...]
    ba = ba_ref[...]
    ng = ng_ref[...]

    def body(n, carry):
        r0 = pl.multiple_of(n * grp, grp)
        z = z_ref[0, pl.ds(r0, grp), :]
        q, k, v, g, lr = z[:, 0:128], z[:, 128:256], z[:, 256:512], z[:, 512:768], z[:, 768:896]
        log_a = _log_sigmoid(_dot_3pass(lr, wa2) + ba) / GLA_TAU
        b = jnp.concatenate([_dot_exact_lhs(tril, log_a[r:r + cum], 3) for r in range(0, grp, cum)], axis=0)
        b_end = jnp.concatenate([jnp.broadcast_to(b[(ci + 1) * c - 1:(ci + 1) * c, :], (c, hk))
                                 for ci in range(group)], axis=0)
        q_t = q * (GLA_DK ** -0.5) * jnp.exp(b)
        q_tb = q_t.astype(BF16)
        k_t = (k * jnp.exp(-b)).astype(BF16)
        k_dec = (k * jnp.exp(b_end - b)).astype(BF16)
        vb = v.astype(BF16)
        outs = []
        for ci in range(group):
            rows = slice(ci * c, (ci + 1) * c)
            q4 = jnp.concatenate([jnp.where(head_k == h, q_t[rows], 0.0) for h in range(GLA_HEADS)],
                                 axis=0).astype(BF16)
            att = jnp.where(causal4, _dot_nt(q4, k_t[rows]), 0.0)
            r = _dot(att.astype(BF16), vb[rows])
            o = _dot(q_tb[rows], s_ref[...].astype(BF16))
            for h in range(GLA_HEADS):
                o = o + jnp.where(head_v == h, r[h * c:(h + 1) * c, :], 0.0)
            outs.append(o)
            kv = jnp.where(state_mask, _dot_tn(k_dec[rows], vb[rows]), 0.0)
            last = b_end[ci * c:ci * c + 1, :]
            dec = jnp.exp(jnp.transpose(jnp.broadcast_to(last, (hk, hk))))
            s_ref[...] = s_ref[...] * jnp.concatenate([dec, dec], axis=1) + kv
        o = jnp.concatenate(outs, axis=0)
        o = o * lax.rsqrt(_dot_exact_rhs(o * o, norm_mat, 2) + EPS) * ng
        o_ref[0, pl.ds(r0, grp), :] = o * (g * jax.nn.sigmoid(g))
        return carry

    lax.fori_loop(0, n_groups, body, 0)


def gla_mixer(z_a, wa2, ba, norm_g):
    bsz, t, _ = z_a.shape
    hk = GLA_HEADS * GLA_DK
    wa2p = jnp.zeros((128, hk), F32).at[:GLA_RANK].set(wa2)
    n_chunks = t // GLA_CHUNK
    group = math.gcd(GLA_GROUP, n_chunks)
    return pl.pallas_call(
        functools.partial(_gla_kernel, group=group, n_groups=n_chunks // group),
        grid=(bsz,),
        in_specs=[pl.BlockSpec((1, t, A_WIDTH), lambda b: (b, 0, 0)),
                  pl.BlockSpec((128, hk), lambda b: (0, 0)),
                  pl.BlockSpec((1, hk), lambda b: (0, 0)),
                  pl.BlockSpec((1, GROUP_WIDTH), lambda b: (0, 0))],
        out_specs=pl.BlockSpec((1, t, GROUP_WIDTH), lambda b: (b, 0, 0)),
        out_shape=jax.ShapeDtypeStruct((bsz, t, GROUP_WIDTH), F32),
        scratch_shapes=[pltpu.VMEM((hk, GROUP_WIDTH), F32)],
        compiler_params=_cparams(("parallel",)),
        name="gla_mixer",
    )(z_a, wa2p, ba.reshape(1, hk), norm_g.reshape(1, GROUP_WIDTH))


def _pool_kernel(u_ref, w_ref, b_ref, sc_ref, o_ref):
    u = u_ref[0]
    t, gw = u.shape
    row = lax.broadcasted_iota(jnp.int32, (t, gw), 0)
    grp = lax.broadcasted_iota(jnp.int32, (t, gw), 1) // POOL_CG

    def shifted(x, k):
        return jnp.where(row >= k, pltpu.roll(x, k, axis=0), 0.0)

    s = u
    p = jnp.zeros_like(u)
    for gi, win in enumerate(POOL_WINDOWS):
        half = win // 2
        s = s + shifted(s, half)
        cnt = jnp.minimum(row + 1, win).astype(F32)
        p = jnp.where(grp == gi, s / cnt - u, p)
    y = _dot(p.astype(BF16), w_ref[...]) + b_ref[...]
    o_ref[0] = y * sc_ref[...]


def pool_mixer(z_b, w, b, scale):
    assert POOL_WINDOWS == (2, 4, 8, 16)
    bsz, t, gw = z_b.shape
    w_bd = jnp.zeros((gw, gw), F32)
    for gi in range(len(POOL_WINDOWS)):
        w_bd = w_bd.at[gi * POOL_CG:(gi + 1) * POOL_CG, gi * POOL_CG:(gi + 1) * POOL_CG].set(w[gi])
    return pl.pallas_call(
        _pool_kernel,
        grid=(bsz,),
        in_specs=[pl.BlockSpec((1, t, gw), lambda i: (i, 0, 0)),
                  pl.BlockSpec((gw, gw), lambda i: (0, 0)),
                  pl.BlockSpec((1, gw), lambda i: (0, 0)),
                  pl.BlockSpec((1, gw), lambda i: (0, 0))],
        out_specs=pl.BlockSpec((1, t, gw), lambda i: (i, 0, 0)),
        out_shape=jax.ShapeDtypeStruct((bsz, t, gw), F32),
        compiler_params=_cparams(("parallel",)),
        name="pool_mixer",
    )(z_b, w_bd.astype(BF16), b.reshape(1, gw), scale.reshape(1, gw))


def _sgu_kernel(z_ref, lg_ref, lb_ref, w_ref, bm_ref, o_ref, *, chunks):
    c = SGU_CHUNK
    gw = GROUP_WIDTH
    rows = SGU_GROUPS * c
    tri = (lax.broadcasted_iota(jnp.int32, (rows, c), 1)
           <= lax.broadcasted_iota(jnp.int32, (rows, c), 0) % c)
    ws = jnp.where(tri, w_ref[...], 0.0).astype(BF16)
    grp = lax.broadcasted_iota(jnp.int32, (1, gw), 1) // SGU_CG
    for ci in range(chunks):
        z = jax.nn.gelu(z_ref[0, ci * c:(ci + 1) * c, :], approximate=True)
        u, v = z[:, :gw], z[:, gw:]
        mu = jnp.mean(v, axis=-1, keepdims=True)
        var = jnp.mean(jnp.square(v - mu), axis=-1, keepdims=True)
        vn = (v - mu) * lax.rsqrt(var + EPS) * lg_ref[...] + lb_ref[...]
        r = _dot(ws, vn.astype(BF16))
        mixed = bm_ref[...]
        for g in range(SGU_GROUPS):
            mixed = mixed + jnp.where(grp == g, r[g * c:(g + 1) * c, :], 0.0)
        o_ref[0, ci * c:(ci + 1) * c, :] = u * mixed


def sgu_mixer(z_d, ln_g, ln_b, w_s, b_s, chunks=4):
    bsz, t, _ = z_d.shape
    gw = GROUP_WIDTH
    tt = chunks * SGU_CHUNK
    bias = jnp.repeat(b_s.T, SGU_CG, axis=1)
    return pl.pallas_call(
        functools.partial(_sgu_kernel, chunks=chunks),
        grid=(bsz, t // tt),
        in_specs=[pl.BlockSpec((1, tt, 2 * gw), lambda b, i: (b, i, 0)),
                  pl.BlockSpec((1, gw), lambda b, i: (0, 0)),
                  pl.BlockSpec((1, gw), lambda b, i: (0, 0)),
                  pl.BlockSpec((SGU_GROUPS * SGU_CHUNK, SGU_CHUNK), lambda b, i: (0, 0)),
                  pl.BlockSpec((SGU_CHUNK, gw), lambda b, i: (0, 0))],
        out_specs=pl.BlockSpec((1, tt, gw), lambda b, i: (b, i, 0)),
        out_shape=jax.ShapeDtypeStruct((bsz, t, gw), F32),
        compiler_params=_cparams(("parallel", "parallel")),
        name="sgu_mixer",
    )(z_d, ln_g.reshape(1, gw), ln_b.reshape(1, gw),
      w_s.reshape(SGU_GROUPS * SGU_CHUNK, SGU_CHUNK), bias)


def _bucket_table():
    assert REL_MAX_DIST <= DSA_BLOCK + 1
    s = np.arange(DSA_BLOCK)[:, None]
    t = np.arange(DSA_BLOCK)[None, :]
    dist = np.stack([t - s, DSA_BLOCK + t - s, 2 * DSA_BLOCK + t - s])
    n = np.maximum(dist, 0)
    max_exact = REL_BUCKETS // 2
    nf = np.maximum(n, 1).astype(np.float32)
    large = max_exact + (np.log(nf / np.float32(max_exact)) / np.float32(math.log(REL_MAX_DIST / max_exact))
                         * np.float32(REL_BUCKETS - max_exact)).astype(np.int32)
    return np.where(n < max_exact, n, np.minimum(large, REL_BUCKETS - 1)).astype(np.int32)


def _bias_table_kernel(rb_ref, bucket_ref, o_ref):
    for back in range(3):
        bucket = bucket_ref[back]
        for h in range(DSA_HEADS):
            acc = jnp.zeros(bucket.shape, F32)
            for b in range(REL_BUCKETS):
                acc = jnp.where(bucket == b, rb_ref[b * DSA_HEADS + h], acc)
            o_ref[back, h] = acc


def rel_bias_tables(rel_bias):
    blk = DSA_BLOCK
    return pl.pallas_call(
        _bias_table_kernel,
        in_specs=[pl.BlockSpec(memory_space=pltpu.SMEM),
                  pl.BlockSpec((3, blk, blk), lambda: (0, 0, 0))],
        out_specs=pl.BlockSpec((3, DSA_HEADS, blk, blk), lambda: (0, 0, 0, 0)),
        out_shape=jax.ShapeDtypeStruct((3, DSA_HEADS, blk, blk), F32),
        name="rel_bias_tables",
    )(rel_bias.reshape(-1), jnp.asarray(_bucket_table()))


def _dsa_kernel(q_ref, k_ref, v_ref, qi_ref, kw_ref, qw_ref, toe_ref, *refs, topk, n_blocks, idx_bits, n_cast):
    for src, dst in zip(refs[:n_cast], refs[n_cast + 1:2 * n_cast + 1]):
        dst[...] = src[...].astype(BF16)
    o_ref = refs[n_cast]
    kpl_ref, vt_ref, keys_ref, khi_ref, klo_ref, acc_ref, am_ref, lg_ref, p_ref = refs[2 * n_cast + 1:]
    blk = DSA_BLOCK
    sub = DSA_SUB
    i = pl.program_id(1)
    hd = DSA_HEADS * DSA_DH
    heads_per_half = 128 // IDX_DIM

    @pl.when(i == 0)
    def _():
        lane = lax.broadcasted_iota(jnp.int32, (blk, 128), 1)

        def build(kb, c):
            r0 = pl.multiple_of(kb * blk, blk)
            ki = jnp.where(lane < IDX_DIM, kw_ref[0, pl.ds(r0, blk), :], 0.0)
            for j in range(heads_per_half):
                kpl_ref[kb, j] = (ki if j == 0 else pltpu.roll(ki, j * IDX_DIM, axis=1)).astype(BF16)
            vt_ref[kb] = jnp.transpose(v_ref[0, pl.ds(r0, blk), :].astype(F32)).astype(BF16)
            return c

        lax.fori_loop(0, n_blocks, build, 0)

    s_loc = lax.broadcasted_iota(jnp.int32, (blk, blk), 0)
    t_loc = lax.broadcasted_iota(jnp.int32, (blk, blk), 1)
    n_vis = i + 1

    qi_t = jnp.transpose(qi_ref[0].astype(F32)).astype(BF16)
    qi_halves = [qi_t[:128, :], qi_t[128:, :]]
    w_t = jnp.transpose(qw_ref[0])
    w_rows = [w_t[IDX_DIM + h:IDX_DIM + h + 1, :] * (IDX_HEADS ** -0.5) * (IDX_DIM ** -0.5)
              for h in range(IDX_HEADS)]
    s_sub = lax.broadcasted_iota(jnp.int32, (sub, blk), 0)
    t_sub = lax.broadcasted_iota(jnp.int32, (sub, blk), 1)

    def score_body(kb, c):
        for ci in range(blk // sub):
            rows = slice(ci * sub, (ci + 1) * sub)
            sc = jnp.zeros((sub, blk), F32)
            for half in range(2):
                for j in range(heads_per_half):
                    d = _dot(kpl_ref[kb, j, rows, :], qi_halves[half])
                    sc = sc + jnp.maximum(d, 0.0) * w_rows[half * heads_per_half + j]
            sc = jnp.where(sc == 0.0, 0.0, sc)
            bits = pltpu.bitcast(sc, jnp.int32)
            key = jnp.where(bits < 0, bits ^ jnp.int32(0x7FFFFFFF), bits)
            vis = (kb < i) | (s_sub + ci * sub <= t_sub)
            key = jnp.where(vis, key, jnp.int32(INT_MIN))
            keys_ref[kb, rows, :] = key
            khi_ref[kb, rows, :] = lax.shift_right_arithmetic(key, 16).astype(jnp.int16)
            klo_ref[kb, rows, :] = ((key & 0xFFFF) - 2 ** 15).astype(jnp.int16)
        return c

    lax.fori_loop(0, n_vis, score_body, 0)

    def count(pred):
        def body(kb, acc):
            hit = jnp.where(pred(keys_ref[kb], kb), 1.0, 0.0)
            return acc + jnp.sum(hit.reshape(blk // 32, 32, blk), axis=0)
        acc = lax.fori_loop(0, n_vis, body, jnp.zeros((32, blk), F32))
        return jnp.sum(acc, axis=0, keepdims=True)

    def count16(ref, cand):
        cand = cand.astype(jnp.int16)

        def body(kb, acc):
            hit = jnp.where(ref[kb] >= cand, jnp.int16(1), jnp.int16(0))
            for j in range(blk // 32):
                acc = acc + hit[j * 32:(j + 1) * 32]
            return acc
        acc = lax.fori_loop(0, n_vis, body, jnp.zeros((32, blk), jnp.int16))
        return jnp.sum(acc.astype(jnp.int32).astype(F32), axis=0, keepdims=True)

    def search16(ref, offset):
        lowest = jnp.full((1, blk), -(2 ** 15), jnp.int32)
        base = jnp.where(offset + count16(ref, jnp.zeros((1, blk), jnp.int32)) >= kf, 0, lowest)

        def bit_body(it, base):
            cand = base | lax.shift_left(jnp.int32(1), 14 - it)
            return jnp.where(offset + count16(ref, cand) >= kf, cand, base)
        return lax.fori_loop(0, 15, bit_body, base)

    kf = float(topk)
    zero = jnp.zeros((1, blk), jnp.int32)
    thr_hi = search16(khi_ref, 0.0)
    thr_hi16 = thr_hi.astype(jnp.int16)

    def low_body(kb, above):
        hi = khi_ref[kb]
        klo_ref[kb] = jnp.where(hi == thr_hi16, klo_ref[kb], jnp.int16(-(2 ** 15)))
        hit = jnp.where(hi > thr_hi16, jnp.int16(1), jnp.int16(0))
        for j in range(blk // 32):
            above = above + hit[j * 32:(j + 1) * 32]
        return above

    above = lax.fori_loop(0, n_vis, low_body, jnp.zeros((32, blk), jnp.int16))
    n_above = jnp.sum(above.astype(jnp.int32).astype(F32), axis=0, keepdims=True)
    thr_lo = search16(klo_ref, n_above)
    thr = lax.shift_left(thr_hi, 16) | (thr_lo + 2 ** 15)
    thr_sel = jnp.maximum(thr, jnp.int32(INT_MIN + 1))

    n_ge = count(lambda key, kb: key >= thr)
    excess = jnp.where((n_ge > kf) & (thr > jnp.int32(INT_MIN)), 1.0, 0.0)

    def tie_search():
        need = kf - count(lambda key, kb: key > thr)

        def tie_body(it, j):
            cand = j | lax.shift_left(jnp.int32(1), idx_bits - 1 - it)
            below = count(lambda key, kb: (key == thr) & (kb * blk + s_loc < cand))
            return jnp.where(below < need, cand, j)
        return lax.fori_loop(0, idx_bits, tie_body, zero)

    last = lax.cond(jnp.max(excess) > 0.0, tie_search, lambda: jnp.full((1, blk), 2 ** 30, jnp.int32))

    assert DSA_DH ** -0.5 == 0.125
    q_t = jnp.transpose(q_ref[0].astype(F32) * (DSA_DH ** -0.5))
    row_h = lax.broadcasted_iota(jnp.int32, (hd, 1), 0) // DSA_DH
    q_heads = [jnp.where(row_h == h, q_t, 0.0).astype(BF16) for h in range(DSA_HEADS)]
    acc_ref[...] = jnp.zeros_like(acc_ref)
    n_sub = blk // sub

    def att_body(kb, carry):
        ms, ls = carry
        back = jnp.minimum(i - kb, 2)
        for ci in range(n_sub):
            rows = slice(ci * sub, (ci + 1) * sub)
            key = keys_ref[kb, rows, :]
            sel = (key > thr_sel) | ((key == thr_sel) & (kb * blk + ci * sub + s_sub <= last))
            am_ref[rows, :] = jnp.where(sel, 0.0, NEG_BIG)
        new_ms, new_ls = list(ms), list(ls)
        for heads in DSA_HEAD_GROUPS:
            alphas = {}
            for h in heads:
                pm = jnp.full((8, blk), NEG_BIG, F32)
                for ci in range(n_sub):
                    rows = slice(ci * sub, (ci + 1) * sub)
                    k_rows = k_ref[0, pl.ds(pl.multiple_of(kb * blk + ci * sub, sub), sub), :]
                    lg = _dot(k_rows, q_heads[h]) + toe_ref[back, h, rows, :] + am_ref[rows, :]
                    lg_ref[h, rows, :] = lg
                    pm = jnp.maximum(pm, jnp.max(lg.reshape(sub // 8, 8, blk), axis=0))
                new_ms[h] = jnp.maximum(ms[h], jnp.max(pm, axis=0, keepdims=True))
                alphas[h] = jnp.exp(ms[h] - new_ms[h])
            for h in heads:
                ps = jnp.zeros((8, blk), F32)
                for ci in range(n_sub):
                    rows = slice(ci * sub, (ci + 1) * sub)
                    p = jnp.exp(lg_ref[h, rows, :] - new_ms[h])
                    ps = ps + jnp.sum(p.reshape(sub // 8, 8, blk), axis=0)
                    p_ref[h, rows, :] = p.astype(BF16)
                new_ls[h] = ls[h] * alphas[h] + jnp.sum(ps, axis=0, keepdims=True)
            for h in heads:
                hrows = slice(h * DSA_DH, (h + 1) * DSA_DH)
                acc_ref[hrows, :] = acc_ref[hrows, :] * alphas[h] + _dot(vt_ref[kb, hrows, :], p_ref[h])
        return tuple(new_ms), tuple(new_ls)

    init = (tuple(jnp.full((1, blk), 0.01 * NEG_BIG, F32) for _ in range(DSA_HEADS)),
            tuple(jnp.zeros((1, blk), F32) for _ in range(DSA_HEADS)))
    _, ls = lax.fori_loop(0, n_vis, att_body, init)
    for h in range(DSA_HEADS):
        rows = slice(h * DSA_DH, (h + 1) * DSA_DH)
        acc_ref[rows, :] = acc_ref[rows, :] * (1.0 / ls[h])
    o_ref[0] = jnp.transpose(acc_ref[...])


def _slab_view(w, steps):
    per_step, rem = divmod(w.size, steps)
    assert rem == 0, (w.shape, steps)
    for cols in (w.shape[-1], 2048, 1024, 512, 256, 128):
        if per_step % (BF16_TILE_ROWS * cols) == 0:
            return w.reshape(steps, per_step // cols, cols)
    raise ValueError(f"cannot split {w.shape} into {steps} bf16-tiled slabs")


def dsa_mixer(z_c, z_aux, toe, to_bf16=()):
    bsz, t, _ = z_c.shape
    blk = DSA_BLOCK
    n_blocks = t // blk
    assert t % blk == 0
    topk = min(DSA_TOPK_MAX, t // 4)
    hd = DSA_HEADS * DSA_DH
    steps = bsz * n_blocks
    slabs = [_slab_view(w, steps) for w in to_bf16]
    slab_specs = [pl.BlockSpec((1,) + s.shape[1:], lambda b, i: (b * n_blocks + i, 0, 0)) for s in slabs]
    kernel = functools.partial(_dsa_kernel, topk=topk, n_blocks=n_blocks,
                               idx_bits=max(1, (t - 1).bit_length()), n_cast=len(slabs))
    out, *converted = pl.pallas_call(
        kernel,
        grid=(bsz, n_blocks),
        in_specs=[pl.BlockSpec((1, blk, hd), lambda b, i: (b, i, 0)),
                  pl.BlockSpec((1, t, hd), lambda b, i: (b, 0, 1)),
                  pl.BlockSpec((1, t, hd), lambda b, i: (b, 0, 2)),
                  pl.BlockSpec((1, blk, hd), lambda b, i: (b, i, 3)),
                  pl.BlockSpec((1, t, C_AUX), lambda b, i: (b, 0, 0)),
                  pl.BlockSpec((1, blk, C_AUX), lambda b, i: (b, i, 0)),
                  pl.BlockSpec((3, DSA_HEADS, blk, blk), lambda b, i: (0, 0, 0, 0))] + slab_specs,
        out_specs=[pl.BlockSpec((1, blk, hd), lambda b, i: (b, i, 0))] + slab_specs,
        out_shape=[jax.ShapeDtypeStruct((bsz, t, hd), F32)]
                  + [jax.ShapeDtypeStruct(s.shape, BF16) for s in slabs],
        scratch_shapes=[pltpu.VMEM((n_blocks, 128 // IDX_DIM, blk, 128), BF16),
                        pltpu.VMEM((n_blocks, hd, blk), BF16),
                        pltpu.VMEM((n_blocks, blk, blk), jnp.int32),
                        pltpu.VMEM((n_blocks, blk, blk), jnp.int16),
                        pltpu.VMEM((n_blocks, blk, blk), jnp.int16),
                        pltpu.VMEM((hd, blk), F32),
                        pltpu.VMEM((blk, blk), F32),
                        pltpu.VMEM((DSA_HEADS, blk, blk), F32),
                        pltpu.VMEM((DSA_HEADS, blk, blk), BF16)],
        compiler_params=_cparams(("parallel", "arbitrary")),
        name="dsa_mixer",
    )(z_c, z_c, z_c, z_c, z_aux, z_aux, toe, *slabs)
    return (out, *[c.reshape(w.shape) for c, w in zip(converted, to_bf16)])


def _mixer_tail_kernel(h_ref, a_ref, b_ref, c_ref, d_ref, wout_ref, gxa_ref, wq_ref, k_ref, v_ref, wo_ref,
                       *refs, route):
    gw = GROUP_WIDTH
    x = h_ref[0]
    for gi, r in enumerate((a_ref, b_ref, c_ref, d_ref)):
        x = x + _dot(r[0].astype(BF16), wout_ref[gi * gw:(gi + 1) * gw, :])

    hd = XA_HEADS * XA_DH
    q = _dot(_rms(x, gxa_ref[...]).astype(BF16), wq_ref[...]).astype(BF16)
    k = k_ref[0]
    v = v_ref[0]
    lane_h = lax.broadcasted_iota(jnp.int32, (1, hd), 1) // XA_DH
    o = jnp.zeros((x.shape[0], hd), F32)
    for h in range(XA_HEADS):
        s = _dot_nt(q, jnp.where(lane_h == h, k, 0.0).astype(BF16)) * (XA_DH ** -0.5)
        p = jnp.exp(s - jnp.max(s, axis=-1, keepdims=True))
        p = p / jnp.sum(p, axis=-1, keepdims=True)
        o = o + _dot(p.astype(BF16), jnp.where(lane_h == h, v, 0.0).astype(BF16))
    x = x + _dot(o.astype(BF16), wo_ref[...])

    if not route:
        o_ref, = refs
        o_ref[0] = x
        return
    gffn_ref, wr_ref, o_ref, hn_ref, gate_ref, cnt_ref = refs
    o_ref[0] = x
    hn = _rms(x, gffn_ref[...])
    hn_ref[0] = hn.astype(BF16)
    tm = hn.shape[0]
    lane = lax.broadcasted_iota(jnp.int32, (tm, GATE_LANES), 1)
    logits = jnp.where(lane < N_EXPERTS, _dot_3pass(hn, wr_ref[...]), -jnp.inf)
    m1 = jnp.max(logits, axis=-1, keepdims=True)
    i1 = jnp.min(jnp.where(logits == m1, lane, GATE_LANES), axis=-1, keepdims=True)
    rest = jnp.where(lane == i1, -jnp.inf, logits)
    m2 = jnp.max(rest, axis=-1, keepdims=True)
    i2 = jnp.min(jnp.where(rest == m2, lane, GATE_LANES), axis=-1, keepdims=True)
    e2 = jnp.exp(m2 - m1)
    g1 = 1.0 / (1.0 + e2)
    gates = jnp.where(lane == i1, g1, 0.0) + jnp.where(lane == i2, e2 * g1, 0.0)
    gates = jnp.where(lane == N_EXPERTS, i1.astype(F32), gates)
    gate_ref[0] = jnp.where(lane == N_EXPERTS + 1, i2.astype(F32), gates)
    sel = jnp.where((lane == i1) | (lane == i2), 1.0, 0.0)
    for c in range(tm // MOE_CHUNK):
        cnt_ref[c] = jnp.sum(sel[c * MOE_CHUNK:(c + 1) * MOE_CHUNK, :], axis=0, keepdims=True)


def mixer_tail(h3, mixers, w_out, g_xa, wq, k, v, wo, routing=None, tm=1024):
    bsz, t, d = h3.shape
    tm = min(tm, t)
    mlen = k.shape[1]
    hd = XA_HEADS * XA_DH
    gw = GROUP_WIDTH
    row = lambda b, i: (b, i, 0)
    fixed2 = lambda b, i: (0, 0)
    in_specs = ([pl.BlockSpec((1, tm, d), row)] + [pl.BlockSpec((1, tm, gw), row)] * 4
                + [pl.BlockSpec((4 * gw, d), fixed2), pl.BlockSpec((1, d), fixed2), pl.BlockSpec((d, hd), fixed2),
                   pl.BlockSpec((1, mlen, hd), lambda b, i: (b, 0, 0)),
                   pl.BlockSpec((1, mlen, hd), lambda b, i: (b, 0, 0)),
                   pl.BlockSpec((hd, d), fixed2)])
    args = [h3, *mixers, w_out, g_xa.reshape(1, d), wq, k, v, wo]
    out_specs = [pl.BlockSpec((1, tm, d), row)]
    out_shape = [jax.ShapeDtypeStruct((bsz, t, d), F32)]
    if routing is not None:
        assert tm % MOE_CHUNK == 0
        g_ffn, wr = routing
        in_specs += [pl.BlockSpec((1, d), fixed2), pl.BlockSpec((d, GATE_LANES), fixed2)]
        args += [g_ffn.reshape(1, d), jnp.zeros((d, GATE_LANES), F32).at[:, :N_EXPERTS].set(wr)]
        per_step = tm // MOE_CHUNK
        out_specs += [pl.BlockSpec((1, tm, d), row), pl.BlockSpec((1, tm, GATE_LANES), row),
                      pl.BlockSpec((per_step, 1, GATE_LANES), lambda b, i: (b * (t // tm) + i, 0, 0))]
        out_shape += [jax.ShapeDtypeStruct((bsz, t, d), BF16), jax.ShapeDtypeStruct((bsz, t, GATE_LANES), F32),
                      jax.ShapeDtypeStruct((bsz * t // MOE_CHUNK, 1, GATE_LANES), F32)]
    return pl.pallas_call(
        functools.partial(_mixer_tail_kernel, route=routing is not None),
        grid=(bsz, t // tm),
        in_specs=in_specs,
        out_specs=out_specs,
        out_shape=out_shape,
        compiler_params=_cparams(("parallel", "parallel")),
        name="mixer_tail",
    )(*args)


def _ffn_kernel(h_ref, g_ref, w1_ref, w3_ref, w2_ref, o_ref, *, tf):
    x = h_ref[...]
    hn = _rms(x, g_ref[...]).astype(BF16)
    acc = x
    for f0 in range(0, w1_ref.shape[1], tf):
        a = _dot(hn, w1_ref[:, f0:f0 + tf])
        b = _dot(hn, w3_ref[:, f0:f0 + tf])
        acc = acc + _dot((a * jax.nn.sigmoid(a) * b).astype(BF16), w2_ref[f0:f0 + tf, :])
    o_ref[...] = acc


def ffn(h, g, w1, w3, w2, tm=512, tf=704):
    m, d = h.shape
    tm = min(tm, m)
    nf = w1.shape[1]
    assert nf % tf == 0
    resident = lambda shape: pl.BlockSpec(shape, lambda i: (0, 0), pipeline_mode=pl.Buffered(1))
    return pl.pallas_call(
        functools.partial(_ffn_kernel, tf=tf),
        grid=(m // tm,),
        in_specs=[pl.BlockSpec((tm, d), lambda i: (i, 0)),
                  pl.BlockSpec((1, d), lambda i: (0, 0)),
                  resident((d, nf)), resident((d, nf)), resident((nf, d))],
        out_specs=pl.BlockSpec((tm, d), lambda i: (i, 0)),
        out_shape=jax.ShapeDtypeStruct((m, d), F32),
        compiler_params=_cparams(("parallel",)),
        name="ffn",
    )(h, g.reshape(1, d), w1, w3, w2)


GATE_LANES = 128
MOE_CHUNK = 256
MOE_ROWS = 128
MOE_TOKENS = 2048
MOE_ALIGN = BF16_TILE_ROWS


def _moe_kernel(cb_ref, h_ref, hn_ref, gate_ref, w1_ref, w3_ref, w2_ref, *refs, final):
    if final:
        gfin_ref, *refs = refs
    y_ref, rank_row, rk_ref, gs_ref, xs_ref, yacc_ref, rc_ref, gc_ref = refs
    t = pl.program_id(0)
    e = pl.program_id(1)
    f = pl.program_id(2)
    ts, d = hn_ref.shape
    ch, rb = MOE_CHUNK, MOE_ROWS
    win = 2 * rb
    n_ch = ts // ch
    lane = lax.broadcasted_iota(jnp.int32, (1, GATE_LANES), 1)

    def before(c):
        return cb_ref[(t * (n_ch + 1) + c) * N_EXPERTS + e]

    n_blocks = (before(n_ch) + rb - 1) // rb
    n_windows = (n_blocks * rb + win - 1) // win + 1

    def windows(c):
        lo, hi = before(c), before(c + 1)
        s0 = (lo // MOE_ALIGN) * MOE_ALIGN
        return s0, jnp.where(hi > lo, (hi - s0 + win - 1) // win, 0)

    spans = [windows(c) for c in range(n_ch)]
    single = functools.reduce(jnp.logical_and, [n_win <= 1 for _, n_win in spans])

    def for_each_window(visit):
        @pl.when(single)
        def _():
            for c, (s0, _) in enumerate(spans):
                visit(c, s0)

        @pl.when(jnp.logical_not(single))
        def _():
            for c, (s0, n_win) in enumerate(spans):
                def body(j, carry, c=c, s0=s0):
                    visit(c, s0 + j * win)
                    return carry
                lax.fori_loop(0, n_win, body, 0)

    @pl.when((e == 0) & (f == 0))
    def _():
        y_ref[...] = h_ref[...]
        strict_lower = (lax.broadcasted_iota(jnp.int32, (ch, ch), 1)
                        < lax.broadcasted_iota(jnp.int32, (ch, ch), 0)).astype(BF16)
        lane_f = lane.astype(F32)
        offs = jnp.ones((1, GATE_LANES), F32)
        for c in range(n_ch):
            rows = slice(c * ch, (c + 1) * ch)
            g = gate_ref[rows, :]
            sel = jnp.where((lane_f == g[:, N_EXPERTS:N_EXPERTS + 1])
                            | (lane_f == g[:, N_EXPERTS + 1:N_EXPERTS + 2]), 1.0, 0.0)
            r = _dot(strict_lower, sel.astype(BF16)) + offs
            r = jnp.where(sel > 0.0, r, 0.0)
            rank_row[:, rows] = jnp.transpose(r) - 1.0
            high = jnp.floor(r * (1.0 / 256.0))
            rk_ref[0, rows, :] = high.astype(BF16)
            rk_ref[1, rows, :] = (r - 256.0 * high).astype(BF16)
            for j, part in enumerate(_split_bf16(g, 3)):
                gs_ref[j, rows, :] = part
            offs = offs + jnp.sum(sel, axis=0, keepdims=True)

    @pl.when(f == 0)
    def _():
        pick = (lax.broadcasted_iota(jnp.int32, (GATE_LANES, GATE_LANES), 0) == e).astype(BF16)
        for c in range(n_ch):
            rows = slice(c * ch, (c + 1) * ch)
            rc_ref[rows, :] = 256.0 * _dot(rk_ref[0, rows, :], pick) + _dot(rk_ref[1, rows, :], pick) - 1.0
            gc_ref[rows, :] = (_dot(gs_ref[0, rows, :], pick) + _dot(gs_ref[1, rows, :], pick)
                               + _dot(gs_ref[2, rows, :], pick))

        def clear(j, carry):
            r0 = pl.multiple_of(j * win, win)
            xs_ref[pl.ds(r0, win), :] = jnp.zeros((win, d), BF16)
            yacc_ref[pl.ds(r0, win), :] = jnp.zeros((win, d), F32)
            return carry

        lax.fori_loop(0, n_windows, clear, 0)

        row_id = lax.broadcasted_iota(jnp.int32, (win, ch), 0)

        def gather_window(c, s):
            s = pl.multiple_of(s, MOE_ALIGN)
            ranks = rank_row[pl.ds(e, 1), c * ch:(c + 1) * ch]
            onehot = jnp.where(ranks == (s + row_id).astype(F32), 1.0, 0.0).astype(BF16)
            xs_ref[pl.ds(s, win), :] += _dot(onehot, hn_ref[c * ch:(c + 1) * ch, :]).astype(BF16)

        for_each_window(gather_window)

    def ffn_rows(r0, rows):
        x = xs_ref[pl.ds(r0, rows), :]
        a = _dot(x, w1_ref[0])
        g3 = _dot(x, w3_ref[0])
        yacc_ref[pl.ds(r0, rows), :] += _dot((a * jax.nn.sigmoid(a) * g3).astype(BF16), w2_ref[0])

    n_quads = n_blocks // 4

    def ffn_quad(j, carry):
        ffn_rows(pl.multiple_of(j * (4 * rb), 4 * rb), 4 * rb)
        return carry

    lax.fori_loop(0, n_quads, ffn_quad, 0)
    tail = pl.multiple_of(n_quads * (4 * rb), 4 * rb)

    @pl.when((n_blocks & 2) != 0)
    def _():
        ffn_rows(tail, 2 * rb)

    @pl.when((n_blocks & 1) != 0)
    def _():
        ffn_rows(pl.multiple_of(tail + (n_blocks & 2) * rb, rb), rb)

    @pl.when(f == pl.num_programs(2) - 1)
    def _():
        def to_bf16(j, carry):
            r0 = pl.multiple_of(j * win, win)
            xs_ref[pl.ds(r0, win), :] = yacc_ref[pl.ds(r0, win), :].astype(BF16)
            return carry

        lax.fori_loop(0, n_windows, to_bf16, 0)

        lane_id = lax.broadcasted_iota(jnp.int32, (ch, rb), 1).astype(F32)

        def scatter_window(c, s):
            rows = slice(c * ch, (c + 1) * ch)
            s = pl.multiple_of(s, MOE_ALIGN)
            rank = rc_ref[rows, :] - s.astype(F32)
            onehot = jnp.concatenate([jnp.where(rank == lane_id, 1.0, 0.0),
                                      jnp.where(rank == lane_id + float(rb), 1.0, 0.0)],
                                     axis=1).astype(BF16)
            gate = jnp.concatenate([gc_ref[rows, :]] * (d // GATE_LANES), axis=1)
            y_ref[rows, :] += gate * _dot(onehot, xs_ref[pl.ds(s, win), :])

        for_each_window(scatter_window)

        if final:
            @pl.when(e == pl.num_programs(1) - 1)
            def _():
                y_ref[...] = _rms(y_ref[...], gfin_ref[...])


def moe(h, hn, gates, cnt, w1, w3, w2, final_gain=None, tf=896):
    m, d = hn.shape
    ts = min(MOE_TOKENS, m)
    n_exp, _, nf = w1.shape
    assert nf % tf == 0 and m % ts == 0 and ts % MOE_CHUNK == 0 and MOE_ROWS == GATE_LANES
    n_ch = ts // MOE_CHUNK
    counts = cnt.reshape(m // ts, n_ch, GATE_LANES)[:, :, :n_exp].astype(jnp.int32)
    before = jnp.concatenate([jnp.zeros((m // ts, 1, n_exp), jnp.int32), jnp.cumsum(counts, axis=1)], axis=1)
    per_tile = lambda width: pl.BlockSpec((ts, width), lambda t, e, f, cb: (t, 0), pipeline_mode=pl.Buffered(1))
    in_specs = [per_tile(d), per_tile(d), per_tile(GATE_LANES),
                pl.BlockSpec((1, d, tf), lambda t, e, f, cb: (e, 0, f)),
                pl.BlockSpec((1, d, tf), lambda t, e, f, cb: (e, 0, f)),
                pl.BlockSpec((1, tf, d), lambda t, e, f, cb: (e, f, 0))]
    args = [before.reshape(-1), h, hn, gates, w1, w3, w2]
    if final_gain is not None:
        in_specs.append(pl.BlockSpec((1, d), lambda t, e, f, cb: (0, 0)))
        args.append(final_gain.reshape(1, d))
    grid_spec = pltpu.PrefetchScalarGridSpec(
        num_scalar_prefetch=1,
        grid=(m // ts, n_exp, nf // tf),
        in_specs=in_specs,
        out_specs=per_tile(d),
        scratch_shapes=[pltpu.VMEM((GATE_LANES, ts), F32),
                        pltpu.VMEM((2, ts, GATE_LANES), BF16),
                        pltpu.VMEM((3, ts, GATE_LANES), BF16),
                        pltpu.VMEM((ts + 2 * MOE_ROWS, d), BF16),
                        pltpu.VMEM((ts + 2 * MOE_ROWS, d), F32),
                        pltpu.VMEM((ts, MOE_ROWS), F32),
                        pltpu.VMEM((ts, GATE_LANES), F32)])
    return pl.pallas_call(
        functools.partial(_moe_kernel, final=final_gain is not None),
        grid_spec=grid_spec,
        out_shape=jax.ShapeDtypeStruct((m, d), F32),
        compiler_params=_cparams(("parallel", "arbitrary", "arbitrary")),
        name="moe",
    )(*args)


def _final_norm_kernel(g_ref, x_ref, o_ref):
    o_ref[...] = _rms(x_ref[...], g_ref[...])


def final_rmsnorm(g, x, tm=1024):
    m, d = x.shape
    tm = min(tm, m)
    return pl.pallas_call(
        _final_norm_kernel,
        grid=(m // tm,),
        in_specs=[pl.BlockSpec((1, d), lambda i: (0, 0)), pl.BlockSpec((tm, d), lambda i: (i, 0))],
        out_specs=pl.BlockSpec((tm, d), lambda i: (i, 0)),
        out_shape=jax.ShapeDtypeStruct((m, d), F32),
        compiler_params=_cparams(("parallel",)),
        name="final_norm",
    )(g.reshape(1, d), x)


def _pad_cols(w, width):
    return jnp.pad(w, ((0, 0), (0, width - w.shape[1])))


def _in_proj_weight(w_in):
    a_end = 2 * GLA_HEADS * GLA_DK + 2 * GROUP_WIDTH + GLA_RANK
    b_end = a_end + GROUP_WIDTH
    c_end = b_end + 3 * GROUP_WIDTH + IDX_HEADS * IDX_DIM + IDX_DIM + IDX_HEADS
    assert w_in.shape[1] == c_end + 2 * GROUP_WIDTH
    return jnp.concatenate([_pad_cols(w_in[:, :a_end], A_WIDTH), w_in[:, a_end:b_end],
                            _pad_cols(w_in[:, b_end:c_end], C_MAIN + C_AUX), w_in[:, c_end:]], axis=1).astype(BF16)


def hybrid_layer(h, mem2, toe, p, bsz, t, mixer_weights, routing=None):
    m, d = h.shape
    z_a, z_b, z_c, z_aux, z_d = norm_matmul(h, p["norm_mix"], _in_proj_weight(p["w_in"]),
                                            (A_WIDTH, B_WIDTH, C_MAIN, C_AUX, D_WIDTH),
                                            (F32, F32, BF16, F32, F32))
    o_a = gla_mixer(z_a.reshape(bsz, t, A_WIDTH), p["gla_wa2"], p["gla_ba"], p["gla_norm"])
    o_b = pool_mixer(z_b.reshape(bsz, t, B_WIDTH), p["pool_w"], p["pool_b"].reshape(-1), p["pool_scale"])
    o_c, *weights = dsa_mixer(z_c.reshape(bsz, t, C_MAIN), z_aux.reshape(bsz, t, C_AUX), toe, mixer_weights)
    o_d = sgu_mixer(z_d.reshape(bsz, t, D_WIDTH), p["sgu_ln_g"], p["sgu_ln_b"], p["sgu_w"], p["sgu_b"])
    hd = XA_HEADS * XA_DH
    k, v = norm_matmul(mem2, p["norm_mem"], p["xa_wkv"].astype(BF16), (hd, hd))
    mlen = mem2.shape[0] // bsz
    outs = mixer_tail(h.reshape(bsz, t, d), (o_a, o_b, o_c, o_d), p["w_out"].astype(BF16), p["norm_xa"],
                      p["xa_wq"].astype(BF16), k.reshape(bsz, mlen, hd), v.reshape(bsz, mlen, hd),
                      p["xa_wo"].astype(BF16), routing)
    if routing is None:
        return weights, outs[0].reshape(m, d)
    h3, hn, gates, counts = outs
    return weights, h3.reshape(m, d), hn.reshape(m, d), gates.reshape(m, GATE_LANES), counts


def kernel(x, mem, rel_bias, final_norm, norm_mix, w_in, gla_wa2, gla_ba, gla_norm, pool_w, pool_b,
           pool_scale, sgu_ln_g, sgu_ln_b, sgu_w, sgu_b, w_out, norm_xa, norm_mem, xa_wq, xa_wkv, xa_wo,
           norm_ffn, ffn_w1, ffn_w3, ffn_w2, router, moe_w1, moe_w3, moe_w2):
    bsz, t, d = x.shape
    depth = norm_mix.shape[0]
    h = x.reshape(bsz * t, d)
    mem2 = mem.reshape(-1, d)
    toe = rel_bias_tables(rel_bias)
    normed = False
    for i in range(depth):
        p = dict(norm_mix=norm_mix[i], w_in=w_in[i], gla_wa2=gla_wa2[i], gla_ba=gla_ba[i],
                 gla_norm=gla_norm[i], pool_w=pool_w[i], pool_b=pool_b[i], pool_scale=pool_scale[i],
                 sgu_ln_g=sgu_ln_g[i], sgu_ln_b=sgu_ln_b[i], sgu_w=sgu_w[i], sgu_b=sgu_b[i],
                 w_out=w_out[i], norm_xa=norm_xa[i], norm_mem=norm_mem[i], xa_wq=xa_wq[i],
                 xa_wkv=xa_wkv[i], xa_wo=xa_wo[i])
        j = i // 2
        if i % 2 == 0:
            (w1, w3, w2), h = hybrid_layer(h, mem2, toe, p, bsz, t, (ffn_w1[j], ffn_w3[j], ffn_w2[j]))
            h = ffn(h, norm_ffn[i], w1, w3, w2)
        else:
            (w1, w3, w2), h, hn, gates, counts = hybrid_layer(h, mem2, toe, p, bsz, t,
                                                              (moe_w1[j], moe_w3[j], moe_w2[j]),
                                                              routing=(norm_ffn[i], router[j]))
            normed = i == depth - 1
            h = moe(h, hn, gates, counts, w1, w3, w2, final_gain=final_norm if normed else None)
    if not normed:
        h = final_rmsnorm(final_norm, h)
    return h.reshape(bsz, t, d)
```

```python
import functools
import math

import jax
import jax.numpy as jnp
import numpy as np
from jax import lax
from jax.experimental import pallas as pl
from jax.experimental.pallas import tpu as pltpu

F32 = jnp.float32
BF16 = jnp.bfloat16
EPS = 1e-6

GROUP_WIDTH = 256

GLA_HEADS = 4
GLA_DV = 64
GLA_DK = 32
GLA_RANK = 16
GLA_TAU = 16.0
GLA_CHUNK = 64
GLA_GROUP = 16

POOL_WINDOWS = (2, 4, 8, 16)
POOL_CG = 64

DSA_HEADS = 4
DSA_DH = 64
IDX_HEADS = 8
IDX_DIM = 32
DSA_TOPK_MAX = 256
DSA_BLOCK = 256
DSA_SUB = 64
DSA_HEAD_GROUPS = ((0, 1, 2, 3),)

SGU_GROUPS = 4
SGU_CHUNK = 128
SGU_CG = 64

REL_BUCKETS = 32
REL_MAX_DIST = 128

XA_HEADS = 4
XA_DH = 64

N_EXPERTS = 8

A_WIDTH = 896
B_WIDTH = 256
C_MAIN = 1024
C_AUX = 128
D_WIDTH = 512

BF16_TILE_ROWS = 16
INT_MIN = -(2 ** 31)
NEG_BIG = -1e30
VMEM_LIMIT = 56 * 1024 * 1024


def _cparams(sem):
    return pltpu.CompilerParams(dimension_semantics=sem, vmem_limit_bytes=VMEM_LIMIT)


def _dot(a, b):
    return jnp.dot(a, b, preferred_element_type=F32)


def _dot_nt(a, b):
    return lax.dot_general(a, b, (((1,), (1,)), ((), ())), preferred_element_type=F32)


def _dot_tn(a, b):
    return lax.dot_general(a, b, (((0,), (0,)), ((), ())), preferred_element_type=F32)


def _split_bf16(x, terms):
    parts = []
    for _ in range(terms):
        p = x.astype(BF16)
        parts.append(p)
        x = x - p.astype(F32)
    return parts


def _dot_exact_rhs(a, b, terms):
    out = None
    for p in _split_bf16(a, terms):
        d = _dot(p, b)
        out = d if out is None else out + d
    return out


def _dot_exact_lhs(a, b, terms):
    out = None
    for p in _split_bf16(b, terms):
        d = _dot(a, p)
        out = d if out is None else out + d
    return out


def _dot_3pass(a, b):
    a_hi, a_lo = _split_bf16(a, 2)
    b_hi, b_lo = _split_bf16(b, 2)
    return _dot(a_hi, b_hi) + (_dot(a_hi, b_lo) + _dot(a_lo, b_hi))


def _rms(x, g):
    return x * lax.rsqrt(jnp.mean(x * x, axis=-1, keepdims=True) + EPS) * g


def _norm_matmul_kernel(h_ref, g_ref, w_ref, *out_refs, widths):
    hb = _rms(h_ref[...], g_ref[...]).astype(BF16)
    off = 0
    for o_ref, wd in zip(out_refs, widths):
        o_ref[...] = _dot(hb, w_ref[:, off:off + wd]).astype(o_ref.dtype)
        off += wd


def norm_matmul(h, g, w, widths, dtypes=None, tm=512):
    dtypes = dtypes or (F32,) * len(widths)
    m, d = h.shape
    tm = min(tm, m)
    n = sum(widths)
    return pl.pallas_call(
        functools.partial(_norm_matmul_kernel, widths=widths),
        grid=(m // tm,),
        in_specs=[pl.BlockSpec((tm, d), lambda i: (i, 0)),
                  pl.BlockSpec((1, d), lambda i: (0, 0)),
                  pl.BlockSpec((d, n), lambda i: (0, 0))],
        out_specs=[pl.BlockSpec((tm, wd), lambda i: (i, 0)) for wd in widths],
        out_shape=[jax.ShapeDtypeStruct((m, wd), dt) for wd, dt in zip(widths, dtypes)],
        compiler_params=_cparams(("parallel",)),
        name="norm_matmul",
    )(h, g.reshape(1, d), w)


def _log_sigmoid(x):
    return jnp.minimum(x, 0.0) - jnp.log1p(jnp.exp(-jnp.abs(x)))


def _gla_kernel(z_ref, wa2_ref, ba_ref, ng_ref, o_ref, s_ref, *, group, n_groups):
    c = GLA_CHUNK
    grp = group * c
    hk = GLA_HEADS * GLA_DK
    hv = GLA_HEADS * GLA_DV
    s_ref[...] = jnp.zeros_like(s_ref)

    head_k = lax.broadcasted_iota(jnp.int32, (1, hk), 1) // GLA_DK
    head_v = lax.broadcasted_iota(jnp.int32, (1, hv), 1) // GLA_DV
    cum = min(grp, 256)
    g_row = lax.broadcasted_iota(jnp.int32, (cum, cum), 0)
    g_col = lax.broadcasted_iota(jnp.int32, (cum, cum), 1)
    tril = (((g_row // c) == (g_col // c)) & (g_col <= g_row)).astype(BF16)
    causal4 = (lax.broadcasted_iota(jnp.int32, (GLA_HEADS * c, c), 1)
               <= lax.broadcasted_iota(jnp.int32, (GLA_HEADS * c, c), 0) % c)
    state_mask = (lax.broadcasted_iota(jnp.int32, (hk, hv), 0) // GLA_DK
                  == lax.broadcasted_iota(jnp.int32, (hk, hv), 1) // GLA_DV)
    norm_mat = jnp.where(lax.broadcasted_iota(jnp.int32, (hv, hv), 0) // GLA_DV
                         == lax.broadcasted_iota(jnp.int32, (hv, hv), 1) // GLA_DV,
                         1.0 / GLA_DV, 0.0).astype(BF16)
    wa2 = wa2_ref[...]
    ba = ba_ref[...]
    ng = ng_ref[...]

    def body(n, carry):
        r0 = pl.multiple_of(n * grp, grp)
        z = z_ref[0, pl.ds(r0, grp), :]
        q, k, v, g, lr = z[:, 0:128], z[:, 128:256], z[:, 256:512], z[:, 512:768], z[:, 768:896]
        log_a = _log_sigmoid(_dot_3pass(lr, wa2) + ba) / GLA_TAU
        b = jnp.concatenate([_dot_exact_lhs(tril, log_a[r:r + cum], 3) for r in range(0, grp, cum)], axis=0)
        b_end = jnp.concatenate([jnp.broadcast_to(b[(ci + 1) * c - 1:(ci + 1) * c, :], (c, hk))
                                 for ci in range(group)], axis=0)
        q_t = q * (GLA_DK ** -0.5) * jnp.exp(b)
        q_tb = q_t.astype(BF16)
        k_t = (k * jnp.exp(-b)).astype(BF16)
        k_dec = (k * jnp.exp(b_end - b)).astype(BF16)
        vb = v.astype(BF16)
        outs = []
        for ci in range(group):
            rows = slice(ci * c, (ci + 1) * c)
            q4 = jnp.concatenate([jnp.where(head_k == h, q_t[rows], 0.0) for h in range(GLA_HEADS)],
                                 axis=0).astype(BF16)
            att = jnp.where(causal4, _dot_nt(q4, k_t[rows]), 0.0)
            r = _dot(att.astype(BF16), vb[rows])
            o = _dot(q_tb[rows], s_ref[...].astype(BF16))
            for h in range(GLA_HEADS):
                o = o + jnp.where(head_v == h, r[h * c:(h + 1) * c, :], 0.0)
            outs.append(o)
            kv = jnp.where(state_mask, _dot_tn(k_dec[rows], vb[rows]), 0.0)
            last = b_end[ci * c:ci * c + 1, :]
            dec = jnp.exp(jnp.transpose(jnp.broadcast_to(last, (hk, hk))))
            s_ref[...] = s_ref[...] * jnp.concatenate([dec, dec], axis=1) + kv
        o = jnp.concatenate(outs, axis=0)
        o = o * lax.rsqrt(_dot_exact_rhs(o * o, norm_mat, 2) + EPS) * ng
        o_ref[0, pl.ds(r0, grp), :] = o * (g * jax.nn.sigmoid(g))
        return carry

    lax.fori_loop(0, n_groups, body, 0)


def gla_mixer(z_a, wa2, ba, norm_g):
    bsz, t, _ = z_a.shape
    hk = GLA_HEADS * GLA_DK
    wa2p = jnp.zeros((128, hk), F32).at[:GLA_RANK].set(wa2)
    n_chunks = t // GLA_CHUNK
    group = math.gcd(GLA_GROUP, n_chunks)
    return pl.pallas_call(
        functools.partial(_gla_kernel, group=group, n_groups=n_chunks // group),
        grid=(bsz,),
        in_specs=[pl.BlockSpec((1, t, A_WIDTH), lambda b: (b, 0, 0)),
                  pl.BlockSpec((128, hk), lambda b: (0, 0)),
                  pl.BlockSpec((1, hk), lambda b: (0, 0)),
                  pl.BlockSpec((1, GROUP_WIDTH), lambda b: (0, 0))],
        out_specs=pl.BlockSpec((1, t, GROUP_WIDTH), lambda b: (b, 0, 0)),
        out_shape=jax.ShapeDtypeStruct((bsz, t, GROUP_WIDTH), F32),
        scratch_shapes=[pltpu.VMEM((hk, GROUP_WIDTH), F32)],
        compiler_params=_cparams(("parallel",)),
        name="gla_mixer",
    )(z_a, wa2p, ba.reshape(1, hk), norm_g.reshape(1, GROUP_WIDTH))


def _pool_kernel(u_ref, w_ref, b_ref, sc_ref, o_ref):
    u = u_ref[0]
    t, gw = u.shape
    row = lax.broadcasted_iota(jnp.int32, (t, gw), 0)
    grp = lax.broadcasted_iota(jnp.int32, (t, gw), 1) // POOL_CG

    def shifted(x, k):
        return jnp.where(row >= k, pltpu.roll(x, k, axis=0), 0.0)

    s = u
    p = jnp.zeros_like(u)
    for gi, win in enumerate(POOL_WINDOWS):
        half = win // 2
        s = s + shifted(s, half)
        cnt = jnp.minimum(row + 1, win).astype(F32)
        p = jnp.where(grp == gi, s / cnt - u, p)
    y = _dot(p.astype(BF16), w_ref[...]) + b_ref[...]
    o_ref[0] = y * sc_ref[...]


def pool_mixer(z_b, w, b, scale):
    assert POOL_WINDOWS == (2, 4, 8, 16)
    bsz, t, gw = z_b.shape
    w_bd = jnp.zeros((gw, gw), F32)
    for gi in range(len(POOL_WINDOWS)):
        w_bd = w_bd.at[gi * POOL_CG:(gi + 1) * POOL_CG, gi * POOL_CG:(gi + 1) * POOL_CG].set(w[gi])
    return pl.pallas_call(
        _pool_kernel,
        grid=(bsz,),
        in_specs=[pl.BlockSpec((1, t, gw), lambda i: (i, 0, 0)),
                  pl.BlockSpec((gw, gw), lambda i: (0, 0)),
                  pl.BlockSpec((1, gw), lambda i: (0, 0)),
                  pl.BlockSpec((1, gw), lambda i: (0, 0))],
        out_specs=pl.BlockSpec((1, t, gw), lambda i: (i, 0, 0)),
        out_shape=jax.ShapeDtypeStruct((bsz, t, gw), F32),
        compiler_params=_cparams(("parallel",)),
        name="pool_mixer",
    )(z_b, w_bd.astype(BF16), b.reshape(1, gw), scale.reshape(1, gw))


def _sgu_kernel(z_ref, lg_ref, lb_ref, w_ref, bm_ref, o_ref, *, chunks):
    c = SGU_CHUNK
    gw = GROUP_WIDTH
    rows = SGU_GROUPS * c
    tri = (lax.broadcasted_iota(jnp.int32, (rows, c), 1)
           <= lax.broadcasted_iota(jnp.int32, (rows, c), 0) % c)
    ws = jnp.where(tri, w_ref[...], 0.0).astype(BF16)
    grp = lax.broadcasted_iota(jnp.int32, (1, gw), 1) // SGU_CG
    for ci in range(chunks):
        z = jax.nn.gelu(z_ref[0, ci * c:(ci + 1) * c, :], approximate=True)
        u, v = z[:, :gw], z[:, gw:]
        mu = jnp.mean(v, axis=-1, keepdims=True)
        var = jnp.mean(jnp.square(v - mu), axis=-1, keepdims=True)
        vn = (v - mu) * lax.rsqrt(var + EPS) * lg_ref[...] + lb_ref[...]
        r = _dot(ws, vn.astype(BF16))
        mixed = bm_ref[...]
        for g in range(SGU_GROUPS):
            mixed = mixed + jnp.where(grp == g, r[g * c:(g + 1) * c, :], 0.0)
        o_ref[0, ci * c:(ci + 1) * c, :] = u * mixed


def sgu_mixer(z_d, ln_g, ln_b, w_s, b_s, chunks=4):
    bsz, t, _ = z_d.shape
    gw = GROUP_WIDTH
    tt = chunks * SGU_CHUNK
    bias = jnp.repeat(b_s.T, SGU_CG, axis=1)
    return pl.pallas_call(
        functools.partial(_sgu_kernel, chunks=chunks),
        grid=(bsz, t // tt),
        in_specs=[pl.BlockSpec((1, tt, 2 * gw), lambda b, i: (b, i, 0)),
                  pl.BlockSpec((1, gw), lambda b, i: (0, 0)),
                  pl.BlockSpec((1, gw), lambda b, i: (0, 0)),
                  pl.BlockSpec((SGU_GROUPS * SGU_CHUNK, SGU_CHUNK), lambda b, i: (0, 0)),
                  pl.BlockSpec((SGU_CHUNK, gw), lambda b, i: (0, 0))],
        out_specs=pl.BlockSpec((1, tt, gw), lambda b, i: (b, i, 0)),
        out_shape=jax.ShapeDtypeStruct((bsz, t, gw), F32),
        compiler_params=_cparams(("parallel", "parallel")),
        name="sgu_mixer",
    )(z_d, ln_g.reshape(1, gw), ln_b.reshape(1, gw),
      w_s.reshape(SGU_GROUPS * SGU_CHUNK, SGU_CHUNK), bias)


def _bucket_table():
    assert REL_MAX_DIST <= DSA_BLOCK + 1
    s = np.arange(DSA_BLOCK)[:, None]
    t = np.arange(DSA_BLOCK)[None, :]
    dist = np.stack([t - s, DSA_BLOCK + t - s, 2 * DSA_BLOCK + t - s])
    n = np.maximum(dist, 0)
    max_exact = REL_BUCKETS // 2
    nf = np.maximum(n, 1).astype(np.float32)
    large = max_exact + (np.log(nf / np.float32(max_exact)) / np.float32(math.log(REL_MAX_DIST / max_exact))
                         * np.float32(REL_BUCKETS - max_exact)).astype(np.int32)
    return np.where(n < max_exact, n, np.minimum(large, REL_BUCKETS - 1)).astype(np.int32)


def _bias_table_kernel(rb_ref, bucket_ref, o_ref):
    for back in range(3):
        bucket = bucket_ref[back]
        for h in range(DSA_HEADS):
            acc = jnp.zeros(bucket.shape, F32)
            for b in range(REL_BUCKETS):
                acc = jnp.where(bucket == b, rb_ref[b * DSA_HEADS + h], acc)
            o_ref[back, h] = acc


def rel_bias_tables(rel_bias):
    blk = DSA_BLOCK
    return pl.pallas_call(
        _bias_table_kernel,
        in_specs=[pl.BlockSpec(memory_space=pltpu.SMEM),
                  pl.BlockSpec((3, blk, blk), lambda: (0, 0, 0))],
        out_specs=pl.BlockSpec((3, DSA_HEADS, blk, blk), lambda: (0, 0, 0, 0)),
        out_shape=jax.ShapeDtypeStruct((3, DSA_HEADS, blk, blk), F32),
        name="rel_bias_tables",
    )(rel_bias.reshape(-1), jnp.asarray(_bucket_table()))


def _dsa_kernel(q_ref, k_ref, v_ref, qi_ref, kw_ref, qw_ref, toe_ref, *refs, topk, n_blocks, idx_bits, n_cast):
    for src, dst in zip(refs[:n_cast], refs[n_cast + 1:2 * n_cast + 1]):
        dst[...] = src[...].astype(BF16)
    o_ref = refs[n_cast]
    kpl_ref, vt_ref, keys_ref, khi_ref, klo_ref, acc_ref, am_ref, lg_ref, p_ref = refs[2 * n_cast + 1:]
    blk = DSA_BLOCK
    sub = DSA_SUB
    i = pl.program_id(1)
    hd = DSA_HEADS * DSA_DH
    heads_per_half = 128 // IDX_DIM

    @pl.when(i == 0)
    def _():
        lane = lax.broadcasted_iota(jnp.int32, (blk, 128), 1)

        def build(kb, c):
            r0 = pl.multiple_of(kb * blk, blk)
            ki = jnp.where(lane < IDX_DIM, kw_ref[0, pl.ds(r0, blk), :], 0.0)
            for j in range(heads_per_half):
                kpl_ref[kb, j] = (ki if j == 0 else pltpu.roll(ki, j * IDX_DIM, axis=1)).astype(BF16)
            vt_ref[kb] = jnp.transpose(v_ref[0, pl.ds(r0, blk), :].astype(F32)).astype(BF16)
            return c

        lax.fori_loop(0, n_blocks, build, 0)

    s_loc = lax.broadcasted_iota(jnp.int32, (blk, blk), 0)
    t_loc = lax.broadcasted_iota(jnp.int32, (blk, blk), 1)
    n_vis = i + 1

    qi_t = jnp.transpose(qi_ref[0].astype(F32)).astype(BF16)
    qi_halves = [qi_t[:128, :], qi_t[128:, :]]
    w_t = jnp.transpose(qw_ref[0])
    w_rows = [w_t[IDX_DIM + h:IDX_DIM + h + 1, :] * (IDX_HEADS ** -0.5) * (IDX_DIM ** -0.5)
              for h in range(IDX_HEADS)]
    s_sub = lax.broadcasted_iota(jnp.int32, (sub, blk), 0)
    t_sub = lax.broadcasted_iota(jnp.int32, (sub, blk), 1)

    def score_body(kb, c):
        for ci in range(blk // sub):
            rows = slice(ci * sub, (ci + 1) * sub)
            sc = jnp.zeros((sub, blk), F32)
            for half in range(2):
                for j in range(heads_per_half):
                    d = _dot(kpl_ref[kb, j, rows, :], qi_halves[half])
                    sc = sc + jnp.maximum(d, 0.0) * w_rows[half * heads_per_half + j]
            sc = jnp.where(sc == 0.0, 0.0, sc)
            bits = pltpu.bitcast(sc, jnp.int32)
            key = jnp.where(bits < 0, bits ^ jnp.int32(0x7FFFFFFF), bits)
            vis = (kb < i) | (s_sub + ci * sub <= t_sub)
            key = jnp.where(vis, key, jnp.int32(INT_MIN))
            keys_ref[kb, rows, :] = key
            khi_ref[kb, rows, :] = lax.shift_right_arithmetic(key, 16).astype(jnp.int16)
            klo_ref[kb, rows, :] = ((key & 0xFFFF) - 2 ** 15).astype(jnp.int16)
        return c

    lax.fori_loop(0, n_vis, score_body, 0)

    def count(pred):
        def body(kb, acc):
            hit = jnp.where(pred(keys_ref[kb], kb), 1.0, 0.0)
            return acc + jnp.sum(hit.reshape(blk // 32, 32, blk), axis=0)
        acc = lax.fori_loop(0, n_vis, body, jnp.zeros((32, blk), F32))
        return jnp.sum(acc, axis=0, keepdims=True)

    def count16(ref, cand):
        cand = cand.astype(jnp.int16)

        def body(kb, acc):
            hit = jnp.where(ref[kb] >= cand, jnp.int16(1), jnp.int16(0))
            for j in range(blk // 32):
                acc = acc + hit[j * 32:(j + 1) * 32]
            return acc
        acc = lax.fori_loop(0, n_vis, body, jnp.zeros((32, blk), jnp.int16))
        return jnp.sum(acc.astype(jnp.int32).astype(F32), axis=0, keepdims=True)

    def search16(ref, offset):
        lowest = jnp.full((1, blk), -(2 ** 15), jnp.int32)
        base = jnp.where(offset + count16(ref, jnp.zeros((1, blk), jnp.int32)) >= kf, 0, lowest)

        def bit_body(it, base):
            cand = base | lax.shift_left(jnp.int32(1), 14 - it)
            return jnp.where(offset + count16(ref, cand) >= kf, cand, base)
        return lax.fori_loop(0, 15, bit_body, base)

    kf = float(topk)
    zero = jnp.zeros((1, blk), jnp.int32)
    thr_hi = search16(khi_ref, 0.0)
    thr_hi16 = thr_hi.astype(jnp.int16)

    def low_body(kb, above):
        hi = khi_ref[kb]
        klo_ref[kb] = jnp.where(hi == thr_hi16, klo_ref[kb], jnp.int16(-(2 ** 15)))
        hit = jnp.where(hi > thr_hi16, jnp.int16(1), jnp.int16(0))
        for j in range(blk // 32):
            above = above + hit[j * 32:(j + 1) * 32]
        return above

    above = lax.fori_loop(0, n_vis, low_body, jnp.zeros((32, blk), jnp.int16))
    n_above = jnp.sum(above.astype(jnp.int32).astype(F32), axis=0, keepdims=True)
    thr_lo = search16(klo_ref, n_above)
    thr = lax.shift_left(thr_hi, 16) | (thr_lo + 2 ** 15)
    thr_sel = jnp.maximum(thr, jnp.int32(INT_MIN + 1))

    n_ge = count(lambda key, kb: key >= thr)
    excess = jnp.where((n_ge > kf) & (thr > jnp.int32(INT_MIN)), 1.0, 0.0)

    def tie_search():
        need = kf - count(lambda key, kb: key > thr)

        def tie_body(it, j):
            cand = j | lax.shift_left(jnp.int32(1), idx_bits - 1 - it)
            below = count(lambda key, kb: (key == thr) & (kb * blk + s_loc < cand))
            return jnp.where(below < need, cand, j)
        return lax.fori_loop(0, idx_bits, tie_body, zero)

    last = lax.cond(jnp.max(excess) > 0.0, tie_search, lambda: jnp.full((1, blk), 2 ** 30, jnp.int32))

    assert DSA_DH ** -0.5 == 0.125
    q_t = jnp.transpose(q_ref[0].astype(F32) * (DSA_DH ** -0.5))
    row_h = lax.broadcasted_iota(jnp.int32, (hd, 1), 0) // DSA_DH
    q_heads = [jnp.where(row_h == h, q_t, 0.0).astype(BF16) for h in range(DSA_HEADS)]
    acc_ref[...] = jnp.zeros_like(acc_ref)
    n_sub = blk // sub

    def att_body(kb, carry):
        ms, ls = carry
        back = jnp.minimum(i - kb, 2)
        for ci in range(n_sub):
            rows = slice(ci * sub, (ci + 1) * sub)
            key = keys_ref[kb, rows, :]
            sel = (key > thr_sel) | ((key == thr_sel) & (kb * blk + ci * sub + s_sub <= last))
            am_ref[rows, :] = jnp.where(sel, 0.0, NEG_BIG)
        new_ms, new_ls = list(ms), list(ls)
        for heads in DSA_HEAD_GROUPS:
            alphas = {}
            for h in heads:
                pm = jnp.full((8, blk), NEG_BIG, F32)
                for ci in range(n_sub):
                    rows = slice(ci * sub, (ci + 1) * sub)
                    k_rows = k_ref[0, pl.ds(pl.multiple_of(kb * blk + ci * sub, sub), sub), :]
                    lg = _dot(k_rows, q_heads[h]) + toe_ref[back, h, rows, :] + am_ref[rows, :]
                    lg_ref[h, rows, :] = lg
                    pm = jnp.maximum(pm, jnp.max(lg.reshape(sub // 8, 8, blk), axis=0))
                new_ms[h] = jnp.maximum(ms[h], jnp.max(pm, axis=0, keepdims=True))
                alphas[h] = jnp.exp(ms[h] - new_ms[h])
            for h in heads:
                ps = jnp.zeros((8, blk), F32)
                for ci in range(n_sub):
                    rows = slice(ci * sub, (ci + 1) * sub)
                    p = jnp.exp(lg_ref[h, rows, :] - new_ms[h])
                    ps = ps + jnp.sum(p.reshape(sub // 8, 8, blk), axis=0)
                    p_ref[h, rows, :] = p.astype(BF16)
                new_ls[h] = ls[h] * alphas[h] + jnp.sum(ps, axis=0, keepdims=True)
            for h in heads:
                hrows = slice(h * DSA_DH, (h + 1) * DSA_DH)
                acc_ref[hrows, :] = acc_ref[hrows, :] * alphas[h] + _dot(vt_ref[kb, hrows, :], p_ref[h])
        return tuple(new_ms), tuple(new_ls)

    init = (tuple(jnp.full((1, blk), 0.01 * NEG_BIG, F32) for _ in range(DSA_HEADS)),
            tuple(jnp.zeros((1, blk), F32) for _ in range(DSA_HEADS)))
    _, ls = lax.fori_loop(0, n_vis, att_body, init)
    for h in range(DSA_HEADS):
        rows = slice(h * DSA_DH, (h + 1) * DSA_DH)
        acc_ref[rows, :] = acc_ref[rows, :] * (1.0 / ls[h])
    o_ref[0] = jnp.transpose(acc_ref[...])


def _slab_view(w, steps):
    rows = math.prod(w.shape[:-1])
    n = max(k for k in range(1, steps + 1) if rows % (k * BF16_TILE_ROWS) == 0)
    return w.reshape(n, rows // n, w.shape[-1])


def dsa_mixer(z_c, z_aux, toe, to_bf16=()):
    bsz, t, _ = z_c.shape
    blk = DSA_BLOCK
    n_blocks = t // blk
    assert t % blk == 0
    topk = min(DSA_TOPK_MAX, t // 4)
    hd = DSA_HEADS * DSA_DH
    steps = bsz * n_blocks
    slabs = [_slab_view(w, steps) for w in to_bf16]
    slab_specs = [pl.BlockSpec((1,) + s.shape[1:],
                               lambda b, i, n=s.shape[0]: (jnp.minimum(b * n_blocks + i, n - 1), 0, 0))
                  for s in slabs]
    kernel = functools.partial(_dsa_kernel, topk=topk, n_blocks=n_blocks,
                               idx_bits=max(1, (t - 1).bit_length()), n_cast=len(slabs))
    out, *converted = pl.pallas_call(
        kernel,
        grid=(bsz, n_blocks),
        in_specs=[pl.BlockSpec((1, blk, hd), lambda b, i: (b, i, 0)),
                  pl.BlockSpec((1, t, hd), lambda b, i: (b, 0, 1)),
                  pl.BlockSpec((1, t, hd), lambda b, i: (b, 0, 2)),
                  pl.BlockSpec((1, blk, hd), lambda b, i: (b, i, 3)),
                  pl.BlockSpec((1, t, C_AUX), lambda b, i: (b, 0, 0)),
                  pl.BlockSpec((1, blk, C_AUX), lambda b, i: (b, i, 0)),
                  pl.BlockSpec((3, DSA_HEADS, blk, blk), lambda b, i: (0, 0, 0, 0))] + slab_specs,
        out_specs=[pl.BlockSpec((1, blk, hd), lambda b, i: (b, i, 0))] + slab_specs,
        out_shape=[jax.ShapeDtypeStruct((bsz, t, hd), F32)]
                  + [jax.ShapeDtypeStruct(s.shape, BF16) for s in slabs],
        scratch_shapes=[pltpu.VMEM((n_blocks, 128 // IDX_DIM, blk, 128), BF16),
                        pltpu.VMEM((n_blocks, hd, blk), BF16),
                        pltpu.VMEM((n_blocks, blk, blk), jnp.int32),
                        pltpu.VMEM((n_blocks, blk, blk), jnp.int16),
                        pltpu.VMEM((n_blocks, blk, blk), jnp.int16),
                        pltpu.VMEM((hd, blk), F32),
                        pltpu.VMEM((blk, blk), F32),
                        pltpu.VMEM((DSA_HEADS, blk, blk), F32),
                        pltpu.VMEM((DSA_HEADS, blk, blk), BF16)],
        compiler_params=_cparams(("parallel", "arbitrary")),
        name="dsa_mixer",
    )(z_c, z_c, z_c, z_c, z_aux, z_aux, toe, *slabs)
    return (out, *[c.reshape(w.shape) for c, w in zip(converted, to_bf16)])


def _mixer_tail_kernel(h_ref, a_ref, b_ref, c_ref, d_ref, wout_ref, gxa_ref, wq_ref, k_ref, v_ref, wo_ref,
                       *refs, route):
    gw = GROUP_WIDTH
    x = h_ref[0]
    for gi, r in enumerate((a_ref, b_ref, c_ref, d_ref)):
        x = x + _dot(r[0].astype(BF16), wout_ref[gi * gw:(gi + 1) * gw, :])

    hd = XA_HEADS * XA_DH
    q = _dot(_rms(x, gxa_ref[...]).astype(BF16), wq_ref[...]).astype(BF16)
    k = k_ref[0]
    v = v_ref[0]
    lane_h = lax.broadcasted_iota(jnp.int32, (1, hd), 1) // XA_DH
    o = jnp.zeros((x.shape[0], hd), F32)
    for h in range(XA_HEADS):
        s = _dot_nt(q, jnp.where(lane_h == h, k, 0.0).astype(BF16)) * (XA_DH ** -0.5)
        p = jnp.exp(s - jnp.max(s, axis=-1, keepdims=True))
        p = p / jnp.sum(p, axis=-1, keepdims=True)
        o = o + _dot(p.astype(BF16), jnp.where(lane_h == h, v, 0.0).astype(BF16))
    x = x + _dot(o.astype(BF16), wo_ref[...])

    if not route:
        o_ref, = refs
        o_ref[0] = x
        return
    gffn_ref, wr_ref, o_ref, hn_ref, gate_ref, cnt_ref = refs
    o_ref[0] = x
    hn = _rms(x, gffn_ref[...])
    hn_ref[0] = hn.astype(BF16)
    tm = hn.shape[0]
    lane = lax.broadcasted_iota(jnp.int32, (tm, GATE_LANES), 1)
    logits = jnp.where(lane < N_EXPERTS, _dot_3pass(hn, wr_ref[...]), -jnp.inf)
    m1 = jnp.max(logits, axis=-1, keepdims=True)
    i1 = jnp.min(jnp.where(logits == m1, lane, GATE_LANES), axis=-1, keepdims=True)
    rest = jnp.where(lane == i1, -jnp.inf, logits)
    m2 = jnp.max(rest, axis=-1, keepdims=True)
    i2 = jnp.min(jnp.where(rest == m2, lane, GATE_LANES), axis=-1, keepdims=True)
    e2 = jnp.exp(m2 - m1)
    g1 = 1.0 / (1.0 + e2)
    gates = jnp.where(lane == i1, g1, 0.0) + jnp.where(lane == i2, e2 * g1, 0.0)
    gates = jnp.where(lane == N_EXPERTS, i1.astype(F32), gates)
    gate_ref[0] = jnp.where(lane == N_EXPERTS + 1, i2.astype(F32), gates)
    sel = jnp.where((lane == i1) | (lane == i2), 1.0, 0.0)
    for c in range(tm // MOE_CHUNK):
        cnt_ref[c] = jnp.sum(sel[c * MOE_CHUNK:(c + 1) * MOE_CHUNK, :], axis=0, keepdims=True)


def mixer_tail(h3, mixers, w_out, g_xa, wq, k, v, wo, routing=None, tm=1024):
    bsz, t, d = h3.shape
    tm = min(tm, t)
    mlen = k.shape[1]
    hd = XA_HEADS * XA_DH
    gw = GROUP_WIDTH
    row = lambda b, i: (b, i, 0)
    fixed2 = lambda b, i: (0, 0)
    in_specs = ([pl.BlockSpec((1, tm, d), row)] + [pl.BlockSpec((1, tm, gw), row)] * 4
                + [pl.BlockSpec((4 * gw, d), fixed2), pl.BlockSpec((1, d), fixed2), pl.BlockSpec((d, hd), fixed2),
                   pl.BlockSpec((1, mlen, hd), lambda b, i: (b, 0, 0)),
                   pl.BlockSpec((1, mlen, hd), lambda b, i: (b, 0, 0)),
                   pl.BlockSpec((hd, d), fixed2)])
    args = [h3, *mixers, w_out, g_xa.reshape(1, d), wq, k, v, wo]
    out_specs = [pl.BlockSpec((1, tm, d), row)]
    out_shape = [jax.ShapeDtypeStruct((bsz, t, d), F32)]
    if routing is not None:
        assert tm % MOE_CHUNK == 0
        g_ffn, wr = routing
        in_specs += [pl.BlockSpec((1, d), fixed2), pl.BlockSpec((d, GATE_LANES), fixed2)]
        args += [g_ffn.reshape(1, d), jnp.zeros((d, GATE_LANES), F32).at[:, :N_EXPERTS].set(wr)]
        per_step = tm // MOE_CHUNK
        out_specs += [pl.BlockSpec((1, tm, d), row), pl.BlockSpec((1, tm, GATE_LANES), row),
                      pl.BlockSpec((per_step, 1, GATE_LANES), lambda b, i: (b * (t // tm) + i, 0, 0))]
        out_shape += [jax.ShapeDtypeStruct((bsz, t, d), BF16), jax.ShapeDtypeStruct((bsz, t, GATE_LANES), F32),
                      jax.ShapeDtypeStruct((bsz * t // MOE_CHUNK, 1, GATE_LANES), F32)]
    return pl.pallas_call(
        functools.partial(_mixer_tail_kernel, route=routing is not None),
        grid=(bsz, t // tm),
        in_specs=in_specs,
        out_specs=out_specs,
        out_shape=out_shape,
        compiler_params=_cparams(("parallel", "parallel")),
        name="mixer_tail",
    )(*args)


def _ffn_kernel(h_ref, g_ref, w1_ref, w3_ref, w2_ref, o_ref, *, tf):
    x = h_ref[...]
    hn = _rms(x, g_ref[...]).astype(BF16)
    acc = x
    for f0 in range(0, w1_ref.shape[1], tf):
        a = _dot(hn, w1_ref[:, f0:f0 + tf])
        b = _dot(hn, w3_ref[:, f0:f0 + tf])
        acc = acc + _dot((a * jax.nn.sigmoid(a) * b).astype(BF16), w2_ref[f0:f0 + tf, :])
    o_ref[...] = acc


def ffn(h, g, w1, w3, w2, tm=512, tf=704):
    m, d = h.shape
    tm = min(tm, m)
    nf = w1.shape[1]
    assert nf % tf == 0
    resident = lambda shape: pl.BlockSpec(shape, lambda i: (0, 0), pipeline_mode=pl.Buffered(1))
    return pl.pallas_call(
        functools.partial(_ffn_kernel, tf=tf),
        grid=(m // tm,),
        in_specs=[pl.BlockSpec((tm, d), lambda i: (i, 0)),
                  pl.BlockSpec((1, d), lambda i: (0, 0)),
                  resident((d, nf)), resident((d, nf)), resident((nf, d))],
        out_specs=pl.BlockSpec((tm, d), lambda i: (i, 0)),
        out_shape=jax.ShapeDtypeStruct((m, d), F32),
        compiler_params=_cparams(("parallel",)),
        name="ffn",
    )(h, g.reshape(1, d), w1, w3, w2)


GATE_LANES = 128
MOE_CHUNK = 256
MOE_ROWS = 128
MOE_TOKENS = 2048
MOE_ALIGN = BF16_TILE_ROWS


def _moe_kernel(cb_ref, h_ref, hn_ref, gate_ref, w1_ref, w3_ref, w2_ref, *refs, final):
    if final:
        gfin_ref, *refs = refs
    y_ref, rank_row, rk_ref, gs_ref, xs_ref, yacc_ref, rc_ref, gc_ref = refs
    t = pl.program_id(0)
    e = pl.program_id(1)
    f = pl.program_id(2)
    ts, d = hn_ref.shape
    ch, rb = MOE_CHUNK, MOE_ROWS
    win = 2 * rb
    n_ch = ts // ch
    lane = lax.broadcasted_iota(jnp.int32, (1, GATE_LANES), 1)

    def before(c):
        return cb_ref[(t * (n_ch + 1) + c) * N_EXPERTS + e]

    n_blocks = (before(n_ch) + rb - 1) // rb
    n_windows = (n_blocks * rb + win - 1) // win + 1

    def windows(c):
        lo, hi = before(c), before(c + 1)
        s0 = (lo // MOE_ALIGN) * MOE_ALIGN
        return s0, jnp.where(hi > lo, (hi - s0 + win - 1) // win, 0)

    spans = [windows(c) for c in range(n_ch)]
    single = functools.reduce(jnp.logical_and, [n_win <= 1 for _, n_win in spans])

    def for_each_window(visit):
        @pl.when(single)
        def _():
            for c, (s0, _) in enumerate(spans):
                visit(c, s0)

        @pl.when(jnp.logical_not(single))
        def _():
            for c, (s0, n_win) in enumerate(spans):
                def body(j, carry, c=c, s0=s0):
                    visit(c, s0 + j * win)
                    return carry
                lax.fori_loop(0, n_win, body, 0)

    @pl.when((e == 0) & (f == 0))
    def _():
        y_ref[...] = h_ref[...]
        strict_lower = (lax.broadcasted_iota(jnp.int32, (ch, ch), 1)
                        < lax.broadcasted_iota(jnp.int32, (ch, ch), 0)).astype(BF16)
        lane_f = lane.astype(F32)
        offs = jnp.ones((1, GATE_LANES), F32)
        for c in range(n_ch):
            rows = slice(c * ch, (c + 1) * ch)
            g = gate_ref[rows, :]
            sel = jnp.where((lane_f == g[:, N_EXPERTS:N_EXPERTS + 1])
                            | (lane_f == g[:, N_EXPERTS + 1:N_EXPERTS + 2]), 1.0, 0.0)
            r = _dot(strict_lower, sel.astype(BF16)) + offs
            r = jnp.where(sel > 0.0, r, 0.0)
            rank_row[:, rows] = jnp.transpose(r) - 1.0
            high = jnp.floor(r * (1.0 / 256.0))
            rk_ref[0, rows, :] = high.astype(BF16)
            rk_ref[1, rows, :] = (r - 256.0 * high).astype(BF16)
            for j, part in enumerate(_split_bf16(g, 3)):
                gs_ref[j, rows, :] = part
            offs = offs + jnp.sum(sel, axis=0, keepdims=True)

    @pl.when(f == 0)
    def _():
        pick = (lax.broadcasted_iota(jnp.int32, (GATE_LANES, GATE_LANES), 0) == e).astype(BF16)
        for c in range(n_ch):
            rows = slice(c * ch, (c + 1) * ch)
            rc_ref[rows, :] = 256.0 * _dot(rk_ref[0, rows, :], pick) + _dot(rk_ref[1, rows, :], pick) - 1.0
            gc_ref[rows, :] = (_dot(gs_ref[0, rows, :], pick) + _dot(gs_ref[1, rows, :], pick)
                               + _dot(gs_ref[2, rows, :], pick))

        def clear(j, carry):
            r0 = pl.multiple_of(j * win, win)
            xs_ref[pl.ds(r0, win), :] = jnp.zeros((win, d), BF16)
            yacc_ref[pl.ds(r0, win), :] = jnp.zeros((win, d), F32)
            return carry

        lax.fori_loop(0, n_windows, clear, 0)

        row_id = lax.broadcasted_iota(jnp.int32, (win, ch), 0)

        def gather_window(c, s):
            s = pl.multiple_of(s, MOE_ALIGN)
            ranks = rank_row[pl.ds(e, 1), c * ch:(c + 1) * ch]
            onehot = jnp.where(ranks == (s + row_id).astype(F32), 1.0, 0.0).astype(BF16)
            xs_ref[pl.ds(s, win), :] += _dot(onehot, hn_ref[c * ch:(c + 1) * ch, :]).astype(BF16)

        for_each_window(gather_window)

    def ffn_rows(r0, rows):
        x = xs_ref[pl.ds(r0, rows), :]
        a = _dot(x, w1_ref[0])
        g3 = _dot(x, w3_ref[0])
        yacc_ref[pl.ds(r0, rows), :] += _dot((a * jax.nn.sigmoid(a) * g3).astype(BF16), w2_ref[0])

    five = n_blocks == 5
    half_tail = before(n_ch) <= 4 * rb + rb // 2

    @pl.when(five & half_tail)
    def _():
        ffn_rows(0, 4 * rb + rb // 2)

    @pl.when(five & jnp.logical_not(half_tail))
    def _():
        ffn_rows(0, 5 * rb)

    @pl.when(jnp.logical_not(five))
    def _():
        n_quads = n_blocks // 4

        def ffn_quad(j, carry):
            ffn_rows(pl.multiple_of(j * (4 * rb), 4 * rb), 4 * rb)
            return carry

        lax.fori_loop(0, n_quads, ffn_quad, 0)
        tail = pl.multiple_of(n_quads * (4 * rb), 4 * rb)

        @pl.when((n_blocks & 2) != 0)
        def _():
            ffn_rows(tail, 2 * rb)

        @pl.when((n_blocks & 1) != 0)
        def _():
            ffn_rows(pl.multiple_of(tail + (n_blocks & 2) * rb, rb), rb)

    @pl.when(f == pl.num_programs(2) - 1)
    def _():
        def to_bf16(j, carry):
            r0 = pl.multiple_of(j * win, win)
            xs_ref[pl.ds(r0, win), :] = yacc_ref[pl.ds(r0, win), :].astype(BF16)
            return carry

        lax.fori_loop(0, n_windows, to_bf16, 0)

        lane_id = lax.broadcasted_iota(jnp.int32, (ch, rb), 1).astype(F32)

        def scatter_window(c, s):
            rows = slice(c * ch, (c + 1) * ch)
            s = pl.multiple_of(s, MOE_ALIGN)
            rank = rc_ref[rows, :] - s.astype(F32)
            onehot = jnp.concatenate([jnp.where(rank == lane_id, 1.0, 0.0),
                                      jnp.where(rank == lane_id + float(rb), 1.0, 0.0)],
                                     axis=1).astype(BF16)
            gate = jnp.concatenate([gc_ref[rows, :]] * (d // GATE_LANES), axis=1)
            y_ref[rows, :] += gate * _dot(onehot, xs_ref[pl.ds(s, win), :])

        for_each_window(scatter_window)

        if final:
            @pl.when(e == pl.num_programs(1) - 1)
            def _():
                y_ref[...] = _rms(y_ref[...], gfin_ref[...])


def moe(h, hn, gates, cnt, w1, w3, w2, final_gain=None, tf=896):
    m, d = hn.shape
    ts = min(MOE_TOKENS, m)
    n_exp, _, nf = w1.shape
    assert nf % tf == 0 and m % ts == 0 and ts % MOE_CHUNK == 0 and MOE_ROWS == GATE_LANES
    n_ch = ts // MOE_CHUNK
    counts = cnt.reshape(m // ts, n_ch, GATE_LANES)[:, :, :n_exp].astype(jnp.int32)
    before = jnp.concatenate([jnp.zeros((m // ts, 1, n_exp), jnp.int32), jnp.cumsum(counts, axis=1)], axis=1)
    per_tile = lambda width: pl.BlockSpec((ts, width), lambda t, e, f, cb: (t, 0), pipeline_mode=pl.Buffered(1))
    in_specs = [per_tile(d), per_tile(d), per_tile(GATE_LANES),
                pl.BlockSpec((1, d, tf), lambda t, e, f, cb: (e, 0, f)),
                pl.BlockSpec((1, d, tf), lambda t, e, f, cb: (e, 0, f)),
                pl.BlockSpec((1, tf, d), lambda t, e, f, cb: (e, f, 0))]
    args = [before.reshape(-1), h, hn, gates, w1, w3, w2]
    if final_gain is not None:
        in_specs.append(pl.BlockSpec((1, d), lambda t, e, f, cb: (0, 0)))
        args.append(final_gain.reshape(1, d))
    grid_spec = pltpu.PrefetchScalarGridSpec(
        num_scalar_prefetch=1,
        grid=(m // ts, n_exp, nf // tf),
        in_specs=in_specs,
        out_specs=per_tile(d),
        scratch_shapes=[pltpu.VMEM((GATE_LANES, ts), F32),
                        pltpu.VMEM((2, ts, GATE_LANES), BF16),
                        pltpu.VMEM((3, ts, GATE_LANES), BF16),
                        pltpu.VMEM((ts + 2 * MOE_ROWS, d), BF16),
                        pltpu.VMEM((ts + 2 * MOE_ROWS, d), F32),
                        pltpu.VMEM((ts, MOE_ROWS), F32),
                        pltpu.VMEM((ts, GATE_LANES), F32)])
    return pl.pallas_call(
        functools.partial(_moe_kernel, final=final_gain is not None),
        grid_spec=grid_spec,
        out_shape=jax.ShapeDtypeStruct((m, d), F32),
        compiler_params=_cparams(("parallel", "arbitrary", "arbitrary")),
        name="moe",
    )(*args)


def _final_norm_kernel(g_ref, x_ref, o_ref):
    o_ref[...] = _rms(x_ref[...], g_ref[...])


def final_rmsnorm(g, x, tm=1024):
    m, d = x.shape
    tm = min(tm, m)
    return pl.pallas_call(
        _final_norm_kernel,
        grid=(m // tm,),
        in_specs=[pl.BlockSpec((1, d), lambda i: (0, 0)), pl.BlockSpec((tm, d), lambda i: (i, 0))],
        out_specs=pl.BlockSpec((tm, d), lambda i: (i, 0)),
        out_shape=jax.ShapeDtypeStruct((m, d), F32),
        compiler_params=_cparams(("parallel",)),
        name="final_norm",
    )(g.reshape(1, d), x)


def _pad_cols(w, width):
    return jnp.pad(w, ((0, 0), (0, width - w.shape[1])))


def _in_proj_weight(w_in):
    a_end = 2 * GLA_HEADS * GLA_DK + 2 * GROUP_WIDTH + GLA_RANK
    b_end = a_end + GROUP_WIDTH
    c_end = b_end + 3 * GROUP_WIDTH + IDX_HEADS * IDX_DIM + IDX_DIM + IDX_HEADS
    assert w_in.shape[1] == c_end + 2 * GROUP_WIDTH
    return jnp.concatenate([_pad_cols(w_in[:, :a_end], A_WIDTH), w_in[:, a_end:b_end],
                            _pad_cols(w_in[:, b_end:c_end], C_MAIN + C_AUX), w_in[:, c_end:]], axis=1).astype(BF16)


def hybrid_layer(h, mem2, toe, p, bsz, t, mixer_weights, routing=None):
    m, d = h.shape
    z_a, z_b, z_c, z_aux, z_d = norm_matmul(h, p["norm_mix"], _in_proj_weight(p["w_in"]),
                                            (A_WIDTH, B_WIDTH, C_MAIN, C_AUX, D_WIDTH),
                                            (F32, F32, BF16, F32, F32))
    o_a = gla_mixer(z_a.reshape(bsz, t, A_WIDTH), p["gla_wa2"], p["gla_ba"], p["gla_norm"])
    o_b = pool_mixer(z_b.reshape(bsz, t, B_WIDTH), p["pool_w"], p["pool_b"].reshape(-1), p["pool_scale"])
    o_c, *weights = dsa_mixer(z_c.reshape(bsz, t, C_MAIN), z_aux.reshape(bsz, t, C_AUX), toe, mixer_weights)
    o_d = sgu_mixer(z_d.reshape(bsz, t, D_WIDTH), p["sgu_ln_g"], p["sgu_ln_b"], p["sgu_w"], p["sgu_b"])
    hd = XA_HEADS * XA_DH
    k, v = norm_matmul(mem2, p["norm_mem"], p["xa_wkv"].astype(BF16), (hd, hd))
    mlen = mem2.shape[0] // bsz
    outs = mixer_tail(h.reshape(bsz, t, d), (o_a, o_b, o_c, o_d), p["w_out"].astype(BF16), p["norm_xa"],
                      p["xa_wq"].astype(BF16), k.reshape(bsz, mlen, hd), v.reshape(bsz, mlen, hd),
                      p["xa_wo"].astype(BF16), routing)
    if routing is None:
        return weights, outs[0].reshape(m, d)
    h3, hn, gates, counts = outs
    return weights, h3.reshape(m, d), hn.reshape(m, d), gates.reshape(m, GATE_LANES), counts


def kernel(x, mem, rel_bias, final_norm, norm_mix, w_in, gla_wa2, gla_ba, gla_norm, pool_w, pool_b,
           pool_scale, sgu_ln_g, sgu_ln_b, sgu_w, sgu_b, w_out, norm_xa, norm_mem, xa_wq, xa_wkv, xa_wo,
           norm_ffn, ffn_w1, ffn_w3, ffn_w2, router, moe_w1, moe_w3, moe_w2):
    bsz, t, d = x.shape
    depth = norm_mix.shape[0]
    h = x.reshape(bsz * t, d)
    mem2 = mem.reshape(-1, d)
    toe = rel_bias_tables(rel_bias)
    normed = False
    for i in range(depth):
        p = dict(norm_mix=norm_mix[i], w_in=w_in[i], gla_wa2=gla_wa2[i], gla_ba=gla_ba[i],
                 gla_norm=gla_norm[i], pool_w=pool_w[i], pool_b=pool_b[i], pool_scale=pool_scale[i],
                 sgu_ln_g=sgu_ln_g[i], sgu_ln_b=sgu_ln_b[i], sgu_w=sgu_w[i], sgu_b=sgu_b[i],
                 w_out=w_out[i], norm_xa=norm_xa[i], norm_mem=norm_mem[i], xa_wq=xa_wq[i],
                 xa_wkv=xa_wkv[i], xa_wo=xa_wo[i])
        j = i // 2
        if i % 2 == 0:
            (w1, w3, w2), h = hybrid_layer(h, mem2, toe, p, bsz, t, (ffn_w1[j], ffn_w3[j], ffn_w2[j]))
            h = ffn(h, norm_ffn[i], w1, w3, w2)
        else:
            (w1, w3, w2), h, hn, gates, counts = hybrid_layer(h, mem2, toe, p, bsz, t,
                                                              (moe_w1[j], moe_w3[j], moe_w2[j]),
                                                              routing=(norm_ffn[i], router[j]))
            normed = i == depth - 1
            h = moe(h, hn, gates, counts, w1, w3, w2, final_gain=final_norm if normed else None)
    if not normed:
        h = final_rmsnorm(final_norm, h)
    return h.reshape(bsz, t, d)
```

```python
import functools
import math

import jax
import jax.numpy as jnp
import numpy as np
from jax import lax
from jax.experimental import pallas as pl
from jax.experimental.pallas import tpu as pltpu

F32 = jnp.float32
BF16 = jnp.bfloat16
EPS = 1e-6

GROUP_WIDTH = 256

GLA_HEADS = 4
GLA_DV = 64
GLA_DK = 32
GLA_RANK = 16
GLA_TAU = 16.0
GLA_CHUNK = 64
GLA_GROUP = 16

POOL_WINDOWS = (2, 4, 8, 16)
POOL_CG = 64

DSA_HEADS = 4
DSA_DH = 64
IDX_HEADS = 8
IDX_DIM = 32
DSA_TOPK_MAX = 256
DSA_BLOCK = 256
DSA_SUB = 64
DSA_HEAD_GROUPS = ((0, 1, 2, 3),)

SGU_GROUPS = 4
SGU_CHUNK = 128
SGU_CG = 64

REL_BUCKETS = 32
REL_MAX_DIST = 128

XA_HEADS = 4
XA_DH = 64

N_EXPERTS = 8

A_WIDTH = 896
B_WIDTH = 256
C_MAIN = 1024
C_AUX = 128
D_WIDTH = 512

BF16_TILE_ROWS = 16
MIXER_DTYPE = BF16
INT_MIN = -(2 ** 31)
NEG_BIG = -1e30
VMEM_LIMIT = 56 * 1024 * 1024


def _cparams(sem):
    return pltpu.CompilerParams(dimension_semantics=sem, vmem_limit_bytes=VMEM_LIMIT)


def _dot(a, b):
    return jnp.dot(a, b, preferred_element_type=F32)


def _dot_nt(a, b):
    return lax.dot_general(a, b, (((1,), (1,)), ((), ())), preferred_element_type=F32)


def _dot_tn(a, b):
    return lax.dot_general(a, b, (((0,), (0,)), ((), ())), preferred_element_type=F32)


def _split_bf16(x, terms):
    parts = []
    for _ in range(terms):
        p = x.astype(BF16)
        parts.append(p)
        x = x - p.astype(F32)
    return parts


def _dot_exact_rhs(a, b, terms):
    out = None
    for p in _split_bf16(a, terms):
        d = _dot(p, b)
        out = d if out is None else out + d
    return out


def _dot_exact_lhs(a, b, terms):
    out = None
    for p in _split_bf16(b, terms):
        d = _dot(a, p)
        out = d if out is None else out + d
    return out


def _dot_3pass(a, b):
    a_hi, a_lo = _split_bf16(a, 2)
    b_hi, b_lo = _split_bf16(b, 2)
    return _dot(a_hi, b_hi) + (_dot(a_hi, b_lo) + _dot(a_lo, b_hi))


def _rms(x, g):
    return x * lax.rsqrt(jnp.mean(x * x, axis=-1, keepdims=True) + EPS) * g


def _norm_matmul_kernel(h_ref, g_ref, w_ref, *out_refs, widths):
    hb = _rms(h_ref[...], g_ref[...]).astype(BF16)
    off = 0
    for o_ref, wd in zip(out_refs, widths):
        o_ref[...] = _dot(hb, w_ref[:, off:off + wd]).astype(o_ref.dtype)
        off += wd


def norm_matmul(h, g, w, widths, dtypes=None, tm=512):
    dtypes = dtypes or (F32,) * len(widths)
    m, d = h.shape
    tm = min(tm, m)
    n = sum(widths)
    return pl.pallas_call(
        functools.partial(_norm_matmul_kernel, widths=widths),
        grid=(m // tm,),
        in_specs=[pl.BlockSpec((tm, d), lambda i: (i, 0)),
                  pl.BlockSpec((1, d), lambda i: (0, 0)),
                  pl.BlockSpec((d, n), lambda i: (0, 0))],
        out_specs=[pl.BlockSpec((tm, wd), lambda i: (i, 0)) for wd in widths],
        out_shape=[jax.ShapeDtypeStruct((m, wd), dt) for wd, dt in zip(widths, dtypes)],
        compiler_params=_cparams(("parallel",)),
        name="norm_matmul",
    )(h, g.reshape(1, d), w)


def _log_sigmoid(x):
    return jnp.minimum(x, 0.0) - jnp.log1p(jnp.exp(-jnp.abs(x)))


def _gla_kernel(z_ref, wa2_ref, ba_ref, ng_ref, o_ref, s_ref, *, group, n_groups):
    c = GLA_CHUNK
    grp = group * c
    hk = GLA_HEADS * GLA_DK
    hv = GLA_HEADS * GLA_DV
    s_ref[...] = jnp.zeros_like(s_ref)

    head_k = lax.broadcasted_iota(jnp.int32, (1, hk), 1) // GLA_DK
    head_v = lax.broadcasted_iota(jnp.int32, (1, hv), 1) // GLA_DV
    cum = min(grp, 256)
    g_row = lax.broadcasted_iota(jnp.int32, (cum, cum), 0)
    g_col = lax.broadcasted_iota(jnp.int32, (cum, cum), 1)
    tril = (((g_row // c) == (g_col // c)) & (g_col <= g_row)).astype(BF16)
    causal4 = (lax.broadcasted_iota(jnp.int32, (GLA_HEADS * c, c), 1)
               <= lax.broadcasted_iota(jnp.int32, (GLA_HEADS * c, c), 0) % c)
    state_mask = (lax.broadcasted_iota(jnp.int32, (hk, hv), 0) // GLA_DK
                  == lax.broadcasted_iota(jnp.int32, (hk, hv), 1) // GLA_DV)
    norm_mat = jnp.where(lax.broadcasted_iota(jnp.int32, (hv, hv), 0) // GLA_DV
                         == lax.broadcasted_iota(jnp.int32, (hv, hv), 1) // GLA_DV,
                         1.0 / GLA_DV, 0.0).astype(BF16)
    wa2 = wa2_ref[...]
    ba = ba_ref[...]
    ng = ng_ref[...]

    def body(n, carry):
        r0 = pl.multiple_of(n * grp, grp)
        z = z_ref[0, pl.ds(r0, grp), :]
        q, k, v, g, lr = z[:, 0:128], z[:, 128:256], z[:, 256:512], z[:, 512:768], z[:, 768:896]
        log_a = _log_sigmoid(_dot_3pass(lr, wa2) + ba) / GLA_TAU
        b = jnp.concatenate([_dot_exact_lhs(tril, log_a[r:r + cum], 3) for r in range(0, grp, cum)], axis=0)
        b_end = jnp.concatenate([jnp.broadcast_to(b[(ci + 1) * c - 1:(ci + 1) * c, :], (c, hk))
                                 for ci in range(group)], axis=0)
        q_t = q * (GLA_DK ** -0.5) * jnp.exp(b)
        q_tb = q_t.astype(BF16)
        k_t = (k * jnp.exp(-b)).astype(BF16)
        k_dec = (k * jnp.exp(b_end - b)).astype(BF16)
        vb = v.astype(BF16)
        outs = []
        for ci in range(group):
            rows = slice(ci * c, (ci + 1) * c)
            q4 = jnp.concatenate([jnp.where(head_k == h, q_t[rows], 0.0) for h in range(GLA_HEADS)],
                                 axis=0).astype(BF16)
            att = jnp.where(causal4, _dot_nt(q4, k_t[rows]), 0.0)
            r = _dot(att.astype(BF16), vb[rows])
            o = _dot(q_tb[rows], s_ref[...].astype(BF16))
            for h in range(GLA_HEADS):
                o = o + jnp.where(head_v == h, r[h * c:(h + 1) * c, :], 0.0)
            outs.append(o)
            kv = jnp.where(state_mask, _dot_tn(k_dec[rows], vb[rows]), 0.0)
            last = b_end[ci * c:ci * c + 1, :]
            dec = jnp.exp(jnp.transpose(jnp.broadcast_to(last, (hk, hk))))
            s_ref[...] = s_ref[...] * jnp.concatenate([dec, dec], axis=1) + kv
        o = jnp.concatenate(outs, axis=0)
        o = o * lax.rsqrt(_dot_exact_rhs(o * o, norm_mat, 2) + EPS) * ng
        o_ref[0, pl.ds(r0, grp), :] = (o * (g * jax.nn.sigmoid(g))).astype(o_ref.dtype)
        return carry

    lax.fori_loop(0, n_groups, body, 0)


def gla_mixer(z_a, wa2, ba, norm_g):
    bsz, t, _ = z_a.shape
    hk = GLA_HEADS * GLA_DK
    wa2p = jnp.zeros((128, hk), F32).at[:GLA_RANK].set(wa2)
    n_chunks = t // GLA_CHUNK
    group = math.gcd(GLA_GROUP, n_chunks)
    return pl.pallas_call(
        functools.partial(_gla_kernel, group=group, n_groups=n_chunks // group),
        grid=(bsz,),
        in_specs=[pl.BlockSpec((1, t, A_WIDTH), lambda b: (b, 0, 0)),
                  pl.BlockSpec((128, hk), lambda b: (0, 0)),
                  pl.BlockSpec((1, hk), lambda b: (0, 0)),
                  pl.BlockSpec((1, GROUP_WIDTH), lambda b: (0, 0))],
        out_specs=pl.BlockSpec((1, t, GROUP_WIDTH), lambda b: (b, 0, 0)),
        out_shape=jax.ShapeDtypeStruct((bsz, t, GROUP_WIDTH), MIXER_DTYPE),
        scratch_shapes=[pltpu.VMEM((hk, GROUP_WIDTH), F32)],
        compiler_params=_cparams(("parallel",)),
        name="gla_mixer",
    )(z_a, wa2p, ba.reshape(1, hk), norm_g.reshape(1, GROUP_WIDTH))


def _pool_kernel(u_ref, w_ref, b_ref, sc_ref, o_ref):
    u = u_ref[0]
    t, gw = u.shape
    row = lax.broadcasted_iota(jnp.int32, (t, gw), 0)
    grp = lax.broadcasted_iota(jnp.int32, (t, gw), 1) // POOL_CG

    def shifted(x, k):
        return jnp.where(row >= k, pltpu.roll(x, k, axis=0), 0.0)

    s = u
    p = jnp.zeros_like(u)
    for gi, win in enumerate(POOL_WINDOWS):
        half = win // 2
        s = s + shifted(s, half)
        cnt = jnp.minimum(row + 1, win).astype(F32)
        p = jnp.where(grp == gi, s / cnt - u, p)
    y = _dot(p.astype(BF16), w_ref[...]) + b_ref[...]
    o_ref[0] = (y * sc_ref[...]).astype(o_ref.dtype)


def pool_mixer(z_b, w, b, scale):
    assert POOL_WINDOWS == (2, 4, 8, 16)
    bsz, t, gw = z_b.shape
    w_bd = jnp.zeros((gw, gw), F32)
    for gi in range(len(POOL_WINDOWS)):
        w_bd = w_bd.at[gi * POOL_CG:(gi + 1) * POOL_CG, gi * POOL_CG:(gi + 1) * POOL_CG].set(w[gi])
    return pl.pallas_call(
        _pool_kernel,
        grid=(bsz,),
        in_specs=[pl.BlockSpec((1, t, gw), lambda i: (i, 0, 0)),
                  pl.BlockSpec((gw, gw), lambda i: (0, 0)),
                  pl.BlockSpec((1, gw), lambda i: (0, 0)),
                  pl.BlockSpec((1, gw), lambda i: (0, 0))],
        out_specs=pl.BlockSpec((1, t, gw), lambda i: (i, 0, 0)),
        out_shape=jax.ShapeDtypeStruct((bsz, t, gw), MIXER_DTYPE),
        compiler_params=_cparams(("parallel",)),
        name="pool_mixer",
    )(z_b, w_bd.astype(BF16), b.reshape(1, gw), scale.reshape(1, gw))


def _sgu_kernel(z_ref, lg_ref, lb_ref, w_ref, bm_ref, o_ref, *, chunks):
    c = SGU_CHUNK
    gw = GROUP_WIDTH
    rows = SGU_GROUPS * c
    tri = (lax.broadcasted_iota(jnp.int32, (rows, c), 1)
           <= lax.broadcasted_iota(jnp.int32, (rows, c), 0) % c)
    ws = jnp.where(tri, w_ref[...], 0.0).astype(BF16)
    grp = lax.broadcasted_iota(jnp.int32, (1, gw), 1) // SGU_CG
    for ci in range(chunks):
        z = jax.nn.gelu(z_ref[0, ci * c:(ci + 1) * c, :], approximate=True)
        u, v = z[:, :gw], z[:, gw:]
        mu = jnp.mean(v, axis=-1, keepdims=True)
        var = jnp.mean(jnp.square(v - mu), axis=-1, keepdims=True)
        vn = (v - mu) * lax.rsqrt(var + EPS) * lg_ref[...] + lb_ref[...]
        r = _dot(ws, vn.astype(BF16))
        mixed = bm_ref[...]
        for g in range(SGU_GROUPS):
            mixed = mixed + jnp.where(grp == g, r[g * c:(g + 1) * c, :], 0.0)
        o_ref[0, ci * c:(ci + 1) * c, :] = (u * mixed).astype(o_ref.dtype)


def sgu_mixer(z_d, ln_g, ln_b, w_s, b_s, chunks=4):
    bsz, t, _ = z_d.shape
    gw = GROUP_WIDTH
    tt = chunks * SGU_CHUNK
    bias = jnp.repeat(b_s.T, SGU_CG, axis=1)
    return pl.pallas_call(
        functools.partial(_sgu_kernel, chunks=chunks),
        grid=(bsz, t // tt),
        in_specs=[pl.BlockSpec((1, tt, 2 * gw), lambda b, i: (b, i, 0)),
                  pl.BlockSpec((1, gw), lambda b, i: (0, 0)),
                  pl.BlockSpec((1, gw), lambda b, i: (0, 0)),
                  pl.BlockSpec((SGU_GROUPS * SGU_CHUNK, SGU_CHUNK), lambda b, i: (0, 0)),
                  pl.BlockSpec((SGU_CHUNK, gw), lambda b, i: (0, 0))],
        out_specs=pl.BlockSpec((1, tt, gw), lambda b, i: (b, i, 0)),
        out_shape=jax.ShapeDtypeStruct((bsz, t, gw), MIXER_DTYPE),
        compiler_params=_cparams(("parallel", "parallel")),
        name="sgu_mixer",
    )(z_d, ln_g.reshape(1, gw), ln_b.reshape(1, gw),
      w_s.reshape(SGU_GROUPS * SGU_CHUNK, SGU_CHUNK), bias)


def _bucket_table():
    assert REL_MAX_DIST <= DSA_BLOCK + 1
    s = np.arange(DSA_BLOCK)[:, None]
    t = np.arange(DSA_BLOCK)[None, :]
    dist = np.stack([t - s, DSA_BLOCK + t - s, 2 * DSA_BLOCK + t - s])
    n = np.maximum(dist, 0)
    max_exact = REL_BUCKETS // 2
    nf = np.maximum(n, 1).astype(np.float32)
    large = max_exact + (np.log(nf / np.float32(max_exact)) / np.float32(math.log(REL_MAX_DIST / max_exact))
                         * np.float32(REL_BUCKETS - max_exact)).astype(np.int32)
    return np.where(n < max_exact, n, np.minimum(large, REL_BUCKETS - 1)).astype(np.int32)


def _bias_table_kernel(rb_ref, bucket_ref, o_ref):
    for back in range(3):
        bucket = bucket_ref[back]
        for h in range(DSA_HEADS):
            acc = jnp.zeros(bucket.shape, F32)
            for b in range(REL_BUCKETS):
                acc = jnp.where(bucket == b, rb_ref[b * DSA_HEADS + h], acc)
            o_ref[back, h] = acc


def rel_bias_tables(rel_bias):
    blk = DSA_BLOCK
    return pl.pallas_call(
        _bias_table_kernel,
        in_specs=[pl.BlockSpec(memory_space=pltpu.SMEM),
                  pl.BlockSpec((3, blk, blk), lambda: (0, 0, 0))],
        out_specs=pl.BlockSpec((3, DSA_HEADS, blk, blk), lambda: (0, 0, 0, 0)),
        out_shape=jax.ShapeDtypeStruct((3, DSA_HEADS, blk, blk), F32),
        name="rel_bias_tables",
    )(rel_bias.reshape(-1), jnp.asarray(_bucket_table()))


def _dsa_kernel(q_ref, k_ref, v_ref, qi_ref, kw_ref, qw_ref, toe_ref, *refs, topk, n_blocks, idx_bits, n_cast):
    for src, dst in zip(refs[:n_cast], refs[n_cast + 1:2 * n_cast + 1]):
        dst[...] = src[...].astype(BF16)
    o_ref = refs[n_cast]
    kpl_ref, vt_ref, keys_ref, khi_ref, klo_ref, acc_ref, am_ref, lg_ref, p_ref = refs[2 * n_cast + 1:]
    blk = DSA_BLOCK
    sub = DSA_SUB
    i = pl.program_id(1)
    hd = DSA_HEADS * DSA_DH
    heads_per_half = 128 // IDX_DIM

    @pl.when(i == 0)
    def _():
        lane = lax.broadcasted_iota(jnp.int32, (blk, 128), 1)

        def build(kb, c):
            r0 = pl.multiple_of(kb * blk, blk)
            ki = jnp.where(lane < IDX_DIM, kw_ref[0, pl.ds(r0, blk), :], 0.0)
            for j in range(heads_per_half):
                kpl_ref[kb, j] = (ki if j == 0 else pltpu.roll(ki, j * IDX_DIM, axis=1)).astype(BF16)
            vt_ref[kb] = jnp.transpose(v_ref[0, pl.ds(r0, blk), :].astype(F32)).astype(BF16)
            return c

        lax.fori_loop(0, n_blocks, build, 0)

    s_loc = lax.broadcasted_iota(jnp.int32, (blk, blk), 0)
    t_loc = lax.broadcasted_iota(jnp.int32, (blk, blk), 1)
    n_vis = i + 1

    qi_t = jnp.transpose(qi_ref[0].astype(F32)).astype(BF16)
    qi_halves = [qi_t[:128, :], qi_t[128:, :]]
    w_t = jnp.transpose(qw_ref[0])
    w_rows = [w_t[IDX_DIM + h:IDX_DIM + h + 1, :] * (IDX_HEADS ** -0.5) * (IDX_DIM ** -0.5)
              for h in range(IDX_HEADS)]
    s_sub = lax.broadcasted_iota(jnp.int32, (sub, blk), 0)
    t_sub = lax.broadcasted_iota(jnp.int32, (sub, blk), 1)

    def score_body(kb, c):
        for ci in range(blk // sub):
            rows = slice(ci * sub, (ci + 1) * sub)
            sc = jnp.zeros((sub, blk), F32)
            for half in range(2):
                for j in range(heads_per_half):
                    d = _dot(kpl_ref[kb, j, rows, :], qi_halves[half])
                    sc = sc + jnp.maximum(d, 0.0) * w_rows[half * heads_per_half + j]
            sc = jnp.where(sc == 0.0, 0.0, sc)
            bits = pltpu.bitcast(sc, jnp.int32)
            key = jnp.where(bits < 0, bits ^ jnp.int32(0x7FFFFFFF), bits)
            vis = (kb < i) | (s_sub + ci * sub <= t_sub)
            key = jnp.where(vis, key, jnp.int32(INT_MIN))
            keys_ref[kb, rows, :] = key
            khi_ref[kb, rows, :] = lax.shift_right_arithmetic(key, 16).astype(jnp.int16)
            klo_ref[kb, rows, :] = ((key & 0xFFFF) - 2 ** 15).astype(jnp.int16)
        return c

    lax.fori_loop(0, n_vis, score_body, 0)

    def count(pred):
        def body(kb, acc):
            hit = jnp.where(pred(keys_ref[kb], kb), 1.0, 0.0)
            return acc + jnp.sum(hit.reshape(blk // 32, 32, blk), axis=0)
        acc = lax.fori_loop(0, n_vis, body, jnp.zeros((32, blk), F32))
        return jnp.sum(acc, axis=0, keepdims=True)

    def count16(ref, cand):
        cand = cand.astype(jnp.int16)

        def body(kb, acc):
            hit = jnp.where(ref[kb] >= cand, jnp.int16(1), jnp.int16(0))
            for j in range(blk // 32):
                acc = acc + hit[j * 32:(j + 1) * 32]
            return acc
        acc = lax.fori_loop(0, n_vis, body, jnp.zeros((32, blk), jnp.int16))
        return jnp.sum(acc.astype(jnp.int32).astype(F32), axis=0, keepdims=True)

    def search16(ref, offset):
        lowest = jnp.full((1, blk), -(2 ** 15), jnp.int32)
        base = jnp.where(offset + count16(ref, jnp.zeros((1, blk), jnp.int32)) >= kf, 0, lowest)

        def bit_body(it, base):
            cand = base | lax.shift_left(jnp.int32(1), 14 - it)
            return jnp.where(offset + count16(ref, cand) >= kf, cand, base)
        return lax.fori_loop(0, 15, bit_body, base)

    kf = float(topk)
    zero = jnp.zeros((1, blk), jnp.int32)
    thr_hi = search16(khi_ref, 0.0)
    thr_hi16 = thr_hi.astype(jnp.int16)

    def low_body(kb, above):
        hi = khi_ref[kb]
        klo_ref[kb] = jnp.where(hi == thr_hi16, klo_ref[kb], jnp.int16(-(2 ** 15)))
        hit = jnp.where(hi > thr_hi16, jnp.int16(1), jnp.int16(0))
        for j in range(blk // 32):
            above = above + hit[j * 32:(j + 1) * 32]
        return above

    above = lax.fori_loop(0, n_vis, low_body, jnp.zeros((32, blk), jnp.int16))
    n_above = jnp.sum(above.astype(jnp.int32).astype(F32), axis=0, keepdims=True)
    thr_lo = search16(klo_ref, n_above)
    thr = lax.shift_left(thr_hi, 16) | (thr_lo + 2 ** 15)
    thr_sel = jnp.maximum(thr, jnp.int32(INT_MIN + 1))

    n_ge = count(lambda key, kb: key >= thr)
    excess = jnp.where((n_ge > kf) & (thr > jnp.int32(INT_MIN)), 1.0, 0.0)

    def tie_search():
        need = kf - count(lambda key, kb: key > thr)

        def tie_body(it, j):
            cand = j | lax.shift_left(jnp.int32(1), idx_bits - 1 - it)
            below = count(lambda key, kb: (key == thr) & (kb * blk + s_loc < cand))
            return jnp.where(below < need, cand, j)
        return lax.fori_loop(0, idx_bits, tie_body, zero)

    last = lax.cond(jnp.max(excess) > 0.0, tie_search, lambda: jnp.full((1, blk), 2 ** 30, jnp.int32))

    assert DSA_DH ** -0.5 == 0.125
    q_t = jnp.transpose(q_ref[0].astype(F32) * (DSA_DH ** -0.5))
    row_h = lax.broadcasted_iota(jnp.int32, (hd, 1), 0) // DSA_DH
    q_heads = [jnp.where(row_h == h, q_t, 0.0).astype(BF16) for h in range(DSA_HEADS)]
    acc_ref[...] = jnp.zeros_like(acc_ref)
    n_sub = blk // sub

    def att_body(kb, carry):
        ms, ls = carry
        back = jnp.minimum(i - kb, 2)
        for ci in range(n_sub):
            rows = slice(ci * sub, (ci + 1) * sub)
            key = keys_ref[kb, rows, :]
            sel = (key > thr_sel) | ((key == thr_sel) & (kb * blk + ci * sub + s_sub <= last))
            am_ref[rows, :] = jnp.where(sel, 0.0, NEG_BIG)
        new_ms, new_ls = list(ms), list(ls)
        for heads in DSA_HEAD_GROUPS:
            alphas = {}
            for h in heads:
                pm = jnp.full((8, blk), NEG_BIG, F32)
                for ci in range(n_sub):
                    rows = slice(ci * sub, (ci + 1) * sub)
                    k_rows = k_ref[0, pl.ds(pl.multiple_of(kb * blk + ci * sub, sub), sub), :]
                    lg = _dot(k_rows, q_heads[h]) + toe_ref[back, h, rows, :] + am_ref[rows, :]
                    lg_ref[h, rows, :] = lg
                    pm = jnp.maximum(pm, jnp.max(lg.reshape(sub // 8, 8, blk), axis=0))
                new_ms[h] = jnp.maximum(ms[h], jnp.max(pm, axis=0, keepdims=True))
                alphas[h] = jnp.exp(ms[h] - new_ms[h])
            for h in heads:
                ps = jnp.zeros((8, blk), F32)
                for ci in range(n_sub):
                    rows = slice(ci * sub, (ci + 1) * sub)
                    p = jnp.exp(lg_ref[h, rows, :] - new_ms[h])
                    ps = ps + jnp.sum(p.reshape(sub // 8, 8, blk), axis=0)
                    p_ref[h, rows, :] = p.astype(BF16)
                new_ls[h] = ls[h] * alphas[h] + jnp.sum(ps, axis=0, keepdims=True)
            for h in heads:
                hrows = slice(h * DSA_DH, (h + 1) * DSA_DH)
                acc_ref[hrows, :] = acc_ref[hrows, :] * alphas[h] + _dot(vt_ref[kb, hrows, :], p_ref[h])
        return tuple(new_ms), tuple(new_ls)

    init = (tuple(jnp.full((1, blk), 0.01 * NEG_BIG, F32) for _ in range(DSA_HEADS)),
            tuple(jnp.zeros((1, blk), F32) for _ in range(DSA_HEADS)))
    _, ls = lax.fori_loop(0, n_vis, att_body, init)
    for h in range(DSA_HEADS):
        rows = slice(h * DSA_DH, (h + 1) * DSA_DH)
        acc_ref[rows, :] = acc_ref[rows, :] * (1.0 / ls[h])
    o_ref[0] = jnp.transpose(acc_ref[...]).astype(o_ref.dtype)


def _slab_view(w, steps):
    rows = math.prod(w.shape[:-1])
    n = max(k for k in range(1, steps + 1) if rows % (k * BF16_TILE_ROWS) == 0)
    return w.reshape(n, rows // n, w.shape[-1])


def dsa_mixer(z_c, z_aux, toe, to_bf16=()):
    bsz, t, _ = z_c.shape
    blk = DSA_BLOCK
    n_blocks = t // blk
    assert t % blk == 0
    topk = min(DSA_TOPK_MAX, t // 4)
    hd = DSA_HEADS * DSA_DH
    steps = bsz * n_blocks
    slabs = [_slab_view(w, steps) for w in to_bf16]
    slab_specs = [pl.BlockSpec((1,) + s.shape[1:],
                               lambda b, i, n=s.shape[0]: (jnp.minimum(b * n_blocks + i, n - 1), 0, 0))
                  for s in slabs]
    kernel = functools.partial(_dsa_kernel, topk=topk, n_blocks=n_blocks,
                               idx_bits=max(1, (t - 1).bit_length()), n_cast=len(slabs))
    out, *converted = pl.pallas_call(
        kernel,
        grid=(bsz, n_blocks),
        in_specs=[pl.BlockSpec((1, blk, hd), lambda b, i: (b, i, 0)),
                  pl.BlockSpec((1, t, hd), lambda b, i: (b, 0, 1)),
                  pl.BlockSpec((1, t, hd), lambda b, i: (b, 0, 2)),
                  pl.BlockSpec((1, blk, hd), lambda b, i: (b, i, 3)),
                  pl.BlockSpec((1, t, C_AUX), lambda b, i: (b, 0, 0)),
                  pl.BlockSpec((1, blk, C_AUX), lambda b, i: (b, i, 0)),
                  pl.BlockSpec((3, DSA_HEADS, blk, blk), lambda b, i: (0, 0, 0, 0))] + slab_specs,
        out_specs=[pl.BlockSpec((1, blk, hd), lambda b, i: (b, i, 0))] + slab_specs,
        out_shape=[jax.ShapeDtypeStruct((bsz, t, hd), MIXER_DTYPE)]
                  + [jax.ShapeDtypeStruct(s.shape, BF16) for s in slabs],
        scratch_shapes=[pltpu.VMEM((n_blocks, 128 // IDX_DIM, blk, 128), BF16),
                        pltpu.VMEM((n_blocks, hd, blk), BF16),
                        pltpu.VMEM((n_blocks, blk, blk), jnp.int32),
                        pltpu.VMEM((n_blocks, blk, blk), jnp.int16),
                        pltpu.VMEM((n_blocks, blk, blk), jnp.int16),
                        pltpu.VMEM((hd, blk), F32),
                        pltpu.VMEM((blk, blk), F32),
                        pltpu.VMEM((DSA_HEADS, blk, blk), F32),
                        pltpu.VMEM((DSA_HEADS, blk, blk), BF16)],
        compiler_params=_cparams(("parallel", "arbitrary")),
        name="dsa_mixer",
    )(z_c, z_c, z_c, z_c, z_aux, z_aux, toe, *slabs)
    return (out, *[c.reshape(w.shape) for c, w in zip(converted, to_bf16)])


def _mixer_tail_kernel(h_ref, a_ref, b_ref, c_ref, d_ref, wout_ref, gxa_ref, wq_ref, k_ref, v_ref, wo_ref,
                       *refs, route):
    gw = GROUP_WIDTH
    x = h_ref[0]
    for gi, r in enumerate((a_ref, b_ref, c_ref, d_ref)):
        x = x + _dot(r[0], wout_ref[gi * gw:(gi + 1) * gw, :])

    hd = XA_HEADS * XA_DH
    q = _dot(_rms(x, gxa_ref[...]).astype(BF16), wq_ref[...]).astype(BF16)
    k = k_ref[0]
    v = v_ref[0]
    lane_h = lax.broadcasted_iota(jnp.int32, (1, hd), 1) // XA_DH
    o = jnp.zeros((x.shape[0], hd), F32)
    for h in range(XA_HEADS):
        s = _dot_nt(q, jnp.where(lane_h == h, k, 0.0).astype(BF16)) * (XA_DH ** -0.5)
        p = jnp.exp(s - jnp.max(s, axis=-1, keepdims=True))
        p = p / jnp.sum(p, axis=-1, keepdims=True)
        o = o + _dot(p.astype(BF16), jnp.where(lane_h == h, v, 0.0).astype(BF16))
    x = x + _dot(o.astype(BF16), wo_ref[...])

    if not route:
        o_ref, = refs
        o_ref[0] = x
        return
    gffn_ref, wr_ref, o_ref, hn_ref, gate_ref, cnt_ref = refs
    o_ref[0] = x
    hn = _rms(x, gffn_ref[...])
    hn_ref[0] = hn.astype(BF16)
    tm = hn.shape[0]
    lane = lax.broadcasted_iota(jnp.int32, (tm, GATE_LANES), 1)
    logits = jnp.where(lane < N_EXPERTS, _dot_3pass(hn, wr_ref[...]), -jnp.inf)
    m1 = jnp.max(logits, axis=-1, keepdims=True)
    i1 = jnp.min(jnp.where(logits == m1, lane, GATE_LANES), axis=-1, keepdims=True)
    rest = jnp.where(lane == i1, -jnp.inf, logits)
    m2 = jnp.max(rest, axis=-1, keepdims=True)
    i2 = jnp.min(jnp.where(rest == m2, lane, GATE_LANES), axis=-1, keepdims=True)
    e2 = jnp.exp(m2 - m1)
    g1 = 1.0 / (1.0 + e2)
    gates = jnp.where(lane == i1, g1, 0.0) + jnp.where(lane == i2, e2 * g1, 0.0)
    gates = jnp.where(lane == N_EXPERTS, i1.astype(F32), gates)
    gate_ref[0] = jnp.where(lane == N_EXPERTS + 1, i2.astype(F32), gates)
    sel = jnp.where((lane == i1) | (lane == i2), 1.0, 0.0)
    for c in range(tm // MOE_CHUNK):
        cnt_ref[c] = jnp.sum(sel[c * MOE_CHUNK:(c + 1) * MOE_CHUNK, :], axis=0, keepdims=True)


def mixer_tail(h3, mixers, w_out, g_xa, wq, k, v, wo, routing=None, tm=1024):
    bsz, t, d = h3.shape
    tm = min(tm, t)
    mlen = k.shape[1]
    hd = XA_HEADS * XA_DH
    gw = GROUP_WIDTH
    row = lambda b, i: (b, i, 0)
    fixed2 = lambda b, i: (0, 0)
    in_specs = ([pl.BlockSpec((1, tm, d), row)] + [pl.BlockSpec((1, tm, gw), row)] * 4
                + [pl.BlockSpec((4 * gw, d), fixed2), pl.BlockSpec((1, d), fixed2), pl.BlockSpec((d, hd), fixed2),
                   pl.BlockSpec((1, mlen, hd), lambda b, i: (b, 0, 0)),
                   pl.BlockSpec((1, mlen, hd), lambda b, i: (b, 0, 0)),
                   pl.BlockSpec((hd, d), fixed2)])
    args = [h3, *mixers, w_out, g_xa.reshape(1, d), wq, k, v, wo]
    out_specs = [pl.BlockSpec((1, tm, d), row)]
    out_shape = [jax.ShapeDtypeStruct((bsz, t, d), F32)]
    if routing is not None:
        assert tm % MOE_CHUNK == 0
        g_ffn, wr = routing
        in_specs += [pl.BlockSpec((1, d), fixed2), pl.BlockSpec((d, GATE_LANES), fixed2)]
        args += [g_ffn.reshape(1, d), jnp.zeros((d, GATE_LANES), F32).at[:, :N_EXPERTS].set(wr)]
        per_step = tm // MOE_CHUNK
        out_specs += [pl.BlockSpec((1, tm, d), row), pl.BlockSpec((1, tm, GATE_LANES), row),
                      pl.BlockSpec((per_step, 1, GATE_LANES), lambda b, i: (b * (t // tm) + i, 0, 0))]
        out_shape += [jax.ShapeDtypeStruct((bsz, t, d), BF16), jax.ShapeDtypeStruct((bsz, t, GATE_LANES), F32),
                      jax.ShapeDtypeStruct((bsz * t // MOE_CHUNK, 1, GATE_LANES), F32)]
    return pl.pallas_call(
        functools.partial(_mixer_tail_kernel, route=routing is not None),
        grid=(bsz, t // tm),
        in_specs=in_specs,
        out_specs=out_specs,
        out_shape=out_shape,
        compiler_params=_cparams(("parallel", "parallel")),
        name="mixer_tail",
    )(*args)


def _ffn_kernel(h_ref, g_ref, w1_ref, w3_ref, w2_ref, o_ref, *, tf):
    x = h_ref[...]
    hn = _rms(x, g_ref[...]).astype(BF16)
    acc = x
    for f0 in range(0, w1_ref.shape[1], tf):
        a = _dot(hn, w1_ref[:, f0:f0 + tf])
        b = _dot(hn, w3_ref[:, f0:f0 + tf])
        acc = acc + _dot((a * jax.nn.sigmoid(a) * b).astype(BF16), w2_ref[f0:f0 + tf, :])
    o_ref[...] = acc


def ffn(h, g, w1, w3, w2, tm=512, tf=704):
    m, d = h.shape
    tm = min(tm, m)
    nf = w1.shape[1]
    assert nf % tf == 0
    resident = lambda shape: pl.BlockSpec(shape, lambda i: (0, 0), pipeline_mode=pl.Buffered(1))
    return pl.pallas_call(
        functools.partial(_ffn_kernel, tf=tf),
        grid=(m // tm,),
        in_specs=[pl.BlockSpec((tm, d), lambda i: (i, 0)),
                  pl.BlockSpec((1, d), lambda i: (0, 0)),
                  resident((d, nf)), resident((d, nf)), resident((nf, d))],
        out_specs=pl.BlockSpec((tm, d), lambda i: (i, 0)),
        out_shape=jax.ShapeDtypeStruct((m, d), F32),
        compiler_params=_cparams(("parallel",)),
        name="ffn",
    )(h, g.reshape(1, d), w1, w3, w2)


GATE_LANES = 128
MOE_CHUNK = 256
MOE_ROWS = 128
MOE_TOKENS = 2048
MOE_ALIGN = BF16_TILE_ROWS


def _moe_kernel(cb_ref, h_ref, hn_ref, gate_ref, w1_ref, w3_ref, w2_ref, *refs, final):
    if final:
        gfin_ref, *refs = refs
    y_ref, rank_row, rk_ref, gs_ref, xs_ref, yacc_ref, rc_ref, gc_ref = refs
    t = pl.program_id(0)
    e = pl.program_id(1)
    f = pl.program_id(2)
    ts, d = hn_ref.shape
    ch, rb = MOE_CHUNK, MOE_ROWS
    win = 2 * rb
    n_ch = ts // ch
    lane = lax.broadcasted_iota(jnp.int32, (1, GATE_LANES), 1)

    def before(c):
        return cb_ref[(t * (n_ch + 1) + c) * N_EXPERTS + e]

    n_blocks = (before(n_ch) + rb - 1) // rb
    n_windows = (n_blocks * rb + win - 1) // win + 1

    def windows(c):
        lo, hi = before(c), before(c + 1)
        s0 = (lo // MOE_ALIGN) * MOE_ALIGN
        return s0, jnp.where(hi > lo, (hi - s0 + win - 1) // win, 0)

    spans = [windows(c) for c in range(n_ch)]
    single = functools.reduce(jnp.logical_and, [n_win <= 1 for _, n_win in spans])

    def for_each_window(visit):
        @pl.when(single)
        def _():
            for c, (s0, _) in enumerate(spans):
                visit(c, s0)

        @pl.when(jnp.logical_not(single))
        def _():
            for c, (s0, n_win) in enumerate(spans):
                def body(j, carry, c=c, s0=s0):
                    visit(c, s0 + j * win)
                    return carry
                lax.fori_loop(0, n_win, body, 0)

    @pl.when((e == 0) & (f == 0))
    def _():
        y_ref[...] = h_ref[...]
        strict_lower = (lax.broadcasted_iota(jnp.int32, (ch, ch), 1)
                        < lax.broadcasted_iota(jnp.int32, (ch, ch), 0)).astype(BF16)
        lane_f = lane.astype(F32)
        offs = jnp.ones((1, GATE_LANES), F32)
        for c in range(n_ch):
            rows = slice(c * ch, (c + 1) * ch)
            g = gate_ref[rows, :]
            sel = jnp.where((lane_f == g[:, N_EXPERTS:N_EXPERTS + 1])
                            | (lane_f == g[:, N_EXPERTS + 1:N_EXPERTS + 2]), 1.0, 0.0)
            r = _dot(strict_lower, sel.astype(BF16)) + offs
            r = jnp.where(sel > 0.0, r, 0.0)
            rank_row[:, rows] = jnp.transpose(r) - 1.0
            high = jnp.floor(r * (1.0 / 256.0))
            rk_ref[0, rows, :] = high.astype(BF16)
            rk_ref[1, rows, :] = (r - 256.0 * high).astype(BF16)
            for j, part in enumerate(_split_bf16(g, 3)):
                gs_ref[j, rows, :] = part
            offs = offs + jnp.sum(sel, axis=0, keepdims=True)

    @pl.when(f == 0)
    def _():
        pick = (lax.broadcasted_iota(jnp.int32, (GATE_LANES, GATE_LANES), 0) == e).astype(BF16)
        for c in range(n_ch):
            rows = slice(c * ch, (c + 1) * ch)
            rc_ref[rows, :] = 256.0 * _dot(rk_ref[0, rows, :], pick) + _dot(rk_ref[1, rows, :], pick) - 1.0
            gc_ref[rows, :] = (_dot(gs_ref[0, rows, :], pick) + _dot(gs_ref[1, rows, :], pick)
                               + _dot(gs_ref[2, rows, :], pick))

        def clear(j, carry):
            r0 = pl.multiple_of(j * win, win)
            xs_ref[pl.ds(r0, win), :] = jnp.zeros((win, d), BF16)
            yacc_ref[pl.ds(r0, win), :] = jnp.zeros((win, d), F32)
            return carry

        lax.fori_loop(0, n_windows, clear, 0)

        row_id = lax.broadcasted_iota(jnp.int32, (win, ch), 0)

        def gather_window(c, s):
            s = pl.multiple_of(s, MOE_ALIGN)
            ranks = rank_row[pl.ds(e, 1), c * ch:(c + 1) * ch]
            onehot = jnp.where(ranks == (s + row_id).astype(F32), 1.0, 0.0).astype(BF16)
            xs_ref[pl.ds(s, win), :] += _dot(onehot, hn_ref[c * ch:(c + 1) * ch, :]).astype(BF16)

        for_each_window(gather_window)

    def ffn_rows(r0, rows):
        x = xs_ref[pl.ds(r0, rows), :]
        a = _dot(x, w1_ref[0])
        g3 = _dot(x, w3_ref[0])
        yacc_ref[pl.ds(r0, rows), :] += _dot((a * jax.nn.sigmoid(a) * g3).astype(BF16), w2_ref[0])

    five = n_blocks == 5
    half_tail = before(n_ch) <= 4 * rb + rb // 2

    @pl.when(five & half_tail)
    def _():
        ffn_rows(0, 4 * rb + rb // 2)

    @pl.when(five & jnp.logical_not(half_tail))
    def _():
        ffn_rows(0, 5 * rb)

    @pl.when(jnp.logical_not(five))
    def _():
        n_quads = n_blocks // 4

        def ffn_quad(j, carry):
            ffn_rows(pl.multiple_of(j * (4 * rb), 4 * rb), 4 * rb)
            return carry

        lax.fori_loop(0, n_quads, ffn_quad, 0)
        tail = pl.multiple_of(n_quads * (4 * rb), 4 * rb)

        @pl.when((n_blocks & 2) != 0)
        def _():
            ffn_rows(tail, 2 * rb)

        @pl.when((n_blocks & 1) != 0)
        def _():
            ffn_rows(pl.multiple_of(tail + (n_blocks & 2) * rb, rb), rb)

    @pl.when(f == pl.num_programs(2) - 1)
    def _():
        def to_bf16(j, carry):
            r0 = pl.multiple_of(j * win, win)
            xs_ref[pl.ds(r0, win), :] = yacc_ref[pl.ds(r0, win), :].astype(BF16)
            return carry

        lax.fori_loop(0, n_windows, to_bf16, 0)

        lane_id = lax.broadcasted_iota(jnp.int32, (ch, rb), 1).astype(F32)

        def scatter_window(c, s):
            rows = slice(c * ch, (c + 1) * ch)
            s = pl.multiple_of(s, MOE_ALIGN)
            rank = rc_ref[rows, :] - s.astype(F32)
            onehot = jnp.concatenate([jnp.where(rank == lane_id, 1.0, 0.0),
                                      jnp.where(rank == lane_id + float(rb), 1.0, 0.0)],
                                     axis=1).astype(BF16)
            gate = jnp.concatenate([gc_ref[rows, :]] * (d // GATE_LANES), axis=1)
            y_ref[rows, :] += gate * _dot(onehot, xs_ref[pl.ds(s, win), :])

        for_each_window(scatter_window)

        if final:
            @pl.when(e == pl.num_programs(1) - 1)
            def _():
                y_ref[...] = _rms(y_ref[...], gfin_ref[...])


def moe(h, hn, gates, cnt, w1, w3, w2, final_gain=None, tf=896):
    m, d = hn.shape
    ts = min(MOE_TOKENS, m)
    n_exp, _, nf = w1.shape
    assert nf % tf == 0 and m % ts == 0 and ts % MOE_CHUNK == 0 and MOE_ROWS == GATE_LANES
    n_ch = ts // MOE_CHUNK
    counts = cnt.reshape(m // ts, n_ch, GATE_LANES)[:, :, :n_exp].astype(jnp.int32)
    before = jnp.concatenate([jnp.zeros((m // ts, 1, n_exp), jnp.int32), jnp.cumsum(counts, axis=1)], axis=1)
    per_tile = lambda width: pl.BlockSpec((ts, width), lambda t, e, f, cb: (t, 0), pipeline_mode=pl.Buffered(1))
    in_specs = [per_tile(d), per_tile(d), per_tile(GATE_LANES),
                pl.BlockSpec((1, d, tf), lambda t, e, f, cb: (e, 0, f)),
                pl.BlockSpec((1, d, tf), lambda t, e, f, cb: (e, 0, f)),
                pl.BlockSpec((1, tf, d), lambda t, e, f, cb: (e, f, 0))]
    args = [before.reshape(-1), h, hn, gates, w1, w3, w2]
    if final_gain is not None:
        in_specs.append(pl.BlockSpec((1, d), lambda t, e, f, cb: (0, 0)))
        args.append(final_gain.reshape(1, d))
    grid_spec = pltpu.PrefetchScalarGridSpec(
        num_scalar_prefetch=1,
        grid=(m // ts, n_exp, nf // tf),
        in_specs=in_specs,
        out_specs=per_tile(d),
        scratch_shapes=[pltpu.VMEM((GATE_LANES, ts), F32),
                        pltpu.VMEM((2, ts, GATE_LANES), BF16),
                        pltpu.VMEM((3, ts, GATE_LANES), BF16),
                        pltpu.VMEM((ts + 2 * MOE_ROWS, d), BF16),
                        pltpu.VMEM((ts + 2 * MOE_ROWS, d), F32),
                        pltpu.VMEM((ts, MOE_ROWS), F32),
                        pltpu.VMEM((ts, GATE_LANES), F32)])
    return pl.pallas_call(
        functools.partial(_moe_kernel, final=final_gain is not None),
        grid_spec=grid_spec,
        out_shape=jax.ShapeDtypeStruct((m, d), F32),
        compiler_params=_cparams(("parallel", "arbitrary", "arbitrary")),
        name="moe",
    )(*args)


def _final_norm_kernel(g_ref, x_ref, o_ref):
    o_ref[...] = _rms(x_ref[...], g_ref[...])


def final_rmsnorm(g, x, tm=1024):
    m, d = x.shape
    tm = min(tm, m)
    return pl.pallas_call(
        _final_norm_kernel,
        grid=(m // tm,),
        in_specs=[pl.BlockSpec((1, d), lambda i: (0, 0)), pl.BlockSpec((tm, d), lambda i: (i, 0))],
        out_specs=pl.BlockSpec((tm, d), lambda i: (i, 0)),
        out_shape=jax.ShapeDtypeStruct((m, d), F32),
        compiler_params=_cparams(("parallel",)),
        name="final_norm",
    )(g.reshape(1, d), x)


def _pad_cols(w, width):
    return jnp.pad(w, ((0, 0), (0, width - w.shape[1])))


def _in_proj_weight(w_in):
    a_end = 2 * GLA_HEADS * GLA_DK + 2 * GROUP_WIDTH + GLA_RANK
    b_end = a_end + GROUP_WIDTH
    c_end = b_end + 3 * GROUP_WIDTH + IDX_HEADS * IDX_DIM + IDX_DIM + IDX_HEADS
    assert w_in.shape[1] == c_end + 2 * GROUP_WIDTH
    return jnp.concatenate([_pad_cols(w_in[:, :a_end], A_WIDTH), w_in[:, a_end:b_end],
                            _pad_cols(w_in[:, b_end:c_end], C_MAIN + C_AUX), w_in[:, c_end:]], axis=1).astype(BF16)


def hybrid_layer(h, mem2, toe, p, bsz, t, mixer_weights, routing=None):
    m, d = h.shape
    z_a, z_b, z_c, z_aux, z_d = norm_matmul(h, p["norm_mix"], _in_proj_weight(p["w_in"]),
                                            (A_WIDTH, B_WIDTH, C_MAIN, C_AUX, D_WIDTH),
                                            (F32, F32, BF16, F32, F32))
    o_a = gla_mixer(z_a.reshape(bsz, t, A_WIDTH), p["gla_wa2"], p["gla_ba"], p["gla_norm"])
    o_b = pool_mixer(z_b.reshape(bsz, t, B_WIDTH), p["pool_w"], p["pool_b"].reshape(-1), p["pool_scale"])
    o_c, *weights = dsa_mixer(z_c.reshape(bsz, t, C_MAIN), z_aux.reshape(bsz, t, C_AUX), toe, mixer_weights)
    o_d = sgu_mixer(z_d.reshape(bsz, t, D_WIDTH), p["sgu_ln_g"], p["sgu_ln_b"], p["sgu_w"], p["sgu_b"])
    hd = XA_HEADS * XA_DH
    k, v = norm_matmul(mem2, p["norm_mem"], p["xa_wkv"].astype(BF16), (hd, hd))
    mlen = mem2.shape[0] // bsz
    outs = mixer_tail(h.reshape(bsz, t, d), (o_a, o_b, o_c, o_d), p["w_out"].astype(BF16), p["norm_xa"],
                      p["xa_wq"].astype(BF16), k.reshape(bsz, mlen, hd), v.reshape(bsz, mlen, hd),
                      p["xa_wo"].astype(BF16), routing)
    if routing is None:
        return weights, outs[0].reshape(m, d)
    h3, hn, gates, counts = outs
    return weights, h3.reshape(m, d), hn.reshape(m, d), gates.reshape(m, GATE_LANES), counts


def kernel(x, mem, rel_bias, final_norm, norm_mix, w_in, gla_wa2, gla_ba, gla_norm, pool_w, pool_b,
           pool_scale, sgu_ln_g, sgu_ln_b, sgu_w, sgu_b, w_out, norm_xa, norm_mem, xa_wq, xa_wkv, xa_wo,
           norm_ffn, ffn_w1, ffn_w3, ffn_w2, router, moe_w1, moe_w3, moe_w2):
    bsz, t, d = x.shape
    depth = norm_mix.shape[0]
    h = x.reshape(bsz * t, d)
    mem2 = mem.reshape(-1, d)
    toe = rel_bias_tables(rel_bias)
    normed = False
    for i in range(depth):
        p = dict(norm_mix=norm_mix[i], w_in=w_in[i], gla_wa2=gla_wa2[i], gla_ba=gla_ba[i],
                 gla_norm=gla_norm[i], pool_w=pool_w[i], pool_b=pool_b[i], pool_scale=pool_scale[i],
                 sgu_ln_g=sgu_ln_g[i], sgu_ln_b=sgu_ln_b[i], sgu_w=sgu_w[i], sgu_b=sgu_b[i],
                 w_out=w_out[i], norm_xa=norm_xa[i], norm_mem=norm_mem[i], xa_wq=xa_wq[i],
                 xa_wkv=xa_wkv[i], xa_wo=xa_wo[i])
        j = i // 2
        if i % 2 == 0:
            (w1, w3, w2), h = hybrid_layer(h, mem2, toe, p, bsz, t, (ffn_w1[j], ffn_w3[j], ffn_w2[j]))
            h = ffn(h, norm_ffn[i], w1, w3, w2)
        else:
            (w1, w3, w2), h, hn, gates, counts = hybrid_layer(h, mem2, toe, p, bsz, t,
                                                              (moe_w1[j], moe_w3[j], moe_w2[j]),
                                                              routing=(norm_ffn[i], router[j]))
            normed = i == depth - 1
            h = moe(h, hn, gates, counts, w1, w3, w2, final_gain=final_norm if normed else None)
    if not normed:
        h = final_rmsnorm(final_norm, h)
    return h.reshape(bsz, t, d)
```

```python
import functools
import math

import jax
import jax.numpy as jnp
import numpy as np
from jax import lax
from jax.experimental import pallas as pl
from jax.experimental.pallas import tpu as pltpu

F32 = jnp.float32
BF16 = jnp.bfloat16
EPS = 1e-6

LANES = 128
BF16_TILE_ROWS = 16
VMEM_LIMIT = 56 * 1024 * 1024

GROUP_WIDTH = 256

GLA_HEADS = 4
GLA_DV = 64
GLA_DK = 32
GLA_RANK = 16
GLA_TAU = 16.0
GLA_CHUNK = 64
GLA_GROUP = 16

POOL_WINDOWS = (2, 4, 8, 16)
POOL_CG = 64

DSA_HEADS = 4
DSA_DH = 64
IDX_HEADS = 8
IDX_DIM = 32
DSA_TOPK_MAX = 256
DSA_BLOCK = 256
DSA_SUB = 64
DSA_HEAD_GROUPS = ((0, 1, 2, 3),)

SGU_GROUPS = 4
SGU_CHUNK = 128
SGU_CG = 64

REL_BUCKETS = 32
REL_MAX_DIST = 128

XA_HEADS = 4
XA_DH = 64

N_EXPERTS = 8

A_WIDTH = 896
B_WIDTH = 256
C_MAIN = 1024
C_AUX = LANES
D_WIDTH = 512

MIXER_DTYPE = BF16
INT_MIN = -(2 ** 31)
NEG_BIG = -1e30


def _cparams(sem):
    return pltpu.CompilerParams(dimension_semantics=sem, vmem_limit_bytes=VMEM_LIMIT)


def _dot(a, b):
    return jnp.dot(a, b, preferred_element_type=F32)


def _dot_nt(a, b):
    return lax.dot_general(a, b, (((1,), (1,)), ((), ())), preferred_element_type=F32)


def _dot_tn(a, b):
    return lax.dot_general(a, b, (((0,), (0,)), ((), ())), preferred_element_type=F32)


def _split_bf16(x, terms):
    parts = []
    for _ in range(terms):
        p = x.astype(BF16)
        parts.append(p)
        x = x - p.astype(F32)
    return parts


def _dot_exact_rhs(a, b, terms):
    out = None
    for p in _split_bf16(a, terms):
        d = _dot(p, b)
        out = d if out is None else out + d
    return out


def _dot_exact_lhs(a, b, terms):
    out = None
    for p in _split_bf16(b, terms):
        d = _dot(a, p)
        out = d if out is None else out + d
    return out


def _dot_3pass(a, b):
    a_hi, a_lo = _split_bf16(a, 2)
    b_hi, b_lo = _split_bf16(b, 2)
    return _dot(a_hi, b_hi) + (_dot(a_hi, b_lo) + _dot(a_lo, b_hi))


def _rms(x, g):
    return x * lax.rsqrt(jnp.mean(x * x, axis=-1, keepdims=True) + EPS) * g


def _norm_matmul_kernel(h_ref, g_ref, w_ref, *out_refs, widths):
    hb = _rms(h_ref[...], g_ref[...]).astype(BF16)
    off = 0
    for o_ref, wd in zip(out_refs, widths):
        o_ref[...] = _dot(hb, w_ref[:, off:off + wd]).astype(o_ref.dtype)
        off += wd


def norm_matmul(h, g, w, widths, dtypes=None, tm=512):
    dtypes = dtypes or (F32,) * len(widths)
    m, d = h.shape
    tm = min(tm, m)
    n = sum(widths)
    return pl.pallas_call(
        functools.partial(_norm_matmul_kernel, widths=widths),
        grid=(m // tm,),
        in_specs=[pl.BlockSpec((tm, d), lambda i: (i, 0)),
                  pl.BlockSpec((1, d), lambda i: (0, 0)),
                  pl.BlockSpec((d, n), lambda i: (0, 0))],
        out_specs=[pl.BlockSpec((tm, wd), lambda i: (i, 0)) for wd in widths],
        out_shape=[jax.ShapeDtypeStruct((m, wd), dt) for wd, dt in zip(widths, dtypes)],
        compiler_params=_cparams(("parallel",)),
        name="norm_matmul",
    )(h, g.reshape(1, d), w)


def _log_sigmoid(x):
    return jnp.minimum(x, 0.0) - jnp.log1p(jnp.exp(-jnp.abs(x)))


def _gla_kernel(z_ref, wa2_ref, ba_ref, ng_ref, o_ref, s_ref, *, group, n_groups):
    c = GLA_CHUNK
    grp = group * c
    hk = GLA_HEADS * GLA_DK
    hv = GLA_HEADS * GLA_DV
    s_ref[...] = jnp.zeros_like(s_ref)

    head_k = lax.broadcasted_iota(jnp.int32, (1, hk), 1) // GLA_DK
    head_v = lax.broadcasted_iota(jnp.int32, (1, hv), 1) // GLA_DV
    cum = min(grp, 256)
    g_row = lax.broadcasted_iota(jnp.int32, (cum, cum), 0)
    g_col = lax.broadcasted_iota(jnp.int32, (cum, cum), 1)
    tril = (((g_row // c) == (g_col // c)) & (g_col <= g_row)).astype(BF16)
    causal4 = (lax.broadcasted_iota(jnp.int32, (GLA_HEADS * c, c), 1)
               <= lax.broadcasted_iota(jnp.int32, (GLA_HEADS * c, c), 0) % c)
    state_mask = (lax.broadcasted_iota(jnp.int32, (hk, hv), 0) // GLA_DK
                  == lax.broadcasted_iota(jnp.int32, (hk, hv), 1) // GLA_DV)
    norm_mat = jnp.where(lax.broadcasted_iota(jnp.int32, (hv, hv), 0) // GLA_DV
                         == lax.broadcasted_iota(jnp.int32, (hv, hv), 1) // GLA_DV,
                         1.0 / GLA_DV, 0.0).astype(BF16)
    wa2 = wa2_ref[...]
    ba = ba_ref[...]
    ng = ng_ref[...]

    def body(n, carry):
        r0 = pl.multiple_of(n * grp, grp)
        z = z_ref[0, pl.ds(r0, grp), :]
        q, k, v, g, lr = z[:, 0:128], z[:, 128:256], z[:, 256:512], z[:, 512:768], z[:, 768:896]
        log_a = _log_sigmoid(_dot_3pass(lr, wa2) + ba) / GLA_TAU
        b = jnp.concatenate([_dot_exact_lhs(tril, log_a[r:r + cum], 3) for r in range(0, grp, cum)], axis=0)
        b_end = jnp.concatenate([jnp.broadcast_to(b[(ci + 1) * c - 1:(ci + 1) * c, :], (c, hk))
                                 for ci in range(group)], axis=0)
        q_t = q * (GLA_DK ** -0.5) * jnp.exp(b)
        q_tb = q_t.astype(BF16)
        k_t = (k * jnp.exp(-b)).astype(BF16)
        k_dec = (k * jnp.exp(b_end - b)).astype(BF16)
        vb = v.astype(BF16)
        outs = []
        for ci in range(group):
            rows = slice(ci * c, (ci + 1) * c)
            q4 = jnp.concatenate([jnp.where(head_k == h, q_t[rows], 0.0) for h in range(GLA_HEADS)],
                                 axis=0).astype(BF16)
            att = jnp.where(causal4, _dot_nt(q4, k_t[rows]), 0.0)
            r = _dot(att.astype(BF16), vb[rows])
            o = _dot(q_tb[rows], s_ref[...].astype(BF16))
            for h in range(GLA_HEADS):
                o = o + jnp.where(head_v == h, r[h * c:(h + 1) * c, :], 0.0)
            outs.append(o)
            kv = jnp.where(state_mask, _dot_tn(k_dec[rows], vb[rows]), 0.0)
            last = b_end[ci * c:ci * c + 1, :]
            dec = jnp.exp(jnp.transpose(jnp.broadcast_to(last, (hk, hk))))
            s_ref[...] = s_ref[...] * jnp.concatenate([dec, dec], axis=1) + kv
        o = jnp.concatenate(outs, axis=0)
        o = o * lax.rsqrt(_dot_exact_rhs(o * o, norm_mat, 2) + EPS) * ng
        o_ref[0, pl.ds(r0, grp), :] = (o * (g * jax.nn.sigmoid(g))).astype(o_ref.dtype)
        return carry

    lax.fori_loop(0, n_groups, body, 0)


def gla_mixer(z_a, wa2, ba, norm_g):
    bsz, t, _ = z_a.shape
    hk = GLA_HEADS * GLA_DK
    wa2p = jnp.zeros((LANES, hk), F32).at[:GLA_RANK].set(wa2)
    n_chunks = t // GLA_CHUNK
    group = math.gcd(GLA_GROUP, n_chunks)
    return pl.pallas_call(
        functools.partial(_gla_kernel, group=group, n_groups=n_chunks // group),
        grid=(bsz,),
        in_specs=[pl.BlockSpec((1, t, A_WIDTH), lambda b: (b, 0, 0)),
                  pl.BlockSpec((LANES, hk), lambda b: (0, 0)),
                  pl.BlockSpec((1, hk), lambda b: (0, 0)),
                  pl.BlockSpec((1, GROUP_WIDTH), lambda b: (0, 0))],
        out_specs=pl.BlockSpec((1, t, GROUP_WIDTH), lambda b: (b, 0, 0)),
        out_shape=jax.ShapeDtypeStruct((bsz, t, GROUP_WIDTH), MIXER_DTYPE),
        scratch_shapes=[pltpu.VMEM((hk, GROUP_WIDTH), F32)],
        compiler_params=_cparams(("parallel",)),
        name="gla_mixer",
    )(z_a, wa2p, ba.reshape(1, hk), norm_g.reshape(1, GROUP_WIDTH))


def _pool_kernel(u_ref, w_ref, b_ref, sc_ref, o_ref):
    u = u_ref[0]
    t, gw = u.shape
    row = lax.broadcasted_iota(jnp.int32, (t, gw), 0)
    grp = lax.broadcasted_iota(jnp.int32, (t, gw), 1) // POOL_CG

    def shifted(x, k):
        return jnp.where(row >= k, pltpu.roll(x, k, axis=0), 0.0)

    s = u
    p = jnp.zeros_like(u)
    for gi, win in enumerate(POOL_WINDOWS):
        half = win // 2
        s = s + shifted(s, half)
        cnt = jnp.minimum(row + 1, win).astype(F32)
        p = jnp.where(grp == gi, s / cnt - u, p)
    y = _dot(p.astype(BF16), w_ref[...]) + b_ref[...]
    o_ref[0] = (y * sc_ref[...]).astype(o_ref.dtype)


def pool_mixer(z_b, w, b, scale):
    assert POOL_WINDOWS == (2, 4, 8, 16)
    bsz, t, gw = z_b.shape
    w_bd = jnp.zeros((gw, gw), F32)
    for gi in range(len(POOL_WINDOWS)):
        w_bd = w_bd.at[gi * POOL_CG:(gi + 1) * POOL_CG, gi * POOL_CG:(gi + 1) * POOL_CG].set(w[gi])
    return pl.pallas_call(
        _pool_kernel,
        grid=(bsz,),
        in_specs=[pl.BlockSpec((1, t, gw), lambda i: (i, 0, 0)),
                  pl.BlockSpec((gw, gw), lambda i: (0, 0)),
                  pl.BlockSpec((1, gw), lambda i: (0, 0)),
                  pl.BlockSpec((1, gw), lambda i: (0, 0))],
        out_specs=pl.BlockSpec((1, t, gw), lambda i: (i, 0, 0)),
        out_shape=jax.ShapeDtypeStruct((bsz, t, gw), MIXER_DTYPE),
        compiler_params=_cparams(("parallel",)),
        name="pool_mixer",
    )(z_b, w_bd.astype(BF16), b.reshape(1, gw), scale.reshape(1, gw))


def _sgu_kernel(z_ref, lg_ref, lb_ref, w_ref, bm_ref, o_ref, *, chunks):
    c = SGU_CHUNK
    gw = GROUP_WIDTH
    rows = SGU_GROUPS * c
    tri = (lax.broadcasted_iota(jnp.int32, (rows, c), 1)
           <= lax.broadcasted_iota(jnp.int32, (rows, c), 0) % c)
    ws = jnp.where(tri, w_ref[...], 0.0).astype(BF16)
    grp = lax.broadcasted_iota(jnp.int32, (1, gw), 1) // SGU_CG
    for ci in range(chunks):
        z = jax.nn.gelu(z_ref[0, ci * c:(ci + 1) * c, :], approximate=True)
        u, v = z[:, :gw], z[:, gw:]
        mu = jnp.mean(v, axis=-1, keepdims=True)
        var = jnp.mean(jnp.square(v - mu), axis=-1, keepdims=True)
        vn = (v - mu) * lax.rsqrt(var + EPS) * lg_ref[...] + lb_ref[...]
        r = _dot(ws, vn.astype(BF16))
        mixed = bm_ref[...]
        for g in range(SGU_GROUPS):
            mixed = mixed + jnp.where(grp == g, r[g * c:(g + 1) * c, :], 0.0)
        o_ref[0, ci * c:(ci + 1) * c, :] = (u * mixed).astype(o_ref.dtype)


def sgu_mixer(z_d, ln_g, ln_b, w_s, b_s, chunks=4):
    bsz, t, _ = z_d.shape
    gw = GROUP_WIDTH
    tt = chunks * SGU_CHUNK
    bias = jnp.repeat(b_s.T, SGU_CG, axis=1)
    return pl.pallas_call(
        functools.partial(_sgu_kernel, chunks=chunks),
        grid=(bsz, t // tt),
        in_specs=[pl.BlockSpec((1, tt, 2 * gw), lambda b, i: (b, i, 0)),
                  pl.BlockSpec((1, gw), lambda b, i: (0, 0)),
                  pl.BlockSpec((1, gw), lambda b, i: (0, 0)),
                  pl.BlockSpec((SGU_GROUPS * SGU_CHUNK, SGU_CHUNK), lambda b, i: (0, 0)),
                  pl.BlockSpec((SGU_CHUNK, gw), lambda b, i: (0, 0))],
        out_specs=pl.BlockSpec((1, tt, gw), lambda b, i: (b, i, 0)),
        out_shape=jax.ShapeDtypeStruct((bsz, t, gw), MIXER_DTYPE),
        compiler_params=_cparams(("parallel", "parallel")),
        name="sgu_mixer",
    )(z_d, ln_g.reshape(1, gw), ln_b.reshape(1, gw),
      w_s.reshape(SGU_GROUPS * SGU_CHUNK, SGU_CHUNK), bias)


def _bucket_table():
    assert REL_MAX_DIST <= DSA_BLOCK + 1
    s = np.arange(DSA_BLOCK)[:, None]
    t = np.arange(DSA_BLOCK)[None, :]
    dist = np.stack([t - s, DSA_BLOCK + t - s, 2 * DSA_BLOCK + t - s])
    n = np.maximum(dist, 0)
    max_exact = REL_BUCKETS // 2
    nf = np.maximum(n, 1).astype(np.float32)
    large = max_exact + (np.log(nf / np.float32(max_exact)) / np.float32(math.log(REL_MAX_DIST / max_exact))
                         * np.float32(REL_BUCKETS - max_exact)).astype(np.int32)
    return np.where(n < max_exact, n, np.minimum(large, REL_BUCKETS - 1)).astype(np.int32)


def _bias_table_kernel(rb_ref, bucket_ref, o_ref):
    for back in range(3):
        bucket = bucket_ref[back]
        for h in range(DSA_HEADS):
            acc = jnp.zeros(bucket.shape, F32)
            for b in range(REL_BUCKETS):
                acc = jnp.where(bucket == b, rb_ref[b * DSA_HEADS + h], acc)
            o_ref[back, h] = acc


def rel_bias_tables(rel_bias):
    blk = DSA_BLOCK
    return pl.pallas_call(
        _bias_table_kernel,
        in_specs=[pl.BlockSpec(memory_space=pltpu.SMEM),
                  pl.BlockSpec((3, blk, blk), lambda: (0, 0, 0))],
        out_specs=pl.BlockSpec((3, DSA_HEADS, blk, blk), lambda: (0, 0, 0, 0)),
        out_shape=jax.ShapeDtypeStruct((3, DSA_HEADS, blk, blk), F32),
        name="rel_bias_tables",
    )(rel_bias.reshape(-1), jnp.asarray(_bucket_table()))


def _dsa_kernel(q_ref, k_ref, v_ref, qi_ref, kw_ref, qw_ref, toe_ref, *refs, topk, n_blocks, idx_bits, n_cast):
    for src, dst in zip(refs[:n_cast], refs[n_cast + 1:2 * n_cast + 1]):
        dst[...] = src[...].astype(BF16)
    o_ref = refs[n_cast]
    kpl_ref, vt_ref, keys_ref, khi_ref, klo_ref, acc_ref, am_ref, lg_ref, p_ref = refs[2 * n_cast + 1:]
    blk = DSA_BLOCK
    sub = DSA_SUB
    i = pl.program_id(1)
    hd = DSA_HEADS * DSA_DH
    heads_per_half = LANES // IDX_DIM

    @pl.when(i == 0)
    def _():
        lane = lax.broadcasted_iota(jnp.int32, (blk, LANES), 1)

        def build(kb, c):
            r0 = pl.multiple_of(kb * blk, blk)
            ki = jnp.where(lane < IDX_DIM, kw_ref[0, pl.ds(r0, blk), :], 0.0)
            for j in range(heads_per_half):
                kpl_ref[kb, j] = (ki if j == 0 else pltpu.roll(ki, j * IDX_DIM, axis=1)).astype(BF16)
            vt_ref[kb] = jnp.transpose(v_ref[0, pl.ds(r0, blk), :].astype(F32)).astype(BF16)
            return c

        lax.fori_loop(0, n_blocks, build, 0)

    s_loc = lax.broadcasted_iota(jnp.int32, (blk, blk), 0)
    t_loc = lax.broadcasted_iota(jnp.int32, (blk, blk), 1)
    n_vis = i + 1

    qi_t = jnp.transpose(qi_ref[0].astype(F32)).astype(BF16)
    qi_halves = [qi_t[:LANES, :], qi_t[LANES:, :]]
    w_t = jnp.transpose(qw_ref[0])
    w_rows = [w_t[IDX_DIM + h:IDX_DIM + h + 1, :] * (IDX_HEADS ** -0.5) * (IDX_DIM ** -0.5)
              for h in range(IDX_HEADS)]
    s_sub = lax.broadcasted_iota(jnp.int32, (sub, blk), 0)
    t_sub = lax.broadcasted_iota(jnp.int32, (sub, blk), 1)

    def score_body(kb, c):
        for ci in range(blk // sub):
            rows = slice(ci * sub, (ci + 1) * sub)
            sc = jnp.zeros((sub, blk), F32)
            for half in range(2):
                for j in range(heads_per_half):
                    d = _dot(kpl_ref[kb, j, rows, :], qi_halves[half])
                    sc = sc + jnp.maximum(d, 0.0) * w_rows[half * heads_per_half + j]
            sc = jnp.where(sc == 0.0, 0.0, sc)
            bits = pltpu.bitcast(sc, jnp.int32)
            key = jnp.where(bits < 0, bits ^ jnp.int32(0x7FFFFFFF), bits)
            vis = (kb < i) | (s_sub + ci * sub <= t_sub)
            key = jnp.where(vis, key, jnp.int32(INT_MIN))
            keys_ref[kb, rows, :] = key
            khi_ref[kb, rows, :] = lax.shift_right_arithmetic(key, 16).astype(jnp.int16)
            klo_ref[kb, rows, :] = ((key & 0xFFFF) - 2 ** 15).astype(jnp.int16)
        return c

    lax.fori_loop(0, n_vis, score_body, 0)

    def count(pred):
        def body(kb, acc):
            hit = jnp.where(pred(keys_ref[kb], kb), 1.0, 0.0)
            return acc + jnp.sum(hit.reshape(blk // 32, 32, blk), axis=0)
        acc = lax.fori_loop(0, n_vis, body, jnp.zeros((32, blk), F32))
        return jnp.sum(acc, axis=0, keepdims=True)

    def count16(ref, cand):
        cand = cand.astype(jnp.int16)

        def body(kb, acc):
            hit = jnp.where(ref[kb] >= cand, jnp.int16(1), jnp.int16(0))
            for j in range(blk // 32):
                acc = acc + hit[j * 32:(j + 1) * 32]
            return acc
        acc = lax.fori_loop(0, n_vis, body, jnp.zeros((32, blk), jnp.int16))
        return jnp.sum(acc.astype(jnp.int32).astype(F32), axis=0, keepdims=True)

    def search16(ref, offset):
        lowest = jnp.full((1, blk), -(2 ** 15), jnp.int32)
        base = jnp.where(offset + count16(ref, jnp.zeros((1, blk), jnp.int32)) >= kf, 0, lowest)

        def bit_body(it, base):
            cand = base | lax.shift_left(jnp.int32(1), 14 - it)
            return jnp.where(offset + count16(ref, cand) >= kf, cand, base)
        return lax.fori_loop(0, 15, bit_body, base)

    kf = float(topk)
    zero = jnp.zeros((1, blk), jnp.int32)
    thr_hi = search16(khi_ref, 0.0)
    thr_hi16 = thr_hi.astype(jnp.int16)

    def low_body(kb, above):
        hi = khi_ref[kb]
        klo_ref[kb] = jnp.where(hi == thr_hi16, klo_ref[kb], jnp.int16(-(2 ** 15)))
        hit = jnp.where(hi > thr_hi16, jnp.int16(1), jnp.int16(0))
        for j in range(blk // 32):
            above = above + hit[j * 32:(j + 1) * 32]
        return above

    above = lax.fori_loop(0, n_vis, low_body, jnp.zeros((32, blk), jnp.int16))
    n_above = jnp.sum(above.astype(jnp.int32).astype(F32), axis=0, keepdims=True)
    thr_lo = search16(klo_ref, n_above)
    thr = lax.shift_left(thr_hi, 16) | (thr_lo + 2 ** 15)
    thr_sel = jnp.maximum(thr, jnp.int32(INT_MIN + 1))

    n_ge = count(lambda key, kb: key >= thr)
    excess = jnp.where((n_ge > kf) & (thr > jnp.int32(INT_MIN)), 1.0, 0.0)

    def tie_search():
        need = kf - count(lambda key, kb: key > thr)

        def tie_body(it, j):
            cand = j | lax.shift_left(jnp.int32(1), idx_bits - 1 - it)
            below = count(lambda key, kb: (key == thr) & (kb * blk + s_loc < cand))
            return jnp.where(below < need, cand, j)
        return lax.fori_loop(0, idx_bits, tie_body, zero)

    last = lax.cond(jnp.max(excess) > 0.0, tie_search, lambda: jnp.full((1, blk), 2 ** 30, jnp.int32))

    assert DSA_DH ** -0.5 == 0.125
    q_t = jnp.transpose(q_ref[0].astype(F32) * (DSA_DH ** -0.5))
    row_h = lax.broadcasted_iota(jnp.int32, (hd, 1), 0) // DSA_DH
    q_heads = [jnp.where(row_h == h, q_t, 0.0).astype(BF16) for h in range(DSA_HEADS)]
    acc_ref[...] = jnp.zeros_like(acc_ref)
    n_sub = blk // sub

    def att_body(kb, carry):
        ms, ls = carry
        back = jnp.minimum(i - kb, 2)
        for ci in range(n_sub):
            rows = slice(ci * sub, (ci + 1) * sub)
            key = keys_ref[kb, rows, :]
            sel = (key > thr_sel) | ((key == thr_sel) & (kb * blk + ci * sub + s_sub <= last))
            am_ref[rows, :] = jnp.where(sel, 0.0, NEG_BIG)
        new_ms, new_ls = list(ms), list(ls)
        for heads in DSA_HEAD_GROUPS:
            alphas = {}
            for h in heads:
                pm = jnp.full((8, blk), NEG_BIG, F32)
                for ci in range(n_sub):
                    rows = slice(ci * sub, (ci + 1) * sub)
                    k_rows = k_ref[0, pl.ds(pl.multiple_of(kb * blk + ci * sub, sub), sub), :]
                    lg = _dot(k_rows, q_heads[h]) + toe_ref[back, h, rows, :] + am_ref[rows, :]
                    lg_ref[h, rows, :] = lg
                    pm = jnp.maximum(pm, jnp.max(lg.reshape(sub // 8, 8, blk), axis=0))
                new_ms[h] = jnp.maximum(ms[h], jnp.max(pm, axis=0, keepdims=True))
                alphas[h] = jnp.exp(ms[h] - new_ms[h])
            for h in heads:
                ps = jnp.zeros((8, blk), F32)
                for ci in range(n_sub):
                    rows = slice(ci * sub, (ci + 1) * sub)
                    p = jnp.exp(lg_ref[h, rows, :] - new_ms[h])
                    ps = ps + jnp.sum(p.reshape(sub // 8, 8, blk), axis=0)
                    p_ref[h, rows, :] = p.astype(BF16)
                new_ls[h] = ls[h] * alphas[h] + jnp.sum(ps, axis=0, keepdims=True)
            for h in heads:
                hrows = slice(h * DSA_DH, (h + 1) * DSA_DH)
                acc_ref[hrows, :] = acc_ref[hrows, :] * alphas[h] + _dot(vt_ref[kb, hrows, :], p_ref[h])
        return tuple(new_ms), tuple(new_ls)

    init = (tuple(jnp.full((1, blk), 0.01 * NEG_BIG, F32) for _ in range(DSA_HEADS)),
            tuple(jnp.zeros((1, blk), F32) for _ in range(DSA_HEADS)))
    _, ls = lax.fori_loop(0, n_vis, att_body, init)
    for h in range(DSA_HEADS):
        rows = slice(h * DSA_DH, (h + 1) * DSA_DH)
        acc_ref[rows, :] = acc_ref[rows, :] * (1.0 / ls[h])
    o_ref[0] = jnp.transpose(acc_ref[...]).astype(o_ref.dtype)


def _slab_view(w, steps):
    rows = math.prod(w.shape[:-1])
    n = max(k for k in range(1, steps + 1) if rows % (k * BF16_TILE_ROWS) == 0)
    return w.reshape(n, rows // n, w.shape[-1])


def dsa_mixer(z_c, z_aux, toe, to_bf16=()):
    bsz, t, _ = z_c.shape
    blk = DSA_BLOCK
    n_blocks = t // blk
    assert t % blk == 0
    topk = min(DSA_TOPK_MAX, t // 4)
    hd = DSA_HEADS * DSA_DH
    steps = bsz * n_blocks
    slabs = [_slab_view(w, steps) for w in to_bf16]
    slab_specs = [pl.BlockSpec((1,) + s.shape[1:],
                               lambda b, i, n=s.shape[0]: (jnp.minimum(b * n_blocks + i, n - 1), 0, 0))
                  for s in slabs]
    kernel = functools.partial(_dsa_kernel, topk=topk, n_blocks=n_blocks,
                               idx_bits=max(1, (t - 1).bit_length()), n_cast=len(slabs))
    out, *converted = pl.pallas_call(
        kernel,
        grid=(bsz, n_blocks),
        in_specs=[pl.BlockSpec((1, blk, hd), lambda b, i: (b, i, 0)),
                  pl.BlockSpec((1, t, hd), lambda b, i: (b, 0, 1)),
                  pl.BlockSpec((1, t, hd), lambda b, i: (b, 0, 2)),
                  pl.BlockSpec((1, blk, hd), lambda b, i: (b, i, 3)),
                  pl.BlockSpec((1, t, C_AUX), lambda b, i: (b, 0, 0)),
                  pl.BlockSpec((1, blk, C_AUX), lambda b, i: (b, i, 0)),
                  pl.BlockSpec((3, DSA_HEADS, blk, blk), lambda b, i: (0, 0, 0, 0))] + slab_specs,
        out_specs=[pl.BlockSpec((1, blk, hd), lambda b, i: (b, i, 0))] + slab_specs,
        out_shape=[jax.ShapeDtypeStruct((bsz, t, hd), MIXER_DTYPE)]
                  + [jax.ShapeDtypeStruct(s.shape, BF16) for s in slabs],
        scratch_shapes=[pltpu.VMEM((n_blocks, LANES // IDX_DIM, blk, LANES), BF16),
                        pltpu.VMEM((n_blocks, hd, blk), BF16),
                        pltpu.VMEM((n_blocks, blk, blk), jnp.int32),
                        pltpu.VMEM((n_blocks, blk, blk), jnp.int16),
                        pltpu.VMEM((n_blocks, blk, blk), jnp.int16),
                        pltpu.VMEM((hd, blk), F32),
                        pltpu.VMEM((blk, blk), F32),
                        pltpu.VMEM((DSA_HEADS, blk, blk), F32),
                        pltpu.VMEM((DSA_HEADS, blk, blk), BF16)],
        compiler_params=_cparams(("parallel", "arbitrary")),
        name="dsa_mixer",
    )(z_c, z_c, z_c, z_c, z_aux, z_aux, toe, *slabs)
    return (out, *[c.reshape(w.shape) for c, w in zip(converted, to_bf16)])


def _mixer_tail_kernel(h_ref, a_ref, b_ref, c_ref, d_ref, wout_ref, gxa_ref, wq_ref, k_ref, v_ref, wo_ref,
                       *refs, route):
    gw = GROUP_WIDTH
    x = h_ref[0]
    for gi, r in enumerate((a_ref, b_ref, c_ref, d_ref)):
        x = x + _dot(r[0], wout_ref[gi * gw:(gi + 1) * gw, :])

    hd = XA_HEADS * XA_DH
    q = _dot(_rms(x, gxa_ref[...]).astype(BF16), wq_ref[...]).astype(BF16)
    k = k_ref[0]
    v = v_ref[0]
    lane_h = lax.broadcasted_iota(jnp.int32, (1, hd), 1) // XA_DH
    o = jnp.zeros((x.shape[0], hd), F32)
    for h in range(XA_HEADS):
        s = _dot_nt(q, jnp.where(lane_h == h, k, 0.0).astype(BF16)) * (XA_DH ** -0.5)
        p = jnp.exp(s - jnp.max(s, axis=-1, keepdims=True))
        p = p / jnp.sum(p, axis=-1, keepdims=True)
        o = o + _dot(p.astype(BF16), jnp.where(lane_h == h, v, 0.0).astype(BF16))
    x = x + _dot(o.astype(BF16), wo_ref[...])

    if not route:
        o_ref, = refs
        o_ref[0] = x
        return
    gffn_ref, wr_ref, o_ref, hn_ref, gate_ref, cnt_ref = refs
    o_ref[0] = x
    hn = _rms(x, gffn_ref[...])
    hn_ref[0] = hn.astype(BF16)
    tm = hn.shape[0]
    lane = lax.broadcasted_iota(jnp.int32, (tm, GATE_LANES), 1)
    logits = jnp.where(lane < N_EXPERTS, _dot_3pass(hn, wr_ref[...]), -jnp.inf)
    m1 = jnp.max(logits, axis=-1, keepdims=True)
    i1 = jnp.min(jnp.where(logits == m1, lane, GATE_LANES), axis=-1, keepdims=True)
    rest = jnp.where(lane == i1, -jnp.inf, logits)
    m2 = jnp.max(rest, axis=-1, keepdims=True)
    i2 = jnp.min(jnp.where(rest == m2, lane, GATE_LANES), axis=-1, keepdims=True)
    e2 = jnp.exp(m2 - m1)
    g1 = 1.0 / (1.0 + e2)
    gates = jnp.where(lane == i1, g1, 0.0) + jnp.where(lane == i2, e2 * g1, 0.0)
    gates = jnp.where(lane == N_EXPERTS, i1.astype(F32), gates)
    gate_ref[0] = jnp.where(lane == N_EXPERTS + 1, i2.astype(F32), gates)
    sel = jnp.where((lane == i1) | (lane == i2), 1.0, 0.0)
    for c in range(tm // MOE_CHUNK):
        cnt_ref[c] = jnp.sum(sel[c * MOE_CHUNK:(c + 1) * MOE_CHUNK, :], axis=0, keepdims=True)


def mixer_tail(h3, mixers, w_out, g_xa, wq, k, v, wo, routing=None, tm=1024):
    bsz, t, d = h3.shape
    tm = min(tm, t)
    mlen = k.shape[1]
    hd = XA_HEADS * XA_DH
    gw = GROUP_WIDTH
    row = lambda b, i: (b, i, 0)
    fixed2 = lambda b, i: (0, 0)
    in_specs = ([pl.BlockSpec((1, tm, d), row)] + [pl.BlockSpec((1, tm, gw), row)] * 4
                + [pl.BlockSpec((4 * gw, d), fixed2), pl.BlockSpec((1, d), fixed2), pl.BlockSpec((d, hd), fixed2),
                   pl.BlockSpec((1, mlen, hd), lambda b, i: (b, 0, 0)),
                   pl.BlockSpec((1, mlen, hd), lambda b, i: (b, 0, 0)),
                   pl.BlockSpec((hd, d), fixed2)])
    args = [h3, *mixers, w_out, g_xa.reshape(1, d), wq, k, v, wo]
    out_specs = [pl.BlockSpec((1, tm, d), row)]
    out_shape = [jax.ShapeDtypeStruct((bsz, t, d), F32)]
    if routing is not None:
        assert tm % MOE_CHUNK == 0
        g_ffn, wr = routing
        in_specs += [pl.BlockSpec((1, d), fixed2), pl.BlockSpec((d, GATE_LANES), fixed2)]
        args += [g_ffn.reshape(1, d), jnp.zeros((d, GATE_LANES), F32).at[:, :N_EXPERTS].set(wr)]
        per_step = tm // MOE_CHUNK
        out_specs += [pl.BlockSpec((1, tm, d), row), pl.BlockSpec((1, tm, GATE_LANES), row),
                      pl.BlockSpec((per_step, 1, GATE_LANES), lambda b, i: (b * (t // tm) + i, 0, 0))]
        out_shape += [jax.ShapeDtypeStruct((bsz, t, d), BF16), jax.ShapeDtypeStruct((bsz, t, GATE_LANES), F32),
                      jax.ShapeDtypeStruct((bsz * t // MOE_CHUNK, 1, GATE_LANES), F32)]
    return pl.pallas_call(
        functools.partial(_mixer_tail_kernel, route=routing is not None),
        grid=(bsz, t // tm),
        in_specs=in_specs,
        out_specs=out_specs,
        out_shape=out_shape,
        compiler_params=_cparams(("parallel", "parallel")),
        name="mixer_tail",
    )(*args)


def _ffn_kernel(h_ref, g_ref, w1_ref, w3_ref, w2_ref, o_ref, *, tf):
    x = h_ref[...]
    hn = _rms(x, g_ref[...]).astype(BF16)
    acc = x
    for f0 in range(0, w1_ref.shape[1], tf):
        a = _dot(hn, w1_ref[:, f0:f0 + tf])
        b = _dot(hn, w3_ref[:, f0:f0 + tf])
        acc = acc + _dot((a * jax.nn.sigmoid(a) * b).astype(BF16), w2_ref[f0:f0 + tf, :])
    o_ref[...] = acc


def ffn(h, g, w1, w3, w2, tm=512, tf=704):
    m, d = h.shape
    tm = min(tm, m)
    nf = w1.shape[1]
    assert nf % tf == 0
    resident = lambda shape: pl.BlockSpec(shape, lambda i: (0, 0), pipeline_mode=pl.Buffered(1))
    return pl.pallas_call(
        functools.partial(_ffn_kernel, tf=tf),
        grid=(m // tm,),
        in_specs=[pl.BlockSpec((tm, d), lambda i: (i, 0)),
                  pl.BlockSpec((1, d), lambda i: (0, 0)),
                  resident((d, nf)), resident((d, nf)), resident((nf, d))],
        out_specs=pl.BlockSpec((tm, d), lambda i: (i, 0)),
        out_shape=jax.ShapeDtypeStruct((m, d), F32),
        compiler_params=_cparams(("parallel",)),
        name="ffn",
    )(h, g.reshape(1, d), w1, w3, w2)


GATE_LANES = LANES
MOE_CHUNK = 256
MOE_ROWS = LANES
MOE_TOKENS = 2048
MOE_ALIGN = BF16_TILE_ROWS


def _moe_kernel(cb_ref, h_ref, hn_ref, gate_ref, w1_ref, w3_ref, w2_ref, *refs, final):
    if final:
        gfin_ref, *refs = refs
    y_ref, rank_row, rk_ref, gs_ref, xs_ref, yacc_ref, rc_ref, gc_ref = refs
    t = pl.program_id(0)
    e = pl.program_id(1)
    f = pl.program_id(2)
    ts, d = hn_ref.shape
    ch, rb = MOE_CHUNK, MOE_ROWS
    win = 2 * rb
    n_ch = ts // ch
    lane = lax.broadcasted_iota(jnp.int32, (1, GATE_LANES), 1)

    def before(c):
        return cb_ref[(t * (n_ch + 1) + c) * N_EXPERTS + e]

    n_blocks = (before(n_ch) + rb - 1) // rb
    n_windows = (n_blocks * rb + win - 1) // win + 1

    def windows(c):
        lo, hi = before(c), before(c + 1)
        s0 = (lo // MOE_ALIGN) * MOE_ALIGN
        return s0, jnp.where(hi > lo, (hi - s0 + win - 1) // win, 0)

    spans = [windows(c) for c in range(n_ch)]
    single = functools.reduce(jnp.logical_and, [n_win <= 1 for _, n_win in spans])

    def for_each_window(visit):
        @pl.when(single)
        def _():
            for c, (s0, _) in enumerate(spans):
                visit(c, s0)

        @pl.when(jnp.logical_not(single))
        def _():
            for c, (s0, n_win) in enumerate(spans):
                def body(j, carry, c=c, s0=s0):
                    visit(c, s0 + j * win)
                    return carry
                lax.fori_loop(0, n_win, body, 0)

    @pl.when((e == 0) & (f == 0))
    def _():
        y_ref[...] = h_ref[...]
        strict_lower = (lax.broadcasted_iota(jnp.int32, (ch, ch), 1)
                        < lax.broadcasted_iota(jnp.int32, (ch, ch), 0)).astype(BF16)
        lane_f = lane.astype(F32)
        offs = jnp.ones((1, GATE_LANES), F32)
        for c in range(n_ch):
            rows = slice(c * ch, (c + 1) * ch)
            g = gate_ref[rows, :]
            sel = jnp.where((lane_f == g[:, N_EXPERTS:N_EXPERTS + 1])
                            | (lane_f == g[:, N_EXPERTS + 1:N_EXPERTS + 2]), 1.0, 0.0)
            r = _dot(strict_lower, sel.astype(BF16)) + offs
            r = jnp.where(sel > 0.0, r, 0.0)
            rank_row[:, rows] = jnp.transpose(r) - 1.0
            high = jnp.floor(r * (1.0 / 256.0))
            rk_ref[rows, :] = jnp.concatenate([high, r - 256.0 * high], axis=1).astype(BF16)
            gs_ref[rows, :] = jnp.concatenate(_split_bf16(g, 3), axis=1)
            offs = offs + jnp.sum(sel, axis=0, keepdims=True)

    @pl.when(f == 0)
    def _():
        pick = jnp.where(lax.broadcasted_iota(jnp.int32, (GATE_LANES, GATE_LANES), 0) == e, 1.0, 0.0)
        pick_rank = jnp.concatenate([256.0 * pick, pick], axis=0).astype(BF16)
        pick_gate = jnp.concatenate([pick] * 3, axis=0).astype(BF16)
        for c in range(n_ch):
            rows = slice(c * ch, (c + 1) * ch)
            rc_ref[rows, :] = _dot(rk_ref[rows, :], pick_rank) - 1.0
            gc_ref[rows, :] = _dot(gs_ref[rows, :], pick_gate)

        def clear(j, carry):
            r0 = pl.multiple_of(j * win, win)
            xs_ref[pl.ds(r0, win), :] = jnp.zeros((win, d), BF16)
            yacc_ref[pl.ds(r0, win), :] = jnp.zeros((win, d), F32)
            return carry

        lax.fori_loop(0, n_windows, clear, 0)

        row_id = lax.broadcasted_iota(jnp.int32, (win, ch), 0)

        def gather_window(c, s):
            s = pl.multiple_of(s, MOE_ALIGN)
            ranks = rank_row[pl.ds(e, 1), c * ch:(c + 1) * ch]
            onehot = jnp.where(ranks == (s + row_id).astype(F32), 1.0, 0.0).astype(BF16)
            xs_ref[pl.ds(s, win), :] += _dot(onehot, hn_ref[c * ch:(c + 1) * ch, :]).astype(BF16)

        for_each_window(gather_window)

    def ffn_rows(r0, rows):
        x = xs_ref[pl.ds(r0, rows), :]
        a = _dot(x, w1_ref[0])
        g3 = _dot(x, w3_ref[0])
        yacc_ref[pl.ds(r0, rows), :] += _dot((a * jax.nn.sigmoid(a) * g3).astype(BF16), w2_ref[0])

    five = n_blocks == 5
    half_tail = before(n_ch) <= 4 * rb + rb // 2

    @pl.when(five & half_tail)
    def _():
        ffn_rows(0, 4 * rb + rb // 2)

    @pl.when(five & jnp.logical_not(half_tail))
    def _():
        ffn_rows(0, 5 * rb)

    @pl.when(jnp.logical_not(five))
    def _():
        n_quads = n_blocks // 4

        def ffn_quad(j, carry):
            ffn_rows(pl.multiple_of(j * (4 * rb), 4 * rb), 4 * rb)
            return carry

        lax.fori_loop(0, n_quads, ffn_quad, 0)
        tail = pl.multiple_of(n_quads * (4 * rb), 4 * rb)

        @pl.when((n_blocks & 2) != 0)
        def _():
            ffn_rows(tail, 2 * rb)

        @pl.when((n_blocks & 1) != 0)
        def _():
            ffn_rows(pl.multiple_of(tail + (n_blocks & 2) * rb, rb), rb)

    @pl.when(f == pl.num_programs(2) - 1)
    def _():
        def to_bf16(j, carry):
            r0 = pl.multiple_of(j * win, win)
            xs_ref[pl.ds(r0, win), :] = yacc_ref[pl.ds(r0, win), :].astype(BF16)
            return carry

        lax.fori_loop(0, n_windows, to_bf16, 0)

        lane_id = lax.broadcasted_iota(jnp.int32, (ch, rb), 1).astype(F32)

        def scatter_window(c, s):
            rows = slice(c * ch, (c + 1) * ch)
            s = pl.multiple_of(s, MOE_ALIGN)
            rank = rc_ref[rows, :] - s.astype(F32)
            onehot = jnp.concatenate([jnp.where(rank == lane_id, 1.0, 0.0),
                                      jnp.where(rank == lane_id + float(rb), 1.0, 0.0)],
                                     axis=1).astype(BF16)
            gate = jnp.concatenate([gc_ref[rows, :]] * (d // GATE_LANES), axis=1)
            y_ref[rows, :] += gate * _dot(onehot, xs_ref[pl.ds(s, win), :])

        for_each_window(scatter_window)

        if final:
            @pl.when(e == pl.num_programs(1) - 1)
            def _():
                y_ref[...] = _rms(y_ref[...], gfin_ref[...])


def moe(h, hn, gates, cnt, w1, w3, w2, final_gain=None, tf=896):
    m, d = hn.shape
    ts = min(MOE_TOKENS, m)
    n_exp, _, nf = w1.shape
    assert nf % tf == 0 and m % ts == 0 and ts % MOE_CHUNK == 0 and MOE_ROWS == GATE_LANES
    n_ch = ts // MOE_CHUNK
    counts = cnt.reshape(m // ts, n_ch, GATE_LANES)[:, :, :n_exp].astype(jnp.int32)
    before = jnp.concatenate([jnp.zeros((m // ts, 1, n_exp), jnp.int32), jnp.cumsum(counts, axis=1)], axis=1)
    per_tile = lambda width: pl.BlockSpec((ts, width), lambda t, e, f, cb: (t, 0), pipeline_mode=pl.Buffered(1))
    in_specs = [per_tile(d), per_tile(d), per_tile(GATE_LANES),
                pl.BlockSpec((1, d, tf), lambda t, e, f, cb: (e, 0, f)),
                pl.BlockSpec((1, d, tf), lambda t, e, f, cb: (e, 0, f)),
                pl.BlockSpec((1, tf, d), lambda t, e, f, cb: (e, f, 0))]
    args = [before.reshape(-1), h, hn, gates, w1, w3, w2]
    if final_gain is not None:
        in_specs.append(pl.BlockSpec((1, d), lambda t, e, f, cb: (0, 0)))
        args.append(final_gain.reshape(1, d))
    grid_spec = pltpu.PrefetchScalarGridSpec(
        num_scalar_prefetch=1,
        grid=(m // ts, n_exp, nf // tf),
        in_specs=in_specs,
        out_specs=per_tile(d),
        scratch_shapes=[pltpu.VMEM((GATE_LANES, ts), F32),
                        pltpu.VMEM((ts, 2 * GATE_LANES), BF16),
                        pltpu.VMEM((ts, 3 * GATE_LANES), BF16),
                        pltpu.VMEM((ts + 2 * MOE_ROWS, d), BF16),
                        pltpu.VMEM((ts + 2 * MOE_ROWS, d), F32),
                        pltpu.VMEM((ts, MOE_ROWS), F32),
                        pltpu.VMEM((ts, GATE_LANES), F32)])
    return pl.pallas_call(
        functools.partial(_moe_kernel, final=final_gain is not None),
        grid_spec=grid_spec,
        out_shape=jax.ShapeDtypeStruct((m, d), F32),
        compiler_params=_cparams(("parallel", "arbitrary", "arbitrary")),
        name="moe",
    )(*args)


def _final_norm_kernel(g_ref, x_ref, o_ref):
    o_ref[...] = _rms(x_ref[...], g_ref[...])


def final_rmsnorm(g, x, tm=1024):
    m, d = x.shape
    tm = min(tm, m)
    return pl.pallas_call(
        _final_norm_kernel,
        grid=(m // tm,),
        in_specs=[pl.BlockSpec((1, d), lambda i: (0, 0)), pl.BlockSpec((tm, d), lambda i: (i, 0))],
        out_specs=pl.BlockSpec((tm, d), lambda i: (i, 0)),
        out_shape=jax.ShapeDtypeStruct((m, d), F32),
        compiler_params=_cparams(("parallel",)),
        name="final_norm",
    )(g.reshape(1, d), x)


def _pad_cols(w, width):
    return jnp.pad(w, ((0, 0), (0, width - w.shape[1])))


def _in_proj_weight(w_in):
    a_end = 2 * GLA_HEADS * GLA_DK + 2 * GROUP_WIDTH + GLA_RANK
    b_end = a_end + GROUP_WIDTH
    c_end = b_end + 3 * GROUP_WIDTH + IDX_HEADS * IDX_DIM + IDX_DIM + IDX_HEADS
    assert w_in.shape[1] == c_end + 2 * GROUP_WIDTH
    return jnp.concatenate([_pad_cols(w_in[:, :a_end], A_WIDTH), w_in[:, a_end:b_end],
                            _pad_cols(w_in[:, b_end:c_end], C_MAIN + C_AUX), w_in[:, c_end:]], axis=1).astype(BF16)


def hybrid_layer(h, mem2, toe, p, bsz, t, mixer_weights, routing=None):
    m, d = h.shape
    z_a, z_b, z_c, z_aux, z_d = norm_matmul(h, p["norm_mix"], _in_proj_weight(p["w_in"]),
                                            (A_WIDTH, B_WIDTH, C_MAIN, C_AUX, D_WIDTH),
                                            (F32, F32, BF16, F32, F32))
    o_a = gla_mixer(z_a.reshape(bsz, t, A_WIDTH), p["gla_wa2"], p["gla_ba"], p["gla_norm"])
    o_b = pool_mixer(z_b.reshape(bsz, t, B_WIDTH), p["pool_w"], p["pool_b"].reshape(-1), p["pool_scale"])
    o_c, *weights = dsa_mixer(z_c.reshape(bsz, t, C_MAIN), z_aux.reshape(bsz, t, C_AUX), toe, mixer_weights)
    o_d = sgu_mixer(z_d.reshape(bsz, t, D_WIDTH), p["sgu_ln_g"], p["sgu_ln_b"], p["sgu_w"], p["sgu_b"])
    hd = XA_HEADS * XA_DH
    k, v = norm_matmul(mem2, p["norm_mem"], p["xa_wkv"].astype(BF16), (hd, hd))
    mlen = mem2.shape[0] // bsz
    outs = mixer_tail(h.reshape(bsz, t, d), (o_a, o_b, o_c, o_d), p["w_out"].astype(BF16), p["norm_xa"],
                      p["xa_wq"].astype(BF16), k.reshape(bsz, mlen, hd), v.reshape(bsz, mlen, hd),
                      p["xa_wo"].astype(BF16), routing)
    if routing is None:
        return weights, outs[0].reshape(m, d)
    h3, hn, gates, counts = outs
    return weights, h3.reshape(m, d), hn.reshape(m, d), gates.reshape(m, GATE_LANES), counts


def kernel(x, mem, rel_bias, final_norm, norm_mix, w_in, gla_wa2, gla_ba, gla_norm, pool_w, pool_b,
           pool_scale, sgu_ln_g, sgu_ln_b, sgu_w, sgu_b, w_out, norm_xa, norm_mem, xa_wq, xa_wkv, xa_wo,
           norm_ffn, ffn_w1, ffn_w3, ffn_w2, router, moe_w1, moe_w3, moe_w2):
    bsz, t, d = x.shape
    depth = norm_mix.shape[0]
    h = x.reshape(bsz * t, d)
    mem2 = mem.reshape(-1, d)
    toe = rel_bias_tables(rel_bias)
    normed = False
    for i in range(depth):
        p = dict(norm_mix=norm_mix[i], w_in=w_in[i], gla_wa2=gla_wa2[i], gla_ba=gla_ba[i],
                 gla_norm=gla_norm[i], pool_w=pool_w[i], pool_b=pool_b[i], pool_scale=pool_scale[i],
                 sgu_ln_g=sgu_ln_g[i], sgu_ln_b=sgu_ln_b[i], sgu_w=sgu_w[i], sgu_b=sgu_b[i],
                 w_out=w_out[i], norm_xa=norm_xa[i], norm_mem=norm_mem[i], xa_wq=xa_wq[i],
                 xa_wkv=xa_wkv[i], xa_wo=xa_wo[i])
        j = i // 2
        if i % 2 == 0:
            (w1, w3, w2), h = hybrid_layer(h, mem2, toe, p, bsz, t, (ffn_w1[j], ffn_w3[j], ffn_w2[j]))
            h = ffn(h, norm_ffn[i], w1, w3, w2)
        else:
            (w1, w3, w2), h, hn, gates, counts = hybrid_layer(h, mem2, toe, p, bsz, t,
                                                              (moe_w1[j], moe_w3[j], moe_w2[j]),
                                                              routing=(norm_ffn[i], router[j]))
            normed = i == depth - 1
            h = moe(h, hn, gates, counts, w1, w3, w2, final_gain=final_norm if normed else None)
    if not normed:
        h = final_rmsnorm(final_norm, h)
    return h.reshape(bsz, t, d)
```

```python
import functools
import math

import jax
import jax.numpy as jnp
import numpy as np
from jax import lax
from jax.experimental import pallas as pl
from jax.experimental.pallas import tpu as pltpu

F32 = jnp.float32
BF16 = jnp.bfloat16
EPS = 1e-6

LANES = 128
BF16_TILE_ROWS = 16
VMEM_LIMIT = 56 * 1024 * 1024

GROUP_WIDTH = 256

GLA_HEADS = 4
GLA_DV = 64
GLA_DK = 32
GLA_RANK = 16
GLA_TAU = 16.0
GLA_CHUNK = 64
GLA_GROUP = 16

POOL_WINDOWS = (2, 4, 8, 16)
POOL_CG = 64

DSA_HEADS = 4
DSA_DH = 64
IDX_HEADS = 8
IDX_DIM = 32
DSA_TOPK_MAX = 256
DSA_BLOCK = 256
DSA_SUB = 64
DSA_HEAD_GROUPS = ((0, 1, 2, 3),)

SGU_GROUPS = 4
SGU_CHUNK = 128
SGU_CG = 64

REL_BUCKETS = 32
REL_MAX_DIST = 128

XA_HEADS = 4
XA_DH = 64

N_EXPERTS = 8

A_WIDTH = 896
B_WIDTH = 256
C_MAIN = 1024
C_AUX = LANES
D_WIDTH = 512

MIXER_DTYPE = BF16
INT_MIN = -(2 ** 31)
NEG_BIG = -1e30


def _cparams(sem):
    return pltpu.CompilerParams(dimension_semantics=sem, vmem_limit_bytes=VMEM_LIMIT)


def _dot(a, b):
    return jnp.dot(a, b, preferred_element_type=F32)


def _dot_nt(a, b):
    return lax.dot_general(a, b, (((1,), (1,)), ((), ())), preferred_element_type=F32)


def _dot_tn(a, b):
    return lax.dot_general(a, b, (((0,), (0,)), ((), ())), preferred_element_type=F32)


def _split_bf16(x, terms):
    parts = []
    for _ in range(terms):
        p = x.astype(BF16)
        parts.append(p)
        x = x - p.astype(F32)
    return parts


def _dot_exact_rhs(a, b, terms):
    out = None
    for p in _split_bf16(a, terms):
        d = _dot(p, b)
        out = d if out is None else out + d
    return out


def _dot_exact_lhs(a, b, terms):
    out = None
    for p in _split_bf16(b, terms):
        d = _dot(a, p)
        out = d if out is None else out + d
    return out


def _dot_3pass(a, b):
    a_hi, a_lo = _split_bf16(a, 2)
    b_hi, b_lo = _split_bf16(b, 2)
    return _dot(a_hi, b_hi) + (_dot(a_hi, b_lo) + _dot(a_lo, b_hi))


def _rms(x, g):
    return x * lax.rsqrt(jnp.mean(x * x, axis=-1, keepdims=True) + EPS) * g


def _norm_matmul_kernel(h_ref, g_ref, w_ref, *out_refs, widths):
    hb = _rms(h_ref[...], g_ref[...]).astype(BF16)
    off = 0
    for o_ref, wd in zip(out_refs, widths):
        o_ref[...] = _dot(hb, w_ref[:, off:off + wd]).astype(o_ref.dtype)
        off += wd


def norm_matmul(h, g, w, widths, dtypes=None, tm=1024):
    dtypes = dtypes or (F32,) * len(widths)
    m, d = h.shape
    tm = min(tm, m)
    n = sum(widths)
    return pl.pallas_call(
        functools.partial(_norm_matmul_kernel, widths=widths),
        grid=(m // tm,),
        in_specs=[pl.BlockSpec((tm, d), lambda i: (i, 0)),
                  pl.BlockSpec((1, d), lambda i: (0, 0)),
                  pl.BlockSpec((d, n), lambda i: (0, 0))],
        out_specs=[pl.BlockSpec((tm, wd), lambda i: (i, 0)) for wd in widths],
        out_shape=[jax.ShapeDtypeStruct((m, wd), dt) for wd, dt in zip(widths, dtypes)],
        compiler_params=_cparams(("parallel",)),
        name="norm_matmul",
    )(h, g.reshape(1, d), w)


def _log_sigmoid(x):
    return jnp.minimum(x, 0.0) - jnp.log1p(jnp.exp(-jnp.abs(x)))


def _gla_kernel(z_ref, wa2_ref, ba_ref, ng_ref, o_ref, s_ref, *, group, n_groups):
    c = GLA_CHUNK
    grp = group * c
    hk = GLA_HEADS * GLA_DK
    hv = GLA_HEADS * GLA_DV
    s_ref[...] = jnp.zeros_like(s_ref)

    head_k = lax.broadcasted_iota(jnp.int32, (1, hk), 1) // GLA_DK
    head_v = lax.broadcasted_iota(jnp.int32, (1, hv), 1) // GLA_DV
    cum = min(grp, 256)
    g_row = lax.broadcasted_iota(jnp.int32, (cum, cum), 0)
    g_col = lax.broadcasted_iota(jnp.int32, (cum, cum), 1)
    tril = (((g_row // c) == (g_col // c)) & (g_col <= g_row)).astype(BF16)
    causal4 = (lax.broadcasted_iota(jnp.int32, (GLA_HEADS * c, c), 1)
               <= lax.broadcasted_iota(jnp.int32, (GLA_HEADS * c, c), 0) % c)
    state_mask = (lax.broadcasted_iota(jnp.int32, (hk, hv), 0) // GLA_DK
                  == lax.broadcasted_iota(jnp.int32, (hk, hv), 1) // GLA_DV)
    norm_mat = jnp.where(lax.broadcasted_iota(jnp.int32, (hv, hv), 0) // GLA_DV
                         == lax.broadcasted_iota(jnp.int32, (hv, hv), 1) // GLA_DV,
                         1.0 / GLA_DV, 0.0).astype(BF16)
    wa2 = wa2_ref[...]
    ba = ba_ref[...]
    ng = ng_ref[...]

    def body(n, carry):
        r0 = pl.multiple_of(n * grp, grp)
        z = z_ref[0, pl.ds(r0, grp), :]
        q, k, v, g, lr = z[:, 0:128], z[:, 128:256], z[:, 256:512], z[:, 512:768], z[:, 768:896]
        log_a = _log_sigmoid(_dot_3pass(lr, wa2) + ba) / GLA_TAU
        b = jnp.concatenate([_dot_exact_lhs(tril, log_a[r:r + cum], 3) for r in range(0, grp, cum)], axis=0)
        b_end = jnp.concatenate([jnp.broadcast_to(b[(ci + 1) * c - 1:(ci + 1) * c, :], (c, hk))
                                 for ci in range(group)], axis=0)
        q_t = q * (GLA_DK ** -0.5) * jnp.exp(b)
        q_tb = q_t.astype(BF16)
        k_t = (k * jnp.exp(-b)).astype(BF16)
        k_dec = (k * jnp.exp(b_end - b)).astype(BF16)
        vb = v.astype(BF16)
        outs = []
        for ci in range(group):
            rows = slice(ci * c, (ci + 1) * c)
            q4 = jnp.concatenate([jnp.where(head_k == h, q_t[rows], 0.0) for h in range(GLA_HEADS)],
                                 axis=0).astype(BF16)
            att = jnp.where(causal4, _dot_nt(q4, k_t[rows]), 0.0)
            r = _dot(att.astype(BF16), vb[rows])
            o = _dot(q_tb[rows], s_ref[...].astype(BF16))
            for h in range(GLA_HEADS):
                o = o + jnp.where(head_v == h, r[h * c:(h + 1) * c, :], 0.0)
            outs.append(o)
            kv = jnp.where(state_mask, _dot_tn(k_dec[rows], vb[rows]), 0.0)
            last = b_end[ci * c:ci * c + 1, :]
            dec = jnp.exp(jnp.transpose(jnp.broadcast_to(last, (hk, hk))))
            s_ref[...] = s_ref[...] * jnp.concatenate([dec, dec], axis=1) + kv
        o = jnp.concatenate(outs, axis=0)
        o = o * lax.rsqrt(_dot_exact_rhs(o * o, norm_mat, 2) + EPS) * ng
        o_ref[0, pl.ds(r0, grp), :] = (o * (g * jax.nn.sigmoid(g))).astype(o_ref.dtype)
        return carry

    lax.fori_loop(0, n_groups, body, 0)


def gla_mixer(z_a, wa2, ba, norm_g):
    bsz, t, _ = z_a.shape
    hk = GLA_HEADS * GLA_DK
    wa2p = jnp.zeros((LANES, hk), F32).at[:GLA_RANK].set(wa2)
    n_chunks = t // GLA_CHUNK
    group = math.gcd(GLA_GROUP, n_chunks)
    return pl.pallas_call(
        functools.partial(_gla_kernel, group=group, n_groups=n_chunks // group),
        grid=(bsz,),
        in_specs=[pl.BlockSpec((1, t, A_WIDTH), lambda b: (b, 0, 0)),
                  pl.BlockSpec((LANES, hk), lambda b: (0, 0)),
                  pl.BlockSpec((1, hk), lambda b: (0, 0)),
                  pl.BlockSpec((1, GROUP_WIDTH), lambda b: (0, 0))],
        out_specs=pl.BlockSpec((1, t, GROUP_WIDTH), lambda b: (b, 0, 0)),
        out_shape=jax.ShapeDtypeStruct((bsz, t, GROUP_WIDTH), MIXER_DTYPE),
        scratch_shapes=[pltpu.VMEM((hk, GROUP_WIDTH), F32)],
        compiler_params=_cparams(("parallel",)),
        name="gla_mixer",
    )(z_a, wa2p, ba.reshape(1, hk), norm_g.reshape(1, GROUP_WIDTH))


def _pool_kernel(u_ref, w_ref, b_ref, sc_ref, o_ref):
    u = u_ref[0]
    t, gw = u.shape
    row = lax.broadcasted_iota(jnp.int32, (t, gw), 0)
    grp = lax.broadcasted_iota(jnp.int32, (t, gw), 1) // POOL_CG

    def shifted(x, k):
        return jnp.where(row >= k, pltpu.roll(x, k, axis=0), 0.0)

    s = u
    p = jnp.zeros_like(u)
    for gi, win in enumerate(POOL_WINDOWS):
        half = win // 2
        s = s + shifted(s, half)
        cnt = jnp.minimum(row + 1, win).astype(F32)
        p = jnp.where(grp == gi, s / cnt - u, p)
    y = _dot(p.astype(BF16), w_ref[...]) + b_ref[...]
    o_ref[0] = (y * sc_ref[...]).astype(o_ref.dtype)


def pool_mixer(z_b, w, b, scale):
    assert POOL_WINDOWS == (2, 4, 8, 16)
    bsz, t, gw = z_b.shape
    w_bd = jnp.zeros((gw, gw), F32)
    for gi in range(len(POOL_WINDOWS)):
        w_bd = w_bd.at[gi * POOL_CG:(gi + 1) * POOL_CG, gi * POOL_CG:(gi + 1) * POOL_CG].set(w[gi])
    return pl.pallas_call(
        _pool_kernel,
        grid=(bsz,),
        in_specs=[pl.BlockSpec((1, t, gw), lambda i: (i, 0, 0)),
                  pl.BlockSpec((gw, gw), lambda i: (0, 0)),
                  pl.BlockSpec((1, gw), lambda i: (0, 0)),
                  pl.BlockSpec((1, gw), lambda i: (0, 0))],
        out_specs=pl.BlockSpec((1, t, gw), lambda i: (i, 0, 0)),
        out_shape=jax.ShapeDtypeStruct((bsz, t, gw), MIXER_DTYPE),
        compiler_params=_cparams(("parallel",)),
        name="pool_mixer",
    )(z_b, w_bd.astype(BF16), b.reshape(1, gw), scale.reshape(1, gw))


def _sgu_kernel(z_ref, lg_ref, lb_ref, w_ref, bm_ref, o_ref, *, chunks):
    c = SGU_CHUNK
    gw = GROUP_WIDTH
    rows = SGU_GROUPS * c
    tri = (lax.broadcasted_iota(jnp.int32, (rows, c), 1)
           <= lax.broadcasted_iota(jnp.int32, (rows, c), 0) % c)
    ws = jnp.where(tri, w_ref[...], 0.0).astype(BF16)
    grp = lax.broadcasted_iota(jnp.int32, (1, gw), 1) // SGU_CG
    for ci in range(chunks):
        z = jax.nn.gelu(z_ref[0, ci * c:(ci + 1) * c, :], approximate=True)
        u, v = z[:, :gw], z[:, gw:]
        mu = jnp.mean(v, axis=-1, keepdims=True)
        var = jnp.mean(jnp.square(v - mu), axis=-1, keepdims=True)
        vn = (v - mu) * lax.rsqrt(var + EPS) * lg_ref[...] + lb_ref[...]
        r = _dot(ws, vn.astype(BF16))
        mixed = bm_ref[...]
        for g in range(SGU_GROUPS):
            mixed = mixed + jnp.where(grp == g, r[g * c:(g + 1) * c, :], 0.0)
        o_ref[0, ci * c:(ci + 1) * c, :] = (u * mixed).astype(o_ref.dtype)


def sgu_mixer(z_d, ln_g, ln_b, w_s, b_s, chunks=4):
    bsz, t, _ = z_d.shape
    gw = GROUP_WIDTH
    tt = chunks * SGU_CHUNK
    bias = jnp.repeat(b_s.T, SGU_CG, axis=1)
    return pl.pallas_call(
        functools.partial(_sgu_kernel, chunks=chunks),
        grid=(bsz, t // tt),
        in_specs=[pl.BlockSpec((1, tt, 2 * gw), lambda b, i: (b, i, 0)),
                  pl.BlockSpec((1, gw), lambda b, i: (0, 0)),
                  pl.BlockSpec((1, gw), lambda b, i: (0, 0)),
                  pl.BlockSpec((SGU_GROUPS * SGU_CHUNK, SGU_CHUNK), lambda b, i: (0, 0)),
                  pl.BlockSpec((SGU_CHUNK, gw), lambda b, i: (0, 0))],
        out_specs=pl.BlockSpec((1, tt, gw), lambda b, i: (b, i, 0)),
        out_shape=jax.ShapeDtypeStruct((bsz, t, gw), MIXER_DTYPE),
        compiler_params=_cparams(("parallel", "parallel")),
        name="sgu_mixer",
    )(z_d, ln_g.reshape(1, gw), ln_b.reshape(1, gw),
      w_s.reshape(SGU_GROUPS * SGU_CHUNK, SGU_CHUNK), bias)


def _bucket_table():
    assert REL_MAX_DIST <= DSA_BLOCK + 1
    s = np.arange(DSA_BLOCK)[:, None]
    t = np.arange(DSA_BLOCK)[None, :]
    dist = np.stack([t - s, DSA_BLOCK + t - s, 2 * DSA_BLOCK + t - s])
    n = np.maximum(dist, 0)
    max_exact = REL_BUCKETS // 2
    nf = np.maximum(n, 1).astype(np.float32)
    large = max_exact + (np.log(nf / np.float32(max_exact)) / np.float32(math.log(REL_MAX_DIST / max_exact))
                         * np.float32(REL_BUCKETS - max_exact)).astype(np.int32)
    return np.where(n < max_exact, n, np.minimum(large, REL_BUCKETS - 1)).astype(np.int32)


def _bias_table_kernel(rb_ref, bucket_ref, o_ref):
    for back in range(3):
        bucket = bucket_ref[back]
        for h in range(DSA_HEADS):
            acc = jnp.zeros(bucket.shape, F32)
            for b in range(REL_BUCKETS):
                acc = jnp.where(bucket == b, rb_ref[b * DSA_HEADS + h], acc)
            o_ref[back, h] = acc


def rel_bias_tables(rel_bias):
    blk = DSA_BLOCK
    return pl.pallas_call(
        _bias_table_kernel,
        in_specs=[pl.BlockSpec(memory_space=pltpu.SMEM),
                  pl.BlockSpec((3, blk, blk), lambda: (0, 0, 0))],
        out_specs=pl.BlockSpec((3, DSA_HEADS, blk, blk), lambda: (0, 0, 0, 0)),
        out_shape=jax.ShapeDtypeStruct((3, DSA_HEADS, blk, blk), F32),
        name="rel_bias_tables",
    )(rel_bias.reshape(-1), jnp.asarray(_bucket_table()))


def _dsa_kernel(q_ref, k_ref, v_ref, qi_ref, kw_ref, qw_ref, toe_ref, *refs, topk, n_blocks, idx_bits, n_cast):
    for src, dst in zip(refs[:n_cast], refs[n_cast + 1:2 * n_cast + 1]):
        dst[...] = src[...].astype(BF16)
    o_ref = refs[n_cast]
    kpl_ref, vt_ref, keys_ref, khi_ref, klo_ref, acc_ref, am_ref, lg_ref, p_ref = refs[2 * n_cast + 1:]
    blk = DSA_BLOCK
    sub = DSA_SUB
    i = pl.program_id(1)
    hd = DSA_HEADS * DSA_DH
    heads_per_half = LANES // IDX_DIM

    @pl.when(i == 0)
    def _():
        lane = lax.broadcasted_iota(jnp.int32, (blk, LANES), 1)

        def build(kb, c):
            r0 = pl.multiple_of(kb * blk, blk)
            ki = jnp.where(lane < IDX_DIM, kw_ref[0, pl.ds(r0, blk), :], 0.0)
            for j in range(heads_per_half):
                kpl_ref[kb, j] = (ki if j == 0 else pltpu.roll(ki, j * IDX_DIM, axis=1)).astype(BF16)
            vt_ref[kb] = jnp.transpose(v_ref[0, pl.ds(r0, blk), :].astype(F32)).astype(BF16)
            return c

        lax.fori_loop(0, n_blocks, build, 0)

    s_loc = lax.broadcasted_iota(jnp.int32, (blk, blk), 0)
    t_loc = lax.broadcasted_iota(jnp.int32, (blk, blk), 1)
    n_vis = i + 1

    qi_t = jnp.transpose(qi_ref[0].astype(F32)).astype(BF16)
    qi_halves = [qi_t[:LANES, :], qi_t[LANES:, :]]
    w_t = jnp.transpose(qw_ref[0])
    w_rows = [w_t[IDX_DIM + h:IDX_DIM + h + 1, :] * (IDX_HEADS ** -0.5) * (IDX_DIM ** -0.5)
              for h in range(IDX_HEADS)]
    s_sub = lax.broadcasted_iota(jnp.int32, (sub, blk), 0)
    t_sub = lax.broadcasted_iota(jnp.int32, (sub, blk), 1)

    def score_body(kb, c):
        for ci in range(blk // sub):
            rows = slice(ci * sub, (ci + 1) * sub)
            sc = jnp.zeros((sub, blk), F32)
            for half in range(2):
                for j in range(heads_per_half):
                    d = _dot(kpl_ref[kb, j, rows, :], qi_halves[half])
                    sc = sc + jnp.maximum(d, 0.0) * w_rows[half * heads_per_half + j]
            sc = jnp.where(sc == 0.0, 0.0, sc)
            bits = pltpu.bitcast(sc, jnp.int32)
            key = jnp.where(bits < 0, bits ^ jnp.int32(0x7FFFFFFF), bits)
            vis = (kb < i) | (s_sub + ci * sub <= t_sub)
            key = jnp.where(vis, key, jnp.int32(INT_MIN))
            keys_ref[kb, rows, :] = key
            khi_ref[kb, rows, :] = lax.shift_right_arithmetic(key, 16).astype(jnp.int16)
            klo_ref[kb, rows, :] = ((key & 0xFFFF) - 2 ** 15).astype(jnp.int16)
        return c

    lax.fori_loop(0, n_vis, score_body, 0)

    def count(pred):
        def body(kb, acc):
            hit = jnp.where(pred(keys_ref[kb], kb), 1.0, 0.0)
            return acc + jnp.sum(hit.reshape(blk // 32, 32, blk), axis=0)
        acc = lax.fori_loop(0, n_vis, body, jnp.zeros((32, blk), F32))
        return jnp.sum(acc, axis=0, keepdims=True)

    def count16(ref, cand):
        cand = cand.astype(jnp.int16)

        def body(kb, acc):
            hit = jnp.where(ref[kb] >= cand, jnp.int16(1), jnp.int16(0))
            for j in range(blk // 32):
                acc = acc + hit[j * 32:(j + 1) * 32]
            return acc
        acc = lax.fori_loop(0, n_vis, body, jnp.zeros((32, blk), jnp.int16))
        return jnp.sum(acc.astype(jnp.int32).astype(F32), axis=0, keepdims=True)

    def search16(ref, offset):
        lowest = jnp.full((1, blk), -(2 ** 15), jnp.int32)
        base = jnp.where(offset + count16(ref, jnp.zeros((1, blk), jnp.int32)) >= kf, 0, lowest)

        def bit_body(it, base):
            cand = base | lax.shift_left(jnp.int32(1), 14 - it)
            return jnp.where(offset + count16(ref, cand) >= kf, cand, base)
        return lax.fori_loop(0, 15, bit_body, base)

    kf = float(topk)
    zero = jnp.zeros((1, blk), jnp.int32)
    thr_hi = search16(khi_ref, 0.0)
    thr_hi16 = thr_hi.astype(jnp.int16)

    def low_body(kb, above):
        hi = khi_ref[kb]
        klo_ref[kb] = jnp.where(hi == thr_hi16, klo_ref[kb], jnp.int16(-(2 ** 15)))
        hit = jnp.where(hi > thr_hi16, jnp.int16(1), jnp.int16(0))
        for j in range(blk // 32):
            above = above + hit[j * 32:(j + 1) * 32]
        return above

    above = lax.fori_loop(0, n_vis, low_body, jnp.zeros((32, blk), jnp.int16))
    n_above = jnp.sum(above.astype(jnp.int32).astype(F32), axis=0, keepdims=True)
    thr_lo = search16(klo_ref, n_above)
    thr = lax.shift_left(thr_hi, 16) | (thr_lo + 2 ** 15)
    thr_sel = jnp.maximum(thr, jnp.int32(INT_MIN + 1))

    n_ge = count(lambda key, kb: key >= thr)
    excess = jnp.where((n_ge > kf) & (thr > jnp.int32(INT_MIN)), 1.0, 0.0)

    def tie_search():
        need = kf - count(lambda key, kb: key > thr)

        def tie_body(it, j):
            cand = j | lax.shift_left(jnp.int32(1), idx_bits - 1 - it)
            below = count(lambda key, kb: (key == thr) & (kb * blk + s_loc < cand))
            return jnp.where(below < need, cand, j)
        return lax.fori_loop(0, idx_bits, tie_body, zero)

    last = lax.cond(jnp.max(excess) > 0.0, tie_search, lambda: jnp.full((1, blk), 2 ** 30, jnp.int32))

    assert DSA_DH ** -0.5 == 0.125
    q_t = jnp.transpose(q_ref[0].astype(F32) * (DSA_DH ** -0.5))
    row_h = lax.broadcasted_iota(jnp.int32, (hd, 1), 0) // DSA_DH
    q_heads = [jnp.where(row_h == h, q_t, 0.0).astype(BF16) for h in range(DSA_HEADS)]
    acc_ref[...] = jnp.zeros_like(acc_ref)
    n_sub = blk // sub

    def att_body(kb, carry):
        ms, ls = carry
        back = jnp.minimum(i - kb, 2)
        for ci in range(n_sub):
            rows = slice(ci * sub, (ci + 1) * sub)
            key = keys_ref[kb, rows, :]
            sel = (key > thr_sel) | ((key == thr_sel) & (kb * blk + ci * sub + s_sub <= last))
            am_ref[rows, :] = jnp.where(sel, 0.0, NEG_BIG)
        new_ms, new_ls = list(ms), list(ls)
        for heads in DSA_HEAD_GROUPS:
            alphas = {}
            for h in heads:
                pm = jnp.full((8, blk), NEG_BIG, F32)
                for ci in range(n_sub):
                    rows = slice(ci * sub, (ci + 1) * sub)
                    k_rows = k_ref[0, pl.ds(pl.multiple_of(kb * blk + ci * sub, sub), sub), :]
                    lg = _dot(k_rows, q_heads[h]) + toe_ref[back, h, rows, :] + am_ref[rows, :]
                    lg_ref[h, rows, :] = lg
                    pm = jnp.maximum(pm, jnp.max(lg.reshape(sub // 8, 8, blk), axis=0))
                new_ms[h] = jnp.maximum(ms[h], jnp.max(pm, axis=0, keepdims=True))
                alphas[h] = jnp.exp(ms[h] - new_ms[h])
            for h in heads:
                ps = jnp.zeros((8, blk), F32)
                for ci in range(n_sub):
                    rows = slice(ci * sub, (ci + 1) * sub)
                    p = jnp.exp(lg_ref[h, rows, :] - new_ms[h])
                    ps = ps + jnp.sum(p.reshape(sub // 8, 8, blk), axis=0)
                    p_ref[h, rows, :] = p.astype(BF16)
                new_ls[h] = ls[h] * alphas[h] + jnp.sum(ps, axis=0, keepdims=True)
            for h in heads:
                hrows = slice(h * DSA_DH, (h + 1) * DSA_DH)
                acc_ref[hrows, :] = acc_ref[hrows, :] * alphas[h] + _dot(vt_ref[kb, hrows, :], p_ref[h])
        return tuple(new_ms), tuple(new_ls)

    init = (tuple(jnp.full((1, blk), 0.01 * NEG_BIG, F32) for _ in range(DSA_HEADS)),
            tuple(jnp.zeros((1, blk), F32) for _ in range(DSA_HEADS)))
    _, ls = lax.fori_loop(0, n_vis, att_body, init)
    for h in range(DSA_HEADS):
        rows = slice(h * DSA_DH, (h + 1) * DSA_DH)
        acc_ref[rows, :] = acc_ref[rows, :] * (1.0 / ls[h])
    o_ref[0] = jnp.transpose(acc_ref[...]).astype(o_ref.dtype)


def _slab_view(w, steps):
    rows = math.prod(w.shape[:-1])
    n = max(k for k in range(1, steps + 1) if rows % (k * BF16_TILE_ROWS) == 0)
    return w.reshape(n, rows // n, w.shape[-1])


def dsa_mixer(z_c, z_aux, toe, to_bf16=()):
    bsz, t, _ = z_c.shape
    blk = DSA_BLOCK
    n_blocks = t // blk
    assert t % blk == 0
    topk = min(DSA_TOPK_MAX, t // 4)
    hd = DSA_HEADS * DSA_DH
    steps = bsz * n_blocks
    slabs = [_slab_view(w, steps) for w in to_bf16]
    slab_specs = [pl.BlockSpec((1,) + s.shape[1:],
                               lambda b, i, n=s.shape[0]: (jnp.minimum(b * n_blocks + i, n - 1), 0, 0))
                  for s in slabs]
    kernel = functools.partial(_dsa_kernel, topk=topk, n_blocks=n_blocks,
                               idx_bits=max(1, (t - 1).bit_length()), n_cast=len(slabs))
    out, *converted = pl.pallas_call(
        kernel,
        grid=(bsz, n_blocks),
        in_specs=[pl.BlockSpec((1, blk, hd), lambda b, i: (b, i, 0)),
                  pl.BlockSpec((1, t, hd), lambda b, i: (b, 0, 1)),
                  pl.BlockSpec((1, t, hd), lambda b, i: (b, 0, 2)),
                  pl.BlockSpec((1, blk, hd), lambda b, i: (b, i, 3)),
                  pl.BlockSpec((1, t, C_AUX), lambda b, i: (b, 0, 0)),
                  pl.BlockSpec((1, blk, C_AUX), lambda b, i: (b, i, 0)),
                  pl.BlockSpec((3, DSA_HEADS, blk, blk), lambda b, i: (0, 0, 0, 0))] + slab_specs,
        out_specs=[pl.BlockSpec((1, blk, hd), lambda b, i: (b, i, 0))] + slab_specs,
        out_shape=[jax.ShapeDtypeStruct((bsz, t, hd), MIXER_DTYPE)]
                  + [jax.ShapeDtypeStruct(s.shape, BF16) for s in slabs],
        scratch_shapes=[pltpu.VMEM((n_blocks, LANES // IDX_DIM, blk, LANES), BF16),
                        pltpu.VMEM((n_blocks, hd, blk), BF16),
                        pltpu.VMEM((n_blocks, blk, blk), jnp.int32),
                        pltpu.VMEM((n_blocks, blk, blk), jnp.int16),
                        pltpu.VMEM((n_blocks, blk, blk), jnp.int16),
                        pltpu.VMEM((hd, blk), F32),
                        pltpu.VMEM((blk, blk), F32),
                        pltpu.VMEM((DSA_HEADS, blk, blk), F32),
                        pltpu.VMEM((DSA_HEADS, blk, blk), BF16)],
        compiler_params=_cparams(("parallel", "arbitrary")),
        name="dsa_mixer",
    )(z_c, z_c, z_c, z_c, z_aux, z_aux, toe, *slabs)
    return (out, *[c.reshape(w.shape) for c, w in zip(converted, to_bf16)])


def _mixer_tail_kernel(h_ref, a_ref, b_ref, c_ref, d_ref, wout_ref, gxa_ref, wq_ref, k_ref, v_ref, wo_ref,
                       *refs, route):
    gw = GROUP_WIDTH
    x = h_ref[0]
    for gi, r in enumerate((a_ref, b_ref, c_ref, d_ref)):
        x = x + _dot(r[0], wout_ref[gi * gw:(gi + 1) * gw, :])

    hd = XA_HEADS * XA_DH
    q = _dot(_rms(x, gxa_ref[...]).astype(BF16), wq_ref[...]).astype(BF16)
    k = k_ref[0]
    v = v_ref[0]
    lane_h = lax.broadcasted_iota(jnp.int32, (1, hd), 1) // XA_DH
    o = jnp.zeros((x.shape[0], hd), F32)
    for h in range(XA_HEADS):
        s = _dot_nt(q, jnp.where(lane_h == h, k, 0.0).astype(BF16)) * (XA_DH ** -0.5)
        p = jnp.exp(s - jnp.max(s, axis=-1, keepdims=True))
        p = p / jnp.sum(p, axis=-1, keepdims=True)
        o = o + _dot(p.astype(BF16), jnp.where(lane_h == h, v, 0.0).astype(BF16))
    x = x + _dot(o.astype(BF16), wo_ref[...])

    if not route:
        o_ref, = refs
        o_ref[0] = x
        return
    gffn_ref, wr_ref, o_ref, hn_ref, gate_ref, cnt_ref = refs
    o_ref[0] = x
    hn = _rms(x, gffn_ref[...])
    hn_ref[0] = hn.astype(BF16)
    tm = hn.shape[0]
    lane = lax.broadcasted_iota(jnp.int32, (tm, GATE_LANES), 1)
    logits = jnp.where(lane < N_EXPERTS, _dot_3pass(hn, wr_ref[...]), -jnp.inf)
    m1 = jnp.max(logits, axis=-1, keepdims=True)
    i1 = jnp.min(jnp.where(logits == m1, lane, GATE_LANES), axis=-1, keepdims=True)
    rest = jnp.where(lane == i1, -jnp.inf, logits)
    m2 = jnp.max(rest, axis=-1, keepdims=True)
    i2 = jnp.min(jnp.where(rest == m2, lane, GATE_LANES), axis=-1, keepdims=True)
    e2 = jnp.exp(m2 - m1)
    g1 = 1.0 / (1.0 + e2)
    gates = jnp.where(lane == i1, g1, 0.0) + jnp.where(lane == i2, e2 * g1, 0.0)
    gates = jnp.where(lane == N_EXPERTS, i1.astype(F32), gates)
    gate_ref[0] = jnp.where(lane == N_EXPERTS + 1, i2.astype(F32), gates)
    sel = jnp.where((lane == i1) | (lane == i2), 1.0, 0.0)
    for c in range(tm // MOE_CHUNK):
        cnt_ref[c] = jnp.sum(sel[c * MOE_CHUNK:(c + 1) * MOE_CHUNK, :], axis=0, keepdims=True)


def mixer_tail(h3, mixers, w_out, g_xa, wq, k, v, wo, routing=None, tm=1024):
    bsz, t, d = h3.shape
    tm = min(tm, t)
    mlen = k.shape[1]
    hd = XA_HEADS * XA_DH
    gw = GROUP_WIDTH
    row = lambda b, i: (b, i, 0)
    fixed2 = lambda b, i: (0, 0)
    in_specs = ([pl.BlockSpec((1, tm, d), row)] + [pl.BlockSpec((1, tm, gw), row)] * 4
                + [pl.BlockSpec((4 * gw, d), fixed2), pl.BlockSpec((1, d), fixed2), pl.BlockSpec((d, hd), fixed2),
                   pl.BlockSpec((1, mlen, hd), lambda b, i: (b, 0, 0)),
                   pl.BlockSpec((1, mlen, hd), lambda b, i: (b, 0, 0)),
                   pl.BlockSpec((hd, d), fixed2)])
    args = [h3, *mixers, w_out, g_xa.reshape(1, d), wq, k, v, wo]
    out_specs = [pl.BlockSpec((1, tm, d), row)]
    out_shape = [jax.ShapeDtypeStruct((bsz, t, d), F32)]
    if routing is not None:
        assert tm % MOE_CHUNK == 0
        g_ffn, wr = routing
        in_specs += [pl.BlockSpec((1, d), fixed2), pl.BlockSpec((d, GATE_LANES), fixed2)]
        args += [g_ffn.reshape(1, d), jnp.zeros((d, GATE_LANES), F32).at[:, :N_EXPERTS].set(wr)]
        per_step = tm // MOE_CHUNK
        out_specs += [pl.BlockSpec((1, tm, d), row), pl.BlockSpec((1, tm, GATE_LANES), row),
                      pl.BlockSpec((per_step, 1, GATE_LANES), lambda b, i: (b * (t // tm) + i, 0, 0))]
        out_shape += [jax.ShapeDtypeStruct((bsz, t, d), BF16), jax.ShapeDtypeStruct((bsz, t, GATE_LANES), F32),
                      jax.ShapeDtypeStruct((bsz * t // MOE_CHUNK, 1, GATE_LANES), F32)]
    return pl.pallas_call(
        functools.partial(_mixer_tail_kernel, route=routing is not None),
        grid=(bsz, t // tm),
        in_specs=in_specs,
        out_specs=out_specs,
        out_shape=out_shape,
        compiler_params=_cparams(("parallel", "parallel")),
        name="mixer_tail",
    )(*args)


def _ffn_kernel(h_ref, g_ref, w1_ref, w3_ref, w2_ref, o_ref, *, tf):
    x = h_ref[...]
    hn = _rms(x, g_ref[...]).astype(BF16)
    acc = x
    for f0 in range(0, w1_ref.shape[1], tf):
        a = _dot(hn, w1_ref[:, f0:f0 + tf])
        b = _dot(hn, w3_ref[:, f0:f0 + tf])
        acc = acc + _dot((a * jax.nn.sigmoid(a) * b).astype(BF16), w2_ref[f0:f0 + tf, :])
    o_ref[...] = acc


def ffn(h, g, w1, w3, w2, tm=1024, tf=704):
    m, d = h.shape
    tm = min(tm, m)
    nf = w1.shape[1]
    assert nf % tf == 0
    resident = lambda shape: pl.BlockSpec(shape, lambda i: (0, 0), pipeline_mode=pl.Buffered(1))
    return pl.pallas_call(
        functools.partial(_ffn_kernel, tf=tf),
        grid=(m // tm,),
        in_specs=[pl.BlockSpec((tm, d), lambda i: (i, 0)),
                  pl.BlockSpec((1, d), lambda i: (0, 0)),
                  resident((d, nf)), resident((d, nf)), resident((nf, d))],
        out_specs=pl.BlockSpec((tm, d), lambda i: (i, 0)),
        out_shape=jax.ShapeDtypeStruct((m, d), F32),
        compiler_params=_cparams(("parallel",)),
        name="ffn",
    )(h, g.reshape(1, d), w1, w3, w2)


GATE_LANES = LANES
MOE_CHUNK = 256
MOE_ROWS = LANES
MOE_TOKENS = 2048
MOE_ALIGN = BF16_TILE_ROWS


def _moe_kernel(cb_ref, h_ref, hn_ref, gate_ref, w1_ref, w3_ref, w2_ref, *refs, final):
    if final:
        gfin_ref, *refs = refs
    y_ref, rank_row, rk_ref, gs_ref, xs_ref, yacc_ref, rc_ref, gc_ref = refs
    t = pl.program_id(0)
    e = pl.program_id(1)
    f = pl.program_id(2)
    ts, d = hn_ref.shape
    ch, rb = MOE_CHUNK, MOE_ROWS
    win = 2 * rb
    n_ch = ts // ch
    lane = lax.broadcasted_iota(jnp.int32, (1, GATE_LANES), 1)

    def before(c):
        return cb_ref[(t * (n_ch + 1) + c) * N_EXPERTS + e]

    n_blocks = (before(n_ch) + rb - 1) // rb
    n_windows = (n_blocks * rb + win - 1) // win + 1

    def windows(c):
        lo, hi = before(c), before(c + 1)
        s0 = (lo // MOE_ALIGN) * MOE_ALIGN
        return s0, jnp.where(hi > lo, (hi - s0 + win - 1) // win, 0)

    spans = [windows(c) for c in range(n_ch)]
    single = functools.reduce(jnp.logical_and, [n_win <= 1 for _, n_win in spans])

    def for_each_window(visit):
        @pl.when(single)
        def _():
            for c, (s0, _) in enumerate(spans):
                visit(c, s0)

        @pl.when(jnp.logical_not(single))
        def _():
            for c, (s0, n_win) in enumerate(spans):
                def body(j, carry, c=c, s0=s0):
                    visit(c, s0 + j * win)
                    return carry
                lax.fori_loop(0, n_win, body, 0)

    @pl.when((e == 0) & (f == 0))
    def _():
        y_ref[...] = h_ref[...]
        strict_lower = (lax.broadcasted_iota(jnp.int32, (ch, ch), 1)
                        < lax.broadcasted_iota(jnp.int32, (ch, ch), 0)).astype(BF16)
        lane_f = lane.astype(F32)
        offs = jnp.ones((1, GATE_LANES), F32)
        for c in range(n_ch):
            rows = slice(c * ch, (c + 1) * ch)
            g = gate_ref[rows, :]
            sel = jnp.where((lane_f == g[:, N_EXPERTS:N_EXPERTS + 1])
                            | (lane_f == g[:, N_EXPERTS + 1:N_EXPERTS + 2]), 1.0, 0.0)
            r = _dot(strict_lower, sel.astype(BF16)) + offs
            r = jnp.where(sel > 0.0, r, 0.0)
            rank_row[:, rows] = jnp.transpose(r) - 1.0
            high = jnp.floor(r * (1.0 / 256.0))
            rk_ref[rows, :] = jnp.concatenate([high, r - 256.0 * high], axis=1).astype(BF16)
            gs_ref[rows, :] = jnp.concatenate(_split_bf16(g, 3), axis=1)
            offs = offs + jnp.sum(sel, axis=0, keepdims=True)

    @pl.when(f == 0)
    def _():
        pick = jnp.where(lax.broadcasted_iota(jnp.int32, (GATE_LANES, GATE_LANES), 0) == e, 1.0, 0.0)
        pick_rank = jnp.concatenate([256.0 * pick, pick], axis=0).astype(BF16)
        pick_gate = jnp.concatenate([pick] * 3, axis=0).astype(BF16)
        for c in range(n_ch):
            rows = slice(c * ch, (c + 1) * ch)
            rc_ref[rows, :] = _dot(rk_ref[rows, :], pick_rank) - 1.0
            gc_ref[rows, :] = _dot(gs_ref[rows, :], pick_gate)

        def clear(j, carry):
            r0 = pl.multiple_of(j * win, win)
            xs_ref[pl.ds(r0, win), :] = jnp.zeros((win, d), BF16)
            yacc_ref[pl.ds(r0, win), :] = jnp.zeros((win, d), F32)
            return carry

        lax.fori_loop(0, n_windows, clear, 0)

        row_id = lax.broadcasted_iota(jnp.int32, (win, ch), 0)

        def gather_window(c, s):
            s = pl.multiple_of(s, MOE_ALIGN)
            ranks = rank_row[pl.ds(e, 1), c * ch:(c + 1) * ch]
            onehot = jnp.where(ranks == (s + row_id).astype(F32), 1.0, 0.0).astype(BF16)
            xs_ref[pl.ds(s, win), :] += _dot(onehot, hn_ref[c * ch:(c + 1) * ch, :]).astype(BF16)

        for_each_window(gather_window)

    def ffn_rows(r0, rows):
        x = xs_ref[pl.ds(r0, rows), :]
        a = _dot(x, w1_ref[0])
        g3 = _dot(x, w3_ref[0])
        yacc_ref[pl.ds(r0, rows), :] += _dot((a * jax.nn.sigmoid(a) * g3).astype(BF16), w2_ref[0])

    five = n_blocks == 5
    half_tail = before(n_ch) <= 4 * rb + rb // 2

    @pl.when(five & half_tail)
    def _():
        ffn_rows(0, 4 * rb + rb // 2)

    @pl.when(five & jnp.logical_not(half_tail))
    def _():
        ffn_rows(0, 5 * rb)

    @pl.when(jnp.logical_not(five))
    def _():
        n_quads = n_blocks // 4

        def ffn_quad(j, carry):
            ffn_rows(pl.multiple_of(j * (4 * rb), 4 * rb), 4 * rb)
            return carry

        lax.fori_loop(0, n_quads, ffn_quad, 0)
        tail = pl.multiple_of(n_quads * (4 * rb), 4 * rb)

        @pl.when((n_blocks & 2) != 0)
        def _():
            ffn_rows(tail, 2 * rb)

        @pl.when((n_blocks & 1) != 0)
        def _():
            ffn_rows(pl.multiple_of(tail + (n_blocks & 2) * rb, rb), rb)

    @pl.when(f == pl.num_programs(2) - 1)
    def _():
        def to_bf16(j, carry):
            r0 = pl.multiple_of(j * win, win)
            xs_ref[pl.ds(r0, win), :] = yacc_ref[pl.ds(r0, win), :].astype(BF16)
            return carry

        lax.fori_loop(0, n_windows, to_bf16, 0)

        lane_id = lax.broadcasted_iota(jnp.int32, (ch, rb), 1).astype(F32)

        def scatter_window(c, s):
            rows = slice(c * ch, (c + 1) * ch)
            s = pl.multiple_of(s, MOE_ALIGN)
            rank = rc_ref[rows, :] - s.astype(F32)
            onehot = jnp.concatenate([jnp.where(rank == lane_id, 1.0, 0.0),
                                      jnp.where(rank == lane_id + float(rb), 1.0, 0.0)],
                                     axis=1).astype(BF16)
            gate = jnp.concatenate([gc_ref[rows, :]] * (d // GATE_LANES), axis=1)
            y_ref[rows, :] += gate * _dot(onehot, xs_ref[pl.ds(s, win), :])

        for_each_window(scatter_window)

        if final:
            @pl.when(e == pl.num_programs(1) - 1)
            def _():
                y_ref[...] = _rms(y_ref[...], gfin_ref[...])


def moe(h, hn, gates, cnt, w1, w3, w2, final_gain=None, tf=896):
    m, d = hn.shape
    ts = min(MOE_TOKENS, m)
    n_exp, _, nf = w1.shape
    assert nf % tf == 0 and m % ts == 0 and ts % MOE_CHUNK == 0 and MOE_ROWS == GATE_LANES
    n_ch = ts // MOE_CHUNK
    counts = cnt.reshape(m // ts, n_ch, GATE_LANES)[:, :, :n_exp].astype(jnp.int32)
    before = jnp.concatenate([jnp.zeros((m // ts, 1, n_exp), jnp.int32), jnp.cumsum(counts, axis=1)], axis=1)
    per_tile = lambda width: pl.BlockSpec((ts, width), lambda t, e, f, cb: (t, 0), pipeline_mode=pl.Buffered(1))
    in_specs = [per_tile(d), per_tile(d), per_tile(GATE_LANES),
                pl.BlockSpec((1, d, tf), lambda t, e, f, cb: (e, 0, f)),
                pl.BlockSpec((1, d, tf), lambda t, e, f, cb: (e, 0, f)),
                pl.BlockSpec((1, tf, d), lambda t, e, f, cb: (e, f, 0))]
    args = [before.reshape(-1), h, hn, gates, w1, w3, w2]
    if final_gain is not None:
        in_specs.append(pl.BlockSpec((1, d), lambda t, e, f, cb: (0, 0)))
        args.append(final_gain.reshape(1, d))
    grid_spec = pltpu.PrefetchScalarGridSpec(
        num_scalar_prefetch=1,
        grid=(m // ts, n_exp, nf // tf),
        in_specs=in_specs,
        out_specs=per_tile(d),
        scratch_shapes=[pltpu.VMEM((GATE_LANES, ts), F32),
                        pltpu.VMEM((ts, 2 * GATE_LANES), BF16),
                        pltpu.VMEM((ts, 3 * GATE_LANES), BF16),
                        pltpu.VMEM((ts + 2 * MOE_ROWS, d), BF16),
                        pltpu.VMEM((ts + 2 * MOE_ROWS, d), F32),
                        pltpu.VMEM((ts, MOE_ROWS), F32),
                        pltpu.VMEM((ts, GATE_LANES), F32)])
    return pl.pallas_call(
        functools.partial(_moe_kernel, final=final_gain is not None),
        grid_spec=grid_spec,
        out_shape=jax.ShapeDtypeStruct((m, d), F32),
        compiler_params=_cparams(("parallel", "arbitrary", "arbitrary")),
        name="moe",
    )(*args)


def _final_norm_kernel(g_ref, x_ref, o_ref):
    o_ref[...] = _rms(x_ref[...], g_ref[...])


def final_rmsnorm(g, x, tm=1024):
    m, d = x.shape
    tm = min(tm, m)
    return pl.pallas_call(
        _final_norm_kernel,
        grid=(m // tm,),
        in_specs=[pl.BlockSpec((1, d), lambda i: (0, 0)), pl.BlockSpec((tm, d), lambda i: (i, 0))],
        out_specs=pl.BlockSpec((tm, d), lambda i: (i, 0)),
        out_shape=jax.ShapeDtypeStruct((m, d), F32),
        compiler_params=_cparams(("parallel",)),
        name="final_norm",
    )(g.reshape(1, d), x)


def _pad_cols(w, width):
    return jnp.pad(w, ((0, 0), (0, width - w.shape[1])))


def _in_proj_weight(w_in):
    a_end = 2 * GLA_HEADS * GLA_DK + 2 * GROUP_WIDTH + GLA_RANK
    b_end = a_end + GROUP_WIDTH
    c_end = b_end + 3 * GROUP_WIDTH + IDX_HEADS * IDX_DIM + IDX_DIM + IDX_HEADS
    assert w_in.shape[1] == c_end + 2 * GROUP_WIDTH
    return jnp.concatenate([_pad_cols(w_in[:, :a_end], A_WIDTH), w_in[:, a_end:b_end],
                            _pad_cols(w_in[:, b_end:c_end], C_MAIN + C_AUX), w_in[:, c_end:]], axis=1).astype(BF16)


def hybrid_layer(h, mem2, toe, p, bsz, t, mixer_weights, routing=None):
    m, d = h.shape
    z_a, z_b, z_c, z_aux, z_d = norm_matmul(h, p["norm_mix"], _in_proj_weight(p["w_in"]),
                                            (A_WIDTH, B_WIDTH, C_MAIN, C_AUX, D_WIDTH),
                                            (F32, F32, BF16, F32, F32))
    o_a = gla_mixer(z_a.reshape(bsz, t, A_WIDTH), p["gla_wa2"], p["gla_ba"], p["gla_norm"])
    o_b = pool_mixer(z_b.reshape(bsz, t, B_WIDTH), p["pool_w"], p["pool_b"].reshape(-1), p["pool_scale"])
    o_c, *weights = dsa_mixer(z_c.reshape(bsz, t, C_MAIN), z_aux.reshape(bsz, t, C_AUX), toe, mixer_weights)
    o_d = sgu_mixer(z_d.reshape(bsz, t, D_WIDTH), p["sgu_ln_g"], p["sgu_ln_b"], p["sgu_w"], p["sgu_b"])
    hd = XA_HEADS * XA_DH
    k, v = norm_matmul(mem2, p["norm_mem"], p["xa_wkv"].astype(BF16), (hd, hd))
    mlen = mem2.shape[0] // bsz
    outs = mixer_tail(h.reshape(bsz, t, d), (o_a, o_b, o_c, o_d), p["w_out"].astype(BF16), p["norm_xa"],
                      p["xa_wq"].astype(BF16), k.reshape(bsz, mlen, hd), v.reshape(bsz, mlen, hd),
                      p["xa_wo"].astype(BF16), routing)
    if routing is None:
        return weights, outs[0].reshape(m, d)
    h3, hn, gates, counts = outs
    return weights, h3.reshape(m, d), hn.reshape(m, d), gates.reshape(m, GATE_LANES), counts


def kernel(x, mem, rel_bias, final_norm, norm_mix, w_in, gla_wa2, gla_ba, gla_norm, pool_w, pool_b,
           pool_scale, sgu_ln_g, sgu_ln_b, sgu_w, sgu_b, w_out, norm_xa, norm_mem, xa_wq, xa_wkv, xa_wo,
           norm_ffn, ffn_w1, ffn_w3, ffn_w2, router, moe_w1, moe_w3, moe_w2):
    bsz, t, d = x.shape
    depth = norm_mix.shape[0]
    h = x.reshape(bsz * t, d)
    mem2 = mem.reshape(-1, d)
    toe = rel_bias_tables(rel_bias)
    normed = False
    for i in range(depth):
        p = dict(norm_mix=norm_mix[i], w_in=w_in[i], gla_wa2=gla_wa2[i], gla_ba=gla_ba[i],
                 gla_norm=gla_norm[i], pool_w=pool_w[i], pool_b=pool_b[i], pool_scale=pool_scale[i],
                 sgu_ln_g=sgu_ln_g[i], sgu_ln_b=sgu_ln_b[i], sgu_w=sgu_w[i], sgu_b=sgu_b[i],
                 w_out=w_out[i], norm_xa=norm_xa[i], norm_mem=norm_mem[i], xa_wq=xa_wq[i],
                 xa_wkv=xa_wkv[i], xa_wo=xa_wo[i])
        j = i // 2
        if i % 2 == 0:
            (w1, w3, w2), h = hybrid_layer(h, mem2, toe, p, bsz, t, (ffn_w1[j], ffn_w3[j], ffn_w2[j]))
            h = ffn(h, norm_ffn[i], w1, w3, w2)
        else:
            (w1, w3, w2), h, hn, gates, counts = hybrid_layer(h, mem2, toe, p, bsz, t,
                                                              (moe_w1[j], moe_w3[j], moe_w2[j]),
                                                              routing=(norm_ffn[i], router[j]))
            normed = i == depth - 1
            h = moe(h, hn, gates, counts, w1, w3, w2, final_gain=final_norm if normed else None)
    if not normed:
        h = final_rmsnorm(final_norm, h)
    return h.reshape(bsz, t, d)
```

```python
import functools
import math

import jax
import jax.numpy as jnp
import numpy as np
from jax import lax
from jax.experimental import pallas as pl
from jax.experimental.pallas import tpu as pltpu

F32 = jnp.float32
BF16 = jnp.bfloat16
EPS = 1e-6

LANES = 128
BF16_TILE_ROWS = 16
VMEM_LIMIT = 56 * 1024 * 1024

GROUP_WIDTH = 256

GLA_HEADS = 4
GLA_DV = 64
GLA_DK = 32
GLA_RANK = 16
GLA_TAU = 16.0
GLA_CHUNK = 64
GLA_GROUP = 16

POOL_WINDOWS = (2, 4, 8, 16)
POOL_CG = 64

DSA_HEADS = 4
DSA_DH = 64
IDX_HEADS = 8
IDX_DIM = 32
DSA_TOPK_MAX = 256
DSA_BLOCK = 256
DSA_SUB = 64
DSA_HEAD_GROUPS = ((0, 1, 2, 3),)

SGU_GROUPS = 4
SGU_CHUNK = 128
SGU_CG = 64

REL_BUCKETS = 32
REL_MAX_DIST = 128

XA_HEADS = 4
XA_DH = 64

N_EXPERTS = 8

A_WIDTH = 896
B_WIDTH = 256
C_MAIN = 1024
C_AUX = LANES
D_WIDTH = 512

MIXER_DTYPE = BF16
INT_MIN = -(2 ** 31)
NEG_BIG = -1e30


def _cparams(sem):
    return pltpu.CompilerParams(dimension_semantics=sem, vmem_limit_bytes=VMEM_LIMIT)


def _dot(a, b):
    return jnp.dot(a, b, preferred_element_type=F32)


def _dot_nt(a, b):
    return lax.dot_general(a, b, (((1,), (1,)), ((), ())), preferred_element_type=F32)


def _dot_tn(a, b):
    return lax.dot_general(a, b, (((0,), (0,)), ((), ())), preferred_element_type=F32)


def _split_bf16(x, terms):
    parts = []
    for _ in range(terms):
        p = x.astype(BF16)
        parts.append(p)
        x = x - p.astype(F32)
    return parts


def _dot_exact_rhs(a, b, terms):
    out = None
    for p in _split_bf16(a, terms):
        d = _dot(p, b)
        out = d if out is None else out + d
    return out


def _dot_exact_lhs(a, b, terms):
    out = None
    for p in _split_bf16(b, terms):
        d = _dot(a, p)
        out = d if out is None else out + d
    return out


def _dot_3pass(a, b):
    a_hi, a_lo = _split_bf16(a, 2)
    b_hi, b_lo = _split_bf16(b, 2)
    return _dot(a_hi, b_hi) + (_dot(a_hi, b_lo) + _dot(a_lo, b_hi))


def _rms(x, g):
    return x * lax.rsqrt(jnp.mean(x * x, axis=-1, keepdims=True) + EPS) * g


def _norm_matmul_kernel(h_ref, g_ref, w_ref, *out_refs, widths):
    hb = _rms(h_ref[...], g_ref[...]).astype(BF16)
    off = 0
    for o_ref, wd in zip(out_refs, widths):
        o_ref[...] = _dot(hb, w_ref[:, off:off + wd]).astype(o_ref.dtype)
        off += wd


def norm_matmul(h, g, w, widths, dtypes=None, tm=1024):
    dtypes = dtypes or (F32,) * len(widths)
    m, d = h.shape
    tm = min(tm, m)
    n = sum(widths)
    return pl.pallas_call(
        functools.partial(_norm_matmul_kernel, widths=widths),
        grid=(m // tm,),
        in_specs=[pl.BlockSpec((tm, d), lambda i: (i, 0)),
                  pl.BlockSpec((1, d), lambda i: (0, 0)),
                  pl.BlockSpec((d, n), lambda i: (0, 0))],
        out_specs=[pl.BlockSpec((tm, wd), lambda i: (i, 0)) for wd in widths],
        out_shape=[jax.ShapeDtypeStruct((m, wd), dt) for wd, dt in zip(widths, dtypes)],
        compiler_params=_cparams(("parallel",)),
        name="norm_matmul",
    )(h, g.reshape(1, d), w)


def _log_sigmoid(x):
    return jnp.minimum(x, 0.0) - jnp.log1p(jnp.exp(-jnp.abs(x)))


def _gla_kernel(z_ref, wa2_ref, ba_ref, ng_ref, o_ref, s_ref, *, group, n_groups):
    c = GLA_CHUNK
    grp = group * c
    hk = GLA_HEADS * GLA_DK
    hv = GLA_HEADS * GLA_DV
    s_ref[...] = jnp.zeros_like(s_ref)

    head_k = lax.broadcasted_iota(jnp.int32, (1, hk), 1) // GLA_DK
    head_v = lax.broadcasted_iota(jnp.int32, (1, hv), 1) // GLA_DV
    cum = min(grp, 256)
    g_row = lax.broadcasted_iota(jnp.int32, (cum, cum), 0)
    g_col = lax.broadcasted_iota(jnp.int32, (cum, cum), 1)
    tril = (((g_row // c) == (g_col // c)) & (g_col <= g_row)).astype(BF16)
    causal4 = (lax.broadcasted_iota(jnp.int32, (GLA_HEADS * c, c), 1)
               <= lax.broadcasted_iota(jnp.int32, (GLA_HEADS * c, c), 0) % c)
    state_mask = (lax.broadcasted_iota(jnp.int32, (hk, hv), 0) // GLA_DK
                  == lax.broadcasted_iota(jnp.int32, (hk, hv), 1) // GLA_DV)
    norm_mat = jnp.where(lax.broadcasted_iota(jnp.int32, (hv, hv), 0) // GLA_DV
                         == lax.broadcasted_iota(jnp.int32, (hv, hv), 1) // GLA_DV,
                         1.0 / GLA_DV, 0.0).astype(BF16)
    wa2 = wa2_ref[...]
    ba = ba_ref[...]
    ng = ng_ref[...]

    def body(n, carry):
        r0 = pl.multiple_of(n * grp, grp)
        z = z_ref[0, pl.ds(r0, grp), :]
        q, k, v, g, lr = z[:, 0:128], z[:, 128:256], z[:, 256:512], z[:, 512:768], z[:, 768:896]
        log_a = _log_sigmoid(_dot_3pass(lr, wa2) + ba) / GLA_TAU
        b = jnp.concatenate([_dot_exact_lhs(tril, log_a[r:r + cum], 3) for r in range(0, grp, cum)], axis=0)
        b_end = jnp.concatenate([jnp.broadcast_to(b[(ci + 1) * c - 1:(ci + 1) * c, :], (c, hk))
                                 for ci in range(group)], axis=0)
        q_t = q * (GLA_DK ** -0.5) * jnp.exp(b)
        q_tb = q_t.astype(BF16)
        k_t = (k * jnp.exp(-b)).astype(BF16)
        k_dec = (k * jnp.exp(b_end - b)).astype(BF16)
        vb = v.astype(BF16)
        outs = []
        for ci in range(group):
            rows = slice(ci * c, (ci + 1) * c)
            q4 = jnp.concatenate([jnp.where(head_k == h, q_t[rows], 0.0) for h in range(GLA_HEADS)],
                                 axis=0).astype(BF16)
            att = jnp.where(causal4, _dot_nt(q4, k_t[rows]), 0.0)
            r = _dot(att.astype(BF16), vb[rows])
            o = _dot(q_tb[rows], s_ref[...].astype(BF16))
            for h in range(GLA_HEADS):
                o = o + jnp.where(head_v == h, r[h * c:(h + 1) * c, :], 0.0)
            outs.append(o)
            kv = jnp.where(state_mask, _dot_tn(k_dec[rows], vb[rows]), 0.0)
            last = b_end[ci * c:ci * c + 1, :]
            dec = jnp.exp(jnp.transpose(jnp.broadcast_to(last, (hk, hk))))
            s_ref[...] = s_ref[...] * jnp.concatenate([dec, dec], axis=1) + kv
        o = jnp.concatenate(outs, axis=0)
        o = o * lax.rsqrt(_dot_exact_rhs(o * o, norm_mat, 2) + EPS) * ng
        o_ref[0, pl.ds(r0, grp), :] = (o * (g * jax.nn.sigmoid(g))).astype(o_ref.dtype)
        return carry

    lax.fori_loop(0, n_groups, body, 0)


def gla_mixer(z_a, wa2, ba, norm_g):
    bsz, t, _ = z_a.shape
    hk = GLA_HEADS * GLA_DK
    wa2p = jnp.zeros((LANES, hk), F32).at[:GLA_RANK].set(wa2)
    n_chunks = t // GLA_CHUNK
    group = math.gcd(GLA_GROUP, n_chunks)
    return pl.pallas_call(
        functools.partial(_gla_kernel, group=group, n_groups=n_chunks // group),
        grid=(bsz,),
        in_specs=[pl.BlockSpec((1, t, A_WIDTH), lambda b: (b, 0, 0)),
                  pl.BlockSpec((LANES, hk), lambda b: (0, 0)),
                  pl.BlockSpec((1, hk), lambda b: (0, 0)),
                  pl.BlockSpec((1, GROUP_WIDTH), lambda b: (0, 0))],
        out_specs=pl.BlockSpec((1, t, GROUP_WIDTH), lambda b: (b, 0, 0)),
        out_shape=jax.ShapeDtypeStruct((bsz, t, GROUP_WIDTH), MIXER_DTYPE),
        scratch_shapes=[pltpu.VMEM((hk, GROUP_WIDTH), F32)],
        compiler_params=_cparams(("parallel",)),
        name="gla_mixer",
    )(z_a, wa2p, ba.reshape(1, hk), norm_g.reshape(1, GROUP_WIDTH))


def _pool_kernel(u_ref, w_ref, b_ref, sc_ref, o_ref):
    u = u_ref[0]
    t, gw = u.shape
    row = lax.broadcasted_iota(jnp.int32, (t, gw), 0)
    grp = lax.broadcasted_iota(jnp.int32, (t, gw), 1) // POOL_CG

    def shifted(x, k):
        return jnp.where(row >= k, pltpu.roll(x, k, axis=0), 0.0)

    s = u
    p = jnp.zeros_like(u)
    for gi, win in enumerate(POOL_WINDOWS):
        half = win // 2
        s = s + shifted(s, half)
        cnt = jnp.minimum(row + 1, win).astype(F32)
        p = jnp.where(grp == gi, s / cnt - u, p)
    y = _dot(p.astype(BF16), w_ref[...]) + b_ref[...]
    o_ref[0] = (y * sc_ref[...]).astype(o_ref.dtype)


def pool_mixer(z_b, w, b, scale):
    assert POOL_WINDOWS == (2, 4, 8, 16)
    bsz, t, gw = z_b.shape
    w_bd = jnp.zeros((gw, gw), F32)
    for gi in range(len(POOL_WINDOWS)):
        w_bd = w_bd.at[gi * POOL_CG:(gi + 1) * POOL_CG, gi * POOL_CG:(gi + 1) * POOL_CG].set(w[gi])
    return pl.pallas_call(
        _pool_kernel,
        grid=(bsz,),
        in_specs=[pl.BlockSpec((1, t, gw), lambda i: (i, 0, 0)),
                  pl.BlockSpec((gw, gw), lambda i: (0, 0)),
                  pl.BlockSpec((1, gw), lambda i: (0, 0)),
                  pl.BlockSpec((1, gw), lambda i: (0, 0))],
        out_specs=pl.BlockSpec((1, t, gw), lambda i: (i, 0, 0)),
        out_shape=jax.ShapeDtypeStruct((bsz, t, gw), MIXER_DTYPE),
        compiler_params=_cparams(("parallel",)),
        name="pool_mixer",
    )(z_b, w_bd.astype(BF16), b.reshape(1, gw), scale.reshape(1, gw))


def _sgu_kernel(z_ref, lg_ref, lb_ref, w_ref, bm_ref, o_ref, *, chunks):
    c = SGU_CHUNK
    gw = GROUP_WIDTH
    rows = SGU_GROUPS * c
    tri = (lax.broadcasted_iota(jnp.int32, (rows, c), 1)
           <= lax.broadcasted_iota(jnp.int32, (rows, c), 0) % c)
    ws = jnp.where(tri, w_ref[...], 0.0).astype(BF16)
    grp = lax.broadcasted_iota(jnp.int32, (1, gw), 1) // SGU_CG
    for ci in range(chunks):
        z = jax.nn.gelu(z_ref[0, ci * c:(ci + 1) * c, :], approximate=True)
        u, v = z[:, :gw], z[:, gw:]
        mu = jnp.mean(v, axis=-1, keepdims=True)
        var = jnp.mean(jnp.square(v - mu), axis=-1, keepdims=True)
        vn = (v - mu) * lax.rsqrt(var + EPS) * lg_ref[...] + lb_ref[...]
        r = _dot(ws, vn.astype(BF16))
        mixed = bm_ref[...]
        for g in range(SGU_GROUPS):
            mixed = mixed + jnp.where(grp == g, r[g * c:(g + 1) * c, :], 0.0)
        o_ref[0, ci * c:(ci + 1) * c, :] = (u * mixed).astype(o_ref.dtype)


def sgu_mixer(z_d, ln_g, ln_b, w_s, b_s, chunks=16):
    bsz, t, _ = z_d.shape
    gw = GROUP_WIDTH
    chunks = math.gcd(chunks, t // SGU_CHUNK)
    tt = chunks * SGU_CHUNK
    bias = jnp.repeat(b_s.T, SGU_CG, axis=1)
    return pl.pallas_call(
        functools.partial(_sgu_kernel, chunks=chunks),
        grid=(bsz, t // tt),
        in_specs=[pl.BlockSpec((1, tt, 2 * gw), lambda b, i: (b, i, 0)),
                  pl.BlockSpec((1, gw), lambda b, i: (0, 0)),
                  pl.BlockSpec((1, gw), lambda b, i: (0, 0)),
                  pl.BlockSpec((SGU_GROUPS * SGU_CHUNK, SGU_CHUNK), lambda b, i: (0, 0)),
                  pl.BlockSpec((SGU_CHUNK, gw), lambda b, i: (0, 0))],
        out_specs=pl.BlockSpec((1, tt, gw), lambda b, i: (b, i, 0)),
        out_shape=jax.ShapeDtypeStruct((bsz, t, gw), MIXER_DTYPE),
        compiler_params=_cparams(("parallel", "parallel")),
        name="sgu_mixer",
    )(z_d, ln_g.reshape(1, gw), ln_b.reshape(1, gw),
      w_s.reshape(SGU_GROUPS * SGU_CHUNK, SGU_CHUNK), bias)


def _bucket_table():
    assert REL_MAX_DIST <= DSA_BLOCK + 1
    s = np.arange(DSA_BLOCK)[:, None]
    t = np.arange(DSA_BLOCK)[None, :]
    dist = np.stack([t - s, DSA_BLOCK + t - s, 2 * DSA_BLOCK + t - s])
    n = np.maximum(dist, 0)
    max_exact = REL_BUCKETS // 2
    nf = np.maximum(n, 1).astype(np.float32)
    large = max_exact + (np.log(nf / np.float32(max_exact)) / np.float32(math.log(REL_MAX_DIST / max_exact))
                         * np.float32(REL_BUCKETS - max_exact)).astype(np.int32)
    return np.where(n < max_exact, n, np.minimum(large, REL_BUCKETS - 1)).astype(np.int32)


def _bias_table_kernel(rb_ref, bucket_ref, o_ref):
    for back in range(3):
        bucket = bucket_ref[back]
        for h in range(DSA_HEADS):
            acc = jnp.zeros(bucket.shape, F32)
            for b in range(REL_BUCKETS):
                acc = jnp.where(bucket == b, rb_ref[b * DSA_HEADS + h], acc)
            o_ref[back, h] = acc


def rel_bias_tables(rel_bias):
    blk = DSA_BLOCK
    return pl.pallas_call(
        _bias_table_kernel,
        in_specs=[pl.BlockSpec(memory_space=pltpu.SMEM),
                  pl.BlockSpec((3, blk, blk), lambda: (0, 0, 0))],
        out_specs=pl.BlockSpec((3, DSA_HEADS, blk, blk), lambda: (0, 0, 0, 0)),
        out_shape=jax.ShapeDtypeStruct((3, DSA_HEADS, blk, blk), F32),
        name="rel_bias_tables",
    )(rel_bias.reshape(-1), jnp.asarray(_bucket_table()))


def _dsa_kernel(q_ref, k_ref, v_ref, qi_ref, kw_ref, qw_ref, toe_ref, *refs, topk, n_blocks, idx_bits, n_cast):
    for src, dst in zip(refs[:n_cast], refs[n_cast + 1:2 * n_cast + 1]):
        dst[...] = src[...].astype(BF16)
    o_ref = refs[n_cast]
    kpl_ref, vt_ref, keys_ref, khi_ref, klo_ref, acc_ref, am_ref, lg_ref, p_ref = refs[2 * n_cast + 1:]
    blk = DSA_BLOCK
    sub = DSA_SUB
    i = pl.program_id(1)
    hd = DSA_HEADS * DSA_DH
    heads_per_half = LANES // IDX_DIM

    @pl.when(i == 0)
    def _():
        lane = lax.broadcasted_iota(jnp.int32, (blk, LANES), 1)

        def build(kb, c):
            r0 = pl.multiple_of(kb * blk, blk)
            ki = jnp.where(lane < IDX_DIM, kw_ref[0, pl.ds(r0, blk), :], 0.0)
            for j in range(heads_per_half):
                kpl_ref[kb, j] = (ki if j == 0 else pltpu.roll(ki, j * IDX_DIM, axis=1)).astype(BF16)
            vt_ref[kb] = jnp.transpose(v_ref[0, pl.ds(r0, blk), :].astype(F32)).astype(BF16)
            return c

        lax.fori_loop(0, n_blocks, build, 0)

    s_loc = lax.broadcasted_iota(jnp.int32, (blk, blk), 0)
    t_loc = lax.broadcasted_iota(jnp.int32, (blk, blk), 1)
    n_vis = i + 1

    qi_t = jnp.transpose(qi_ref[0].astype(F32)).astype(BF16)
    qi_halves = [qi_t[:LANES, :], qi_t[LANES:, :]]
    w_t = jnp.transpose(qw_ref[0])
    w_rows = [w_t[IDX_DIM + h:IDX_DIM + h + 1, :] * (IDX_HEADS ** -0.5) * (IDX_DIM ** -0.5)
              for h in range(IDX_HEADS)]
    s_sub = lax.broadcasted_iota(jnp.int32, (sub, blk), 0)
    t_sub = lax.broadcasted_iota(jnp.int32, (sub, blk), 1)

    def score_body(kb, c):
        for ci in range(blk // sub):
            rows = slice(ci * sub, (ci + 1) * sub)
            sc = jnp.zeros((sub, blk), F32)
            for half in range(2):
                for j in range(heads_per_half):
                    d = _dot(kpl_ref[kb, j, rows, :], qi_halves[half])
                    sc = sc + jnp.maximum(d, 0.0) * w_rows[half * heads_per_half + j]
            sc = jnp.where(sc == 0.0, 0.0, sc)
            bits = pltpu.bitcast(sc, jnp.int32)
            key = jnp.where(bits < 0, bits ^ jnp.int32(0x7FFFFFFF), bits)
            vis = (kb < i) | (s_sub + ci * sub <= t_sub)
            key = jnp.where(vis, key, jnp.int32(INT_MIN))
            keys_ref[kb, rows, :] = key
            khi_ref[kb, rows, :] = lax.shift_right_arithmetic(key, 16).astype(jnp.int16)
            klo_ref[kb, rows, :] = ((key & 0xFFFF) - 2 ** 15).astype(jnp.int16)
        return c

    lax.fori_loop(0, n_vis, score_body, 0)

    def count(pred):
        def body(kb, acc):
            hit = jnp.where(pred(keys_ref[kb], kb), 1.0, 0.0)
            return acc + jnp.sum(hit.reshape(blk // 32, 32, blk), axis=0)
        acc = lax.fori_loop(0, n_vis, body, jnp.zeros((32, blk), F32))
        return jnp.sum(acc, axis=0, keepdims=True)

    def count16(ref, cand):
        cand = cand.astype(jnp.int16)

        def body(kb, acc):
            hit = jnp.where(ref[kb] >= cand, jnp.int16(1), jnp.int16(0))
            for j in range(blk // 32):
                acc = acc + hit[j * 32:(j + 1) * 32]
            return acc
        acc = lax.fori_loop(0, n_vis, body, jnp.zeros((32, blk), jnp.int16))
        return jnp.sum(acc.astype(jnp.int32).astype(F32), axis=0, keepdims=True)

    def search16(ref, offset):
        lowest = jnp.full((1, blk), -(2 ** 15), jnp.int32)
        base = jnp.where(offset + count16(ref, jnp.zeros((1, blk), jnp.int32)) >= kf, 0, lowest)

        def bit_body(it, base):
            cand = base | lax.shift_left(jnp.int32(1), 14 - it)
            return jnp.where(offset + count16(ref, cand) >= kf, cand, base)
        return lax.fori_loop(0, 15, bit_body, base)

    kf = float(topk)
    zero = jnp.zeros((1, blk), jnp.int32)
    thr_hi = search16(khi_ref, 0.0)
    thr_hi16 = thr_hi.astype(jnp.int16)

    def low_body(kb, above):
        hi = khi_ref[kb]
        klo_ref[kb] = jnp.where(hi == thr_hi16, klo_ref[kb], jnp.int16(-(2 ** 15)))
        hit = jnp.where(hi > thr_hi16, jnp.int16(1), jnp.int16(0))
        for j in range(blk // 32):
            above = above + hit[j * 32:(j + 1) * 32]
        return above

    above = lax.fori_loop(0, n_vis, low_body, jnp.zeros((32, blk), jnp.int16))
    n_above = jnp.sum(above.astype(jnp.int32).astype(F32), axis=0, keepdims=True)
    thr_lo = search16(klo_ref, n_above)
    thr = lax.shift_left(thr_hi, 16) | (thr_lo + 2 ** 15)
    thr_sel = jnp.maximum(thr, jnp.int32(INT_MIN + 1))

    n_ge = count(lambda key, kb: key >= thr)
    excess = jnp.where((n_ge > kf) & (thr > jnp.int32(INT_MIN)), 1.0, 0.0)

    def tie_search():
        need = kf - count(lambda key, kb: key > thr)

        def tie_body(it, j):
            cand = j | lax.shift_left(jnp.int32(1), idx_bits - 1 - it)
            below = count(lambda key, kb: (key == thr) & (kb * blk + s_loc < cand))
            return jnp.where(below < need, cand, j)
        return lax.fori_loop(0, idx_bits, tie_body, zero)

    last = lax.cond(jnp.max(excess) > 0.0, tie_search, lambda: jnp.full((1, blk), 2 ** 30, jnp.int32))

    assert DSA_DH ** -0.5 == 0.125
    q_t = jnp.transpose(q_ref[0].astype(F32) * (DSA_DH ** -0.5))
    row_h = lax.broadcasted_iota(jnp.int32, (hd, 1), 0) // DSA_DH
    q_heads = [jnp.where(row_h == h, q_t, 0.0).astype(BF16) for h in range(DSA_HEADS)]
    acc_ref[...] = jnp.zeros_like(acc_ref)
    n_sub = blk // sub

    def att_body(kb, carry):
        ms, ls = carry
        back = jnp.minimum(i - kb, 2)
        for ci in range(n_sub):
            rows = slice(ci * sub, (ci + 1) * sub)
            key = keys_ref[kb, rows, :]
            sel = (key > thr_sel) | ((key == thr_sel) & (kb * blk + ci * sub + s_sub <= last))
            am_ref[rows, :] = jnp.where(sel, 0.0, NEG_BIG)
        new_ms, new_ls = list(ms), list(ls)
        for heads in DSA_HEAD_GROUPS:
            alphas = {}
            for h in heads:
                pm = jnp.full((8, blk), NEG_BIG, F32)
                for ci in range(n_sub):
                    rows = slice(ci * sub, (ci + 1) * sub)
                    k_rows = k_ref[0, pl.ds(pl.multiple_of(kb * blk + ci * sub, sub), sub), :]
                    lg = _dot(k_rows, q_heads[h]) + toe_ref[back, h, rows, :] + am_ref[rows, :]
                    lg_ref[h, rows, :] = lg
                    pm = jnp.maximum(pm, jnp.max(lg.reshape(sub // 8, 8, blk), axis=0))
                new_ms[h] = jnp.maximum(ms[h], jnp.max(pm, axis=0, keepdims=True))
                alphas[h] = jnp.exp(ms[h] - new_ms[h])
            for h in heads:
                ps = jnp.zeros((8, blk), F32)
                for ci in range(n_sub):
                    rows = slice(ci * sub, (ci + 1) * sub)
                    p = jnp.exp(lg_ref[h, rows, :] - new_ms[h])
                    ps = ps + jnp.sum(p.reshape(sub // 8, 8, blk), axis=0)
                    p_ref[h, rows, :] = p.astype(BF16)
                new_ls[h] = ls[h] * alphas[h] + jnp.sum(ps, axis=0, keepdims=True)
            for h in heads:
                hrows = slice(h * DSA_DH, (h + 1) * DSA_DH)
                acc_ref[hrows, :] = acc_ref[hrows, :] * alphas[h] + _dot(vt_ref[kb, hrows, :], p_ref[h])
        return tuple(new_ms), tuple(new_ls)

    init = (tuple(jnp.full((1, blk), 0.01 * NEG_BIG, F32) for _ in range(DSA_HEADS)),
            tuple(jnp.zeros((1, blk), F32) for _ in range(DSA_HEADS)))
    _, ls = lax.fori_loop(0, n_vis, att_body, init)
    for h in range(DSA_HEADS):
        rows = slice(h * DSA_DH, (h + 1) * DSA_DH)
        acc_ref[rows, :] = acc_ref[rows, :] * (1.0 / ls[h])
    o_ref[0] = jnp.transpose(acc_ref[...]).astype(o_ref.dtype)


def _slab_view(w, steps):
    rows = math.prod(w.shape[:-1])
    n = max(k for k in range(1, steps + 1) if rows % (k * BF16_TILE_ROWS) == 0)
    return w.reshape(n, rows // n, w.shape[-1])


def dsa_mixer(z_c, z_aux, toe, to_bf16=()):
    bsz, t, _ = z_c.shape
    blk = DSA_BLOCK
    n_blocks = t // blk
    assert t % blk == 0
    topk = min(DSA_TOPK_MAX, t // 4)
    hd = DSA_HEADS * DSA_DH
    steps = bsz * n_blocks
    slabs = [_slab_view(w, steps) for w in to_bf16]
    slab_specs = [pl.BlockSpec((1,) + s.shape[1:],
                               lambda b, i, n=s.shape[0]: (jnp.minimum(b * n_blocks + i, n - 1), 0, 0))
                  for s in slabs]
    kernel = functools.partial(_dsa_kernel, topk=topk, n_blocks=n_blocks,
                               idx_bits=max(1, (t - 1).bit_length()), n_cast=len(slabs))
    out, *converted = pl.pallas_call(
        kernel,
        grid=(bsz, n_blocks),
        in_specs=[pl.BlockSpec((1, blk, hd), lambda b, i: (b, i, 0)),
                  pl.BlockSpec((1, t, hd), lambda b, i: (b, 0, 1)),
                  pl.BlockSpec((1, t, hd), lambda b, i: (b, 0, 2)),
                  pl.BlockSpec((1, blk, hd), lambda b, i: (b, i, 3)),
                  pl.BlockSpec((1, t, C_AUX), lambda b, i: (b, 0, 0)),
                  pl.BlockSpec((1, blk, C_AUX), lambda b, i: (b, i, 0)),
                  pl.BlockSpec((3, DSA_HEADS, blk, blk), lambda b, i: (0, 0, 0, 0))] + slab_specs,
        out_specs=[pl.BlockSpec((1, blk, hd), lambda b, i: (b, i, 0))] + slab_specs,
        out_shape=[jax.ShapeDtypeStruct((bsz, t, hd), MIXER_DTYPE)]
                  + [jax.ShapeDtypeStruct(s.shape, BF16) for s in slabs],
        scratch_shapes=[pltpu.VMEM((n_blocks, LANES // IDX_DIM, blk, LANES), BF16),
                        pltpu.VMEM((n_blocks, hd, blk), BF16),
                        pltpu.VMEM((n_blocks, blk, blk), jnp.int32),
                        pltpu.VMEM((n_blocks, blk, blk), jnp.int16),
                        pltpu.VMEM((n_blocks, blk, blk), jnp.int16),
                        pltpu.VMEM((hd, blk), F32),
                        pltpu.VMEM((blk, blk), F32),
                        pltpu.VMEM((DSA_HEADS, blk, blk), F32),
                        pltpu.VMEM((DSA_HEADS, blk, blk), BF16)],
        compiler_params=_cparams(("parallel", "arbitrary")),
        name="dsa_mixer",
    )(z_c, z_c, z_c, z_c, z_aux, z_aux, toe, *slabs)
    return (out, *[c.reshape(w.shape) for c, w in zip(converted, to_bf16)])


def _mixer_tail_kernel(h_ref, a_ref, b_ref, c_ref, d_ref, wout_ref, gxa_ref, wq_ref, k_ref, v_ref, wo_ref,
                       *refs, route):
    gw = GROUP_WIDTH
    x = h_ref[0]
    for gi, r in enumerate((a_ref, b_ref, c_ref, d_ref)):
        x = x + _dot(r[0], wout_ref[gi * gw:(gi + 1) * gw, :])

    hd = XA_HEADS * XA_DH
    q = _dot(_rms(x, gxa_ref[...]).astype(BF16), wq_ref[...]).astype(BF16)
    k = k_ref[0]
    v = v_ref[0]
    lane_h = lax.broadcasted_iota(jnp.int32, (1, hd), 1) // XA_DH
    o = jnp.zeros((x.shape[0], hd), F32)
    for h in range(XA_HEADS):
        s = _dot_nt(q, jnp.where(lane_h == h, k, 0.0).astype(BF16)) * (XA_DH ** -0.5)
        p = jnp.exp(s - jnp.max(s, axis=-1, keepdims=True))
        p = p / jnp.sum(p, axis=-1, keepdims=True)
        o = o + _dot(p.astype(BF16), jnp.where(lane_h == h, v, 0.0).astype(BF16))
    x = x + _dot(o.astype(BF16), wo_ref[...])

    if not route:
        o_ref, = refs
        o_ref[0] = x
        return
    gffn_ref, wr_ref, o_ref, hn_ref, gate_ref, cnt_ref = refs
    o_ref[0] = x
    hn = _rms(x, gffn_ref[...])
    hn_ref[0] = hn.astype(BF16)
    tm = hn.shape[0]
    lane = lax.broadcasted_iota(jnp.int32, (tm, GATE_LANES), 1)
    logits = jnp.where(lane < N_EXPERTS, _dot_3pass(hn, wr_ref[...]), -jnp.inf)
    m1 = jnp.max(logits, axis=-1, keepdims=True)
    i1 = jnp.min(jnp.where(logits == m1, lane, GATE_LANES), axis=-1, keepdims=True)
    rest = jnp.where(lane == i1, -jnp.inf, logits)
    m2 = jnp.max(rest, axis=-1, keepdims=True)
    i2 = jnp.min(jnp.where(rest == m2, lane, GATE_LANES), axis=-1, keepdims=True)
    e2 = jnp.exp(m2 - m1)
    g1 = 1.0 / (1.0 + e2)
    gates = jnp.where(lane == i1, g1, 0.0) + jnp.where(lane == i2, e2 * g1, 0.0)
    gates = jnp.where(lane == N_EXPERTS, i1.astype(F32), gates)
    gate_ref[0] = jnp.where(lane == N_EXPERTS + 1, i2.astype(F32), gates)
    sel = jnp.where((lane == i1) | (lane == i2), 1.0, 0.0)
    for c in range(tm // MOE_CHUNK):
        cnt_ref[c] = jnp.sum(sel[c * MOE_CHUNK:(c + 1) * MOE_CHUNK, :], axis=0, keepdims=True)


def mixer_tail(h3, mixers, w_out, g_xa, wq, k, v, wo, routing=None, tm=1024):
    bsz, t, d = h3.shape
    tm = min(tm, t)
    mlen = k.shape[1]
    hd = XA_HEADS * XA_DH
    gw = GROUP_WIDTH
    row = lambda b, i: (b, i, 0)
    fixed2 = lambda b, i: (0, 0)
    in_specs = ([pl.BlockSpec((1, tm, d), row)] + [pl.BlockSpec((1, tm, gw), row)] * 4
                + [pl.BlockSpec((4 * gw, d), fixed2), pl.BlockSpec((1, d), fixed2), pl.BlockSpec((d, hd), fixed2),
                   pl.BlockSpec((1, mlen, hd), lambda b, i: (b, 0, 0)),
                   pl.BlockSpec((1, mlen, hd), lambda b, i: (b, 0, 0)),
                   pl.BlockSpec((hd, d), fixed2)])
    args = [h3, *mixers, w_out, g_xa.reshape(1, d), wq, k, v, wo]
    out_specs = [pl.BlockSpec((1, tm, d), row)]
    out_shape = [jax.ShapeDtypeStruct((bsz, t, d), F32)]
    if routing is not None:
        assert tm % MOE_CHUNK == 0
        g_ffn, wr = routing
        in_specs += [pl.BlockSpec((1, d), fixed2), pl.BlockSpec((d, GATE_LANES), fixed2)]
        args += [g_ffn.reshape(1, d), jnp.zeros((d, GATE_LANES), F32).at[:, :N_EXPERTS].set(wr)]
        per_step = tm // MOE_CHUNK
        out_specs += [pl.BlockSpec((1, tm, d), row), pl.BlockSpec((1, tm, GATE_LANES), row),
                      pl.BlockSpec((per_step, 1, GATE_LANES), lambda b, i: (b * (t // tm) + i, 0, 0))]
        out_shape += [jax.ShapeDtypeStruct((bsz, t, d), BF16), jax.ShapeDtypeStruct((bsz, t, GATE_LANES), F32),
                      jax.ShapeDtypeStruct((bsz * t // MOE_CHUNK, 1, GATE_LANES), F32)]
    return pl.pallas_call(
        functools.partial(_mixer_tail_kernel, route=routing is not None),
        grid=(bsz, t // tm),
        in_specs=in_specs,
        out_specs=out_specs,
        out_shape=out_shape,
        compiler_params=_cparams(("parallel", "parallel")),
        name="mixer_tail",
    )(*args)


def _ffn_kernel(h_ref, g_ref, w1_ref, w3_ref, w2_ref, o_ref, *, tf):
    x = h_ref[...]
    hn = _rms(x, g_ref[...]).astype(BF16)
    acc = x
    for f0 in range(0, w1_ref.shape[1], tf):
        a = _dot(hn, w1_ref[:, f0:f0 + tf])
        b = _dot(hn, w3_ref[:, f0:f0 + tf])
        acc = acc + _dot((a * jax.nn.sigmoid(a) * b).astype(BF16), w2_ref[f0:f0 + tf, :])
    o_ref[...] = acc


def ffn(h, g, w1, w3, w2, tm=1024, tf=704):
    m, d = h.shape
    tm = min(tm, m)
    nf = w1.shape[1]
    assert nf % tf == 0
    resident = lambda shape: pl.BlockSpec(shape, lambda i: (0, 0), pipeline_mode=pl.Buffered(1))
    return pl.pallas_call(
        functools.partial(_ffn_kernel, tf=tf),
        grid=(m // tm,),
        in_specs=[pl.BlockSpec((tm, d), lambda i: (i, 0)),
                  pl.BlockSpec((1, d), lambda i: (0, 0)),
                  resident((d, nf)), resident((d, nf)), resident((nf, d))],
        out_specs=pl.BlockSpec((tm, d), lambda i: (i, 0)),
        out_shape=jax.ShapeDtypeStruct((m, d), F32),
        compiler_params=_cparams(("parallel",)),
        name="ffn",
    )(h, g.reshape(1, d), w1, w3, w2)


GATE_LANES = LANES
MOE_CHUNK = 256
MOE_ROWS = LANES
MOE_TOKENS = 2048
MOE_ALIGN = BF16_TILE_ROWS


def _moe_kernel(cb_ref, h_ref, hn_ref, gate_ref, w1_ref, w3_ref, w2_ref, *refs, final):
    if final:
        gfin_ref, *refs = refs
    y_ref, rank_row, rk_ref, gs_ref, xs_ref, yacc_ref, rc_ref, gc_ref = refs
    t = pl.program_id(0)
    e = pl.program_id(1)
    f = pl.program_id(2)
    ts, d = hn_ref.shape
    ch, rb = MOE_CHUNK, MOE_ROWS
    win = 2 * rb
    n_ch = ts // ch
    lane = lax.broadcasted_iota(jnp.int32, (1, GATE_LANES), 1)

    def before(c):
        return cb_ref[(t * (n_ch + 1) + c) * N_EXPERTS + e]

    n_blocks = (before(n_ch) + rb - 1) // rb
    n_windows = (n_blocks * rb + win - 1) // win + 1

    def windows(c):
        lo, hi = before(c), before(c + 1)
        s0 = (lo // MOE_ALIGN) * MOE_ALIGN
        return s0, jnp.where(hi > lo, (hi - s0 + win - 1) // win, 0)

    spans = [windows(c) for c in range(n_ch)]
    single = functools.reduce(jnp.logical_and, [n_win <= 1 for _, n_win in spans])

    def for_each_window(visit):
        @pl.when(single)
        def _():
            for c, (s0, _) in enumerate(spans):
                visit(c, s0)

        @pl.when(jnp.logical_not(single))
        def _():
            for c, (s0, n_win) in enumerate(spans):
                def body(j, carry, c=c, s0=s0):
                    visit(c, s0 + j * win)
                    return carry
                lax.fori_loop(0, n_win, body, 0)

    @pl.when((e == 0) & (f == 0))
    def _():
        y_ref[...] = h_ref[...]
        strict_lower = (lax.broadcasted_iota(jnp.int32, (ch, ch), 1)
                        < lax.broadcasted_iota(jnp.int32, (ch, ch), 0)).astype(BF16)
        lane_f = lane.astype(F32)
        offs = jnp.ones((1, GATE_LANES), F32)
        for c in range(n_ch):
            rows = slice(c * ch, (c + 1) * ch)
            g = gate_ref[rows, :]
            sel = jnp.where((lane_f == g[:, N_EXPERTS:N_EXPERTS + 1])
                            | (lane_f == g[:, N_EXPERTS + 1:N_EXPERTS + 2]), 1.0, 0.0)
            r = _dot(strict_lower, sel.astype(BF16)) + offs
            r = jnp.where(sel > 0.0, r, 0.0)
            rank_row[:, rows] = jnp.transpose(r) - 1.0
            high = jnp.floor(r * (1.0 / 256.0))
            rk_ref[rows, :] = jnp.concatenate([high, r - 256.0 * high], axis=1).astype(BF16)
            gs_ref[rows, :] = jnp.concatenate(_split_bf16(g, 3), axis=1)
            offs = offs + jnp.sum(sel, axis=0, keepdims=True)

    @pl.when(f == 0)
    def _():
        pick = jnp.where(lax.broadcasted_iota(jnp.int32, (GATE_LANES, GATE_LANES), 0) == e, 1.0, 0.0)
        pick_rank = jnp.concatenate([256.0 * pick, pick], axis=0).astype(BF16)
        pick_gate = jnp.concatenate([pick] * 3, axis=0).astype(BF16)
        for c in range(n_ch):
            rows = slice(c * ch, (c + 1) * ch)
            rc_ref[rows, :] = _dot(rk_ref[rows, :], pick_rank) - 1.0
            gc_ref[rows, :] = _dot(gs_ref[rows, :], pick_gate)

        def clear(j, carry):
            r0 = pl.multiple_of(j * win, win)
            xs_ref[pl.ds(r0, win), :] = jnp.zeros((win, d), BF16)
            yacc_ref[pl.ds(r0, win), :] = jnp.zeros((win, d), F32)
            return carry

        lax.fori_loop(0, n_windows, clear, 0)

        row_id = lax.broadcasted_iota(jnp.int32, (win, ch), 0)

        def gather_window(c, s):
            s = pl.multiple_of(s, MOE_ALIGN)
            ranks = rank_row[pl.ds(e, 1), c * ch:(c + 1) * ch]
            onehot = jnp.where(ranks == (s + row_id).astype(F32), 1.0, 0.0).astype(BF16)
            xs_ref[pl.ds(s, win), :] += _dot(onehot, hn_ref[c * ch:(c + 1) * ch, :]).astype(BF16)

        for_each_window(gather_window)

    def ffn_rows(r0, rows):
        x = xs_ref[pl.ds(r0, rows), :]
        a = _dot(x, w1_ref[0])
        g3 = _dot(x, w3_ref[0])
        yacc_ref[pl.ds(r0, rows), :] += _dot((a * jax.nn.sigmoid(a) * g3).astype(BF16), w2_ref[0])

    five = n_blocks == 5
    half_tail = before(n_ch) <= 4 * rb + rb // 2

    @pl.when(five & half_tail)
    def _():
        ffn_rows(0, 4 * rb + rb // 2)

    @pl.when(five & jnp.logical_not(half_tail))
    def _():
        ffn_rows(0, 5 * rb)

    @pl.when(jnp.logical_not(five))
    def _():
        n_quads = n_blocks // 4

        def ffn_quad(j, carry):
            ffn_rows(pl.multiple_of(j * (4 * rb), 4 * rb), 4 * rb)
            return carry

        lax.fori_loop(0, n_quads, ffn_quad, 0)
        tail = pl.multiple_of(n_quads * (4 * rb), 4 * rb)

        @pl.when((n_blocks & 2) != 0)
        def _():
            ffn_rows(tail, 2 * rb)

        @pl.when((n_blocks & 1) != 0)
        def _():
            ffn_rows(pl.multiple_of(tail + (n_blocks & 2) * rb, rb), rb)

    @pl.when(f == pl.num_programs(2) - 1)
    def _():
        def to_bf16(j, carry):
            r0 = pl.multiple_of(j * win, win)
            xs_ref[pl.ds(r0, win), :] = yacc_ref[pl.ds(r0, win), :].astype(BF16)
            return carry

        lax.fori_loop(0, n_windows, to_bf16, 0)

        lane_id = lax.broadcasted_iota(jnp.int32, (ch, rb), 1).astype(F32)

        def scatter_window(c, s):
            rows = slice(c * ch, (c + 1) * ch)
            s = pl.multiple_of(s, MOE_ALIGN)
            rank = rc_ref[rows, :] - s.astype(F32)
            onehot = jnp.concatenate([jnp.where(rank == lane_id, 1.0, 0.0),
                                      jnp.where(rank == lane_id + float(rb), 1.0, 0.0)],
                                     axis=1).astype(BF16)
            gate = jnp.concatenate([gc_ref[rows, :]] * (d // GATE_LANES), axis=1)
            y_ref[rows, :] += gate * _dot(onehot, xs_ref[pl.ds(s, win), :])

        for_each_window(scatter_window)

        if final:
            @pl.when(e == pl.num_programs(1) - 1)
            def _():
                y_ref[...] = _rms(y_ref[...], gfin_ref[...])


def moe(h, hn, gates, cnt, w1, w3, w2, final_gain=None, tf=896):
    m, d = hn.shape
    ts = min(MOE_TOKENS, m)
    n_exp, _, nf = w1.shape
    assert nf % tf == 0 and m % ts == 0 and ts % MOE_CHUNK == 0 and MOE_ROWS == GATE_LANES
    n_ch = ts // MOE_CHUNK
    counts = cnt.reshape(m // ts, n_ch, GATE_LANES)[:, :, :n_exp].astype(jnp.int32)
    before = jnp.concatenate([jnp.zeros((m // ts, 1, n_exp), jnp.int32), jnp.cumsum(counts, axis=1)], axis=1)
    per_tile = lambda width: pl.BlockSpec((ts, width), lambda t, e, f, cb: (t, 0), pipeline_mode=pl.Buffered(1))
    in_specs = [per_tile(d), per_tile(d), per_tile(GATE_LANES),
                pl.BlockSpec((1, d, tf), lambda t, e, f, cb: (e, 0, f)),
                pl.BlockSpec((1, d, tf), lambda t, e, f, cb: (e, 0, f)),
                pl.BlockSpec((1, tf, d), lambda t, e, f, cb: (e, f, 0))]
    args = [before.reshape(-1), h, hn, gates, w1, w3, w2]
    if final_gain is not None:
        in_specs.append(pl.BlockSpec((1, d), lambda t, e, f, cb: (0, 0)))
        args.append(final_gain.reshape(1, d))
    grid_spec = pltpu.PrefetchScalarGridSpec(
        num_scalar_prefetch=1,
        grid=(m // ts, n_exp, nf // tf),
        in_specs=in_specs,
        out_specs=per_tile(d),
        scratch_shapes=[pltpu.VMEM((GATE_LANES, ts), F32),
                        pltpu.VMEM((ts, 2 * GATE_LANES), BF16),
                        pltpu.VMEM((ts, 3 * GATE_LANES), BF16),
                        pltpu.VMEM((ts + 2 * MOE_ROWS, d), BF16),
                        pltpu.VMEM((ts + 2 * MOE_ROWS, d), F32),
                        pltpu.VMEM((ts, MOE_ROWS), F32),
                        pltpu.VMEM((ts, GATE_LANES), F32)])
    return pl.pallas_call(
        functools.partial(_moe_kernel, final=final_gain is not None),
        grid_spec=grid_spec,
        out_shape=jax.ShapeDtypeStruct((m, d), F32),
        compiler_params=_cparams(("parallel", "arbitrary", "arbitrary")),
        name="moe",
    )(*args)


def _final_norm_kernel(g_ref, x_ref, o_ref):
    o_ref[...] = _rms(x_ref[...], g_ref[...])


def final_rmsnorm(g, x, tm=1024):
    m, d = x.shape
    tm = min(tm, m)
    return pl.pallas_call(
        _final_norm_kernel,
        grid=(m // tm,),
        in_specs=[pl.BlockSpec((1, d), lambda i: (0, 0)), pl.BlockSpec((tm, d), lambda i: (i, 0))],
        out_specs=pl.BlockSpec((tm, d), lambda i: (i, 0)),
        out_shape=jax.ShapeDtypeStruct((m, d), F32),
        compiler_params=_cparams(("parallel",)),
        name="final_norm",
    )(g.reshape(1, d), x)


def _pad_cols(w, width):
    return jnp.pad(w, ((0, 0), (0, width - w.shape[1])))


def _in_proj_weight(w_in):
    a_end = 2 * GLA_HEADS * GLA_DK + 2 * GROUP_WIDTH + GLA_RANK
    b_end = a_end + GROUP_WIDTH
    c_end = b_end + 3 * GROUP_WIDTH + IDX_HEADS * IDX_DIM + IDX_DIM + IDX_HEADS
    assert w_in.shape[1] == c_end + 2 * GROUP_WIDTH
    return jnp.concatenate([_pad_cols(w_in[:, :a_end], A_WIDTH), w_in[:, a_end:b_end],
                            _pad_cols(w_in[:, b_end:c_end], C_MAIN + C_AUX), w_in[:, c_end:]], axis=1).astype(BF16)


def hybrid_layer(h, mem2, toe, p, bsz, t, mixer_weights, routing=None):
    m, d = h.shape
    z_a, z_b, z_c, z_aux, z_d = norm_matmul(h, p["norm_mix"], _in_proj_weight(p["w_in"]),
                                            (A_WIDTH, B_WIDTH, C_MAIN, C_AUX, D_WIDTH),
                                            (F32, F32, BF16, F32, F32))
    o_a = gla_mixer(z_a.reshape(bsz, t, A_WIDTH), p["gla_wa2"], p["gla_ba"], p["gla_norm"])
    o_b = pool_mixer(z_b.reshape(bsz, t, B_WIDTH), p["pool_w"], p["pool_b"].reshape(-1), p["pool_scale"])
    o_c, *weights = dsa_mixer(z_c.reshape(bsz, t, C_MAIN), z_aux.reshape(bsz, t, C_AUX), toe, mixer_weights)
    o_d = sgu_mixer(z_d.reshape(bsz, t, D_WIDTH), p["sgu_ln_g"], p["sgu_ln_b"], p["sgu_w"], p["sgu_b"])
    hd = XA_HEADS * XA_DH
    k, v = norm_matmul(mem2, p["norm_mem"], p["xa_wkv"].astype(BF16), (hd, hd))
    mlen = mem2.shape[0] // bsz
    outs = mixer_tail(h.reshape(bsz, t, d), (o_a, o_b, o_c, o_d), p["w_out"].astype(BF16), p["norm_xa"],
                      p["xa_wq"].astype(BF16), k.reshape(bsz, mlen, hd), v.reshape(bsz, mlen, hd),
                      p["xa_wo"].astype(BF16), routing)
    if routing is None:
        return weights, outs[0].reshape(m, d)
    h3, hn, gates, counts = outs
    return weights, h3.reshape(m, d), hn.reshape(m, d), gates.reshape(m, GATE_LANES), counts


def kernel(x, mem, rel_bias, final_norm, norm_mix, w_in, gla_wa2, gla_ba, gla_norm, pool_w, pool_b,
           pool_scale, sgu_ln_g, sgu_ln_b, sgu_w, sgu_b, w_out, norm_xa, norm_mem, xa_wq, xa_wkv, xa_wo,
           norm_ffn, ffn_w1, ffn_w3, ffn_w2, router, moe_w1, moe_w3, moe_w2):
    bsz, t, d = x.shape
    depth = norm_mix.shape[0]
    h = x.reshape(bsz * t, d)
    mem2 = mem.reshape(-1, d)
    toe = rel_bias_tables(rel_bias)
    normed = False
    for i in range(depth):
        p = dict(norm_mix=norm_mix[i], w_in=w_in[i], gla_wa2=gla_wa2[i], gla_ba=gla_ba[i],
                 gla_norm=gla_norm[i], pool_w=pool_w[i], pool_b=pool_b[i], pool_scale=pool_scale[i],
                 sgu_ln_g=sgu_ln_g[i], sgu_ln_b=sgu_ln_b[i], sgu_w=sgu_w[i], sgu_b=sgu_b[i],
                 w_out=w_out[i], norm_xa=norm_xa[i], norm_mem=norm_mem[i], xa_wq=xa_wq[i],
                 xa_wkv=xa_wkv[i], xa_wo=xa_wo[i])
        j = i // 2
        if i % 2 == 0:
            (w1, w3, w2), h = hybrid_layer(h, mem2, toe, p, bsz, t, (ffn_w1[j], ffn_w3[j], ffn_w2[j]))
            h = ffn(h, norm_ffn[i], w1, w3, w2)
        else:
            (w1, w3, w2), h, hn, gates, counts = hybrid_layer(h, mem2, toe, p, bsz, t,
                                                              (moe_w1[j], moe_w3[j], moe_w2[j]),
                                                              routing=(norm_ffn[i], router[j]))
            normed = i == depth - 1
            h = moe(h, hn, gates, counts, w1, w3, w2, final_gain=final_norm if normed else None)
    if not normed:
        h = final_rmsnorm(final_norm, h)
    return h.reshape(bsz, t, d)
```

```python
import functools
import math

import jax
import jax.numpy as jnp
import numpy as np
from jax import lax
from jax.experimental import pallas as pl
from jax.experimental.pallas import tpu as pltpu

F32 = jnp.float32
BF16 = jnp.bfloat16
EPS = 1e-6

LANES = 128
BF16_TILE_ROWS = 16
VMEM_LIMIT = 56 * 1024 * 1024

GROUP_WIDTH = 256

GLA_HEADS = 4
GLA_DV = 64
GLA_DK = 32
GLA_RANK = 16
GLA_TAU = 16.0
GLA_CHUNK = 64
GLA_GROUP = 16

POOL_WINDOWS = (2, 4, 8, 16)
POOL_CG = 64

DSA_HEADS = 4
DSA_DH = 64
IDX_HEADS = 8
IDX_DIM = 32
DSA_TOPK_MAX = 256
DSA_BLOCK = 256
DSA_SUB = 64
DSA_HEAD_GROUPS = ((0, 1, 2, 3),)

SGU_GROUPS = 4
SGU_CHUNK = 128
SGU_CG = 64

REL_BUCKETS = 32
REL_MAX_DIST = 128

XA_HEADS = 4
XA_DH = 64

N_EXPERTS = 8

A_WIDTH = 896
B_WIDTH = 256
C_MAIN = 1024
C_AUX = LANES
D_WIDTH = 512

MIXER_DTYPE = BF16
INT_MIN = -(2 ** 31)
NEG_BIG = -1e30


def _cparams(sem):
    return pltpu.CompilerParams(dimension_semantics=sem, vmem_limit_bytes=VMEM_LIMIT)


def _dot(a, b):
    return jnp.dot(a, b, preferred_element_type=F32)


def _dot_nt(a, b):
    return lax.dot_general(a, b, (((1,), (1,)), ((), ())), preferred_element_type=F32)


def _dot_tn(a, b):
    return lax.dot_general(a, b, (((0,), (0,)), ((), ())), preferred_element_type=F32)


def _split_bf16(x, terms):
    parts = []
    for _ in range(terms):
        p = x.astype(BF16)
        parts.append(p)
        x = x - p.astype(F32)
    return parts


def _dot_exact_rhs(a, b, terms):
    out = None
    for p in _split_bf16(a, terms):
        d = _dot(p, b)
        out = d if out is None else out + d
    return out


def _dot_exact_lhs(a, b, terms):
    out = None
    for p in _split_bf16(b, terms):
        d = _dot(a, p)
        out = d if out is None else out + d
    return out


def _dot_3pass(a, b):
    a_hi, a_lo = _split_bf16(a, 2)
    b_hi, b_lo = _split_bf16(b, 2)
    return _dot(a_hi, b_hi) + (_dot(a_hi, b_lo) + _dot(a_lo, b_hi))


def _rms(x, g):
    return x * lax.rsqrt(jnp.mean(x * x, axis=-1, keepdims=True) + EPS) * g


def _norm_matmul_kernel(h_ref, g_ref, w_ref, *out_refs, widths):
    hb = _rms(h_ref[...], g_ref[...]).astype(BF16)
    off = 0
    for o_ref, wd in zip(out_refs, widths):
        o_ref[...] = _dot(hb, w_ref[:, off:off + wd]).astype(o_ref.dtype)
        off += wd


def norm_matmul(h, g, w, widths, dtypes=None, tm=1024):
    dtypes = dtypes or (F32,) * len(widths)
    m, d = h.shape
    tm = min(tm, m)
    n = sum(widths)
    return pl.pallas_call(
        functools.partial(_norm_matmul_kernel, widths=widths),
        grid=(m // tm,),
        in_specs=[pl.BlockSpec((tm, d), lambda i: (i, 0)),
                  pl.BlockSpec((1, d), lambda i: (0, 0)),
                  pl.BlockSpec((d, n), lambda i: (0, 0))],
        out_specs=[pl.BlockSpec((tm, wd), lambda i: (i, 0)) for wd in widths],
        out_shape=[jax.ShapeDtypeStruct((m, wd), dt) for wd, dt in zip(widths, dtypes)],
        compiler_params=_cparams(("parallel",)),
        name="norm_matmul",
    )(h, g.reshape(1, d), w)


def _log_sigmoid(x):
    return jnp.minimum(x, 0.0) - jnp.log1p(jnp.exp(-jnp.abs(x)))


def _gla_kernel(z_ref, wa2_ref, ba_ref, ng_ref, o_ref, s_ref, *, group, n_groups):
    c = GLA_CHUNK
    grp = group * c
    hk = GLA_HEADS * GLA_DK
    hv = GLA_HEADS * GLA_DV
    s_ref[...] = jnp.zeros_like(s_ref)

    head_k = lax.broadcasted_iota(jnp.int32, (1, hk), 1) // GLA_DK
    head_v = lax.broadcasted_iota(jnp.int32, (1, hv), 1) // GLA_DV
    cum = min(grp, 256)
    g_row = lax.broadcasted_iota(jnp.int32, (cum, cum), 0)
    g_col = lax.broadcasted_iota(jnp.int32, (cum, cum), 1)
    tril = (((g_row // c) == (g_col // c)) & (g_col <= g_row)).astype(BF16)
    causal4 = (lax.broadcasted_iota(jnp.int32, (GLA_HEADS * c, c), 1)
               <= lax.broadcasted_iota(jnp.int32, (GLA_HEADS * c, c), 0) % c)
    state_mask = (lax.broadcasted_iota(jnp.int32, (hk, hv), 0) // GLA_DK
                  == lax.broadcasted_iota(jnp.int32, (hk, hv), 1) // GLA_DV)
    norm_mat = jnp.where(lax.broadcasted_iota(jnp.int32, (hv, hv), 0) // GLA_DV
                         == lax.broadcasted_iota(jnp.int32, (hv, hv), 1) // GLA_DV,
                         1.0 / GLA_DV, 0.0).astype(BF16)
    wa2 = wa2_ref[...]
    ba = ba_ref[...]
    ng = ng_ref[...]

    def body(n, carry):
        r0 = pl.multiple_of(n * grp, grp)
        z = z_ref[0, pl.ds(r0, grp), :]
        q, k, v, g, lr = z[:, 0:128], z[:, 128:256], z[:, 256:512], z[:, 512:768], z[:, 768:896]
        log_a = _log_sigmoid(_dot_3pass(lr, wa2) + ba) / GLA_TAU
        b = jnp.concatenate([_dot_exact_lhs(tril, log_a[r:r + cum], 3) for r in range(0, grp, cum)], axis=0)
        b_end = jnp.concatenate([jnp.broadcast_to(b[(ci + 1) * c - 1:(ci + 1) * c, :], (c, hk))
                                 for ci in range(group)], axis=0)
        q_t = q * (GLA_DK ** -0.5) * jnp.exp(b)
        q_tb = q_t.astype(BF16)
        k_t = (k * jnp.exp(-b)).astype(BF16)
        k_dec = (k * jnp.exp(b_end - b)).astype(BF16)
        vb = v.astype(BF16)
        outs = []
        for ci in range(group):
            rows = slice(ci * c, (ci + 1) * c)
            q4 = jnp.concatenate([jnp.where(head_k == h, q_t[rows], 0.0) for h in range(GLA_HEADS)],
                                 axis=0).astype(BF16)
            att = jnp.where(causal4, _dot_nt(q4, k_t[rows]), 0.0)
            r = _dot(att.astype(BF16), vb[rows])
            o = _dot(q_tb[rows], s_ref[...].astype(BF16))
            for h in range(GLA_HEADS):
                o = o + jnp.where(head_v == h, r[h * c:(h + 1) * c, :], 0.0)
            outs.append(o)
            kv = jnp.where(state_mask, _dot_tn(k_dec[rows], vb[rows]), 0.0)
            last = b_end[ci * c:ci * c + 1, :]
            dec = jnp.exp(jnp.transpose(jnp.broadcast_to(last, (hk, hk))))
            s_ref[...] = s_ref[...] * jnp.concatenate([dec, dec], axis=1) + kv
        o = jnp.concatenate(outs, axis=0)
        o = o * lax.rsqrt(_dot_exact_rhs(o * o, norm_mat, 2) + EPS) * ng
        o_ref[0, pl.ds(r0, grp), :] = (o * (g * jax.nn.sigmoid(g))).astype(o_ref.dtype)
        return carry

    lax.fori_loop(0, n_groups, body, 0)


def gla_mixer(z_a, wa2, ba, norm_g):
    bsz, t, _ = z_a.shape
    hk = GLA_HEADS * GLA_DK
    wa2p = jnp.pad(wa2, ((0, LANES - GLA_RANK), (0, 0)))
    n_chunks = t // GLA_CHUNK
    group = math.gcd(GLA_GROUP, n_chunks)
    return pl.pallas_call(
        functools.partial(_gla_kernel, group=group, n_groups=n_chunks // group),
        grid=(bsz,),
        in_specs=[pl.BlockSpec((1, t, A_WIDTH), lambda b: (b, 0, 0)),
                  pl.BlockSpec((LANES, hk), lambda b: (0, 0)),
                  pl.BlockSpec((1, hk), lambda b: (0, 0)),
                  pl.BlockSpec((1, GROUP_WIDTH), lambda b: (0, 0))],
        out_specs=pl.BlockSpec((1, t, GROUP_WIDTH), lambda b: (b, 0, 0)),
        out_shape=jax.ShapeDtypeStruct((bsz, t, GROUP_WIDTH), MIXER_DTYPE),
        scratch_shapes=[pltpu.VMEM((hk, GROUP_WIDTH), F32)],
        compiler_params=_cparams(("parallel",)),
        name="gla_mixer",
    )(z_a, wa2p, ba.reshape(1, hk), norm_g.reshape(1, GROUP_WIDTH))


def _pool_kernel(u_ref, w_ref, b_ref, sc_ref, o_ref):
    u = u_ref[0]
    t, gw = u.shape
    row = lax.broadcasted_iota(jnp.int32, (t, gw), 0)
    grp = lax.broadcasted_iota(jnp.int32, (t, gw), 1) // POOL_CG

    def shifted(x, k):
        return jnp.where(row >= k, pltpu.roll(x, k, axis=0), 0.0)

    s = u
    p = jnp.zeros_like(u)
    for gi, win in enumerate(POOL_WINDOWS):
        half = win // 2
        s = s + shifted(s, half)
        cnt = jnp.minimum(row + 1, win).astype(F32)
        p = jnp.where(grp == gi, s / cnt - u, p)
    y = _dot(p.astype(BF16), w_ref[...]) + b_ref[...]
    o_ref[0] = (y * sc_ref[...]).astype(o_ref.dtype)


def pool_mixer(z_b, w, b, scale):
    assert POOL_WINDOWS == (2, 4, 8, 16)
    bsz, t, gw = z_b.shape
    n_grp = len(POOL_WINDOWS)
    w_bd = (w[:, :, None, :] * jnp.eye(n_grp, dtype=F32)[:, None, :, None]).reshape(gw, gw)
    return pl.pallas_call(
        _pool_kernel,
        grid=(bsz,),
        in_specs=[pl.BlockSpec((1, t, gw), lambda i: (i, 0, 0)),
                  pl.BlockSpec((gw, gw), lambda i: (0, 0)),
                  pl.BlockSpec((1, gw), lambda i: (0, 0)),
                  pl.BlockSpec((1, gw), lambda i: (0, 0))],
        out_specs=pl.BlockSpec((1, t, gw), lambda i: (i, 0, 0)),
        out_shape=jax.ShapeDtypeStruct((bsz, t, gw), MIXER_DTYPE),
        compiler_params=_cparams(("parallel",)),
        name="pool_mixer",
    )(z_b, w_bd.astype(BF16), b.reshape(1, gw), scale.reshape(1, gw))


def _sgu_kernel(z_ref, lg_ref, lb_ref, w_ref, bm_ref, o_ref, *, chunks):
    c = SGU_CHUNK
    gw = GROUP_WIDTH
    rows = SGU_GROUPS * c
    tri = (lax.broadcasted_iota(jnp.int32, (rows, c), 1)
           <= lax.broadcasted_iota(jnp.int32, (rows, c), 0) % c)
    ws = jnp.where(tri, w_ref[...], 0.0).astype(BF16)
    grp = lax.broadcasted_iota(jnp.int32, (1, gw), 1) // SGU_CG
    for ci in range(chunks):
        z = jax.nn.gelu(z_ref[0, ci * c:(ci + 1) * c, :], approximate=True)
        u, v = z[:, :gw], z[:, gw:]
        mu = jnp.mean(v, axis=-1, keepdims=True)
        var = jnp.mean(jnp.square(v - mu), axis=-1, keepdims=True)
        vn = (v - mu) * lax.rsqrt(var + EPS) * lg_ref[...] + lb_ref[...]
        r = _dot(ws, vn.astype(BF16))
        mixed = bm_ref[...]
        for g in range(SGU_GROUPS):
            mixed = mixed + jnp.where(grp == g, r[g * c:(g + 1) * c, :], 0.0)
        o_ref[0, ci * c:(ci + 1) * c, :] = (u * mixed).astype(o_ref.dtype)


def sgu_mixer(z_d, ln_g, ln_b, w_s, b_s, chunks=16):
    bsz, t, _ = z_d.shape
    gw = GROUP_WIDTH
    chunks = math.gcd(chunks, t // SGU_CHUNK)
    tt = chunks * SGU_CHUNK
    bias = jnp.repeat(b_s.T, SGU_CG, axis=1)
    return pl.pallas_call(
        functools.partial(_sgu_kernel, chunks=chunks),
        grid=(bsz, t // tt),
        in_specs=[pl.BlockSpec((1, tt, 2 * gw), lambda b, i: (b, i, 0)),
                  pl.BlockSpec((1, gw), lambda b, i: (0, 0)),
                  pl.BlockSpec((1, gw), lambda b, i: (0, 0)),
                  pl.BlockSpec((SGU_GROUPS * SGU_CHUNK, SGU_CHUNK), lambda b, i: (0, 0)),
                  pl.BlockSpec((SGU_CHUNK, gw), lambda b, i: (0, 0))],
        out_specs=pl.BlockSpec((1, tt, gw), lambda b, i: (b, i, 0)),
        out_shape=jax.ShapeDtypeStruct((bsz, t, gw), MIXER_DTYPE),
        compiler_params=_cparams(("parallel", "parallel")),
        name="sgu_mixer",
    )(z_d, ln_g.reshape(1, gw), ln_b.reshape(1, gw),
      w_s.reshape(SGU_GROUPS * SGU_CHUNK, SGU_CHUNK), bias)


def _bucket_table():
    assert REL_MAX_DIST <= DSA_BLOCK + 1
    s = np.arange(DSA_BLOCK)[:, None]
    t = np.arange(DSA_BLOCK)[None, :]
    dist = np.stack([t - s, DSA_BLOCK + t - s, 2 * DSA_BLOCK + t - s])
    n = np.maximum(dist, 0)
    max_exact = REL_BUCKETS // 2
    nf = np.maximum(n, 1).astype(np.float32)
    large = max_exact + (np.log(nf / np.float32(max_exact)) / np.float32(math.log(REL_MAX_DIST / max_exact))
                         * np.float32(REL_BUCKETS - max_exact)).astype(np.int32)
    return np.where(n < max_exact, n, np.minimum(large, REL_BUCKETS - 1)).astype(np.int32)


def _bias_table_kernel(rb_ref, bucket_ref, o_ref):
    for back in range(3):
        bucket = bucket_ref[back]
        for h in range(DSA_HEADS):
            acc = jnp.zeros(bucket.shape, F32)
            for b in range(REL_BUCKETS):
                acc = jnp.where(bucket == b, rb_ref[b * DSA_HEADS + h], acc)
            o_ref[back, h] = acc


def rel_bias_tables(rel_bias):
    blk = DSA_BLOCK
    return pl.pallas_call(
        _bias_table_kernel,
        in_specs=[pl.BlockSpec(memory_space=pltpu.SMEM),
                  pl.BlockSpec((3, blk, blk), lambda: (0, 0, 0))],
        out_specs=pl.BlockSpec((3, DSA_HEADS, blk, blk), lambda: (0, 0, 0, 0)),
        out_shape=jax.ShapeDtypeStruct((3, DSA_HEADS, blk, blk), F32),
        name="rel_bias_tables",
    )(rel_bias.reshape(-1), jnp.asarray(_bucket_table()))


def _dsa_kernel(q_ref, k_ref, v_ref, qi_ref, kw_ref, qw_ref, toe_ref, *refs, topk, n_blocks, idx_bits, n_cast):
    for src, dst in zip(refs[:n_cast], refs[n_cast + 1:2 * n_cast + 1]):
        dst[...] = src[...].astype(BF16)
    o_ref = refs[n_cast]
    kpl_ref, vt_ref, keys_ref, khi_ref, klo_ref, acc_ref, am_ref, lg_ref, p_ref = refs[2 * n_cast + 1:]
    blk = DSA_BLOCK
    sub = DSA_SUB
    i = pl.program_id(1)
    hd = DSA_HEADS * DSA_DH
    heads_per_half = LANES // IDX_DIM

    @pl.when(i == 0)
    def _():
        lane = lax.broadcasted_iota(jnp.int32, (blk, LANES), 1)

        def build(kb, c):
            r0 = pl.multiple_of(kb * blk, blk)
            ki = jnp.where(lane < IDX_DIM, kw_ref[0, pl.ds(r0, blk), :], 0.0)
            for j in range(heads_per_half):
                kpl_ref[kb, j] = (ki if j == 0 else pltpu.roll(ki, j * IDX_DIM, axis=1)).astype(BF16)
            vt_ref[kb] = jnp.transpose(v_ref[0, pl.ds(r0, blk), :].astype(F32)).astype(BF16)
            return c

        lax.fori_loop(0, n_blocks, build, 0)

    s_loc = lax.broadcasted_iota(jnp.int32, (blk, blk), 0)
    t_loc = lax.broadcasted_iota(jnp.int32, (blk, blk), 1)
    n_vis = i + 1

    qi_t = jnp.transpose(qi_ref[0].astype(F32)).astype(BF16)
    qi_halves = [qi_t[:LANES, :], qi_t[LANES:, :]]
    w_t = jnp.transpose(qw_ref[0])
    w_rows = [w_t[IDX_DIM + h:IDX_DIM + h + 1, :] * (IDX_HEADS ** -0.5) * (IDX_DIM ** -0.5)
              for h in range(IDX_HEADS)]
    s_sub = lax.broadcasted_iota(jnp.int32, (sub, blk), 0)
    t_sub = lax.broadcasted_iota(jnp.int32, (sub, blk), 1)

    def score_body(kb, c):
        for ci in range(blk // sub):
            rows = slice(ci * sub, (ci + 1) * sub)
            sc = jnp.zeros((sub, blk), F32)
            for half in range(2):
                for j in range(heads_per_half):
                    d = _dot(kpl_ref[kb, j, rows, :], qi_halves[half])
                    sc = sc + jnp.maximum(d, 0.0) * w_rows[half * heads_per_half + j]
            sc = jnp.where(sc == 0.0, 0.0, sc)
            bits = pltpu.bitcast(sc, jnp.int32)
            key = jnp.where(bits < 0, bits ^ jnp.int32(0x7FFFFFFF), bits)
            vis = (kb < i) | (s_sub + ci * sub <= t_sub)
            key = jnp.where(vis, key, jnp.int32(INT_MIN))
            keys_ref[kb, rows, :] = key
            khi_ref[kb, rows, :] = lax.shift_right_arithmetic(key, 16).astype(jnp.int16)
            klo_ref[kb, rows, :] = ((key & 0xFFFF) - 2 ** 15).astype(jnp.int16)
        return c

    lax.fori_loop(0, n_vis, score_body, 0)

    def count(pred):
        def body(kb, acc):
            hit = jnp.where(pred(keys_ref[kb], kb), 1.0, 0.0)
            return acc + jnp.sum(hit.reshape(blk // 32, 32, blk), axis=0)
        acc = lax.fori_loop(0, n_vis, body, jnp.zeros((32, blk), F32))
        return jnp.sum(acc, axis=0, keepdims=True)

    def count16(ref, cand):
        cand = cand.astype(jnp.int16)

        def body(kb, acc):
            hit = jnp.where(ref[kb] >= cand, jnp.int16(1), jnp.int16(0))
            for j in range(blk // 32):
                acc = acc + hit[j * 32:(j + 1) * 32]
            return acc
        acc = lax.fori_loop(0, n_vis, body, jnp.zeros((32, blk), jnp.int16))
        return jnp.sum(acc.astype(jnp.int32).astype(F32), axis=0, keepdims=True)

    def search16(ref, offset):
        lowest = jnp.full((1, blk), -(2 ** 15), jnp.int32)
        base = jnp.where(offset + count16(ref, jnp.zeros((1, blk), jnp.int32)) >= kf, 0, lowest)

        def bit_body(it, base):
            cand = base | lax.shift_left(jnp.int32(1), 14 - it)
            return jnp.where(offset + count16(ref, cand) >= kf, cand, base)
        return lax.fori_loop(0, 15, bit_body, base)

    kf = float(topk)
    zero = jnp.zeros((1, blk), jnp.int32)
    thr_hi = search16(khi_ref, 0.0)
    thr_hi16 = thr_hi.astype(jnp.int16)

    def low_body(kb, above):
        hi = khi_ref[kb]
        klo_ref[kb] = jnp.where(hi == thr_hi16, klo_ref[kb], jnp.int16(-(2 ** 15)))
        hit = jnp.where(hi > thr_hi16, jnp.int16(1), jnp.int16(0))
        for j in range(blk // 32):
            above = above + hit[j * 32:(j + 1) * 32]
        return above

    above = lax.fori_loop(0, n_vis, low_body, jnp.zeros((32, blk), jnp.int16))
    n_above = jnp.sum(above.astype(jnp.int32).astype(F32), axis=0, keepdims=True)
    thr_lo = search16(klo_ref, n_above)
    thr = lax.shift_left(thr_hi, 16) | (thr_lo + 2 ** 15)
    thr_sel = jnp.maximum(thr, jnp.int32(INT_MIN + 1))

    n_ge = count(lambda key, kb: key >= thr)
    excess = jnp.where((n_ge > kf) & (thr > jnp.int32(INT_MIN)), 1.0, 0.0)

    def tie_search():
        need = kf - count(lambda key, kb: key > thr)

        def tie_body(it, j):
            cand = j | lax.shift_left(jnp.int32(1), idx_bits - 1 - it)
            below = count(lambda key, kb: (key == thr) & (kb * blk + s_loc < cand))
            return jnp.where(below < need, cand, j)
        return lax.fori_loop(0, idx_bits, tie_body, zero)

    last = lax.cond(jnp.max(excess) > 0.0, tie_search, lambda: jnp.full((1, blk), 2 ** 30, jnp.int32))

    assert DSA_DH ** -0.5 == 0.125
    q_t = jnp.transpose(q_ref[0].astype(F32) * (DSA_DH ** -0.5))
    row_h = lax.broadcasted_iota(jnp.int32, (hd, 1), 0) // DSA_DH
    q_heads = [jnp.where(row_h == h, q_t, 0.0).astype(BF16) for h in range(DSA_HEADS)]
    acc_ref[...] = jnp.zeros_like(acc_ref)
    n_sub = blk // sub

    def att_body(kb, carry):
        ms, ls = carry
        back = jnp.minimum(i - kb, 2)
        for ci in range(n_sub):
            rows = slice(ci * sub, (ci + 1) * sub)
            key = keys_ref[kb, rows, :]
            sel = (key > thr_sel) | ((key == thr_sel) & (kb * blk + ci * sub + s_sub <= last))
            am_ref[rows, :] = jnp.where(sel, 0.0, NEG_BIG)
        new_ms, new_ls = list(ms), list(ls)
        for heads in DSA_HEAD_GROUPS:
            alphas = {}
            for h in heads:
                pm = jnp.full((8, blk), NEG_BIG, F32)
                for ci in range(n_sub):
                    rows = slice(ci * sub, (ci + 1) * sub)
                    k_rows = k_ref[0, pl.ds(pl.multiple_of(kb * blk + ci * sub, sub), sub), :]
                    lg = _dot(k_rows, q_heads[h]) + toe_ref[back, h, rows, :] + am_ref[rows, :]
                    lg_ref[h, rows, :] = lg
                    pm = jnp.maximum(pm, jnp.max(lg.reshape(sub // 8, 8, blk), axis=0))
                new_ms[h] = jnp.maximum(ms[h], jnp.max(pm, axis=0, keepdims=True))
                alphas[h] = jnp.exp(ms[h] - new_ms[h])
            for h in heads:
                ps = jnp.zeros((8, blk), F32)
                for ci in range(n_sub):
                    rows = slice(ci * sub, (ci + 1) * sub)
                    p = jnp.exp(lg_ref[h, rows, :] - new_ms[h])
                    ps = ps + jnp.sum(p.reshape(sub // 8, 8, blk), axis=0)
                    p_ref[h, rows, :] = p.astype(BF16)
                new_ls[h] = ls[h] * alphas[h] + jnp.sum(ps, axis=0, keepdims=True)
            for h in heads:
                hrows = slice(h * DSA_DH, (h + 1) * DSA_DH)
                acc_ref[hrows, :] = acc_ref[hrows, :] * alphas[h] + _dot(vt_ref[kb, hrows, :], p_ref[h])
        return tuple(new_ms), tuple(new_ls)

    init = (tuple(jnp.full((1, blk), 0.01 * NEG_BIG, F32) for _ in range(DSA_HEADS)),
            tuple(jnp.zeros((1, blk), F32) for _ in range(DSA_HEADS)))
    _, ls = lax.fori_loop(0, n_vis, att_body, init)
    for h in range(DSA_HEADS):
        rows = slice(h * DSA_DH, (h + 1) * DSA_DH)
        acc_ref[rows, :] = acc_ref[rows, :] * (1.0 / ls[h])
    o_ref[0] = jnp.transpose(acc_ref[...]).astype(o_ref.dtype)


def _slab_view(w, steps):
    rows = math.prod(w.shape[:-1])
    n = max(k for k in range(1, steps + 1) if rows % (k * BF16_TILE_ROWS) == 0)
    return w.reshape(n, rows // n, w.shape[-1])


def dsa_mixer(z_c, z_aux, toe, to_bf16=()):
    bsz, t, _ = z_c.shape
    blk = DSA_BLOCK
    n_blocks = t // blk
    assert t % blk == 0
    topk = min(DSA_TOPK_MAX, t // 4)
    hd = DSA_HEADS * DSA_DH
    steps = bsz * n_blocks
    slabs = [_slab_view(w, steps) for w in to_bf16]
    slab_specs = [pl.BlockSpec((1,) + s.shape[1:],
                               lambda b, i, n=s.shape[0]: (jnp.minimum(b * n_blocks + i, n - 1), 0, 0))
                  for s in slabs]
    kernel = functools.partial(_dsa_kernel, topk=topk, n_blocks=n_blocks,
                               idx_bits=max(1, (t - 1).bit_length()), n_cast=len(slabs))
    out, *converted = pl.pallas_call(
        kernel,
        grid=(bsz, n_blocks),
        in_specs=[pl.BlockSpec((1, blk, hd), lambda b, i: (b, i, 0)),
                  pl.BlockSpec((1, t, hd), lambda b, i: (b, 0, 1)),
                  pl.BlockSpec((1, t, hd), lambda b, i: (b, 0, 2)),
                  pl.BlockSpec((1, blk, hd), lambda b, i: (b, i, 3)),
                  pl.BlockSpec((1, t, C_AUX), lambda b, i: (b, 0, 0)),
                  pl.BlockSpec((1, blk, C_AUX), lambda b, i: (b, i, 0)),
                  pl.BlockSpec((3, DSA_HEADS, blk, blk), lambda b, i: (0, 0, 0, 0))] + slab_specs,
        out_specs=[pl.BlockSpec((1, blk, hd), lambda b, i: (b, i, 0))] + slab_specs,
        out_shape=[jax.ShapeDtypeStruct((bsz, t, hd), MIXER_DTYPE)]
                  + [jax.ShapeDtypeStruct(s.shape, BF16) for s in slabs],
        scratch_shapes=[pltpu.VMEM((n_blocks, LANES // IDX_DIM, blk, LANES), BF16),
                        pltpu.VMEM((n_blocks, hd, blk), BF16),
                        pltpu.VMEM((n_blocks, blk, blk), jnp.int32),
                        pltpu.VMEM((n_blocks, blk, blk), jnp.int16),
                        pltpu.VMEM((n_blocks, blk, blk), jnp.int16),
                        pltpu.VMEM((hd, blk), F32),
                        pltpu.VMEM((blk, blk), F32),
                        pltpu.VMEM((DSA_HEADS, blk, blk), F32),
                        pltpu.VMEM((DSA_HEADS, blk, blk), BF16)],
        compiler_params=_cparams(("parallel", "arbitrary")),
        name="dsa_mixer",
    )(z_c, z_c, z_c, z_c, z_aux, z_aux, toe, *slabs)
    return (out, *[c.reshape(w.shape) for c, w in zip(converted, to_bf16)])


def _mixer_tail_kernel(h_ref, a_ref, b_ref, c_ref, d_ref, wout_ref, gxa_ref, wq_ref, k_ref, v_ref, wo_ref,
                       *refs, route):
    gw = GROUP_WIDTH
    x = h_ref[0]
    for gi, r in enumerate((a_ref, b_ref, c_ref, d_ref)):
        x = x + _dot(r[0], wout_ref[gi * gw:(gi + 1) * gw, :])

    hd = XA_HEADS * XA_DH
    q = _dot(_rms(x, gxa_ref[...]).astype(BF16), wq_ref[...]).astype(BF16)
    k = k_ref[0]
    v = v_ref[0]
    lane_h = lax.broadcasted_iota(jnp.int32, (1, hd), 1) // XA_DH
    o = jnp.zeros((x.shape[0], hd), F32)
    for h in range(XA_HEADS):
        s = _dot_nt(q, jnp.where(lane_h == h, k, 0.0).astype(BF16)) * (XA_DH ** -0.5)
        p = jnp.exp(s - jnp.max(s, axis=-1, keepdims=True))
        p = p / jnp.sum(p, axis=-1, keepdims=True)
        o = o + _dot(p.astype(BF16), jnp.where(lane_h == h, v, 0.0).astype(BF16))
    x = x + _dot(o.astype(BF16), wo_ref[...])

    if not route:
        o_ref, = refs
        o_ref[0] = x
        return
    gffn_ref, wr_ref, o_ref, hn_ref, gate_ref, cnt_ref = refs
    o_ref[0] = x
    hn = _rms(x, gffn_ref[...])
    hn_ref[0] = hn.astype(BF16)
    tm = hn.shape[0]
    lane = lax.broadcasted_iota(jnp.int32, (tm, GATE_LANES), 1)
    logits = jnp.where(lane < N_EXPERTS, _dot_3pass(hn, wr_ref[...]), -jnp.inf)
    m1 = jnp.max(logits, axis=-1, keepdims=True)
    i1 = jnp.min(jnp.where(logits == m1, lane, GATE_LANES), axis=-1, keepdims=True)
    rest = jnp.where(lane == i1, -jnp.inf, logits)
    m2 = jnp.max(rest, axis=-1, keepdims=True)
    i2 = jnp.min(jnp.where(rest == m2, lane, GATE_LANES), axis=-1, keepdims=True)
    e2 = jnp.exp(m2 - m1)
    g1 = 1.0 / (1.0 + e2)
    gates = jnp.where(lane == i1, g1, 0.0) + jnp.where(lane == i2, e2 * g1, 0.0)
    gates = jnp.where(lane == N_EXPERTS, i1.astype(F32), gates)
    gate_ref[0] = jnp.where(lane == N_EXPERTS + 1, i2.astype(F32), gates)
    sel = jnp.where((lane == i1) | (lane == i2), 1.0, 0.0)
    for c in range(tm // MOE_CHUNK):
        cnt_ref[c] = jnp.sum(sel[c * MOE_CHUNK:(c + 1) * MOE_CHUNK, :], axis=0, keepdims=True)


def mixer_tail(h3, mixers, w_out, g_xa, wq, k, v, wo, routing=None, tm=1024):
    bsz, t, d = h3.shape
    tm = min(tm, t)
    mlen = k.shape[1]
    hd = XA_HEADS * XA_DH
    gw = GROUP_WIDTH
    row = lambda b, i: (b, i, 0)
    fixed2 = lambda b, i: (0, 0)
    in_specs = ([pl.BlockSpec((1, tm, d), row)] + [pl.BlockSpec((1, tm, gw), row)] * 4
                + [pl.BlockSpec((4 * gw, d), fixed2), pl.BlockSpec((1, d), fixed2), pl.BlockSpec((d, hd), fixed2),
                   pl.BlockSpec((1, mlen, hd), lambda b, i: (b, 0, 0)),
                   pl.BlockSpec((1, mlen, hd), lambda b, i: (b, 0, 0)),
                   pl.BlockSpec((hd, d), fixed2)])
    args = [h3, *mixers, w_out, g_xa.reshape(1, d), wq, k, v, wo]
    out_specs = [pl.BlockSpec((1, tm, d), row)]
    out_shape = [jax.ShapeDtypeStruct((bsz, t, d), F32)]
    if routing is not None:
        assert tm % MOE_CHUNK == 0
        g_ffn, wr = routing
        in_specs += [pl.BlockSpec((1, d), fixed2), pl.BlockSpec((d, GATE_LANES), fixed2)]
        args += [g_ffn.reshape(1, d), jnp.pad(wr, ((0, 0), (0, GATE_LANES - N_EXPERTS)))]
        per_step = tm // MOE_CHUNK
        out_specs += [pl.BlockSpec((1, tm, d), row), pl.BlockSpec((1, tm, GATE_LANES), row),
                      pl.BlockSpec((per_step, 1, GATE_LANES), lambda b, i: (b * (t // tm) + i, 0, 0))]
        out_shape += [jax.ShapeDtypeStruct((bsz, t, d), BF16), jax.ShapeDtypeStruct((bsz, t, GATE_LANES), F32),
                      jax.ShapeDtypeStruct((bsz * t // MOE_CHUNK, 1, GATE_LANES), F32)]
    return pl.pallas_call(
        functools.partial(_mixer_tail_kernel, route=routing is not None),
        grid=(bsz, t // tm),
        in_specs=in_specs,
        out_specs=out_specs,
        out_shape=out_shape,
        compiler_params=_cparams(("parallel", "parallel")),
        name="mixer_tail",
    )(*args)


def _ffn_kernel(h_ref, g_ref, w1_ref, w3_ref, w2_ref, o_ref, *, tf):
    x = h_ref[...]
    hn = _rms(x, g_ref[...]).astype(BF16)
    acc = x
    for f0 in range(0, w1_ref.shape[1], tf):
        a = _dot(hn, w1_ref[:, f0:f0 + tf])
        b = _dot(hn, w3_ref[:, f0:f0 + tf])
        acc = acc + _dot((a * jax.nn.sigmoid(a) * b).astype(BF16), w2_ref[f0:f0 + tf, :])
    o_ref[...] = acc


def ffn(h, g, w1, w3, w2, tm=1024, tf=704):
    m, d = h.shape
    tm = min(tm, m)
    nf = w1.shape[1]
    assert nf % tf == 0
    resident = lambda shape: pl.BlockSpec(shape, lambda i: (0, 0), pipeline_mode=pl.Buffered(1))
    return pl.pallas_call(
        functools.partial(_ffn_kernel, tf=tf),
        grid=(m // tm,),
        in_specs=[pl.BlockSpec((tm, d), lambda i: (i, 0)),
                  pl.BlockSpec((1, d), lambda i: (0, 0)),
                  resident((d, nf)), resident((d, nf)), resident((nf, d))],
        out_specs=pl.BlockSpec((tm, d), lambda i: (i, 0)),
        out_shape=jax.ShapeDtypeStruct((m, d), F32),
        compiler_params=_cparams(("parallel",)),
        name="ffn",
    )(h, g.reshape(1, d), w1, w3, w2)


GATE_LANES = LANES
MOE_CHUNK = 256
MOE_ROWS = LANES
MOE_TOKENS = 2048
MOE_ALIGN = BF16_TILE_ROWS


def _moe_kernel(cb_ref, h_ref, hn_ref, gate_ref, w1_ref, w3_ref, w2_ref, *refs, final):
    if final:
        gfin_ref, *refs = refs
    y_ref, rank_row, rk_ref, gs_ref, xs_ref, yacc_ref, rc_ref, gc_ref = refs
    t = pl.program_id(0)
    e = pl.program_id(1)
    f = pl.program_id(2)
    ts, d = hn_ref.shape
    ch, rb = MOE_CHUNK, MOE_ROWS
    win = 2 * rb
    n_ch = ts // ch
    lane = lax.broadcasted_iota(jnp.int32, (1, GATE_LANES), 1)

    def before(c):
        return cb_ref[(t * (n_ch + 1) + c) * N_EXPERTS + e]

    n_blocks = (before(n_ch) + rb - 1) // rb
    n_windows = (n_blocks * rb + win - 1) // win + 1

    def windows(c):
        lo, hi = before(c), before(c + 1)
        s0 = (lo // MOE_ALIGN) * MOE_ALIGN
        return s0, jnp.where(hi > lo, (hi - s0 + win - 1) // win, 0)

    spans = [windows(c) for c in range(n_ch)]
    single = functools.reduce(jnp.logical_and, [n_win <= 1 for _, n_win in spans])

    def for_each_window(visit):
        @pl.when(single)
        def _():
            for c, (s0, _) in enumerate(spans):
                visit(c, s0)

        @pl.when(jnp.logical_not(single))
        def _():
            for c, (s0, n_win) in enumerate(spans):
                def body(j, carry, c=c, s0=s0):
                    visit(c, s0 + j * win)
                    return carry
                lax.fori_loop(0, n_win, body, 0)

    @pl.when((e == 0) & (f == 0))
    def _():
        y_ref[...] = h_ref[...]
        strict_lower = (lax.broadcasted_iota(jnp.int32, (ch, ch), 1)
                        < lax.broadcasted_iota(jnp.int32, (ch, ch), 0)).astype(BF16)
        lane_f = lane.astype(F32)
        offs = jnp.ones((1, GATE_LANES), F32)
        for c in range(n_ch):
            rows = slice(c * ch, (c + 1) * ch)
            g = gate_ref[rows, :]
            sel = jnp.where((lane_f == g[:, N_EXPERTS:N_EXPERTS + 1])
                            | (lane_f == g[:, N_EXPERTS + 1:N_EXPERTS + 2]), 1.0, 0.0)
            r = _dot(strict_lower, sel.astype(BF16)) + offs
            r = jnp.where(sel > 0.0, r, 0.0)
            rank_row[:, rows] = jnp.transpose(r) - 1.0
            high = jnp.floor(r * (1.0 / 256.0))
            rk_ref[rows, :] = jnp.concatenate([high, r - 256.0 * high], axis=1).astype(BF16)
            gs_ref[rows, :] = jnp.concatenate(_split_bf16(g, 3), axis=1)
            offs = offs + jnp.sum(sel, axis=0, keepdims=True)

    @pl.when(f == 0)
    def _():
        pick = jnp.where(lax.broadcasted_iota(jnp.int32, (GATE_LANES, GATE_LANES), 0) == e, 1.0, 0.0)
        pick_rank = jnp.concatenate([256.0 * pick, pick], axis=0).astype(BF16)
        pick_gate = jnp.concatenate([pick] * 3, axis=0).astype(BF16)
        for c in range(n_ch):
            rows = slice(c * ch, (c + 1) * ch)
            rc_ref[rows, :] = _dot(rk_ref[rows, :], pick_rank) - 1.0
            gc_ref[rows, :] = _dot(gs_ref[rows, :], pick_gate)

        def clear(j, carry):
            r0 = pl.multiple_of(j * win, win)
            xs_ref[pl.ds(r0, win), :] = jnp.zeros((win, d), BF16)
            yacc_ref[pl.ds(r0, win), :] = jnp.zeros((win, d), F32)
            return carry

        lax.fori_loop(0, n_windows, clear, 0)

        row_id = lax.broadcasted_iota(jnp.int32, (win, ch), 0)

        def gather_window(c, s):
            s = pl.multiple_of(s, MOE_ALIGN)
            ranks = rank_row[pl.ds(e, 1), c * ch:(c + 1) * ch]
            onehot = jnp.where(ranks == (s + row_id).astype(F32), 1.0, 0.0).astype(BF16)
            xs_ref[pl.ds(s, win), :] += _dot(onehot, hn_ref[c * ch:(c + 1) * ch, :]).astype(BF16)

        for_each_window(gather_window)

    def ffn_rows(r0, rows):
        x = xs_ref[pl.ds(r0, rows), :]
        a = _dot(x, w1_ref[0])
        g3 = _dot(x, w3_ref[0])
        yacc_ref[pl.ds(r0, rows), :] += _dot((a * jax.nn.sigmoid(a) * g3).astype(BF16), w2_ref[0])

    five = n_blocks == 5
    half_tail = before(n_ch) <= 4 * rb + rb // 2

    @pl.when(five & half_tail)
    def _():
        ffn_rows(0, 4 * rb + rb // 2)

    @pl.when(five & jnp.logical_not(half_tail))
    def _():
        ffn_rows(0, 5 * rb)

    @pl.when(jnp.logical_not(five))
    def _():
        n_quads = n_blocks // 4

        def ffn_quad(j, carry):
            ffn_rows(pl.multiple_of(j * (4 * rb), 4 * rb), 4 * rb)
            return carry

        lax.fori_loop(0, n_quads, ffn_quad, 0)
        tail = pl.multiple_of(n_quads * (4 * rb), 4 * rb)

        @pl.when((n_blocks & 2) != 0)
        def _():
            ffn_rows(tail, 2 * rb)

        @pl.when((n_blocks & 1) != 0)
        def _():
            ffn_rows(pl.multiple_of(tail + (n_blocks & 2) * rb, rb), rb)

    @pl.when(f == pl.num_programs(2) - 1)
    def _():
        def to_bf16(j, carry):
            r0 = pl.multiple_of(j * win, win)
            xs_ref[pl.ds(r0, win), :] = yacc_ref[pl.ds(r0, win), :].astype(BF16)
            return carry

        lax.fori_loop(0, n_windows, to_bf16, 0)

        lane_id = lax.broadcasted_iota(jnp.int32, (ch, rb), 1).astype(F32)

        def scatter_window(c, s):
            rows = slice(c * ch, (c + 1) * ch)
            s = pl.multiple_of(s, MOE_ALIGN)
            rank = rc_ref[rows, :] - s.astype(F32)
            onehot = jnp.concatenate([jnp.where(rank == lane_id, 1.0, 0.0),
                                      jnp.where(rank == lane_id + float(rb), 1.0, 0.0)],
                                     axis=1).astype(BF16)
            gate = jnp.concatenate([gc_ref[rows, :]] * (d // GATE_LANES), axis=1)
            y_ref[rows, :] += gate * _dot(onehot, xs_ref[pl.ds(s, win), :])

        for_each_window(scatter_window)

        if final:
            @pl.when(e == pl.num_programs(1) - 1)
            def _():
                y_ref[...] = _rms(y_ref[...], gfin_ref[...])


def moe(h, hn, gates, cnt, w1, w3, w2, final_gain=None, tf=896):
    m, d = hn.shape
    ts = min(MOE_TOKENS, m)
    n_exp, _, nf = w1.shape
    assert nf % tf == 0 and m % ts == 0 and ts % MOE_CHUNK == 0 and MOE_ROWS == GATE_LANES
    n_ch = ts // MOE_CHUNK
    counts = cnt.reshape(m // ts, n_ch, GATE_LANES)[:, :, :n_exp].astype(jnp.int32)
    before = jnp.concatenate([jnp.zeros((m // ts, 1, n_exp), jnp.int32), jnp.cumsum(counts, axis=1)], axis=1)
    per_tile = lambda width: pl.BlockSpec((ts, width), lambda t, e, f, cb: (t, 0), pipeline_mode=pl.Buffered(1))
    in_specs = [per_tile(d), per_tile(d), per_tile(GATE_LANES),
                pl.BlockSpec((1, d, tf), lambda t, e, f, cb: (e, 0, f)),
                pl.BlockSpec((1, d, tf), lambda t, e, f, cb: (e, 0, f)),
                pl.BlockSpec((1, tf, d), lambda t, e, f, cb: (e, f, 0))]
    args = [before.reshape(-1), h, hn, gates, w1, w3, w2]
    if final_gain is not None:
        in_specs.append(pl.BlockSpec((1, d), lambda t, e, f, cb: (0, 0)))
        args.append(final_gain.reshape(1, d))
    grid_spec = pltpu.PrefetchScalarGridSpec(
        num_scalar_prefetch=1,
        grid=(m // ts, n_exp, nf // tf),
        in_specs=in_specs,
        out_specs=per_tile(d),
        scratch_shapes=[pltpu.VMEM((GATE_LANES, ts), F32),
                        pltpu.VMEM((ts, 2 * GATE_LANES), BF16),
                        pltpu.VMEM((ts, 3 * GATE_LANES), BF16),
                        pltpu.VMEM((ts + 2 * MOE_ROWS, d), BF16),
                        pltpu.VMEM((ts + 2 * MOE_ROWS, d), F32),
                        pltpu.VMEM((ts, MOE_ROWS), F32),
                        pltpu.VMEM((ts, GATE_LANES), F32)])
    return pl.pallas_call(
        functools.partial(_moe_kernel, final=final_gain is not None),
        grid_spec=grid_spec,
        out_shape=jax.ShapeDtypeStruct((m, d), F32),
        compiler_params=_cparams(("parallel", "arbitrary", "arbitrary")),
        name="moe",
    )(*args)


def _final_norm_kernel(g_ref, x_ref, o_ref):
    o_ref[...] = _rms(x_ref[...], g_ref[...])


def final_rmsnorm(g, x, tm=1024):
    m, d = x.shape
    tm = min(tm, m)
    return pl.pallas_call(
        _final_norm_kernel,
        grid=(m // tm,),
        in_specs=[pl.BlockSpec((1, d), lambda i: (0, 0)), pl.BlockSpec((tm, d), lambda i: (i, 0))],
        out_specs=pl.BlockSpec((tm, d), lambda i: (i, 0)),
        out_shape=jax.ShapeDtypeStruct((m, d), F32),
        compiler_params=_cparams(("parallel",)),
        name="final_norm",
    )(g.reshape(1, d), x)


def _pad_cols(w, width):
    return jnp.pad(w, ((0, 0), (0, width - w.shape[1])))


def _in_proj_weight(w_in):
    a_end = 2 * GLA_HEADS * GLA_DK + 2 * GROUP_WIDTH + GLA_RANK
    b_end = a_end + GROUP_WIDTH
    c_end = b_end + 3 * GROUP_WIDTH + IDX_HEADS * IDX_DIM + IDX_DIM + IDX_HEADS
    assert w_in.shape[1] == c_end + 2 * GROUP_WIDTH
    return jnp.concatenate([_pad_cols(w_in[:, :a_end], A_WIDTH), w_in[:, a_end:b_end],
                            _pad_cols(w_in[:, b_end:c_end], C_MAIN + C_AUX), w_in[:, c_end:]], axis=1).astype(BF16)


def hybrid_layer(h, mem2, toe, p, bsz, t, mixer_weights, routing=None):
    m, d = h.shape
    z_a, z_b, z_c, z_aux, z_d = norm_matmul(h, p["norm_mix"], _in_proj_weight(p["w_in"]),
                                            (A_WIDTH, B_WIDTH, C_MAIN, C_AUX, D_WIDTH),
                                            (F32, F32, BF16, F32, F32))
    o_a = gla_mixer(z_a.reshape(bsz, t, A_WIDTH), p["gla_wa2"], p["gla_ba"], p["gla_norm"])
    o_b = pool_mixer(z_b.reshape(bsz, t, B_WIDTH), p["pool_w"], p["pool_b"].reshape(-1), p["pool_scale"])
    o_c, *weights = dsa_mixer(z_c.reshape(bsz, t, C_MAIN), z_aux.reshape(bsz, t, C_AUX), toe, mixer_weights)
    o_d = sgu_mixer(z_d.reshape(bsz, t, D_WIDTH), p["sgu_ln_g"], p["sgu_ln_b"], p["sgu_w"], p["sgu_b"])
    hd = XA_HEADS * XA_DH
    k, v = norm_matmul(mem2, p["norm_mem"], p["xa_wkv"].astype(BF16), (hd, hd))
    mlen = mem2.shape[0] // bsz
    outs = mixer_tail(h.reshape(bsz, t, d), (o_a, o_b, o_c, o_d), p["w_out"].astype(BF16), p["norm_xa"],
                      p["xa_wq"].astype(BF16), k.reshape(bsz, mlen, hd), v.reshape(bsz, mlen, hd),
                      p["xa_wo"].astype(BF16), routing)
    if routing is None:
        return weights, outs[0].reshape(m, d)
    h3, hn, gates, counts = outs
    return weights, h3.reshape(m, d), hn.reshape(m, d), gates.reshape(m, GATE_LANES), counts


def kernel(x, mem, rel_bias, final_norm, norm_mix, w_in, gla_wa2, gla_ba, gla_norm, pool_w, pool_b,
           pool_scale, sgu_ln_g, sgu_ln_b, sgu_w, sgu_b, w_out, norm_xa, norm_mem, xa_wq, xa_wkv, xa_wo,
           norm_ffn, ffn_w1, ffn_w3, ffn_w2, router, moe_w1, moe_w3, moe_w2):
    bsz, t, d = x.shape
    depth = norm_mix.shape[0]
    h = x.reshape(bsz * t, d)
    mem2 = mem.reshape(-1, d)
    toe = rel_bias_tables(rel_bias)
    normed = False
    for i in range(depth):
        p = dict(norm_mix=norm_mix[i], w_in=w_in[i], gla_wa2=gla_wa2[i], gla_ba=gla_ba[i],
                 gla_norm=gla_norm[i], pool_w=pool_w[i], pool_b=pool_b[i], pool_scale=pool_scale[i],
                 sgu_ln_g=sgu_ln_g[i], sgu_ln_b=sgu_ln_b[i], sgu_w=sgu_w[i], sgu_b=sgu_b[i],
                 w_out=w_out[i], norm_xa=norm_xa[i], norm_mem=norm_mem[i], xa_wq=xa_wq[i],
                 xa_wkv=xa_wkv[i], xa_wo=xa_wo[i])
        j = i // 2
        if i % 2 == 0:
            (w1, w3, w2), h = hybrid_layer(h, mem2, toe, p, bsz, t, (ffn_w1[j], ffn_w3[j], ffn_w2[j]))
            h = ffn(h, norm_ffn[i], w1, w3, w2)
        else:
            (w1, w3, w2), h, hn, gates, counts = hybrid_layer(h, mem2, toe, p, bsz, t,
                                                              (moe_w1[j], moe_w3[j], moe_w2[j]),
                                                              routing=(norm_ffn[i], router[j]))
            normed = i == depth - 1
            h = moe(h, hn, gates, counts, w1, w3, w2, final_gain=final_norm if normed else None)
    if not normed:
        h = final_rmsnorm(final_norm, h)
    return h.reshape(bsz, t, d)
```

```python
import functools
import math

import jax
import jax.numpy as jnp
import numpy as np
from jax import lax
from jax.experimental import pallas as pl
from jax.experimental.pallas import tpu as pltpu

F32 = jnp.float32
BF16 = jnp.bfloat16
EPS = 1e-6

LANES = 128
BF16_TILE_ROWS = 16
VMEM_LIMIT = 56 * 1024 * 1024

GROUP_WIDTH = 256

GLA_HEADS = 4
GLA_DV = 64
GLA_DK = 32
GLA_RANK = 16
GLA_TAU = 16.0
GLA_CHUNK = 64
GLA_GROUP = 16

POOL_WINDOWS = (2, 4, 8, 16)
POOL_CG = 64

DSA_HEADS = 4
DSA_DH = 64
IDX_HEADS = 8
IDX_DIM = 32
DSA_TOPK_MAX = 256
DSA_BLOCK = 256
DSA_SUB = 64

SGU_GROUPS = 4
SGU_CHUNK = 128
SGU_CG = 64

REL_BUCKETS = 32
REL_MAX_DIST = 128

XA_HEADS = 4
XA_DH = 64

N_EXPERTS = 8

A_WIDTH = 896
B_WIDTH = 256
C_MAIN = 1024
C_AUX = LANES
D_WIDTH = 512

MIXER_DTYPE = BF16
INT_MIN = -(2 ** 31)
NEG_BIG = -1e30


def _cparams(sem):
    return pltpu.CompilerParams(dimension_semantics=sem, vmem_limit_bytes=VMEM_LIMIT)


def _dot(a, b):
    return jnp.dot(a, b, preferred_element_type=F32)


def _dot_nt(a, b):
    return lax.dot_general(a, b, (((1,), (1,)), ((), ())), preferred_element_type=F32)


def _dot_tn(a, b):
    return lax.dot_general(a, b, (((0,), (0,)), ((), ())), preferred_element_type=F32)


def _split_bf16(x, terms):
    parts = []
    for _ in range(terms):
        p = x.astype(BF16)
        parts.append(p)
        x = x - p.astype(F32)
    return parts


def _dot_exact_rhs(a, b, terms):
    out = None
    for p in _split_bf16(a, terms):
        d = _dot(p, b)
        out = d if out is None else out + d
    return out


def _dot_exact_lhs(a, b, terms):
    out = None
    for p in _split_bf16(b, terms):
        d = _dot(a, p)
        out = d if out is None else out + d
    return out


def _dot_3pass(a, b):
    a_hi, a_lo = _split_bf16(a, 2)
    b_hi, b_lo = _split_bf16(b, 2)
    return _dot(a_hi, b_hi) + (_dot(a_hi, b_lo) + _dot(a_lo, b_hi))


def _rms(x, g):
    return x * lax.rsqrt(jnp.mean(x * x, axis=-1, keepdims=True) + EPS) * g


def _norm_matmul_kernel(h_ref, g_ref, w_ref, *out_refs, widths):
    hb = _rms(h_ref[...], g_ref[...]).astype(BF16)
    off = 0
    for o_ref, wd in zip(out_refs, widths):
        o_ref[...] = _dot(hb, w_ref[:, off:off + wd]).astype(o_ref.dtype)
        off += wd


def norm_matmul(h, g, w, widths, dtypes=None, tm=1024):
    dtypes = dtypes or (F32,) * len(widths)
    m, d = h.shape
    tm = min(tm, m)
    n = sum(widths)
    return pl.pallas_call(
        functools.partial(_norm_matmul_kernel, widths=widths),
        grid=(m // tm,),
        in_specs=[pl.BlockSpec((tm, d), lambda i: (i, 0)),
                  pl.BlockSpec((1, d), lambda i: (0, 0)),
                  pl.BlockSpec((d, n), lambda i: (0, 0))],
        out_specs=[pl.BlockSpec((tm, wd), lambda i: (i, 0)) for wd in widths],
        out_shape=[jax.ShapeDtypeStruct((m, wd), dt) for wd, dt in zip(widths, dtypes)],
        compiler_params=_cparams(("parallel",)),
        name="norm_matmul",
    )(h, g.reshape(1, d), w)


def _log_sigmoid(x):
    return jnp.minimum(x, 0.0) - jnp.log1p(jnp.exp(-jnp.abs(x)))


def _gla_kernel(z_ref, wa2_ref, ba_ref, ng_ref, o_ref, s_ref, *, group, n_groups):
    c = GLA_CHUNK
    grp = group * c
    hk = GLA_HEADS * GLA_DK
    hv = GLA_HEADS * GLA_DV
    s_ref[...] = jnp.zeros_like(s_ref)

    head_k = lax.broadcasted_iota(jnp.int32, (1, hk), 1) // GLA_DK
    head_v = lax.broadcasted_iota(jnp.int32, (1, hv), 1) // GLA_DV
    cum = min(grp, 256)
    g_row = lax.broadcasted_iota(jnp.int32, (cum, cum), 0)
    g_col = lax.broadcasted_iota(jnp.int32, (cum, cum), 1)
    tril = (((g_row // c) == (g_col // c)) & (g_col <= g_row)).astype(BF16)
    causal4 = (lax.broadcasted_iota(jnp.int32, (GLA_HEADS * c, c), 1)
               <= lax.broadcasted_iota(jnp.int32, (GLA_HEADS * c, c), 0) % c)
    state_mask = (lax.broadcasted_iota(jnp.int32, (hk, hv), 0) // GLA_DK
                  == lax.broadcasted_iota(jnp.int32, (hk, hv), 1) // GLA_DV)
    norm_mat = jnp.where(lax.broadcasted_iota(jnp.int32, (hv, hv), 0) // GLA_DV
                         == lax.broadcasted_iota(jnp.int32, (hv, hv), 1) // GLA_DV,
                         1.0 / GLA_DV, 0.0).astype(BF16)
    wa2 = wa2_ref[...]
    ba = ba_ref[...]
    ng = ng_ref[...]

    def body(n, carry):
        r0 = pl.multiple_of(n * grp, grp)
        z = z_ref[0, pl.ds(r0, grp), :]
        q, k, v, g, lr = z[:, 0:128], z[:, 128:256], z[:, 256:512], z[:, 512:768], z[:, 768:896]
        log_a = _log_sigmoid(_dot_3pass(lr, wa2) + ba) / GLA_TAU
        b = jnp.concatenate([_dot_exact_lhs(tril, log_a[r:r + cum], 3) for r in range(0, grp, cum)], axis=0)
        b_end = jnp.concatenate([jnp.broadcast_to(b[(ci + 1) * c - 1:(ci + 1) * c, :], (c, hk))
                                 for ci in range(group)], axis=0)
        q_t = q * (GLA_DK ** -0.5) * jnp.exp(b)
        q_tb = q_t.astype(BF16)
        k_t = (k * jnp.exp(-b)).astype(BF16)
        k_dec = (k * jnp.exp(b_end - b)).astype(BF16)
        vb = v.astype(BF16)
        outs = []
        for ci in range(group):
            rows = slice(ci * c, (ci + 1) * c)
            q4 = jnp.concatenate([jnp.where(head_k == h, q_t[rows], 0.0) for h in range(GLA_HEADS)],
                                 axis=0).astype(BF16)
            att = jnp.where(causal4, _dot_nt(q4, k_t[rows]), 0.0)
            r = _dot(att.astype(BF16), vb[rows])
            o = _dot(q_tb[rows], s_ref[...].astype(BF16))
            for h in range(GLA_HEADS):
                o = o + jnp.where(head_v == h, r[h * c:(h + 1) * c, :], 0.0)
            outs.append(o)
            kv = jnp.where(state_mask, _dot_tn(k_dec[rows], vb[rows]), 0.0)
            last = b_end[ci * c:ci * c + 1, :]
            dec = jnp.exp(jnp.transpose(jnp.broadcast_to(last, (hk, hk))))
            s_ref[...] = s_ref[...] * jnp.concatenate([dec, dec], axis=1) + kv
        o = jnp.concatenate(outs, axis=0)
        o = o * lax.rsqrt(_dot_exact_rhs(o * o, norm_mat, 2) + EPS) * ng
        o_ref[0, pl.ds(r0, grp), :] = (o * (g * jax.nn.sigmoid(g))).astype(o_ref.dtype)
        return carry

    lax.fori_loop(0, n_groups, body, 0)


def gla_mixer(z_a, wa2, ba, norm_g):
    bsz, t, _ = z_a.shape
    hk = GLA_HEADS * GLA_DK
    wa2p = jnp.zeros((LANES, hk), F32).at[:GLA_RANK].set(wa2)
    n_chunks = t // GLA_CHUNK
    group = math.gcd(GLA_GROUP, n_chunks)
    return pl.pallas_call(
        functools.partial(_gla_kernel, group=group, n_groups=n_chunks // group),
        grid=(bsz,),
        in_specs=[pl.BlockSpec((1, t, A_WIDTH), lambda b: (b, 0, 0)),
                  pl.BlockSpec((LANES, hk), lambda b: (0, 0)),
                  pl.BlockSpec((1, hk), lambda b: (0, 0)),
                  pl.BlockSpec((1, GROUP_WIDTH), lambda b: (0, 0))],
        out_specs=pl.BlockSpec((1, t, GROUP_WIDTH), lambda b: (b, 0, 0)),
        out_shape=jax.ShapeDtypeStruct((bsz, t, GROUP_WIDTH), MIXER_DTYPE),
        scratch_shapes=[pltpu.VMEM((hk, GROUP_WIDTH), F32)],
        compiler_params=_cparams(("parallel",)),
        name="gla_mixer",
    )(z_a, wa2p, ba.reshape(1, hk), norm_g.reshape(1, GROUP_WIDTH))


def _pool_kernel(u_ref, w_ref, b_ref, sc_ref, o_ref):
    u = u_ref[0]
    t, gw = u.shape
    row = lax.broadcasted_iota(jnp.int32, (t, gw), 0)
    grp = lax.broadcasted_iota(jnp.int32, (t, gw), 1) // POOL_CG

    def shifted(x, k):
        return jnp.where(row >= k, pltpu.roll(x, k, axis=0), 0.0)

    s = u
    p = jnp.zeros_like(u)
    for gi, win in enumerate(POOL_WINDOWS):
        half = win // 2
        s = s + shifted(s, half)
        cnt = jnp.minimum(row + 1, win).astype(F32)
        p = jnp.where(grp == gi, s / cnt - u, p)
    y = _dot(p.astype(BF16), w_ref[...]) + b_ref[...]
    o_ref[0] = (y * sc_ref[...]).astype(o_ref.dtype)


def pool_mixer(z_b, w, b, scale):
    assert POOL_WINDOWS == (2, 4, 8, 16)
    bsz, t, gw = z_b.shape
    w_bd = jnp.zeros((gw, gw), F32)
    for gi in range(len(POOL_WINDOWS)):
        w_bd = w_bd.at[gi * POOL_CG:(gi + 1) * POOL_CG, gi * POOL_CG:(gi + 1) * POOL_CG].set(w[gi])
    return pl.pallas_call(
        _pool_kernel,
        grid=(bsz,),
        in_specs=[pl.BlockSpec((1, t, gw), lambda i: (i, 0, 0)),
                  pl.BlockSpec((gw, gw), lambda i: (0, 0)),
                  pl.BlockSpec((1, gw), lambda i: (0, 0)),
                  pl.BlockSpec((1, gw), lambda i: (0, 0))],
        out_specs=pl.BlockSpec((1, t, gw), lambda i: (i, 0, 0)),
        out_shape=jax.ShapeDtypeStruct((bsz, t, gw), MIXER_DTYPE),
        compiler_params=_cparams(("parallel",)),
        name="pool_mixer",
    )(z_b, w_bd.astype(BF16), b.reshape(1, gw), scale.reshape(1, gw))


def _sgu_kernel(z_ref, lg_ref, lb_ref, w_ref, bm_ref, o_ref, *, chunks):
    c = SGU_CHUNK
    gw = GROUP_WIDTH
    rows = SGU_GROUPS * c
    tri = (lax.broadcasted_iota(jnp.int32, (rows, c), 1)
           <= lax.broadcasted_iota(jnp.int32, (rows, c), 0) % c)
    ws = jnp.where(tri, w_ref[...], 0.0).astype(BF16)
    grp = lax.broadcasted_iota(jnp.int32, (1, gw), 1) // SGU_CG
    for ci in range(chunks):
        z = jax.nn.gelu(z_ref[0, ci * c:(ci + 1) * c, :], approximate=True)
        u, v = z[:, :gw], z[:, gw:]
        mu = jnp.mean(v, axis=-1, keepdims=True)
        var = jnp.mean(jnp.square(v - mu), axis=-1, keepdims=True)
        vn = (v - mu) * lax.rsqrt(var + EPS) * lg_ref[...] + lb_ref[...]
        r = _dot(ws, vn.astype(BF16))
        mixed = bm_ref[...]
        for g in range(SGU_GROUPS):
            mixed = mixed + jnp.where(grp == g, r[g * c:(g + 1) * c, :], 0.0)
        o_ref[0, ci * c:(ci + 1) * c, :] = (u * mixed).astype(o_ref.dtype)


def sgu_mixer(z_d, ln_g, ln_b, w_s, b_s, chunks=16):
    bsz, t, _ = z_d.shape
    gw = GROUP_WIDTH
    chunks = math.gcd(chunks, t // SGU_CHUNK)
    tt = chunks * SGU_CHUNK
    bias = jnp.repeat(b_s.T, SGU_CG, axis=1)
    return pl.pallas_call(
        functools.partial(_sgu_kernel, chunks=chunks),
        grid=(bsz, t // tt),
        in_specs=[pl.BlockSpec((1, tt, 2 * gw), lambda b, i: (b, i, 0)),
                  pl.BlockSpec((1, gw), lambda b, i: (0, 0)),
                  pl.BlockSpec((1, gw), lambda b, i: (0, 0)),
                  pl.BlockSpec((SGU_GROUPS * SGU_CHUNK, SGU_CHUNK), lambda b, i: (0, 0)),
                  pl.BlockSpec((SGU_CHUNK, gw), lambda b, i: (0, 0))],
        out_specs=pl.BlockSpec((1, tt, gw), lambda b, i: (b, i, 0)),
        out_shape=jax.ShapeDtypeStruct((bsz, t, gw), MIXER_DTYPE),
        compiler_params=_cparams(("parallel", "parallel")),
        name="sgu_mixer",
    )(z_d, ln_g.reshape(1, gw), ln_b.reshape(1, gw),
      w_s.reshape(SGU_GROUPS * SGU_CHUNK, SGU_CHUNK), bias)


def _bucket_table():
    assert REL_MAX_DIST <= DSA_BLOCK + 1
    s = np.arange(DSA_BLOCK)[:, None]
    t = np.arange(DSA_BLOCK)[None, :]
    dist = np.stack([t - s, DSA_BLOCK + t - s, 2 * DSA_BLOCK + t - s])
    n = np.maximum(dist, 0)
    max_exact = REL_BUCKETS // 2
    nf = np.maximum(n, 1).astype(np.float32)
    large = max_exact + (np.log(nf / np.float32(max_exact)) / np.float32(math.log(REL_MAX_DIST / max_exact))
                         * np.float32(REL_BUCKETS - max_exact)).astype(np.int32)
    return np.where(n < max_exact, n, np.minimum(large, REL_BUCKETS - 1)).astype(np.int32)


def _bias_table_kernel(rb_ref, bucket_ref, o_ref):
    for back in range(3):
        bucket = bucket_ref[back]
        for h in range(DSA_HEADS):
            acc = jnp.zeros(bucket.shape, F32)
            for b in range(REL_BUCKETS):
                acc = jnp.where(bucket == b, rb_ref[b * DSA_HEADS + h], acc)
            o_ref[back, h] = acc


def rel_bias_tables(rel_bias):
    blk = DSA_BLOCK
    return pl.pallas_call(
        _bias_table_kernel,
        in_specs=[pl.BlockSpec(memory_space=pltpu.SMEM),
                  pl.BlockSpec((3, blk, blk), lambda: (0, 0, 0))],
        out_specs=pl.BlockSpec((3, DSA_HEADS, blk, blk), lambda: (0, 0, 0, 0)),
        out_shape=jax.ShapeDtypeStruct((3, DSA_HEADS, blk, blk), F32),
        name="rel_bias_tables",
    )(rel_bias.reshape(-1), jnp.asarray(_bucket_table()))


def _dsa_kernel(q_ref, k_ref, v_ref, qi_ref, kw_ref, qw_ref, toe_ref, *refs, topk, n_blocks, idx_bits, n_cast):
    for src, dst in zip(refs[:n_cast], refs[n_cast + 1:2 * n_cast + 1]):
        dst[...] = src[...].astype(BF16)
    o_ref = refs[n_cast]
    kpl_ref, vt_ref, keys_ref, khi_ref, klo_ref, acc_ref, am_ref, lg_ref, p_ref = refs[2 * n_cast + 1:]
    blk = DSA_BLOCK
    sub = DSA_SUB
    i = pl.program_id(1)
    hd = DSA_HEADS * DSA_DH
    heads_per_half = LANES // IDX_DIM

    @pl.when(i == 0)
    def _():
        lane = lax.broadcasted_iota(jnp.int32, (blk, LANES), 1)

        def build(kb, c):
            r0 = pl.multiple_of(kb * blk, blk)
            ki = jnp.where(lane < IDX_DIM, kw_ref[0, pl.ds(r0, blk), :], 0.0)
            for j in range(heads_per_half):
                kpl_ref[kb, j] = (ki if j == 0 else pltpu.roll(ki, j * IDX_DIM, axis=1)).astype(BF16)
            vt_ref[kb] = jnp.transpose(v_ref[0, pl.ds(r0, blk), :].astype(F32)).astype(BF16)
            return c

        lax.fori_loop(0, n_blocks, build, 0)

    s_loc = lax.broadcasted_iota(jnp.int32, (blk, blk), 0)
    t_loc = lax.broadcasted_iota(jnp.int32, (blk, blk), 1)
    n_vis = i + 1

    qi_t = jnp.transpose(qi_ref[0].astype(F32)).astype(BF16)
    qi_halves = [qi_t[:LANES, :], qi_t[LANES:, :]]
    w_t = jnp.transpose(qw_ref[0])
    w_rows = [w_t[IDX_DIM + h:IDX_DIM + h + 1, :] * (IDX_HEADS ** -0.5) * (IDX_DIM ** -0.5)
              for h in range(IDX_HEADS)]
    s_sub = lax.broadcasted_iota(jnp.int32, (sub, blk), 0)
    t_sub = lax.broadcasted_iota(jnp.int32, (sub, blk), 1)

    def score_body(kb, c):
        for ci in range(blk // sub):
            rows = slice(ci * sub, (ci + 1) * sub)
            sc = jnp.zeros((sub, blk), F32)
            for half in range(2):
                for j in range(heads_per_half):
                    d = _dot(kpl_ref[kb, j, rows, :], qi_halves[half])
                    sc = sc + jnp.maximum(d, 0.0) * w_rows[half * heads_per_half + j]
            sc = jnp.where(sc == 0.0, 0.0, sc)
            bits = pltpu.bitcast(sc, jnp.int32)
            key = jnp.where(bits < 0, bits ^ jnp.int32(0x7FFFFFFF), bits)
            vis = (kb < i) | (s_sub + ci * sub <= t_sub)
            key = jnp.where(vis, key, jnp.int32(INT_MIN))
            keys_ref[kb, rows, :] = key
            khi_ref[kb, rows, :] = lax.shift_right_arithmetic(key, 16).astype(jnp.int16)
            klo_ref[kb, rows, :] = ((key & 0xFFFF) - 2 ** 15).astype(jnp.int16)
        return c

    lax.fori_loop(0, n_vis, score_body, 0)

    def count(pred):
        def body(kb, acc):
            hit = jnp.where(pred(keys_ref[kb], kb), 1.0, 0.0)
            return acc + jnp.sum(hit.reshape(blk // 32, 32, blk), axis=0)
        acc = lax.fori_loop(0, n_vis, body, jnp.zeros((32, blk), F32))
        return jnp.sum(acc, axis=0, keepdims=True)

    def count16(ref, cand):
        cand = cand.astype(jnp.int16)

        def body(kb, acc):
            hit = jnp.where(ref[kb] >= cand, jnp.int16(1), jnp.int16(0))
            for j in range(blk // 32):
                acc = acc + hit[j * 32:(j + 1) * 32]
            return acc
        acc = lax.fori_loop(0, n_vis, body, jnp.zeros((32, blk), jnp.int16))
        return jnp.sum(acc.astype(jnp.int32).astype(F32), axis=0, keepdims=True)

    def search16(ref, offset):
        lowest = jnp.full((1, blk), -(2 ** 15), jnp.int32)
        base = jnp.where(offset + count16(ref, jnp.zeros((1, blk), jnp.int32)) >= kf, 0, lowest)

        def bit_body(it, base):
            cand = base | lax.shift_left(jnp.int32(1), 14 - it)
            return jnp.where(offset + count16(ref, cand) >= kf, cand, base)
        return lax.fori_loop(0, 15, bit_body, base)

    kf = float(topk)
    zero = jnp.zeros((1, blk), jnp.int32)
    thr_hi = search16(khi_ref, 0.0)
    thr_hi16 = thr_hi.astype(jnp.int16)

    def low_body(kb, above):
        hi = khi_ref[kb]
        klo_ref[kb] = jnp.where(hi == thr_hi16, klo_ref[kb], jnp.int16(-(2 ** 15)))
        hit = jnp.where(hi > thr_hi16, jnp.int16(1), jnp.int16(0))
        for j in range(blk // 32):
            above = above + hit[j * 32:(j + 1) * 32]
        return above

    above = lax.fori_loop(0, n_vis, low_body, jnp.zeros((32, blk), jnp.int16))
    n_above = jnp.sum(above.astype(jnp.int32).astype(F32), axis=0, keepdims=True)
    thr_lo = search16(klo_ref, n_above)
    thr = lax.shift_left(thr_hi, 16) | (thr_lo + 2 ** 15)
    thr_sel = jnp.maximum(thr, jnp.int32(INT_MIN + 1))

    n_ge = count(lambda key, kb: key >= thr)
    excess = jnp.where((n_ge > kf) & (thr > jnp.int32(INT_MIN)), 1.0, 0.0)

    def tie_search():
        need = kf - count(lambda key, kb: key > thr)

        def tie_body(it, j):
            cand = j | lax.shift_left(jnp.int32(1), idx_bits - 1 - it)
            below = count(lambda key, kb: (key == thr) & (kb * blk + s_loc < cand))
            return jnp.where(below < need, cand, j)
        return lax.fori_loop(0, idx_bits, tie_body, zero)

    last = lax.cond(jnp.max(excess) > 0.0, tie_search, lambda: jnp.full((1, blk), 2 ** 30, jnp.int32))

    assert DSA_DH ** -0.5 == 0.125
    q_t = jnp.transpose(q_ref[0].astype(F32) * (DSA_DH ** -0.5))
    row_h = lax.broadcasted_iota(jnp.int32, (hd, 1), 0) // DSA_DH
    q_heads = [jnp.where(row_h == h, q_t, 0.0).astype(BF16) for h in range(DSA_HEADS)]
    acc_ref[...] = jnp.zeros_like(acc_ref)
    n_sub = blk // sub

    def logit_body(kb, ms):
        back = jnp.minimum(i - kb, 2)
        for ci in range(n_sub):
            rows = slice(ci * sub, (ci + 1) * sub)
            key = keys_ref[kb, rows, :]
            sel = (key > thr_sel) | ((key == thr_sel) & (kb * blk + ci * sub + s_sub <= last))
            am_ref[rows, :] = jnp.where(sel, 0.0, NEG_BIG)
        new_ms = []
        for h in range(DSA_HEADS):
            pm = jnp.full((8, blk), NEG_BIG, F32)
            for ci in range(n_sub):
                rows = slice(ci * sub, (ci + 1) * sub)
                k_rows = k_ref[0, pl.ds(pl.multiple_of(kb * blk + ci * sub, sub), sub), :]
                lg = _dot(k_rows, q_heads[h]) + toe_ref[back, h, rows, :] + am_ref[rows, :]
                lg_ref[h, kb, rows, :] = lg
                pm = jnp.maximum(pm, jnp.max(lg.reshape(sub // 8, 8, blk), axis=0))
            new_ms.append(jnp.maximum(ms[h], jnp.max(pm, axis=0, keepdims=True)))
        return tuple(new_ms)

    ms = lax.fori_loop(0, n_vis, logit_body,
                       tuple(jnp.full((1, blk), NEG_BIG, F32) for _ in range(DSA_HEADS)))

    def pv_body(kb, ls):
        new_ls = []
        for h in range(DSA_HEADS):
            ps = jnp.zeros((8, blk), F32)
            for ci in range(n_sub):
                rows = slice(ci * sub, (ci + 1) * sub)
                p = jnp.exp(lg_ref[h, kb, rows, :] - ms[h])
                ps = ps + jnp.sum(p.reshape(sub // 8, 8, blk), axis=0)
                p_ref[h, rows, :] = p.astype(BF16)
            new_ls.append(ls[h] + jnp.sum(ps, axis=0, keepdims=True))
        for h in range(DSA_HEADS):
            hrows = slice(h * DSA_DH, (h + 1) * DSA_DH)
            acc_ref[hrows, :] += _dot(vt_ref[kb, hrows, :], p_ref[h])
        return tuple(new_ls)

    ls = lax.fori_loop(0, n_vis, pv_body, tuple(jnp.zeros((1, blk), F32) for _ in range(DSA_HEADS)))
    for h in range(DSA_HEADS):
        rows = slice(h * DSA_DH, (h + 1) * DSA_DH)
        acc_ref[rows, :] = acc_ref[rows, :] * (1.0 / ls[h])
    o_ref[0] = jnp.transpose(acc_ref[...]).astype(o_ref.dtype)


def _slab_view(w, steps):
    rows = math.prod(w.shape[:-1])
    n = max(k for k in range(1, steps + 1) if rows % (k * BF16_TILE_ROWS) == 0)
    return w.reshape(n, rows // n, w.shape[-1])


def dsa_mixer(z_c, z_aux, toe, to_bf16=()):
    bsz, t, _ = z_c.shape
    blk = DSA_BLOCK
    n_blocks = t // blk
    assert t % blk == 0
    topk = min(DSA_TOPK_MAX, t // 4)
    hd = DSA_HEADS * DSA_DH
    steps = bsz * n_blocks
    slabs = [_slab_view(w, steps) for w in to_bf16]
    slab_specs = [pl.BlockSpec((1,) + s.shape[1:],
                               lambda b, i, n=s.shape[0]: (jnp.minimum(b * n_blocks + i, n - 1), 0, 0))
                  for s in slabs]
    kernel = functools.partial(_dsa_kernel, topk=topk, n_blocks=n_blocks,
                               idx_bits=max(1, (t - 1).bit_length()), n_cast=len(slabs))
    out, *converted = pl.pallas_call(
        kernel,
        grid=(bsz, n_blocks),
        in_specs=[pl.BlockSpec((1, blk, hd), lambda b, i: (b, i, 0)),
                  pl.BlockSpec((1, t, hd), lambda b, i: (b, 0, 1)),
                  pl.BlockSpec((1, t, hd), lambda b, i: (b, 0, 2)),
                  pl.BlockSpec((1, blk, hd), lambda b, i: (b, i, 3)),
                  pl.BlockSpec((1, t, C_AUX), lambda b, i: (b, 0, 0)),
                  pl.BlockSpec((1, blk, C_AUX), lambda b, i: (b, i, 0)),
                  pl.BlockSpec((3, DSA_HEADS, blk, blk), lambda b, i: (0, 0, 0, 0))] + slab_specs,
        out_specs=[pl.BlockSpec((1, blk, hd), lambda b, i: (b, i, 0))] + slab_specs,
        out_shape=[jax.ShapeDtypeStruct((bsz, t, hd), MIXER_DTYPE)]
                  + [jax.ShapeDtypeStruct(s.shape, BF16) for s in slabs],
        scratch_shapes=[pltpu.VMEM((n_blocks, LANES // IDX_DIM, blk, LANES), BF16),
                        pltpu.VMEM((n_blocks, hd, blk), BF16),
                        pltpu.VMEM((n_blocks, blk, blk), jnp.int32),
                        pltpu.VMEM((n_blocks, blk, blk), jnp.int16),
                        pltpu.VMEM((n_blocks, blk, blk), jnp.int16),
                        pltpu.VMEM((hd, blk), F32),
                        pltpu.VMEM((blk, blk), F32),
                        pltpu.VMEM((DSA_HEADS, n_blocks, blk, blk), F32),
                        pltpu.VMEM((DSA_HEADS, blk, blk), BF16)],
        compiler_params=_cparams(("parallel", "arbitrary")),
        name="dsa_mixer",
    )(z_c, z_c, z_c, z_c, z_aux, z_aux, toe, *slabs)
    return (out, *[c.reshape(w.shape) for c, w in zip(converted, to_bf16)])


def _mixer_tail_kernel(h_ref, a_ref, b_ref, c_ref, d_ref, wout_ref, gxa_ref, wq_ref, k_ref, v_ref, wo_ref,
                       *refs, route):
    gw = GROUP_WIDTH
    x = h_ref[0]
    for gi, r in enumerate((a_ref, b_ref, c_ref, d_ref)):
        x = x + _dot(r[0], wout_ref[gi * gw:(gi + 1) * gw, :])

    hd = XA_HEADS * XA_DH
    q = _dot(_rms(x, gxa_ref[...]).astype(BF16), wq_ref[...]).astype(BF16)
    k = k_ref[0]
    v = v_ref[0]
    lane_h = lax.broadcasted_iota(jnp.int32, (1, hd), 1) // XA_DH
    o = jnp.zeros((x.shape[0], hd), F32)
    for h in range(XA_HEADS):
        s = _dot_nt(q, jnp.where(lane_h == h, k, 0.0).astype(BF16)) * (XA_DH ** -0.5)
        p = jnp.exp(s - jnp.max(s, axis=-1, keepdims=True))
        p = p / jnp.sum(p, axis=-1, keepdims=True)
        o = o + _dot(p.astype(BF16), jnp.where(lane_h == h, v, 0.0).astype(BF16))
    x = x + _dot(o.astype(BF16), wo_ref[...])

    if not route:
        o_ref, = refs
        o_ref[0] = x
        return
    gffn_ref, wr_ref, o_ref, hn_ref, gate_ref, cnt_ref = refs
    o_ref[0] = x
    hn = _rms(x, gffn_ref[...])
    hn_ref[0] = hn.astype(BF16)
    tm = hn.shape[0]
    lane = lax.broadcasted_iota(jnp.int32, (tm, GATE_LANES), 1)
    logits = jnp.where(lane < N_EXPERTS, _dot_3pass(hn, wr_ref[...]), -jnp.inf)
    m1 = jnp.max(logits, axis=-1, keepdims=True)
    i1 = jnp.min(jnp.where(logits == m1, lane, GATE_LANES), axis=-1, keepdims=True)
    rest = jnp.where(lane == i1, -jnp.inf, logits)
    m2 = jnp.max(rest, axis=-1, keepdims=True)
    i2 = jnp.min(jnp.where(rest == m2, lane, GATE_LANES), axis=-1, keepdims=True)
    e2 = jnp.exp(m2 - m1)
    g1 = 1.0 / (1.0 + e2)
    gates = jnp.where(lane == i1, g1, 0.0) + jnp.where(lane == i2, e2 * g1, 0.0)
    gates = jnp.where(lane == N_EXPERTS, i1.astype(F32), gates)
    gate_ref[0] = jnp.where(lane == N_EXPERTS + 1, i2.astype(F32), gates)
    sel = jnp.where((lane == i1) | (lane == i2), 1.0, 0.0)
    for c in range(tm // MOE_CHUNK):
        cnt_ref[c] = jnp.sum(sel[c * MOE_CHUNK:(c + 1) * MOE_CHUNK, :], axis=0, keepdims=True)


def mixer_tail(h3, mixers, w_out, g_xa, wq, k, v, wo, routing=None, tm=1024):
    bsz, t, d = h3.shape
    tm = min(tm, t)
    mlen = k.shape[1]
    hd = XA_HEADS * XA_DH
    gw = GROUP_WIDTH
    row = lambda b, i: (b, i, 0)
    fixed2 = lambda b, i: (0, 0)
    in_specs = ([pl.BlockSpec((1, tm, d), row)] + [pl.BlockSpec((1, tm, gw), row)] * 4
                + [pl.BlockSpec((4 * gw, d), fixed2), pl.BlockSpec((1, d), fixed2), pl.BlockSpec((d, hd), fixed2),
                   pl.BlockSpec((1, mlen, hd), lambda b, i: (b, 0, 0)),
                   pl.BlockSpec((1, mlen, hd), lambda b, i: (b, 0, 0)),
                   pl.BlockSpec((hd, d), fixed2)])
    args = [h3, *mixers, w_out, g_xa.reshape(1, d), wq, k, v, wo]
    out_specs = [pl.BlockSpec((1, tm, d), row)]
    out_shape = [jax.ShapeDtypeStruct((bsz, t, d), F32)]
    if routing is not None:
        assert tm % MOE_CHUNK == 0
        g_ffn, wr = routing
        in_specs += [pl.BlockSpec((1, d), fixed2), pl.BlockSpec((d, GATE_LANES), fixed2)]
        args += [g_ffn.reshape(1, d), jnp.zeros((d, GATE_LANES), F32).at[:, :N_EXPERTS].set(wr)]
        per_step = tm // MOE_CHUNK
        out_specs += [pl.BlockSpec((1, tm, d), row), pl.BlockSpec((1, tm, GATE_LANES), row),
                      pl.BlockSpec((per_step, 1, GATE_LANES), lambda b, i: (b * (t // tm) + i, 0, 0))]
        out_shape += [jax.ShapeDtypeStruct((bsz, t, d), BF16), jax.ShapeDtypeStruct((bsz, t, GATE_LANES), F32),
                      jax.ShapeDtypeStruct((bsz * t // MOE_CHUNK, 1, GATE_LANES), F32)]
    return pl.pallas_call(
        functools.partial(_mixer_tail_kernel, route=routing is not None),
        grid=(bsz, t // tm),
        in_specs=in_specs,
        out_specs=out_specs,
        out_shape=out_shape,
        compiler_params=_cparams(("parallel", "parallel")),
        name="mixer_tail",
    )(*args)


def _ffn_kernel(h_ref, g_ref, w1_ref, w3_ref, w2_ref, o_ref, *, tf):
    x = h_ref[...]
    hn = _rms(x, g_ref[...]).astype(BF16)
    acc = x
    for f0 in range(0, w1_ref.shape[1], tf):
        a = _dot(hn, w1_ref[:, f0:f0 + tf])
        b = _dot(hn, w3_ref[:, f0:f0 + tf])
        acc = acc + _dot((a * jax.nn.sigmoid(a) * b).astype(BF16), w2_ref[f0:f0 + tf, :])
    o_ref[...] = acc


def ffn(h, g, w1, w3, w2, tm=1024, tf=704):
    m, d = h.shape
    tm = min(tm, m)
    nf = w1.shape[1]
    assert nf % tf == 0
    resident = lambda shape: pl.BlockSpec(shape, lambda i: (0, 0), pipeline_mode=pl.Buffered(1))
    return pl.pallas_call(
        functools.partial(_ffn_kernel, tf=tf),
        grid=(m // tm,),
        in_specs=[pl.BlockSpec((tm, d), lambda i: (i, 0)),
                  pl.BlockSpec((1, d), lambda i: (0, 0)),
                  resident((d, nf)), resident((d, nf)), resident((nf, d))],
        out_specs=pl.BlockSpec((tm, d), lambda i: (i, 0)),
        out_shape=jax.ShapeDtypeStruct((m, d), F32),
        compiler_params=_cparams(("parallel",)),
        name="ffn",
    )(h, g.reshape(1, d), w1, w3, w2)


GATE_LANES = LANES
MOE_CHUNK = 256
MOE_ROWS = LANES
MOE_TOKENS = 2048
MOE_ALIGN = BF16_TILE_ROWS


def _moe_kernel(cb_ref, h_ref, hn_ref, gate_ref, w1_ref, w3_ref, w2_ref, *refs, final):
    if final:
        gfin_ref, *refs = refs
    y_ref, rank_row, rk_ref, gs_ref, xs_ref, yacc_ref, rc_ref, gc_ref = refs
    t = pl.program_id(0)
    e = pl.program_id(1)
    f = pl.program_id(2)
    ts, d = hn_ref.shape
    ch, rb = MOE_CHUNK, MOE_ROWS
    win = 2 * rb
    n_ch = ts // ch
    lane = lax.broadcasted_iota(jnp.int32, (1, GATE_LANES), 1)

    def before(c):
        return cb_ref[(t * (n_ch + 1) + c) * N_EXPERTS + e]

    n_blocks = (before(n_ch) + rb - 1) // rb
    n_windows = (n_blocks * rb + win - 1) // win + 1

    def windows(c):
        lo, hi = before(c), before(c + 1)
        s0 = (lo // MOE_ALIGN) * MOE_ALIGN
        return s0, jnp.where(hi > lo, (hi - s0 + win - 1) // win, 0)

    spans = [windows(c) for c in range(n_ch)]
    single = functools.reduce(jnp.logical_and, [n_win <= 1 for _, n_win in spans])

    def for_each_window(visit):
        @pl.when(single)
        def _():
            for c, (s0, _) in enumerate(spans):
                visit(c, s0)

        @pl.when(jnp.logical_not(single))
        def _():
            for c, (s0, n_win) in enumerate(spans):
                def body(j, carry, c=c, s0=s0):
                    visit(c, s0 + j * win)
                    return carry
                lax.fori_loop(0, n_win, body, 0)

    @pl.when((e == 0) & (f == 0))
    def _():
        y_ref[...] = h_ref[...]
        strict_lower = (lax.broadcasted_iota(jnp.int32, (ch, ch), 1)
                        < lax.broadcasted_iota(jnp.int32, (ch, ch), 0)).astype(BF16)
        lane_f = lane.astype(F32)
        offs = jnp.ones((1, GATE_LANES), F32)
        for c in range(n_ch):
            rows = slice(c * ch, (c + 1) * ch)
            g = gate_ref[rows, :]
            sel = jnp.where((lane_f == g[:, N_EXPERTS:N_EXPERTS + 1])
                            | (lane_f == g[:, N_EXPERTS + 1:N_EXPERTS + 2]), 1.0, 0.0)
            r = _dot(strict_lower, sel.astype(BF16)) + offs
            r = jnp.where(sel > 0.0, r, 0.0)
            rank_row[:, rows] = jnp.transpose(r) - 1.0
            high = jnp.floor(r * (1.0 / 256.0))
            rk_ref[rows, :] = jnp.concatenate([high, r - 256.0 * high], axis=1).astype(BF16)
            gs_ref[rows, :] = jnp.concatenate(_split_bf16(g, 3), axis=1)
            offs = offs + jnp.sum(sel, axis=0, keepdims=True)

    @pl.when(f == 0)
    def _():
        pick = jnp.where(lax.broadcasted_iota(jnp.int32, (GATE_LANES, GATE_LANES), 0) == e, 1.0, 0.0)
        pick_rank = jnp.concatenate([256.0 * pick, pick], axis=0).astype(BF16)
        pick_gate = jnp.concatenate([pick] * 3, axis=0).astype(BF16)
        for c in range(n_ch):
            rows = slice(c * ch, (c + 1) * ch)
            rc_ref[rows, :] = _dot(rk_ref[rows, :], pick_rank) - 1.0
            gc_ref[rows, :] = _dot(gs_ref[rows, :], pick_gate)

        def clear(j, carry):
            r0 = pl.multiple_of(j * win, win)
            xs_ref[pl.ds(r0, win), :] = jnp.zeros((win, d), BF16)
            yacc_ref[pl.ds(r0, win), :] = jnp.zeros((win, d), F32)
            return carry

        lax.fori_loop(0, n_windows, clear, 0)

        row_id = lax.broadcasted_iota(jnp.int32, (win, ch), 0)

        def gather_window(c, s):
            s = pl.multiple_of(s, MOE_ALIGN)
            ranks = rank_row[pl.ds(e, 1), c * ch:(c + 1) * ch]
            onehot = jnp.where(ranks == (s + row_id).astype(F32), 1.0, 0.0).astype(BF16)
            xs_ref[pl.ds(s, win), :] += _dot(onehot, hn_ref[c * ch:(c + 1) * ch, :]).astype(BF16)

        for_each_window(gather_window)

    def ffn_rows(r0, rows):
        x = xs_ref[pl.ds(r0, rows), :]
        a = _dot(x, w1_ref[0])
        g3 = _dot(x, w3_ref[0])
        yacc_ref[pl.ds(r0, rows), :] += _dot((a * jax.nn.sigmoid(a) * g3).astype(BF16), w2_ref[0])

    five = n_blocks == 5
    half_tail = before(n_ch) <= 4 * rb + rb // 2

    @pl.when(five & half_tail)
    def _():
        ffn_rows(0, 4 * rb + rb // 2)

    @pl.when(five & jnp.logical_not(half_tail))
    def _():
        ffn_rows(0, 5 * rb)

    @pl.when(jnp.logical_not(five))
    def _():
        n_quads = n_blocks // 4

        def ffn_quad(j, carry):
            ffn_rows(pl.multiple_of(j * (4 * rb), 4 * rb), 4 * rb)
            return carry

        lax.fori_loop(0, n_quads, ffn_quad, 0)
        tail = pl.multiple_of(n_quads * (4 * rb), 4 * rb)

        @pl.when((n_blocks & 2) != 0)
        def _():
            ffn_rows(tail, 2 * rb)

        @pl.when((n_blocks & 1) != 0)
        def _():
            ffn_rows(pl.multiple_of(tail + (n_blocks & 2) * rb, rb), rb)

    @pl.when(f == pl.num_programs(2) - 1)
    def _():
        def to_bf16(j, carry):
            r0 = pl.multiple_of(j * win, win)
            xs_ref[pl.ds(r0, win), :] = yacc_ref[pl.ds(r0, win), :].astype(BF16)
            return carry

        lax.fori_loop(0, n_windows, to_bf16, 0)

        lane_id = lax.broadcasted_iota(jnp.int32, (ch, rb), 1).astype(F32)

        def scatter_window(c, s):
            rows = slice(c * ch, (c + 1) * ch)
            s = pl.multiple_of(s, MOE_ALIGN)
            rank = rc_ref[rows, :] - s.astype(F32)
            onehot = jnp.concatenate([jnp.where(rank == lane_id, 1.0, 0.0),
                                      jnp.where(rank == lane_id + float(rb), 1.0, 0.0)],
                                     axis=1).astype(BF16)
            gate = jnp.concatenate([gc_ref[rows, :]] * (d // GATE_LANES), axis=1)
            y_ref[rows, :] += gate * _dot(onehot, xs_ref[pl.ds(s, win), :])

        for_each_window(scatter_window)

        if final:
            @pl.when(e == pl.num_programs(1) - 1)
            def _():
                y_ref[...] = _rms(y_ref[...], gfin_ref[...])


def moe(h, hn, gates, cnt, w1, w3, w2, final_gain=None, tf=896):
    m, d = hn.shape
    ts = min(MOE_TOKENS, m)
    n_exp, _, nf = w1.shape
    assert nf % tf == 0 and m % ts == 0 and ts % MOE_CHUNK == 0 and MOE_ROWS == GATE_LANES
    n_ch = ts // MOE_CHUNK
    counts = cnt.reshape(m // ts, n_ch, GATE_LANES)[:, :, :n_exp].astype(jnp.int32)
    before = jnp.concatenate([jnp.zeros((m // ts, 1, n_exp), jnp.int32), jnp.cumsum(counts, axis=1)], axis=1)
    per_tile = lambda width: pl.BlockSpec((ts, width), lambda t, e, f, cb: (t, 0), pipeline_mode=pl.Buffered(1))
    in_specs = [per_tile(d), per_tile(d), per_tile(GATE_LANES),
                pl.BlockSpec((1, d, tf), lambda t, e, f, cb: (e, 0, f)),
                pl.BlockSpec((1, d, tf), lambda t, e, f, cb: (e, 0, f)),
                pl.BlockSpec((1, tf, d), lambda t, e, f, cb: (e, f, 0))]
    args = [before.reshape(-1), h, hn, gates, w1, w3, w2]
    if final_gain is not None:
        in_specs.append(pl.BlockSpec((1, d), lambda t, e, f, cb: (0, 0)))
        args.append(final_gain.reshape(1, d))
    grid_spec = pltpu.PrefetchScalarGridSpec(
        num_scalar_prefetch=1,
        grid=(m // ts, n_exp, nf // tf),
        in_specs=in_specs,
        out_specs=per_tile(d),
        scratch_shapes=[pltpu.VMEM((GATE_LANES, ts), F32),
                        pltpu.VMEM((ts, 2 * GATE_LANES), BF16),
                        pltpu.VMEM((ts, 3 * GATE_LANES), BF16),
                        pltpu.VMEM((ts + 2 * MOE_ROWS, d), BF16),
                        pltpu.VMEM((ts + 2 * MOE_ROWS, d), F32),
                        pltpu.VMEM((ts, MOE_ROWS), F32),
                        pltpu.VMEM((ts, GATE_LANES), F32)])
    return pl.pallas_call(
        functools.partial(_moe_kernel, final=final_gain is not None),
        grid_spec=grid_spec,
        out_shape=jax.ShapeDtypeStruct((m, d), F32),
        compiler_params=_cparams(("parallel", "arbitrary", "arbitrary")),
        name="moe",
    )(*args)


def _final_norm_kernel(g_ref, x_ref, o_ref):
    o_ref[...] = _rms(x_ref[...], g_ref[...])


def final_rmsnorm(g, x, tm=1024):
    m, d = x.shape
    tm = min(tm, m)
    return pl.pallas_call(
        _final_norm_kernel,
        grid=(m // tm,),
        in_specs=[pl.BlockSpec((1, d), lambda i: (0, 0)), pl.BlockSpec((tm, d), lambda i: (i, 0))],
        out_specs=pl.BlockSpec((tm, d), lambda i: (i, 0)),
        out_shape=jax.ShapeDtypeStruct((m, d), F32),
        compiler_params=_cparams(("parallel",)),
        name="final_norm",
    )(g.reshape(1, d), x)


def _pad_cols(w, width):
    return jnp.pad(w, ((0, 0), (0, width - w.shape[1])))


def _in_proj_weight(w_in):
    a_end = 2 * GLA_HEADS * GLA_DK + 2 * GROUP_WIDTH + GLA_RANK
    b_end = a_end + GROUP_WIDTH
    c_end = b_end + 3 * GROUP_WIDTH + IDX_HEADS * IDX_DIM + IDX_DIM + IDX_HEADS
    assert w_in.shape[1] == c_end + 2 * GROUP_WIDTH
    return jnp.concatenate([_pad_cols(w_in[:, :a_end], A_WIDTH), w_in[:, a_end:b_end],
                            _pad_cols(w_in[:, b_end:c_end], C_MAIN + C_AUX), w_in[:, c_end:]], axis=1).astype(BF16)


def hybrid_layer(h, mem2, toe, p, bsz, t, mixer_weights, routing=None):
    m, d = h.shape
    z_a, z_b, z_c, z_aux, z_d = norm_matmul(h, p["norm_mix"], _in_proj_weight(p["w_in"]),
                                            (A_WIDTH, B_WIDTH, C_MAIN, C_AUX, D_WIDTH),
                                            (F32, F32, BF16, F32, F32))
    o_a = gla_mixer(z_a.reshape(bsz, t, A_WIDTH), p["gla_wa2"], p["gla_ba"], p["gla_norm"])
    o_b = pool_mixer(z_b.reshape(bsz, t, B_WIDTH), p["pool_w"], p["pool_b"].reshape(-1), p["pool_scale"])
    o_c, *weights = dsa_mixer(z_c.reshape(bsz, t, C_MAIN), z_aux.reshape(bsz, t, C_AUX), toe, mixer_weights)
    o_d = sgu_mixer(z_d.reshape(bsz, t, D_WIDTH), p["sgu_ln_g"], p["sgu_ln_b"], p["sgu_w"], p["sgu_b"])
    hd = XA_HEADS * XA_DH
    k, v = norm_matmul(mem2, p["norm_mem"], p["xa_wkv"].astype(BF16), (hd, hd))
    mlen = mem2.shape[0] // bsz
    outs = mixer_tail(h.reshape(bsz, t, d), (o_a, o_b, o_c, o_d), p["w_out"].astype(BF16), p["norm_xa"],
                      p["xa_wq"].astype(BF16), k.reshape(bsz, mlen, hd), v.reshape(bsz, mlen, hd),
                      p["xa_wo"].astype(BF16), routing)
    if routing is None:
        return weights, outs[0].reshape(m, d)
    h3, hn, gates, counts = outs
    return weights, h3.reshape(m, d), hn.reshape(m, d), gates.reshape(m, GATE_LANES), counts


def kernel(x, mem, rel_bias, final_norm, norm_mix, w_in, gla_wa2, gla_ba, gla_norm, pool_w, pool_b,
           pool_scale, sgu_ln_g, sgu_ln_b, sgu_w, sgu_b, w_out, norm_xa, norm_mem, xa_wq, xa_wkv, xa_wo,
           norm_ffn, ffn_w1, ffn_w3, ffn_w2, router, moe_w1, moe_w3, moe_w2):
    bsz, t, d = x.shape
    depth = norm_mix.shape[0]
    h = x.reshape(bsz * t, d)
    mem2 = mem.reshape(-1, d)
    toe = rel_bias_tables(rel_bias)
    normed = False
    for i in range(depth):
        p = dict(norm_mix=norm_mix[i], w_in=w_in[i], gla_wa2=gla_wa2[i], gla_ba=gla_ba[i],
                 gla_norm=gla_norm[i], pool_w=pool_w[i], pool_b=pool_b[i], pool_scale=pool_scale[i],
                 sgu_ln_g=sgu_ln_g[i], sgu_ln_b=sgu_ln_b[i], sgu_w=sgu_w[i], sgu_b=sgu_b[i],
                 w_out=w_out[i], norm_xa=norm_xa[i], norm_mem=norm_mem[i], xa_wq=xa_wq[i],
                 xa_wkv=xa_wkv[i], xa_wo=xa_wo[i])
        j = i // 2
        if i % 2 == 0:
            (w1, w3, w2), h = hybrid_layer(h, mem2, toe, p, bsz, t, (ffn_w1[j], ffn_w3[j], ffn_w2[j]))
            h = ffn(h, norm_ffn[i], w1, w3, w2)
        else:
            (w1, w3, w2), h, hn, gates, counts = hybrid_layer(h, mem2, toe, p, bsz, t,
                                                              (moe_w1[j], moe_w3[j], moe_w2[j]),
                                                              routing=(norm_ffn[i], router[j]))
            normed = i == depth - 1
            h = moe(h, hn, gates, counts, w1, w3, w2, final_gain=final_norm if normed else None)
    if not normed:
        h = final_rmsnorm(final_norm, h)
    return h.reshape(bsz, t, d)
```

```python
import functools
import math

import jax
import jax.numpy as jnp
import numpy as np
from jax import lax
from jax.experimental import pallas as pl
from jax.experimental.pallas import tpu as pltpu

F32 = jnp.float32
BF16 = jnp.bfloat16
EPS = 1e-6

LANES = 128
BF16_TILE_ROWS = 16
VMEM_LIMIT = 56 * 1024 * 1024

GROUP_WIDTH = 256

GLA_HEADS = 4
GLA_DV = 64
GLA_DK = 32
GLA_RANK = 16
GLA_TAU = 16.0
GLA_CHUNK = 64
GLA_GROUP = 32

POOL_WINDOWS = (2, 4, 8, 16)
POOL_CG = 64

DSA_HEADS = 4
DSA_DH = 64
IDX_HEADS = 8
IDX_DIM = 32
DSA_TOPK_MAX = 256
DSA_BLOCK = 256
DSA_SUB = 64
DSA_HEAD_GROUPS = ((0, 1, 2, 3),)

SGU_GROUPS = 4
SGU_CHUNK = 128
SGU_CG = 64

REL_BUCKETS = 32
REL_MAX_DIST = 128

XA_HEADS = 4
XA_DH = 64

N_EXPERTS = 8

A_WIDTH = 896
B_WIDTH = 256
C_MAIN = 1024
C_AUX = LANES
D_WIDTH = 512

MIXER_DTYPE = BF16
INT_MIN = -(2 ** 31)
NEG_BIG = -1e30


def _cparams(sem):
    return pltpu.CompilerParams(dimension_semantics=sem, vmem_limit_bytes=VMEM_LIMIT)


def _dot(a, b):
    return jnp.dot(a, b, preferred_element_type=F32)


def _dot_nt(a, b):
    return lax.dot_general(a, b, (((1,), (1,)), ((), ())), preferred_element_type=F32)


def _dot_tn(a, b):
    return lax.dot_general(a, b, (((0,), (0,)), ((), ())), preferred_element_type=F32)


def _split_bf16(x, terms):
    parts = []
    for _ in range(terms):
        p = x.astype(BF16)
        parts.append(p)
        x = x - p.astype(F32)
    return parts


def _dot_exact_rhs(a, b, terms):
    out = None
    for p in _split_bf16(a, terms):
        d = _dot(p, b)
        out = d if out is None else out + d
    return out


def _dot_exact_lhs(a, b, terms):
    out = None
    for p in _split_bf16(b, terms):
        d = _dot(a, p)
        out = d if out is None else out + d
    return out


def _dot_3pass(a, b):
    a_hi, a_lo = _split_bf16(a, 2)
    b_hi, b_lo = _split_bf16(b, 2)
    return _dot(a_hi, b_hi) + (_dot(a_hi, b_lo) + _dot(a_lo, b_hi))


def _rms(x, g):
    return x * lax.rsqrt(jnp.mean(x * x, axis=-1, keepdims=True) + EPS) * g


def _norm_matmul_kernel(h_ref, g_ref, w_ref, *out_refs, widths):
    hb = _rms(h_ref[...], g_ref[...]).astype(BF16)
    off = 0
    for o_ref, wd in zip(out_refs, widths):
        o_ref[...] = _dot(hb, w_ref[:, off:off + wd]).astype(o_ref.dtype)
        off += wd


def norm_matmul(h, g, w, widths, dtypes=None, tm=1024):
    dtypes = dtypes or (F32,) * len(widths)
    m, d = h.shape
    tm = min(tm, m)
    n = sum(widths)
    return pl.pallas_call(
        functools.partial(_norm_matmul_kernel, widths=widths),
        grid=(m // tm,),
        in_specs=[pl.BlockSpec((tm, d), lambda i: (i, 0)),
                  pl.BlockSpec((1, d), lambda i: (0, 0)),
                  pl.BlockSpec((d, n), lambda i: (0, 0))],
        out_specs=[pl.BlockSpec((tm, wd), lambda i: (i, 0)) for wd in widths],
        out_shape=[jax.ShapeDtypeStruct((m, wd), dt) for wd, dt in zip(widths, dtypes)],
        compiler_params=_cparams(("parallel",)),
        name="norm_matmul",
    )(h, g.reshape(1, d), w)


def _log_sigmoid(x):
    return jnp.minimum(x, 0.0) - jnp.log1p(jnp.exp(-jnp.abs(x)))


def _gla_kernel(z_ref, wa2_ref, ba_ref, ng_ref, o_ref, s_ref, *, group, n_groups):
    c = GLA_CHUNK
    grp = group * c
    hk = GLA_HEADS * GLA_DK
    hv = GLA_HEADS * GLA_DV
    s_ref[...] = jnp.zeros_like(s_ref)

    head_k = lax.broadcasted_iota(jnp.int32, (1, hk), 1) // GLA_DK
    head_v = lax.broadcasted_iota(jnp.int32, (1, hv), 1) // GLA_DV
    cum = min(grp, 256)
    g_row = lax.broadcasted_iota(jnp.int32, (cum, cum), 0)
    g_col = lax.broadcasted_iota(jnp.int32, (cum, cum), 1)
    tril = (((g_row // c) == (g_col // c)) & (g_col <= g_row)).astype(BF16)
    causal4 = (lax.broadcasted_iota(jnp.int32, (GLA_HEADS * c, c), 1)
               <= lax.broadcasted_iota(jnp.int32, (GLA_HEADS * c, c), 0) % c)
    state_mask = (lax.broadcasted_iota(jnp.int32, (hk, hv), 0) // GLA_DK
                  == lax.broadcasted_iota(jnp.int32, (hk, hv), 1) // GLA_DV)
    norm_mat = jnp.where(lax.broadcasted_iota(jnp.int32, (hv, hv), 0) // GLA_DV
                         == lax.broadcasted_iota(jnp.int32, (hv, hv), 1) // GLA_DV,
                         1.0 / GLA_DV, 0.0).astype(BF16)
    wa2 = wa2_ref[...]
    ba = ba_ref[...]
    ng = ng_ref[...]

    def body(n, carry):
        r0 = pl.multiple_of(n * grp, grp)
        z = z_ref[0, pl.ds(r0, grp), :]
        q, k, v, g, lr = z[:, 0:128], z[:, 128:256], z[:, 256:512], z[:, 512:768], z[:, 768:896]
        log_a = _log_sigmoid(_dot_3pass(lr, wa2) + ba) / GLA_TAU
        b = jnp.concatenate([_dot_exact_lhs(tril, log_a[r:r + cum], 3) for r in range(0, grp, cum)], axis=0)
        b_end = jnp.concatenate([jnp.broadcast_to(b[(ci + 1) * c - 1:(ci + 1) * c, :], (c, hk))
                                 for ci in range(group)], axis=0)
        q_t = q * (GLA_DK ** -0.5) * jnp.exp(b)
        q_tb = q_t.astype(BF16)
        k_t = (k * jnp.exp(-b)).astype(BF16)
        k_dec = (k * jnp.exp(b_end - b)).astype(BF16)
        vb = v.astype(BF16)
        outs = []
        for ci in range(group):
            rows = slice(ci * c, (ci + 1) * c)
            q4 = jnp.concatenate([jnp.where(head_k == h, q_t[rows], 0.0) for h in range(GLA_HEADS)],
                                 axis=0).astype(BF16)
            att = jnp.where(causal4, _dot_nt(q4, k_t[rows]), 0.0)
            r = _dot(att.astype(BF16), vb[rows])
            o = _dot(q_tb[rows], s_ref[...].astype(BF16))
            for h in range(GLA_HEADS):
                o = o + jnp.where(head_v == h, r[h * c:(h + 1) * c, :], 0.0)
            outs.append(o)
            kv = jnp.where(state_mask, _dot_tn(k_dec[rows], vb[rows]), 0.0)
            last = b_end[ci * c:ci * c + 1, :]
            dec = jnp.exp(jnp.transpose(jnp.broadcast_to(last, (hk, hk))))
            s_ref[...] = s_ref[...] * jnp.concatenate([dec, dec], axis=1) + kv
        o = jnp.concatenate(outs, axis=0)
        o = o * lax.rsqrt(_dot_exact_rhs(o * o, norm_mat, 2) + EPS) * ng
        o_ref[0, pl.ds(r0, grp), :] = (o * (g * jax.nn.sigmoid(g))).astype(o_ref.dtype)
        return carry

    lax.fori_loop(0, n_groups, body, 0)


def gla_mixer(z_a, wa2, ba, norm_g):
    bsz, t, _ = z_a.shape
    hk = GLA_HEADS * GLA_DK
    wa2p = jnp.zeros((LANES, hk), F32).at[:GLA_RANK].set(wa2)
    n_chunks = t // GLA_CHUNK
    group = math.gcd(GLA_GROUP, n_chunks)
    return pl.pallas_call(
        functools.partial(_gla_kernel, group=group, n_groups=n_chunks // group),
        grid=(bsz,),
        in_specs=[pl.BlockSpec((1, t, A_WIDTH), lambda b: (b, 0, 0)),
                  pl.BlockSpec((LANES, hk), lambda b: (0, 0)),
                  pl.BlockSpec((1, hk), lambda b: (0, 0)),
                  pl.BlockSpec((1, GROUP_WIDTH), lambda b: (0, 0))],
        out_specs=pl.BlockSpec((1, t, GROUP_WIDTH), lambda b: (b, 0, 0)),
        out_shape=jax.ShapeDtypeStruct((bsz, t, GROUP_WIDTH), MIXER_DTYPE),
        scratch_shapes=[pltpu.VMEM((hk, GROUP_WIDTH), F32)],
        compiler_params=_cparams(("parallel",)),
        name="gla_mixer",
    )(z_a, wa2p, ba.reshape(1, hk), norm_g.reshape(1, GROUP_WIDTH))


def _pool_kernel(u_ref, w_ref, b_ref, sc_ref, o_ref):
    u = u_ref[0]
    t, gw = u.shape
    row = lax.broadcasted_iota(jnp.int32, (t, gw), 0)
    grp = lax.broadcasted_iota(jnp.int32, (t, gw), 1) // POOL_CG

    def shifted(x, k):
        return jnp.where(row >= k, pltpu.roll(x, k, axis=0), 0.0)

    s = u
    p = jnp.zeros_like(u)
    for gi, win in enumerate(POOL_WINDOWS):
        half = win // 2
        s = s + shifted(s, half)
        cnt = jnp.minimum(row + 1, win).astype(F32)
        p = jnp.where(grp == gi, s / cnt - u, p)
    y = _dot(p.astype(BF16), w_ref[...]) + b_ref[...]
    o_ref[0] = (y * sc_ref[...]).astype(o_ref.dtype)


def pool_mixer(z_b, w, b, scale):
    assert POOL_WINDOWS == (2, 4, 8, 16)
    bsz, t, gw = z_b.shape
    w_bd = jnp.zeros((gw, gw), F32)
    for gi in range(len(POOL_WINDOWS)):
        w_bd = w_bd.at[gi * POOL_CG:(gi + 1) * POOL_CG, gi * POOL_CG:(gi + 1) * POOL_CG].set(w[gi])
    return pl.pallas_call(
        _pool_kernel,
        grid=(bsz,),
        in_specs=[pl.BlockSpec((1, t, gw), lambda i: (i, 0, 0)),
                  pl.BlockSpec((gw, gw), lambda i: (0, 0)),
                  pl.BlockSpec((1, gw), lambda i: (0, 0)),
                  pl.BlockSpec((1, gw), lambda i: (0, 0))],
        out_specs=pl.BlockSpec((1, t, gw), lambda i: (i, 0, 0)),
        out_shape=jax.ShapeDtypeStruct((bsz, t, gw), MIXER_DTYPE),
        compiler_params=_cparams(("parallel",)),
        name="pool_mixer",
    )(z_b, w_bd.astype(BF16), b.reshape(1, gw), scale.reshape(1, gw))


def _sgu_kernel(z_ref, lg_ref, lb_ref, w_ref, bm_ref, o_ref, *, chunks):
    c = SGU_CHUNK
    gw = GROUP_WIDTH
    rows = SGU_GROUPS * c
    tri = (lax.broadcasted_iota(jnp.int32, (rows, c), 1)
           <= lax.broadcasted_iota(jnp.int32, (rows, c), 0) % c)
    ws = jnp.where(tri, w_ref[...], 0.0).astype(BF16)
    grp = lax.broadcasted_iota(jnp.int32, (1, gw), 1) // SGU_CG
    for ci in range(chunks):
        z = jax.nn.gelu(z_ref[0, ci * c:(ci + 1) * c, :], approximate=True)
        u, v = z[:, :gw], z[:, gw:]
        mu = jnp.mean(v, axis=-1, keepdims=True)
        var = jnp.mean(jnp.square(v - mu), axis=-1, keepdims=True)
        vn = (v - mu) * lax.rsqrt(var + EPS) * lg_ref[...] + lb_ref[...]
        r = _dot(ws, vn.astype(BF16))
        mixed = bm_ref[...]
        for g in range(SGU_GROUPS):
            mixed = mixed + jnp.where(grp == g, r[g * c:(g + 1) * c, :], 0.0)
        o_ref[0, ci * c:(ci + 1) * c, :] = (u * mixed).astype(o_ref.dtype)


def sgu_mixer(z_d, ln_g, ln_b, w_s, b_s, chunks=16):
    bsz, t, _ = z_d.shape
    gw = GROUP_WIDTH
    chunks = math.gcd(chunks, t // SGU_CHUNK)
    tt = chunks * SGU_CHUNK
    bias = jnp.repeat(b_s.T, SGU_CG, axis=1)
    return pl.pallas_call(
        functools.partial(_sgu_kernel, chunks=chunks),
        grid=(bsz, t // tt),
        in_specs=[pl.BlockSpec((1, tt, 2 * gw), lambda b, i: (b, i, 0)),
                  pl.BlockSpec((1, gw), lambda b, i: (0, 0)),
                  pl.BlockSpec((1, gw), lambda b, i: (0, 0)),
                  pl.BlockSpec((SGU_GROUPS * SGU_CHUNK, SGU_CHUNK), lambda b, i: (0, 0)),
                  pl.BlockSpec((SGU_CHUNK, gw), lambda b, i: (0, 0))],
        out_specs=pl.BlockSpec((1, tt, gw), lambda b, i: (b, i, 0)),
        out_shape=jax.ShapeDtypeStruct((bsz, t, gw), MIXER_DTYPE),
        compiler_params=_cparams(("parallel", "parallel")),
        name="sgu_mixer",
    )(z_d, ln_g.reshape(1, gw), ln_b.reshape(1, gw),
      w_s.reshape(SGU_GROUPS * SGU_CHUNK, SGU_CHUNK), bias)


def _bucket_table():
    assert REL_MAX_DIST <= DSA_BLOCK + 1
    s = np.arange(DSA_BLOCK)[:, None]
    t = np.arange(DSA_BLOCK)[None, :]
    dist = np.stack([t - s, DSA_BLOCK + t - s, 2 * DSA_BLOCK + t - s])
    n = np.maximum(dist, 0)
    max_exact = REL_BUCKETS // 2
    nf = np.maximum(n, 1).astype(np.float32)
    large = max_exact + (np.log(nf / np.float32(max_exact)) / np.float32(math.log(REL_MAX_DIST / max_exact))
                         * np.float32(REL_BUCKETS - max_exact)).astype(np.int32)
    return np.where(n < max_exact, n, np.minimum(large, REL_BUCKETS - 1)).astype(np.int32)


def _bias_table_kernel(rb_ref, bucket_ref, o_ref):
    for back in range(3):
        bucket = bucket_ref[back]
        for h in range(DSA_HEADS):
            acc = jnp.zeros(bucket.shape, F32)
            for b in range(REL_BUCKETS):
                acc = jnp.where(bucket == b, rb_ref[b * DSA_HEADS + h], acc)
            o_ref[back, h] = acc


def rel_bias_tables(rel_bias):
    blk = DSA_BLOCK
    return pl.pallas_call(
        _bias_table_kernel,
        in_specs=[pl.BlockSpec(memory_space=pltpu.SMEM),
                  pl.BlockSpec((3, blk, blk), lambda: (0, 0, 0))],
        out_specs=pl.BlockSpec((3, DSA_HEADS, blk, blk), lambda: (0, 0, 0, 0)),
        out_shape=jax.ShapeDtypeStruct((3, DSA_HEADS, blk, blk), F32),
        name="rel_bias_tables",
    )(rel_bias.reshape(-1), jnp.asarray(_bucket_table()))


def _dsa_kernel(q_ref, k_ref, v_ref, qi_ref, kw_ref, qw_ref, toe_ref, *refs, topk, n_blocks, idx_bits, n_cast):
    for src, dst in zip(refs[:n_cast], refs[n_cast + 1:2 * n_cast + 1]):
        dst[...] = src[...].astype(BF16)
    o_ref = refs[n_cast]
    kpl_ref, vt_ref, keys_ref, khi_ref, klo_ref, acc_ref, am_ref, lg_ref, p_ref = refs[2 * n_cast + 1:]
    blk = DSA_BLOCK
    sub = DSA_SUB
    i = pl.program_id(1)
    hd = DSA_HEADS * DSA_DH
    heads_per_half = LANES // IDX_DIM

    @pl.when(i == 0)
    def _():
        lane = lax.broadcasted_iota(jnp.int32, (blk, LANES), 1)

        def build(kb, c):
            r0 = pl.multiple_of(kb * blk, blk)
            ki = jnp.where(lane < IDX_DIM, kw_ref[0, pl.ds(r0, blk), :], 0.0)
            for j in range(heads_per_half):
                kpl_ref[kb, j] = (ki if j == 0 else pltpu.roll(ki, j * IDX_DIM, axis=1)).astype(BF16)
            vt_ref[kb] = jnp.transpose(v_ref[0, pl.ds(r0, blk), :].astype(F32)).astype(BF16)
            return c

        lax.fori_loop(0, n_blocks, build, 0)

    s_loc = lax.broadcasted_iota(jnp.int32, (blk, blk), 0)
    t_loc = lax.broadcasted_iota(jnp.int32, (blk, blk), 1)
    n_vis = i + 1

    qi_t = jnp.transpose(qi_ref[0].astype(F32)).astype(BF16)
    qi_halves = [qi_t[:LANES, :], qi_t[LANES:, :]]
    w_t = jnp.transpose(qw_ref[0])
    w_rows = [w_t[IDX_DIM + h:IDX_DIM + h + 1, :] * (IDX_HEADS ** -0.5) * (IDX_DIM ** -0.5)
              for h in range(IDX_HEADS)]
    s_sub = lax.broadcasted_iota(jnp.int32, (sub, blk), 0)
    t_sub = lax.broadcasted_iota(jnp.int32, (sub, blk), 1)

    def score_body(kb, c):
        for ci in range(blk // sub):
            rows = slice(ci * sub, (ci + 1) * sub)
            sc = jnp.zeros((sub, blk), F32)
            for half in range(2):
                for j in range(heads_per_half):
                    d = _dot(kpl_ref[kb, j, rows, :], qi_halves[half])
                    sc = sc + jnp.maximum(d, 0.0) * w_rows[half * heads_per_half + j]
            sc = jnp.where(sc == 0.0, 0.0, sc)
            bits = pltpu.bitcast(sc, jnp.int32)
            key = jnp.where(bits < 0, bits ^ jnp.int32(0x7FFFFFFF), bits)
            vis = (kb < i) | (s_sub + ci * sub <= t_sub)
            key = jnp.where(vis, key, jnp.int32(INT_MIN))
            keys_ref[kb, rows, :] = key
            khi_ref[kb, rows, :] = lax.shift_right_arithmetic(key, 16).astype(jnp.int16)
            klo_ref[kb, rows, :] = ((key & 0xFFFF) - 2 ** 15).astype(jnp.int16)
        return c

    lax.fori_loop(0, n_vis, score_body, 0)

    def count(pred):
        def body(kb, acc):
            hit = jnp.where(pred(keys_ref[kb], kb), 1.0, 0.0)
            return acc + jnp.sum(hit.reshape(blk // 32, 32, blk), axis=0)
        acc = lax.fori_loop(0, n_vis, body, jnp.zeros((32, blk), F32))
        return jnp.sum(acc, axis=0, keepdims=True)

    def count16(ref, cand):
        cand = cand.astype(jnp.int16)

        def body(kb, acc):
            hit = jnp.where(ref[kb] >= cand, jnp.int16(1), jnp.int16(0))
            for j in range(blk // 32):
                acc = acc + hit[j * 32:(j + 1) * 32]
            return acc
        acc = lax.fori_loop(0, n_vis, body, jnp.zeros((32, blk), jnp.int16))
        return jnp.sum(acc.astype(jnp.int32).astype(F32), axis=0, keepdims=True)

    def search16(ref, offset):
        lowest = jnp.full((1, blk), -(2 ** 15), jnp.int32)
        base = jnp.where(offset + count16(ref, jnp.zeros((1, blk), jnp.int32)) >= kf, 0, lowest)

        def bit_body(it, base):
            cand = base | lax.shift_left(jnp.int32(1), 14 - it)
            return jnp.where(offset + count16(ref, cand) >= kf, cand, base)
        return lax.fori_loop(0, 15, bit_body, base)

    kf = float(topk)
    zero = jnp.zeros((1, blk), jnp.int32)
    thr_hi = search16(khi_ref, 0.0)
    thr_hi16 = thr_hi.astype(jnp.int16)

    def low_body(kb, above):
        hi = khi_ref[kb]
        klo_ref[kb] = jnp.where(hi == thr_hi16, klo_ref[kb], jnp.int16(-(2 ** 15)))
        hit = jnp.where(hi > thr_hi16, jnp.int16(1), jnp.int16(0))
        for j in range(blk // 32):
            above = above + hit[j * 32:(j + 1) * 32]
        return above

    above = lax.fori_loop(0, n_vis, low_body, jnp.zeros((32, blk), jnp.int16))
    n_above = jnp.sum(above.astype(jnp.int32).astype(F32), axis=0, keepdims=True)
    thr_lo = search16(klo_ref, n_above)
    thr = lax.shift_left(thr_hi, 16) | (thr_lo + 2 ** 15)
    thr_sel = jnp.maximum(thr, jnp.int32(INT_MIN + 1))

    n_ge = count(lambda key, kb: key >= thr)
    excess = jnp.where((n_ge > kf) & (thr > jnp.int32(INT_MIN)), 1.0, 0.0)

    def tie_search():
        need = kf - count(lambda key, kb: key > thr)

        def tie_body(it, j):
            cand = j | lax.shift_left(jnp.int32(1), idx_bits - 1 - it)
            below = count(lambda key, kb: (key == thr) & (kb * blk + s_loc < cand))
            return jnp.where(below < need, cand, j)
        return lax.fori_loop(0, idx_bits, tie_body, zero)

    last = lax.cond(jnp.max(excess) > 0.0, tie_search, lambda: jnp.full((1, blk), 2 ** 30, jnp.int32))

    assert DSA_DH ** -0.5 == 0.125
    q_t = jnp.transpose(q_ref[0].astype(F32) * (DSA_DH ** -0.5))
    row_h = lax.broadcasted_iota(jnp.int32, (hd, 1), 0) // DSA_DH
    q_heads = [jnp.where(row_h == h, q_t, 0.0).astype(BF16) for h in range(DSA_HEADS)]
    acc_ref[...] = jnp.zeros_like(acc_ref)
    n_sub = blk // sub

    def att_body(kb, carry):
        ms, ls = carry
        back = jnp.minimum(i - kb, 2)
        for ci in range(n_sub):
            rows = slice(ci * sub, (ci + 1) * sub)
            key = keys_ref[kb, rows, :]
            sel = (key > thr_sel) | ((key == thr_sel) & (kb * blk + ci * sub + s_sub <= last))
            am_ref[rows, :] = jnp.where(sel, 0.0, NEG_BIG)
        new_ms, new_ls = list(ms), list(ls)
        for heads in DSA_HEAD_GROUPS:
            alphas = {}
            for h in heads:
                pm = jnp.full((8, blk), NEG_BIG, F32)
                for ci in range(n_sub):
                    rows = slice(ci * sub, (ci + 1) * sub)
                    k_rows = k_ref[0, pl.ds(pl.multiple_of(kb * blk + ci * sub, sub), sub), :]
                    lg = _dot(k_rows, q_heads[h]) + toe_ref[back, h, rows, :] + am_ref[rows, :]
                    lg_ref[h, rows, :] = lg
                    pm = jnp.maximum(pm, jnp.max(lg.reshape(sub // 8, 8, blk), axis=0))
                new_ms[h] = jnp.maximum(ms[h], jnp.max(pm, axis=0, keepdims=True))
                alphas[h] = jnp.exp(ms[h] - new_ms[h])
            for h in heads:
                ps = jnp.zeros((8, blk), F32)
                for ci in range(n_sub):
                    rows = slice(ci * sub, (ci + 1) * sub)
                    p = jnp.exp(lg_ref[h, rows, :] - new_ms[h])
                    ps = ps + jnp.sum(p.reshape(sub // 8, 8, blk), axis=0)
                    p_ref[h, rows, :] = p.astype(BF16)
                new_ls[h] = ls[h] * alphas[h] + jnp.sum(ps, axis=0, keepdims=True)
            for h in heads:
                hrows = slice(h * DSA_DH, (h + 1) * DSA_DH)
                acc_ref[hrows, :] = acc_ref[hrows, :] * alphas[h] + _dot(vt_ref[kb, hrows, :], p_ref[h])
        return tuple(new_ms), tuple(new_ls)

    init = (tuple(jnp.full((1, blk), 0.01 * NEG_BIG, F32) for _ in range(DSA_HEADS)),
            tuple(jnp.zeros((1, blk), F32) for _ in range(DSA_HEADS)))
    _, ls = lax.fori_loop(0, n_vis, att_body, init)
    for h in range(DSA_HEADS):
        rows = slice(h * DSA_DH, (h + 1) * DSA_DH)
        acc_ref[rows, :] = acc_ref[rows, :] * (1.0 / ls[h])
    o_ref[0] = jnp.transpose(acc_ref[...]).astype(o_ref.dtype)


def _slab_view(w, steps):
    rows = math.prod(w.shape[:-1])
    n = max(k for k in range(1, steps + 1) if rows % (k * BF16_TILE_ROWS) == 0)
    return w.reshape(n, rows // n, w.shape[-1])


def dsa_mixer(z_c, z_aux, toe, to_bf16=()):
    bsz, t, _ = z_c.shape
    blk = DSA_BLOCK
    n_blocks = t // blk
    assert t % blk == 0
    topk = min(DSA_TOPK_MAX, t // 4)
    hd = DSA_HEADS * DSA_DH
    steps = bsz * n_blocks
    slabs = [_slab_view(w, steps) for w in to_bf16]
    slab_specs = [pl.BlockSpec((1,) + s.shape[1:],
                               lambda b, i, n=s.shape[0]: (jnp.minimum(b * n_blocks + i, n - 1), 0, 0))
                  for s in slabs]
    kernel = functools.partial(_dsa_kernel, topk=topk, n_blocks=n_blocks,
                               idx_bits=max(1, (t - 1).bit_length()), n_cast=len(slabs))
    out, *converted = pl.pallas_call(
        kernel,
        grid=(bsz, n_blocks),
        in_specs=[pl.BlockSpec((1, blk, hd), lambda b, i: (b, i, 0)),
                  pl.BlockSpec((1, t, hd), lambda b, i: (b, 0, 1)),
                  pl.BlockSpec((1, t, hd), lambda b, i: (b, 0, 2)),
                  pl.BlockSpec((1, blk, hd), lambda b, i: (b, i, 3)),
                  pl.BlockSpec((1, t, C_AUX), lambda b, i: (b, 0, 0)),
                  pl.BlockSpec((1, blk, C_AUX), lambda b, i: (b, i, 0)),
                  pl.BlockSpec((3, DSA_HEADS, blk, blk), lambda b, i: (0, 0, 0, 0))] + slab_specs,
        out_specs=[pl.BlockSpec((1, blk, hd), lambda b, i: (b, i, 0))] + slab_specs,
        out_shape=[jax.ShapeDtypeStruct((bsz, t, hd), MIXER_DTYPE)]
                  + [jax.ShapeDtypeStruct(s.shape, BF16) for s in slabs],
        scratch_shapes=[pltpu.VMEM((n_blocks, LANES // IDX_DIM, blk, LANES), BF16),
                        pltpu.VMEM((n_blocks, hd, blk), BF16),
                        pltpu.VMEM((n_blocks, blk, blk), jnp.int32),
                        pltpu.VMEM((n_blocks, blk, blk), jnp.int16),
                        pltpu.VMEM((n_blocks, blk, blk), jnp.int16),
                        pltpu.VMEM((hd, blk), F32),
                        pltpu.VMEM((blk, blk), F32),
                        pltpu.VMEM((DSA_HEADS, blk, blk), F32),
                        pltpu.VMEM((DSA_HEADS, blk, blk), BF16)],
        compiler_params=_cparams(("parallel", "arbitrary")),
        name="dsa_mixer",
    )(z_c, z_c, z_c, z_c, z_aux, z_aux, toe, *slabs)
    return (out, *[c.reshape(w.shape) for c, w in zip(converted, to_bf16)])


def _mixer_tail_kernel(h_ref, a_ref, b_ref, c_ref, d_ref, wout_ref, gxa_ref, wq_ref, k_ref, v_ref, wo_ref,
                       *refs, route):
    gw = GROUP_WIDTH
    x = h_ref[0]
    for gi, r in enumerate((a_ref, b_ref, c_ref, d_ref)):
        x = x + _dot(r[0], wout_ref[gi * gw:(gi + 1) * gw, :])

    hd = XA_HEADS * XA_DH
    q = _dot(_rms(x, gxa_ref[...]).astype(BF16), wq_ref[...]).astype(BF16)
    k = k_ref[0]
    v = v_ref[0]
    lane_h = lax.broadcasted_iota(jnp.int32, (1, hd), 1) // XA_DH
    o = jnp.zeros((x.shape[0], hd), F32)
    for h in range(XA_HEADS):
        s = _dot_nt(q, jnp.where(lane_h == h, k, 0.0).astype(BF16)) * (XA_DH ** -0.5)
        p = jnp.exp(s - jnp.max(s, axis=-1, keepdims=True))
        p = p / jnp.sum(p, axis=-1, keepdims=True)
        o = o + _dot(p.astype(BF16), jnp.where(lane_h == h, v, 0.0).astype(BF16))
    x = x + _dot(o.astype(BF16), wo_ref[...])

    if not route:
        o_ref, = refs
        o_ref[0] = x
        return
    gffn_ref, wr_ref, o_ref, hn_ref, gate_ref, cnt_ref = refs
    o_ref[0] = x
    hn = _rms(x, gffn_ref[...])
    hn_ref[0] = hn.astype(BF16)
    tm = hn.shape[0]
    lane = lax.broadcasted_iota(jnp.int32, (tm, GATE_LANES), 1)
    logits = jnp.where(lane < N_EXPERTS, _dot_3pass(hn, wr_ref[...]), -jnp.inf)
    m1 = jnp.max(logits, axis=-1, keepdims=True)
    i1 = jnp.min(jnp.where(logits == m1, lane, GATE_LANES), axis=-1, keepdims=True)
    rest = jnp.where(lane == i1, -jnp.inf, logits)
    m2 = jnp.max(rest, axis=-1, keepdims=True)
    i2 = jnp.min(jnp.where(rest == m2, lane, GATE_LANES), axis=-1, keepdims=True)
    e2 = jnp.exp(m2 - m1)
    g1 = 1.0 / (1.0 + e2)
    gates = jnp.where(lane == i1, g1, 0.0) + jnp.where(lane == i2, e2 * g1, 0.0)
    gates = jnp.where(lane == N_EXPERTS, i1.astype(F32), gates)
    gate_ref[0] = jnp.where(lane == N_EXPERTS + 1, i2.astype(F32), gates)
    sel = jnp.where((lane == i1) | (lane == i2), 1.0, 0.0)
    for c in range(tm // MOE_CHUNK):
        cnt_ref[c] = jnp.sum(sel[c * MOE_CHUNK:(c + 1) * MOE_CHUNK, :], axis=0, keepdims=True)


def mixer_tail(h3, mixers, w_out, g_xa, wq, k, v, wo, routing=None, tm=1024):
    bsz, t, d = h3.shape
    tm = min(tm, t)
    mlen = k.shape[1]
    hd = XA_HEADS * XA_DH
    gw = GROUP_WIDTH
    row = lambda b, i: (b, i, 0)
    fixed2 = lambda b, i: (0, 0)
    in_specs = ([pl.BlockSpec((1, tm, d), row)] + [pl.BlockSpec((1, tm, gw), row)] * 4
                + [pl.BlockSpec((4 * gw, d), fixed2), pl.BlockSpec((1, d), fixed2), pl.BlockSpec((d, hd), fixed2),
                   pl.BlockSpec((1, mlen, hd), lambda b, i: (b, 0, 0)),
                   pl.BlockSpec((1, mlen, hd), lambda b, i: (b, 0, 0)),
                   pl.BlockSpec((hd, d), fixed2)])
    args = [h3, *mixers, w_out, g_xa.reshape(1, d), wq, k, v, wo]
    out_specs = [pl.BlockSpec((1, tm, d), row)]
    out_shape = [jax.ShapeDtypeStruct((bsz, t, d), F32)]
    if routing is not None:
        assert tm % MOE_CHUNK == 0
        g_ffn, wr = routing
        in_specs += [pl.BlockSpec((1, d), fixed2), pl.BlockSpec((d, GATE_LANES), fixed2)]
        args += [g_ffn.reshape(1, d), jnp.zeros((d, GATE_LANES), F32).at[:, :N_EXPERTS].set(wr)]
        per_step = tm // MOE_CHUNK
        out_specs += [pl.BlockSpec((1, tm, d), row), pl.BlockSpec((1, tm, GATE_LANES), row),
                      pl.BlockSpec((per_step, 1, GATE_LANES), lambda b, i: (b * (t // tm) + i, 0, 0))]
        out_shape += [jax.ShapeDtypeStruct((bsz, t, d), BF16), jax.ShapeDtypeStruct((bsz, t, GATE_LANES), F32),
                      jax.ShapeDtypeStruct((bsz * t // MOE_CHUNK, 1, GATE_LANES), F32)]
    return pl.pallas_call(
        functools.partial(_mixer_tail_kernel, route=routing is not None),
        grid=(bsz, t // tm),
        in_specs=in_specs,
        out_specs=out_specs,
        out_shape=out_shape,
        compiler_params=_cparams(("parallel", "parallel")),
        name="mixer_tail",
    )(*args)


def _ffn_kernel(h_ref, g_ref, w1_ref, w3_ref, w2_ref, o_ref, *, tf):
    x = h_ref[...]
    hn = _rms(x, g_ref[...]).astype(BF16)
    acc = x
    for f0 in range(0, w1_ref.shape[1], tf):
        a = _dot(hn, w1_ref[:, f0:f0 + tf])
        b = _dot(hn, w3_ref[:, f0:f0 + tf])
        acc = acc + _dot((a * jax.nn.sigmoid(a) * b).astype(BF16), w2_ref[f0:f0 + tf, :])
    o_ref[...] = acc


def ffn(h, g, w1, w3, w2, tm=1024, tf=704):
    m, d = h.shape
    tm = min(tm, m)
    nf = w1.shape[1]
    assert nf % tf == 0
    resident = lambda shape: pl.BlockSpec(shape, lambda i: (0, 0), pipeline_mode=pl.Buffered(1))
    return pl.pallas_call(
        functools.partial(_ffn_kernel, tf=tf),
        grid=(m // tm,),
        in_specs=[pl.BlockSpec((tm, d), lambda i: (i, 0)),
                  pl.BlockSpec((1, d), lambda i: (0, 0)),
                  resident((d, nf)), resident((d, nf)), resident((nf, d))],
        out_specs=pl.BlockSpec((tm, d), lambda i: (i, 0)),
        out_shape=jax.ShapeDtypeStruct((m, d), F32),
        compiler_params=_cparams(("parallel",)),
        name="ffn",
    )(h, g.reshape(1, d), w1, w3, w2)


GATE_LANES = LANES
MOE_CHUNK = 256
MOE_ROWS = LANES
MOE_TOKENS = 2048
MOE_ALIGN = BF16_TILE_ROWS


def _moe_kernel(cb_ref, h_ref, hn_ref, gate_ref, w1_ref, w3_ref, w2_ref, *refs, final):
    if final:
        gfin_ref, *refs = refs
    y_ref, rank_row, rk_ref, gs_ref, xs_ref, yacc_ref, rc_ref, gc_ref = refs
    t = pl.program_id(0)
    e = pl.program_id(1)
    f = pl.program_id(2)
    ts, d = hn_ref.shape
    ch, rb = MOE_CHUNK, MOE_ROWS
    win = 2 * rb
    n_ch = ts // ch
    lane = lax.broadcasted_iota(jnp.int32, (1, GATE_LANES), 1)

    def before(c):
        return cb_ref[(t * (n_ch + 1) + c) * N_EXPERTS + e]

    n_blocks = (before(n_ch) + rb - 1) // rb
    n_windows = (n_blocks * rb + win - 1) // win + 1

    def windows(c):
        lo, hi = before(c), before(c + 1)
        s0 = (lo // MOE_ALIGN) * MOE_ALIGN
        return s0, jnp.where(hi > lo, (hi - s0 + win - 1) // win, 0)

    spans = [windows(c) for c in range(n_ch)]
    single = functools.reduce(jnp.logical_and, [n_win <= 1 for _, n_win in spans])

    def for_each_window(visit):
        @pl.when(single)
        def _():
            for c, (s0, _) in enumerate(spans):
                visit(c, s0)

        @pl.when(jnp.logical_not(single))
        def _():
            for c, (s0, n_win) in enumerate(spans):
                def body(j, carry, c=c, s0=s0):
                    visit(c, s0 + j * win)
                    return carry
                lax.fori_loop(0, n_win, body, 0)

    @pl.when((e == 0) & (f == 0))
    def _():
        y_ref[...] = h_ref[...]
        strict_lower = (lax.broadcasted_iota(jnp.int32, (ch, ch), 1)
                        < lax.broadcasted_iota(jnp.int32, (ch, ch), 0)).astype(BF16)
        lane_f = lane.astype(F32)
        offs = jnp.ones((1, GATE_LANES), F32)
        for c in range(n_ch):
            rows = slice(c * ch, (c + 1) * ch)
            g = gate_ref[rows, :]
            sel = jnp.where((lane_f == g[:, N_EXPERTS:N_EXPERTS + 1])
                            | (lane_f == g[:, N_EXPERTS + 1:N_EXPERTS + 2]), 1.0, 0.0)
            r = _dot(strict_lower, sel.astype(BF16)) + offs
            r = jnp.where(sel > 0.0, r, 0.0)
            rank_row[:, rows] = jnp.transpose(r) - 1.0
            high = jnp.floor(r * (1.0 / 256.0))
            rk_ref[rows, :] = jnp.concatenate([high, r - 256.0 * high], axis=1).astype(BF16)
            gs_ref[rows, :] = jnp.concatenate(_split_bf16(g, 3), axis=1)
            offs = offs + jnp.sum(sel, axis=0, keepdims=True)

    @pl.when(f == 0)
    def _():
        pick = jnp.where(lax.broadcasted_iota(jnp.int32, (GATE_LANES, GATE_LANES), 0) == e, 1.0, 0.0)
        pick_rank = jnp.concatenate([256.0 * pick, pick], axis=0).astype(BF16)
        pick_gate = jnp.concatenate([pick] * 3, axis=0).astype(BF16)
        for c in range(n_ch):
            rows = slice(c * ch, (c + 1) * ch)
            rc_ref[rows, :] = _dot(rk_ref[rows, :], pick_rank) - 1.0
            gc_ref[rows, :] = _dot(gs_ref[rows, :], pick_gate)

        def clear(j, carry):
            r0 = pl.multiple_of(j * win, win)
            xs_ref[pl.ds(r0, win), :] = jnp.zeros((win, d), BF16)
            yacc_ref[pl.ds(r0, win), :] = jnp.zeros((win, d), F32)
            return carry

        lax.fori_loop(0, n_windows, clear, 0)

        row_id = lax.broadcasted_iota(jnp.int32, (win, ch), 0)

        def gather_window(c, s):
            s = pl.multiple_of(s, MOE_ALIGN)
            ranks = rank_row[pl.ds(e, 1), c * ch:(c + 1) * ch]
            onehot = jnp.where(ranks == (s + row_id).astype(F32), 1.0, 0.0).astype(BF16)
            xs_ref[pl.ds(s, win), :] += _dot(onehot, hn_ref[c * ch:(c + 1) * ch, :]).astype(BF16)

        for_each_window(gather_window)

    def ffn_rows(r0, rows):
        x = xs_ref[pl.ds(r0, rows), :]
        a = _dot(x, w1_ref[0])
        g3 = _dot(x, w3_ref[0])
        yacc_ref[pl.ds(r0, rows), :] += _dot((a * jax.nn.sigmoid(a) * g3).astype(BF16), w2_ref[0])

    five = n_blocks == 5
    half_tail = before(n_ch) <= 4 * rb + rb // 2

    @pl.when(five & half_tail)
    def _():
        ffn_rows(0, 4 * rb + rb // 2)

    @pl.when(five & jnp.logical_not(half_tail))
    def _():
        ffn_rows(0, 5 * rb)

    @pl.when(jnp.logical_not(five))
    def _():
        n_quads = n_blocks // 4

        def ffn_quad(j, carry):
            ffn_rows(pl.multiple_of(j * (4 * rb), 4 * rb), 4 * rb)
            return carry

        lax.fori_loop(0, n_quads, ffn_quad, 0)
        tail = pl.multiple_of(n_quads * (4 * rb), 4 * rb)

        @pl.when((n_blocks & 2) != 0)
        def _():
            ffn_rows(tail, 2 * rb)

        @pl.when((n_blocks & 1) != 0)
        def _():
            ffn_rows(pl.multiple_of(tail + (n_blocks & 2) * rb, rb), rb)

    @pl.when(f == pl.num_programs(2) - 1)
    def _():
        def to_bf16(j, carry):
            r0 = pl.multiple_of(j * win, win)
            xs_ref[pl.ds(r0, win), :] = yacc_ref[pl.ds(r0, win), :].astype(BF16)
            return carry

        lax.fori_loop(0, n_windows, to_bf16, 0)

        lane_id = lax.broadcasted_iota(jnp.int32, (ch, rb), 1).astype(F32)

        def scatter_window(c, s):
            rows = slice(c * ch, (c + 1) * ch)
            s = pl.multiple_of(s, MOE_ALIGN)
            rank = rc_ref[rows, :] - s.astype(F32)
            onehot = jnp.concatenate([jnp.where(rank == lane_id, 1.0, 0.0),
                                      jnp.where(rank == lane_id + float(rb), 1.0, 0.0)],
                                     axis=1).astype(BF16)
            gate = jnp.concatenate([gc_ref[rows, :]] * (d // GATE_LANES), axis=1)
            y_ref[rows, :] += gate * _dot(onehot, xs_ref[pl.ds(s, win), :])

        for_each_window(scatter_window)

        if final:
            @pl.when(e == pl.num_programs(1) - 1)
            def _():
                y_ref[...] = _rms(y_ref[...], gfin_ref[...])


def moe(h, hn, gates, cnt, w1, w3, w2, final_gain=None, tf=896):
    m, d = hn.shape
    ts = min(MOE_TOKENS, m)
    n_exp, _, nf = w1.shape
    assert nf % tf == 0 and m % ts == 0 and ts % MOE_CHUNK == 0 and MOE_ROWS == GATE_LANES
    n_ch = ts // MOE_CHUNK
    counts = cnt.reshape(m // ts, n_ch, GATE_LANES)[:, :, :n_exp].astype(jnp.int32)
    before = jnp.concatenate([jnp.zeros((m // ts, 1, n_exp), jnp.int32), jnp.cumsum(counts, axis=1)], axis=1)
    per_tile = lambda width: pl.BlockSpec((ts, width), lambda t, e, f, cb: (t, 0), pipeline_mode=pl.Buffered(1))
    in_specs = [per_tile(d), per_tile(d), per_tile(GATE_LANES),
                pl.BlockSpec((1, d, tf), lambda t, e, f, cb: (e, 0, f)),
                pl.BlockSpec((1, d, tf), lambda t, e, f, cb: (e, 0, f)),
                pl.BlockSpec((1, tf, d), lambda t, e, f, cb: (e, f, 0))]
    args = [before.reshape(-1), h, hn, gates, w1, w3, w2]
    if final_gain is not None:
        in_specs.append(pl.BlockSpec((1, d), lambda t, e, f, cb: (0, 0)))
        args.append(final_gain.reshape(1, d))
    grid_spec = pltpu.PrefetchScalarGridSpec(
        num_scalar_prefetch=1,
        grid=(m // ts, n_exp, nf // tf),
        in_specs=in_specs,
        out_specs=per_tile(d),
        scratch_shapes=[pltpu.VMEM((GATE_LANES, ts), F32),
                        pltpu.VMEM((ts, 2 * GATE_LANES), BF16),
                        pltpu.VMEM((ts, 3 * GATE_LANES), BF16),
                        pltpu.VMEM((ts + 2 * MOE_ROWS, d), BF16),
                        pltpu.VMEM((ts + 2 * MOE_ROWS, d), F32),
                        pltpu.VMEM((ts, MOE_ROWS), F32),
                        pltpu.VMEM((ts, GATE_LANES), F32)])
    return pl.pallas_call(
        functools.partial(_moe_kernel, final=final_gain is not None),
        grid_spec=grid_spec,
        out_shape=jax.ShapeDtypeStruct((m, d), F32),
        compiler_params=_cparams(("parallel", "arbitrary", "arbitrary")),
        name="moe",
    )(*args)


def _final_norm_kernel(g_ref, x_ref, o_ref):
    o_ref[...] = _rms(x_ref[...], g_ref[...])


def final_rmsnorm(g, x, tm=1024):
    m, d = x.shape
    tm = min(tm, m)
    return pl.pallas_call(
        _final_norm_kernel,
        grid=(m // tm,),
        in_specs=[pl.BlockSpec((1, d), lambda i: (0, 0)), pl.BlockSpec((tm, d), lambda i: (i, 0))],
        out_specs=pl.BlockSpec((tm, d), lambda i: (i, 0)),
        out_shape=jax.ShapeDtypeStruct((m, d), F32),
        compiler_params=_cparams(("parallel",)),
        name="final_norm",
    )(g.reshape(1, d), x)


def _pad_cols(w, width):
    return jnp.pad(w, ((0, 0), (0, width - w.shape[1])))


def _in_proj_weight(w_in):
    a_end = 2 * GLA_HEADS * GLA_DK + 2 * GROUP_WIDTH + GLA_RANK
    b_end = a_end + GROUP_WIDTH
    c_end = b_end + 3 * GROUP_WIDTH + IDX_HEADS * IDX_DIM + IDX_DIM + IDX_HEADS
    assert w_in.shape[1] == c_end + 2 * GROUP_WIDTH
    return jnp.concatenate([_pad_cols(w_in[:, :a_end], A_WIDTH), w_in[:, a_end:b_end],
                            _pad_cols(w_in[:, b_end:c_end], C_MAIN + C_AUX), w_in[:, c_end:]], axis=1).astype(BF16)


def hybrid_layer(h, mem2, toe, p, bsz, t, mixer_weights, routing=None):
    m, d = h.shape
    z_a, z_b, z_c, z_aux, z_d = norm_matmul(h, p["norm_mix"], _in_proj_weight(p["w_in"]),
                                            (A_WIDTH, B_WIDTH, C_MAIN, C_AUX, D_WIDTH),
                                            (F32, F32, BF16, F32, F32))
    o_a = gla_mixer(z_a.reshape(bsz, t, A_WIDTH), p["gla_wa2"], p["gla_ba"], p["gla_norm"])
    o_b = pool_mixer(z_b.reshape(bsz, t, B_WIDTH), p["pool_w"], p["pool_b"].reshape(-1), p["pool_scale"])
    o_c, *weights = dsa_mixer(z_c.reshape(bsz, t, C_MAIN), z_aux.reshape(bsz, t, C_AUX), toe, mixer_weights)
    o_d = sgu_mixer(z_d.reshape(bsz, t, D_WIDTH), p["sgu_ln_g"], p["sgu_ln_b"], p["sgu_w"], p["sgu_b"])
    hd = XA_HEADS * XA_DH
    k, v = norm_matmul(mem2, p["norm_mem"], p["xa_wkv"].astype(BF16), (hd, hd))
    mlen = mem2.shape[0] // bsz
    outs = mixer_tail(h.reshape(bsz, t, d), (o_a, o_b, o_c, o_d), p["w_out"].astype(BF16), p["norm_xa"],
                      p["xa_wq"].astype(BF16), k.reshape(bsz, mlen, hd), v.reshape(bsz, mlen, hd),
                      p["xa_wo"].astype(BF16), routing)
    if routing is None:
        return weights, outs[0].reshape(m, d)
    h3, hn, gates, counts = outs
    return weights, h3.reshape(m, d), hn.reshape(m, d), gates.reshape(m, GATE_LANES), counts


def kernel(x, mem, rel_bias, final_norm, norm_mix, w_in, gla_wa2, gla_ba, gla_norm, pool_w, pool_b,
           pool_scale, sgu_ln_g, sgu_ln_b, sgu_w, sgu_b, w_out, norm_xa, norm_mem, xa_wq, xa_wkv, xa_wo,
           norm_ffn, ffn_w1, ffn_w3, ffn_w2, router, moe_w1, moe_w3, moe_w2):
    bsz, t, d = x.shape
    depth = norm_mix.shape[0]
    h = x.reshape(bsz * t, d)
    mem2 = mem.reshape(-1, d)
    toe = rel_bias_tables(rel_bias)
    normed = False
    for i in range(depth):
        p = dict(norm_mix=norm_mix[i], w_in=w_in[i], gla_wa2=gla_wa2[i], gla_ba=gla_ba[i],
                 gla_norm=gla_norm[i], pool_w=pool_w[i], pool_b=pool_b[i], pool_scale=pool_scale[i],
                 sgu_ln_g=sgu_ln_g[i], sgu_ln_b=sgu_ln_b[i], sgu_w=sgu_w[i], sgu_b=sgu_b[i],
                 w_out=w_out[i], norm_xa=norm_xa[i], norm_mem=norm_mem[i], xa_wq=xa_wq[i],
                 xa_wkv=xa_wkv[i], xa_wo=xa_wo[i])
        j = i // 2
        if i % 2 == 0:
            (w1, w3, w2), h = hybrid_layer(h, mem2, toe, p, bsz, t, (ffn_w1[j], ffn_w3[j], ffn_w2[j]))
            h = ffn(h, norm_ffn[i], w1, w3, w2)
        else:
            (w1, w3, w2), h, hn, gates, counts = hybrid_layer(h, mem2, toe, p, bsz, t,
                                                              (moe_w1[j], moe_w3[j], moe_w2[j]),
                                                              routing=(norm_ffn[i], router[j]))
            normed = i == depth - 1
            h = moe(h, hn, gates, counts, w1, w3, w2, final_gain=final_norm if normed else None)
    if not normed:
        h = final_rmsnorm(final_norm, h)
    return h.reshape(bsz, t, d)
```
